```python
import jax
import jax.numpy as jnp
from jax import lax
import numpy as np

D_MODEL = 1024
BATCH = 16
SEQ = 256
DEPTH = 2
DEC_BATCH = 4
DEC_SEQ = 1024
PAST_LEN = 512

GRID_W = 64
EPS = 1e-6
ROPE_THETA = 10000.0
Q_BLOCK = 128
N_BRANCH = 4
BRANCH_W = 512

HG_HEADS = 4
HG_DK = 128
HG_DV = 128
HG_CHUNK = 16

GQ_HEADS = 8
GQ_KV_HEADS = 2
GQ_HD = 64

NA_HEADS = 8
NA_HD = 64
NA_WIN_R = 8
NA_WIN_C = 16

ML_HEADS = 8
ML_NOPE = 64
ML_ROPE = 32
ML_V = 64
ML_Q_RANK = 256
ML_KV_RANK = 128

FFN_HIDDEN = -(-(8 * D_MODEL) // (3 * 256)) * 256

GQ_SCALE = GQ_HD ** -0.5
NA_SCALE = NA_HD ** -0.5
ML_SCALE = (ML_NOPE + ML_ROPE) ** -0.5

IN_SPLITS = (
    HG_HEADS * HG_DK,
    HG_HEADS * HG_DK,
    HG_HEADS * HG_DK,
    HG_HEADS * HG_DV,
    HG_HEADS * HG_DV,
    GQ_HEADS * GQ_HD,
    GQ_KV_HEADS * GQ_HD,
    GQ_KV_HEADS * GQ_HD,
    NA_HEADS * NA_HD,
    NA_HEADS * NA_HD,
    NA_HEADS * NA_HD,
    ML_Q_RANK,
    ML_KV_RANK,
    ML_ROPE,
    N_BRANCH * D_MODEL,
)
IN_WIDTH = sum(IN_SPLITS)

kernel_name = 'hybrid_diffusion_prefix_trunk_step'


def rms_norm(x, gain=None):
    xf = x.astype(jnp.float32)
    y = xf * lax.rsqrt(jnp.mean(xf * xf, axis=-1, keepdims=True) + EPS)
    if gain is not None:
        y = y * gain.astype(jnp.float32)
    return y.astype(x.dtype)


def adaln(cond, w_ada, b_ada):
    m = jax.nn.silu(cond) @ w_ada + b_ada
    return [t[:, None, :] for t in jnp.split(m, 6, axis=-1)]


def axial_rope(n_tokens, rot_dim):
    n_freq = rot_dim // 4
    inv_freq = ROPE_THETA ** (-jnp.arange(n_freq, dtype=jnp.float32) / n_freq)
    t = jnp.arange(n_tokens)
    row = (t // GRID_W).astype(jnp.float32)[:, None]
    col = (t % GRID_W).astype(jnp.float32)[:, None]
    ang = jnp.concatenate([row * inv_freq, col * inv_freq], axis=-1)
    return jnp.cos(ang), jnp.sin(ang)


def apply_rope(x, cos, sin):
    xf = x.astype(jnp.float32).reshape(x.shape[:-1] + (x.shape[-1] // 2, 2))
    x1, x2 = xf[..., 0], xf[..., 1]
    c, s = cos[:, None, :], sin[:, None, :]
    return jnp.stack([x1 * c - x2 * s, x1 * s + x2 * c], axis=-1).reshape(x.shape).astype(x.dtype)


def blocked_attention(q, k, v, scale):
    B, Tq, H, dk = q.shape
    kvh = k.shape[2]
    g = H // kvh
    nb = Tq // Q_BLOCK
    qb = jnp.moveaxis(q.reshape(B, nb, Q_BLOCK, kvh, g, dk), 1, 0)

    def block(qblk):
        s = jnp.einsum('bqngd,bknd->bngqk', qblk, k).astype(jnp.float32) * scale
        p = jax.nn.softmax(s, axis=-1).astype(v.dtype)
        return jnp.einsum('bngqk,bknd->bqngd', p, v)

    o = lax.map(block, qb)
    return jnp.moveaxis(o, 0, 1).reshape(B, Tq, H, v.shape[-1])


def chunk_gla(q, k, v, log_f, s0):
    B, T, H, DK = q.shape
    DV = v.shape[-1]
    C = HG_CHUNK
    n = T // C

    def chunks(a):
        return a.astype(jnp.float32).reshape(B, n, C, H, a.shape[-1]).transpose(1, 0, 3, 2, 4)

    qc, kc, vc, gc = chunks(q), chunks(k), chunks(v), chunks(log_f)
    b = jnp.cumsum(gc, axis=-2)
    causal = jnp.tril(jnp.ones((C, C), dtype=bool))
    diff = b[..., :, None, :] - b[..., None, :, :]
    decay = jnp.where(causal[:, :, None], jnp.exp(jnp.minimum(diff, 0.0)), 0.0)
    a_intra = jnp.einsum('nbhtd,nbhtsd,nbhsd->nbhts', qc, decay, kc)
    o_intra = jnp.einsum('nbhts,nbhsv->nbhtv', a_intra, vc)
    b_last = b[..., -1:, :]
    k_to_end = kc * jnp.exp(b_last - b)
    q_from_start = qc * jnp.exp(b)
    chunk_decay = jnp.exp(b_last[..., 0, :])

    def step(S, inp):
        qs, kend, vv, dec = inp
        o_inter = jnp.einsum('bhtk,bhkv->bhtv', qs, S)
        S_new = dec[..., None] * S + jnp.einsum('bhtk,bhtv->bhkv', kend, vv)
        return S_new, o_inter

    s_final, o_inter = lax.scan(step, s0.astype(jnp.float32), (q_from_start, k_to_end, vc, chunk_decay))
    o = (o_intra + o_inter).transpose(1, 0, 3, 2, 4).reshape(B, T, H, DV)
    return o, s_final


def hgrn_mixer(hq, hff, hfb, hi, hg, lb_f, lb_b, gain, s0_f, s0_b):
    B, T, _ = hq.shape

    def heads(a):
        return a.reshape(B, T, HG_HEADS, -1)

    q = heads(hq.astype(jnp.float32)) * (HG_DK ** -0.5)
    v = heads(hi.astype(jnp.float32))

    def forget(pre, lb):
        f = lb + (1.0 - lb) * jax.nn.sigmoid(pre.astype(jnp.float32))
        return heads(1.0 - f), heads(jnp.log(f))

    k_f, lf_f = forget(hff, lb_f)
    k_b, lf_b = forget(hfb, lb_b)
    o_f, s_f = chunk_gla(q, k_f, v, lf_f, s0_f)

    def rev(a):
        return jnp.flip(a, axis=1)

    o_b, s_b = chunk_gla(rev(q), rev(k_b), rev(v), rev(lf_b), s0_b)
    o = rms_norm(o_f + rev(o_b), gain) * jax.nn.silu(heads(hg.astype(jnp.float32)))
    return o.reshape(B, T, HG_HEADS * HG_DV), s_f, s_b


def neighbourhood_attention(q, k, v, k_ctx, v_ctx, rpb):
    B, T, H, d = q.shape
    rows = T // GRID_W
    wr = min(NA_WIN_R, rows)
    r = jnp.arange(rows)
    key_rows = jnp.clip(r - wr // 2, 0, rows - wr)[:, None] + jnp.arange(wr)[None, :]
    col = jnp.arange(GRID_W)
    col_start = jnp.clip(col - NA_WIN_C // 2, 0, GRID_W - NA_WIN_C)
    col_ok = (col[None, :] >= col_start[:, None]) & (col[None, :] < col_start[:, None] + NA_WIN_C)
    qg = q.reshape(B, rows, GRID_W, H, d)
    kg = k.reshape(B, rows, GRID_W, H, d)[:, key_rows]
    vg = v.reshape(B, rows, GRID_W, H, d)[:, key_rows]
    s_loc = jnp.einsum('brqhd,brjkhd->bhrqjk', qg, kg).astype(jnp.float32) * NA_SCALE
    dr_idx = key_rows - r[:, None] + NA_WIN_R - 1
    dc_idx = jnp.clip(col[None, :] - col[:, None] + NA_WIN_C - 1, 0, 2 * NA_WIN_C - 2)
    bias = rpb.astype(jnp.float32)[:, dr_idx[:, None, :, None], dc_idx[None, :, None, :]]
    s_loc = jnp.where(col_ok[:, None, :], s_loc + bias, -jnp.inf)
    n_loc = wr * GRID_W
    s_ctx = jnp.einsum('brqhd,bphd->bhrqp', qg, k_ctx).astype(jnp.float32) * NA_SCALE
    s = jnp.concatenate([s_loc.reshape(B, H, rows, GRID_W, n_loc), s_ctx], axis=-1)
    prob = jax.nn.softmax(s, axis=-1).astype(v.dtype)
    p_loc = prob[..., :n_loc].reshape(B, H, rows, GRID_W, wr, GRID_W)
    o = (jnp.einsum('bhrqjk,brjkhd->brqhd', p_loc, vg)
         + jnp.einsum('bhrqp,bphd->brqhd', prob[..., n_loc:], v_ctx))
    return o.reshape(B, T, H, d)


def mla_keys(ckv, k_rope, w_kv_b):
    B, T, _ = ckv.shape
    kv = (ckv @ w_kv_b).reshape(B, T, ML_HEADS, ML_NOPE + ML_V)
    k = jnp.concatenate([kv[..., :ML_NOPE],
                         jnp.broadcast_to(k_rope[:, :, None, :], (B, T, ML_HEADS, ML_ROPE))], axis=-1)
    return k, kv[..., ML_NOPE:]


def merge_branches(branches, gates, w_branch, w_out):
    B, T, _ = gates.shape
    o = jnp.stack(branches, axis=2)
    bo = jnp.einsum('btnc,ncd->btnd', o, w_branch)
    g = jax.nn.sigmoid(gates.astype(jnp.float32)).reshape(B, T, N_BRANCH, D_MODEL)
    merged = jnp.sum(g * bo.astype(jnp.float32), axis=2).astype(gates.dtype)
    return merged @ w_out


def trunk_layer(x, mod, p, ctx=None):
    B, T, _ = x.shape
    sh1, sc1, g1, sh2, sc2, g2 = mod
    h = rms_norm(x) * (1 + sc1) + sh1
    split_at = np.cumsum(IN_SPLITS)[:-1].tolist()
    (hq, hff, hfb, hi, hg, gq, gk, gv, nq, nk, nv, mqa, mkva, mkr, gates) = jnp.split(h @ p['w_in'], split_at, axis=-1)
    latent = ctx is not None

    if latent:
        s0_f, s0_b = ctx['hg'][:, 0], ctx['hg'][:, 1]
    else:
        s0_f = jnp.zeros((B, HG_HEADS, HG_DK, HG_DV), jnp.float32)
        s0_b = s0_f
    out_a, s_f, s_b = hgrn_mixer(hq, hff, hfb, hi, hg, p['lb_f'], p['lb_b'], p['hg_gain'], s0_f, s0_b)

    q_b = rms_norm(gq.reshape(B, T, GQ_HEADS, GQ_HD), p['gq_q_gain'])
    k_b = rms_norm(gk.reshape(B, T, GQ_KV_HEADS, GQ_HD), p['gq_k_gain'])
    v_b = gv.reshape(B, T, GQ_KV_HEADS, GQ_HD)
    q_c = nq.reshape(B, T, NA_HEADS, NA_HD)
    k_c = nk.reshape(B, T, NA_HEADS, NA_HD)
    v_c = nv.reshape(B, T, NA_HEADS, NA_HD)
    q_d = (rms_norm(mqa, p['ml_q_a_gain']) @ p['ml_w_q_b']).reshape(B, T, ML_HEADS, ML_NOPE + ML_ROPE)
    ckv = rms_norm(mkva, p['ml_kv_a_gain'])

    if latent:
        cos_b, sin_b = axial_rope(T, GQ_HD)
        cos_d, sin_d = axial_rope(T, ML_ROPE)
        q_b = apply_rope(q_b, cos_b, sin_b)
        k_b = apply_rope(k_b, cos_b, sin_b)
        out_b = blocked_attention(q_b, jnp.concatenate([ctx['gq_k'], k_b], axis=1),
                                  jnp.concatenate([ctx['gq_v'], v_b], axis=1), GQ_SCALE)
        out_c = neighbourhood_attention(q_c, k_c, v_c, ctx['na_k'], ctx['na_v'], p['na_rpb'])
        q_d = jnp.concatenate([q_d[..., :ML_NOPE], apply_rope(q_d[..., ML_NOPE:], cos_d, sin_d)], axis=-1)
        kr = apply_rope(mkr[:, :, None, :], cos_d, sin_d)[:, :, 0]
        k_lat, v_lat = mla_keys(ckv, kr, p['ml_w_kv_b'])
        k_ctx, v_ctx = mla_keys(ctx['ml_ckv'], ctx['ml_kr'], p['ml_w_kv_b'])
        out_d = blocked_attention(q_d, jnp.concatenate([k_ctx, k_lat], axis=1),
                                  jnp.concatenate([v_ctx, v_lat], axis=1), ML_SCALE)
        new = None
    else:
        out_b = blocked_attention(q_b, k_b, v_b, GQ_SCALE)
        out_c = blocked_attention(q_c, k_c, v_c, NA_SCALE)
        k_d, v_d = mla_keys(ckv, mkr, p['ml_w_kv_b'])
        out_d = blocked_attention(q_d, k_d, v_d, ML_SCALE)
        new = (jnp.stack([s_f, s_b], axis=1), k_b, v_b, k_c, v_c, ckv, mkr)

    branches = [o.reshape(B, T, BRANCH_W).astype(x.dtype) for o in (out_a, out_b, out_c, out_d)]
    x = x + g1 * merge_branches(branches, gates, p['w_branch'], p['w_out'])
    h2 = rms_norm(x) * (1 + sc2) + sh2
    a, gate_ffn = jnp.split(h2 @ p['w_ffn_in'], 2, axis=-1)
    x = x + g2 * ((jax.nn.silu(gate_ffn) * a) @ p['w_ffn_out'])
    return x, new


def setup_inputs(seed: int = 0) -> dict:
    key = jax.random.key(seed)
    ks = jax.random.split(key, 32)

    def nrm(k, shape, scale):
        return jax.random.normal(k, shape, jnp.float32) * scale

    D = D_MODEL
    P = PAST_LEN
    return {
        'x_prompt': nrm(ks[0], (BATCH, SEQ, D), 1.0),
        'x_sample': nrm(ks[1], (DEC_BATCH, DEC_SEQ, D), 1.0),
        'state_hgrn': nrm(ks[2], (DEC_BATCH, DEPTH, 2, HG_HEADS, HG_DK, HG_DV), 0.5),
        'cache_gqa_k': nrm(ks[3], (DEC_BATCH, DEPTH, P, GQ_KV_HEADS, GQ_HD), 1.0),
        'cache_gqa_v': nrm(ks[4], (DEC_BATCH, DEPTH, P, GQ_KV_HEADS, GQ_HD), 1.0),
        'cache_na_k': nrm(ks[5], (DEC_BATCH, DEPTH, P, NA_HEADS, NA_HD), 1.0),
        'cache_na_v': nrm(ks[6], (DEC_BATCH, DEPTH, P, NA_HEADS, NA_HD), 1.0),
        'cache_mla_ckv': nrm(ks[7], (DEC_BATCH, DEPTH, P, ML_KV_RANK), 1.0),
        'cache_mla_krope': nrm(ks[8], (DEC_BATCH, DEPTH, P, ML_ROPE), 1.0),
        'c': nrm(ks[9], (DEC_BATCH, D), 1.0),
        'c_ctx': nrm(ks[10], (D,), 1.0),
        'w_ada': nrm(ks[11], (DEPTH, D, 6 * D), 0.5 * D ** -0.5),
        'b_ada': nrm(ks[12], (DEPTH, 6 * D), 0.01),
        'w_in': nrm(ks[13], (DEPTH, D, IN_WIDTH), D ** -0.5),
        'hg_lb_logits': nrm(ks[14], (DEPTH, 2, HG_HEADS * HG_DK), 1.0),
        'hg_gain': 1.0 + nrm(ks[15], (DEPTH, HG_DV), 0.05),
        'gq_q_gain': 1.0 + nrm(ks[16], (DEPTH, GQ_HD), 0.05),
        'gq_k_gain': 1.0 + nrm(ks[17], (DEPTH, GQ_HD), 0.05),
        'na_rpb': nrm(ks[18], (DEPTH, NA_HEADS, 2 * NA_WIN_R - 1, 2 * NA_WIN_C - 1), 0.5),
        'ml_q_a_gain': 1.0 + nrm(ks[19], (DEPTH, ML_Q_RANK), 0.05),
        'ml_kv_a_gain': 1.0 + nrm(ks[20], (DEPTH, ML_KV_RANK), 0.05),
        'ml_w_q_b': nrm(ks[21], (DEPTH, ML_Q_RANK, ML_HEADS * (ML_NOPE + ML_ROPE)), ML_Q_RANK ** -0.5),
        'ml_w_kv_b': nrm(ks[22], (DEPTH, ML_KV_RANK, ML_HEADS * (ML_NOPE + ML_V)), ML_KV_RANK ** -0.5),
        'w_branch': nrm(ks[23], (DEPTH, N_BRANCH, BRANCH_W, D), BRANCH_W ** -0.5),
        'w_out': nrm(ks[24], (DEPTH, D, D), D ** -0.5),
        'w_ffn_in': nrm(ks[25], (DEPTH, D, 2 * FFN_HIDDEN), D ** -0.5),
        'w_ffn_out': nrm(ks[26], (DEPTH, FFN_HIDDEN, D), FFN_HIDDEN ** -0.5),
        'final_gain': 1.0 + nrm(ks[27], (D,), 0.05),
    }


def reference(x_prompt, x_sample, state_hgrn, cache_gqa_k, cache_gqa_v, cache_na_k, cache_na_v,
              cache_mla_ckv, cache_mla_krope, c, c_ctx, w_ada, b_ada, w_in, hg_lb_logits, hg_gain,
              gq_q_gain, gq_k_gain, na_rpb, ml_q_a_gain, ml_kv_a_gain, ml_w_q_b, ml_w_kv_b,
              w_branch, w_out, w_ffn_in, w_ffn_out, final_gain):
    lb = jnp.cumsum(jax.nn.softmax(hg_lb_logits.astype(jnp.float32), axis=0), axis=0)
    lb = lb - lb[:1]
    y_p, y_s = x_prompt, x_sample
    per_layer = []
    for l in range(DEPTH):
        p = {
            'w_in': w_in[l], 'lb_f': lb[l, 0], 'lb_b': lb[l, 1], 'hg_gain': hg_gain[l],
            'gq_q_gain': gq_q_gain[l], 'gq_k_gain': gq_k_gain[l], 'na_rpb': na_rpb[l],
            'ml_q_a_gain': ml_q_a_gain[l], 'ml_kv_a_gain': ml_kv_a_gain[l],
            'ml_w_q_b': ml_w_q_b[l], 'ml_w_kv_b': ml_w_kv_b[l],
            'w_branch': w_branch[l], 'w_out': w_out[l],
            'w_ffn_in': w_ffn_in[l], 'w_ffn_out': w_ffn_out[l],
        }
        y_p, new = trunk_layer(y_p, adaln(c_ctx[None, :], w_ada[l], b_ada[l]), p)
        per_layer.append(new)
        ctx = {
            'hg': state_hgrn[:, l], 'gq_k': cache_gqa_k[:, l], 'gq_v': cache_gqa_v[:, l],
            'na_k': cache_na_k[:, l], 'na_v': cache_na_v[:, l],
            'ml_ckv': cache_mla_ckv[:, l], 'ml_kr': cache_mla_krope[:, l],
        }
        y_s, _ = trunk_layer(y_s, adaln(c, w_ada[l], b_ada[l]), p, ctx)
    y_prompt = rms_norm(y_p, final_gain)
    y_sample = rms_norm(y_s, final_gain)
    new_state_hgrn = jnp.stack([n[0] for n in per_layer], axis=1)
    new_cache_gqa_k = jnp.stack([n[1] for n in per_layer], axis=1)
    new_cache_gqa_v = jnp.stack([n[2] for n in per_layer], axis=1)
    new_cache_na_k = jnp.stack([n[3] for n in per_layer], axis=1)
    new_cache_na_v = jnp.stack([n[4] for n in per_layer], axis=1)
    new_cache_mla_ckv = jnp.stack([n[5] for n in per_layer], axis=1)
    new_cache_mla_krope = jnp.stack([n[6] for n in per_layer], axis=1)
    return (y_prompt, y_sample, new_state_hgrn, new_cache_gqa_k, new_cache_gqa_v,
            new_cache_na_k, new_cache_na_v, new_cache_mla_ckv, new_cache_mla_krope)
```

```python
import functools

import numpy as np
import jax
import jax.numpy as jnp
from jax import lax
from jax.experimental import pallas as pl
from jax.experimental.pallas import tpu as pltpu

F32 = jnp.float32
BF16 = jnp.bfloat16

D_MODEL = 1024
BATCH = 16
SEQ = 256
DEPTH = 2
DEC_BATCH = 4
DEC_SEQ = 1024
PAST_LEN = 512
GRID_W = 64
EPS = 1e-6
ROPE_THETA = 10000.0
N_BRANCH = 4
BRANCH_W = 512
HG_HEADS = 4
HG_DK = 128
HG_DV = 128
GQ_HEADS = 8
GQ_KV_HEADS = 2
GQ_HD = 64
NA_HEADS = 8
NA_HD = 64
NA_WIN_R = 8
NA_WIN_C = 16
ML_HEADS = 8
ML_NOPE = 64
ML_ROPE = 32
ML_V = 64
ML_Q_RANK = 256
ML_KV_RANK = 128
FFN_HIDDEN = 2816
GQ_SCALE = GQ_HD ** -0.5
NA_SCALE = NA_HD ** -0.5
ML_SCALE = (ML_NOPE + ML_ROPE) ** -0.5

M_ROWS = BATCH * SEQ
assert M_ROWS == DEC_BATCH * DEC_SEQ

LANE = 128
HALF = 64
MIX_W = 5376
GATE_W = N_BRANCH * D_MODEL
HG_CHUNK = 16
NEG = -1e30
VMEM_LIMIT = 56 * 1024 * 1024

COL_HQ, COL_HFF, COL_HFB, COL_HI, COL_HG, COL_GQ, COL_NQ, COL_NK, COL_NV = range(9)
COL_MQA = 18
COL_GK, COL_GV, COL_MKVA, COL_MKR = 38, 39, 40, 41


def _dot(a, b):
    return jnp.dot(a, b, preferred_element_type=F32)


def _dot_nt(a, b):
    return lax.dot_general(a, b, (((1,), (1,)), ((), ())), preferred_element_type=F32)


def _dot_tn(a, b):
    return lax.dot_general(a, b, (((0,), (0,)), ((), ())), preferred_element_type=F32)


def _split3(x):
    x1 = x.astype(BF16)
    r1 = x - x1.astype(F32)
    x2 = r1.astype(BF16)
    x3 = (r1 - x2.astype(F32)).astype(BF16)
    return x1, x2, x3


def _dot_exact_lhs(a_bf16, x):
    x1, x2, x3 = _split3(x)
    return (_dot(a_bf16, x3) + _dot(a_bf16, x2)) + _dot(a_bf16, x1)


def _dot_exact_rhs(x, b_bf16):
    x1, x2, x3 = _split3(x)
    return (_dot(x3, b_bf16) + _dot(x2, b_bf16)) + _dot(x1, b_bf16)


def _dot_f32(a, b):
    a1, a2, a3 = _split3(a)
    b1, b2, b3 = _split3(b)
    small = (_dot(a1, b3) + _dot(a3, b1)) + _dot(a2, b2)
    mid = _dot(a1, b2) + _dot(a2, b1)
    return (small + mid) + _dot(a1, b1)


def _rms(x):
    return x * lax.rsqrt(jnp.mean(x * x, axis=-1, keepdims=True) + EPS)


def _silu(x):
    return x * jax.nn.sigmoid(x)


def _params(*sem):
    return pltpu.CompilerParams(dimension_semantics=sem, vmem_limit_bytes=VMEM_LIMIT)


def _mod_row(latent, tm):
    if latent:
        return lambda i: 1 + (i * tm) // DEC_SEQ
    return lambda i: 0


def _ada_kernel(c_ref, w_ref, b_ref, o_ref):
    c = c_ref[...]
    o_ref[...] = _dot_f32(_silu(c), w_ref[...]) + b_ref[...]


def _ada(cond8, w_ada, b_ada):
    tn = 1536
    out = pl.pallas_call(
        _ada_kernel,
        grid=(DEPTH, 6 * D_MODEL // tn),
        in_specs=[
            pl.BlockSpec((8, D_MODEL), lambda l, j: (0, 0)),
            pl.BlockSpec((None, D_MODEL, tn), lambda l, j: (l, 0, j)),
            pl.BlockSpec((None, 1, tn), lambda l, j: (l, 0, j)),
        ],
        out_specs=pl.BlockSpec((None, 8, tn), lambda l, j: (l, 0, j)),
        out_shape=jax.ShapeDtypeStruct((DEPTH, 8, 6 * D_MODEL), F32),
        compiler_params=_params("parallel", "parallel"),
        name="ada",
    )(cond8, w_ada, b_ada.reshape(DEPTH, 1, 6 * D_MODEL))
    return out.reshape(DEPTH, 8, 6, 1, D_MODEL)


def _inproj_kernel(x_ref, sc_ref, sh_ref, w_ref, o_ref, h_ref):
    @pl.when(pl.program_id(1) == 0)
    def _():
        h = _rms(x_ref[...]) * (1.0 + sc_ref[...]) + sh_ref[...]
        h_ref[...] = h.astype(BF16)

    o_ref[...] = _dot(h_ref[...], w_ref[...])


def _inproj(x, mod, w, latent, tn, name):
    tm = 1024
    n = w.shape[1]
    row = _mod_row(latent, tm)
    return pl.pallas_call(
        _inproj_kernel,
        grid=(M_ROWS // tm, n // tn),
        in_specs=[
            pl.BlockSpec((tm, D_MODEL), lambda i, j: (i, 0)),
            pl.BlockSpec((None, None, 1, D_MODEL), lambda i, j: (row(i), 1, 0, 0)),
            pl.BlockSpec((None, None, 1, D_MODEL), lambda i, j: (row(i), 0, 0, 0)),
            pl.BlockSpec((D_MODEL, tn), lambda i, j: (0, j)),
        ],
        out_specs=pl.BlockSpec((tm, tn), lambda i, j: (i, j)),
        out_shape=jax.ShapeDtypeStruct((M_ROWS, n), F32),
        scratch_shapes=[pltpu.VMEM((tm, D_MODEL), BF16)],
        compiler_params=_params("parallel", "arbitrary"),
        name=name,
    )(x, mod, mod, w)


def _rope(x, tab_ref, reps):
    w = x.shape[1]
    c = jnp.concatenate([tab_ref[0]] * reps, axis=1) if reps > 1 else tab_ref[0]
    se = jnp.concatenate([tab_ref[1]] * reps, axis=1) if reps > 1 else tab_ref[1]
    so = jnp.concatenate([tab_ref[2]] * reps, axis=1) if reps > 1 else tab_ref[2]
    return x * c + pltpu.roll(x, w - 1, 1) * se + pltpu.roll(x, 1, 1) * so


def _head_rms(x, gain, avg_bf16):
    ms = _dot_exact_rhs(x * x, avg_bf16)
    return x * lax.rsqrt(ms + EPS) * gain


def _prep_kernel(*refs, rope):
    (gq_ref, gk_ref, mqa_ref, mkva_ref, mkr_ref, gqg_ref, gkg_ref, mqg_ref, mkvg_ref,
     wqb_ref, wk_ref, wv_ref, avg_ref) = refs[:13]
    if rope:
        gtab_ref, mtab_ref = refs[13:15]
        outs = refs[15:]
    else:
        outs = refs[13:]
    qb_ref, kb_ref, qd_ref, ckv_ref, kd_ref, vd_ref = outs

    q = _head_rms(gq_ref[...], gqg_ref[...], avg_ref[...])
    k = _head_rms(gk_ref[...], gkg_ref[...], avg_ref[0:LANE, 0:LANE])
    qd = _dot((_rms(mqa_ref[...]) * mqg_ref[...]).astype(BF16), wqb_ref[...])
    ckv = _rms(mkva_ref[...]) * mkvg_ref[...]
    kr = mkr_ref[...]
    if rope:
        q = _rope(q, gtab_ref, GQ_HEADS * GQ_HD // LANE)
        k = _rope(k, gtab_ref, 1)
        qd = _rope(qd, mtab_ref, ML_HEADS)
        kr = _rope(kr, mtab_ref, 1)
    qb_ref[...] = q
    kb_ref[...] = k
    qd_ref[...] = qd
    ckv_ref[...] = ckv
    cb = ckv.astype(BF16)
    kd_ref[...] = _dot(cb, wk_ref[...]) + jnp.concatenate([kr] * ML_HEADS, axis=1)
    vd_ref[...] = _dot(cb, wv_ref[...])


def _prep(ymix, lw, tabs, latent):
    tm = 512
    const = lambda i: (0, 0)
    in_specs = [
        pl.BlockSpec((tm, 512), lambda i: (i, COL_GQ)),
        pl.BlockSpec((tm, LANE), lambda i: (i, COL_GK)),
        pl.BlockSpec((tm, 256), lambda i: (i, COL_MQA)),
        pl.BlockSpec((tm, LANE), lambda i: (i, COL_MKVA)),
        pl.BlockSpec((tm, LANE), lambda i: (i, COL_MKR)),
        pl.BlockSpec((1, 512), const),
        pl.BlockSpec((1, LANE), const),
        pl.BlockSpec((1, 256), const),
        pl.BlockSpec((1, LANE), const),
        pl.BlockSpec((ML_Q_RANK, ML_HEADS * LANE), const),
        pl.BlockSpec((ML_KV_RANK, ML_HEADS * LANE), const),
        pl.BlockSpec((ML_KV_RANK, ML_HEADS * ML_V), const),
        pl.BlockSpec((512, 512), const),
    ]
    args = [ymix, ymix, ymix, ymix, ymix, lw["gq_q_gain"], lw["gq_k_gain"], lw["ml_q_gain"],
            lw["ml_kv_gain"], lw["w_qb"], lw["w_k"], lw["w_v"], lw["avg"]]
    if latent:
        per = DEC_SEQ // tm
        in_specs += [pl.BlockSpec((3, tm, LANE), lambda i: (0, i % per, 0))] * 2
        args += [tabs["gq"], tabs["ml"]]
    widths = (512, LANE, ML_HEADS * LANE, LANE, ML_HEADS * LANE, ML_HEADS * ML_V)
    return pl.pallas_call(
        functools.partial(_prep_kernel, rope=latent),
        grid=(M_ROWS // tm,),
        in_specs=in_specs,
        out_specs=[pl.BlockSpec((tm, w), lambda i: (i, 0)) for w in widths],
        out_shape=[jax.ShapeDtypeStruct((M_ROWS, w), F32) for w in widths],
        compiler_params=_params("parallel"),
        name="prep_lat" if latent else "prep_ctx",
    )(*args)


def _mla_cache_kernel(ckv_ref, kr_ref, wk_ref, wv_ref, kd_ref, vd_ref):
    cb = ckv_ref[...].astype(BF16)
    kd_ref[...] = _dot(cb, wk_ref[...]) + jnp.concatenate([kr_ref[...]] * ML_HEADS, axis=1)
    vd_ref[...] = _dot(cb, wv_ref[...])


def _mla_cache(ckv, kr_blk, lw):
    rows = ckv.shape[0]
    tm = 512
    const = lambda i: (0, 0)
    return pl.pallas_call(
        _mla_cache_kernel,
        grid=(rows // tm,),
        in_specs=[
            pl.BlockSpec((tm, LANE), lambda i: (i, 0)),
            pl.BlockSpec((tm, LANE), lambda i: (i, 0)),
            pl.BlockSpec((ML_KV_RANK, ML_HEADS * LANE), const),
            pl.BlockSpec((ML_KV_RANK, ML_HEADS * ML_V), const),
        ],
        out_specs=[pl.BlockSpec((tm, ML_HEADS * LANE), lambda i: (i, 0)),
                   pl.BlockSpec((tm, ML_HEADS * ML_V), lambda i: (i, 0))],
        out_shape=[jax.ShapeDtypeStruct((rows, ML_HEADS * LANE), F32),
                   jax.ShapeDtypeStruct((rows, ML_HEADS * ML_V), F32)],
        compiler_params=_params("parallel"),
        name="mla_cache",
    )(ckv, kr_blk, lw["w_k"], lw["w_v"])


def _softmax_pv(scores, values):
    m = scores[0].max(axis=-1, keepdims=True)
    for s in scores[1:]:
        m = jnp.maximum(m, s.max(axis=-1, keepdims=True))
    den = None
    out = None
    for s, v in zip(scores, values):
        p = jnp.exp(s - m)
        d = p.sum(axis=-1, keepdims=True)
        o = _dot(p.astype(BF16), v)
        den = d if den is None else den + d
        out = o if out is None else out + o
    return out / den


def _lane_halves():
    lane = lax.broadcasted_iota(jnp.int32, (1, LANE), 1)
    return lane < HALF, lane >= HALF


def _attn_kernel(*refs, plan, nseg, scale):
    q_ref = refs[0]
    k_refs = [refs[1 + 2 * i] for i in range(nseg)]
    v_refs = [refs[2 + 2 * i] for i in range(nseg)]
    o_ref = refs[1 + 2 * nseg]
    lo, hi = _lane_halves()
    cache = {}

    def block(kind, seg, blk, swap):
        key = (kind, seg, blk, swap)
        if key not in cache:
            ref = (k_refs if kind == "k" else v_refs)[seg]
            x = ref[:, blk * LANE:(blk + 1) * LANE]
            if swap:
                x = pltpu.roll(x, HALF, 1)
            cache[key] = x.astype(BF16)
        return cache[key]

    parts = {}
    for qblk, qhalf, kblk, swap, vblk, oblk, ohalf in plan:
        q = q_ref[:, qblk * LANE:(qblk + 1) * LANE]
        if qhalf is not None:
            q = jnp.where(lo if qhalf == 0 else hi, q, 0.0)
        q = q.astype(BF16)
        scores = [_dot_nt(q, block("k", s, kblk, swap)) * scale for s in range(nseg)]
        parts[(oblk, ohalf)] = _softmax_pv(scores, [block("v", s, vblk, swap) for s in range(nseg)])
    for oblk in sorted({key[0] for key in parts}):
        o_ref[:, oblk * LANE:(oblk + 1) * LANE] = jnp.where(lo, parts[(oblk, 0)], parts[(oblk, 1)])


def _plan_gqa():
    plan = []
    for h in range(GQ_HEADS):
        blk, half = divmod(h, 2)
        g = h // (GQ_HEADS // GQ_KV_HEADS)
        plan.append((blk, half, 0, half != g, 0, blk, half))
    return tuple(plan)


def _plan_na():
    return tuple((h // 2, h % 2, h // 2, False, h // 2, h // 2, h % 2) for h in range(NA_HEADS))


def _plan_mla():
    return tuple((h, None, h, False, h // 2, h // 2, h % 2) for h in range(ML_HEADS))


def _attention(q, segs, plan, scale, batch, tq_total, name):
    tq = min(tq_total, 256)
    per = tq_total // tq
    q_arr, q_w, q_col, q_off = q
    in_specs = [pl.BlockSpec((tq, q_w), lambda b, i: (q_off + b * per + i, q_col))]
    args = [q_arr]
    for k, v in segs:
        for arr, w, col, rows, stride, off in (k, v):
            in_specs.append(pl.BlockSpec(
                (rows, w), functools.partial(lambda b, i, col, stride, off: (off + b * stride, col),
                                             col=col, stride=stride, off=off)))
            args.append(arr)
    out_w = 512
    return pl.pallas_call(
        functools.partial(_attn_kernel, plan=plan, nseg=len(segs), scale=scale),
        grid=(batch, per),
        in_specs=in_specs,
        out_specs=pl.BlockSpec((tq, out_w), lambda b, i: (b * per + i, 0)),
        out_shape=jax.ShapeDtypeStruct((M_ROWS, out_w), F32),
        compiler_params=_params("parallel", "parallel"),
        name=name,
    )(*args)


def _na_kernel(q_ref, kl_ref, vl_ref, kc_ref, vc_ref, bias_ref, o_ref):
    rows = DEC_SEQ // GRID_W
    r = pl.program_id(1)
    start = jnp.clip(r - NA_WIN_R // 2, 0, rows - NA_WIN_R)
    k0 = pl.multiple_of(start * GRID_W, GRID_W)
    band = NA_WIN_R * GRID_W
    lo, hi = _lane_halves()
    for j in range(NA_HEADS // 2):
        sl = slice(j * LANE, (j + 1) * LANE)
        k_loc = kl_ref[pl.ds(k0, band), sl].astype(BF16)
        v_loc = vl_ref[pl.ds(k0, band), sl].astype(BF16)
        k_ctx = kc_ref[:, sl].astype(BF16)
        v_ctx = vc_ref[:, sl].astype(BF16)
        q_pair = q_ref[:, sl]
        halves = []
        for p in range(2):
            q = jnp.where(lo if p == 0 else hi, q_pair, 0.0).astype(BF16)
            s_loc = _dot_nt(q, k_loc) * NA_SCALE + bias_ref[2 * j + p]
            s_ctx = _dot_nt(q, k_ctx) * NA_SCALE
            halves.append(_softmax_pv([s_loc, s_ctx], [v_loc, v_ctx]))
        o_ref[:, sl] = jnp.where(lo, halves[0], halves[1])


def _na_bias(rpb):
    rows = DEC_SEQ // GRID_W
    r = np.arange(rows)
    key_rows = np.clip(r - NA_WIN_R // 2, 0, rows - NA_WIN_R)[:, None] + np.arange(NA_WIN_R)[None, :]
    col = np.arange(GRID_W)
    col_start = np.clip(col - NA_WIN_C // 2, 0, GRID_W - NA_WIN_C)
    col_ok = (col[None, :] >= col_start[:, None]) & (col[None, :] < col_start[:, None] + NA_WIN_C)
    dr_idx = key_rows - r[:, None] + NA_WIN_R - 1
    dc_idx = np.clip(col[None, :] - col[:, None] + NA_WIN_C - 1, 0, 2 * NA_WIN_C - 2)
    bias = rpb[:, dr_idx[:, None, :, None], dc_idx[None, :, None, :]]
    bias = jnp.where(col_ok[None, None, :, None, :], bias, NEG)
    return bias.transpose(1, 0, 2, 3, 4).reshape(rows, NA_HEADS, GRID_W, NA_WIN_R * GRID_W)


def _na_latent(ymix, cache_k, cache_v, bias, layer):
    rows = DEC_SEQ // GRID_W
    lat0 = 0
    return pl.pallas_call(
        _na_kernel,
        grid=(DEC_BATCH, rows),
        in_specs=[
            pl.BlockSpec((GRID_W, 512), lambda b, r: (lat0 + b * rows + r, COL_NQ)),
            pl.BlockSpec((DEC_SEQ, 512), lambda b, r: (b, COL_NK)),
            pl.BlockSpec((DEC_SEQ, 512), lambda b, r: (b, COL_NV)),
            pl.BlockSpec((PAST_LEN, 512), lambda b, r: (b * DEPTH + layer, 0)),
            pl.BlockSpec((PAST_LEN, 512), lambda b, r: (b * DEPTH + layer, 0)),
            pl.BlockSpec((None, NA_HEADS, GRID_W, NA_WIN_R * GRID_W), lambda b, r: (r, 0, 0, 0)),
        ],
        out_specs=pl.BlockSpec((GRID_W, 512), lambda b, r: (b * rows + r, 0)),
        out_shape=jax.ShapeDtypeStruct((M_ROWS, 512), F32),
        compiler_params=_params("parallel", "parallel"),
        name="na_lat",
    )(ymix, ymix, ymix, cache_k, cache_v, bias)


def _hgrn_kernel(*refs, seq, has_state):
    if has_state:
        (q_ref, ff_ref, fb_ref, v_ref, g_ref, lb_ref, gain_ref, s0_ref,
         o_ref, of_ref, ob_ref, st_ref) = refs
    else:
        (q_ref, ff_ref, fb_ref, v_ref, g_ref, lb_ref, gain_ref,
         o_ref, sout_ref, of_ref, ob_ref, st_ref) = refs
    c = HG_CHUNK
    n_chunks = seq // c
    width = HG_HEADS * HG_DK
    if has_state:
        st_ref[...] = s0_ref[...]
    else:
        st_ref[...] = jnp.zeros(st_ref.shape, F32)

    ri = lax.broadcasted_iota(jnp.int32, (c, c), 0)
    ci = lax.broadcasted_iota(jnp.int32, (c, c), 1)
    tri_f = jnp.where(ci <= ri, 1.0, 0.0).astype(BF16)
    tri_b = jnp.where(ci >= ri, 1.0, 0.0).astype(BF16)
    ones = jnp.ones((HG_DK, HG_DV), BF16)
    rowid = lax.broadcasted_iota(jnp.int32, (c, width), 0)

    def direction(r0, pre_ref, lb, tri, fwd, d):
        f = lb + (1.0 - lb) * jax.nn.sigmoid(pre_ref[pl.ds(r0, c), :])
        k = 1.0 - f
        b = _dot_exact_lhs(tri, jnp.log(f))
        b_last = b[c - 1:c] if fwd else b[0:1]
        q = q_ref[pl.ds(r0, c), :] * (HG_DK ** -0.5)
        v = v_ref[pl.ds(r0, c), :]
        q_in = (q * jnp.exp(b)).astype(BF16)
        k_end = (k * jnp.exp(b_last - b)).astype(BF16)
        dec = jnp.exp(b_last)
        v16 = v.astype(BF16)
        pairs = []
        for s in range(c):
            decay = jnp.exp(jnp.minimum(b - b[s:s + 1], 0.0))
            p = q * decay * k[s:s + 1]
            keep = (rowid >= s) if fwd else (rowid <= s)
            pairs.append(jnp.where(keep, p, 0.0).astype(BF16))
        pairs = jnp.concatenate(pairs, axis=0)
        outs = []
        for h in range(HG_HEADS):
            sl = slice(h * HG_DK, (h + 1) * HG_DK)
            a = _dot(pairs[:, sl], ones)
            st = st_ref[d, h]
            o = _dot_nt(q_in[:, sl], st.astype(BF16))
            for s in range(c):
                o = o + a[s * c:(s + 1) * c] * v[s:s + 1, sl]
            st_ref[d, h] = st * dec[:, sl] + _dot_tn(v16[:, sl], k_end[:, sl])
            outs.append(o)
        return jnp.concatenate(outs, axis=1)

    lb_f = lb_ref[0:1, :]
    lb_b = lb_ref[1:2, :]

    def body(n, carry):
        r0 = pl.multiple_of(n * c, c)
        of_ref[pl.ds(r0, c), :] = direction(r0, ff_ref, lb_f, tri_f, True, 0)
        r1 = pl.multiple_of((n_chunks - 1 - n) * c, c)
        ob_ref[pl.ds(r1, c), :] = direction(r1, fb_ref, lb_b, tri_b, False, 1)
        return carry

    lax.fori_loop(0, n_chunks, body, 0)

    for h in range(HG_HEADS):
        sl = slice(h * HG_DV, (h + 1) * HG_DV)
        o = of_ref[:, sl] + ob_ref[:, sl]
        o_ref[:, sl] = _rms(o) * gain_ref[:, sl] * _silu(g_ref[:, sl])
    if not has_state:
        sout_ref[...] = st_ref[...]


def _hgrn(ymix, lb, gain, batch, seq, s0_t):
    width = HG_HEADS * HG_DK
    has_state = s0_t is not None
    st_shape = (2, HG_HEADS, HG_DV, HG_DK)
    in_specs = [pl.BlockSpec((seq, width), functools.partial(lambda b, col: (b, col), col=col))
                for col in (COL_HQ, COL_HFF, COL_HFB, COL_HI, COL_HG)]
    in_specs += [pl.BlockSpec((2, width), lambda b: (0, 0)), pl.BlockSpec((1, width), lambda b: (0, 0))]
    args = [ymix] * 5 + [lb, gain]
    out_specs = [pl.BlockSpec((seq, width), lambda b: (b, 0))]
    out_shape = [jax.ShapeDtypeStruct((M_ROWS, width), F32)]
    if has_state:
        in_specs.append(pl.BlockSpec((None,) + st_shape, lambda b: (b, 0, 0, 0, 0)))
        args.append(s0_t)
    else:
        out_specs.append(pl.BlockSpec((None,) + st_shape, lambda b: (b, 0, 0, 0, 0)))
        out_shape.append(jax.ShapeDtypeStruct((batch,) + st_shape, F32))
    return pl.pallas_call(
        functools.partial(_hgrn_kernel, seq=seq, has_state=has_state),
        grid=(batch,),
        in_specs=in_specs,
        out_specs=out_specs,
        out_shape=out_shape,
        scratch_shapes=[pltpu.VMEM((seq, width), F32), pltpu.VMEM((seq, width), F32),
                        pltpu.VMEM(st_shape, F32)],
        compiler_params=_params("parallel"),
        name="hgrn_lat" if has_state else "hgrn_ctx",
    )(*args)


def _merge_kernel(oa_ref, ob_ref, oc_ref, od_ref, gt_ref, wb_ref, wo_ref, x_ref, g1_ref, out_ref):
    acc = None
    for n, o_ref in enumerate((oa_ref, ob_ref, oc_ref, od_ref)):
        bo = _dot(o_ref[...].astype(BF16), wb_ref[n])
        term = jax.nn.sigmoid(gt_ref[:, n * D_MODEL:(n + 1) * D_MODEL]) * bo
        acc = term if acc is None else acc + term
    out_ref[...] = x_ref[...] + g1_ref[...] * _dot(acc.astype(BF16), wo_ref[...])


def _merge(branches, gates, w_branch, w_out, x, mod, latent):
    tm = 256
    row = _mod_row(latent, tm)
    tile = lambda w: pl.BlockSpec((tm, w), lambda i: (i, 0))
    return pl.pallas_call(
        _merge_kernel,
        grid=(M_ROWS // tm,),
        in_specs=[tile(BRANCH_W)] * N_BRANCH + [
            tile(GATE_W),
            pl.BlockSpec((N_BRANCH, BRANCH_W, D_MODEL), lambda i: (0, 0, 0)),
            pl.BlockSpec((D_MODEL, D_MODEL), lambda i: (0, 0)),
            tile(D_MODEL),
            pl.BlockSpec((None, None, 1, D_MODEL), lambda i: (row(i), 2, 0, 0)),
        ],
        out_specs=tile(D_MODEL),
        out_shape=jax.ShapeDtypeStruct((M_ROWS, D_MODEL), F32),
        compiler_params=_params("parallel"),
        name="merge_lat" if latent else "merge_ctx",
    )(*branches, gates, w_branch, w_out, x, mod)


FFN_CHUNK = 256
FFN_STEPS = FFN_HIDDEN // FFN_CHUNK


def _ffn_kernel(x_ref, sc_ref, sh_ref, g2_ref, wa_ref, wg_ref, wo_ref, out_ref, h_ref, acc_ref):
    j = pl.program_id(1)

    @pl.when(j == 0)
    def _():
        h = _rms(x_ref[...]) * (1.0 + sc_ref[...]) + sh_ref[...]
        h_ref[...] = h.astype(BF16)
        acc_ref[...] = jnp.zeros(acc_ref.shape, F32)

    h = h_ref[...]
    act = _silu(_dot(h, wg_ref[...])) * _dot(h, wa_ref[...])
    acc_ref[...] += _dot(act.astype(BF16), wo_ref[...])

    @pl.when(j == FFN_STEPS - 1)
    def _():
        out_ref[...] = x_ref[...] + g2_ref[...] * acc_ref[...]


def _ffn(x, mod, w_in, w_out, latent):
    tm = 1024
    row = _mod_row(latent, tm)
    modspec = lambda which: pl.BlockSpec((None, None, 1, D_MODEL), lambda i, j: (row(i), which, 0, 0))
    return pl.pallas_call(
        _ffn_kernel,
        grid=(M_ROWS // tm, FFN_STEPS),
        in_specs=[
            pl.BlockSpec((tm, D_MODEL), lambda i, j: (i, 0)),
            modspec(4), modspec(3), modspec(5),
            pl.BlockSpec((D_MODEL, FFN_CHUNK), lambda i, j: (0, j)),
            pl.BlockSpec((D_MODEL, FFN_CHUNK), lambda i, j: (0, FFN_STEPS + j)),
            pl.BlockSpec((FFN_CHUNK, D_MODEL), lambda i, j: (j, 0)),
        ],
        out_specs=pl.BlockSpec((tm, D_MODEL), lambda i, j: (i, 0)),
        out_shape=jax.ShapeDtypeStruct((M_ROWS, D_MODEL), F32),
        scratch_shapes=[pltpu.VMEM((tm, D_MODEL), BF16), pltpu.VMEM((tm, D_MODEL), F32)],
        compiler_params=_params("parallel", "arbitrary"),
        name="ffn_lat" if latent else "ffn_ctx",
    )(x, mod, mod, mod, w_in, w_in, w_out)


def _final_kernel(x_ref, g_ref, o_ref):
    o_ref[...] = _rms(x_ref[...]) * g_ref[...]


def _final_norm(x, gain):
    tm = 512
    return pl.pallas_call(
        _final_kernel,
        grid=(M_ROWS // tm,),
        in_specs=[pl.BlockSpec((tm, D_MODEL), lambda i: (i, 0)), pl.BlockSpec((1, D_MODEL), lambda i: (0, 0))],
        out_specs=pl.BlockSpec((tm, D_MODEL), lambda i: (i, 0)),
        out_shape=jax.ShapeDtypeStruct((M_ROWS, D_MODEL), F32),
        compiler_params=_params("parallel"),
        name="final_norm",
    )(x, gain.reshape(1, D_MODEL))


def _rope_tables():
    t = jnp.arange(DEC_SEQ)
    row = (t // GRID_W).astype(F32)[:, None]
    col = (t % GRID_W).astype(F32)[:, None]

    def angles(rot_dim):
        n_freq = rot_dim // 4
        inv_freq = ROPE_THETA ** (-jnp.arange(n_freq, dtype=F32) / n_freq)
        return jnp.concatenate([row * inv_freq, col * inv_freq], axis=-1)

    def expand(ang):
        cos = jnp.repeat(jnp.cos(ang), 2, axis=-1)
        sin = jnp.repeat(jnp.sin(ang), 2, axis=-1)
        even = (jnp.arange(cos.shape[-1]) % 2 == 0)[None, :]
        return cos, jnp.where(even, -sin, 0.0), jnp.where(even, 0.0, sin)

    gq = [jnp.concatenate([a, a], axis=-1) for a in expand(angles(GQ_HD))]
    pad_l = LANE - ML_NOPE - ML_ROPE
    ml = []
    for idx, a in enumerate(expand(angles(ML_ROPE))):
        fill = 1.0 if idx == 0 else 0.0
        ml.append(jnp.concatenate([jnp.full((DEC_SEQ, ML_NOPE), fill, F32), a,
                                   jnp.full((DEC_SEQ, pad_l), fill, F32)], axis=-1))
    return {"gq": jnp.stack(gq), "ml": jnp.stack(ml)}


def _layer_weights(l, w_in, gq_q_gain, gq_k_gain, ml_q_a_gain, ml_kv_a_gain, ml_w_q_b, ml_w_kv_b,
                   w_branch, w_out, w_ffn_in, w_ffn_out, hg_gain, avg):
    w = w_in[l]
    zeros = lambda n: jnp.zeros((D_MODEL, n), F32)
    w_mix = jnp.concatenate([
        w[:, 0:3072], w[:, 3328:4864], w[:, 4864:5120], w[:, 3072:3328], w[:, 5120:5248],
        zeros(ML_NOPE), w[:, 5248:5280], zeros(LANE - ML_NOPE - ML_ROPE)], axis=1).astype(BF16)
    assert w_mix.shape[1] == MIX_W
    w_gate = w[:, 5280:].astype(BF16)
    qb = ml_w_q_b[l].reshape(ML_Q_RANK, ML_HEADS, ML_NOPE + ML_ROPE)
    qb = jnp.pad(qb, ((0, 0), (0, 0), (0, LANE - ML_NOPE - ML_ROPE))).reshape(ML_Q_RANK, ML_HEADS * LANE)
    kvb = ml_w_kv_b[l].reshape(ML_KV_RANK, ML_HEADS, ML_NOPE + ML_V)
    wk = jnp.pad(kvb[:, :, :ML_NOPE], ((0, 0), (0, 0), (0, LANE - ML_NOPE))).reshape(ML_KV_RANK, ML_HEADS * LANE)
    wv = kvb[:, :, ML_NOPE:].reshape(ML_KV_RANK, ML_HEADS * ML_V)
    return {
        "w_mix": w_mix, "w_gate": w_gate,
        "gq_q_gain": jnp.tile(gq_q_gain[l], GQ_HEADS).reshape(1, -1),
        "gq_k_gain": jnp.tile(gq_k_gain[l], GQ_KV_HEADS).reshape(1, -1),
        "ml_q_gain": ml_q_a_gain[l].reshape(1, -1), "ml_kv_gain": ml_kv_a_gain[l].reshape(1, -1),
        "w_qb": qb.astype(BF16), "w_k": wk.astype(BF16), "w_v": wv.astype(BF16),
        "w_branch": w_branch[l].astype(BF16), "w_out": w_out[l].astype(BF16),
        "w_ffn_in": w_ffn_in[l].astype(BF16), "w_ffn_out": w_ffn_out[l].astype(BF16),
        "hg_gain": jnp.tile(hg_gain[l], HG_HEADS).reshape(1, -1), "avg": avg,
    }


def _seg(arr, width, col, rows, stride, off):
    return (arr, width, col, rows, stride, off)


def kernel(x_prompt, x_sample, state_hgrn, cache_gqa_k, cache_gqa_v, cache_na_k, cache_na_v, cache_mla_ckv, cache_mla_krope, c, c_ctx, w_ada, b_ada, w_in, hg_lb_logits, hg_gain, gq_q_gain, gq_k_gain, na_rpb, ml_q_a_gain, ml_kv_a_gain, ml_w_q_b, ml_w_kv_b, w_branch, w_out, w_ffn_in, w_ffn_out, final_gain):
    cond8 = jnp.concatenate([c_ctx[None, :], c, jnp.zeros((8 - 1 - DEC_BATCH, D_MODEL), F32)], axis=0)
    mods = _ada(cond8, w_ada, b_ada)

    lb = jnp.cumsum(jax.nn.softmax(hg_lb_logits.astype(F32), axis=0), axis=0)
    lb = lb - lb[:1]
    avg = jnp.asarray(np.kron(np.eye(512 // GQ_HD), np.full((GQ_HD, GQ_HD), 1.0 / GQ_HD)), BF16)
    tabs = _rope_tables()

    gqk_c = cache_gqa_k.reshape(DEC_BATCH * DEPTH * PAST_LEN, GQ_KV_HEADS * GQ_HD)
    gqv_c = cache_gqa_v.reshape(DEC_BATCH * DEPTH * PAST_LEN, GQ_KV_HEADS * GQ_HD)
    nak_c = cache_na_k.reshape(DEC_BATCH * DEPTH * PAST_LEN, NA_HEADS * NA_HD)
    nav_c = cache_na_v.reshape(DEC_BATCH * DEPTH * PAST_LEN, NA_HEADS * NA_HD)
    mckv_c = cache_mla_ckv.reshape(DEC_BATCH * DEPTH * PAST_LEN, ML_KV_RANK)
    mkr_c = jnp.pad(cache_mla_krope.reshape(DEC_BATCH * DEPTH * PAST_LEN, ML_ROPE),
                    ((0, 0), (ML_NOPE, LANE - ML_NOPE - ML_ROPE)))

    xp = x_prompt.reshape(M_ROWS, D_MODEL)
    xs = x_sample.reshape(M_ROWS, D_MODEL)
    new = []
    for l in range(DEPTH):
        lw = _layer_weights(l, w_in, gq_q_gain, gq_k_gain, ml_q_a_gain, ml_kv_a_gain, ml_w_q_b,
                            ml_w_kv_b, w_branch, w_out, w_ffn_in, w_ffn_out, hg_gain, avg)
        mod = mods[l]

        ymix = _inproj(xp, mod, lw["w_mix"], False, 1792, "inproj_mix_ctx")
        gates = _inproj(xp, mod, lw["w_gate"], False, 2048, "inproj_gate_ctx")
        qb, kb, qd, ckv, kd, vd = _prep(ymix, lw, tabs, False)
        out_a, st = _hgrn(ymix, lb[l], lw["hg_gain"], BATCH, SEQ, None)
        out_b = _attention((qb, 512, 0, 0),
                           [(_seg(kb, LANE, 0, SEQ, 1, 0), _seg(ymix, LANE, COL_GV, SEQ, 1, 0))],
                           _plan_gqa(), GQ_SCALE, BATCH, SEQ, "gqa_ctx")
        out_c = _attention((ymix, 512, COL_NQ, 0),
                           [(_seg(ymix, 512, COL_NK, SEQ, 1, 0), _seg(ymix, 512, COL_NV, SEQ, 1, 0))],
                           _plan_na(), NA_SCALE, BATCH, SEQ, "na_ctx")
        out_d = _attention((qd, ML_HEADS * LANE, 0, 0),
                           [(_seg(kd, ML_HEADS * LANE, 0, SEQ, 1, 0), _seg(vd, 512, 0, SEQ, 1, 0))],
                           _plan_mla(), ML_SCALE, BATCH, SEQ, "mla_ctx")
        xp = _merge((out_a, out_b, out_c, out_d), gates, lw["w_branch"], lw["w_out"], xp, mod, False)
        xp = _ffn(xp, mod, lw["w_ffn_in"], lw["w_ffn_out"], False)
        new.append((
            jnp.swapaxes(st, -1, -2),
            kb.reshape(BATCH, SEQ, GQ_KV_HEADS, GQ_HD),
            ymix[:, COL_GV * LANE:(COL_GV + 1) * LANE].reshape(BATCH, SEQ, GQ_KV_HEADS, GQ_HD),
            ymix[:, COL_NK * 512:(COL_NK + 1) * 512].reshape(BATCH, SEQ, NA_HEADS, NA_HD),
            ymix[:, COL_NV * 512:(COL_NV + 1) * 512].reshape(BATCH, SEQ, NA_HEADS, NA_HD),
            ckv.reshape(BATCH, SEQ, ML_KV_RANK),
            ymix[:, COL_MKR * LANE + ML_NOPE:COL_MKR * LANE + ML_NOPE + ML_ROPE].reshape(BATCH, SEQ, ML_ROPE),
        ))

        ymix = _inproj(xs, mod, lw["w_mix"], True, 1792, "inproj_mix_lat")
        gates = _inproj(xs, mod, lw["w_gate"], True, 2048, "inproj_gate_lat")
        qb, kb, qd, ckv, kd, vd = _prep(ymix, lw, tabs, True)
        kd_c, vd_c = _mla_cache(mckv_c, mkr_c, lw)
        s0_t = jnp.swapaxes(state_hgrn[:, l], -1, -2)
        out_a, = _hgrn(ymix, lb[l], lw["hg_gain"], DEC_BATCH, DEC_SEQ, s0_t)
        out_b = _attention((qb, 512, 0, 0),
                           [(_seg(gqk_c, LANE, 0, PAST_LEN, DEPTH, l), _seg(gqv_c, LANE, 0, PAST_LEN, DEPTH, l)),
                            (_seg(kb, LANE, 0, DEC_SEQ, 1, 0), _seg(ymix, LANE, COL_GV, DEC_SEQ, 1, 0))],
                           _plan_gqa(), GQ_SCALE, DEC_BATCH, DEC_SEQ, "gqa_lat")
        out_c = _na_latent(ymix, nak_c, nav_c, _na_bias(na_rpb[l]), l)
        out_d = _attention((qd, ML_HEADS * LANE, 0, 0),
                           [(_seg(kd_c, ML_HEADS * LANE, 0, PAST_LEN, DEPTH, l), _seg(vd_c, 512, 0, PAST_LEN, DEPTH, l)),
                            (_seg(kd, ML_HEADS * LANE, 0, DEC_SEQ, 1, 0), _seg(vd, 512, 0, DEC_SEQ, 1, 0))],
                           _plan_mla(), ML_SCALE, DEC_BATCH, DEC_SEQ, "mla_lat")
        xs = _merge((out_a, out_b, out_c, out_d), gates, lw["w_branch"], lw["w_out"], xs, mod, True)
        xs = _ffn(xs, mod, lw["w_ffn_in"], lw["w_ffn_out"], True)

    y_prompt = _final_norm(xp, final_gain).reshape(BATCH, SEQ, D_MODEL)
    y_sample = _final_norm(xs, final_gain).reshape(DEC_BATCH, DEC_SEQ, D_MODEL)
    stacked = tuple(jnp.stack([n[i] for n in new], axis=1) for i in range(7))
    return (y_prompt, y_sample) + stacked
```

```python
import functools

import numpy as np
import jax
import jax.numpy as jnp
from jax import lax
from jax.experimental import pallas as pl
from jax.experimental.pallas import tpu as pltpu

F32 = jnp.float32
BF16 = jnp.bfloat16

D_MODEL = 1024
BATCH = 16
SEQ = 256
DEPTH = 2
DEC_BATCH = 4
DEC_SEQ = 1024
PAST_LEN = 512
GRID_W = 64
EPS = 1e-6
ROPE_THETA = 10000.0
N_BRANCH = 4
BRANCH_W = 512
HG_HEADS = 4
HG_DK = 128
HG_DV = 128
GQ_HEADS = 8
GQ_KV_HEADS = 2
GQ_HD = 64
NA_HEADS = 8
NA_HD = 64
NA_WIN_R = 8
NA_WIN_C = 16
ML_HEADS = 8
ML_NOPE = 64
ML_ROPE = 32
ML_V = 64
ML_Q_RANK = 256
ML_KV_RANK = 128
FFN_HIDDEN = 2816
GQ_SCALE = GQ_HD ** -0.5
NA_SCALE = NA_HD ** -0.5
ML_SCALE = (ML_NOPE + ML_ROPE) ** -0.5

M_ROWS = BATCH * SEQ
assert M_ROWS == DEC_BATCH * DEC_SEQ

LANE = 128
HALF = 64
MIX_W = 5376
GATE_W = N_BRANCH * D_MODEL
HG_CHUNK = 16
NEG = -1e30
VMEM_LIMIT = 56 * 1024 * 1024

COL_HQ, COL_HFF, COL_HFB, COL_HI, COL_HG, COL_GQ, COL_NQ, COL_NK, COL_NV = range(9)
COL_MQA = 18
COL_GK, COL_GV, COL_MKVA, COL_MKR = 38, 39, 40, 41


def _dot(a, b):
    return jnp.dot(a, b, preferred_element_type=F32)


def _dot_nt(a, b):
    return lax.dot_general(a, b, (((1,), (1,)), ((), ())), preferred_element_type=F32)


def _dot_tn(a, b):
    return lax.dot_general(a, b, (((0,), (0,)), ((), ())), preferred_element_type=F32)


def _split3(x):
    x1 = x.astype(BF16)
    r1 = x - x1.astype(F32)
    x2 = r1.astype(BF16)
    x3 = (r1 - x2.astype(F32)).astype(BF16)
    return x1, x2, x3


def _dot_exact_lhs(a_bf16, x):
    x1, x2, x3 = _split3(x)
    return (_dot(a_bf16, x3) + _dot(a_bf16, x2)) + _dot(a_bf16, x1)


def _dot_exact_rhs(x, b_bf16):
    x1, x2, x3 = _split3(x)
    return (_dot(x3, b_bf16) + _dot(x2, b_bf16)) + _dot(x1, b_bf16)


def _dot_f32(a, b):
    a1, a2, a3 = _split3(a)
    b1, b2, b3 = _split3(b)
    small = (_dot(a1, b3) + _dot(a3, b1)) + _dot(a2, b2)
    mid = _dot(a1, b2) + _dot(a2, b1)
    return (small + mid) + _dot(a1, b1)


def _rms(x):
    return x * lax.rsqrt(jnp.mean(x * x, axis=-1, keepdims=True) + EPS)


def _silu(x):
    return x * jax.nn.sigmoid(x)


def _params(*sem):
    return pltpu.CompilerParams(dimension_semantics=sem, vmem_limit_bytes=VMEM_LIMIT)


def _mod_row(latent, tm):
    if latent:
        return lambda i: 1 + (i * tm) // DEC_SEQ
    return lambda i: 0


def _ada_kernel(c_ref, w_ref, b_ref, o_ref):
    c = c_ref[...]
    o_ref[...] = _dot_f32(_silu(c), w_ref[...]) + b_ref[...]


def _ada(cond8, w_ada, b_ada):
    tn = 1536
    out = pl.pallas_call(
        _ada_kernel,
        grid=(DEPTH, 6 * D_MODEL // tn),
        in_specs=[
            pl.BlockSpec((8, D_MODEL), lambda l, j: (0, 0)),
            pl.BlockSpec((None, D_MODEL, tn), lambda l, j: (l, 0, j)),
            pl.BlockSpec((None, 1, tn), lambda l, j: (l, 0, j)),
        ],
        out_specs=pl.BlockSpec((None, 8, tn), lambda l, j: (l, 0, j)),
        out_shape=jax.ShapeDtypeStruct((DEPTH, 8, 6 * D_MODEL), F32),
        compiler_params=_params("parallel", "parallel"),
        name="ada",
    )(cond8, w_ada, b_ada.reshape(DEPTH, 1, 6 * D_MODEL))
    return out.reshape(DEPTH, 8, 6, 1, D_MODEL)


def _inproj_kernel(x_ref, sc_ref, sh_ref, w_ref, o_ref, h_ref):
    @pl.when(pl.program_id(1) == 0)
    def _():
        h = _rms(x_ref[...]) * (1.0 + sc_ref[...]) + sh_ref[...]
        h_ref[...] = h.astype(BF16)

    o_ref[...] = _dot(h_ref[...], w_ref[...])


def _inproj(x, mod, w, latent, tn, name):
    tm = 1024
    n = w.shape[1]
    row = _mod_row(latent, tm)
    return pl.pallas_call(
        _inproj_kernel,
        grid=(M_ROWS // tm, n // tn),
        in_specs=[
            pl.BlockSpec((tm, D_MODEL), lambda i, j: (i, 0)),
            pl.BlockSpec((None, None, 1, D_MODEL), lambda i, j: (row(i), 1, 0, 0)),
            pl.BlockSpec((None, None, 1, D_MODEL), lambda i, j: (row(i), 0, 0, 0)),
            pl.BlockSpec((D_MODEL, tn), lambda i, j: (0, j)),
        ],
        out_specs=pl.BlockSpec((tm, tn), lambda i, j: (i, j)),
        out_shape=jax.ShapeDtypeStruct((M_ROWS, n), F32),
        scratch_shapes=[pltpu.VMEM((tm, D_MODEL), BF16)],
        compiler_params=_params("parallel", "arbitrary"),
        name=name,
    )(x, mod, mod, w)


def _rope(x, tab_ref, reps):
    w = x.shape[1]
    c = jnp.concatenate([tab_ref[0]] * reps, axis=1) if reps > 1 else tab_ref[0]
    se = jnp.concatenate([tab_ref[1]] * reps, axis=1) if reps > 1 else tab_ref[1]
    so = jnp.concatenate([tab_ref[2]] * reps, axis=1) if reps > 1 else tab_ref[2]
    return x * c + pltpu.roll(x, w - 1, 1) * se + pltpu.roll(x, 1, 1) * so


def _head_rms(x, gain, avg_bf16):
    ms = _dot_exact_rhs(x * x, avg_bf16)
    return x * lax.rsqrt(ms + EPS) * gain


def _prep_kernel(*refs, rope):
    (gq_ref, gk_ref, mqa_ref, mkva_ref, mkr_ref, gqg_ref, gkg_ref, mqg_ref, mkvg_ref,
     wqb_ref, wk_ref, wv_ref, avg_ref) = refs[:13]
    if rope:
        gtab_ref, mtab_ref = refs[13:15]
        outs = refs[15:]
    else:
        outs = refs[13:]
    qb_ref, kb_ref, qd_ref, ckv_ref, kd_ref, vd_ref = outs

    q = _head_rms(gq_ref[...], gqg_ref[...], avg_ref[...])
    k = _head_rms(gk_ref[...], gkg_ref[...], avg_ref[0:LANE, 0:LANE])
    qd = _dot((_rms(mqa_ref[...]) * mqg_ref[...]).astype(BF16), wqb_ref[...])
    ckv = _rms(mkva_ref[...]) * mkvg_ref[...]
    kr = mkr_ref[...]
    if rope:
        q = _rope(q, gtab_ref, GQ_HEADS * GQ_HD // LANE)
        k = _rope(k, gtab_ref, 1)
        qd = _rope(qd, mtab_ref, ML_HEADS)
        kr = _rope(kr, mtab_ref, 1)
    qb_ref[...] = q
    kb_ref[...] = k
    qd_ref[...] = qd
    ckv_ref[...] = ckv
    cb = ckv.astype(BF16)
    kd_ref[...] = _dot(cb, wk_ref[...]) + jnp.concatenate([kr] * ML_HEADS, axis=1)
    vd_ref[...] = _dot(cb, wv_ref[...])


def _prep(ymix, lw, tabs, latent):
    tm = 512
    const = lambda i: (0, 0)
    in_specs = [
        pl.BlockSpec((tm, 512), lambda i: (i, COL_GQ)),
        pl.BlockSpec((tm, LANE), lambda i: (i, COL_GK)),
        pl.BlockSpec((tm, 256), lambda i: (i, COL_MQA)),
        pl.BlockSpec((tm, LANE), lambda i: (i, COL_MKVA)),
        pl.BlockSpec((tm, LANE), lambda i: (i, COL_MKR)),
        pl.BlockSpec((1, 512), const),
        pl.BlockSpec((1, LANE), const),
        pl.BlockSpec((1, 256), const),
        pl.BlockSpec((1, LANE), const),
        pl.BlockSpec((ML_Q_RANK, ML_HEADS * LANE), const),
        pl.BlockSpec((ML_KV_RANK, ML_HEADS * LANE), const),
        pl.BlockSpec((ML_KV_RANK, ML_HEADS * ML_V), const),
        pl.BlockSpec((512, 512), const),
    ]
    args = [ymix, ymix, ymix, ymix, ymix, lw["gq_q_gain"], lw["gq_k_gain"], lw["ml_q_gain"],
            lw["ml_kv_gain"], lw["w_qb"], lw["w_k"], lw["w_v"], lw["avg"]]
    if latent:
        per = DEC_SEQ // tm
        in_specs += [pl.BlockSpec((3, tm, LANE), lambda i: (0, i % per, 0))] * 2
        args += [tabs["gq"], tabs["ml"]]
    widths = (512, LANE, ML_HEADS * LANE, LANE, ML_HEADS * LANE, ML_HEADS * ML_V)
    return pl.pallas_call(
        functools.partial(_prep_kernel, rope=latent),
        grid=(M_ROWS // tm,),
        in_specs=in_specs,
        out_specs=[pl.BlockSpec((tm, w), lambda i: (i, 0)) for w in widths],
        out_shape=[jax.ShapeDtypeStruct((M_ROWS, w), F32) for w in widths],
        compiler_params=_params("parallel"),
        name="prep_lat" if latent else "prep_ctx",
    )(*args)


def _mla_cache_kernel(ckv_ref, kr_ref, wk_ref, wv_ref, kd_ref, vd_ref):
    cb = ckv_ref[...].astype(BF16)
    kd_ref[...] = _dot(cb, wk_ref[...]) + jnp.concatenate([kr_ref[...]] * ML_HEADS, axis=1)
    vd_ref[...] = _dot(cb, wv_ref[...])


def _mla_cache(ckv, kr_blk, lw, layer):
    rows = DEC_BATCH * PAST_LEN
    tm = PAST_LEN
    const = lambda i: (0, 0)
    return pl.pallas_call(
        _mla_cache_kernel,
        grid=(DEC_BATCH,),
        in_specs=[
            pl.BlockSpec((tm, LANE), lambda i: (i * DEPTH + layer, 0)),
            pl.BlockSpec((tm, LANE), lambda i: (i * DEPTH + layer, 0)),
            pl.BlockSpec((ML_KV_RANK, ML_HEADS * LANE), const),
            pl.BlockSpec((ML_KV_RANK, ML_HEADS * ML_V), const),
        ],
        out_specs=[pl.BlockSpec((tm, ML_HEADS * LANE), lambda i: (i, 0)),
                   pl.BlockSpec((tm, ML_HEADS * ML_V), lambda i: (i, 0))],
        out_shape=[jax.ShapeDtypeStruct((rows, ML_HEADS * LANE), F32),
                   jax.ShapeDtypeStruct((rows, ML_HEADS * ML_V), F32)],
        compiler_params=_params("parallel"),
        name="mla_cache",
    )(ckv, kr_blk, lw["w_k"], lw["w_v"])


def _softmax_pv(scores, values):
    m = scores[0].max(axis=-1, keepdims=True)
    for s in scores[1:]:
        m = jnp.maximum(m, s.max(axis=-1, keepdims=True))
    den = None
    out = None
    for s, v in zip(scores, values):
        p = jnp.exp(s - m)
        d = p.sum(axis=-1, keepdims=True)
        o = _dot(p.astype(BF16), v)
        den = d if den is None else den + d
        out = o if out is None else out + o
    return out / den


def _lane_halves():
    lane = lax.broadcasted_iota(jnp.int32, (1, LANE), 1)
    return lane < HALF, lane >= HALF


def _attn_kernel(*refs, plan, nseg, scale):
    q_ref = refs[0]
    k_refs = [refs[1 + 2 * i] for i in range(nseg)]
    v_refs = [refs[2 + 2 * i] for i in range(nseg)]
    o_ref = refs[1 + 2 * nseg]
    lo, hi = _lane_halves()
    cache = {}

    def block(kind, seg, blk, swap):
        key = (kind, seg, blk, swap)
        if key not in cache:
            ref = (k_refs if kind == "k" else v_refs)[seg]
            x = ref[:, blk * LANE:(blk + 1) * LANE]
            if swap:
                x = pltpu.roll(x, HALF, 1)
            cache[key] = x.astype(BF16)
        return cache[key]

    parts = {}
    for qblk, qhalf, kblk, swap, vblk, oblk, ohalf in plan:
        q = q_ref[:, qblk * LANE:(qblk + 1) * LANE]
        if qhalf is not None:
            q = jnp.where(lo if qhalf == 0 else hi, q, 0.0)
        q = q.astype(BF16)
        scores = [_dot_nt(q, block("k", s, kblk, swap)) * scale for s in range(nseg)]
        parts[(oblk, ohalf)] = _softmax_pv(scores, [block("v", s, vblk, swap) for s in range(nseg)])
    for oblk in sorted({key[0] for key in parts}):
        o_ref[:, oblk * LANE:(oblk + 1) * LANE] = jnp.where(lo, parts[(oblk, 0)], parts[(oblk, 1)])


def _plan_gqa():
    plan = []
    for h in range(GQ_HEADS):
        blk, half = divmod(h, 2)
        g = h // (GQ_HEADS // GQ_KV_HEADS)
        plan.append((blk, half, 0, half != g, 0, blk, half))
    return tuple(plan)


def _plan_na():
    return tuple((h // 2, h % 2, h // 2, False, h // 2, h // 2, h % 2) for h in range(NA_HEADS))


def _plan_mla():
    return tuple((h, None, h, False, h // 2, h // 2, h % 2) for h in range(ML_HEADS))


def _attention(q, segs, plan, scale, batch, tq_total, name):
    tq = min(tq_total, 256)
    per = tq_total // tq
    q_arr, q_w, q_col, q_off = q
    in_specs = [pl.BlockSpec((tq, q_w), lambda b, i: (q_off + b * per + i, q_col))]
    args = [q_arr]
    for k, v in segs:
        for arr, w, col, rows, stride, off in (k, v):
            in_specs.append(pl.BlockSpec(
                (rows, w), functools.partial(lambda b, i, col, stride, off: (off + b * stride, col),
                                             col=col, stride=stride, off=off)))
            args.append(arr)
    out_w = 512
    return pl.pallas_call(
        functools.partial(_attn_kernel, plan=plan, nseg=len(segs), scale=scale),
        grid=(batch, per),
        in_specs=in_specs,
        out_specs=pl.BlockSpec((tq, out_w), lambda b, i: (b * per + i, 0)),
        out_shape=jax.ShapeDtypeStruct((M_ROWS, out_w), F32),
        compiler_params=_params("parallel", "parallel"),
        name=name,
    )(*args)


NA_DR = 2 * NA_WIN_R - 1
NA_DC = 2 * NA_WIN_C - 1


def _na_kernel(q_ref, kl_ref, vl_ref, kc_ref, vc_ref, tz_ref, o_ref):
    rows = DEC_SEQ // GRID_W
    r = pl.program_id(1)
    start = jnp.clip(r - NA_WIN_R // 2, 0, rows - NA_WIN_R)
    k0 = pl.multiple_of(start * GRID_W, GRID_W)
    dr0 = start - r + NA_WIN_R - 1
    band = NA_WIN_R * GRID_W
    lo, hi = _lane_halves()
    for j in range(NA_HEADS // 2):
        sl = slice(j * LANE, (j + 1) * LANE)
        k_loc = kl_ref[pl.ds(k0, band), sl].astype(BF16)
        v_loc = vl_ref[pl.ds(k0, band), sl].astype(BF16)
        k_ctx = kc_ref[:, sl].astype(BF16)
        v_ctx = vc_ref[:, sl].astype(BF16)
        q_pair = q_ref[:, sl]
        halves = []
        for p in range(2):
            h = 2 * j + p
            q = jnp.where(lo if p == 0 else hi, q_pair, 0.0).astype(BF16)
            bias = jnp.concatenate([tz_ref[h, dr0 + 2 * i] for i in range(NA_WIN_R // 2)], axis=1)
            s_loc = _dot_nt(q, k_loc) * NA_SCALE + bias
            s_ctx = _dot_nt(q, k_ctx) * NA_SCALE
            halves.append(_softmax_pv([s_loc, s_ctx], [v_loc, v_ctx]))
        o_ref[:, sl] = jnp.where(lo, halves[0], halves[1])


def _na_bias_kernel(rpb_ref, sel_ref, neg_ref, o_ref):
    o_ref[...] = _dot_exact_rhs(rpb_ref[...], sel_ref[...]) + neg_ref[...]


def _na_bias_tables(rpb):
    col = np.arange(GRID_W)
    col_start = np.clip(col - NA_WIN_C // 2, 0, GRID_W - NA_WIN_C)
    col_ok = (col[None, :] >= col_start[:, None]) & (col[None, :] < col_start[:, None] + NA_WIN_C)
    dc = col[None, :] - col[:, None] + NA_WIN_C - 1
    kpad = 32
    sel = (dc[None, :, :] == np.arange(kpad)[:, None, None]) & col_ok[None]
    sel = jnp.asarray(sel.reshape(kpad, GRID_W * GRID_W), BF16)
    neg = jnp.asarray(np.where(col_ok, 0.0, NEG).reshape(1, GRID_W * GRID_W), F32)
    rpb2 = jnp.pad(rpb.reshape(NA_HEADS * NA_DR, NA_DC), ((0, 0), (0, kpad - NA_DC)))
    n_rows = NA_HEADS * NA_DR
    full = lambda shape: pl.BlockSpec(shape, lambda i: (0, 0))
    t = pl.pallas_call(
        _na_bias_kernel,
        grid=(1,),
        in_specs=[full((n_rows, kpad)), full((kpad, GRID_W * GRID_W)), full((1, GRID_W * GRID_W))],
        out_specs=full((n_rows, GRID_W * GRID_W)),
        out_shape=jax.ShapeDtypeStruct((n_rows, GRID_W * GRID_W), F32),
        compiler_params=_params("arbitrary"),
        name="na_bias",
    )(rpb2, sel, neg)
    t = t.reshape(NA_HEADS, NA_DR, GRID_W, GRID_W)
    return jnp.concatenate([t[:, :-1], t[:, 1:]], axis=-1)


def _na_latent(ymix, cache_k, cache_v, tz, layer):
    rows = DEC_SEQ // GRID_W
    return pl.pallas_call(
        _na_kernel,
        grid=(DEC_BATCH, rows),
        in_specs=[
            pl.BlockSpec((GRID_W, 512), lambda b, r: (b * rows + r, COL_NQ)),
            pl.BlockSpec((DEC_SEQ, 512), lambda b, r: (b, COL_NK)),
            pl.BlockSpec((DEC_SEQ, 512), lambda b, r: (b, COL_NV)),
            pl.BlockSpec((PAST_LEN, 512), lambda b, r: (b * DEPTH + layer, 0)),
            pl.BlockSpec((PAST_LEN, 512), lambda b, r: (b * DEPTH + layer, 0)),
            pl.BlockSpec((NA_HEADS, NA_DR - 1, GRID_W, LANE), lambda b, r: (0, 0, 0, 0)),
        ],
        out_specs=pl.BlockSpec((GRID_W, 512), lambda b, r: (b * rows + r, 0)),
        out_shape=jax.ShapeDtypeStruct((M_ROWS, 512), F32),
        compiler_params=_params("parallel", "parallel"),
        name="na_lat",
    )(ymix, ymix, ymix, cache_k, cache_v, tz)


def _hgrn_kernel(*refs, seq, has_state):
    if has_state:
        (q_ref, ff_ref, fb_ref, v_ref, g_ref, lb_ref, gain_ref, s0_ref,
         o_ref, of_ref, ob_ref, st_ref) = refs
    else:
        (q_ref, ff_ref, fb_ref, v_ref, g_ref, lb_ref, gain_ref,
         o_ref, sout_ref, of_ref, ob_ref, st_ref) = refs
    c = HG_CHUNK
    n_chunks = seq // c
    width = HG_HEADS * HG_DK
    if has_state:
        st_ref[...] = s0_ref[...]
    else:
        st_ref[...] = jnp.zeros(st_ref.shape, F32)

    ri = lax.broadcasted_iota(jnp.int32, (c, c), 0)
    ci = lax.broadcasted_iota(jnp.int32, (c, c), 1)
    tri_f = jnp.where(ci <= ri, 1.0, 0.0).astype(BF16)
    tri_b = jnp.where(ci >= ri, 1.0, 0.0).astype(BF16)
    ones = jnp.ones((HG_DK, HG_DV), BF16)
    rowid = lax.broadcasted_iota(jnp.int32, (c, width), 0)

    def direction(r0, pre_ref, lb, tri, fwd, d):
        f = lb + (1.0 - lb) * jax.nn.sigmoid(pre_ref[pl.ds(r0, c), :])
        k = 1.0 - f
        b = _dot_exact_lhs(tri, jnp.log(f))
        b_last = b[c - 1:c] if fwd else b[0:1]
        q = q_ref[pl.ds(r0, c), :] * (HG_DK ** -0.5)
        v = v_ref[pl.ds(r0, c), :]
        q_in = (q * jnp.exp(b)).astype(BF16)
        k_end = (k * jnp.exp(b_last - b)).astype(BF16)
        dec = jnp.exp(b_last)
        v16 = v.astype(BF16)
        pairs = []
        for s in range(c):
            decay = jnp.exp(jnp.minimum(b - b[s:s + 1], 0.0))
            p = q * decay * k[s:s + 1]
            keep = (rowid >= s) if fwd else (rowid <= s)
            pairs.append(jnp.where(keep, p, 0.0).astype(BF16))
        pairs = jnp.concatenate(pairs, axis=0)
        outs = []
        for h in range(HG_HEADS):
            sl = slice(h * HG_DK, (h + 1) * HG_DK)
            a = _dot(pairs[:, sl], ones)
            st = st_ref[d, h]
            o = _dot_nt(q_in[:, sl], st.astype(BF16))
            for s in range(c):
                o = o + a[s * c:(s + 1) * c] * v[s:s + 1, sl]
            st_ref[d, h] = st * dec[:, sl] + _dot_tn(v16[:, sl], k_end[:, sl])
            outs.append(o)
        return jnp.concatenate(outs, axis=1)

    lb_f = lb_ref[0:1, :]
    lb_b = lb_ref[1:2, :]

    def body(n, carry):
        r0 = pl.multiple_of(n * c, c)
        of_ref[pl.ds(r0, c), :] = direction(r0, ff_ref, lb_f, tri_f, True, 0)
        r1 = pl.multiple_of((n_chunks - 1 - n) * c, c)
        ob_ref[pl.ds(r1, c), :] = direction(r1, fb_ref, lb_b, tri_b, False, 1)
        return carry

    lax.fori_loop(0, n_chunks, body, 0)

    for h in range(HG_HEADS):
        sl = slice(h * HG_DV, (h + 1) * HG_DV)
        o = of_ref[:, sl] + ob_ref[:, sl]
        o_ref[:, sl] = _rms(o) * gain_ref[:, sl] * _silu(g_ref[:, sl])
    if not has_state:
        sout_ref[...] = st_ref[...]


def _hgrn(ymix, lb, gain, batch, seq, s0_t):
    width = HG_HEADS * HG_DK
    has_state = s0_t is not None
    st_shape = (2, HG_HEADS, HG_DV, HG_DK)
    in_specs = [pl.BlockSpec((seq, width), functools.partial(lambda b, col: (b, col), col=col))
                for col in (COL_HQ, COL_HFF, COL_HFB, COL_HI, COL_HG)]
    in_specs += [pl.BlockSpec((2, width), lambda b: (0, 0)), pl.BlockSpec((1, width), lambda b: (0, 0))]
    args = [ymix] * 5 + [lb, gain]
    out_specs = [pl.BlockSpec((seq, width), lambda b: (b, 0))]
    out_shape = [jax.ShapeDtypeStruct((M_ROWS, width), F32)]
    if has_state:
        in_specs.append(pl.BlockSpec((None,) + st_shape, lambda b: (b, 0, 0, 0, 0)))
        args.append(s0_t)
    else:
        out_specs.append(pl.BlockSpec((None,) + st_shape, lambda b: (b, 0, 0, 0, 0)))
        out_shape.append(jax.ShapeDtypeStruct((batch,) + st_shape, F32))
    return pl.pallas_call(
        functools.partial(_hgrn_kernel, seq=seq, has_state=has_state),
        grid=(batch,),
        in_specs=in_specs,
        out_specs=out_specs,
        out_shape=out_shape,
        scratch_shapes=[pltpu.VMEM((seq, width), F32), pltpu.VMEM((seq, width), F32),
                        pltpu.VMEM(st_shape, F32)],
        compiler_params=_params("parallel"),
        name="hgrn_lat" if has_state else "hgrn_ctx",
    )(*args)


def _merge_kernel(oa_ref, ob_ref, oc_ref, od_ref, gt_ref, wb_ref, wo_ref, x_ref, g1_ref, out_ref):
    acc = None
    for n, o_ref in enumerate((oa_ref, ob_ref, oc_ref, od_ref)):
        bo = _dot(o_ref[...].astype(BF16), wb_ref[n])
        term = jax.nn.sigmoid(gt_ref[:, n * D_MODEL:(n + 1) * D_MODEL]) * bo
        acc = term if acc is None else acc + term
    out_ref[...] = x_ref[...] + g1_ref[...] * _dot(acc.astype(BF16), wo_ref[...])


def _merge(branches, gates, w_branch, w_out, x, mod, latent):
    tm = 256
    row = _mod_row(latent, tm)
    tile = lambda w: pl.BlockSpec((tm, w), lambda i: (i, 0))
    return pl.pallas_call(
        _merge_kernel,
        grid=(M_ROWS // tm,),
        in_specs=[tile(BRANCH_W)] * N_BRANCH + [
            tile(GATE_W),
            pl.BlockSpec((N_BRANCH, BRANCH_W, D_MODEL), lambda i: (0, 0, 0)),
            pl.BlockSpec((D_MODEL, D_MODEL), lambda i: (0, 0)),
            tile(D_MODEL),
            pl.BlockSpec((None, None, 1, D_MODEL), lambda i: (row(i), 2, 0, 0)),
        ],
        out_specs=tile(D_MODEL),
        out_shape=jax.ShapeDtypeStruct((M_ROWS, D_MODEL), F32),
        compiler_params=_params("parallel"),
        name="merge_lat" if latent else "merge_ctx",
    )(*branches, gates, w_branch, w_out, x, mod)


FFN_CHUNK = 256
FFN_STEPS = FFN_HIDDEN // FFN_CHUNK


def _ffn_kernel(x_ref, sc_ref, sh_ref, g2_ref, wa_ref, wg_ref, wo_ref, out_ref, h_ref, acc_ref):
    j = pl.program_id(1)

    @pl.when(j == 0)
    def _():
        h = _rms(x_ref[...]) * (1.0 + sc_ref[...]) + sh_ref[...]
        h_ref[...] = h.astype(BF16)
        acc_ref[...] = jnp.zeros(acc_ref.shape, F32)

    h = h_ref[...]
    act = _silu(_dot(h, wg_ref[...])) * _dot(h, wa_ref[...])
    acc_ref[...] += _dot(act.astype(BF16), wo_ref[...])

    @pl.when(j == FFN_STEPS - 1)
    def _():
        out_ref[...] = x_ref[...] + g2_ref[...] * acc_ref[...]


def _ffn(x, mod, w_in, w_out, latent):
    tm = 1024
    row = _mod_row(latent, tm)
    modspec = lambda which: pl.BlockSpec((None, None, 1, D_MODEL), lambda i, j: (row(i), which, 0, 0))
    return pl.pallas_call(
        _ffn_kernel,
        grid=(M_ROWS // tm, FFN_STEPS),
        in_specs=[
            pl.BlockSpec((tm, D_MODEL), lambda i, j: (i, 0)),
            modspec(4), modspec(3), modspec(5),
            pl.BlockSpec((D_MODEL, FFN_CHUNK), lambda i, j: (0, j)),
            pl.BlockSpec((D_MODEL, FFN_CHUNK), lambda i, j: (0, FFN_STEPS + j)),
            pl.BlockSpec((FFN_CHUNK, D_MODEL), lambda i, j: (j, 0)),
        ],
        out_specs=pl.BlockSpec((tm, D_MODEL), lambda i, j: (i, 0)),
        out_shape=jax.ShapeDtypeStruct((M_ROWS, D_MODEL), F32),
        scratch_shapes=[pltpu.VMEM((tm, D_MODEL), BF16), pltpu.VMEM((tm, D_MODEL), F32)],
        compiler_params=_params("parallel", "arbitrary"),
        name="ffn_lat" if latent else "ffn_ctx",
    )(x, mod, mod, mod, w_in, w_in, w_out)


def _final_kernel(x_ref, g_ref, o_ref):
    o_ref[...] = _rms(x_ref[...]) * g_ref[...]


def _final_norm(x, gain):
    tm = 512
    return pl.pallas_call(
        _final_kernel,
        grid=(M_ROWS // tm,),
        in_specs=[pl.BlockSpec((tm, D_MODEL), lambda i: (i, 0)), pl.BlockSpec((1, D_MODEL), lambda i: (0, 0))],
        out_specs=pl.BlockSpec((tm, D_MODEL), lambda i: (i, 0)),
        out_shape=jax.ShapeDtypeStruct((M_ROWS, D_MODEL), F32),
        compiler_params=_params("parallel"),
        name="final_norm",
    )(x, gain.reshape(1, D_MODEL))


def _rope_tables():
    t = jnp.arange(DEC_SEQ)
    row = (t // GRID_W).astype(F32)[:, None]
    col = (t % GRID_W).astype(F32)[:, None]

    def angles(rot_dim):
        n_freq = rot_dim // 4
        inv_freq = ROPE_THETA ** (-jnp.arange(n_freq, dtype=F32) / n_freq)
        return jnp.concatenate([row * inv_freq, col * inv_freq], axis=-1)

    def expand(ang):
        cos = jnp.repeat(jnp.cos(ang), 2, axis=-1)
        sin = jnp.repeat(jnp.sin(ang), 2, axis=-1)
        even = (jnp.arange(cos.shape[-1]) % 2 == 0)[None, :]
        return cos, jnp.where(even, -sin, 0.0), jnp.where(even, 0.0, sin)

    gq = [jnp.concatenate([a, a], axis=-1) for a in expand(angles(GQ_HD))]
    pad_l = LANE - ML_NOPE - ML_ROPE
    ml = []
    for idx, a in enumerate(expand(angles(ML_ROPE))):
        fill = 1.0 if idx == 0 else 0.0
        ml.append(jnp.concatenate([jnp.full((DEC_SEQ, ML_NOPE), fill, F32), a,
                                   jnp.full((DEC_SEQ, pad_l), fill, F32)], axis=-1))
    return {"gq": jnp.stack(gq), "ml": jnp.stack(ml)}


def _layer_weights(l, w_in, gq_q_gain, gq_k_gain, ml_q_a_gain, ml_kv_a_gain, ml_w_q_b, ml_w_kv_b,
                   w_branch, w_out, w_ffn_in, w_ffn_out, hg_gain, avg):
    w = w_in[l]
    zeros = lambda n: jnp.zeros((D_MODEL, n), F32)
    w_mix = jnp.concatenate([
        w[:, 0:3072], w[:, 3328:4864], w[:, 4864:5120], w[:, 3072:3328], w[:, 5120:5248],
        zeros(ML_NOPE), w[:, 5248:5280], zeros(LANE - ML_NOPE - ML_ROPE)], axis=1).astype(BF16)
    assert w_mix.shape[1] == MIX_W
    w_gate = w[:, 5280:].astype(BF16)
    qb = ml_w_q_b[l].reshape(ML_Q_RANK, ML_HEADS, ML_NOPE + ML_ROPE)
    qb = jnp.pad(qb, ((0, 0), (0, 0), (0, LANE - ML_NOPE - ML_ROPE))).reshape(ML_Q_RANK, ML_HEADS * LANE)
    kvb = ml_w_kv_b[l].reshape(ML_KV_RANK, ML_HEADS, ML_NOPE + ML_V)
    wk = jnp.pad(kvb[:, :, :ML_NOPE], ((0, 0), (0, 0), (0, LANE - ML_NOPE))).reshape(ML_KV_RANK, ML_HEADS * LANE)
    wv = kvb[:, :, ML_NOPE:].reshape(ML_KV_RANK, ML_HEADS * ML_V)
    return {
        "w_mix": w_mix, "w_gate": w_gate,
        "gq_q_gain": jnp.tile(gq_q_gain[l], GQ_HEADS).reshape(1, -1),
        "gq_k_gain": jnp.tile(gq_k_gain[l], GQ_KV_HEADS).reshape(1, -1),
        "ml_q_gain": ml_q_a_gain[l].reshape(1, -1), "ml_kv_gain": ml_kv_a_gain[l].reshape(1, -1),
        "w_qb": qb.astype(BF16), "w_k": wk.astype(BF16), "w_v": wv.astype(BF16),
        "w_branch": w_branch[l].astype(BF16), "w_out": w_out[l].astype(BF16),
        "w_ffn_in": w_ffn_in[l].astype(BF16), "w_ffn_out": w_ffn_out[l].astype(BF16),
        "hg_gain": jnp.tile(hg_gain[l], HG_HEADS).reshape(1, -1), "avg": avg,
    }


def _seg(arr, width, col, rows, stride, off):
    return (arr, width, col, rows, stride, off)


def kernel(x_prompt, x_sample, state_hgrn, cache_gqa_k, cache_gqa_v, cache_na_k, cache_na_v, cache_mla_ckv, cache_mla_krope, c, c_ctx, w_ada, b_ada, w_in, hg_lb_logits, hg_gain, gq_q_gain, gq_k_gain, na_rpb, ml_q_a_gain, ml_kv_a_gain, ml_w_q_b, ml_w_kv_b, w_branch, w_out, w_ffn_in, w_ffn_out, final_gain):
    cond8 = jnp.concatenate([c_ctx[None, :], c, jnp.zeros((8 - 1 - DEC_BATCH, D_MODEL), F32)], axis=0)
    mods = _ada(cond8, w_ada, b_ada)

    lb = jnp.cumsum(jax.nn.softmax(hg_lb_logits.astype(F32), axis=0), axis=0)
    lb = lb - lb[:1]
    avg = jnp.asarray(np.kron(np.eye(512 // GQ_HD), np.full((GQ_HD, GQ_HD), 1.0 / GQ_HD)), BF16)
    tabs = _rope_tables()

    gqk_c = cache_gqa_k.reshape(DEC_BATCH * DEPTH * PAST_LEN, GQ_KV_HEADS * GQ_HD)
    gqv_c = cache_gqa_v.reshape(DEC_BATCH * DEPTH * PAST_LEN, GQ_KV_HEADS * GQ_HD)
    nak_c = cache_na_k.reshape(DEC_BATCH * DEPTH * PAST_LEN, NA_HEADS * NA_HD)
    nav_c = cache_na_v.reshape(DEC_BATCH * DEPTH * PAST_LEN, NA_HEADS * NA_HD)
    mckv_c = cache_mla_ckv.reshape(DEC_BATCH * DEPTH * PAST_LEN, ML_KV_RANK)
    mkr_c = jnp.pad(cache_mla_krope.reshape(DEC_BATCH * DEPTH * PAST_LEN, ML_ROPE),
                    ((0, 0), (ML_NOPE, LANE - ML_NOPE - ML_ROPE)))

    xp = x_prompt.reshape(M_ROWS, D_MODEL)
    xs = x_sample.reshape(M_ROWS, D_MODEL)
    new = []
    for l in range(DEPTH):
        lw = _layer_weights(l, w_in, gq_q_gain, gq_k_gain, ml_q_a_gain, ml_kv_a_gain, ml_w_q_b,
                            ml_w_kv_b, w_branch, w_out, w_ffn_in, w_ffn_out, hg_gain, avg)
        mod = mods[l]

        ymix = _inproj(xp, mod, lw["w_mix"], False, 1792, "inproj_mix_ctx")
        gates = _inproj(xp, mod, lw["w_gate"], False, 2048, "inproj_gate_ctx")
        qb, kb, qd, ckv, kd, vd = _prep(ymix, lw, tabs, False)
        out_a, st = _hgrn(ymix, lb[l], lw["hg_gain"], BATCH, SEQ, None)
        out_b = _attention((qb, 512, 0, 0),
                           [(_seg(kb, LANE, 0, SEQ, 1, 0), _seg(ymix, LANE, COL_GV, SEQ, 1, 0))],
                           _plan_gqa(), GQ_SCALE, BATCH, SEQ, "gqa_ctx")
        out_c = _attention((ymix, 512, COL_NQ, 0),
                           [(_seg(ymix, 512, COL_NK, SEQ, 1, 0), _seg(ymix, 512, COL_NV, SEQ, 1, 0))],
                           _plan_na(), NA_SCALE, BATCH, SEQ, "na_ctx")
        out_d = _attention((qd, ML_HEADS * LANE, 0, 0),
                           [(_seg(kd, ML_HEADS * LANE, 0, SEQ, 1, 0), _seg(vd, 512, 0, SEQ, 1, 0))],
                           _plan_mla(), ML_SCALE, BATCH, SEQ, "mla_ctx")
        xp = _merge((out_a, out_b, out_c, out_d), gates, lw["w_branch"], lw["w_out"], xp, mod, False)
        xp = _ffn(xp, mod, lw["w_ffn_in"], lw["w_ffn_out"], False)
        new.append((
            jnp.swapaxes(st, -1, -2),
            kb.reshape(BATCH, SEQ, GQ_KV_HEADS, GQ_HD),
            ymix[:, COL_GV * LANE:(COL_GV + 1) * LANE].reshape(BATCH, SEQ, GQ_KV_HEADS, GQ_HD),
            ymix[:, COL_NK * 512:(COL_NK + 1) * 512].reshape(BATCH, SEQ, NA_HEADS, NA_HD),
            ymix[:, COL_NV * 512:(COL_NV + 1) * 512].reshape(BATCH, SEQ, NA_HEADS, NA_HD),
            ckv.reshape(BATCH, SEQ, ML_KV_RANK),
            ymix[:, COL_MKR * LANE + ML_NOPE:COL_MKR * LANE + ML_NOPE + ML_ROPE].reshape(BATCH, SEQ, ML_ROPE),
        ))

        ymix = _inproj(xs, mod, lw["w_mix"], True, 1792, "inproj_mix_lat")
        gates = _inproj(xs, mod, lw["w_gate"], True, 2048, "inproj_gate_lat")
        qb, kb, qd, ckv, kd, vd = _prep(ymix, lw, tabs, True)
        kd_c, vd_c = _mla_cache(mckv_c, mkr_c, lw, l)
        s0_t = jnp.swapaxes(state_hgrn[:, l], -1, -2)
        out_a, = _hgrn(ymix, lb[l], lw["hg_gain"], DEC_BATCH, DEC_SEQ, s0_t)
        out_b = _attention((qb, 512, 0, 0),
                           [(_seg(gqk_c, LANE, 0, PAST_LEN, DEPTH, l), _seg(gqv_c, LANE, 0, PAST_LEN, DEPTH, l)),
                            (_seg(kb, LANE, 0, DEC_SEQ, 1, 0), _seg(ymix, LANE, COL_GV, DEC_SEQ, 1, 0))],
                           _plan_gqa(), GQ_SCALE, DEC_BATCH, DEC_SEQ, "gqa_lat")
        out_c = _na_latent(ymix, nak_c, nav_c, _na_bias_tables(na_rpb[l]), l)
        out_d = _attention((qd, ML_HEADS * LANE, 0, 0),
                           [(_seg(kd_c, ML_HEADS * LANE, 0, PAST_LEN, 1, 0), _seg(vd_c, 512, 0, PAST_LEN, 1, 0)),
                            (_seg(kd, ML_HEADS * LANE, 0, DEC_SEQ, 1, 0), _seg(vd, 512, 0, DEC_SEQ, 1, 0))],
                           _plan_mla(), ML_SCALE, DEC_BATCH, DEC_SEQ, "mla_lat")
        xs = _merge((out_a, out_b, out_c, out_d), gates, lw["w_branch"], lw["w_out"], xs, mod, True)
        xs = _ffn(xs, mod, lw["w_ffn_in"], lw["w_ffn_out"], True)

    y_prompt = _final_norm(xp, final_gain).reshape(BATCH, SEQ, D_MODEL)
    y_sample = _final_norm(xs, final_gain).reshape(DEC_BATCH, DEC_SEQ, D_MODEL)
    stacked = tuple(jnp.stack([n[i] for n in new], axis=1) for i in range(7))
    return (y_prompt, y_sample) + stacked
```

```python
import functools

import numpy as np
import jax
import jax.numpy as jnp
from jax import lax
from jax.experimental import pallas as pl
from jax.experimental.pallas import tpu as pltpu

F32 = jnp.float32
BF16 = jnp.bfloat16

D_MODEL = 1024
BATCH = 16
SEQ = 256
DEPTH = 2
DEC_BATCH = 4
DEC_SEQ = 1024
PAST_LEN = 512
GRID_W = 64
EPS = 1e-6
ROPE_THETA = 10000.0
N_BRANCH = 4
BRANCH_W = 512
HG_HEADS = 4
HG_DK = 128
HG_DV = 128
GQ_HEADS = 8
GQ_KV_HEADS = 2
GQ_HD = 64
NA_HEADS = 8
NA_HD = 64
NA_WIN_R = 8
NA_WIN_C = 16
ML_HEADS = 8
ML_NOPE = 64
ML_ROPE = 32
ML_V = 64
ML_Q_RANK = 256
ML_KV_RANK = 128
FFN_HIDDEN = 2816
GQ_SCALE = GQ_HD ** -0.5
NA_SCALE = NA_HD ** -0.5
ML_SCALE = (ML_NOPE + ML_ROPE) ** -0.5

M_ROWS = BATCH * SEQ
assert M_ROWS == DEC_BATCH * DEC_SEQ

LANE = 128
HALF = 64
MIX_W = 5376
GATE_W = N_BRANCH * D_MODEL
HG_CHUNK = 16
NEG = -1e30
VMEM_LIMIT = 56 * 1024 * 1024

COL_HQ, COL_HFF, COL_HFB, COL_HI, COL_HG, COL_GQ, COL_NQ, COL_NK, COL_NV = range(9)
COL_MQA = 18
COL_GK, COL_GV, COL_MKVA, COL_MKR = 38, 39, 40, 41


def _dot(a, b):
    return jnp.dot(a, b, preferred_element_type=F32)


def _dot_nt(a, b):
    return lax.dot_general(a, b, (((1,), (1,)), ((), ())), preferred_element_type=F32)


def _dot_tn(a, b):
    return lax.dot_general(a, b, (((0,), (0,)), ((), ())), preferred_element_type=F32)


def _split3(x):
    x1 = x.astype(BF16)
    r1 = x - x1.astype(F32)
    x2 = r1.astype(BF16)
    x3 = (r1 - x2.astype(F32)).astype(BF16)
    return x1, x2, x3


def _dot_exact_lhs(a_bf16, x):
    x1, x2, x3 = _split3(x)
    return (_dot(a_bf16, x3) + _dot(a_bf16, x2)) + _dot(a_bf16, x1)


def _dot_exact_rhs(x, b_bf16):
    x1, x2, x3 = _split3(x)
    return (_dot(x3, b_bf16) + _dot(x2, b_bf16)) + _dot(x1, b_bf16)


def _dot_hi(a, b):
    a1 = a.astype(BF16)
    a2 = (a - a1.astype(F32)).astype(BF16)
    b1 = b.astype(BF16)
    b2 = (b - b1.astype(F32)).astype(BF16)
    return (_dot(a1, b2) + _dot(a2, b1)) + _dot(a1, b1)


def _rms(x):
    return x * lax.rsqrt(jnp.mean(x * x, axis=-1, keepdims=True) + EPS)


def _silu(x):
    return x * jax.nn.sigmoid(x)


def _params(*sem):
    return pltpu.CompilerParams(dimension_semantics=sem, vmem_limit_bytes=VMEM_LIMIT)


def _mod_row(latent, tm):
    if latent:
        return lambda i: 1 + (i * tm) // DEC_SEQ
    return lambda i: 0


def _ada_kernel(c_ref, w_ref, b_ref, o_ref):
    c = c_ref[...]
    o_ref[...] = _dot_hi(_silu(c), w_ref[...]) + b_ref[...]


def _ada(cond8, w_ada, b_ada):
    tn = 1536
    out = pl.pallas_call(
        _ada_kernel,
        grid=(DEPTH, 6 * D_MODEL // tn),
        in_specs=[
            pl.BlockSpec((8, D_MODEL), lambda l, j: (0, 0)),
            pl.BlockSpec((None, D_MODEL, tn), lambda l, j: (l, 0, j)),
            pl.BlockSpec((None, 1, tn), lambda l, j: (l, 0, j)),
        ],
        out_specs=pl.BlockSpec((None, 8, tn), lambda l, j: (l, 0, j)),
        out_shape=jax.ShapeDtypeStruct((DEPTH, 8, 6 * D_MODEL), F32),
        compiler_params=_params("parallel", "parallel"),
        name="ada",
    )(cond8, w_ada, b_ada.reshape(DEPTH, 1, 6 * D_MODEL))
    return out.reshape(DEPTH, 8, 6, 1, D_MODEL)


def _inproj_kernel(x_ref, sc_ref, sh_ref, w_ref, o_ref, h_ref):
    @pl.when(pl.program_id(1) == 0)
    def _():
        h = _rms(x_ref[...]) * (1.0 + sc_ref[...]) + sh_ref[...]
        h_ref[...] = h.astype(BF16)

    o_ref[...] = _dot(h_ref[...], w_ref[...])


def _inproj(x, mod, w, latent, tn, name):
    tm = 1024
    n = w.shape[1]
    row = _mod_row(latent, tm)
    return pl.pallas_call(
        _inproj_kernel,
        grid=(M_ROWS // tm, n // tn),
        in_specs=[
            pl.BlockSpec((tm, D_MODEL), lambda i, j: (i, 0)),
            pl.BlockSpec((None, None, 1, D_MODEL), lambda i, j: (row(i), 1, 0, 0)),
            pl.BlockSpec((None, None, 1, D_MODEL), lambda i, j: (row(i), 0, 0, 0)),
            pl.BlockSpec((D_MODEL, tn), lambda i, j: (0, j)),
        ],
        out_specs=pl.BlockSpec((tm, tn), lambda i, j: (i, j)),
        out_shape=jax.ShapeDtypeStruct((M_ROWS, n), F32),
        scratch_shapes=[pltpu.VMEM((tm, D_MODEL), BF16)],
        compiler_params=_params("parallel", "arbitrary"),
        name=name,
    )(x, mod, mod, w)


def _rope(x, tab_ref, reps):
    w = x.shape[1]
    c = jnp.concatenate([tab_ref[0]] * reps, axis=1) if reps > 1 else tab_ref[0]
    se = jnp.concatenate([tab_ref[1]] * reps, axis=1) if reps > 1 else tab_ref[1]
    so = jnp.concatenate([tab_ref[2]] * reps, axis=1) if reps > 1 else tab_ref[2]
    return x * c + pltpu.roll(x, w - 1, 1) * se + pltpu.roll(x, 1, 1) * so


def _head_rms(x, gain, avg_bf16):
    ms = _dot_exact_rhs(x * x, avg_bf16)
    return x * lax.rsqrt(ms + EPS) * gain


def _prep_kernel(*refs, rope):
    (gq_ref, gk_ref, mqa_ref, mkva_ref, mkr_ref, gqg_ref, gkg_ref, mqg_ref, mkvg_ref,
     wqb_ref, wk_ref, wv_ref, avg_ref) = refs[:13]
    if rope:
        gtab_ref, mtab_ref = refs[13:15]
        outs = refs[15:]
    else:
        outs = refs[13:]
    qb_ref, kb_ref, qd_ref, ckv_ref, kd_ref, vd_ref = outs

    q = _head_rms(gq_ref[...], gqg_ref[...], avg_ref[...])
    k = _head_rms(gk_ref[...], gkg_ref[...], avg_ref[0:LANE, 0:LANE])
    qd = _dot((_rms(mqa_ref[...]) * mqg_ref[...]).astype(BF16), wqb_ref[...])
    ckv = _rms(mkva_ref[...]) * mkvg_ref[...]
    kr = mkr_ref[...]
    if rope:
        q = _rope(q, gtab_ref, GQ_HEADS * GQ_HD // LANE)
        k = _rope(k, gtab_ref, 1)
        qd = _rope(qd, mtab_ref, ML_HEADS)
        kr = _rope(kr, mtab_ref, 1)
    qb_ref[...] = q
    kb_ref[...] = k
    qd_ref[...] = qd
    ckv_ref[...] = ckv
    cb = ckv.astype(BF16)
    kd_ref[...] = _dot(cb, wk_ref[...]) + jnp.concatenate([kr] * ML_HEADS, axis=1)
    vd_ref[...] = _dot(cb, wv_ref[...])


def _prep(ymix, lw, tabs, latent):
    tm = 512
    const = lambda i: (0, 0)
    in_specs = [
        pl.BlockSpec((tm, 512), lambda i: (i, COL_GQ)),
        pl.BlockSpec((tm, LANE), lambda i: (i, COL_GK)),
        pl.BlockSpec((tm, 256), lambda i: (i, COL_MQA)),
        pl.BlockSpec((tm, LANE), lambda i: (i, COL_MKVA)),
        pl.BlockSpec((tm, LANE), lambda i: (i, COL_MKR)),
        pl.BlockSpec((1, 512), const),
        pl.BlockSpec((1, LANE), const),
        pl.BlockSpec((1, 256), const),
        pl.BlockSpec((1, LANE), const),
        pl.BlockSpec((ML_Q_RANK, ML_HEADS * LANE), const),
        pl.BlockSpec((ML_KV_RANK, ML_HEADS * LANE), const),
        pl.BlockSpec((ML_KV_RANK, ML_HEADS * ML_V), const),
        pl.BlockSpec((512, 512), const),
    ]
    args = [ymix, ymix, ymix, ymix, ymix, lw["gq_q_gain"], lw["gq_k_gain"], lw["ml_q_gain"],
            lw["ml_kv_gain"], lw["w_qb"], lw["w_k"], lw["w_v"], lw["avg"]]
    if latent:
        per = DEC_SEQ // tm
        in_specs += [pl.BlockSpec((3, tm, LANE), lambda i: (0, i % per, 0))] * 2
        args += [tabs["gq"], tabs["ml"]]
    widths = (512, LANE, ML_HEADS * LANE, LANE, ML_HEADS * LANE, ML_HEADS * ML_V)
    return pl.pallas_call(
        functools.partial(_prep_kernel, rope=latent),
        grid=(M_ROWS // tm,),
        in_specs=in_specs,
        out_specs=[pl.BlockSpec((tm, w), lambda i: (i, 0)) for w in widths],
        out_shape=[jax.ShapeDtypeStruct((M_ROWS, w), F32) for w in widths],
        compiler_params=_params("parallel"),
        name="prep_lat" if latent else "prep_ctx",
    )(*args)


def _mla_cache_kernel(ckv_ref, kr_ref, wk_ref, wv_ref, kd_ref, vd_ref):
    cb = ckv_ref[...].astype(BF16)
    kd_ref[...] = _dot(cb, wk_ref[...]) + jnp.concatenate([kr_ref[...]] * ML_HEADS, axis=1)
    vd_ref[...] = _dot(cb, wv_ref[...])


def _mla_cache(ckv, kr_blk, lw, layer):
    rows = DEC_BATCH * PAST_LEN
    tm = PAST_LEN
    const = lambda i: (0, 0)
    return pl.pallas_call(
        _mla_cache_kernel,
        grid=(DEC_BATCH,),
        in_specs=[
            pl.BlockSpec((tm, LANE), lambda i: (i * DEPTH + layer, 0)),
            pl.BlockSpec((tm, LANE), lambda i: (i * DEPTH + layer, 0)),
            pl.BlockSpec((ML_KV_RANK, ML_HEADS * LANE), const),
            pl.BlockSpec((ML_KV_RANK, ML_HEADS * ML_V), const),
        ],
        out_specs=[pl.BlockSpec((tm, ML_HEADS * LANE), lambda i: (i, 0)),
                   pl.BlockSpec((tm, ML_HEADS * ML_V), lambda i: (i, 0))],
        out_shape=[jax.ShapeDtypeStruct((rows, ML_HEADS * LANE), F32),
                   jax.ShapeDtypeStruct((rows, ML_HEADS * ML_V), F32)],
        compiler_params=_params("parallel"),
        name="mla_cache",
    )(ckv, kr_blk, lw["w_k"], lw["w_v"])


def _softmax_pv(scores, values):
    m = scores[0].max(axis=-1, keepdims=True)
    for s in scores[1:]:
        m = jnp.maximum(m, s.max(axis=-1, keepdims=True))
    den = None
    out = None
    for s, v in zip(scores, values):
        p = jnp.exp(s - m)
        d = p.sum(axis=-1, keepdims=True)
        o = _dot(p.astype(BF16), v)
        den = d if den is None else den + d
        out = o if out is None else out + o
    return out / den


def _lane_halves():
    lane = lax.broadcasted_iota(jnp.int32, (1, LANE), 1)
    return lane < HALF, lane >= HALF


def _attn_kernel(*refs, plan, nseg, scale, fold_scale):
    q_ref = refs[0]
    k_refs = [refs[1 + 2 * i] for i in range(nseg)]
    v_refs = [refs[2 + 2 * i] for i in range(nseg)]
    o_ref = refs[1 + 2 * nseg]
    tq = q_ref.shape[0]
    lo, hi = _lane_halves()
    cache = {}

    def block(kind, seg, blk, swap):
        key = (kind, seg, blk, swap)
        if key not in cache:
            ref = (k_refs if kind == "k" else v_refs)[seg]
            x = ref[:, blk * LANE:(blk + 1) * LANE]
            if swap:
                x = pltpu.roll(x, HALF, 1)
            cache[key] = x.astype(BF16)
        return cache[key]

    parts = {}
    for members, kblk, swap, vblk in plan:
        qs = []
        for qblk, qhalf, _, _ in members:
            q = q_ref[:, qblk * LANE:(qblk + 1) * LANE]
            if fold_scale:
                q = q * scale
            if qhalf is not None:
                q = jnp.where(lo if qhalf == 0 else hi, q, 0.0)
            qs.append(q)
        q = (jnp.concatenate(qs, axis=0) if len(qs) > 1 else qs[0]).astype(BF16)
        scores = [_dot_nt(q, block("k", s, kblk, swap)) for s in range(nseg)]
        if not fold_scale:
            scores = [s * scale for s in scores]
        out = _softmax_pv(scores, [block("v", s, vblk, swap) for s in range(nseg)])
        for n, (_, _, oblk, ohalf) in enumerate(members):
            parts[(oblk, ohalf)] = out[n * tq:(n + 1) * tq]
    for oblk in sorted({key[0] for key in parts}):
        o_ref[:, oblk * LANE:(oblk + 1) * LANE] = jnp.where(lo, parts[(oblk, 0)], parts[(oblk, 1)])


def _plan_gqa():
    per_kv = GQ_HEADS // GQ_KV_HEADS
    plan = []
    for g in range(GQ_KV_HEADS):
        for half in range(2):
            heads = [h for h in range(g * per_kv, (g + 1) * per_kv) if h % 2 == half]
            plan.append((tuple((h // 2, half, h // 2, half) for h in heads), 0, half != g, 0))
    return tuple(plan)


def _plan_na():
    return tuple((((j, 0, j, 0), (j, 1, j, 1)), j, False, j) for j in range(NA_HEADS // 2))


def _plan_mla():
    return tuple((((h, None, h // 2, h % 2),), h, False, h // 2) for h in range(ML_HEADS))


def _is_pow2(x):
    return float(np.log2(x)).is_integer()


def _attention(q, segs, plan, scale, batch, tq_total, name):
    tq = min(tq_total, 256)
    per = tq_total // tq
    q_arr, q_w, q_col, q_off = q
    in_specs = [pl.BlockSpec((tq, q_w), lambda b, i: (q_off + b * per + i, q_col))]
    args = [q_arr]
    for k, v in segs:
        for arr, w, col, rows, stride, off in (k, v):
            in_specs.append(pl.BlockSpec(
                (rows, w), functools.partial(lambda b, i, col, stride, off: (off + b * stride, col),
                                             col=col, stride=stride, off=off)))
            args.append(arr)
    out_w = 512
    return pl.pallas_call(
        functools.partial(_attn_kernel, plan=plan, nseg=len(segs), scale=scale, fold_scale=_is_pow2(scale)),
        grid=(batch, per),
        in_specs=in_specs,
        out_specs=pl.BlockSpec((tq, out_w), lambda b, i: (b * per + i, 0)),
        out_shape=jax.ShapeDtypeStruct((M_ROWS, out_w), F32),
        compiler_params=_params("parallel", "parallel"),
        name=name,
    )(*args)


NA_DR = 2 * NA_WIN_R - 1
NA_DC = 2 * NA_WIN_C - 1
NA_ROWS = DEC_SEQ // GRID_W
NA_QROWS = 4
NA_WROWS = 12
NA_T_RIGHT = NA_DR - 1
NA_T_LEFT = NA_DR
NA_T_NONE = NA_DR + 1
NA_T_SIZE = NA_DR + 2
assert _is_pow2(NA_SCALE)


def _na_kernel(q_ref, kl_ref, vl_ref, kc_ref, vc_ref, tz_ref, o_ref):
    g = pl.program_id(1)
    w0 = (g // 2) * (NA_ROWS - NA_WROWS)
    k0 = pl.multiple_of(w0 * GRID_W, (NA_ROWS - NA_WROWS) * GRID_W)
    win = NA_WROWS * GRID_W
    nq = NA_QROWS * GRID_W
    entry = []
    for a in range(NA_QROWS):
        rq = g * NA_QROWS + a
        start = jnp.clip(rq - NA_WIN_R // 2, 0, NA_ROWS - NA_WIN_R)
        per_pair = []
        for i in range(NA_WROWS // 2):
            kr = w0 + 2 * i
            in_l = jnp.logical_and(kr >= start, kr < start + NA_WIN_R)
            in_r = jnp.logical_and(kr + 1 >= start, kr + 1 < start + NA_WIN_R)
            both = jnp.logical_and(in_l, in_r)
            d_l = kr - rq + NA_WIN_R - 1
            per_pair.append(jnp.where(both, d_l, jnp.where(in_r, NA_T_RIGHT, jnp.where(in_l, NA_T_LEFT, NA_T_NONE))))
        entry.append(per_pair)
    lo, hi = _lane_halves()
    for j in range(NA_HEADS // 2):
        sl = slice(j * LANE, (j + 1) * LANE)
        k_loc = kl_ref[pl.ds(k0, win), sl].astype(BF16)
        v_loc = vl_ref[pl.ds(k0, win), sl].astype(BF16)
        k_ctx = kc_ref[:, sl].astype(BF16)
        v_ctx = vc_ref[:, sl].astype(BF16)
        q_pair = q_ref[:, sl] * NA_SCALE
        q = jnp.concatenate([jnp.where(lo, q_pair, 0.0), jnp.where(hi, q_pair, 0.0)], axis=0).astype(BF16)
        bias = jnp.concatenate(
            [jnp.concatenate([tz_ref[2 * j + p, e] for e in entry[a]], axis=1)
             for p in range(2) for a in range(NA_QROWS)], axis=0)
        s_loc = _dot_nt(q, k_loc) + bias
        s_ctx = _dot_nt(q, k_ctx)
        out = _softmax_pv([s_loc, s_ctx], [v_loc, v_ctx])
        o_ref[:, sl] = jnp.where(lo, out[:nq], out[nq:])


def _na_bias_kernel(rpb_ref, sel_ref, neg_ref, o_ref):
    o_ref[...] = _dot_exact_rhs(rpb_ref[...], sel_ref[...]) + neg_ref[...]


def _na_bias_tables(rpb):
    col = np.arange(GRID_W)
    col_start = np.clip(col - NA_WIN_C // 2, 0, GRID_W - NA_WIN_C)
    col_ok = (col[None, :] >= col_start[:, None]) & (col[None, :] < col_start[:, None] + NA_WIN_C)
    dc = col[None, :] - col[:, None] + NA_WIN_C - 1
    kpad = 32
    sel = (dc[None, :, :] == np.arange(kpad)[:, None, None]) & col_ok[None]
    sel = jnp.asarray(sel.reshape(kpad, GRID_W * GRID_W), BF16)
    neg = jnp.asarray(np.where(col_ok, 0.0, NEG).reshape(1, GRID_W * GRID_W), F32)
    rpb2 = jnp.pad(rpb.reshape(NA_HEADS * NA_DR, NA_DC), ((0, 0), (0, kpad - NA_DC)))
    n_rows = NA_HEADS * NA_DR
    full = lambda shape: pl.BlockSpec(shape, lambda i: (0, 0))
    t = pl.pallas_call(
        _na_bias_kernel,
        grid=(1,),
        in_specs=[full((n_rows, kpad)), full((kpad, GRID_W * GRID_W)), full((1, GRID_W * GRID_W))],
        out_specs=full((n_rows, GRID_W * GRID_W)),
        out_shape=jax.ShapeDtypeStruct((n_rows, GRID_W * GRID_W), F32),
        compiler_params=_params("arbitrary"),
        name="na_bias",
    )(rpb2, sel, neg)
    t = t.reshape(NA_HEADS, NA_DR, GRID_W, GRID_W)
    masked = jnp.full((NA_HEADS, 1, GRID_W, GRID_W), NEG, F32)
    first, last = NA_WIN_R // 2 - 1, NA_WIN_R // 2 + NA_WIN_R - 2
    return jnp.concatenate([
        jnp.concatenate([t[:, :-1], t[:, 1:]], axis=-1),
        jnp.concatenate([masked, t[:, first:first + 1]], axis=-1),
        jnp.concatenate([t[:, last:last + 1], masked], axis=-1),
        jnp.concatenate([masked, masked], axis=-1)], axis=1)


def _na_latent(ymix, cache_k, cache_v, tz, layer):
    groups = NA_ROWS // NA_QROWS
    nq = NA_QROWS * GRID_W
    return pl.pallas_call(
        _na_kernel,
        grid=(DEC_BATCH, groups),
        in_specs=[
            pl.BlockSpec((nq, 512), lambda b, g: (b * groups + g, COL_NQ)),
            pl.BlockSpec((DEC_SEQ, 512), lambda b, g: (b, COL_NK)),
            pl.BlockSpec((DEC_SEQ, 512), lambda b, g: (b, COL_NV)),
            pl.BlockSpec((PAST_LEN, 512), lambda b, g: (b * DEPTH + layer, 0)),
            pl.BlockSpec((PAST_LEN, 512), lambda b, g: (b * DEPTH + layer, 0)),
            pl.BlockSpec((NA_HEADS, NA_T_SIZE, GRID_W, LANE), lambda b, g: (0, 0, 0, 0)),
        ],
        out_specs=pl.BlockSpec((nq, 512), lambda b, g: (b * groups + g, 0)),
        out_shape=jax.ShapeDtypeStruct((M_ROWS, 512), F32),
        compiler_params=_params("parallel", "parallel"),
        name="na_lat",
    )(ymix, ymix, ymix, cache_k, cache_v, tz)


def _hgrn_kernel(*refs, seq, has_state):
    if has_state:
        (q_ref, ff_ref, fb_ref, v_ref, g_ref, lb_ref, gain_ref, s0_ref,
         o_ref, of_ref, ob_ref, st_ref) = refs
    else:
        (q_ref, ff_ref, fb_ref, v_ref, g_ref, lb_ref, gain_ref,
         o_ref, sout_ref, of_ref, ob_ref, st_ref) = refs
    c = HG_CHUNK
    hc = c // 2
    n_chunks = seq // c
    width = HG_HEADS * HG_DK
    if has_state:
        st_ref[...] = s0_ref[...]
    else:
        st_ref[...] = jnp.zeros(st_ref.shape, F32)

    ri = lax.broadcasted_iota(jnp.int32, (c, c), 0)
    ci = lax.broadcasted_iota(jnp.int32, (c, c), 1)
    tri_f = jnp.where(ci <= ri, 1.0, 0.0).astype(BF16)
    tri_b = jnp.where(ci >= ri, 1.0, 0.0).astype(BF16)
    ones = jnp.ones((HG_DK, HG_DV), BF16)
    rowid = lax.broadcasted_iota(jnp.int32, (c, width), 0)
    rowid_half = lax.broadcasted_iota(jnp.int32, (hc, width), 0)

    def direction(r0, pre_ref, lb, tri, fwd, d):
        f = lb + (1.0 - lb) * jax.nn.sigmoid(pre_ref[pl.ds(r0, c), :])
        k = 1.0 - f
        b = _dot_exact_lhs(tri, jnp.log(f))
        b_last = b[c - 1:c] if fwd else b[0:1]
        q = q_ref[pl.ds(r0, c), :] * (HG_DK ** -0.5)
        v = v_ref[pl.ds(r0, c), :]
        q_in = (q * jnp.exp(b)).astype(BF16)
        k_end = (k * jnp.exp(b_last - b)).astype(BF16)
        dec = jnp.exp(b_last)
        v16 = v.astype(BF16)
        wide = range(0, hc) if fwd else range(hc, c)
        narrow = range(hc, c) if fwd else range(0, hc)
        far = slice(hc, c) if fwd else slice(0, hc)
        blocks = []
        for s in wide:
            p = q * jnp.exp(b - b[s:s + 1]) * k[s:s + 1]
            keep = (rowid >= s) if fwd else (rowid <= s)
            blocks.append(jnp.where(keep, p, 0.0).astype(BF16))
        q_far, b_far = q[far], b[far]
        row_far = rowid_half + (hc if fwd else 0)
        halves = []
        for s in narrow:
            p = q_far * jnp.exp(b_far - b[s:s + 1]) * k[s:s + 1]
            keep = (row_far >= s) if fwd else (row_far <= s)
            halves.append(jnp.where(keep, p, 0.0))
        for i in range(0, hc, 2):
            blocks.append(jnp.concatenate(halves[i:i + 2], axis=0).astype(BF16))
        pairs = jnp.concatenate(blocks, axis=0)
        outs = []
        for h in range(HG_HEADS):
            sl = slice(h * HG_DK, (h + 1) * HG_DK)
            a = _dot(pairs[:, sl], ones)
            st = st_ref[d, h]
            o = _dot_nt(q_in[:, sl], st.astype(BF16))
            for n, s in enumerate(wide):
                o = o + a[n * c:(n + 1) * c] * v[s:s + 1, sl]
            o_far = None
            for n, s in enumerate(narrow):
                term = a[hc * c + n * hc:hc * c + (n + 1) * hc] * v[s:s + 1, sl]
                o_far = term if o_far is None else o_far + term
            zero = jnp.zeros_like(o_far)
            o = o + jnp.concatenate([zero, o_far] if fwd else [o_far, zero], axis=0)
            st_ref[d, h] = st * dec[:, sl] + _dot_tn(v16[:, sl], k_end[:, sl])
            outs.append(o)
        return jnp.concatenate(outs, axis=1)

    lb_f = lb_ref[0:1, :]
    lb_b = lb_ref[1:2, :]

    def body(n, carry):
        r0 = pl.multiple_of(n * c, c)
        of_ref[pl.ds(r0, c), :] = direction(r0, ff_ref, lb_f, tri_f, True, 0)
        r1 = pl.multiple_of((n_chunks - 1 - n) * c, c)
        ob_ref[pl.ds(r1, c), :] = direction(r1, fb_ref, lb_b, tri_b, False, 1)
        return carry

    lax.fori_loop(0, n_chunks, body, 0)

    for h in range(HG_HEADS):
        sl = slice(h * HG_DV, (h + 1) * HG_DV)
        o = of_ref[:, sl] + ob_ref[:, sl]
        o_ref[:, sl] = _rms(o) * gain_ref[:, sl] * _silu(g_ref[:, sl])
    if not has_state:
        sout_ref[...] = st_ref[...]


def _hgrn(ymix, lb, gain, batch, seq, s0_t):
    width = HG_HEADS * HG_DK
    has_state = s0_t is not None
    st_shape = (2, HG_HEADS, HG_DV, HG_DK)
    in_specs = [pl.BlockSpec((seq, width), functools.partial(lambda b, col: (b, col), col=col))
                for col in (COL_HQ, COL_HFF, COL_HFB, COL_HI, COL_HG)]
    in_specs += [pl.BlockSpec((2, width), lambda b: (0, 0)), pl.BlockSpec((1, width), lambda b: (0, 0))]
    args = [ymix] * 5 + [lb, gain]
    out_specs = [pl.BlockSpec((seq, width), lambda b: (b, 0))]
    out_shape = [jax.ShapeDtypeStruct((M_ROWS, width), F32)]
    if has_state:
        in_specs.append(pl.BlockSpec((None,) + st_shape, lambda b: (b, 0, 0, 0, 0)))
        args.append(s0_t)
    else:
        out_specs.append(pl.BlockSpec((None,) + st_shape, lambda b: (b, 0, 0, 0, 0)))
        out_shape.append(jax.ShapeDtypeStruct((batch,) + st_shape, F32))
    return pl.pallas_call(
        functools.partial(_hgrn_kernel, seq=seq, has_state=has_state),
        grid=(batch,),
        in_specs=in_specs,
        out_specs=out_specs,
        out_shape=out_shape,
        scratch_shapes=[pltpu.VMEM((seq, width), F32), pltpu.VMEM((seq, width), F32),
                        pltpu.VMEM(st_shape, F32)],
        compiler_params=_params("parallel"),
        name="hgrn_lat" if has_state else "hgrn_ctx",
    )(*args)


def _merge_kernel(oa_ref, ob_ref, oc_ref, od_ref, gt_ref, wb_ref, wo_ref, x_ref, g1_ref, out_ref):
    acc = None
    for n, o_ref in enumerate((oa_ref, ob_ref, oc_ref, od_ref)):
        bo = _dot(o_ref[...].astype(BF16), wb_ref[n])
        term = jax.nn.sigmoid(gt_ref[:, n * D_MODEL:(n + 1) * D_MODEL]) * bo
        acc = term if acc is None else acc + term
    out_ref[...] = x_ref[...] + g1_ref[...] * _dot(acc.astype(BF16), wo_ref[...])


def _merge(branches, gates, w_branch, w_out, x, mod, latent):
    tm = 256
    row = _mod_row(latent, tm)
    tile = lambda w: pl.BlockSpec((tm, w), lambda i: (i, 0))
    return pl.pallas_call(
        _merge_kernel,
        grid=(M_ROWS // tm,),
        in_specs=[tile(BRANCH_W)] * N_BRANCH + [
            tile(GATE_W),
            pl.BlockSpec((N_BRANCH, BRANCH_W, D_MODEL), lambda i: (0, 0, 0)),
            pl.BlockSpec((D_MODEL, D_MODEL), lambda i: (0, 0)),
            tile(D_MODEL),
            pl.BlockSpec((None, None, 1, D_MODEL), lambda i: (row(i), 2, 0, 0)),
        ],
        out_specs=tile(D_MODEL),
        out_shape=jax.ShapeDtypeStruct((M_ROWS, D_MODEL), F32),
        compiler_params=_params("parallel"),
        name="merge_lat" if latent else "merge_ctx",
    )(*branches, gates, w_branch, w_out, x, mod)


FFN_CHUNK = 256
FFN_STEPS = FFN_HIDDEN // FFN_CHUNK


def _ffn_kernel(x_ref, sc_ref, sh_ref, g2_ref, wa_ref, wg_ref, wo_ref, out_ref, h_ref, acc_ref):
    j = pl.program_id(1)

    @pl.when(j == 0)
    def _():
        h = _rms(x_ref[...]) * (1.0 + sc_ref[...]) + sh_ref[...]
        h_ref[...] = h.astype(BF16)
        acc_ref[...] = jnp.zeros(acc_ref.shape, F32)

    h = h_ref[...]
    act = _silu(_dot(h, wg_ref[...])) * _dot(h, wa_ref[...])
    acc_ref[...] += _dot(act.astype(BF16), wo_ref[...])

    @pl.when(j == FFN_STEPS - 1)
    def _():
        out_ref[...] = x_ref[...] + g2_ref[...] * acc_ref[...]


def _ffn(x, mod, w_in, w_out, latent):
    tm = 1024
    row = _mod_row(latent, tm)
    modspec = lambda which: pl.BlockSpec((None, None, 1, D_MODEL), lambda i, j: (row(i), which, 0, 0))
    return pl.pallas_call(
        _ffn_kernel,
        grid=(M_ROWS // tm, FFN_STEPS),
        in_specs=[
            pl.BlockSpec((tm, D_MODEL), lambda i, j: (i, 0)),
            modspec(4), modspec(3), modspec(5),
            pl.BlockSpec((D_MODEL, FFN_CHUNK), lambda i, j: (0, j)),
            pl.BlockSpec((D_MODEL, FFN_CHUNK), lambda i, j: (0, FFN_STEPS + j)),
            pl.BlockSpec((FFN_CHUNK, D_MODEL), lambda i, j: (j, 0)),
        ],
        out_specs=pl.BlockSpec((tm, D_MODEL), lambda i, j: (i, 0)),
        out_shape=jax.ShapeDtypeStruct((M_ROWS, D_MODEL), F32),
        scratch_shapes=[pltpu.VMEM((tm, D_MODEL), BF16), pltpu.VMEM((tm, D_MODEL), F32)],
        compiler_params=_params("parallel", "arbitrary"),
        name="ffn_lat" if latent else "ffn_ctx",
    )(x, mod, mod, mod, w_in, w_in, w_out)


def _final_kernel(x_ref, g_ref, o_ref):
    o_ref[...] = _rms(x_ref[...]) * g_ref[...]


def _final_norm(x, gain):
    tm = 512
    return pl.pallas_call(
        _final_kernel,
        grid=(M_ROWS // tm,),
        in_specs=[pl.BlockSpec((tm, D_MODEL), lambda i: (i, 0)), pl.BlockSpec((1, D_MODEL), lambda i: (0, 0))],
        out_specs=pl.BlockSpec((tm, D_MODEL), lambda i: (i, 0)),
        out_shape=jax.ShapeDtypeStruct((M_ROWS, D_MODEL), F32),
        compiler_params=_params("parallel"),
        name="final_norm",
    )(x, gain.reshape(1, D_MODEL))


def _rope_tables():
    t = jnp.arange(DEC_SEQ)
    row = (t // GRID_W).astype(F32)[:, None]
    col = (t % GRID_W).astype(F32)[:, None]

    def angles(rot_dim):
        n_freq = rot_dim // 4
        inv_freq = ROPE_THETA ** (-jnp.arange(n_freq, dtype=F32) / n_freq)
        return jnp.concatenate([row * inv_freq, col * inv_freq], axis=-1)

    def expand(ang):
        cos = jnp.repeat(jnp.cos(ang), 2, axis=-1)
        sin = jnp.repeat(jnp.sin(ang), 2, axis=-1)
        even = (jnp.arange(cos.shape[-1]) % 2 == 0)[None, :]
        return cos, jnp.where(even, -sin, 0.0), jnp.where(even, 0.0, sin)

    gq = [jnp.concatenate([a, a], axis=-1) for a in expand(angles(GQ_HD))]
    pad_l = LANE - ML_NOPE - ML_ROPE
    ml = []
    for idx, a in enumerate(expand(angles(ML_ROPE))):
        fill = 1.0 if idx == 0 else 0.0
        ml.append(jnp.concatenate([jnp.full((DEC_SEQ, ML_NOPE), fill, F32), a,
                                   jnp.full((DEC_SEQ, pad_l), fill, F32)], axis=-1))
    return {"gq": jnp.stack(gq), "ml": jnp.stack(ml)}


def _layer_weights(l, w_in, gq_q_gain, gq_k_gain, ml_q_a_gain, ml_kv_a_gain, ml_w_q_b, ml_w_kv_b,
                   w_branch, w_out, w_ffn_in, w_ffn_out, hg_gain, avg):
    w = w_in[l]
    zeros = lambda n: jnp.zeros((D_MODEL, n), F32)
    w_mix = jnp.concatenate([
        w[:, 0:3072], w[:, 3328:4864], w[:, 4864:5120], w[:, 3072:3328], w[:, 5120:5248],
        zeros(ML_NOPE), w[:, 5248:5280], zeros(LANE - ML_NOPE - ML_ROPE)], axis=1).astype(BF16)
    assert w_mix.shape[1] == MIX_W
    w_gate = w[:, 5280:].astype(BF16)
    qb = ml_w_q_b[l].reshape(ML_Q_RANK, ML_HEADS, ML_NOPE + ML_ROPE)
    qb = jnp.pad(qb, ((0, 0), (0, 0), (0, LANE - ML_NOPE - ML_ROPE))).reshape(ML_Q_RANK, ML_HEADS * LANE)
    kvb = ml_w_kv_b[l].reshape(ML_KV_RANK, ML_HEADS, ML_NOPE + ML_V)
    wk = jnp.pad(kvb[:, :, :ML_NOPE], ((0, 0), (0, 0), (0, LANE - ML_NOPE))).reshape(ML_KV_RANK, ML_HEADS * LANE)
    wv = kvb[:, :, ML_NOPE:].reshape(ML_KV_RANK, ML_HEADS * ML_V)
    return {
        "w_mix": w_mix, "w_gate": w_gate,
        "gq_q_gain": jnp.tile(gq_q_gain[l], GQ_HEADS).reshape(1, -1),
        "gq_k_gain": jnp.tile(gq_k_gain[l], GQ_KV_HEADS).reshape(1, -1),
        "ml_q_gain": ml_q_a_gain[l].reshape(1, -1), "ml_kv_gain": ml_kv_a_gain[l].reshape(1, -1),
        "w_qb": qb.astype(BF16), "w_k": wk.astype(BF16), "w_v": wv.astype(BF16),
        "w_branch": w_branch[l].astype(BF16), "w_out": w_out[l].astype(BF16),
        "w_ffn_in": w_ffn_in[l].astype(BF16), "w_ffn_out": w_ffn_out[l].astype(BF16),
        "hg_gain": jnp.tile(hg_gain[l], HG_HEADS).reshape(1, -1), "avg": avg,
    }


def _seg(arr, width, col, rows, stride, off):
    return (arr, width, col, rows, stride, off)


def kernel(x_prompt, x_sample, state_hgrn, cache_gqa_k, cache_gqa_v, cache_na_k, cache_na_v, cache_mla_ckv, cache_mla_krope, c, c_ctx, w_ada, b_ada, w_in, hg_lb_logits, hg_gain, gq_q_gain, gq_k_gain, na_rpb, ml_q_a_gain, ml_kv_a_gain, ml_w_q_b, ml_w_kv_b, w_branch, w_out, w_ffn_in, w_ffn_out, final_gain):
    cond8 = jnp.concatenate([c_ctx[None, :], c, jnp.zeros((8 - 1 - DEC_BATCH, D_MODEL), F32)], axis=0)
    mods = _ada(cond8, w_ada, b_ada)

    lb = jnp.cumsum(jax.nn.softmax(hg_lb_logits.astype(F32), axis=0), axis=0)
    lb = lb - lb[:1]
    avg = jnp.asarray(np.kron(np.eye(512 // GQ_HD), np.full((GQ_HD, GQ_HD), 1.0 / GQ_HD)), BF16)
    tabs = _rope_tables()

    gqk_c = cache_gqa_k.reshape(DEC_BATCH * DEPTH * PAST_LEN, GQ_KV_HEADS * GQ_HD)
    gqv_c = cache_gqa_v.reshape(DEC_BATCH * DEPTH * PAST_LEN, GQ_KV_HEADS * GQ_HD)
    nak_c = cache_na_k.reshape(DEC_BATCH * DEPTH * PAST_LEN, NA_HEADS * NA_HD)
    nav_c = cache_na_v.reshape(DEC_BATCH * DEPTH * PAST_LEN, NA_HEADS * NA_HD)
    mckv_c = cache_mla_ckv.reshape(DEC_BATCH * DEPTH * PAST_LEN, ML_KV_RANK)
    mkr_c = jnp.pad(cache_mla_krope.reshape(DEC_BATCH * DEPTH * PAST_LEN, ML_ROPE),
                    ((0, 0), (ML_NOPE, LANE - ML_NOPE - ML_ROPE)))

    xp = x_prompt.reshape(M_ROWS, D_MODEL)
    xs = x_sample.reshape(M_ROWS, D_MODEL)
    new = []
    for l in range(DEPTH):
        lw = _layer_weights(l, w_in, gq_q_gain, gq_k_gain, ml_q_a_gain, ml_kv_a_gain, ml_w_q_b,
                            ml_w_kv_b, w_branch, w_out, w_ffn_in, w_ffn_out, hg_gain, avg)
        mod = mods[l]

        ymix = _inproj(xp, mod, lw["w_mix"], False, 1792, "inproj_mix_ctx")
        gates = _inproj(xp, mod, lw["w_gate"], False, 2048, "inproj_gate_ctx")
        qb, kb, qd, ckv, kd, vd = _prep(ymix, lw, tabs, False)
        out_a, st = _hgrn(ymix, lb[l], lw["hg_gain"], BATCH, SEQ, None)
        out_b = _attention((qb, 512, 0, 0),
                           [(_seg(kb, LANE, 0, SEQ, 1, 0), _seg(ymix, LANE, COL_GV, SEQ, 1, 0))],
                           _plan_gqa(), GQ_SCALE, BATCH, SEQ, "gqa_ctx")
        out_c = _attention((ymix, 512, COL_NQ, 0),
                           [(_seg(ymix, 512, COL_NK, SEQ, 1, 0), _seg(ymix, 512, COL_NV, SEQ, 1, 0))],
                           _plan_na(), NA_SCALE, BATCH, SEQ, "na_ctx")
        out_d = _attention((qd, ML_HEADS * LANE, 0, 0),
                           [(_seg(kd, ML_HEADS * LANE, 0, SEQ, 1, 0), _seg(vd, 512, 0, SEQ, 1, 0))],
                           _plan_mla(), ML_SCALE, BATCH, SEQ, "mla_ctx")
        xp = _merge((out_a, out_b, out_c, out_d), gates, lw["w_branch"], lw["w_out"], xp, mod, False)
        xp = _ffn(xp, mod, lw["w_ffn_in"], lw["w_ffn_out"], False)
        new.append((
            jnp.swapaxes(st, -1, -2),
            kb.reshape(BATCH, SEQ, GQ_KV_HEADS, GQ_HD),
            ymix[:, COL_GV * LANE:(COL_GV + 1) * LANE].reshape(BATCH, SEQ, GQ_KV_HEADS, GQ_HD),
            ymix[:, COL_NK * 512:(COL_NK + 1) * 512].reshape(BATCH, SEQ, NA_HEADS, NA_HD),
            ymix[:, COL_NV * 512:(COL_NV + 1) * 512].reshape(BATCH, SEQ, NA_HEADS, NA_HD),
            ckv.reshape(BATCH, SEQ, ML_KV_RANK),
            ymix[:, COL_MKR * LANE + ML_NOPE:COL_MKR * LANE + ML_NOPE + ML_ROPE].reshape(BATCH, SEQ, ML_ROPE),
        ))

        ymix = _inproj(xs, mod, lw["w_mix"], True, 1792, "inproj_mix_lat")
        gates = _inproj(xs, mod, lw["w_gate"], True, 2048, "inproj_gate_lat")
        qb, kb, qd, ckv, kd, vd = _prep(ymix, lw, tabs, True)
        kd_c, vd_c = _mla_cache(mckv_c, mkr_c, lw, l)
        s0_t = jnp.swapaxes(state_hgrn[:, l], -1, -2)
        out_a, = _hgrn(ymix, lb[l], lw["hg_gain"], DEC_BATCH, DEC_SEQ, s0_t)
        out_b = _attention((qb, 512, 0, 0),
                           [(_seg(gqk_c, LANE, 0, PAST_LEN, DEPTH, l), _seg(gqv_c, LANE, 0, PAST_LEN, DEPTH, l)),
                            (_seg(kb, LANE, 0, DEC_SEQ, 1, 0), _seg(ymix, LANE, COL_GV, DEC_SEQ, 1, 0))],
                           _plan_gqa(), GQ_SCALE, DEC_BATCH, DEC_SEQ, "gqa_lat")
        out_c = _na_latent(ymix, nak_c, nav_c, _na_bias_tables(na_rpb[l]), l)
        out_d = _attention((qd, ML_HEADS * LANE, 0, 0),
                           [(_seg(kd_c, ML_HEADS * LANE, 0, PAST_LEN, 1, 0), _seg(vd_c, 512, 0, PAST_LEN, 1, 0)),
                            (_seg(kd, ML_HEADS * LANE, 0, DEC_SEQ, 1, 0), _seg(vd, 512, 0, DEC_SEQ, 1, 0))],
                           _plan_mla(), ML_SCALE, DEC_BATCH, DEC_SEQ, "mla_lat")
        xs = _merge((out_a, out_b, out_c, out_d), gates, lw["w_branch"], lw["w_out"], xs, mod, True)
        xs = _ffn(xs, mod, lw["w_ffn_in"], lw["w_ffn_out"], True)

    y_prompt = _final_norm(xp, final_gain).reshape(BATCH, SEQ, D_MODEL)
    y_sample = _final_norm(xs, final_gain).reshape(DEC_BATCH, DEC_SEQ, D_MODEL)
    stacked = tuple(jnp.stack([n[i] for n in new], axis=1) for i in range(7))
    return (y_prompt, y_sample) + stacked
```

```python
import functools

import numpy as np
import jax
import jax.numpy as jnp
from jax import lax
from jax.experimental import pallas as pl
from jax.experimental.pallas import tpu as pltpu

F32 = jnp.float32
BF16 = jnp.bfloat16

D_MODEL = 1024
BATCH = 16
SEQ = 256
DEPTH = 2
DEC_BATCH = 4
DEC_SEQ = 1024
PAST_LEN = 512
GRID_W = 64
EPS = 1e-6
ROPE_THETA = 10000.0
N_BRANCH = 4
BRANCH_W = 512
HG_HEADS = 4
HG_DK = 128
HG_DV = 128
GQ_HEADS = 8
GQ_KV_HEADS = 2
GQ_HD = 64
NA_HEADS = 8
NA_HD = 64
NA_WIN_R = 8
NA_WIN_C = 16
ML_HEADS = 8
ML_NOPE = 64
ML_ROPE = 32
ML_V = 64
ML_Q_RANK = 256
ML_KV_RANK = 128
FFN_HIDDEN = 2816
GQ_SCALE = GQ_HD ** -0.5
NA_SCALE = NA_HD ** -0.5
ML_SCALE = (ML_NOPE + ML_ROPE) ** -0.5

M_ROWS = BATCH * SEQ
assert M_ROWS == DEC_BATCH * DEC_SEQ

LANE = 128
HALF = 64
MIX_W = 5376
GATE_W = N_BRANCH * D_MODEL
HG_CHUNK = 16
NEG = -1e30
LOG2E = 1.4426950408889634
VMEM_LIMIT = 56 * 1024 * 1024

COL_HQ, COL_HFF, COL_HFB, COL_HI, COL_HG, COL_GQ, COL_NQ, COL_NK, COL_NV = range(9)
COL_MQA = 18
COL_GK, COL_GV, COL_MKVA, COL_MKR = 38, 39, 40, 41


def _dot(a, b):
    return jnp.dot(a, b, preferred_element_type=F32)


def _dot_nt(a, b):
    return lax.dot_general(a, b, (((1,), (1,)), ((), ())), preferred_element_type=F32)


def _dot_tn(a, b):
    return lax.dot_general(a, b, (((0,), (0,)), ((), ())), preferred_element_type=F32)


def _split3(x):
    x1 = x.astype(BF16)
    r1 = x - x1.astype(F32)
    x2 = r1.astype(BF16)
    x3 = (r1 - x2.astype(F32)).astype(BF16)
    return x1, x2, x3


def _dot_exact_lhs(a_bf16, x):
    x1, x2, x3 = _split3(x)
    return (_dot(a_bf16, x3) + _dot(a_bf16, x2)) + _dot(a_bf16, x1)


def _dot_exact_rhs(x, b_bf16):
    x1, x2, x3 = _split3(x)
    return (_dot(x3, b_bf16) + _dot(x2, b_bf16)) + _dot(x1, b_bf16)


def _dot_hi(a, b):
    a1 = a.astype(BF16)
    a2 = (a - a1.astype(F32)).astype(BF16)
    b1 = b.astype(BF16)
    b2 = (b - b1.astype(F32)).astype(BF16)
    return (_dot(a1, b2) + _dot(a2, b1)) + _dot(a1, b1)


def _rms(x):
    return x * lax.rsqrt(jnp.mean(x * x, axis=-1, keepdims=True) + EPS)


def _silu(x):
    return x * jax.nn.sigmoid(x)


def _aligned(x, m):
    return x if isinstance(x, int) else pl.multiple_of(x, m)


def _params(*sem):
    return pltpu.CompilerParams(dimension_semantics=sem, vmem_limit_bytes=VMEM_LIMIT)


def _mod_row(latent, tm):
    if latent:
        return lambda i: 1 + (i * tm) // DEC_SEQ
    return lambda i: 0


def _ada_kernel(c_ref, w_ref, b_ref, o_ref):
    c = c_ref[...]
    o_ref[...] = _dot_hi(_silu(c), w_ref[...]) + b_ref[...]


def _ada(cond8, w_ada, b_ada):
    tn = 1536
    out = pl.pallas_call(
        _ada_kernel,
        grid=(DEPTH, 6 * D_MODEL // tn),
        in_specs=[
            pl.BlockSpec((8, D_MODEL), lambda l, j: (0, 0)),
            pl.BlockSpec((None, D_MODEL, tn), lambda l, j: (l, 0, j)),
            pl.BlockSpec((None, 1, tn), lambda l, j: (l, 0, j)),
        ],
        out_specs=pl.BlockSpec((None, 8, tn), lambda l, j: (l, 0, j)),
        out_shape=jax.ShapeDtypeStruct((DEPTH, 8, 6 * D_MODEL), F32),
        compiler_params=_params("parallel", "parallel"),
        name="ada",
    )(cond8, w_ada, b_ada.reshape(DEPTH, 1, 6 * D_MODEL))
    return out.reshape(DEPTH, 8, 6, 1, D_MODEL)


def _inproj_kernel(x_ref, sc_ref, sh_ref, w_ref, o_ref, h_ref, *, gate):
    @pl.when(pl.program_id(1) == 0)
    def _():
        h = _rms(x_ref[...]) * (1.0 + sc_ref[...]) + sh_ref[...]
        h_ref[...] = h.astype(BF16)

    y = _dot(h_ref[...], w_ref[...])
    o_ref[...] = jax.nn.sigmoid(y).astype(o_ref.dtype) if gate else y


def _inproj(x, mod, w, latent, tn, name, gate=False):
    tm = 1024
    n = w.shape[1]
    row = _mod_row(latent, tm)
    return pl.pallas_call(
        functools.partial(_inproj_kernel, gate=gate),
        grid=(M_ROWS // tm, n // tn),
        in_specs=[
            pl.BlockSpec((tm, D_MODEL), lambda i, j: (i, 0)),
            pl.BlockSpec((None, None, 1, D_MODEL), lambda i, j: (row(i), 1, 0, 0)),
            pl.BlockSpec((None, None, 1, D_MODEL), lambda i, j: (row(i), 0, 0, 0)),
            pl.BlockSpec((D_MODEL, tn), lambda i, j: (0, j)),
        ],
        out_specs=pl.BlockSpec((tm, tn), lambda i, j: (i, j)),
        out_shape=jax.ShapeDtypeStruct((M_ROWS, n), BF16 if gate else F32),
        scratch_shapes=[pltpu.VMEM((tm, D_MODEL), BF16)],
        compiler_params=_params("parallel", "arbitrary"),
        name=name,
    )(x, mod, mod, w)


def _rope(x, tab_ref, reps):
    w = x.shape[1]
    c = jnp.concatenate([tab_ref[0]] * reps, axis=1) if reps > 1 else tab_ref[0]
    se = jnp.concatenate([tab_ref[1]] * reps, axis=1) if reps > 1 else tab_ref[1]
    so = jnp.concatenate([tab_ref[2]] * reps, axis=1) if reps > 1 else tab_ref[2]
    return x * c + pltpu.roll(x, w - 1, 1) * se + pltpu.roll(x, 1, 1) * so


def _head_rms(x, gain, avg_bf16):
    ms = _dot_exact_rhs(x * x, avg_bf16)
    return x * lax.rsqrt(ms + EPS) * gain


def _prep_kernel(*refs, rope):
    (gq_ref, gk_ref, mqa_ref, mkva_ref, mkr_ref, gqg_ref, gkg_ref, mqg_ref, mkvg_ref,
     wqb_ref, wk_ref, wv_ref, avg_ref) = refs[:13]
    if rope:
        gtab_ref, mtab_ref = refs[13:15]
        outs = refs[15:]
    else:
        outs = refs[13:]
    qb_ref, kb_ref, qd_ref, ckv_ref, kd_ref, vd_ref = outs

    q = _head_rms(gq_ref[...], gqg_ref[...], avg_ref[...])
    k = _head_rms(gk_ref[...], gkg_ref[...], avg_ref[0:LANE, 0:LANE])
    qd = _dot((_rms(mqa_ref[...]) * mqg_ref[...]).astype(BF16), wqb_ref[...])
    ckv = _rms(mkva_ref[...]) * mkvg_ref[...]
    kr = mkr_ref[...]
    if rope:
        q = _rope(q, gtab_ref, GQ_HEADS * GQ_HD // LANE)
        k = _rope(k, gtab_ref, 1)
        qd = _rope(qd, mtab_ref, ML_HEADS)
        kr = _rope(kr, mtab_ref, 1)
    qb_ref[...] = q.astype(BF16)
    kb_ref[...] = k
    qd_ref[...] = qd.astype(BF16)
    ckv_ref[...] = ckv
    cb = ckv.astype(BF16)
    kd_ref[...] = (_dot(cb, wk_ref[...]) + jnp.concatenate([kr] * ML_HEADS, axis=1)).astype(BF16)
    vd_ref[...] = _dot(cb, wv_ref[...]).astype(BF16)


def _prep(ymix, lw, tabs, latent):
    tm = 512
    const = lambda i: (0, 0)
    in_specs = [
        pl.BlockSpec((tm, 512), lambda i: (i, COL_GQ)),
        pl.BlockSpec((tm, LANE), lambda i: (i, COL_GK)),
        pl.BlockSpec((tm, 256), lambda i: (i, COL_MQA)),
        pl.BlockSpec((tm, LANE), lambda i: (i, COL_MKVA)),
        pl.BlockSpec((tm, LANE), lambda i: (i, COL_MKR)),
        pl.BlockSpec((1, 512), const),
        pl.BlockSpec((1, LANE), const),
        pl.BlockSpec((1, 256), const),
        pl.BlockSpec((1, LANE), const),
        pl.BlockSpec((ML_Q_RANK, ML_HEADS * LANE), const),
        pl.BlockSpec((ML_KV_RANK, ML_HEADS * LANE), const),
        pl.BlockSpec((ML_KV_RANK, ML_HEADS * ML_V), const),
        pl.BlockSpec((512, 512), const),
    ]
    args = [ymix, ymix, ymix, ymix, ymix, lw["gq_q_gain"], lw["gq_k_gain"], lw["ml_q_gain"],
            lw["ml_kv_gain"], lw["w_qb"], lw["w_k"], lw["w_v"], lw["avg"]]
    if latent:
        per = DEC_SEQ // tm
        in_specs += [pl.BlockSpec((3, tm, LANE), lambda i: (0, i % per, 0))] * 2
        args += [tabs["gq"], tabs["ml"]]
    widths = (512, LANE, ML_HEADS * LANE, LANE, ML_HEADS * LANE, ML_HEADS * ML_V)
    return pl.pallas_call(
        functools.partial(_prep_kernel, rope=latent),
        grid=(M_ROWS // tm,),
        in_specs=in_specs,
        out_specs=[pl.BlockSpec((tm, w), lambda i: (i, 0)) for w in widths],
        out_shape=[jax.ShapeDtypeStruct((M_ROWS, w), dt)
                   for w, dt in zip(widths, (BF16, F32, BF16, F32, BF16, BF16))],
        compiler_params=_params("parallel"),
        name="prep_lat" if latent else "prep_ctx",
    )(*args)


def _mla_cache_kernel(ckv_ref, kr_ref, wk_ref, wv_ref, kd_ref, vd_ref):
    cb = ckv_ref[...].astype(BF16)
    kd_ref[...] = (_dot(cb, wk_ref[...]) + jnp.concatenate([kr_ref[...]] * ML_HEADS, axis=1)).astype(BF16)
    vd_ref[...] = _dot(cb, wv_ref[...]).astype(BF16)


def _mla_cache(ckv, kr_blk, lw, layer):
    rows = DEC_BATCH * PAST_LEN
    tm = PAST_LEN
    const = lambda i: (0, 0)
    return pl.pallas_call(
        _mla_cache_kernel,
        grid=(DEC_BATCH,),
        in_specs=[
            pl.BlockSpec((tm, LANE), lambda i: (i * DEPTH + layer, 0)),
            pl.BlockSpec((tm, LANE), lambda i: (i * DEPTH + layer, 0)),
            pl.BlockSpec((ML_KV_RANK, ML_HEADS * LANE), const),
            pl.BlockSpec((ML_KV_RANK, ML_HEADS * ML_V), const),
        ],
        out_specs=[pl.BlockSpec((tm, ML_HEADS * LANE), lambda i: (i, 0)),
                   pl.BlockSpec((tm, ML_HEADS * ML_V), lambda i: (i, 0))],
        out_shape=[jax.ShapeDtypeStruct((rows, ML_HEADS * LANE), BF16),
                   jax.ShapeDtypeStruct((rows, ML_HEADS * ML_V), BF16)],
        compiler_params=_params("parallel"),
        name="mla_cache",
    )(ckv, kr_blk, lw["w_k"], lw["w_v"])


def _softmax_pv(scores, values):
    m = scores[0].max(axis=-1, keepdims=True)
    for s in scores[1:]:
        m = jnp.maximum(m, s.max(axis=-1, keepdims=True))
    den = None
    out = None
    for s, v in zip(scores, values):
        p = jnp.exp(s - m)
        d = p.sum(axis=-1, keepdims=True)
        o = _dot(p.astype(BF16), v)
        den = d if den is None else den + d
        out = o if out is None else out + o
    return out / den


def _lane_halves():
    lane = lax.broadcasted_iota(jnp.int32, (1, LANE), 1)
    return lane < HALF, lane >= HALF


def _attn_kernel(*refs, plan, nseg, scale, fold_scale):
    q_ref = refs[0]
    k_refs = [refs[1 + 2 * i] for i in range(nseg)]
    v_refs = [refs[2 + 2 * i] for i in range(nseg)]
    o_ref = refs[1 + 2 * nseg]
    tq = q_ref.shape[0]
    lo, hi = _lane_halves()
    cache = {}

    def block(kind, seg, blk, swap):
        key = (kind, seg, blk, swap)
        if key not in cache:
            ref = (k_refs if kind == "k" else v_refs)[seg]
            x = ref[:, blk * LANE:(blk + 1) * LANE]
            if swap:
                x = pltpu.roll(x, HALF, 1)
            cache[key] = x.astype(BF16)
        return cache[key]

    parts = {}
    for members, kblk, swap, vblk in plan:
        qs = []
        for qblk, qhalf, _, _ in members:
            q = q_ref[:, qblk * LANE:(qblk + 1) * LANE]
            if fold_scale:
                q = q * scale
            if qhalf is not None:
                q = jnp.where(lo if qhalf == 0 else hi, q, 0.0)
            qs.append(q)
        q = (jnp.concatenate(qs, axis=0) if len(qs) > 1 else qs[0]).astype(BF16)
        scores = [_dot_nt(q, block("k", s, kblk, swap)) for s in range(nseg)]
        if not fold_scale:
            scores = [s * scale for s in scores]
        out = _softmax_pv(scores, [block("v", s, vblk, swap) for s in range(nseg)])
        for n, (_, _, oblk, ohalf) in enumerate(members):
            parts[(oblk, ohalf)] = out[n * tq:(n + 1) * tq]
    for oblk in sorted({key[0] for key in parts}):
        o_ref[:, oblk * LANE:(oblk + 1) * LANE] = jnp.where(lo, parts[(oblk, 0)], parts[(oblk, 1)]).astype(BF16)


def _plan_gqa():
    per_kv = GQ_HEADS // GQ_KV_HEADS
    plan = []
    for g in range(GQ_KV_HEADS):
        for half in range(2):
            heads = [h for h in range(g * per_kv, (g + 1) * per_kv) if h % 2 == half]
            plan.append((tuple((h // 2, half, h // 2, half) for h in heads), 0, half != g, 0))
    return tuple(plan)


def _plan_na():
    return tuple((((j, 0, j, 0), (j, 1, j, 1)), j, False, j) for j in range(NA_HEADS // 2))


def _plan_mla():
    return tuple((((h, None, h // 2, h % 2),), h, False, h // 2) for h in range(ML_HEADS))


def _is_pow2(x):
    return float(np.log2(x)).is_integer()


def _attention(q, segs, plan, scale, batch, tq_total, name):
    tq = min(tq_total, 256)
    per = tq_total // tq
    q_arr, q_w, q_col, q_off = q
    in_specs = [pl.BlockSpec((tq, q_w), lambda b, i: (q_off + b * per + i, q_col))]
    args = [q_arr]
    for k, v in segs:
        for arr, w, col, rows, stride, off in (k, v):
            in_specs.append(pl.BlockSpec(
                (rows, w), functools.partial(lambda b, i, col, stride, off: (off + b * stride, col),
                                             col=col, stride=stride, off=off)))
            args.append(arr)
    out_w = 512
    return pl.pallas_call(
        functools.partial(_attn_kernel, plan=plan, nseg=len(segs), scale=scale, fold_scale=_is_pow2(scale)),
        grid=(batch, per),
        in_specs=in_specs,
        out_specs=pl.BlockSpec((tq, out_w), lambda b, i: (b * per + i, 0)),
        out_shape=jax.ShapeDtypeStruct((M_ROWS, out_w), BF16),
        compiler_params=_params("parallel", "parallel"),
        name=name,
    )(*args)


NA_DR = 2 * NA_WIN_R - 1
NA_DC = 2 * NA_WIN_C - 1
NA_ROWS = DEC_SEQ // GRID_W
NA_QROWS = 4
NA_WROWS = 12
NA_T_RIGHT = NA_DR - 1
NA_T_LEFT = NA_DR
NA_T_NONE = NA_DR + 1
NA_T_SIZE = NA_DR + 2
assert _is_pow2(NA_SCALE)


def _na_kernel(q_ref, kl_ref, vl_ref, kc_ref, vc_ref, tz_ref, o_ref):
    g = pl.program_id(1)
    w0 = (g // 2) * (NA_ROWS - NA_WROWS)
    k0 = pl.multiple_of(w0 * GRID_W, (NA_ROWS - NA_WROWS) * GRID_W)
    win = NA_WROWS * GRID_W
    nq = NA_QROWS * GRID_W
    entry = []
    for a in range(NA_QROWS):
        rq = g * NA_QROWS + a
        start = jnp.clip(rq - NA_WIN_R // 2, 0, NA_ROWS - NA_WIN_R)
        per_pair = []
        for i in range(NA_WROWS // 2):
            kr = w0 + 2 * i
            in_l = jnp.logical_and(kr >= start, kr < start + NA_WIN_R)
            in_r = jnp.logical_and(kr + 1 >= start, kr + 1 < start + NA_WIN_R)
            both = jnp.logical_and(in_l, in_r)
            d_l = kr - rq + NA_WIN_R - 1
            per_pair.append(jnp.where(both, d_l, jnp.where(in_r, NA_T_RIGHT, jnp.where(in_l, NA_T_LEFT, NA_T_NONE))))
        entry.append(per_pair)
    lo, hi = _lane_halves()
    for j in range(NA_HEADS // 2):
        sl = slice(j * LANE, (j + 1) * LANE)
        k_loc = kl_ref[pl.ds(k0, win), sl].astype(BF16)
        v_loc = vl_ref[pl.ds(k0, win), sl].astype(BF16)
        k_ctx = kc_ref[:, sl].astype(BF16)
        v_ctx = vc_ref[:, sl].astype(BF16)
        q_pair = q_ref[:, sl] * NA_SCALE
        q = jnp.concatenate([jnp.where(lo, q_pair, 0.0), jnp.where(hi, q_pair, 0.0)], axis=0).astype(BF16)
        bias = jnp.concatenate(
            [jnp.concatenate([tz_ref[2 * j + p, e] for e in entry[a]], axis=1)
             for p in range(2) for a in range(NA_QROWS)], axis=0)
        s_loc = _dot_nt(q, k_loc) + bias
        s_ctx = _dot_nt(q, k_ctx)
        out = _softmax_pv([s_loc, s_ctx], [v_loc, v_ctx])
        o_ref[:, sl] = jnp.where(lo, out[:nq], out[nq:]).astype(BF16)


def _na_bias_kernel(rpb_ref, sel_ref, neg_ref, o_ref):
    o_ref[...] = _dot_exact_rhs(rpb_ref[...], sel_ref[...]) + neg_ref[...]


def _na_bias_tables(rpb):
    col = np.arange(GRID_W)
    col_start = np.clip(col - NA_WIN_C // 2, 0, GRID_W - NA_WIN_C)
    col_ok = (col[None, :] >= col_start[:, None]) & (col[None, :] < col_start[:, None] + NA_WIN_C)
    dc = col[None, :] - col[:, None] + NA_WIN_C - 1
    kpad = 32
    sel = (dc[None, :, :] == np.arange(kpad)[:, None, None]) & col_ok[None]
    sel = jnp.asarray(sel.reshape(kpad, GRID_W * GRID_W), BF16)
    neg = jnp.asarray(np.where(col_ok, 0.0, NEG).reshape(1, GRID_W * GRID_W), F32)
    rpb2 = jnp.pad(rpb.reshape(NA_HEADS * NA_DR, NA_DC), ((0, 0), (0, kpad - NA_DC)))
    n_rows = NA_HEADS * NA_DR
    full = lambda shape: pl.BlockSpec(shape, lambda i: (0, 0))
    t = pl.pallas_call(
        _na_bias_kernel,
        grid=(1,),
        in_specs=[full((n_rows, kpad)), full((kpad, GRID_W * GRID_W)), full((1, GRID_W * GRID_W))],
        out_specs=full((n_rows, GRID_W * GRID_W)),
        out_shape=jax.ShapeDtypeStruct((n_rows, GRID_W * GRID_W), F32),
        compiler_params=_params("arbitrary"),
        name="na_bias",
    )(rpb2, sel, neg)
    t = t.reshape(NA_HEADS, NA_DR, GRID_W, GRID_W)
    masked = jnp.full((NA_HEADS, 1, GRID_W, GRID_W), NEG, F32)
    first, last = NA_WIN_R // 2 - 1, NA_WIN_R // 2 + NA_WIN_R - 2
    return jnp.concatenate([
        jnp.concatenate([t[:, :-1], t[:, 1:]], axis=-1),
        jnp.concatenate([masked, t[:, first:first + 1]], axis=-1),
        jnp.concatenate([t[:, last:last + 1], masked], axis=-1),
        jnp.concatenate([masked, masked], axis=-1)], axis=1)


def _na_latent(ymix, cache_k, cache_v, tz, layer):
    groups = NA_ROWS // NA_QROWS
    nq = NA_QROWS * GRID_W
    return pl.pallas_call(
        _na_kernel,
        grid=(DEC_BATCH, groups),
        in_specs=[
            pl.BlockSpec((nq, 512), lambda b, g: (b * groups + g, COL_NQ)),
            pl.BlockSpec((DEC_SEQ, 512), lambda b, g: (b, COL_NK)),
            pl.BlockSpec((DEC_SEQ, 512), lambda b, g: (b, COL_NV)),
            pl.BlockSpec((PAST_LEN, 512), lambda b, g: (b * DEPTH + layer, 0)),
            pl.BlockSpec((PAST_LEN, 512), lambda b, g: (b * DEPTH + layer, 0)),
            pl.BlockSpec((NA_HEADS, NA_T_SIZE, GRID_W, LANE), lambda b, g: (0, 0, 0, 0)),
        ],
        out_specs=pl.BlockSpec((nq, 512), lambda b, g: (b * groups + g, 0)),
        out_shape=jax.ShapeDtypeStruct((M_ROWS, 512), BF16),
        compiler_params=_params("parallel", "parallel"),
        name="na_lat",
    )(ymix, ymix, ymix, cache_k, cache_v, tz)


HG_BLOCK = 64
HG_STATE_UNROLL = 8
HG_DEC_ROWS = 8


def _hgrn_kernel(*refs, seq, has_state):
    if has_state:
        (q_ref, ff_ref, fb_ref, v_ref, g_ref, lb_ref, gain_ref, s0_ref,
         o_ref, of_ref, ob_ref, qin_ref, kend_ref, dec_ref, st_ref) = refs
    else:
        (q_ref, ff_ref, fb_ref, v_ref, g_ref, lb_ref, gain_ref,
         o_ref, sout_ref, of_ref, ob_ref, qin_ref, kend_ref, dec_ref, st_ref) = refs
    c = HG_CHUNK
    hc = c // 2
    rb = HG_BLOCK
    per_block = rb // c
    n_chunks = seq // c
    width = HG_HEADS * HG_DK
    sub = HG_DEC_ROWS

    ri = lax.broadcasted_iota(jnp.int32, (rb, rb), 0)
    ci = lax.broadcasted_iota(jnp.int32, (rb, rb), 1)
    same = (ri // c) == (ci // c)
    tri_f = jnp.where(jnp.logical_and(same, ci <= ri), 1.0, 0.0).astype(BF16)
    tri_b = jnp.where(jnp.logical_and(same, ci >= ri), 1.0, 0.0).astype(BF16)
    ones = jnp.ones((HG_DK, HG_DV), BF16)
    rowid = lax.broadcasted_iota(jnp.int32, (c, width), 0)
    rowid_half = lax.broadcasted_iota(jnp.int32, (hc, width), 0)
    o_refs = (of_ref, ob_ref)

    n_blocks = seq // rb

    def block_of(d, step):
        return step if d == 0 else n_blocks - 1 - step

    def pair_block(step):
        for d, (pre_ref, tri) in enumerate(((ff_ref, tri_f), (fb_ref, tri_b))):
            fwd = d == 0
            blk = block_of(d, step)
            r0 = _aligned(blk * rb, rb)
            q_all = q_ref[pl.ds(r0, rb), :] * (HG_DK ** -0.5)
            v_all = v_ref[pl.ds(r0, rb), :]
            lb = lb_ref[d:d + 1, :]
            f = lb + (1.0 - lb) * jax.nn.sigmoid(pre_ref[pl.ds(r0, rb), :])
            k_all = 1.0 - f
            b_all = _dot_exact_lhs(tri, jnp.log(f)) * LOG2E
            src_all = b_all - jnp.log(k_all) * LOG2E
            wide = range(0, hc) if fwd else range(hc, c)
            narrow = range(hc, c) if fwd else range(0, hc)
            far = slice(hc, c) if fwd else slice(0, hc)
            row_far = rowid_half + (hc if fwd else 0)
            blocks, q_in, k_end = [], [], []
            for m in range(per_block):
                rows = slice(m * c, (m + 1) * c)
                q, k, b, src = q_all[rows], k_all[rows], b_all[rows], src_all[rows]
                b_last = b[c - 1:c] if fwd else b[0:1]
                q_in.append(q * jnp.exp2(b))
                k_end.append(k * jnp.exp2(b_last - b))
                dec_row = _aligned((blk * per_block + m) * sub, sub)
                dec_ref[d, pl.ds(dec_row, sub), :] = jnp.broadcast_to(jnp.exp2(b_last), (sub, width))
                for s in wide:
                    p = q * jnp.exp2(b - src[s:s + 1])
                    keep = (rowid >= s) if fwd else (rowid <= s)
                    blocks.append(jnp.where(keep, p, 0.0).astype(BF16))
                q_far, b_far = q[far], b[far]
                halves = []
                for s in narrow:
                    p = q_far * jnp.exp2(b_far - src[s:s + 1])
                    keep = (row_far >= s) if fwd else (row_far <= s)
                    halves.append(jnp.where(keep, p, 0.0))
                for i in range(0, hc, 2):
                    blocks.append(jnp.concatenate(halves[i:i + 2], axis=0).astype(BF16))
            qin_ref[d, pl.ds(r0, rb), :] = jnp.concatenate(q_in, axis=0).astype(BF16)
            kend_ref[d, pl.ds(r0, rb), :] = jnp.concatenate(k_end, axis=0).astype(BF16)
            pairs = jnp.concatenate(blocks, axis=0)
            per_chunk = hc * c + hc * hc
            outs = []
            for h in range(HG_HEADS):
                sl = slice(h * HG_DK, (h + 1) * HG_DK)
                a = _dot(pairs[:, sl], ones)
                chunks = []
                for m in range(per_block):
                    base = m * per_chunk
                    v = v_all[m * c:(m + 1) * c]
                    o = None
                    for n, s in enumerate(wide):
                        term = a[base + n * c:base + (n + 1) * c] * v[s:s + 1, sl]
                        o = term if o is None else o + term
                    o_far = None
                    for n, s in enumerate(narrow):
                        first = base + hc * c + n * hc
                        term = a[first:first + hc] * v[s:s + 1, sl]
                        o_far = term if o_far is None else o_far + term
                    zero = jnp.zeros_like(o_far)
                    chunks.append(o + jnp.concatenate([zero, o_far] if fwd else [o_far, zero], axis=0))
                outs.append(jnp.concatenate(chunks, axis=0))
            o_refs[d][pl.ds(r0, rb), :] = jnp.concatenate(outs, axis=1)

    def state_step(n, carry):
        for d in range(2):
            chunk = n if d == 0 else n_chunks - 1 - n
            r0 = _aligned(chunk * c, c)
            q_in = qin_ref[d, pl.ds(r0, c), :]
            k_end = kend_ref[d, pl.ds(r0, c), :]
            v16 = v_ref[pl.ds(r0, c), :].astype(BF16)
            dec = dec_ref[d, pl.ds(_aligned(chunk * sub, sub), sub), :][0:1]
            outs = []
            for h in range(HG_HEADS):
                sl = slice(h * HG_DK, (h + 1) * HG_DK)
                st = st_ref[d, h]
                outs.append(_dot_nt(q_in[:, sl], st.astype(BF16)))
                st_ref[d, h] = st * dec[:, sl] + _dot_tn(v16[:, sl], k_end[:, sl])
            o_refs[d][pl.ds(r0, c), :] += jnp.concatenate(outs, axis=1)
        return carry

    if has_state:
        st_ref[...] = s0_ref[...]
    else:
        st_ref[...] = jnp.zeros(st_ref.shape, F32)

    def pair_step(i, carry):
        pair_block(i)
        return carry

    lax.fori_loop(0, n_blocks, pair_step, 0)
    lax.fori_loop(0, n_chunks, state_step, 0, unroll=HG_STATE_UNROLL)

    for h in range(HG_HEADS):
        sl = slice(h * HG_DV, (h + 1) * HG_DV)
        o = of_ref[:, sl] + ob_ref[:, sl]
        o_ref[:, sl] = (_rms(o) * gain_ref[:, sl] * _silu(g_ref[:, sl])).astype(BF16)
    if not has_state:
        sout_ref[...] = st_ref[...]


def _hgrn(ymix, lb, gain, batch, seq, s0_t):
    width = HG_HEADS * HG_DK
    has_state = s0_t is not None
    st_shape = (2, HG_HEADS, HG_DV, HG_DK)
    in_specs = [pl.BlockSpec((seq, width), functools.partial(lambda b, col: (b, col), col=col))
                for col in (COL_HQ, COL_HFF, COL_HFB, COL_HI, COL_HG)]
    in_specs += [pl.BlockSpec((2, width), lambda b: (0, 0)), pl.BlockSpec((1, width), lambda b: (0, 0))]
    args = [ymix] * 5 + [lb, gain]
    out_specs = [pl.BlockSpec((seq, width), lambda b: (b, 0))]
    out_shape = [jax.ShapeDtypeStruct((M_ROWS, width), BF16)]
    if has_state:
        in_specs.append(pl.BlockSpec((None,) + st_shape, lambda b: (b, 0, 0, 0, 0)))
        args.append(s0_t)
    else:
        out_specs.append(pl.BlockSpec((None,) + st_shape, lambda b: (b, 0, 0, 0, 0)))
        out_shape.append(jax.ShapeDtypeStruct((batch,) + st_shape, F32))
    return pl.pallas_call(
        functools.partial(_hgrn_kernel, seq=seq, has_state=has_state),
        grid=(batch,),
        in_specs=in_specs,
        out_specs=out_specs,
        out_shape=out_shape,
        scratch_shapes=[pltpu.VMEM((seq, width), F32), pltpu.VMEM((seq, width), F32),
                        pltpu.VMEM((2, seq, width), BF16), pltpu.VMEM((2, seq, width), BF16),
                        pltpu.VMEM((2, seq // HG_CHUNK * HG_DEC_ROWS, width), F32),
                        pltpu.VMEM(st_shape, F32)],
        compiler_params=_params("parallel"),
        name="hgrn_lat" if has_state else "hgrn_ctx",
    )(*args)


def _merge_kernel(oa_ref, ob_ref, oc_ref, od_ref, gt_ref, wb_ref, wo_ref, x_ref, g1_ref, out_ref):
    acc = None
    for n, o_ref in enumerate((oa_ref, ob_ref, oc_ref, od_ref)):
        bo = _dot(o_ref[...], wb_ref[n])
        term = gt_ref[:, n * D_MODEL:(n + 1) * D_MODEL] * bo
        acc = term if acc is None else acc + term
    out_ref[...] = x_ref[...] + g1_ref[...] * _dot(acc.astype(BF16), wo_ref[...])


def _merge(branches, gates, w_branch, w_out, x, mod, latent):
    tm = 512
    row = _mod_row(latent, tm)
    tile = lambda w: pl.BlockSpec((tm, w), lambda i: (i, 0))
    return pl.pallas_call(
        _merge_kernel,
        grid=(M_ROWS // tm,),
        in_specs=[tile(BRANCH_W)] * N_BRANCH + [
            tile(GATE_W),
            pl.BlockSpec((N_BRANCH, BRANCH_W, D_MODEL), lambda i: (0, 0, 0)),
            pl.BlockSpec((D_MODEL, D_MODEL), lambda i: (0, 0)),
            tile(D_MODEL),
            pl.BlockSpec((None, None, 1, D_MODEL), lambda i: (row(i), 2, 0, 0)),
        ],
        out_specs=tile(D_MODEL),
        out_shape=jax.ShapeDtypeStruct((M_ROWS, D_MODEL), F32),
        compiler_params=_params("parallel"),
        name="merge_lat" if latent else "merge_ctx",
    )(*branches, gates, w_branch, w_out, x, mod)


FFN_CHUNK = FFN_HIDDEN // 2
FFN_STEPS = FFN_HIDDEN // FFN_CHUNK
assert FFN_CHUNK % LANE == 0 and FFN_STEPS == 2


def _ffn_kernel(x_ref, sc_ref, sh_ref, g2_ref, wa_ref, wg_ref, wo_ref, out_ref, h_ref, acc_ref):
    j = pl.program_id(1)

    @pl.when(j == 0)
    def _():
        h = _rms(x_ref[...]) * (1.0 + sc_ref[...]) + sh_ref[...]
        h_ref[...] = h.astype(BF16)

    h = h_ref[...]
    act = _silu(_dot(h, wg_ref[...])) * _dot(h, wa_ref[...])
    y = _dot(act.astype(BF16), wo_ref[...])

    @pl.when(j == 0)
    def _():
        acc_ref[...] = y

    @pl.when(j == FFN_STEPS - 1)
    def _():
        out_ref[...] = x_ref[...] + g2_ref[...] * (acc_ref[...] + y)


def _ffn(x, mod, w_in, w_out, latent):
    tm = 512
    row = _mod_row(latent, tm)
    modspec = lambda which: pl.BlockSpec((None, None, 1, D_MODEL), lambda i, j: (row(i), which, 0, 0))
    return pl.pallas_call(
        _ffn_kernel,
        grid=(M_ROWS // tm, FFN_STEPS),
        in_specs=[
            pl.BlockSpec((tm, D_MODEL), lambda i, j: (i, 0)),
            modspec(4), modspec(3), modspec(5),
            pl.BlockSpec((D_MODEL, FFN_CHUNK), lambda i, j: (0, j)),
            pl.BlockSpec((D_MODEL, FFN_CHUNK), lambda i, j: (0, FFN_STEPS + j)),
            pl.BlockSpec((FFN_CHUNK, D_MODEL), lambda i, j: (j, 0)),
        ],
        out_specs=pl.BlockSpec((tm, D_MODEL), lambda i, j: (i, 0)),
        out_shape=jax.ShapeDtypeStruct((M_ROWS, D_MODEL), F32),
        scratch_shapes=[pltpu.VMEM((tm, D_MODEL), BF16), pltpu.VMEM((tm, D_MODEL), F32)],
        compiler_params=_params("parallel", "arbitrary"),
        name="ffn_lat" if latent else "ffn_ctx",
    )(x, mod, mod, mod, w_in, w_in, w_out)


def _final_kernel(x_ref, g_ref, o_ref):
    o_ref[...] = _rms(x_ref[...]) * g_ref[...]


def _final_norm(x, gain):
    tm = 512
    return pl.pallas_call(
        _final_kernel,
        grid=(M_ROWS // tm,),
        in_specs=[pl.BlockSpec((tm, D_MODEL), lambda i: (i, 0)), pl.BlockSpec((1, D_MODEL), lambda i: (0, 0))],
        out_specs=pl.BlockSpec((tm, D_MODEL), lambda i: (i, 0)),
        out_shape=jax.ShapeDtypeStruct((M_ROWS, D_MODEL), F32),
        compiler_params=_params("parallel"),
        name="final_norm",
    )(x, gain.reshape(1, D_MODEL))


def _rope_tables():
    t = jnp.arange(DEC_SEQ)
    row = (t // GRID_W).astype(F32)[:, None]
    col = (t % GRID_W).astype(F32)[:, None]

    def angles(rot_dim):
        n_freq = rot_dim // 4
        inv_freq = ROPE_THETA ** (-jnp.arange(n_freq, dtype=F32) / n_freq)
        return jnp.concatenate([row * inv_freq, col * inv_freq], axis=-1)

    def expand(ang):
        cos = jnp.repeat(jnp.cos(ang), 2, axis=-1)
        sin = jnp.repeat(jnp.sin(ang), 2, axis=-1)
        even = (jnp.arange(cos.shape[-1]) % 2 == 0)[None, :]
        return cos, jnp.where(even, -sin, 0.0), jnp.where(even, 0.0, sin)

    gq = [jnp.concatenate([a, a], axis=-1) for a in expand(angles(GQ_HD))]
    pad_l = LANE - ML_NOPE - ML_ROPE
    ml = []
    for idx, a in enumerate(expand(angles(ML_ROPE))):
        fill = 1.0 if idx == 0 else 0.0
        ml.append(jnp.concatenate([jnp.full((DEC_SEQ, ML_NOPE), fill, F32), a,
                                   jnp.full((DEC_SEQ, pad_l), fill, F32)], axis=-1))
    return {"gq": jnp.stack(gq), "ml": jnp.stack(ml)}


def _layer_weights(l, w_in, gq_q_gain, gq_k_gain, ml_q_a_gain, ml_kv_a_gain, ml_w_q_b, ml_w_kv_b,
                   w_branch, w_out, w_ffn_in, w_ffn_out, hg_gain, avg):
    w = w_in[l]
    zeros = lambda n: jnp.zeros((D_MODEL, n), F32)
    w_mix = jnp.concatenate([
        w[:, 0:3072], w[:, 3328:4864], w[:, 4864:5120], w[:, 3072:3328], w[:, 5120:5248],
        zeros(ML_NOPE), w[:, 5248:5280], zeros(LANE - ML_NOPE - ML_ROPE)], axis=1).astype(BF16)
    assert w_mix.shape[1] == MIX_W
    w_gate = w[:, 5280:].astype(BF16)
    qb = ml_w_q_b[l].reshape(ML_Q_RANK, ML_HEADS, ML_NOPE + ML_ROPE)
    qb = jnp.pad(qb, ((0, 0), (0, 0), (0, LANE - ML_NOPE - ML_ROPE))).reshape(ML_Q_RANK, ML_HEADS * LANE)
    kvb = ml_w_kv_b[l].reshape(ML_KV_RANK, ML_HEADS, ML_NOPE + ML_V)
    wk = jnp.pad(kvb[:, :, :ML_NOPE], ((0, 0), (0, 0), (0, LANE - ML_NOPE))).reshape(ML_KV_RANK, ML_HEADS * LANE)
    wv = kvb[:, :, ML_NOPE:].reshape(ML_KV_RANK, ML_HEADS * ML_V)
    return {
        "w_mix": w_mix, "w_gate": w_gate,
        "gq_q_gain": jnp.tile(gq_q_gain[l], GQ_HEADS).reshape(1, -1),
        "gq_k_gain": jnp.tile(gq_k_gain[l], GQ_KV_HEADS).reshape(1, -1),
        "ml_q_gain": ml_q_a_gain[l].reshape(1, -1), "ml_kv_gain": ml_kv_a_gain[l].reshape(1, -1),
        "w_qb": qb.astype(BF16), "w_k": wk.astype(BF16), "w_v": wv.astype(BF16),
        "w_branch": w_branch[l].astype(BF16), "w_out": w_out[l].astype(BF16),
        "w_ffn_in": w_ffn_in[l].astype(BF16), "w_ffn_out": w_ffn_out[l].astype(BF16),
        "hg_gain": jnp.tile(hg_gain[l], HG_HEADS).reshape(1, -1), "avg": avg,
    }


def _seg(arr, width, col, rows, stride, off):
    return (arr, width, col, rows, stride, off)


def kernel(x_prompt, x_sample, state_hgrn, cache_gqa_k, cache_gqa_v, cache_na_k, cache_na_v, cache_mla_ckv, cache_mla_krope, c, c_ctx, w_ada, b_ada, w_in, hg_lb_logits, hg_gain, gq_q_gain, gq_k_gain, na_rpb, ml_q_a_gain, ml_kv_a_gain, ml_w_q_b, ml_w_kv_b, w_branch, w_out, w_ffn_in, w_ffn_out, final_gain):
    cond8 = jnp.concatenate([c_ctx[None, :], c, jnp.zeros((8 - 1 - DEC_BATCH, D_MODEL), F32)], axis=0)
    mods = _ada(cond8, w_ada, b_ada)

    lb = jnp.cumsum(jax.nn.softmax(hg_lb_logits.astype(F32), axis=0), axis=0)
    lb = lb - lb[:1]
    avg = jnp.asarray(np.kron(np.eye(512 // GQ_HD), np.full((GQ_HD, GQ_HD), 1.0 / GQ_HD)), BF16)
    tabs = _rope_tables()

    gqk_c = cache_gqa_k.reshape(DEC_BATCH * DEPTH * PAST_LEN, GQ_KV_HEADS * GQ_HD)
    gqv_c = cache_gqa_v.reshape(DEC_BATCH * DEPTH * PAST_LEN, GQ_KV_HEADS * GQ_HD)
    nak_c = cache_na_k.reshape(DEC_BATCH * DEPTH * PAST_LEN, NA_HEADS * NA_HD)
    nav_c = cache_na_v.reshape(DEC_BATCH * DEPTH * PAST_LEN, NA_HEADS * NA_HD)
    mckv_c = cache_mla_ckv.reshape(DEC_BATCH * DEPTH * PAST_LEN, ML_KV_RANK)
    mkr_c = jnp.pad(cache_mla_krope.reshape(DEC_BATCH * DEPTH * PAST_LEN, ML_ROPE),
                    ((0, 0), (ML_NOPE, LANE - ML_NOPE - ML_ROPE)))

    xp = x_prompt.reshape(M_ROWS, D_MODEL)
    xs = x_sample.reshape(M_ROWS, D_MODEL)
    new = []
    for l in range(DEPTH):
        lw = _layer_weights(l, w_in, gq_q_gain, gq_k_gain, ml_q_a_gain, ml_kv_a_gain, ml_w_q_b,
                            ml_w_kv_b, w_branch, w_out, w_ffn_in, w_ffn_out, hg_gain, avg)
        mod = mods[l]

        ymix = _inproj(xp, mod, lw["w_mix"], False, 1792, "inproj_mix_ctx")
        gates = _inproj(xp, mod, lw["w_gate"], False, 2048, "inproj_gate_ctx", gate=True)
        qb, kb, qd, ckv, kd, vd = _prep(ymix, lw, tabs, False)
        out_a, st = _hgrn(ymix, lb[l], lw["hg_gain"], BATCH, SEQ, None)
        out_b = _attention((qb, 512, 0, 0),
                           [(_seg(kb, LANE, 0, SEQ, 1, 0), _seg(ymix, LANE, COL_GV, SEQ, 1, 0))],
                           _plan_gqa(), GQ_SCALE, BATCH, SEQ, "gqa_ctx")
        out_c = _attention((ymix, 512, COL_NQ, 0),
                           [(_seg(ymix, 512, COL_NK, SEQ, 1, 0), _seg(ymix, 512, COL_NV, SEQ, 1, 0))],
                           _plan_na(), NA_SCALE, BATCH, SEQ, "na_ctx")
        out_d = _attention((qd, ML_HEADS * LANE, 0, 0),
                           [(_seg(kd, ML_HEADS * LANE, 0, SEQ, 1, 0), _seg(vd, 512, 0, SEQ, 1, 0))],
                           _plan_mla(), ML_SCALE, BATCH, SEQ, "mla_ctx")
        xp = _merge((out_a, out_b, out_c, out_d), gates, lw["w_branch"], lw["w_out"], xp, mod, False)
        xp = _ffn(xp, mod, lw["w_ffn_in"], lw["w_ffn_out"], False)
        new.append((
            jnp.swapaxes(st, -1, -2),
            kb.reshape(BATCH, SEQ, GQ_KV_HEADS, GQ_HD),
            ymix[:, COL_GV * LANE:(COL_GV + 1) * LANE].reshape(BATCH, SEQ, GQ_KV_HEADS, GQ_HD),
            ymix[:, COL_NK * 512:(COL_NK + 1) * 512].reshape(BATCH, SEQ, NA_HEADS, NA_HD),
            ymix[:, COL_NV * 512:(COL_NV + 1) * 512].reshape(BATCH, SEQ, NA_HEADS, NA_HD),
            ckv.reshape(BATCH, SEQ, ML_KV_RANK),
            ymix[:, COL_MKR * LANE + ML_NOPE:COL_MKR * LANE + ML_NOPE + ML_ROPE].reshape(BATCH, SEQ, ML_ROPE),
        ))

        ymix = _inproj(xs, mod, lw["w_mix"], True, 1792, "inproj_mix_lat")
        gates = _inproj(xs, mod, lw["w_gate"], True, 2048, "inproj_gate_lat", gate=True)
        qb, kb, qd, ckv, kd, vd = _prep(ymix, lw, tabs, True)
        kd_c, vd_c = _mla_cache(mckv_c, mkr_c, lw, l)
        s0_t = jnp.swapaxes(state_hgrn[:, l], -1, -2)
        out_a, = _hgrn(ymix, lb[l], lw["hg_gain"], DEC_BATCH, DEC_SEQ, s0_t)
        out_b = _attention((qb, 512, 0, 0),
                           [(_seg(gqk_c, LANE, 0, PAST_LEN, DEPTH, l), _seg(gqv_c, LANE, 0, PAST_LEN, DEPTH, l)),
                            (_seg(kb, LANE, 0, DEC_SEQ, 1, 0), _seg(ymix, LANE, COL_GV, DEC_SEQ, 1, 0))],
                           _plan_gqa(), GQ_SCALE, DEC_BATCH, DEC_SEQ, "gqa_lat")
        out_c = _na_latent(ymix, nak_c, nav_c, _na_bias_tables(na_rpb[l]), l)
        out_d = _attention((qd, ML_HEADS * LANE, 0, 0),
                           [(_seg(kd_c, ML_HEADS * LANE, 0, PAST_LEN, 1, 0), _seg(vd_c, 512, 0, PAST_LEN, 1, 0)),
                            (_seg(kd, ML_HEADS * LANE, 0, DEC_SEQ, 1, 0), _seg(vd, 512, 0, DEC_SEQ, 1, 0))],
                           _plan_mla(), ML_SCALE, DEC_BATCH, DEC_SEQ, "mla_lat")
        xs = _merge((out_a, out_b, out_c, out_d), gates, lw["w_branch"], lw["w_out"], xs, mod, True)
        xs = _ffn(xs, mod, lw["w_ffn_in"], lw["w_ffn_out"], True)

    y_prompt = _final_norm(xp, final_gain).reshape(BATCH, SEQ, D_MODEL)
    y_sample = _final_norm(xs, final_gain).reshape(DEC_BATCH, DEC_SEQ, D_MODEL)
    stacked = tuple(jnp.stack([n[i] for n in new], axis=1) for i in range(7))
    return (y_prompt, y_sample) + stacked
```

```python
import functools

import numpy as np
import jax
import jax.numpy as jnp
from jax import lax
from jax.experimental import pallas as pl
from jax.experimental.pallas import tpu as pltpu

F32 = jnp.float32
BF16 = jnp.bfloat16

D_MODEL = 1024
BATCH = 16
SEQ = 256
DEPTH = 2
DEC_BATCH = 4
DEC_SEQ = 1024
PAST_LEN = 512
GRID_W = 64
EPS = 1e-6
ROPE_THETA = 10000.0
N_BRANCH = 4
BRANCH_W = 512
HG_HEADS = 4
HG_DK = 128
HG_DV = 128
GQ_HEADS = 8
GQ_KV_HEADS = 2
GQ_HD = 64
NA_HEADS = 8
NA_HD = 64
NA_WIN_R = 8
NA_WIN_C = 16
ML_HEADS = 8
ML_NOPE = 64
ML_ROPE = 32
ML_V = 64
ML_Q_RANK = 256
ML_KV_RANK = 128
FFN_HIDDEN = 2816
GQ_SCALE = GQ_HD ** -0.5
NA_SCALE = NA_HD ** -0.5
ML_SCALE = (ML_NOPE + ML_ROPE) ** -0.5

M_ROWS = BATCH * SEQ
assert M_ROWS == DEC_BATCH * DEC_SEQ

LANE = 128
HALF = 64
MIX_W = 5376
IN_MIX = 5280
GATE_W = N_BRANCH * D_MODEL
HG_CHUNK = 16
NEG = -1e30
LOG2E = 1.4426950408889634
VMEM_LIMIT = 56 * 1024 * 1024

COL_HQ, COL_HFF, COL_HFB, COL_HI, COL_HG, COL_GQ, COL_NQ, COL_NK, COL_NV = range(9)
COL_MQA = 18
COL_GK, COL_GV, COL_MKVA, COL_MKR = 38, 39, 40, 41


def _dot(a, b):
    return jnp.dot(a, b, preferred_element_type=F32)


def _dot_nt(a, b):
    return lax.dot_general(a, b, (((1,), (1,)), ((), ())), preferred_element_type=F32)


def _dot_tn(a, b):
    return lax.dot_general(a, b, (((0,), (0,)), ((), ())), preferred_element_type=F32)


def _split3(x):
    x1 = x.astype(BF16)
    r1 = x - x1.astype(F32)
    x2 = r1.astype(BF16)
    x3 = (r1 - x2.astype(F32)).astype(BF16)
    return x1, x2, x3


def _dot_exact_lhs(a_bf16, x):
    x1, x2, x3 = _split3(x)
    return (_dot(a_bf16, x3) + _dot(a_bf16, x2)) + _dot(a_bf16, x1)


def _dot_exact_rhs(x, b_bf16):
    x1, x2, x3 = _split3(x)
    return (_dot(x3, b_bf16) + _dot(x2, b_bf16)) + _dot(x1, b_bf16)


def _rms(x):
    return x * lax.rsqrt(jnp.mean(x * x, axis=-1, keepdims=True) + EPS)


def _silu(x):
    return x * jax.nn.sigmoid(x)


def _aligned(x, m):
    return x if isinstance(x, int) else pl.multiple_of(x, m)


def _params(*sem):
    return pltpu.CompilerParams(dimension_semantics=sem, vmem_limit_bytes=VMEM_LIMIT)


def _mod_row(latent, tm):
    if latent:
        return lambda i: 1 + (i * tm) // DEC_SEQ
    return lambda i: 0


def _ada_kernel(c_ref, w_ref, b_ref, o_ref):
    c = c_ref[...]
    o_ref[...] = _dot(_silu(c).astype(BF16), w_ref[...].astype(BF16)) + b_ref[...]


def _ada(cond8, w_ada, b_ada):
    tn = 1536
    out = pl.pallas_call(
        _ada_kernel,
        grid=(DEPTH, 6 * D_MODEL // tn),
        in_specs=[
            pl.BlockSpec((8, D_MODEL), lambda l, j: (0, 0)),
            pl.BlockSpec((None, D_MODEL, tn), lambda l, j: (l, 0, j)),
            pl.BlockSpec((None, 1, tn), lambda l, j: (l, 0, j)),
        ],
        out_specs=pl.BlockSpec((None, 8, tn), lambda l, j: (l, 0, j)),
        out_shape=jax.ShapeDtypeStruct((DEPTH, 8, 6 * D_MODEL), F32),
        compiler_params=_params("parallel", "parallel"),
        name="ada",
    )(cond8, w_ada, b_ada.reshape(DEPTH, 1, 6 * D_MODEL))
    return out.reshape(DEPTH, 8, 6, 1, D_MODEL)


def _inproj_kernel(x_ref, sc_ref, sh_ref, w_ref, o_ref, h_ref, *, gate):
    @pl.when(pl.program_id(1) == 0)
    def _():
        h = _rms(x_ref[...]) * (1.0 + sc_ref[...]) + sh_ref[...]
        h_ref[...] = h.astype(BF16)

    y = _dot(h_ref[...], w_ref[...])
    o_ref[...] = jax.nn.sigmoid(y).astype(o_ref.dtype) if gate else y


def _inproj(x, mod, w, latent, tn, name, gate=False):
    tm = 1024
    n = w.shape[1]
    row = _mod_row(latent, tm)
    return pl.pallas_call(
        functools.partial(_inproj_kernel, gate=gate),
        grid=(M_ROWS // tm, n // tn),
        in_specs=[
            pl.BlockSpec((tm, D_MODEL), lambda i, j: (i, 0)),
            pl.BlockSpec((None, None, 1, D_MODEL), lambda i, j: (row(i), 1, 0, 0)),
            pl.BlockSpec((None, None, 1, D_MODEL), lambda i, j: (row(i), 0, 0, 0)),
            pl.BlockSpec((D_MODEL, tn), lambda i, j: (0, j)),
        ],
        out_specs=pl.BlockSpec((tm, tn), lambda i, j: (i, j)),
        out_shape=jax.ShapeDtypeStruct((M_ROWS, n), BF16 if gate else F32),
        scratch_shapes=[pltpu.VMEM((tm, D_MODEL), BF16)],
        compiler_params=_params("parallel", "arbitrary"),
        name=name,
    )(x, mod, mod, w)


MIX_PIECE = 256
MIX_TILE = 1792
MIX_PIECES = MIX_TILE // MIX_PIECE


def _mix_source(t):
    return jnp.where(t < 12, t, jnp.where(t < 18, t + 1, jnp.where(t == 18, 19, jnp.where(t == 19, 12, 20))))


def _inproj_mix_kernel(*refs):
    x_ref, sc_ref, sh_ref = refs[:3]
    w_refs = refs[3:3 + MIX_PIECES]
    o_ref, h_ref = refs[3 + MIX_PIECES:]

    @pl.when(pl.program_id(1) == 0)
    def _():
        h = _rms(x_ref[...]) * (1.0 + sc_ref[...]) + sh_ref[...]
        h_ref[...] = h.astype(BF16)

    h = h_ref[...]
    for p, w_ref in enumerate(w_refs):
        o_ref[:, p * MIX_PIECE:(p + 1) * MIX_PIECE] = _dot(h, w_ref[...].astype(BF16))


def _inproj_mix(x, mod, w_in, layer, latent):
    tm = 1024
    row = _mod_row(latent, tm)
    piece = lambda p: pl.BlockSpec((None, D_MODEL, MIX_PIECE),
                                   lambda i, j: (layer, 0, _mix_source(j * MIX_PIECES + p)))
    return pl.pallas_call(
        _inproj_mix_kernel,
        grid=(M_ROWS // tm, MIX_W // MIX_TILE),
        in_specs=[
            pl.BlockSpec((tm, D_MODEL), lambda i, j: (i, 0)),
            pl.BlockSpec((None, None, 1, D_MODEL), lambda i, j: (row(i), 1, 0, 0)),
            pl.BlockSpec((None, None, 1, D_MODEL), lambda i, j: (row(i), 0, 0, 0)),
        ] + [piece(p) for p in range(MIX_PIECES)],
        out_specs=pl.BlockSpec((tm, MIX_TILE), lambda i, j: (i, j)),
        out_shape=jax.ShapeDtypeStruct((M_ROWS, MIX_W), F32),
        scratch_shapes=[pltpu.VMEM((tm, D_MODEL), BF16)],
        compiler_params=_params("parallel", "arbitrary"),
        name="inproj_mix_lat" if latent else "inproj_mix_ctx",
    )(x, mod, mod, *([w_in] * MIX_PIECES))


def _rope(x, tab_ref, reps):
    w = x.shape[1]
    c = jnp.concatenate([tab_ref[0]] * reps, axis=1) if reps > 1 else tab_ref[0]
    se = jnp.concatenate([tab_ref[1]] * reps, axis=1) if reps > 1 else tab_ref[1]
    so = jnp.concatenate([tab_ref[2]] * reps, axis=1) if reps > 1 else tab_ref[2]
    return x * c + pltpu.roll(x, w - 1, 1) * se + pltpu.roll(x, 1, 1) * so


def _head_rms(x, gain, avg_bf16):
    ms = _dot_exact_rhs(x * x, avg_bf16)
    return x * lax.rsqrt(ms + EPS) * gain


def _prep_kernel(*refs, rope):
    (gq_ref, gk_ref, mqa_ref, mkva_ref, mkr_ref, gqg_ref, gkg_ref, mqg_ref, mkvg_ref,
     wqb_ref, wk_ref, wv_ref, avg_ref) = refs[:13]
    if rope:
        gtab_ref, mtab_ref = refs[13:15]
        outs = refs[15:]
    else:
        outs = refs[13:]
    qb_ref, kb_ref, qd_ref, ckv_ref, kd_ref, vd_ref = outs

    q = _head_rms(gq_ref[...], gqg_ref[...], avg_ref[...])
    k = _head_rms(gk_ref[...], gkg_ref[...], avg_ref[0:LANE, 0:LANE])
    qd = _dot((_rms(mqa_ref[...]) * mqg_ref[...]).astype(BF16), wqb_ref[...])
    ckv = _rms(mkva_ref[...]) * mkvg_ref[...]
    lane = lax.broadcasted_iota(jnp.int32, (1, LANE), 1)
    kr = jnp.where(lane < ML_ROPE, mkr_ref[...], 0.0)
    if rope:
        q = _rope(q, gtab_ref, GQ_HEADS * GQ_HD // LANE)
        k = _rope(k, gtab_ref, 1)
        qd = _rope(qd, mtab_ref, ML_HEADS)
        kr = _rope(kr, mtab_ref, 1)
    qb_ref[...] = q.astype(BF16)
    kb_ref[...] = k
    qd_ref[...] = qd.astype(BF16)
    ckv_ref[...] = ckv
    cb = ckv.astype(BF16)
    kd_ref[...] = (_dot(cb, wk_ref[...]) + jnp.concatenate([kr] * ML_HEADS, axis=1)).astype(BF16)
    vd_ref[...] = _dot(cb, wv_ref[...]).astype(BF16)


def _prep(ymix, lw, tabs, latent):
    tm = 512
    const = lambda i: (0, 0)
    in_specs = [
        pl.BlockSpec((tm, 512), lambda i: (i, COL_GQ)),
        pl.BlockSpec((tm, LANE), lambda i: (i, COL_GK)),
        pl.BlockSpec((tm, 256), lambda i: (i, COL_MQA)),
        pl.BlockSpec((tm, LANE), lambda i: (i, COL_MKVA)),
        pl.BlockSpec((tm, LANE), lambda i: (i, COL_MKR)),
        pl.BlockSpec((1, 512), const),
        pl.BlockSpec((1, LANE), const),
        pl.BlockSpec((1, 256), const),
        pl.BlockSpec((1, LANE), const),
        pl.BlockSpec((ML_Q_RANK, ML_HEADS * LANE), const),
        pl.BlockSpec((ML_KV_RANK, ML_HEADS * LANE), const),
        pl.BlockSpec((ML_KV_RANK, ML_HEADS * ML_V), const),
        pl.BlockSpec((512, 512), const),
    ]
    args = [ymix, ymix, ymix, ymix, ymix, lw["gq_q_gain"], lw["gq_k_gain"], lw["ml_q_gain"],
            lw["ml_kv_gain"], lw["w_qb"], lw["w_k"], lw["w_v"], lw["avg"]]
    if latent:
        per = DEC_SEQ // tm
        in_specs += [pl.BlockSpec((3, tm, LANE), lambda i: (0, i % per, 0))] * 2
        args += [tabs["gq"], tabs["ml"]]
    widths = (512, LANE, ML_HEADS * LANE, LANE, ML_HEADS * LANE, ML_HEADS * ML_V)
    return pl.pallas_call(
        functools.partial(_prep_kernel, rope=latent),
        grid=(M_ROWS // tm,),
        in_specs=in_specs,
        out_specs=[pl.BlockSpec((tm, w), lambda i: (i, 0)) for w in widths],
        out_shape=[jax.ShapeDtypeStruct((M_ROWS, w), dt)
                   for w, dt in zip(widths, (BF16, F32, BF16, F32, BF16, BF16))],
        compiler_params=_params("parallel"),
        name="prep_lat" if latent else "prep_ctx",
    )(*args)


def _mla_cache_kernel(ckv_ref, kr_ref, wk_ref, wv_ref, kd_ref, vd_ref):
    cb = ckv_ref[...].astype(BF16)
    kd_ref[...] = (_dot(cb, wk_ref[...]) + jnp.concatenate([kr_ref[...]] * ML_HEADS, axis=1)).astype(BF16)
    vd_ref[...] = _dot(cb, wv_ref[...]).astype(BF16)


def _mla_cache(ckv, kr_blk, lw, layer):
    rows = DEC_BATCH * PAST_LEN
    tm = PAST_LEN
    const = lambda i: (0, 0)
    return pl.pallas_call(
        _mla_cache_kernel,
        grid=(DEC_BATCH,),
        in_specs=[
            pl.BlockSpec((tm, LANE), lambda i: (i * DEPTH + layer, 0)),
            pl.BlockSpec((tm, LANE), lambda i: (i * DEPTH + layer, 0)),
            pl.BlockSpec((ML_KV_RANK, ML_HEADS * LANE), const),
            pl.BlockSpec((ML_KV_RANK, ML_HEADS * ML_V), const),
        ],
        out_specs=[pl.BlockSpec((tm, ML_HEADS * LANE), lambda i: (i, 0)),
                   pl.BlockSpec((tm, ML_HEADS * ML_V), lambda i: (i, 0))],
        out_shape=[jax.ShapeDtypeStruct((rows, ML_HEADS * LANE), BF16),
                   jax.ShapeDtypeStruct((rows, ML_HEADS * ML_V), BF16)],
        compiler_params=_params("parallel"),
        name="mla_cache",
    )(ckv, kr_blk, lw["w_k"], lw["w_v"])


def _softmax_pv(scores, values):
    m = scores[0].max(axis=-1, keepdims=True)
    for s in scores[1:]:
        m = jnp.maximum(m, s.max(axis=-1, keepdims=True))
    den = None
    out = None
    for s, v in zip(scores, values):
        p = jnp.exp(s - m)
        d = p.sum(axis=-1, keepdims=True)
        o = _dot(p.astype(BF16), v)
        den = d if den is None else den + d
        out = o if out is None else out + o
    return out / den


def _lane_halves():
    lane = lax.broadcasted_iota(jnp.int32, (1, LANE), 1)
    return lane < HALF, lane >= HALF


def _attn_kernel(*refs, plan, nseg, scale, fold_scale):
    q_ref = refs[0]
    k_refs = [refs[1 + 2 * i] for i in range(nseg)]
    v_refs = [refs[2 + 2 * i] for i in range(nseg)]
    o_ref = refs[1 + 2 * nseg]
    tq = q_ref.shape[0]
    lo, hi = _lane_halves()
    cache = {}

    def block(kind, seg, blk, swap):
        key = (kind, seg, blk, swap)
        if key not in cache:
            ref = (k_refs if kind == "k" else v_refs)[seg]
            x = ref[:, blk * LANE:(blk + 1) * LANE]
            if swap:
                x = pltpu.roll(x, HALF, 1)
            cache[key] = x.astype(BF16)
        return cache[key]

    parts = {}
    for members, kblk, swap, vblk in plan:
        qs = []
        for qblk, qhalf, _, _ in members:
            q = q_ref[:, qblk * LANE:(qblk + 1) * LANE]
            if fold_scale:
                q = q * scale
            if qhalf is not None:
                q = jnp.where(lo if qhalf == 0 else hi, q, 0.0)
            qs.append(q)
        q = (jnp.concatenate(qs, axis=0) if len(qs) > 1 else qs[0]).astype(BF16)
        scores = [_dot_nt(q, block("k", s, kblk, swap)) for s in range(nseg)]
        if not fold_scale:
            scores = [s * scale for s in scores]
        out = _softmax_pv(scores, [block("v", s, vblk, swap) for s in range(nseg)])
        for n, (_, _, oblk, ohalf) in enumerate(members):
            parts[(oblk, ohalf)] = out[n * tq:(n + 1) * tq]
    for oblk in sorted({key[0] for key in parts}):
        o_ref[:, oblk * LANE:(oblk + 1) * LANE] = jnp.where(lo, parts[(oblk, 0)], parts[(oblk, 1)]).astype(BF16)


def _plan_gqa():
    per_kv = GQ_HEADS // GQ_KV_HEADS
    plan = []
    for g in range(GQ_KV_HEADS):
        for half in range(2):
            heads = [h for h in range(g * per_kv, (g + 1) * per_kv) if h % 2 == half]
            plan.append((tuple((h // 2, half, h // 2, half) for h in heads), 0, half != g, 0))
    return tuple(plan)


def _plan_na():
    return tuple((((j, 0, j, 0), (j, 1, j, 1)), j, False, j) for j in range(NA_HEADS // 2))


def _plan_mla():
    return tuple((((h, None, h // 2, h % 2),), h, False, h // 2) for h in range(ML_HEADS))


def _is_pow2(x):
    return float(np.log2(x)).is_integer()


def _attention(q, segs, plan, scale, batch, tq_total, name):
    tq = min(tq_total, 256)
    per = tq_total // tq
    q_arr, q_w, q_col, q_off = q
    in_specs = [pl.BlockSpec((tq, q_w), lambda b, i: (q_off + b * per + i, q_col))]
    args = [q_arr]
    for k, v in segs:
        for arr, w, col, rows, stride, off in (k, v):
            in_specs.append(pl.BlockSpec(
                (rows, w), functools.partial(lambda b, i, col, stride, off: (off + b * stride, col),
                                             col=col, stride=stride, off=off)))
            args.append(arr)
    out_w = 512
    return pl.pallas_call(
        functools.partial(_attn_kernel, plan=plan, nseg=len(segs), scale=scale, fold_scale=_is_pow2(scale)),
        grid=(batch, per),
        in_specs=in_specs,
        out_specs=pl.BlockSpec((tq, out_w), lambda b, i: (b * per + i, 0)),
        out_shape=jax.ShapeDtypeStruct((M_ROWS, out_w), BF16),
        compiler_params=_params("parallel", "parallel"),
        name=name,
    )(*args)


NA_DR = 2 * NA_WIN_R - 1
NA_DC = 2 * NA_WIN_C - 1
NA_ROWS = DEC_SEQ // GRID_W
NA_QROWS = 4
NA_WROWS = 12
NA_T_RIGHT = NA_DR - 1
NA_T_LEFT = NA_DR
NA_T_NONE = NA_DR + 1
NA_T_SIZE = NA_DR + 2
assert _is_pow2(NA_SCALE)


def _na_kernel(q_ref, kl_ref, vl_ref, kc_ref, vc_ref, tz_ref, o_ref):
    g = pl.program_id(1)
    w0 = (g // 2) * (NA_ROWS - NA_WROWS)
    k0 = pl.multiple_of(w0 * GRID_W, (NA_ROWS - NA_WROWS) * GRID_W)
    win = NA_WROWS * GRID_W
    nq = NA_QROWS * GRID_W
    entry = []
    for a in range(NA_QROWS):
        rq = g * NA_QROWS + a
        start = jnp.clip(rq - NA_WIN_R // 2, 0, NA_ROWS - NA_WIN_R)
        per_pair = []
        for i in range(NA_WROWS // 2):
            kr = w0 + 2 * i
            in_l = jnp.logical_and(kr >= start, kr < start + NA_WIN_R)
            in_r = jnp.logical_and(kr + 1 >= start, kr + 1 < start + NA_WIN_R)
            both = jnp.logical_and(in_l, in_r)
            d_l = kr - rq + NA_WIN_R - 1
            per_pair.append(jnp.where(both, d_l, jnp.where(in_r, NA_T_RIGHT, jnp.where(in_l, NA_T_LEFT, NA_T_NONE))))
        entry.append(per_pair)
    lo, hi = _lane_halves()
    for j in range(NA_HEADS // 2):
        sl = slice(j * LANE, (j + 1) * LANE)
        k_loc = kl_ref[pl.ds(k0, win), sl].astype(BF16)
        v_loc = vl_ref[pl.ds(k0, win), sl].astype(BF16)
        k_ctx = kc_ref[:, sl].astype(BF16)
        v_ctx = vc_ref[:, sl].astype(BF16)
        q_pair = q_ref[:, sl] * NA_SCALE
        q = jnp.concatenate([jnp.where(lo, q_pair, 0.0), jnp.where(hi, q_pair, 0.0)], axis=0).astype(BF16)
        bias = jnp.concatenate(
            [jnp.concatenate([tz_ref[2 * j + p, e] for e in entry[a]], axis=1)
             for p in range(2) for a in range(NA_QROWS)], axis=0)
        s_loc = _dot_nt(q, k_loc) + bias
        s_ctx = _dot_nt(q, k_ctx)
        out = _softmax_pv([s_loc, s_ctx], [v_loc, v_ctx])
        o_ref[:, sl] = jnp.where(lo, out[:nq], out[nq:]).astype(BF16)


def _na_bias_kernel(rpb_ref, sel_ref, neg_ref, o_ref):
    o_ref[...] = _dot_exact_rhs(rpb_ref[...], sel_ref[...]) + neg_ref[...]


def _na_bias_tables(rpb):
    col = np.arange(GRID_W)
    col_start = np.clip(col - NA_WIN_C // 2, 0, GRID_W - NA_WIN_C)
    col_ok = (col[None, :] >= col_start[:, None]) & (col[None, :] < col_start[:, None] + NA_WIN_C)
    dc = col[None, :] - col[:, None] + NA_WIN_C - 1
    kpad = 32
    sel = (dc[None, :, :] == np.arange(kpad)[:, None, None]) & col_ok[None]
    sel = jnp.asarray(sel.reshape(kpad, GRID_W * GRID_W), BF16)
    neg = jnp.asarray(np.where(col_ok, 0.0, NEG).reshape(1, GRID_W * GRID_W), F32)
    rpb2 = jnp.pad(rpb.reshape(NA_HEADS * NA_DR, NA_DC), ((0, 0), (0, kpad - NA_DC)))
    n_rows = NA_HEADS * NA_DR
    full = lambda shape: pl.BlockSpec(shape, lambda i: (0, 0))
    t = pl.pallas_call(
        _na_bias_kernel,
        grid=(1,),
        in_specs=[full((n_rows, kpad)), full((kpad, GRID_W * GRID_W)), full((1, GRID_W * GRID_W))],
        out_specs=full((n_rows, GRID_W * GRID_W)),
        out_shape=jax.ShapeDtypeStruct((n_rows, GRID_W * GRID_W), F32),
        compiler_params=_params("arbitrary"),
        name="na_bias",
    )(rpb2, sel, neg)
    t = t.reshape(NA_HEADS, NA_DR, GRID_W, GRID_W)
    masked = jnp.full((NA_HEADS, 1, GRID_W, GRID_W), NEG, F32)
    first, last = NA_WIN_R // 2 - 1, NA_WIN_R // 2 + NA_WIN_R - 2
    return jnp.concatenate([
        jnp.concatenate([t[:, :-1], t[:, 1:]], axis=-1),
        jnp.concatenate([masked, t[:, first:first + 1]], axis=-1),
        jnp.concatenate([t[:, last:last + 1], masked], axis=-1),
        jnp.concatenate([masked, masked], axis=-1)], axis=1)


def _na_latent(ymix, cache_k, cache_v, tz, layer):
    groups = NA_ROWS // NA_QROWS
    nq = NA_QROWS * GRID_W
    return pl.pallas_call(
        _na_kernel,
        grid=(DEC_BATCH, groups),
        in_specs=[
            pl.BlockSpec((nq, 512), lambda b, g: (b * groups + g, COL_NQ)),
            pl.BlockSpec((DEC_SEQ, 512), lambda b, g: (b, COL_NK)),
            pl.BlockSpec((DEC_SEQ, 512), lambda b, g: (b, COL_NV)),
            pl.BlockSpec((PAST_LEN, 512), lambda b, g: (b * DEPTH + layer, 0)),
            pl.BlockSpec((PAST_LEN, 512), lambda b, g: (b * DEPTH + layer, 0)),
            pl.BlockSpec((NA_HEADS, NA_T_SIZE, GRID_W, LANE), lambda b, g: (0, 0, 0, 0)),
        ],
        out_specs=pl.BlockSpec((nq, 512), lambda b, g: (b * groups + g, 0)),
        out_shape=jax.ShapeDtypeStruct((M_ROWS, 512), BF16),
        compiler_params=_params("parallel", "parallel"),
        name="na_lat",
    )(ymix, ymix, ymix, cache_k, cache_v, tz)


HG_BLOCK = 64
HG_STATE_UNROLL = 8
HG_DEC_ROWS = 8


def _hgrn_kernel(*refs, seq, has_state):
    if has_state:
        (q_ref, ff_ref, fb_ref, v_ref, g_ref, lb_ref, gain_ref, s0_ref,
         o_ref, of_ref, ob_ref, qin_ref, kend_ref, dec_ref, st_ref) = refs
    else:
        (q_ref, ff_ref, fb_ref, v_ref, g_ref, lb_ref, gain_ref,
         o_ref, sout_ref, of_ref, ob_ref, qin_ref, kend_ref, dec_ref, st_ref) = refs
    c = HG_CHUNK
    hc = c // 2
    rb = HG_BLOCK
    per_block = rb // c
    n_chunks = seq // c
    width = HG_HEADS * HG_DK
    sub = HG_DEC_ROWS

    ri = lax.broadcasted_iota(jnp.int32, (rb, rb), 0)
    ci = lax.broadcasted_iota(jnp.int32, (rb, rb), 1)
    same = (ri // c) == (ci // c)
    tri_f = jnp.where(jnp.logical_and(same, ci <= ri), 1.0, 0.0).astype(BF16)
    tri_b = jnp.where(jnp.logical_and(same, ci >= ri), 1.0, 0.0).astype(BF16)
    ones = jnp.ones((HG_DK, HG_DV), BF16)
    rowid = lax.broadcasted_iota(jnp.int32, (c, width), 0)
    rowid_half = lax.broadcasted_iota(jnp.int32, (hc, width), 0)
    o_refs = (of_ref, ob_ref)

    n_blocks = seq // rb

    def block_of(d, step):
        return step if d == 0 else n_blocks - 1 - step

    def pair_block(step):
        for d, (pre_ref, tri) in enumerate(((ff_ref, tri_f), (fb_ref, tri_b))):
            fwd = d == 0
            blk = block_of(d, step)
            r0 = _aligned(blk * rb, rb)
            q_all = q_ref[pl.ds(r0, rb), :] * (HG_DK ** -0.5)
            v_all = v_ref[pl.ds(r0, rb), :]
            lb = lb_ref[d:d + 1, :]
            f = lb + (1.0 - lb) * jax.nn.sigmoid(pre_ref[pl.ds(r0, rb), :])
            k_all = 1.0 - f
            b_all = _dot_exact_lhs(tri, jnp.log(f)) * LOG2E
            src_all = b_all - jnp.log(k_all) * LOG2E
            wide = range(0, hc) if fwd else range(hc, c)
            narrow = range(hc, c) if fwd else range(0, hc)
            far = slice(hc, c) if fwd else slice(0, hc)
            row_far = rowid_half + (hc if fwd else 0)
            blocks, q_in, k_end = [], [], []
            for m in range(per_block):
                rows = slice(m * c, (m + 1) * c)
                q, k, b, src = q_all[rows], k_all[rows], b_all[rows], src_all[rows]
                b_last = b[c - 1:c] if fwd else b[0:1]
                q_in.append(q * jnp.exp2(b))
                k_end.append(k * jnp.exp2(b_last - b))
                dec_row = _aligned((blk * per_block + m) * sub, sub)
                dec_ref[d, pl.ds(dec_row, sub), :] = jnp.broadcast_to(jnp.exp2(b_last), (sub, width))
                for s in wide:
                    p = q * jnp.exp2(b - src[s:s + 1])
                    keep = (rowid >= s) if fwd else (rowid <= s)
                    blocks.append(jnp.where(keep, p, 0.0).astype(BF16))
                q_far, b_far = q[far], b[far]
                halves = []
                for s in narrow:
                    p = q_far * jnp.exp2(b_far - src[s:s + 1])
                    keep = (row_far >= s) if fwd else (row_far <= s)
                    halves.append(jnp.where(keep, p, 0.0))
                for i in range(0, hc, 2):
                    blocks.append(jnp.concatenate(halves[i:i + 2], axis=0).astype(BF16))
            qin_ref[d, pl.ds(r0, rb), :] = jnp.concatenate(q_in, axis=0).astype(BF16)
            kend_ref[d, pl.ds(r0, rb), :] = jnp.concatenate(k_end, axis=0).astype(BF16)
            pairs = jnp.concatenate(blocks, axis=0)
            per_chunk = hc * c + hc * hc
            outs = []
            for h in range(HG_HEADS):
                sl = slice(h * HG_DK, (h + 1) * HG_DK)
                a = _dot(pairs[:, sl], ones)
                chunks = []
                for m in range(per_block):
                    base = m * per_chunk
                    v = v_all[m * c:(m + 1) * c]
                    o = None
                    for n, s in enumerate(wide):
                        term = a[base + n * c:base + (n + 1) * c] * v[s:s + 1, sl]
                        o = term if o is None else o + term
                    o_far = None
                    for n, s in enumerate(narrow):
                        first = base + hc * c + n * hc
                        term = a[first:first + hc] * v[s:s + 1, sl]
                        o_far = term if o_far is None else o_far + term
                    zero = jnp.zeros_like(o_far)
                    chunks.append(o + jnp.concatenate([zero, o_far] if fwd else [o_far, zero], axis=0))
                outs.append(jnp.concatenate(chunks, axis=0))
            o_refs[d][pl.ds(r0, rb), :] = jnp.concatenate(outs, axis=1)

    def state_step(n, carry):
        for d in range(2):
            chunk = n if d == 0 else n_chunks - 1 - n
            r0 = _aligned(chunk * c, c)
            q_in = qin_ref[d, pl.ds(r0, c), :]
            k_end = kend_ref[d, pl.ds(r0, c), :]
            v16 = v_ref[pl.ds(r0, c), :].astype(BF16)
            dec = dec_ref[d, pl.ds(_aligned(chunk * sub, sub), sub), :][0:1]
            outs = []
            for h in range(HG_HEADS):
                sl = slice(h * HG_DK, (h + 1) * HG_DK)
                st = st_ref[d, h]
                outs.append(_dot_nt(q_in[:, sl], st.astype(BF16)))
                st_ref[d, h] = st * dec[:, sl] + _dot_tn(v16[:, sl], k_end[:, sl])
            o_refs[d][pl.ds(r0, c), :] += jnp.concatenate(outs, axis=1)
        return carry

    if has_state:
        st_ref[...] = s0_ref[...]
    else:
        st_ref[...] = jnp.zeros(st_ref.shape, F32)

    def pair_step(i, carry):
        pair_block(i)
        return carry

    lax.fori_loop(0, n_blocks, pair_step, 0)
    lax.fori_loop(0, n_chunks, state_step, 0, unroll=HG_STATE_UNROLL)

    for h in range(HG_HEADS):
        sl = slice(h * HG_DV, (h + 1) * HG_DV)
        o = of_ref[:, sl] + ob_ref[:, sl]
        o_ref[:, sl] = (_rms(o) * gain_ref[:, sl] * _silu(g_ref[:, sl])).astype(BF16)
    if not has_state:
        sout_ref[...] = st_ref[...]


def _hgrn(ymix, lb, gain, batch, seq, s0_t):
    width = HG_HEADS * HG_DK
    has_state = s0_t is not None
    st_shape = (2, HG_HEADS, HG_DV, HG_DK)
    in_specs = [pl.BlockSpec((seq, width), functools.partial(lambda b, col: (b, col), col=col))
                for col in (COL_HQ, COL_HFF, COL_HFB, COL_HI, COL_HG)]
    in_specs += [pl.BlockSpec((2, width), lambda b: (0, 0)), pl.BlockSpec((1, width), lambda b: (0, 0))]
    args = [ymix] * 5 + [lb, gain]
    out_specs = [pl.BlockSpec((seq, width), lambda b: (b, 0))]
    out_shape = [jax.ShapeDtypeStruct((M_ROWS, width), BF16)]
    if has_state:
        in_specs.append(pl.BlockSpec((None,) + st_shape, lambda b: (b, 0, 0, 0, 0)))
        args.append(s0_t)
    else:
        out_specs.append(pl.BlockSpec((None,) + st_shape, lambda b: (b, 0, 0, 0, 0)))
        out_shape.append(jax.ShapeDtypeStruct((batch,) + st_shape, F32))
    return pl.pallas_call(
        functools.partial(_hgrn_kernel, seq=seq, has_state=has_state),
        grid=(batch,),
        in_specs=in_specs,
        out_specs=out_specs,
        out_shape=out_shape,
        scratch_shapes=[pltpu.VMEM((seq, width), F32), pltpu.VMEM((seq, width), F32),
                        pltpu.VMEM((2, seq, width), BF16), pltpu.VMEM((2, seq, width), BF16),
                        pltpu.VMEM((2, seq // HG_CHUNK * HG_DEC_ROWS, width), F32),
                        pltpu.VMEM(st_shape, F32)],
        compiler_params=_params("parallel"),
        name="hgrn_lat" if has_state else "hgrn_ctx",
    )(*args)


def _merge_kernel(oa_ref, ob_ref, oc_ref, od_ref, gt_ref, wb_ref, wo_ref, x_ref, g1_ref, out_ref):
    acc = None
    for n, o_ref in enumerate((oa_ref, ob_ref, oc_ref, od_ref)):
        bo = _dot(o_ref[...], wb_ref[n])
        term = gt_ref[:, n * D_MODEL:(n + 1) * D_MODEL] * bo
        acc = term if acc is None else acc + term
    out_ref[...] = x_ref[...] + g1_ref[...] * _dot(acc.astype(BF16), wo_ref[...])


def _merge(branches, gates, w_branch, w_out, x, mod, latent):
    tm = 512
    row = _mod_row(latent, tm)
    tile = lambda w: pl.BlockSpec((tm, w), lambda i: (i, 0))
    return pl.pallas_call(
        _merge_kernel,
        grid=(M_ROWS // tm,),
        in_specs=[tile(BRANCH_W)] * N_BRANCH + [
            tile(GATE_W),
            pl.BlockSpec((N_BRANCH, BRANCH_W, D_MODEL), lambda i: (0, 0, 0)),
            pl.BlockSpec((D_MODEL, D_MODEL), lambda i: (0, 0)),
            tile(D_MODEL),
            pl.BlockSpec((None, None, 1, D_MODEL), lambda i: (row(i), 2, 0, 0)),
        ],
        out_specs=tile(D_MODEL),
        out_shape=jax.ShapeDtypeStruct((M_ROWS, D_MODEL), F32),
        compiler_params=_params("parallel"),
        name="merge_lat" if latent else "merge_ctx",
    )(*branches, gates, w_branch, w_out, x, mod)


FFN_CHUNK = 256
FFN_STEPS = FFN_HIDDEN // FFN_CHUNK
assert FFN_STEPS * FFN_CHUNK == FFN_HIDDEN


def _ffn_kernel(*refs, final):
    x_ref, sc_ref, sh_ref, g2_ref, wa_ref, wg_ref, wo_ref = refs[:7]
    if final:
        fg_ref, out_ref, h_ref, acc_ref = refs[7:]
    else:
        out_ref, h_ref, acc_ref = refs[7:]
    j = pl.program_id(1)

    @pl.when(j == 0)
    def _():
        h = _rms(x_ref[...]) * (1.0 + sc_ref[...]) + sh_ref[...]
        h_ref[...] = h.astype(BF16)
        acc_ref[...] = jnp.zeros(acc_ref.shape, F32)

    h = h_ref[...]
    act = _silu(_dot(h, wg_ref[...].astype(BF16))) * _dot(h, wa_ref[...].astype(BF16))
    acc_ref[...] += _dot(act.astype(BF16), wo_ref[...].astype(BF16))

    @pl.when(j == FFN_STEPS - 1)
    def _():
        y = x_ref[...] + g2_ref[...] * acc_ref[...]
        out_ref[...] = _rms(y) * fg_ref[...] if final else y


def _ffn(x, mod, w_in, w_out, layer, latent, final_gain):
    tm = 1024
    row = _mod_row(latent, tm)
    modspec = lambda which: pl.BlockSpec((None, None, 1, D_MODEL), lambda i, j: (row(i), which, 0, 0))
    final = final_gain is not None
    in_specs = [
        pl.BlockSpec((tm, D_MODEL), lambda i, j: (i, 0)),
        modspec(4), modspec(3), modspec(5),
        pl.BlockSpec((None, D_MODEL, FFN_CHUNK), lambda i, j: (layer, 0, j)),
        pl.BlockSpec((None, D_MODEL, FFN_CHUNK), lambda i, j: (layer, 0, FFN_STEPS + j)),
        pl.BlockSpec((None, FFN_CHUNK, D_MODEL), lambda i, j: (layer, j, 0)),
    ]
    args = [x, mod, mod, mod, w_in, w_in, w_out]
    if final:
        in_specs.append(pl.BlockSpec((1, D_MODEL), lambda i, j: (0, 0)))
        args.append(final_gain.reshape(1, D_MODEL))
    return pl.pallas_call(
        functools.partial(_ffn_kernel, final=final),
        grid=(M_ROWS // tm, FFN_STEPS),
        in_specs=in_specs,
        out_specs=pl.BlockSpec((tm, D_MODEL), lambda i, j: (i, 0)),
        out_shape=jax.ShapeDtypeStruct((M_ROWS, D_MODEL), F32),
        scratch_shapes=[pltpu.VMEM((tm, D_MODEL), BF16), pltpu.VMEM((tm, D_MODEL), F32)],
        compiler_params=_params("parallel", "arbitrary"),
        name="ffn_lat" if latent else "ffn_ctx",
    )(*args)


def _rope_tables():
    t = jnp.arange(DEC_SEQ)
    row = (t // GRID_W).astype(F32)[:, None]
    col = (t % GRID_W).astype(F32)[:, None]

    def angles(rot_dim):
        n_freq = rot_dim // 4
        inv_freq = ROPE_THETA ** (-jnp.arange(n_freq, dtype=F32) / n_freq)
        return jnp.concatenate([row * inv_freq, col * inv_freq], axis=-1)

    def expand(ang):
        cos = jnp.repeat(jnp.cos(ang), 2, axis=-1)
        sin = jnp.repeat(jnp.sin(ang), 2, axis=-1)
        even = (jnp.arange(cos.shape[-1]) % 2 == 0)[None, :]
        return cos, jnp.where(even, -sin, 0.0), jnp.where(even, 0.0, sin)

    gq = [jnp.concatenate([a, a], axis=-1) for a in expand(angles(GQ_HD))]
    ml = []
    for idx, a in enumerate(expand(angles(ML_ROPE))):
        fill = 1.0 if idx == 0 else 0.0
        ml.append(jnp.concatenate([a, jnp.full((DEC_SEQ, LANE - ML_ROPE), fill, F32)], axis=-1))
    return {"gq": jnp.stack(gq), "ml": jnp.stack(ml)}


def _layer_weights(l, w_in, gq_q_gain, gq_k_gain, ml_q_a_gain, ml_kv_a_gain, ml_w_q_b, ml_w_kv_b,
                   w_branch, w_out, hg_gain, avg):
    w_gate = w_in[l][:, IN_MIX:].astype(BF16)
    pad = LANE - ML_NOPE - ML_ROPE
    qb = ml_w_q_b[l].reshape(ML_Q_RANK, ML_HEADS, ML_NOPE + ML_ROPE)
    qb = jnp.concatenate([qb[:, :, ML_NOPE:], qb[:, :, :ML_NOPE], jnp.zeros((ML_Q_RANK, ML_HEADS, pad), F32)],
                         axis=-1).reshape(ML_Q_RANK, ML_HEADS * LANE)
    kvb = ml_w_kv_b[l].reshape(ML_KV_RANK, ML_HEADS, ML_NOPE + ML_V)
    wk = jnp.pad(kvb[:, :, :ML_NOPE], ((0, 0), (0, 0), (ML_ROPE, pad))).reshape(ML_KV_RANK, ML_HEADS * LANE)
    wv = kvb[:, :, ML_NOPE:].reshape(ML_KV_RANK, ML_HEADS * ML_V)
    return {
        "w_gate": w_gate,
        "gq_q_gain": jnp.tile(gq_q_gain[l], GQ_HEADS).reshape(1, -1),
        "gq_k_gain": jnp.tile(gq_k_gain[l], GQ_KV_HEADS).reshape(1, -1),
        "ml_q_gain": ml_q_a_gain[l].reshape(1, -1), "ml_kv_gain": ml_kv_a_gain[l].reshape(1, -1),
        "w_qb": qb.astype(BF16), "w_k": wk.astype(BF16), "w_v": wv.astype(BF16),
        "w_branch": w_branch[l].astype(BF16), "w_out": w_out[l].astype(BF16),
        "hg_gain": jnp.tile(hg_gain[l], HG_HEADS).reshape(1, -1), "avg": avg,
    }


def _seg(arr, width, col, rows, stride, off):
    return (arr, width, col, rows, stride, off)


def kernel(x_prompt, x_sample, state_hgrn, cache_gqa_k, cache_gqa_v, cache_na_k, cache_na_v, cache_mla_ckv, cache_mla_krope, c, c_ctx, w_ada, b_ada, w_in, hg_lb_logits, hg_gain, gq_q_gain, gq_k_gain, na_rpb, ml_q_a_gain, ml_kv_a_gain, ml_w_q_b, ml_w_kv_b, w_branch, w_out, w_ffn_in, w_ffn_out, final_gain):
    cond8 = jnp.concatenate([c_ctx[None, :], c, jnp.zeros((8 - 1 - DEC_BATCH, D_MODEL), F32)], axis=0)
    mods = _ada(cond8, w_ada, b_ada)

    lb = jnp.cumsum(jax.nn.softmax(hg_lb_logits.astype(F32), axis=0), axis=0)
    lb = lb - lb[:1]
    avg = jnp.asarray(np.kron(np.eye(512 // GQ_HD), np.full((GQ_HD, GQ_HD), 1.0 / GQ_HD)), BF16)
    tabs = _rope_tables()

    gqk_c = cache_gqa_k.reshape(DEC_BATCH * DEPTH * PAST_LEN, GQ_KV_HEADS * GQ_HD)
    gqv_c = cache_gqa_v.reshape(DEC_BATCH * DEPTH * PAST_LEN, GQ_KV_HEADS * GQ_HD)
    nak_c = cache_na_k.reshape(DEC_BATCH * DEPTH * PAST_LEN, NA_HEADS * NA_HD)
    nav_c = cache_na_v.reshape(DEC_BATCH * DEPTH * PAST_LEN, NA_HEADS * NA_HD)
    mckv_c = cache_mla_ckv.reshape(DEC_BATCH * DEPTH * PAST_LEN, ML_KV_RANK)
    mkr_c = jnp.pad(cache_mla_krope.reshape(DEC_BATCH * DEPTH * PAST_LEN, ML_ROPE),
                    ((0, 0), (0, LANE - ML_ROPE)))

    xp = x_prompt.reshape(M_ROWS, D_MODEL)
    xs = x_sample.reshape(M_ROWS, D_MODEL)
    new = []
    for l in range(DEPTH):
        lw = _layer_weights(l, w_in, gq_q_gain, gq_k_gain, ml_q_a_gain, ml_kv_a_gain, ml_w_q_b,
                            ml_w_kv_b, w_branch, w_out, hg_gain, avg)
        last = l == DEPTH - 1
        mod = mods[l]

        ymix = _inproj_mix(xp, mod, w_in, l, False)
        gates = _inproj(xp, mod, lw["w_gate"], False, 2048, "inproj_gate_ctx", gate=True)
        qb, kb, qd, ckv, kd, vd = _prep(ymix, lw, tabs, False)
        out_a, st = _hgrn(ymix, lb[l], lw["hg_gain"], BATCH, SEQ, None)
        out_b = _attention((qb, 512, 0, 0),
                           [(_seg(kb, LANE, 0, SEQ, 1, 0), _seg(ymix, LANE, COL_GV, SEQ, 1, 0))],
                           _plan_gqa(), GQ_SCALE, BATCH, SEQ, "gqa_ctx")
        out_c = _attention((ymix, 512, COL_NQ, 0),
                           [(_seg(ymix, 512, COL_NK, SEQ, 1, 0), _seg(ymix, 512, COL_NV, SEQ, 1, 0))],
                           _plan_na(), NA_SCALE, BATCH, SEQ, "na_ctx")
        out_d = _attention((qd, ML_HEADS * LANE, 0, 0),
                           [(_seg(kd, ML_HEADS * LANE, 0, SEQ, 1, 0), _seg(vd, 512, 0, SEQ, 1, 0))],
                           _plan_mla(), ML_SCALE, BATCH, SEQ, "mla_ctx")
        xp = _merge((out_a, out_b, out_c, out_d), gates, lw["w_branch"], lw["w_out"], xp, mod, False)
        xp = _ffn(xp, mod, w_ffn_in, w_ffn_out, l, False, final_gain if last else None)
        new.append((
            jnp.swapaxes(st, -1, -2),
            kb.reshape(BATCH, SEQ, GQ_KV_HEADS, GQ_HD),
            ymix[:, COL_GV * LANE:(COL_GV + 1) * LANE].reshape(BATCH, SEQ, GQ_KV_HEADS, GQ_HD),
            ymix[:, COL_NK * 512:(COL_NK + 1) * 512].reshape(BATCH, SEQ, NA_HEADS, NA_HD),
            ymix[:, COL_NV * 512:(COL_NV + 1) * 512].reshape(BATCH, SEQ, NA_HEADS, NA_HD),
            ckv.reshape(BATCH, SEQ, ML_KV_RANK),
            ymix[:, COL_MKR * LANE:COL_MKR * LANE + ML_ROPE].reshape(BATCH, SEQ, ML_ROPE),
        ))

        ymix = _inproj_mix(xs, mod, w_in, l, True)
        gates = _inproj(xs, mod, lw["w_gate"], True, 2048, "inproj_gate_lat", gate=True)
        qb, kb, qd, ckv, kd, vd = _prep(ymix, lw, tabs, True)
        kd_c, vd_c = _mla_cache(mckv_c, mkr_c, lw, l)
        s0_t = jnp.swapaxes(state_hgrn[:, l], -1, -2)
        out_a, = _hgrn(ymix, lb[l], lw["hg_gain"], DEC_BATCH, DEC_SEQ, s0_t)
        out_b = _attention((qb, 512, 0, 0),
                           [(_seg(gqk_c, LANE, 0, PAST_LEN, DEPTH, l), _seg(gqv_c, LANE, 0, PAST_LEN, DEPTH, l)),
                            (_seg(kb, LANE, 0, DEC_SEQ, 1, 0), _seg(ymix, LANE, COL_GV, DEC_SEQ, 1, 0))],
                           _plan_gqa(), GQ_SCALE, DEC_BATCH, DEC_SEQ, "gqa_lat")
        out_c = _na_latent(ymix, nak_c, nav_c, _na_bias_tables(na_rpb[l]), l)
        out_d = _attention((qd, ML_HEADS * LANE, 0, 0),
                           [(_seg(kd_c, ML_HEADS * LANE, 0, PAST_LEN, 1, 0), _seg(vd_c, 512, 0, PAST_LEN, 1, 0)),
                            (_seg(kd, ML_HEADS * LANE, 0, DEC_SEQ, 1, 0), _seg(vd, 512, 0, DEC_SEQ, 1, 0))],
                           _plan_mla(), ML_SCALE, DEC_BATCH, DEC_SEQ, "mla_lat")
        xs = _merge((out_a, out_b, out_c, out_d), gates, lw["w_branch"], lw["w_out"], xs, mod, True)
        xs = _ffn(xs, mod, w_ffn_in, w_ffn_out, l, True, final_gain if last else None)

    y_prompt = xp.reshape(BATCH, SEQ, D_MODEL)
    y_sample = xs.reshape(DEC_BATCH, DEC_SEQ, D_MODEL)
    stacked = tuple(jnp.stack([n[i] for n in new], axis=1) for i in range(7))
    return (y_prompt, y_sample) + stacked
```

```python
import functools

import numpy as np
import jax
import jax.numpy as jnp
from jax import lax
from jax.experimental import pallas as pl
from jax.experimental.pallas import tpu as pltpu

F32 = jnp.float32
BF16 = jnp.bfloat16

D_MODEL = 1024
BATCH = 16
SEQ = 256
DEPTH = 2
DEC_BATCH = 4
DEC_SEQ = 1024
PAST_LEN = 512
GRID_W = 64
EPS = 1e-6
ROPE_THETA = 10000.0
N_BRANCH = 4
BRANCH_W = 512
HG_HEADS = 4
HG_DK = 128
HG_DV = 128
GQ_HEADS = 8
GQ_KV_HEADS = 2
GQ_HD = 64
NA_HEADS = 8
NA_HD = 64
NA_WIN_R = 8
NA_WIN_C = 16
ML_HEADS = 8
ML_NOPE = 64
ML_ROPE = 32
ML_V = 64
ML_Q_RANK = 256
ML_KV_RANK = 128
FFN_HIDDEN = 2816
GQ_SCALE = GQ_HD ** -0.5
NA_SCALE = NA_HD ** -0.5
ML_SCALE = (ML_NOPE + ML_ROPE) ** -0.5

M_ROWS = BATCH * SEQ
assert M_ROWS == DEC_BATCH * DEC_SEQ

LANE = 128
HALF = 64
MIX_W = 5376
IN_MIX = 5280
GATE_W = N_BRANCH * D_MODEL
HG_CHUNK = 16
NEG = -1e30
LOG2E = 1.4426950408889634
VMEM_LIMIT = 56 * 1024 * 1024

COL_HQ, COL_HFF, COL_HFB, COL_HI, COL_HG, COL_GQ, COL_NQ, COL_NK, COL_NV = range(9)
COL_MQA = 18
COL_GK, COL_GV, COL_MKVA, COL_MKR = 38, 39, 40, 41


def _dot(a, b):
    return jnp.dot(a, b, preferred_element_type=F32)


def _dot_nt(a, b):
    return lax.dot_general(a, b, (((1,), (1,)), ((), ())), preferred_element_type=F32)


def _dot_tn(a, b):
    return lax.dot_general(a, b, (((0,), (0,)), ((), ())), preferred_element_type=F32)


def _split3(x):
    x1 = x.astype(BF16)
    r1 = x - x1.astype(F32)
    x2 = r1.astype(BF16)
    x3 = (r1 - x2.astype(F32)).astype(BF16)
    return x1, x2, x3


def _dot_exact_lhs(a_bf16, x):
    x1, x2, x3 = _split3(x)
    return (_dot(a_bf16, x3) + _dot(a_bf16, x2)) + _dot(a_bf16, x1)


def _dot_exact_rhs(x, b_bf16):
    x1, x2, x3 = _split3(x)
    return (_dot(x3, b_bf16) + _dot(x2, b_bf16)) + _dot(x1, b_bf16)


def _rms(x):
    return x * lax.rsqrt(jnp.mean(x * x, axis=-1, keepdims=True) + EPS)


def _silu(x):
    return x * jax.nn.sigmoid(x)


def _aligned(x, m):
    return x if isinstance(x, int) else pl.multiple_of(x, m)


def _params(*sem):
    return pltpu.CompilerParams(dimension_semantics=sem, vmem_limit_bytes=VMEM_LIMIT)


def _mod_row(latent, tm):
    if latent:
        return lambda i: 1 + (i * tm) // DEC_SEQ
    return lambda i: 0


def _ada_kernel(c_ref, w_ref, b_ref, o_ref):
    c = c_ref[...]
    o_ref[...] = _dot(_silu(c).astype(BF16), w_ref[...].astype(BF16)) + b_ref[...]


def _ada(cond8, w_ada, b_ada):
    tn = 1536
    out = pl.pallas_call(
        _ada_kernel,
        grid=(DEPTH, 6 * D_MODEL // tn),
        in_specs=[
            pl.BlockSpec((8, D_MODEL), lambda l, j: (0, 0)),
            pl.BlockSpec((None, D_MODEL, tn), lambda l, j: (l, 0, j)),
            pl.BlockSpec((None, 1, tn), lambda l, j: (l, 0, j)),
        ],
        out_specs=pl.BlockSpec((None, 8, tn), lambda l, j: (l, 0, j)),
        out_shape=jax.ShapeDtypeStruct((DEPTH, 8, 6 * D_MODEL), F32),
        compiler_params=_params("parallel", "parallel"),
        name="ada",
    )(cond8, w_ada, b_ada.reshape(DEPTH, 1, 6 * D_MODEL))
    return out.reshape(DEPTH, 8, 6, 1, D_MODEL)


def _inproj_kernel(x_ref, sc_ref, sh_ref, w_ref, o_ref, h_ref, *, gate):
    @pl.when(pl.program_id(1) == 0)
    def _():
        h = _rms(x_ref[...]) * (1.0 + sc_ref[...]) + sh_ref[...]
        h_ref[...] = h.astype(BF16)

    y = _dot(h_ref[...], w_ref[...])
    o_ref[...] = jax.nn.sigmoid(y).astype(o_ref.dtype) if gate else y


def _inproj(x, mod, w, latent, tn, name, gate=False):
    tm = 1024
    n = w.shape[1]
    row = _mod_row(latent, tm)
    return pl.pallas_call(
        functools.partial(_inproj_kernel, gate=gate),
        grid=(M_ROWS // tm, n // tn),
        in_specs=[
            pl.BlockSpec((tm, D_MODEL), lambda i, j: (i, 0)),
            pl.BlockSpec((None, None, 1, D_MODEL), lambda i, j: (row(i), 1, 0, 0)),
            pl.BlockSpec((None, None, 1, D_MODEL), lambda i, j: (row(i), 0, 0, 0)),
            pl.BlockSpec((D_MODEL, tn), lambda i, j: (0, j)),
        ],
        out_specs=pl.BlockSpec((tm, tn), lambda i, j: (i, j)),
        out_shape=jax.ShapeDtypeStruct((M_ROWS, n), BF16 if gate else F32),
        scratch_shapes=[pltpu.VMEM((tm, D_MODEL), BF16)],
        compiler_params=_params("parallel", "arbitrary"),
        name=name,
    )(x, mod, mod, w)


MIX_PIECE = 256
MIX_TILE = 1792
MIX_PIECES = MIX_TILE // MIX_PIECE


def _mix_source(t):
    return jnp.where(t < 12, t, jnp.where(t < 18, t + 1, jnp.where(t == 18, 19, jnp.where(t == 19, 12, 20))))


def _inproj_mix_kernel(*refs):
    x_ref, sc_ref, sh_ref = refs[:3]
    w_refs = refs[3:3 + MIX_PIECES]
    o_ref, h_ref = refs[3 + MIX_PIECES:]

    @pl.when(pl.program_id(1) == 0)
    def _():
        h = _rms(x_ref[...]) * (1.0 + sc_ref[...]) + sh_ref[...]
        h_ref[...] = h.astype(BF16)

    h = h_ref[...]
    for p, w_ref in enumerate(w_refs):
        o_ref[:, p * MIX_PIECE:(p + 1) * MIX_PIECE] = _dot(h, w_ref[...])


def _inproj_mix(x, mod, w_in, layer, latent):
    tm = 1024
    row = _mod_row(latent, tm)
    piece = lambda p: pl.BlockSpec((None, D_MODEL, MIX_PIECE),
                                   lambda i, j: (layer, 0, _mix_source(j * MIX_PIECES + p)))
    return pl.pallas_call(
        _inproj_mix_kernel,
        grid=(M_ROWS // tm, MIX_W // MIX_TILE),
        in_specs=[
            pl.BlockSpec((tm, D_MODEL), lambda i, j: (i, 0)),
            pl.BlockSpec((None, None, 1, D_MODEL), lambda i, j: (row(i), 1, 0, 0)),
            pl.BlockSpec((None, None, 1, D_MODEL), lambda i, j: (row(i), 0, 0, 0)),
        ] + [piece(p) for p in range(MIX_PIECES)],
        out_specs=pl.BlockSpec((tm, MIX_TILE), lambda i, j: (i, j)),
        out_shape=jax.ShapeDtypeStruct((M_ROWS, MIX_W), F32),
        scratch_shapes=[pltpu.VMEM((tm, D_MODEL), BF16)],
        compiler_params=_params("parallel", "arbitrary"),
        name="inproj_mix_lat" if latent else "inproj_mix_ctx",
    )(x, mod, mod, *([w_in] * MIX_PIECES))


def _rope(x, tab_ref, reps):
    w = x.shape[1]
    c = jnp.concatenate([tab_ref[0]] * reps, axis=1) if reps > 1 else tab_ref[0]
    se = jnp.concatenate([tab_ref[1]] * reps, axis=1) if reps > 1 else tab_ref[1]
    so = jnp.concatenate([tab_ref[2]] * reps, axis=1) if reps > 1 else tab_ref[2]
    return x * c + pltpu.roll(x, w - 1, 1) * se + pltpu.roll(x, 1, 1) * so


def _head_rms(x, gain, avg_bf16):
    ms = _dot_exact_rhs(x * x, avg_bf16)
    return x * lax.rsqrt(ms + EPS) * gain


def _prep_kernel(*refs, rope):
    (gq_ref, gk_ref, mqa_ref, mkva_ref, mkr_ref, gqg_ref, gkg_ref, mqg_ref, mkvg_ref,
     wqb_ref, wk_ref, wv_ref, avg_ref) = refs[:13]
    if rope:
        gtab_ref, mtab_ref = refs[13:15]
        outs = refs[15:]
    else:
        outs = refs[13:]
    qb_ref, kb_ref, qd_ref, ckv_ref, kd_ref, vd_ref = outs

    q = _head_rms(gq_ref[...], gqg_ref[...], avg_ref[...])
    k = _head_rms(gk_ref[...], gkg_ref[...], avg_ref[0:LANE, 0:LANE])
    qd = _dot((_rms(mqa_ref[...]) * mqg_ref[...]).astype(BF16), wqb_ref[...])
    ckv = _rms(mkva_ref[...]) * mkvg_ref[...]
    lane = lax.broadcasted_iota(jnp.int32, (1, LANE), 1)
    kr = jnp.where(lane < ML_ROPE, mkr_ref[...], 0.0)
    if rope:
        q = _rope(q, gtab_ref, GQ_HEADS * GQ_HD // LANE)
        k = _rope(k, gtab_ref, 1)
        qd = _rope(qd, mtab_ref, ML_HEADS)
        kr = _rope(kr, mtab_ref, 1)
    qb_ref[...] = q.astype(BF16)
    kb_ref[...] = k
    qd_ref[...] = qd.astype(BF16)
    ckv_ref[...] = ckv
    cb = ckv.astype(BF16)
    kd_ref[...] = (_dot(cb, wk_ref[...]) + jnp.concatenate([kr] * ML_HEADS, axis=1)).astype(BF16)
    vd_ref[...] = _dot(cb, wv_ref[...]).astype(BF16)


def _prep(ymix, lw, tabs, latent):
    tm = 512
    const = lambda i: (0, 0)
    in_specs = [
        pl.BlockSpec((tm, 512), lambda i: (i, COL_GQ)),
        pl.BlockSpec((tm, LANE), lambda i: (i, COL_GK)),
        pl.BlockSpec((tm, 256), lambda i: (i, COL_MQA)),
        pl.BlockSpec((tm, LANE), lambda i: (i, COL_MKVA)),
        pl.BlockSpec((tm, LANE), lambda i: (i, COL_MKR)),
        pl.BlockSpec((1, 512), const),
        pl.BlockSpec((1, LANE), const),
        pl.BlockSpec((1, 256), const),
        pl.BlockSpec((1, LANE), const),
        pl.BlockSpec((ML_Q_RANK, ML_HEADS * LANE), const),
        pl.BlockSpec((ML_KV_RANK, ML_HEADS * LANE), const),
        pl.BlockSpec((ML_KV_RANK, ML_HEADS * ML_V), const),
        pl.BlockSpec((512, 512), const),
    ]
    args = [ymix, ymix, ymix, ymix, ymix, lw["gq_q_gain"], lw["gq_k_gain"], lw["ml_q_gain"],
            lw["ml_kv_gain"], lw["w_qb"], lw["w_k"], lw["w_v"], lw["avg"]]
    if latent:
        per = DEC_SEQ // tm
        in_specs += [pl.BlockSpec((3, tm, LANE), lambda i: (0, i % per, 0))] * 2
        args += [tabs["gq"], tabs["ml"]]
    widths = (512, LANE, ML_HEADS * LANE, LANE, ML_HEADS * LANE, ML_HEADS * ML_V)
    return pl.pallas_call(
        functools.partial(_prep_kernel, rope=latent),
        grid=(M_ROWS // tm,),
        in_specs=in_specs,
        out_specs=[pl.BlockSpec((tm, w), lambda i: (i, 0)) for w in widths],
        out_shape=[jax.ShapeDtypeStruct((M_ROWS, w), dt)
                   for w, dt in zip(widths, (BF16, F32, BF16, F32, BF16, BF16))],
        compiler_params=_params("parallel"),
        name="prep_lat" if latent else "prep_ctx",
    )(*args)


def _mla_cache_kernel(ckv_ref, kr_ref, wk_ref, wv_ref, kd_ref, vd_ref):
    cb = ckv_ref[...].astype(BF16)
    kd_ref[...] = (_dot(cb, wk_ref[...]) + jnp.concatenate([kr_ref[...]] * ML_HEADS, axis=1)).astype(BF16)
    vd_ref[...] = _dot(cb, wv_ref[...]).astype(BF16)


def _mla_cache(ckv, kr_blk, lw, layer):
    rows = DEC_BATCH * PAST_LEN
    tm = PAST_LEN
    const = lambda i: (0, 0)
    return pl.pallas_call(
        _mla_cache_kernel,
        grid=(DEC_BATCH,),
        in_specs=[
            pl.BlockSpec((tm, LANE), lambda i: (i * DEPTH + layer, 0)),
            pl.BlockSpec((tm, LANE), lambda i: (i * DEPTH + layer, 0)),
            pl.BlockSpec((ML_KV_RANK, ML_HEADS * LANE), const),
            pl.BlockSpec((ML_KV_RANK, ML_HEADS * ML_V), const),
        ],
        out_specs=[pl.BlockSpec((tm, ML_HEADS * LANE), lambda i: (i, 0)),
                   pl.BlockSpec((tm, ML_HEADS * ML_V), lambda i: (i, 0))],
        out_shape=[jax.ShapeDtypeStruct((rows, ML_HEADS * LANE), BF16),
                   jax.ShapeDtypeStruct((rows, ML_HEADS * ML_V), BF16)],
        compiler_params=_params("parallel"),
        name="mla_cache",
    )(ckv, kr_blk, lw["w_k"], lw["w_v"])


def _softmax_pv(scores, values):
    m = scores[0].max(axis=-1, keepdims=True)
    for s in scores[1:]:
        m = jnp.maximum(m, s.max(axis=-1, keepdims=True))
    den = None
    out = None
    for s, v in zip(scores, values):
        p = jnp.exp(s - m)
        d = p.sum(axis=-1, keepdims=True)
        o = _dot(p.astype(BF16), v)
        den = d if den is None else den + d
        out = o if out is None else out + o
    return out / den


def _lane_halves():
    lane = lax.broadcasted_iota(jnp.int32, (1, LANE), 1)
    return lane < HALF, lane >= HALF


def _attn_kernel(*refs, plan, nseg, scale, fold_scale):
    q_ref = refs[0]
    k_refs = [refs[1 + 2 * i] for i in range(nseg)]
    v_refs = [refs[2 + 2 * i] for i in range(nseg)]
    o_ref = refs[1 + 2 * nseg]
    tq = q_ref.shape[0]
    lo, hi = _lane_halves()
    cache = {}

    def block(kind, seg, blk, swap):
        key = (kind, seg, blk, swap)
        if key not in cache:
            ref = (k_refs if kind == "k" else v_refs)[seg]
            x = ref[:, blk * LANE:(blk + 1) * LANE]
            if swap:
                x = pltpu.roll(x, HALF, 1)
            cache[key] = x.astype(BF16)
        return cache[key]

    parts = {}
    for members, kblk, swap, vblk in plan:
        qs = []
        for qblk, qhalf, _, _ in members:
            q = q_ref[:, qblk * LANE:(qblk + 1) * LANE]
            if fold_scale:
                q = q * scale
            if qhalf is not None:
                q = jnp.where(lo if qhalf == 0 else hi, q, 0.0)
            qs.append(q)
        q = (jnp.concatenate(qs, axis=0) if len(qs) > 1 else qs[0]).astype(BF16)
        scores = [_dot_nt(q, block("k", s, kblk, swap)) for s in range(nseg)]
        if not fold_scale:
            scores = [s * scale for s in scores]
        out = _softmax_pv(scores, [block("v", s, vblk, swap) for s in range(nseg)])
        for n, (_, _, oblk, ohalf) in enumerate(members):
            parts[(oblk, ohalf)] = out[n * tq:(n + 1) * tq]
    for oblk in sorted({key[0] for key in parts}):
        o_ref[:, oblk * LANE:(oblk + 1) * LANE] = jnp.where(lo, parts[(oblk, 0)], parts[(oblk, 1)]).astype(BF16)


def _plan_gqa():
    per_kv = GQ_HEADS // GQ_KV_HEADS
    plan = []
    for g in range(GQ_KV_HEADS):
        for half in range(2):
            heads = [h for h in range(g * per_kv, (g + 1) * per_kv) if h % 2 == half]
            plan.append((tuple((h // 2, half, h // 2, half) for h in heads), 0, half != g, 0))
    return tuple(plan)


def _plan_na():
    return tuple((((j, 0, j, 0), (j, 1, j, 1)), j, False, j) for j in range(NA_HEADS // 2))


def _plan_mla():
    return tuple((((h, None, h // 2, h % 2),), h, False, h // 2) for h in range(ML_HEADS))


def _is_pow2(x):
    return float(np.log2(x)).is_integer()


def _attention(q, segs, plan, scale, batch, tq_total, name):
    tq = min(tq_total, 256)
    per = tq_total // tq
    q_arr, q_w, q_col, q_off = q
    in_specs = [pl.BlockSpec((tq, q_w), lambda b, i: (q_off + b * per + i, q_col))]
    args = [q_arr]
    for k, v in segs:
        for arr, w, col, rows, stride, off in (k, v):
            in_specs.append(pl.BlockSpec(
                (rows, w), functools.partial(lambda b, i, col, stride, off: (off + b * stride, col),
                                             col=col, stride=stride, off=off)))
            args.append(arr)
    out_w = 512
    return pl.pallas_call(
        functools.partial(_attn_kernel, plan=plan, nseg=len(segs), scale=scale, fold_scale=_is_pow2(scale)),
        grid=(batch, per),
        in_specs=in_specs,
        out_specs=pl.BlockSpec((tq, out_w), lambda b, i: (b * per + i, 0)),
        out_shape=jax.ShapeDtypeStruct((M_ROWS, out_w), BF16),
        compiler_params=_params("parallel", "parallel"),
        name=name,
    )(*args)


NA_DR = 2 * NA_WIN_R - 1
NA_DC = 2 * NA_WIN_C - 1
NA_ROWS = DEC_SEQ // GRID_W
NA_QROWS = 4
NA_WROWS = 12
NA_T_RIGHT = NA_DR - 1
NA_T_LEFT = NA_DR
NA_T_NONE = NA_DR + 1
NA_T_SIZE = NA_DR + 2
assert _is_pow2(NA_SCALE)


def _na_kernel(q_ref, kl_ref, vl_ref, kc_ref, vc_ref, tz_ref, o_ref, kc_scr, vc_scr):
    g = pl.program_id(1)

    @pl.when(g == 0)
    def _():
        for src, dst in ((kc_ref, kc_scr), (vc_ref, vc_scr)):
            for h in range(NA_HEADS):
                dst[:, h * NA_HD:(h + 1) * NA_HD] = src[pl.ds(h, PAST_LEN, stride=NA_HEADS), :].astype(BF16)

    w0 = (g // 2) * (NA_ROWS - NA_WROWS)
    k0 = pl.multiple_of(w0 * GRID_W, (NA_ROWS - NA_WROWS) * GRID_W)
    win = NA_WROWS * GRID_W
    nq = NA_QROWS * GRID_W
    entry = []
    for a in range(NA_QROWS):
        rq = g * NA_QROWS + a
        start = jnp.clip(rq - NA_WIN_R // 2, 0, NA_ROWS - NA_WIN_R)
        per_pair = []
        for i in range(NA_WROWS // 2):
            kr = w0 + 2 * i
            in_l = jnp.logical_and(kr >= start, kr < start + NA_WIN_R)
            in_r = jnp.logical_and(kr + 1 >= start, kr + 1 < start + NA_WIN_R)
            both = jnp.logical_and(in_l, in_r)
            d_l = kr - rq + NA_WIN_R - 1
            per_pair.append(jnp.where(both, d_l, jnp.where(in_r, NA_T_RIGHT, jnp.where(in_l, NA_T_LEFT, NA_T_NONE))))
        entry.append(per_pair)
    lo, hi = _lane_halves()
    for j in range(NA_HEADS // 2):
        sl = slice(j * LANE, (j + 1) * LANE)
        k_loc = kl_ref[pl.ds(k0, win), sl].astype(BF16)
        v_loc = vl_ref[pl.ds(k0, win), sl].astype(BF16)
        k_ctx = kc_scr[:, sl]
        v_ctx = vc_scr[:, sl]
        q_pair = q_ref[:, sl] * NA_SCALE
        q = jnp.concatenate([jnp.where(lo, q_pair, 0.0), jnp.where(hi, q_pair, 0.0)], axis=0).astype(BF16)
        bias = jnp.concatenate(
            [jnp.concatenate([tz_ref[2 * j + p, e] for e in entry[a]], axis=1)
             for p in range(2) for a in range(NA_QROWS)], axis=0)
        s_loc = _dot_nt(q, k_loc) + bias
        s_ctx = _dot_nt(q, k_ctx)
        out = _softmax_pv([s_loc, s_ctx], [v_loc, v_ctx])
        o_ref[:, sl] = jnp.where(lo, out[:nq], out[nq:]).astype(BF16)


def _na_bias_kernel(rpb_ref, sel_ref, neg_ref, o_ref):
    o_ref[...] = _dot_exact_rhs(rpb_ref[...], sel_ref[...]) + neg_ref[...]


def _na_bias_tables(rpb):
    col = np.arange(GRID_W)
    col_start = np.clip(col - NA_WIN_C // 2, 0, GRID_W - NA_WIN_C)
    col_ok = (col[None, :] >= col_start[:, None]) & (col[None, :] < col_start[:, None] + NA_WIN_C)
    dc = col[None, :] - col[:, None] + NA_WIN_C - 1
    kpad = 32
    sel = (dc[None, :, :] == np.arange(kpad)[:, None, None]) & col_ok[None]
    sel = jnp.asarray(sel.reshape(kpad, GRID_W * GRID_W), BF16)
    neg = jnp.asarray(np.where(col_ok, 0.0, NEG).reshape(1, GRID_W * GRID_W), F32)
    rpb2 = jnp.pad(rpb.reshape(NA_HEADS * NA_DR, NA_DC), ((0, 0), (0, kpad - NA_DC)))
    n_rows = NA_HEADS * NA_DR
    full = lambda shape: pl.BlockSpec(shape, lambda i: (0, 0))
    t = pl.pallas_call(
        _na_bias_kernel,
        grid=(1,),
        in_specs=[full((n_rows, kpad)), full((kpad, GRID_W * GRID_W)), full((1, GRID_W * GRID_W))],
        out_specs=full((n_rows, GRID_W * GRID_W)),
        out_shape=jax.ShapeDtypeStruct((n_rows, GRID_W * GRID_W), F32),
        compiler_params=_params("arbitrary"),
        name="na_bias",
    )(rpb2, sel, neg)
    t = t.reshape(NA_HEADS, NA_DR, GRID_W, GRID_W)
    masked = jnp.full((NA_HEADS, 1, GRID_W, GRID_W), NEG, F32)
    first, last = NA_WIN_R // 2 - 1, NA_WIN_R // 2 + NA_WIN_R - 2
    return jnp.concatenate([
        jnp.concatenate([t[:, :-1], t[:, 1:]], axis=-1),
        jnp.concatenate([masked, t[:, first:first + 1]], axis=-1),
        jnp.concatenate([t[:, last:last + 1], masked], axis=-1),
        jnp.concatenate([masked, masked], axis=-1)], axis=1)


def _na_latent(ymix, cache_k, cache_v, tz, layer):
    groups = NA_ROWS // NA_QROWS
    nq = NA_QROWS * GRID_W
    return pl.pallas_call(
        _na_kernel,
        grid=(DEC_BATCH, groups),
        in_specs=[
            pl.BlockSpec((nq, 512), lambda b, g: (b * groups + g, COL_NQ)),
            pl.BlockSpec((DEC_SEQ, 512), lambda b, g: (b, COL_NK)),
            pl.BlockSpec((DEC_SEQ, 512), lambda b, g: (b, COL_NV)),
            pl.BlockSpec((PAST_LEN * NA_HEADS, NA_HD), lambda b, g: (b * DEPTH + layer, 0)),
            pl.BlockSpec((PAST_LEN * NA_HEADS, NA_HD), lambda b, g: (b * DEPTH + layer, 0)),
            pl.BlockSpec((NA_HEADS, NA_T_SIZE, GRID_W, LANE), lambda b, g: (0, 0, 0, 0)),
        ],
        out_specs=pl.BlockSpec((nq, 512), lambda b, g: (b * groups + g, 0)),
        out_shape=jax.ShapeDtypeStruct((M_ROWS, 512), BF16),
        scratch_shapes=[pltpu.VMEM((PAST_LEN, NA_HEADS * NA_HD), BF16)] * 2,
        compiler_params=_params("parallel", "arbitrary"),
        name="na_lat",
    )(ymix, ymix, ymix, cache_k, cache_v, tz)


def _na_cache_out_kernel(*refs):
    srcs, (ko_ref, vo_ref) = refs[:2 * DEPTH], refs[2 * DEPTH:]
    layer = pl.program_id(1)
    for l in range(DEPTH):
        @pl.when(layer == l)
        def _():
            for src, dst in ((srcs[2 * l], ko_ref), (srcs[2 * l + 1], vo_ref)):
                x = src[...]
                for h in range(NA_HEADS):
                    dst[pl.ds(h, SEQ, stride=NA_HEADS), :] = x[:, h * NA_HD:(h + 1) * NA_HD]


def _na_cache_out(ymix_layers):
    rows = SEQ * NA_HEADS
    in_specs, args = [], []
    for ymix in ymix_layers:
        for col in (COL_NK, COL_NV):
            in_specs.append(pl.BlockSpec((SEQ, 512), functools.partial(lambda b, l, col: (b, col), col=col)))
            args.append(ymix)
    out = pl.pallas_call(
        _na_cache_out_kernel,
        grid=(BATCH, DEPTH),
        in_specs=in_specs,
        out_specs=[pl.BlockSpec((rows, NA_HD), lambda b, l: (b * DEPTH + l, 0))] * 2,
        out_shape=[jax.ShapeDtypeStruct((BATCH * DEPTH * rows, NA_HD), F32)] * 2,
        compiler_params=_params("parallel", "arbitrary"),
        name="na_cache_out",
    )(*args)
    return tuple(o.reshape(BATCH, DEPTH, SEQ, NA_HEADS, NA_HD) for o in out)


HG_BLOCK = 64
HG_STATE_UNROLL = 8
HG_DEC_ROWS = 8


def _hgrn_kernel(*refs, seq, has_state):
    if has_state:
        (q_ref, ff_ref, fb_ref, v_ref, g_ref, lb_ref, gain_ref, s0_ref,
         o_ref, of_ref, ob_ref, qin_ref, kend_ref, dec_ref, st_ref) = refs
    else:
        (q_ref, ff_ref, fb_ref, v_ref, g_ref, lb_ref, gain_ref,
         o_ref, sout_ref, of_ref, ob_ref, qin_ref, kend_ref, dec_ref, st_ref) = refs
    c = HG_CHUNK
    hc = c // 2
    rb = HG_BLOCK
    per_block = rb // c
    n_chunks = seq // c
    width = HG_HEADS * HG_DK
    sub = HG_DEC_ROWS

    ri = lax.broadcasted_iota(jnp.int32, (rb, rb), 0)
    ci = lax.broadcasted_iota(jnp.int32, (rb, rb), 1)
    same = (ri // c) == (ci // c)
    tri_f = jnp.where(jnp.logical_and(same, ci <= ri), 1.0, 0.0).astype(BF16)
    tri_b = jnp.where(jnp.logical_and(same, ci >= ri), 1.0, 0.0).astype(BF16)
    ones = jnp.ones((HG_DK, HG_DV), BF16)
    rowid = lax.broadcasted_iota(jnp.int32, (c, width), 0)
    rowid_half = lax.broadcasted_iota(jnp.int32, (hc, width), 0)
    o_refs = (of_ref, ob_ref)

    n_blocks = seq // rb

    def block_of(d, step):
        return step if d == 0 else n_blocks - 1 - step

    def pair_block(step):
        for d, (pre_ref, tri) in enumerate(((ff_ref, tri_f), (fb_ref, tri_b))):
            fwd = d == 0
            blk = block_of(d, step)
            r0 = _aligned(blk * rb, rb)
            q_all = q_ref[pl.ds(r0, rb), :] * (HG_DK ** -0.5)
            v_all = v_ref[pl.ds(r0, rb), :]
            lb = lb_ref[d:d + 1, :]
            f = lb + (1.0 - lb) * jax.nn.sigmoid(pre_ref[pl.ds(r0, rb), :])
            k_all = 1.0 - f
            b_all = _dot_exact_lhs(tri, jnp.log(f)) * LOG2E
            src_all = b_all - jnp.log(k_all) * LOG2E
            wide = range(0, hc) if fwd else range(hc, c)
            narrow = range(hc, c) if fwd else range(0, hc)
            far = slice(hc, c) if fwd else slice(0, hc)
            row_far = rowid_half + (hc if fwd else 0)
            blocks, q_in, k_end = [], [], []
            for m in range(per_block):
                rows = slice(m * c, (m + 1) * c)
                q, k, b, src = q_all[rows], k_all[rows], b_all[rows], src_all[rows]
                b_last = b[c - 1:c] if fwd else b[0:1]
                q_in.append(q * jnp.exp2(b))
                k_end.append(k * jnp.exp2(b_last - b))
                dec_row = _aligned((blk * per_block + m) * sub, sub)
                dec_ref[d, pl.ds(dec_row, sub), :] = jnp.broadcast_to(jnp.exp2(b_last), (sub, width))
                for s in wide:
                    p = q * jnp.exp2(b - src[s:s + 1])
                    keep = (rowid >= s) if fwd else (rowid <= s)
                    blocks.append(jnp.where(keep, p, 0.0).astype(BF16))
                q_far, b_far = q[far], b[far]
                halves = []
                for s in narrow:
                    p = q_far * jnp.exp2(b_far - src[s:s + 1])
                    keep = (row_far >= s) if fwd else (row_far <= s)
                    halves.append(jnp.where(keep, p, 0.0))
                for i in range(0, hc, 2):
                    blocks.append(jnp.concatenate(halves[i:i + 2], axis=0).astype(BF16))
            qin_ref[d, pl.ds(r0, rb), :] = jnp.concatenate(q_in, axis=0).astype(BF16)
            kend_ref[d, pl.ds(r0, rb), :] = jnp.concatenate(k_end, axis=0).astype(BF16)
            pairs = jnp.concatenate(blocks, axis=0)
            per_chunk = hc * c + hc * hc
            outs = []
            for h in range(HG_HEADS):
                sl = slice(h * HG_DK, (h + 1) * HG_DK)
                a = _dot(pairs[:, sl], ones)
                chunks = []
                for m in range(per_block):
                    base = m * per_chunk
                    v = v_all[m * c:(m + 1) * c]
                    o = None
                    for n, s in enumerate(wide):
                        term = a[base + n * c:base + (n + 1) * c] * v[s:s + 1, sl]
                        o = term if o is None else o + term
                    o_far = None
                    for n, s in enumerate(narrow):
                        first = base + hc * c + n * hc
                        term = a[first:first + hc] * v[s:s + 1, sl]
                        o_far = term if o_far is None else o_far + term
                    zero = jnp.zeros_like(o_far)
                    chunks.append(o + jnp.concatenate([zero, o_far] if fwd else [o_far, zero], axis=0))
                outs.append(jnp.concatenate(chunks, axis=0))
            o_refs[d][pl.ds(r0, rb), :] = jnp.concatenate(outs, axis=1)

    def state_step(n, carry):
        for d in range(2):
            chunk = n if d == 0 else n_chunks - 1 - n
            r0 = _aligned(chunk * c, c)
            q_in = qin_ref[d, pl.ds(r0, c), :]
            k_end = kend_ref[d, pl.ds(r0, c), :]
            v16 = v_ref[pl.ds(r0, c), :].astype(BF16)
            dec = dec_ref[d, pl.ds(_aligned(chunk * sub, sub), sub), :][0:1]
            outs = []
            for h in range(HG_HEADS):
                sl = slice(h * HG_DK, (h + 1) * HG_DK)
                st = st_ref[d, h]
                outs.append(_dot_nt(q_in[:, sl], st.astype(BF16)))
                st_ref[d, h] = st * dec[:, sl] + _dot_tn(v16[:, sl], k_end[:, sl])
            o_refs[d][pl.ds(r0, c), :] += jnp.concatenate(outs, axis=1)
        return carry

    if has_state:
        st_ref[...] = s0_ref[...]
    else:
        st_ref[...] = jnp.zeros(st_ref.shape, F32)

    def pair_step(i, carry):
        pair_block(i)
        return carry

    lax.fori_loop(0, n_blocks, pair_step, 0)
    lax.fori_loop(0, n_chunks, state_step, 0, unroll=HG_STATE_UNROLL)

    for h in range(HG_HEADS):
        sl = slice(h * HG_DV, (h + 1) * HG_DV)
        o = of_ref[:, sl] + ob_ref[:, sl]
        o_ref[:, sl] = (_rms(o) * gain_ref[:, sl] * _silu(g_ref[:, sl])).astype(BF16)
    if not has_state:
        sout_ref[...] = st_ref[...]


def _hgrn(ymix, lb, gain, batch, seq, s0_t):
    width = HG_HEADS * HG_DK
    has_state = s0_t is not None
    st_shape = (2, HG_HEADS, HG_DV, HG_DK)
    in_specs = [pl.BlockSpec((seq, width), functools.partial(lambda b, col: (b, col), col=col))
                for col in (COL_HQ, COL_HFF, COL_HFB, COL_HI, COL_HG)]
    in_specs += [pl.BlockSpec((2, width), lambda b: (0, 0)), pl.BlockSpec((1, width), lambda b: (0, 0))]
    args = [ymix] * 5 + [lb, gain]
    out_specs = [pl.BlockSpec((seq, width), lambda b: (b, 0))]
    out_shape = [jax.ShapeDtypeStruct((M_ROWS, width), BF16)]
    if has_state:
        in_specs.append(pl.BlockSpec((None,) + st_shape, lambda b: (b, 0, 0, 0, 0)))
        args.append(s0_t)
    else:
        out_specs.append(pl.BlockSpec((None,) + st_shape, lambda b: (b, 0, 0, 0, 0)))
        out_shape.append(jax.ShapeDtypeStruct((batch,) + st_shape, F32))
    return pl.pallas_call(
        functools.partial(_hgrn_kernel, seq=seq, has_state=has_state),
        grid=(batch,),
        in_specs=in_specs,
        out_specs=out_specs,
        out_shape=out_shape,
        scratch_shapes=[pltpu.VMEM((seq, width), F32), pltpu.VMEM((seq, width), F32),
                        pltpu.VMEM((2, seq, width), BF16), pltpu.VMEM((2, seq, width), BF16),
                        pltpu.VMEM((2, seq // HG_CHUNK * HG_DEC_ROWS, width), F32),
                        pltpu.VMEM(st_shape, F32)],
        compiler_params=_params("parallel"),
        name="hgrn_lat" if has_state else "hgrn_ctx",
    )(*args)


def _merge_kernel(oa_ref, ob_ref, oc_ref, od_ref, gt_ref, wb_ref, wo_ref, x_ref, g1_ref, out_ref):
    acc = None
    for n, o_ref in enumerate((oa_ref, ob_ref, oc_ref, od_ref)):
        bo = _dot(o_ref[...], wb_ref[n])
        term = gt_ref[:, n * D_MODEL:(n + 1) * D_MODEL] * bo
        acc = term if acc is None else acc + term
    out_ref[...] = x_ref[...] + g1_ref[...] * _dot(acc.astype(BF16), wo_ref[...])


def _merge(branches, gates, w_branch, w_out, x, mod, latent):
    tm = 512
    row = _mod_row(latent, tm)
    tile = lambda w: pl.BlockSpec((tm, w), lambda i: (i, 0))
    return pl.pallas_call(
        _merge_kernel,
        grid=(M_ROWS // tm,),
        in_specs=[tile(BRANCH_W)] * N_BRANCH + [
            tile(GATE_W),
            pl.BlockSpec((N_BRANCH, BRANCH_W, D_MODEL), lambda i: (0, 0, 0)),
            pl.BlockSpec((D_MODEL, D_MODEL), lambda i: (0, 0)),
            tile(D_MODEL),
            pl.BlockSpec((None, None, 1, D_MODEL), lambda i: (row(i), 2, 0, 0)),
        ],
        out_specs=tile(D_MODEL),
        out_shape=jax.ShapeDtypeStruct((M_ROWS, D_MODEL), F32),
        compiler_params=_params("parallel"),
        name="merge_lat" if latent else "merge_ctx",
    )(*branches, gates, w_branch, w_out, x, mod)


FFN_CHUNK = 256
FFN_STEPS = FFN_HIDDEN // FFN_CHUNK
assert FFN_STEPS * FFN_CHUNK == FFN_HIDDEN


def _ffn_kernel(*refs, final):
    x_ref, sc_ref, sh_ref, g2_ref, wa_ref, wg_ref, wo_ref = refs[:7]
    if final:
        fg_ref, out_ref, h_ref, acc_ref = refs[7:]
    else:
        out_ref, h_ref, acc_ref = refs[7:]
    j = pl.program_id(1)

    @pl.when(j == 0)
    def _():
        h = _rms(x_ref[...]) * (1.0 + sc_ref[...]) + sh_ref[...]
        h_ref[...] = h.astype(BF16)
        acc_ref[...] = jnp.zeros(acc_ref.shape, F32)

    h = h_ref[...]
    act = _silu(_dot(h, wg_ref[...].astype(BF16))) * _dot(h, wa_ref[...].astype(BF16))
    acc_ref[...] += _dot(act.astype(BF16), wo_ref[...].astype(BF16))

    @pl.when(j == FFN_STEPS - 1)
    def _():
        y = x_ref[...] + g2_ref[...] * acc_ref[...]
        out_ref[...] = _rms(y) * fg_ref[...] if final else y


def _ffn(x, mod, w_in, w_out, layer, latent, final_gain):
    tm = 1024
    row = _mod_row(latent, tm)
    modspec = lambda which: pl.BlockSpec((None, None, 1, D_MODEL), lambda i, j: (row(i), which, 0, 0))
    final = final_gain is not None
    in_specs = [
        pl.BlockSpec((tm, D_MODEL), lambda i, j: (i, 0)),
        modspec(4), modspec(3), modspec(5),
        pl.BlockSpec((None, D_MODEL, FFN_CHUNK), lambda i, j: (layer, 0, j)),
        pl.BlockSpec((None, D_MODEL, FFN_CHUNK), lambda i, j: (layer, 0, FFN_STEPS + j)),
        pl.BlockSpec((None, FFN_CHUNK, D_MODEL), lambda i, j: (layer, j, 0)),
    ]
    args = [x, mod, mod, mod, w_in, w_in, w_out]
    if final:
        in_specs.append(pl.BlockSpec((1, D_MODEL), lambda i, j: (0, 0)))
        args.append(final_gain.reshape(1, D_MODEL))
    return pl.pallas_call(
        functools.partial(_ffn_kernel, final=final),
        grid=(M_ROWS // tm, FFN_STEPS),
        in_specs=in_specs,
        out_specs=pl.BlockSpec((tm, D_MODEL), lambda i, j: (i, 0)),
        out_shape=jax.ShapeDtypeStruct((M_ROWS, D_MODEL), F32),
        scratch_shapes=[pltpu.VMEM((tm, D_MODEL), BF16), pltpu.VMEM((tm, D_MODEL), F32)],
        compiler_params=_params("parallel", "arbitrary"),
        name="ffn_lat" if latent else "ffn_ctx",
    )(*args)


def _rope_tables():
    t = jnp.arange(DEC_SEQ)
    row = (t // GRID_W).astype(F32)[:, None]
    col = (t % GRID_W).astype(F32)[:, None]

    def angles(rot_dim):
        n_freq = rot_dim // 4
        inv_freq = ROPE_THETA ** (-jnp.arange(n_freq, dtype=F32) / n_freq)
        return jnp.concatenate([row * inv_freq, col * inv_freq], axis=-1)

    def expand(ang):
        cos = jnp.repeat(jnp.cos(ang), 2, axis=-1)
        sin = jnp.repeat(jnp.sin(ang), 2, axis=-1)
        even = (jnp.arange(cos.shape[-1]) % 2 == 0)[None, :]
        return cos, jnp.where(even, -sin, 0.0), jnp.where(even, 0.0, sin)

    gq = [jnp.concatenate([a, a], axis=-1) for a in expand(angles(GQ_HD))]
    ml = []
    for idx, a in enumerate(expand(angles(ML_ROPE))):
        fill = 1.0 if idx == 0 else 0.0
        ml.append(jnp.concatenate([a, jnp.full((DEC_SEQ, LANE - ML_ROPE), fill, F32)], axis=-1))
    return {"gq": jnp.stack(gq), "ml": jnp.stack(ml)}


def _layer_weights(l, w_in, gq_q_gain, gq_k_gain, ml_q_a_gain, ml_kv_a_gain, ml_w_q_b, ml_w_kv_b,
                   w_branch, w_out, hg_gain, avg):
    w_gate = w_in[l][:, IN_MIX:]
    pad = LANE - ML_NOPE - ML_ROPE
    qb = ml_w_q_b[l].reshape(ML_Q_RANK, ML_HEADS, ML_NOPE + ML_ROPE)
    qb = jnp.concatenate([qb[:, :, ML_NOPE:], qb[:, :, :ML_NOPE], jnp.zeros((ML_Q_RANK, ML_HEADS, pad), F32)],
                         axis=-1).reshape(ML_Q_RANK, ML_HEADS * LANE)
    kvb = ml_w_kv_b[l].reshape(ML_KV_RANK, ML_HEADS, ML_NOPE + ML_V)
    wk = jnp.pad(kvb[:, :, :ML_NOPE], ((0, 0), (0, 0), (ML_ROPE, pad))).reshape(ML_KV_RANK, ML_HEADS * LANE)
    wv = kvb[:, :, ML_NOPE:].reshape(ML_KV_RANK, ML_HEADS * ML_V)
    return {
        "w_gate": w_gate,
        "gq_q_gain": jnp.tile(gq_q_gain[l], GQ_HEADS).reshape(1, -1),
        "gq_k_gain": jnp.tile(gq_k_gain[l], GQ_KV_HEADS).reshape(1, -1),
        "ml_q_gain": ml_q_a_gain[l].reshape(1, -1), "ml_kv_gain": ml_kv_a_gain[l].reshape(1, -1),
        "w_qb": qb.astype(BF16), "w_k": wk.astype(BF16), "w_v": wv.astype(BF16),
        "w_branch": w_branch[l].astype(BF16), "w_out": w_out[l].astype(BF16),
        "hg_gain": jnp.tile(hg_gain[l], HG_HEADS).reshape(1, -1), "avg": avg,
    }


def _seg(arr, width, col, rows, stride, off):
    return (arr, width, col, rows, stride, off)


def kernel(x_prompt, x_sample, state_hgrn, cache_gqa_k, cache_gqa_v, cache_na_k, cache_na_v, cache_mla_ckv, cache_mla_krope, c, c_ctx, w_ada, b_ada, w_in, hg_lb_logits, hg_gain, gq_q_gain, gq_k_gain, na_rpb, ml_q_a_gain, ml_kv_a_gain, ml_w_q_b, ml_w_kv_b, w_branch, w_out, w_ffn_in, w_ffn_out, final_gain):
    cond8 = jnp.concatenate([c_ctx[None, :], c, jnp.zeros((8 - 1 - DEC_BATCH, D_MODEL), F32)], axis=0)
    mods = _ada(cond8, w_ada, b_ada)

    lb = jnp.cumsum(jax.nn.softmax(hg_lb_logits.astype(F32), axis=0), axis=0)
    lb = lb - lb[:1]
    avg = jnp.asarray(np.kron(np.eye(512 // GQ_HD), np.full((GQ_HD, GQ_HD), 1.0 / GQ_HD)), BF16)
    tabs = _rope_tables()
    w_in16 = w_in.astype(BF16)

    gqk_c = cache_gqa_k.reshape(DEC_BATCH * DEPTH * PAST_LEN, GQ_KV_HEADS * GQ_HD)
    gqv_c = cache_gqa_v.reshape(DEC_BATCH * DEPTH * PAST_LEN, GQ_KV_HEADS * GQ_HD)
    nak_c = cache_na_k.reshape(DEC_BATCH * DEPTH * PAST_LEN * NA_HEADS, NA_HD)
    nav_c = cache_na_v.reshape(DEC_BATCH * DEPTH * PAST_LEN * NA_HEADS, NA_HD)
    mckv_c = cache_mla_ckv.reshape(DEC_BATCH * DEPTH * PAST_LEN, ML_KV_RANK)
    mkr_c = jnp.pad(cache_mla_krope.reshape(DEC_BATCH * DEPTH * PAST_LEN, ML_ROPE),
                    ((0, 0), (0, LANE - ML_ROPE)))

    xp = x_prompt.reshape(M_ROWS, D_MODEL)
    xs = x_sample.reshape(M_ROWS, D_MODEL)
    new = []
    ymix_ctx = []
    for l in range(DEPTH):
        lw = _layer_weights(l, w_in16, gq_q_gain, gq_k_gain, ml_q_a_gain, ml_kv_a_gain, ml_w_q_b,
                            ml_w_kv_b, w_branch, w_out, hg_gain, avg)
        last = l == DEPTH - 1
        mod = mods[l]

        ymix = _inproj_mix(xp, mod, w_in16, l, False)
        gates = _inproj(xp, mod, lw["w_gate"], False, 2048, "inproj_gate_ctx", gate=True)
        qb, kb, qd, ckv, kd, vd = _prep(ymix, lw, tabs, False)
        out_a, st = _hgrn(ymix, lb[l], lw["hg_gain"], BATCH, SEQ, None)
        out_b = _attention((qb, 512, 0, 0),
                           [(_seg(kb, LANE, 0, SEQ, 1, 0), _seg(ymix, LANE, COL_GV, SEQ, 1, 0))],
                           _plan_gqa(), GQ_SCALE, BATCH, SEQ, "gqa_ctx")
        out_c = _attention((ymix, 512, COL_NQ, 0),
                           [(_seg(ymix, 512, COL_NK, SEQ, 1, 0), _seg(ymix, 512, COL_NV, SEQ, 1, 0))],
                           _plan_na(), NA_SCALE, BATCH, SEQ, "na_ctx")
        out_d = _attention((qd, ML_HEADS * LANE, 0, 0),
                           [(_seg(kd, ML_HEADS * LANE, 0, SEQ, 1, 0), _seg(vd, 512, 0, SEQ, 1, 0))],
                           _plan_mla(), ML_SCALE, BATCH, SEQ, "mla_ctx")
        xp = _merge((out_a, out_b, out_c, out_d), gates, lw["w_branch"], lw["w_out"], xp, mod, False)
        xp = _ffn(xp, mod, w_ffn_in, w_ffn_out, l, False, final_gain if last else None)
        new.append((
            jnp.swapaxes(st, -1, -2),
            kb.reshape(BATCH, SEQ, GQ_KV_HEADS, GQ_HD),
            ymix[:, COL_GV * LANE:(COL_GV + 1) * LANE].reshape(BATCH, SEQ, GQ_KV_HEADS, GQ_HD),
            ckv.reshape(BATCH, SEQ, ML_KV_RANK),
            ymix[:, COL_MKR * LANE:COL_MKR * LANE + ML_ROPE].reshape(BATCH, SEQ, ML_ROPE),
        ))
        ymix_ctx.append(ymix)

        ymix = _inproj_mix(xs, mod, w_in16, l, True)
        gates = _inproj(xs, mod, lw["w_gate"], True, 2048, "inproj_gate_lat", gate=True)
        qb, kb, qd, ckv, kd, vd = _prep(ymix, lw, tabs, True)
        kd_c, vd_c = _mla_cache(mckv_c, mkr_c, lw, l)
        s0_t = jnp.swapaxes(state_hgrn[:, l], -1, -2)
        out_a, = _hgrn(ymix, lb[l], lw["hg_gain"], DEC_BATCH, DEC_SEQ, s0_t)
        out_b = _attention((qb, 512, 0, 0),
                           [(_seg(gqk_c, LANE, 0, PAST_LEN, DEPTH, l), _seg(gqv_c, LANE, 0, PAST_LEN, DEPTH, l)),
                            (_seg(kb, LANE, 0, DEC_SEQ, 1, 0), _seg(ymix, LANE, COL_GV, DEC_SEQ, 1, 0))],
                           _plan_gqa(), GQ_SCALE, DEC_BATCH, DEC_SEQ, "gqa_lat")
        out_c = _na_latent(ymix, nak_c, nav_c, _na_bias_tables(na_rpb[l]), l)
        out_d = _attention((qd, ML_HEADS * LANE, 0, 0),
                           [(_seg(kd_c, ML_HEADS * LANE, 0, PAST_LEN, 1, 0), _seg(vd_c, 512, 0, PAST_LEN, 1, 0)),
                            (_seg(kd, ML_HEADS * LANE, 0, DEC_SEQ, 1, 0), _seg(vd, 512, 0, DEC_SEQ, 1, 0))],
                           _plan_mla(), ML_SCALE, DEC_BATCH, DEC_SEQ, "mla_lat")
        xs = _merge((out_a, out_b, out_c, out_d), gates, lw["w_branch"], lw["w_out"], xs, mod, True)
        xs = _ffn(xs, mod, w_ffn_in, w_ffn_out, l, True, final_gain if last else None)

    y_prompt = xp.reshape(BATCH, SEQ, D_MODEL)
    y_sample = xs.reshape(DEC_BATCH, DEC_SEQ, D_MODEL)
    state, gqa_k, gqa_v, ckv_new, krope_new = (jnp.stack([n[i] for n in new], axis=1) for i in range(5))
    na_k, na_v = _na_cache_out(ymix_ctx)
    return (y_prompt, y_sample, state, gqa_k, gqa_v, na_k, na_v, ckv_new, krope_new)
```

```python
import functools

import numpy as np
import jax
import jax.numpy as jnp
from jax import lax
from jax.experimental import pallas as pl
from jax.experimental.pallas import tpu as pltpu

F32 = jnp.float32
BF16 = jnp.bfloat16

D_MODEL = 1024
BATCH = 16
SEQ = 256
DEPTH = 2
DEC_BATCH = 4
DEC_SEQ = 1024
PAST_LEN = 512
GRID_W = 64
EPS = 1e-6
ROPE_THETA = 10000.0
N_BRANCH = 4
BRANCH_W = 512
HG_HEADS = 4
HG_DK = 128
HG_DV = 128
GQ_HEADS = 8
GQ_KV_HEADS = 2
GQ_HD = 64
NA_HEADS = 8
NA_HD = 64
NA_WIN_R = 8
NA_WIN_C = 16
ML_HEADS = 8
ML_NOPE = 64
ML_ROPE = 32
ML_V = 64
ML_Q_RANK = 256
ML_KV_RANK = 128
FFN_HIDDEN = 2816
GQ_SCALE = GQ_HD ** -0.5
NA_SCALE = NA_HD ** -0.5
ML_SCALE = (ML_NOPE + ML_ROPE) ** -0.5

M_ROWS = BATCH * SEQ
assert M_ROWS == DEC_BATCH * DEC_SEQ

LANE = 128
HALF = 64
MIX_W = 5376
IN_MIX = 5280
GATE_W = N_BRANCH * D_MODEL
HG_CHUNK = 16
NEG = -1e30
LOG2E = 1.4426950408889634
VMEM_LIMIT = 56 * 1024 * 1024

COL_HQ, COL_HFF, COL_HFB, COL_HI, COL_HG, COL_GQ, COL_NQ, COL_NK, COL_NV = range(9)
COL_MQA = 18
COL_GK, COL_GV, COL_MKVA, COL_MKR = 38, 39, 40, 41


def _dot(a, b):
    return jnp.dot(a, b, preferred_element_type=F32)


def _dot_nt(a, b):
    return lax.dot_general(a, b, (((1,), (1,)), ((), ())), preferred_element_type=F32)


def _dot_tn(a, b):
    return lax.dot_general(a, b, (((0,), (0,)), ((), ())), preferred_element_type=F32)


def _split3(x):
    x1 = x.astype(BF16)
    r1 = x - x1.astype(F32)
    x2 = r1.astype(BF16)
    x3 = (r1 - x2.astype(F32)).astype(BF16)
    return x1, x2, x3


def _dot_exact_lhs(a_bf16, x):
    x1, x2, x3 = _split3(x)
    return (_dot(a_bf16, x3) + _dot(a_bf16, x2)) + _dot(a_bf16, x1)


def _dot_exact_rhs(x, b_bf16):
    x1, x2, x3 = _split3(x)
    return (_dot(x3, b_bf16) + _dot(x2, b_bf16)) + _dot(x1, b_bf16)


def _rms(x):
    return x * lax.rsqrt(jnp.mean(x * x, axis=-1, keepdims=True) + EPS)


def _silu(x):
    return x * jax.nn.sigmoid(x)


def _aligned(x, m):
    return x if isinstance(x, int) else pl.multiple_of(x, m)


def _params(*sem):
    return pltpu.CompilerParams(dimension_semantics=sem, vmem_limit_bytes=VMEM_LIMIT)


def _mod_row(latent, tm):
    if latent:
        return lambda i: 1 + (i * tm) // DEC_SEQ
    return lambda i: 0


def _ada_kernel(c_ref, w_ref, b_ref, o_ref):
    c = c_ref[...]
    o_ref[...] = _dot(_silu(c).astype(BF16), w_ref[...].astype(BF16)) + b_ref[...]


def _ada(cond8, w_ada, b_ada):
    tn = 1536
    out = pl.pallas_call(
        _ada_kernel,
        grid=(DEPTH, 6 * D_MODEL // tn),
        in_specs=[
            pl.BlockSpec((8, D_MODEL), lambda l, j: (0, 0)),
            pl.BlockSpec((None, D_MODEL, tn), lambda l, j: (l, 0, j)),
            pl.BlockSpec((None, 1, tn), lambda l, j: (l, 0, j)),
        ],
        out_specs=pl.BlockSpec((None, 8, tn), lambda l, j: (l, 0, j)),
        out_shape=jax.ShapeDtypeStruct((DEPTH, 8, 6 * D_MODEL), F32),
        compiler_params=_params("parallel", "parallel"),
        name="ada",
    )(cond8, w_ada, b_ada.reshape(DEPTH, 1, 6 * D_MODEL))
    return out.reshape(DEPTH, 8, 6, 1, D_MODEL)


IN_PIECE = 256
IN_TM = 1024


def _mix_source(t):
    blk = jnp.where(t < 12, t, jnp.where(t < 18, t + 1, jnp.where(t == 18, 19, jnp.where(t == 19, 12, 20))))
    return blk * IN_PIECE


def _gate_source(t):
    return IN_MIX + t * IN_PIECE


def _inproj_kernel(*refs, pieces, gate):
    x_ref, sc_ref, sh_ref = refs[:3]
    w_refs = refs[3:3 + pieces]
    o_ref, h_ref = refs[3 + pieces:]
    rows = pl.ds(pl.multiple_of(pl.program_id(1) * IN_TM, IN_TM), IN_TM)

    @pl.when(pl.program_id(0) == 0)
    def _():
        h = _rms(x_ref[...]) * (1.0 + sc_ref[...]) + sh_ref[...]
        h_ref[rows, :] = h.astype(BF16)

    h = h_ref[rows, :]
    for p, w_ref in enumerate(w_refs):
        y = _dot_nt(h, w_ref[...].astype(BF16))
        o_ref[:, p * IN_PIECE:(p + 1) * IN_PIECE] = jax.nn.sigmoid(y).astype(o_ref.dtype) if gate else y


def _inproj(x, mod, w_in_t, layer, latent, gate):
    width, tile, source = (GATE_W, 2048, _gate_source) if gate else (MIX_W, 1792, _mix_source)
    pieces = tile // IN_PIECE
    row = _mod_row(latent, IN_TM)
    first = lambda c, i: jnp.where(c == 0, i, 0)
    in_width = w_in_t.shape[0] // DEPTH
    piece = lambda p: pl.BlockSpec((pl.Element(IN_PIECE), pl.Element(D_MODEL)),
                                   lambda c, i: (pl.multiple_of(layer * in_width + source(c * pieces + p), 8), 0))
    return pl.pallas_call(
        functools.partial(_inproj_kernel, pieces=pieces, gate=gate),
        grid=(width // tile, M_ROWS // IN_TM),
        in_specs=[
            pl.BlockSpec((IN_TM, D_MODEL), lambda c, i: (first(c, i), 0)),
            pl.BlockSpec((None, None, 1, D_MODEL), lambda c, i: (row(first(c, i)), 1, 0, 0)),
            pl.BlockSpec((None, None, 1, D_MODEL), lambda c, i: (row(first(c, i)), 0, 0, 0)),
        ] + [piece(p) for p in range(pieces)],
        out_specs=pl.BlockSpec((IN_TM, tile), lambda c, i: (i, c)),
        out_shape=jax.ShapeDtypeStruct((M_ROWS, width), BF16 if gate else F32),
        scratch_shapes=[pltpu.VMEM((M_ROWS, D_MODEL), BF16)],
        compiler_params=_params("arbitrary", "arbitrary"),
        name="inproj_" + ("gate_" if gate else "mix_") + ("lat" if latent else "ctx"),
    )(x, mod, mod, *([w_in_t] * pieces))


def _rope(x, tab_ref, reps):
    w = x.shape[1]
    c = jnp.concatenate([tab_ref[0]] * reps, axis=1) if reps > 1 else tab_ref[0]
    se = jnp.concatenate([tab_ref[1]] * reps, axis=1) if reps > 1 else tab_ref[1]
    so = jnp.concatenate([tab_ref[2]] * reps, axis=1) if reps > 1 else tab_ref[2]
    return x * c + pltpu.roll(x, w - 1, 1) * se + pltpu.roll(x, 1, 1) * so


def _head_rms(x, gain, avg_bf16):
    ms = _dot_exact_rhs(x * x, avg_bf16)
    return x * lax.rsqrt(ms + EPS) * gain


def _prep_kernel(*refs, rope):
    (gq_ref, gk_ref, mqa_ref, mkva_ref, mkr_ref, gqg_ref, gkg_ref, mqg_ref, mkvg_ref,
     wqb_ref, wk_ref, wv_ref, avg_ref) = refs[:13]
    if rope:
        gtab_ref, mtab_ref = refs[13:15]
        outs = refs[15:]
    else:
        outs = refs[13:]
    qb_ref, kb_ref, qd_ref, ckv_ref, kd_ref, vd_ref = outs

    q = _head_rms(gq_ref[...], gqg_ref[...], avg_ref[...])
    k = _head_rms(gk_ref[...], gkg_ref[...], avg_ref[0:LANE, 0:LANE])
    qd = _dot((_rms(mqa_ref[...]) * mqg_ref[...]).astype(BF16), wqb_ref[...])
    ckv = _rms(mkva_ref[...]) * mkvg_ref[...]
    lane = lax.broadcasted_iota(jnp.int32, (1, LANE), 1)
    kr = jnp.where(lane < ML_ROPE, mkr_ref[...], 0.0)
    if rope:
        q = _rope(q, gtab_ref, GQ_HEADS * GQ_HD // LANE)
        k = _rope(k, gtab_ref, 1)
        qd = _rope(qd, mtab_ref, ML_HEADS)
        kr = _rope(kr, mtab_ref, 1)
    qb_ref[...] = q.astype(BF16)
    kb_ref[...] = k
    qd_ref[...] = qd.astype(BF16)
    ckv_ref[...] = ckv
    cb = ckv.astype(BF16)
    kd_ref[...] = (_dot(cb, wk_ref[...]) + jnp.concatenate([kr] * ML_HEADS, axis=1)).astype(BF16)
    vd_ref[...] = _dot(cb, wv_ref[...]).astype(BF16)


def _prep(ymix, lw, tabs, latent):
    tm = 512
    const = lambda i: (0, 0)
    in_specs = [
        pl.BlockSpec((tm, 512), lambda i: (i, COL_GQ)),
        pl.BlockSpec((tm, LANE), lambda i: (i, COL_GK)),
        pl.BlockSpec((tm, 256), lambda i: (i, COL_MQA)),
        pl.BlockSpec((tm, LANE), lambda i: (i, COL_MKVA)),
        pl.BlockSpec((tm, LANE), lambda i: (i, COL_MKR)),
        pl.BlockSpec((1, 512), const),
        pl.BlockSpec((1, LANE), const),
        pl.BlockSpec((1, 256), const),
        pl.BlockSpec((1, LANE), const),
        pl.BlockSpec((ML_Q_RANK, ML_HEADS * LANE), const),
        pl.BlockSpec((ML_KV_RANK, ML_HEADS * LANE), const),
        pl.BlockSpec((ML_KV_RANK, ML_HEADS * ML_V), const),
        pl.BlockSpec((512, 512), const),
    ]
    args = [ymix, ymix, ymix, ymix, ymix, lw["gq_q_gain"], lw["gq_k_gain"], lw["ml_q_gain"],
            lw["ml_kv_gain"], lw["w_qb"], lw["w_k"], lw["w_v"], lw["avg"]]
    if latent:
        per = DEC_SEQ // tm
        in_specs += [pl.BlockSpec((3, tm, LANE), lambda i: (0, i % per, 0))] * 2
        args += [tabs["gq"], tabs["ml"]]
    widths = (512, LANE, ML_HEADS * LANE, LANE, ML_HEADS * LANE, ML_HEADS * ML_V)
    return pl.pallas_call(
        functools.partial(_prep_kernel, rope=latent),
        grid=(M_ROWS // tm,),
        in_specs=in_specs,
        out_specs=[pl.BlockSpec((tm, w), lambda i: (i, 0)) for w in widths],
        out_shape=[jax.ShapeDtypeStruct((M_ROWS, w), dt)
                   for w, dt in zip(widths, (BF16, F32, BF16, F32, BF16, BF16))],
        compiler_params=_params("parallel"),
        name="prep_lat" if latent else "prep_ctx",
    )(*args)


def _mla_cache_kernel(ckv_ref, kr_ref, wk_ref, wv_ref, kd_ref, vd_ref):
    cb = ckv_ref[...].astype(BF16)
    kd_ref[...] = (_dot(cb, wk_ref[...]) + jnp.concatenate([kr_ref[...]] * ML_HEADS, axis=1)).astype(BF16)
    vd_ref[...] = _dot(cb, wv_ref[...]).astype(BF16)


def _mla_cache(ckv, kr_blk, lw, layer):
    rows = DEC_BATCH * PAST_LEN
    tm = PAST_LEN
    const = lambda i: (0, 0)
    return pl.pallas_call(
        _mla_cache_kernel,
        grid=(DEC_BATCH,),
        in_specs=[
            pl.BlockSpec((tm, LANE), lambda i: (i * DEPTH + layer, 0)),
            pl.BlockSpec((tm, LANE), lambda i: (i * DEPTH + layer, 0)),
            pl.BlockSpec((ML_KV_RANK, ML_HEADS * LANE), const),
            pl.BlockSpec((ML_KV_RANK, ML_HEADS * ML_V), const),
        ],
        out_specs=[pl.BlockSpec((tm, ML_HEADS * LANE), lambda i: (i, 0)),
                   pl.BlockSpec((tm, ML_HEADS * ML_V), lambda i: (i, 0))],
        out_shape=[jax.ShapeDtypeStruct((rows, ML_HEADS * LANE), BF16),
                   jax.ShapeDtypeStruct((rows, ML_HEADS * ML_V), BF16)],
        compiler_params=_params("parallel"),
        name="mla_cache",
    )(ckv, kr_blk, lw["w_k"], lw["w_v"])


def _softmax_pv(scores, values, transposed=None):
    transposed = transposed or [False] * len(values)
    m = scores[0].max(axis=-1, keepdims=True)
    for s in scores[1:]:
        m = jnp.maximum(m, s.max(axis=-1, keepdims=True))
    den = None
    out = None
    for s, v, vt in zip(scores, values, transposed):
        p = jnp.exp(s - m)
        d = p.sum(axis=-1, keepdims=True)
        o = _dot_nt(p.astype(BF16), v) if vt else _dot(p.astype(BF16), v)
        den = d if den is None else den + d
        out = o if out is None else out + o
    return out / den


def _lane_halves():
    lane = lax.broadcasted_iota(jnp.int32, (1, LANE), 1)
    return lane < HALF, lane >= HALF


def _attn_kernel(*refs, plan, nseg, scale, fold_scale):
    q_ref = refs[0]
    k_refs = [refs[1 + 2 * i] for i in range(nseg)]
    v_refs = [refs[2 + 2 * i] for i in range(nseg)]
    o_ref = refs[1 + 2 * nseg]
    tq = q_ref.shape[0]
    lo, hi = _lane_halves()
    cache = {}

    def block(kind, seg, blk, swap):
        key = (kind, seg, blk, swap)
        if key not in cache:
            ref = (k_refs if kind == "k" else v_refs)[seg]
            x = ref[:, blk * LANE:(blk + 1) * LANE]
            if swap:
                x = pltpu.roll(x, HALF, 1)
            cache[key] = x.astype(BF16)
        return cache[key]

    parts = {}
    for members, kblk, swap, vblk in plan:
        qs = []
        for qblk, qhalf, _, _ in members:
            q = q_ref[:, qblk * LANE:(qblk + 1) * LANE]
            if fold_scale:
                q = q * scale
            if qhalf is not None:
                q = jnp.where(lo if qhalf == 0 else hi, q, 0.0)
            qs.append(q)
        q = (jnp.concatenate(qs, axis=0) if len(qs) > 1 else qs[0]).astype(BF16)
        scores = [_dot_nt(q, block("k", s, kblk, swap)) for s in range(nseg)]
        if not fold_scale:
            scores = [s * scale for s in scores]
        out = _softmax_pv(scores, [block("v", s, vblk, swap) for s in range(nseg)])
        for n, (_, _, oblk, ohalf) in enumerate(members):
            parts[(oblk, ohalf)] = out[n * tq:(n + 1) * tq]
    for oblk in sorted({key[0] for key in parts}):
        o_ref[:, oblk * LANE:(oblk + 1) * LANE] = jnp.where(lo, parts[(oblk, 0)], parts[(oblk, 1)]).astype(BF16)


def _plan_gqa():
    per_kv = GQ_HEADS // GQ_KV_HEADS
    plan = []
    for g in range(GQ_KV_HEADS):
        for half in range(2):
            heads = [h for h in range(g * per_kv, (g + 1) * per_kv) if h % 2 == half]
            plan.append((tuple((h // 2, half, h // 2, half) for h in heads), 0, half != g, 0))
    return tuple(plan)


def _plan_na():
    return tuple((((j, 0, j, 0), (j, 1, j, 1)), j, False, j) for j in range(NA_HEADS // 2))


def _plan_mla():
    return tuple((((h, None, h // 2, h % 2),), h, False, h // 2) for h in range(ML_HEADS))


def _is_pow2(x):
    return float(np.log2(x)).is_integer()


def _attention(q, segs, plan, scale, batch, tq_total, name):
    tq = min(tq_total, 256)
    per = tq_total // tq
    q_arr, q_w, q_col, q_off = q
    in_specs = [pl.BlockSpec((tq, q_w), lambda b, i: (q_off + b * per + i, q_col))]
    args = [q_arr]
    for k, v in segs:
        for arr, w, col, rows, stride, off in (k, v):
            in_specs.append(pl.BlockSpec(
                (rows, w), functools.partial(lambda b, i, col, stride, off: (off + b * stride, col),
                                             col=col, stride=stride, off=off)))
            args.append(arr)
    out_w = 512
    return pl.pallas_call(
        functools.partial(_attn_kernel, plan=plan, nseg=len(segs), scale=scale, fold_scale=_is_pow2(scale)),
        grid=(batch, per),
        in_specs=in_specs,
        out_specs=pl.BlockSpec((tq, out_w), lambda b, i: (b * per + i, 0)),
        out_shape=jax.ShapeDtypeStruct((M_ROWS, out_w), BF16),
        compiler_params=_params("parallel", "parallel"),
        name=name,
    )(*args)


NA_DR = 2 * NA_WIN_R - 1
NA_DC = 2 * NA_WIN_C - 1
NA_ROWS = DEC_SEQ // GRID_W
NA_QROWS = 4
NA_WROWS = 12
NA_T_RIGHT = NA_DR - 1
NA_T_LEFT = NA_DR
NA_T_NONE = NA_DR + 1
NA_T_SIZE = NA_DR + 2
assert _is_pow2(NA_SCALE)


def _na_kernel(q_ref, kl_ref, vl_ref, kc_ref, vc_ref, tz_ref, o_ref, kc_scr, vc_scr):
    g = pl.program_id(1)

    @pl.when(g == 0)
    def _():
        kc_scr[...] = kc_ref[...].astype(BF16)
        vc_scr[...] = vc_ref[...].astype(BF16)

    w0 = (g // 2) * (NA_ROWS - NA_WROWS)
    k0 = pl.multiple_of(w0 * GRID_W, (NA_ROWS - NA_WROWS) * GRID_W)
    win = NA_WROWS * GRID_W
    nq = NA_QROWS * GRID_W
    entry = []
    for a in range(NA_QROWS):
        rq = g * NA_QROWS + a
        start = jnp.clip(rq - NA_WIN_R // 2, 0, NA_ROWS - NA_WIN_R)
        per_pair = []
        for i in range(NA_WROWS // 2):
            kr = w0 + 2 * i
            in_l = jnp.logical_and(kr >= start, kr < start + NA_WIN_R)
            in_r = jnp.logical_and(kr + 1 >= start, kr + 1 < start + NA_WIN_R)
            both = jnp.logical_and(in_l, in_r)
            d_l = kr - rq + NA_WIN_R - 1
            per_pair.append(jnp.where(both, d_l, jnp.where(in_r, NA_T_RIGHT, jnp.where(in_l, NA_T_LEFT, NA_T_NONE))))
        entry.append(per_pair)
    lo, hi = _lane_halves()
    for j in range(NA_HEADS // 2):
        sl = slice(j * LANE, (j + 1) * LANE)
        k_loc = kl_ref[pl.ds(k0, win), sl].astype(BF16)
        v_loc = vl_ref[pl.ds(k0, win), sl].astype(BF16)
        k_ctx_t = kc_scr[sl, :]
        v_ctx_t = vc_scr[sl, :]
        q_pair = q_ref[:, sl] * NA_SCALE
        q = jnp.concatenate([jnp.where(lo, q_pair, 0.0), jnp.where(hi, q_pair, 0.0)], axis=0).astype(BF16)
        bias = jnp.concatenate(
            [jnp.concatenate([tz_ref[2 * j + p, e] for e in entry[a]], axis=1)
             for p in range(2) for a in range(NA_QROWS)], axis=0)
        s_loc = _dot_nt(q, k_loc) + bias
        s_ctx = _dot(q, k_ctx_t)
        out = _softmax_pv([s_loc, s_ctx], [v_loc, v_ctx_t], [False, True])
        o_ref[:, sl] = jnp.where(lo, out[:nq], out[nq:]).astype(BF16)


def _na_bias_kernel(rpb_ref, sel_ref, neg_ref, o_ref):
    o_ref[...] = _dot_exact_rhs(rpb_ref[...], sel_ref[...]) + neg_ref[...]


def _na_bias_tables(rpb):
    col = np.arange(GRID_W)
    col_start = np.clip(col - NA_WIN_C // 2, 0, GRID_W - NA_WIN_C)
    col_ok = (col[None, :] >= col_start[:, None]) & (col[None, :] < col_start[:, None] + NA_WIN_C)
    dc = col[None, :] - col[:, None] + NA_WIN_C - 1
    kpad = 32
    sel = (dc[None, :, :] == np.arange(kpad)[:, None, None]) & col_ok[None]
    sel = jnp.asarray(sel.reshape(kpad, GRID_W * GRID_W), BF16)
    neg = jnp.asarray(np.where(col_ok, 0.0, NEG).reshape(1, GRID_W * GRID_W), F32)
    rpb2 = jnp.pad(rpb.reshape(NA_HEADS * NA_DR, NA_DC), ((0, 0), (0, kpad - NA_DC)))
    n_rows = NA_HEADS * NA_DR
    full = lambda shape: pl.BlockSpec(shape, lambda i: (0, 0))
    t = pl.pallas_call(
        _na_bias_kernel,
        grid=(1,),
        in_specs=[full((n_rows, kpad)), full((kpad, GRID_W * GRID_W)), full((1, GRID_W * GRID_W))],
        out_specs=full((n_rows, GRID_W * GRID_W)),
        out_shape=jax.ShapeDtypeStruct((n_rows, GRID_W * GRID_W), F32),
        compiler_params=_params("arbitrary"),
        name="na_bias",
    )(rpb2, sel, neg)
    t = t.reshape(NA_HEADS, NA_DR, GRID_W, GRID_W)
    masked = jnp.full((NA_HEADS, 1, GRID_W, GRID_W), NEG, F32)
    first, last = NA_WIN_R // 2 - 1, NA_WIN_R // 2 + NA_WIN_R - 2
    return jnp.concatenate([
        jnp.concatenate([t[:, :-1], t[:, 1:]], axis=-1),
        jnp.concatenate([masked, t[:, first:first + 1]], axis=-1),
        jnp.concatenate([t[:, last:last + 1], masked], axis=-1),
        jnp.concatenate([masked, masked], axis=-1)], axis=1)


def _na_latent(ymix, cache_k, cache_v, tz, layer):
    groups = NA_ROWS // NA_QROWS
    nq = NA_QROWS * GRID_W
    return pl.pallas_call(
        _na_kernel,
        grid=(DEC_BATCH, groups),
        in_specs=[
            pl.BlockSpec((nq, 512), lambda b, g: (b * groups + g, COL_NQ)),
            pl.BlockSpec((DEC_SEQ, 512), lambda b, g: (b, COL_NK)),
            pl.BlockSpec((DEC_SEQ, 512), lambda b, g: (b, COL_NV)),
            pl.BlockSpec((NA_HEADS * NA_HD, PAST_LEN), lambda b, g: (b * DEPTH + layer, 0)),
            pl.BlockSpec((NA_HEADS * NA_HD, PAST_LEN), lambda b, g: (b * DEPTH + layer, 0)),
            pl.BlockSpec((NA_HEADS, NA_T_SIZE, GRID_W, LANE), lambda b, g: (0, 0, 0, 0)),
        ],
        out_specs=pl.BlockSpec((nq, 512), lambda b, g: (b * groups + g, 0)),
        out_shape=jax.ShapeDtypeStruct((M_ROWS, 512), BF16),
        scratch_shapes=[pltpu.VMEM((NA_HEADS * NA_HD, PAST_LEN), BF16)] * 2,
        compiler_params=_params("parallel", "arbitrary"),
        name="na_lat",
    )(ymix, ymix, ymix, cache_k, cache_v, tz)


def _na_cache_out_kernel(*refs):
    srcs, (ko_ref, vo_ref) = refs[:2 * DEPTH], refs[2 * DEPTH:]
    layer = pl.program_id(1)
    for l in range(DEPTH):
        @pl.when(layer == l)
        def _():
            ko_ref[...] = srcs[2 * l][...].T
            vo_ref[...] = srcs[2 * l + 1][...].T


def _na_cache_out(ymix_layers):
    rows = NA_HEADS * NA_HD
    in_specs, args = [], []
    for ymix in ymix_layers:
        for col in (COL_NK, COL_NV):
            in_specs.append(pl.BlockSpec((SEQ, 512), functools.partial(lambda b, l, col: (b, col), col=col)))
            args.append(ymix)
    out = pl.pallas_call(
        _na_cache_out_kernel,
        grid=(BATCH, DEPTH),
        in_specs=in_specs,
        out_specs=[pl.BlockSpec((rows, SEQ), lambda b, l: (b * DEPTH + l, 0))] * 2,
        out_shape=[jax.ShapeDtypeStruct((BATCH * DEPTH * rows, SEQ), F32)] * 2,
        compiler_params=_params("parallel", "arbitrary"),
        name="na_cache_out",
    )(*args)
    return tuple(o.reshape(BATCH, DEPTH, NA_HEADS, NA_HD, SEQ).transpose(0, 1, 4, 2, 3) for o in out)


HG_BLOCK = 64
HG_STATE_UNROLL = 8
HG_DEC_ROWS = 8


def _hgrn_kernel(*refs, seq, has_state):
    if has_state:
        (q_ref, ff_ref, fb_ref, v_ref, g_ref, lb_ref, gain_ref, s0_ref,
         o_ref, of_ref, ob_ref, qin_ref, kend_ref, dec_ref, st_ref) = refs
    else:
        (q_ref, ff_ref, fb_ref, v_ref, g_ref, lb_ref, gain_ref,
         o_ref, sout_ref, of_ref, ob_ref, qin_ref, kend_ref, dec_ref, st_ref) = refs
    c = HG_CHUNK
    hc = c // 2
    rb = HG_BLOCK
    per_block = rb // c
    n_chunks = seq // c
    width = HG_HEADS * HG_DK
    sub = HG_DEC_ROWS

    ri = lax.broadcasted_iota(jnp.int32, (rb, rb), 0)
    ci = lax.broadcasted_iota(jnp.int32, (rb, rb), 1)
    same = (ri // c) == (ci // c)
    tri_f = jnp.where(jnp.logical_and(same, ci <= ri), 1.0, 0.0).astype(BF16)
    tri_b = jnp.where(jnp.logical_and(same, ci >= ri), 1.0, 0.0).astype(BF16)
    ones = jnp.ones((HG_DK, HG_DV), BF16)
    rowid = lax.broadcasted_iota(jnp.int32, (c, width), 0)
    rowid_half = lax.broadcasted_iota(jnp.int32, (hc, width), 0)
    o_refs = (of_ref, ob_ref)

    n_blocks = seq // rb

    def block_of(d, step):
        return step if d == 0 else n_blocks - 1 - step

    def pair_block(step):
        for d, (pre_ref, tri) in enumerate(((ff_ref, tri_f), (fb_ref, tri_b))):
            fwd = d == 0
            blk = block_of(d, step)
            r0 = _aligned(blk * rb, rb)
            q_all = q_ref[pl.ds(r0, rb), :] * (HG_DK ** -0.5)
            v_all = v_ref[pl.ds(r0, rb), :]
            lb = lb_ref[d:d + 1, :]
            f = lb + (1.0 - lb) * jax.nn.sigmoid(pre_ref[pl.ds(r0, rb), :])
            k_all = 1.0 - f
            b_all = _dot_exact_lhs(tri, jnp.log(f)) * LOG2E
            src_all = b_all - jnp.log(k_all) * LOG2E
            wide = range(0, hc) if fwd else range(hc, c)
            narrow = range(hc, c) if fwd else range(0, hc)
            far = slice(hc, c) if fwd else slice(0, hc)
            row_far = rowid_half + (hc if fwd else 0)
            blocks, q_in, k_end = [], [], []
            for m in range(per_block):
                rows = slice(m * c, (m + 1) * c)
                q, k, b, src = q_all[rows], k_all[rows], b_all[rows], src_all[rows]
                b_last = b[c - 1:c] if fwd else b[0:1]
                q_in.append(q * jnp.exp2(b))
                k_end.append(k * jnp.exp2(b_last - b))
                dec_row = _aligned((blk * per_block + m) * sub, sub)
                dec_ref[d, pl.ds(dec_row, sub), :] = jnp.broadcast_to(jnp.exp2(b_last), (sub, width))
                for s in wide:
                    p = q * jnp.exp2(b - src[s:s + 1])
                    keep = (rowid >= s) if fwd else (rowid <= s)
                    blocks.append(jnp.where(keep, p, 0.0).astype(BF16))
                q_far, b_far = q[far], b[far]
                halves = []
                for s in narrow:
                    p = q_far * jnp.exp2(b_far - src[s:s + 1])
                    keep = (row_far >= s) if fwd else (row_far <= s)
                    halves.append(jnp.where(keep, p, 0.0))
                for i in range(0, hc, 2):
                    blocks.append(jnp.concatenate(halves[i:i + 2], axis=0).astype(BF16))
            qin_ref[d, pl.ds(r0, rb), :] = jnp.concatenate(q_in, axis=0).astype(BF16)
            kend_ref[d, pl.ds(r0, rb), :] = jnp.concatenate(k_end, axis=0).astype(BF16)
            pairs = jnp.concatenate(blocks, axis=0)
            per_chunk = hc * c + hc * hc
            outs = []
            for h in range(HG_HEADS):
                sl = slice(h * HG_DK, (h + 1) * HG_DK)
                a = _dot(pairs[:, sl], ones)
                chunks = []
                for m in range(per_block):
                    base = m * per_chunk
                    v = v_all[m * c:(m + 1) * c]
                    o = None
                    for n, s in enumerate(wide):
                        term = a[base + n * c:base + (n + 1) * c] * v[s:s + 1, sl]
                        o = term if o is None else o + term
                    o_far = None
                    for n, s in enumerate(narrow):
                        first = base + hc * c + n * hc
                        term = a[first:first + hc] * v[s:s + 1, sl]
                        o_far = term if o_far is None else o_far + term
                    zero = jnp.zeros_like(o_far)
                    chunks.append(o + jnp.concatenate([zero, o_far] if fwd else [o_far, zero], axis=0))
                outs.append(jnp.concatenate(chunks, axis=0))
            o_refs[d][pl.ds(r0, rb), :] = jnp.concatenate(outs, axis=1)

    def state_step(n, carry):
        for d in range(2):
            chunk = n if d == 0 else n_chunks - 1 - n
            r0 = _aligned(chunk * c, c)
            q_in = qin_ref[d, pl.ds(r0, c), :]
            k_end = kend_ref[d, pl.ds(r0, c), :]
            v16 = v_ref[pl.ds(r0, c), :].astype(BF16)
            dec = dec_ref[d, pl.ds(_aligned(chunk * sub, sub), sub), :][0:1]
            outs = []
            for h in range(HG_HEADS):
                sl = slice(h * HG_DK, (h + 1) * HG_DK)
                st = st_ref[d, h]
                outs.append(_dot_nt(q_in[:, sl], st.astype(BF16)))
                st_ref[d, h] = st * dec[:, sl] + _dot_tn(v16[:, sl], k_end[:, sl])
            o_refs[d][pl.ds(r0, c), :] += jnp.concatenate(outs, axis=1)
        return carry

    if has_state:
        for d in range(2):
            for h in range(HG_HEADS):
                st_ref[d, h] = s0_ref[d, h].T
    else:
        st_ref[...] = jnp.zeros(st_ref.shape, F32)

    def pair_step(i, carry):
        pair_block(i)
        return carry

    lax.fori_loop(0, n_blocks, pair_step, 0)
    lax.fori_loop(0, n_chunks, state_step, 0, unroll=HG_STATE_UNROLL)

    for h in range(HG_HEADS):
        sl = slice(h * HG_DV, (h + 1) * HG_DV)
        o = of_ref[:, sl] + ob_ref[:, sl]
        o_ref[:, sl] = (_rms(o) * gain_ref[:, sl] * _silu(g_ref[:, sl])).astype(BF16)
    if not has_state:
        for d in range(2):
            for h in range(HG_HEADS):
                sout_ref[d, h] = st_ref[d, h].T


def _hgrn(ymix, lb, gain, batch, seq, state, layer):
    width = HG_HEADS * HG_DK
    has_state = state is not None
    assert HG_DK == HG_DV
    st_shape = (2, HG_HEADS, HG_DK, HG_DV)
    in_specs = [pl.BlockSpec((seq, width), functools.partial(lambda b, col: (b, col), col=col))
                for col in (COL_HQ, COL_HFF, COL_HFB, COL_HI, COL_HG)]
    in_specs += [pl.BlockSpec((2, width), lambda b: (0, 0)), pl.BlockSpec((1, width), lambda b: (0, 0))]
    args = [ymix] * 5 + [lb, gain]
    out_specs = [pl.BlockSpec((seq, width), lambda b: (b, 0))]
    out_shape = [jax.ShapeDtypeStruct((M_ROWS, width), BF16)]
    if has_state:
        in_specs.append(pl.BlockSpec((None, None) + st_shape, lambda b: (b, layer, 0, 0, 0, 0)))
        args.append(state)
    else:
        out_specs.append(pl.BlockSpec((None,) + st_shape, lambda b: (b, 0, 0, 0, 0)))
        out_shape.append(jax.ShapeDtypeStruct((batch,) + st_shape, F32))
    return pl.pallas_call(
        functools.partial(_hgrn_kernel, seq=seq, has_state=has_state),
        grid=(batch,),
        in_specs=in_specs,
        out_specs=out_specs,
        out_shape=out_shape,
        scratch_shapes=[pltpu.VMEM((seq, width), F32), pltpu.VMEM((seq, width), F32),
                        pltpu.VMEM((2, seq, width), BF16), pltpu.VMEM((2, seq, width), BF16),
                        pltpu.VMEM((2, seq // HG_CHUNK * HG_DEC_ROWS, width), F32),
                        pltpu.VMEM(st_shape, F32)],
        compiler_params=_params("parallel"),
        name="hgrn_lat" if has_state else "hgrn_ctx",
    )(*args)


def _merge_kernel(oa_ref, ob_ref, oc_ref, od_ref, gt_ref, wb_ref, wo_ref, x_ref, g1_ref, out_ref):
    acc = None
    for n, o_ref in enumerate((oa_ref, ob_ref, oc_ref, od_ref)):
        bo = _dot(o_ref[...], wb_ref[n])
        term = gt_ref[:, n * D_MODEL:(n + 1) * D_MODEL] * bo
        acc = term if acc is None else acc + term
    out_ref[...] = x_ref[...] + g1_ref[...] * _dot(acc.astype(BF16), wo_ref[...])


def _merge(branches, gates, w_branch, w_out, x, mod, latent):
    tm = 512
    row = _mod_row(latent, tm)
    tile = lambda w: pl.BlockSpec((tm, w), lambda i: (i, 0))
    return pl.pallas_call(
        _merge_kernel,
        grid=(M_ROWS // tm,),
        in_specs=[tile(BRANCH_W)] * N_BRANCH + [
            tile(GATE_W),
            pl.BlockSpec((N_BRANCH, BRANCH_W, D_MODEL), lambda i: (0, 0, 0)),
            pl.BlockSpec((D_MODEL, D_MODEL), lambda i: (0, 0)),
            tile(D_MODEL),
            pl.BlockSpec((None, None, 1, D_MODEL), lambda i: (row(i), 2, 0, 0)),
        ],
        out_specs=tile(D_MODEL),
        out_shape=jax.ShapeDtypeStruct((M_ROWS, D_MODEL), F32),
        compiler_params=_params("parallel"),
        name="merge_lat" if latent else "merge_ctx",
    )(*branches, gates, w_branch, w_out, x, mod)


FFN_CHUNK = 256
FFN_STEPS = FFN_HIDDEN // FFN_CHUNK
assert FFN_STEPS * FFN_CHUNK == FFN_HIDDEN


def _ffn_kernel(*refs, final):
    x_ref, sc_ref, sh_ref, g2_ref, wa_ref, wg_ref, wo_ref = refs[:7]
    if final:
        fg_ref, out_ref, h_ref, acc_ref = refs[7:]
    else:
        out_ref, h_ref, acc_ref = refs[7:]
    j = pl.program_id(1)

    @pl.when(j == 0)
    def _():
        h = _rms(x_ref[...]) * (1.0 + sc_ref[...]) + sh_ref[...]
        h_ref[...] = h.astype(BF16)
        acc_ref[...] = jnp.zeros(acc_ref.shape, F32)

    h = h_ref[...]
    act = _silu(_dot(h, wg_ref[...].astype(BF16))) * _dot(h, wa_ref[...].astype(BF16))
    acc_ref[...] += _dot(act.astype(BF16), wo_ref[...].astype(BF16))

    @pl.when(j == FFN_STEPS - 1)
    def _():
        y = x_ref[...] + g2_ref[...] * acc_ref[...]
        out_ref[...] = _rms(y) * fg_ref[...] if final else y


def _ffn(x, mod, w_in, w_out, layer, latent, final_gain):
    tm = 1024
    row = _mod_row(latent, tm)
    modspec = lambda which: pl.BlockSpec((None, None, 1, D_MODEL), lambda i, j: (row(i), which, 0, 0))
    final = final_gain is not None
    in_specs = [
        pl.BlockSpec((tm, D_MODEL), lambda i, j: (i, 0)),
        modspec(4), modspec(3), modspec(5),
        pl.BlockSpec((None, D_MODEL, FFN_CHUNK), lambda i, j: (layer, 0, j)),
        pl.BlockSpec((None, D_MODEL, FFN_CHUNK), lambda i, j: (layer, 0, FFN_STEPS + j)),
        pl.BlockSpec((None, FFN_CHUNK, D_MODEL), lambda i, j: (layer, j, 0)),
    ]
    args = [x, mod, mod, mod, w_in, w_in, w_out]
    if final:
        in_specs.append(pl.BlockSpec((1, D_MODEL), lambda i, j: (0, 0)))
        args.append(final_gain.reshape(1, D_MODEL))
    return pl.pallas_call(
        functools.partial(_ffn_kernel, final=final),
        grid=(M_ROWS // tm, FFN_STEPS),
        in_specs=in_specs,
        out_specs=pl.BlockSpec((tm, D_MODEL), lambda i, j: (i, 0)),
        out_shape=jax.ShapeDtypeStruct((M_ROWS, D_MODEL), F32),
        scratch_shapes=[pltpu.VMEM((tm, D_MODEL), BF16), pltpu.VMEM((tm, D_MODEL), F32)],
        compiler_params=_params("parallel", "arbitrary"),
        name="ffn_lat" if latent else "ffn_ctx",
    )(*args)


def _rope_tables():
    t = jnp.arange(DEC_SEQ)
    row = (t // GRID_W).astype(F32)[:, None]
    col = (t % GRID_W).astype(F32)[:, None]

    def angles(rot_dim):
        n_freq = rot_dim // 4
        inv_freq = ROPE_THETA ** (-jnp.arange(n_freq, dtype=F32) / n_freq)
        return jnp.concatenate([row * inv_freq, col * inv_freq], axis=-1)

    def expand(ang):
        cos = jnp.repeat(jnp.cos(ang), 2, axis=-1)
        sin = jnp.repeat(jnp.sin(ang), 2, axis=-1)
        even = (jnp.arange(cos.shape[-1]) % 2 == 0)[None, :]
        return cos, jnp.where(even, -sin, 0.0), jnp.where(even, 0.0, sin)

    gq = [jnp.concatenate([a, a], axis=-1) for a in expand(angles(GQ_HD))]
    ml = []
    for idx, a in enumerate(expand(angles(ML_ROPE))):
        fill = 1.0 if idx == 0 else 0.0
        ml.append(jnp.concatenate([a, jnp.full((DEC_SEQ, LANE - ML_ROPE), fill, F32)], axis=-1))
    return {"gq": jnp.stack(gq), "ml": jnp.stack(ml)}


def _layer_weights(l, gq_q_gain, gq_k_gain, ml_q_a_gain, ml_kv_a_gain, ml_w_q_b, ml_w_kv_b,
                   w_branch, w_out, hg_gain, avg):
    pad = LANE - ML_NOPE - ML_ROPE
    qb = ml_w_q_b[l].reshape(ML_Q_RANK, ML_HEADS, ML_NOPE + ML_ROPE)
    qb = jnp.concatenate([qb[:, :, ML_NOPE:], qb[:, :, :ML_NOPE], jnp.zeros((ML_Q_RANK, ML_HEADS, pad), F32)],
                         axis=-1).reshape(ML_Q_RANK, ML_HEADS * LANE)
    kvb = ml_w_kv_b[l].reshape(ML_KV_RANK, ML_HEADS, ML_NOPE + ML_V)
    wk = jnp.pad(kvb[:, :, :ML_NOPE], ((0, 0), (0, 0), (ML_ROPE, pad))).reshape(ML_KV_RANK, ML_HEADS * LANE)
    wv = kvb[:, :, ML_NOPE:].reshape(ML_KV_RANK, ML_HEADS * ML_V)
    return {
        "gq_q_gain": jnp.tile(gq_q_gain[l], GQ_HEADS).reshape(1, -1),
        "gq_k_gain": jnp.tile(gq_k_gain[l], GQ_KV_HEADS).reshape(1, -1),
        "ml_q_gain": ml_q_a_gain[l].reshape(1, -1), "ml_kv_gain": ml_kv_a_gain[l].reshape(1, -1),
        "w_qb": qb.astype(BF16), "w_k": wk.astype(BF16), "w_v": wv.astype(BF16),
        "w_branch": w_branch[l].astype(BF16), "w_out": w_out[l].astype(BF16),
        "hg_gain": jnp.tile(hg_gain[l], HG_HEADS).reshape(1, -1), "avg": avg,
    }


def _seg(arr, width, col, rows, stride, off):
    return (arr, width, col, rows, stride, off)


def kernel(x_prompt, x_sample, state_hgrn, cache_gqa_k, cache_gqa_v, cache_na_k, cache_na_v, cache_mla_ckv, cache_mla_krope, c, c_ctx, w_ada, b_ada, w_in, hg_lb_logits, hg_gain, gq_q_gain, gq_k_gain, na_rpb, ml_q_a_gain, ml_kv_a_gain, ml_w_q_b, ml_w_kv_b, w_branch, w_out, w_ffn_in, w_ffn_out, final_gain):
    cond8 = jnp.concatenate([c_ctx[None, :], c, jnp.zeros((8 - 1 - DEC_BATCH, D_MODEL), F32)], axis=0)
    mods = _ada(cond8, w_ada, b_ada)

    lb = jnp.cumsum(jax.nn.softmax(hg_lb_logits.astype(F32), axis=0), axis=0)
    lb = lb - lb[:1]
    avg = jnp.asarray(np.kron(np.eye(512 // GQ_HD), np.full((GQ_HD, GQ_HD), 1.0 / GQ_HD)), BF16)
    tabs = _rope_tables()
    w_in_t = jnp.swapaxes(w_in, 1, 2).reshape(DEPTH * w_in.shape[2], D_MODEL)

    gqk_c = cache_gqa_k.reshape(DEC_BATCH * DEPTH * PAST_LEN, GQ_KV_HEADS * GQ_HD)
    gqv_c = cache_gqa_v.reshape(DEC_BATCH * DEPTH * PAST_LEN, GQ_KV_HEADS * GQ_HD)
    nak_c = cache_na_k.transpose(0, 1, 3, 4, 2).reshape(DEC_BATCH * DEPTH * NA_HEADS * NA_HD, PAST_LEN)
    nav_c = cache_na_v.transpose(0, 1, 3, 4, 2).reshape(DEC_BATCH * DEPTH * NA_HEADS * NA_HD, PAST_LEN)
    mckv_c = cache_mla_ckv.reshape(DEC_BATCH * DEPTH * PAST_LEN, ML_KV_RANK)
    mkr_c = jnp.pad(cache_mla_krope.reshape(DEC_BATCH * DEPTH * PAST_LEN, ML_ROPE),
                    ((0, 0), (0, LANE - ML_ROPE)))

    xp = x_prompt.reshape(M_ROWS, D_MODEL)
    xs = x_sample.reshape(M_ROWS, D_MODEL)
    new = []
    ymix_ctx = []
    for l in range(DEPTH):
        lw = _layer_weights(l, gq_q_gain, gq_k_gain, ml_q_a_gain, ml_kv_a_gain, ml_w_q_b,
                            ml_w_kv_b, w_branch, w_out, hg_gain, avg)
        last = l == DEPTH - 1
        mod = mods[l]

        ymix = _inproj(xp, mod, w_in_t, l, False, gate=False)
        gates = _inproj(xp, mod, w_in_t, l, False, gate=True)
        qb, kb, qd, ckv, kd, vd = _prep(ymix, lw, tabs, False)
        out_a, st = _hgrn(ymix, lb[l], lw["hg_gain"], BATCH, SEQ, None, l)
        out_b = _attention((qb, 512, 0, 0),
                           [(_seg(kb, LANE, 0, SEQ, 1, 0), _seg(ymix, LANE, COL_GV, SEQ, 1, 0))],
                           _plan_gqa(), GQ_SCALE, BATCH, SEQ, "gqa_ctx")
        out_c = _attention((ymix, 512, COL_NQ, 0),
                           [(_seg(ymix, 512, COL_NK, SEQ, 1, 0), _seg(ymix, 512, COL_NV, SEQ, 1, 0))],
                           _plan_na(), NA_SCALE, BATCH, SEQ, "na_ctx")
        out_d = _attention((qd, ML_HEADS * LANE, 0, 0),
                           [(_seg(kd, ML_HEADS * LANE, 0, SEQ, 1, 0), _seg(vd, 512, 0, SEQ, 1, 0))],
                           _plan_mla(), ML_SCALE, BATCH, SEQ, "mla_ctx")
        xp = _merge((out_a, out_b, out_c, out_d), gates, lw["w_branch"], lw["w_out"], xp, mod, False)
        xp = _ffn(xp, mod, w_ffn_in, w_ffn_out, l, False, final_gain if last else None)
        new.append((
            st,
            kb.reshape(BATCH, SEQ, GQ_KV_HEADS, GQ_HD),
            ymix[:, COL_GV * LANE:(COL_GV + 1) * LANE].reshape(BATCH, SEQ, GQ_KV_HEADS, GQ_HD),
            ckv.reshape(BATCH, SEQ, ML_KV_RANK),
            ymix[:, COL_MKR * LANE:COL_MKR * LANE + ML_ROPE].reshape(BATCH, SEQ, ML_ROPE),
        ))
        ymix_ctx.append(ymix)

        ymix = _inproj(xs, mod, w_in_t, l, True, gate=False)
        gates = _inproj(xs, mod, w_in_t, l, True, gate=True)
        qb, kb, qd, ckv, kd, vd = _prep(ymix, lw, tabs, True)
        kd_c, vd_c = _mla_cache(mckv_c, mkr_c, lw, l)
        out_a, = _hgrn(ymix, lb[l], lw["hg_gain"], DEC_BATCH, DEC_SEQ, state_hgrn, l)
        out_b = _attention((qb, 512, 0, 0),
                           [(_seg(gqk_c, LANE, 0, PAST_LEN, DEPTH, l), _seg(gqv_c, LANE, 0, PAST_LEN, DEPTH, l)),
                            (_seg(kb, LANE, 0, DEC_SEQ, 1, 0), _seg(ymix, LANE, COL_GV, DEC_SEQ, 1, 0))],
                           _plan_gqa(), GQ_SCALE, DEC_BATCH, DEC_SEQ, "gqa_lat")
        out_c = _na_latent(ymix, nak_c, nav_c, _na_bias_tables(na_rpb[l]), l)
        out_d = _attention((qd, ML_HEADS * LANE, 0, 0),
                           [(_seg(kd_c, ML_HEADS * LANE, 0, PAST_LEN, 1, 0), _seg(vd_c, 512, 0, PAST_LEN, 1, 0)),
                            (_seg(kd, ML_HEADS * LANE, 0, DEC_SEQ, 1, 0), _seg(vd, 512, 0, DEC_SEQ, 1, 0))],
                           _plan_mla(), ML_SCALE, DEC_BATCH, DEC_SEQ, "mla_lat")
        xs = _merge((out_a, out_b, out_c, out_d), gates, lw["w_branch"], lw["w_out"], xs, mod, True)
        xs = _ffn(xs, mod, w_ffn_in, w_ffn_out, l, True, final_gain if last else None)

    y_prompt = xp.reshape(BATCH, SEQ, D_MODEL)
    y_sample = xs.reshape(DEC_BATCH, DEC_SEQ, D_MODEL)
    state, gqa_k, gqa_v, ckv_new, krope_new = (jnp.stack([n[i] for n in new], axis=1) for i in range(5))
    na_k, na_v = _na_cache_out(ymix_ctx)
    return (y_prompt, y_sample, state, gqa_k, gqa_v, na_k, na_v, ckv_new, krope_new)
```

```python
import functools

import numpy as np
import jax
import jax.numpy as jnp
from jax import lax
from jax.experimental import pallas as pl
from jax.experimental.pallas import tpu as pltpu

F32 = jnp.float32
BF16 = jnp.bfloat16

D_MODEL = 1024
BATCH = 16
SEQ = 256
DEPTH = 2
DEC_BATCH = 4
DEC_SEQ = 1024
PAST_LEN = 512
GRID_W = 64
EPS = 1e-6
ROPE_THETA = 10000.0
N_BRANCH = 4
BRANCH_W = 512
HG_HEADS = 4
HG_DK = 128
HG_DV = 128
GQ_HEADS = 8
GQ_KV_HEADS = 2
GQ_HD = 64
NA_HEADS = 8
NA_HD = 64
NA_WIN_R = 8
NA_WIN_C = 16
ML_HEADS = 8
ML_NOPE = 64
ML_ROPE = 32
ML_V = 64
ML_Q_RANK = 256
ML_KV_RANK = 128
FFN_HIDDEN = 2816
GQ_SCALE = GQ_HD ** -0.5
NA_SCALE = NA_HD ** -0.5
ML_SCALE = (ML_NOPE + ML_ROPE) ** -0.5

M_ROWS = BATCH * SEQ
assert M_ROWS == DEC_BATCH * DEC_SEQ

LANE = 128
HALF = 64
MIX_W = 5376
IN_MIX = 5280
GATE_W = N_BRANCH * D_MODEL
HG_CHUNK = 16
NEG = -1e30
LOG2E = 1.4426950408889634
VMEM_LIMIT = 56 * 1024 * 1024

COL_HQ, COL_HFF, COL_HFB, COL_HI, COL_HG, COL_GQ, COL_NQ, COL_NK, COL_NV = range(9)
COL_MQA = 18
COL_GK, COL_GV, COL_MKVA, COL_MKR = 38, 39, 40, 41


def _dot(a, b):
    return jnp.dot(a, b, preferred_element_type=F32)


def _dot_nt(a, b):
    return lax.dot_general(a, b, (((1,), (1,)), ((), ())), preferred_element_type=F32)


def _dot_tn(a, b):
    return lax.dot_general(a, b, (((0,), (0,)), ((), ())), preferred_element_type=F32)


def _split3(x):
    x1 = x.astype(BF16)
    r1 = x - x1.astype(F32)
    x2 = r1.astype(BF16)
    x3 = (r1 - x2.astype(F32)).astype(BF16)
    return x1, x2, x3


def _dot_exact_lhs(a_bf16, x):
    x1, x2, x3 = _split3(x)
    return (_dot(a_bf16, x3) + _dot(a_bf16, x2)) + _dot(a_bf16, x1)


def _dot_exact_rhs(x, b_bf16):
    x1, x2, x3 = _split3(x)
    return (_dot(x3, b_bf16) + _dot(x2, b_bf16)) + _dot(x1, b_bf16)


def _rms(x):
    return x * lax.rsqrt(jnp.mean(x * x, axis=-1, keepdims=True) + EPS)


def _silu(x):
    return x * jax.nn.sigmoid(x)


def _aligned(x, m):
    return x if isinstance(x, int) else pl.multiple_of(x, m)


def _params(*sem):
    return pltpu.CompilerParams(dimension_semantics=sem, vmem_limit_bytes=VMEM_LIMIT)


def _mod_row(latent, tm):
    if latent:
        return lambda i: 1 + (i * tm) // DEC_SEQ
    return lambda i: 0


def _ada_kernel(c_ref, w_ref, b_ref, o_ref):
    c = c_ref[...]
    o_ref[...] = _dot(_silu(c).astype(BF16), w_ref[...].astype(BF16)) + b_ref[...]


def _ada(cond8, w_ada, b_ada):
    tn = 1536
    out = pl.pallas_call(
        _ada_kernel,
        grid=(DEPTH, 6 * D_MODEL // tn),
        in_specs=[
            pl.BlockSpec((8, D_MODEL), lambda l, j: (0, 0)),
            pl.BlockSpec((None, D_MODEL, tn), lambda l, j: (l, 0, j)),
            pl.BlockSpec((None, 1, tn), lambda l, j: (l, 0, j)),
        ],
        out_specs=pl.BlockSpec((None, 8, tn), lambda l, j: (l, 0, j)),
        out_shape=jax.ShapeDtypeStruct((DEPTH, 8, 6 * D_MODEL), F32),
        compiler_params=_params("parallel", "parallel"),
        name="ada",
    )(cond8, w_ada, b_ada.reshape(DEPTH, 1, 6 * D_MODEL))
    return out.reshape(DEPTH, 8, 6, 1, D_MODEL)


IN_PIECE = 256
IN_TM = 1024


def _mix_source(t):
    blk = jnp.where(t < 12, t, jnp.where(t < 18, t + 1, jnp.where(t == 18, 19, jnp.where(t == 19, 12, 20))))
    return blk * IN_PIECE


def _gate_source(t):
    return IN_MIX + t * IN_PIECE


def _inproj_kernel(*refs, pieces, gate):
    x_ref, sc_ref, sh_ref = refs[:3]
    w_refs = refs[3:3 + pieces]
    o_ref, h_ref = refs[3 + pieces:]
    rows = pl.ds(pl.multiple_of(pl.program_id(1) * IN_TM, IN_TM), IN_TM)

    @pl.when(pl.program_id(0) == 0)
    def _():
        h = _rms(x_ref[...]) * (1.0 + sc_ref[...]) + sh_ref[...]
        h_ref[rows, :] = h.astype(BF16)

    h = h_ref[rows, :]
    for p, w_ref in enumerate(w_refs):
        y = _dot_nt(h, w_ref[...].astype(BF16))
        o_ref[:, p * IN_PIECE:(p + 1) * IN_PIECE] = jax.nn.sigmoid(y).astype(o_ref.dtype) if gate else y


def _inproj(x, mod, w_in_t, layer, latent, gate):
    width, tile, source = (GATE_W, 2048, _gate_source) if gate else (MIX_W, 1792, _mix_source)
    pieces = tile // IN_PIECE
    row = _mod_row(latent, IN_TM)
    first = lambda c, i: jnp.where(c == 0, i, 0)
    in_width = w_in_t.shape[0] // DEPTH
    piece = lambda p: pl.BlockSpec((pl.Element(IN_PIECE), pl.Element(D_MODEL)),
                                   lambda c, i: (pl.multiple_of(layer * in_width + source(c * pieces + p), 8), 0))
    return pl.pallas_call(
        functools.partial(_inproj_kernel, pieces=pieces, gate=gate),
        grid=(width // tile, M_ROWS // IN_TM),
        in_specs=[
            pl.BlockSpec((IN_TM, D_MODEL), lambda c, i: (first(c, i), 0)),
            pl.BlockSpec((None, None, 1, D_MODEL), lambda c, i: (row(first(c, i)), 1, 0, 0)),
            pl.BlockSpec((None, None, 1, D_MODEL), lambda c, i: (row(first(c, i)), 0, 0, 0)),
        ] + [piece(p) for p in range(pieces)],
        out_specs=pl.BlockSpec((IN_TM, tile), lambda c, i: (i, c)),
        out_shape=jax.ShapeDtypeStruct((M_ROWS, width), BF16 if gate else F32),
        scratch_shapes=[pltpu.VMEM((M_ROWS, D_MODEL), BF16)],
        compiler_params=_params("arbitrary", "arbitrary"),
        name="inproj_" + ("gate_" if gate else "mix_") + ("lat" if latent else "ctx"),
    )(x, mod, mod, *([w_in_t] * pieces))


def _rope(x, tab_ref, reps):
    w = x.shape[1]
    c = jnp.concatenate([tab_ref[0]] * reps, axis=1) if reps > 1 else tab_ref[0]
    se = jnp.concatenate([tab_ref[1]] * reps, axis=1) if reps > 1 else tab_ref[1]
    so = jnp.concatenate([tab_ref[2]] * reps, axis=1) if reps > 1 else tab_ref[2]
    return x * c + pltpu.roll(x, w - 1, 1) * se + pltpu.roll(x, 1, 1) * so


def _head_rms(x, gain, avg_bf16):
    ms = _dot_exact_rhs(x * x, avg_bf16)
    return x * lax.rsqrt(ms + EPS) * gain


def _prep_kernel(*refs, rope):
    (gq_ref, gk_ref, mqa_ref, mkva_ref, mkr_ref, gqg_ref, gkg_ref, mqg_ref, mkvg_ref,
     wqb_ref, wk_ref, wv_ref, avg_ref) = refs[:13]
    if rope:
        gtab_ref, mtab_ref = refs[13:15]
        outs = refs[15:]
    else:
        outs = refs[13:]
    qb_ref, kb_ref, qd_ref, ckv_ref, kd_ref, vd_ref = outs

    q = _head_rms(gq_ref[...], gqg_ref[...], avg_ref[...])
    k = _head_rms(gk_ref[...], gkg_ref[...], avg_ref[0:LANE, 0:LANE])
    qd = _dot((_rms(mqa_ref[...]) * mqg_ref[...]).astype(BF16), wqb_ref[...])
    ckv = _rms(mkva_ref[...]) * mkvg_ref[...]
    lane = lax.broadcasted_iota(jnp.int32, (1, LANE), 1)
    kr = jnp.where(lane < ML_ROPE, mkr_ref[...], 0.0)
    if rope:
        q = _rope(q, gtab_ref, GQ_HEADS * GQ_HD // LANE)
        k = _rope(k, gtab_ref, 1)
        qd = _rope(qd, mtab_ref, ML_HEADS)
        kr = _rope(kr, mtab_ref, 1)
    qb_ref[...] = q.astype(BF16)
    kb_ref[...] = k
    qd_ref[...] = qd.astype(BF16)
    ckv_ref[...] = ckv
    cb = ckv.astype(BF16)
    kd_ref[...] = (_dot(cb, wk_ref[...]) + jnp.concatenate([kr] * ML_HEADS, axis=1)).astype(BF16)
    vd_ref[...] = _dot(cb, wv_ref[...]).astype(BF16)


def _prep(ymix, lw, tabs, latent):
    tm = 512
    const = lambda i: (0, 0)
    in_specs = [
        pl.BlockSpec((tm, 512), lambda i: (i, COL_GQ)),
        pl.BlockSpec((tm, LANE), lambda i: (i, COL_GK)),
        pl.BlockSpec((tm, 256), lambda i: (i, COL_MQA)),
        pl.BlockSpec((tm, LANE), lambda i: (i, COL_MKVA)),
        pl.BlockSpec((tm, LANE), lambda i: (i, COL_MKR)),
        pl.BlockSpec((1, 512), const),
        pl.BlockSpec((1, LANE), const),
        pl.BlockSpec((1, 256), const),
        pl.BlockSpec((1, LANE), const),
        pl.BlockSpec((ML_Q_RANK, ML_HEADS * LANE), const),
        pl.BlockSpec((ML_KV_RANK, ML_HEADS * LANE), const),
        pl.BlockSpec((ML_KV_RANK, ML_HEADS * ML_V), const),
        pl.BlockSpec((512, 512), const),
    ]
    args = [ymix, ymix, ymix, ymix, ymix, lw["gq_q_gain"], lw["gq_k_gain"], lw["ml_q_gain"],
            lw["ml_kv_gain"], lw["w_qb"], lw["w_k"], lw["w_v"], lw["avg"]]
    if latent:
        per = DEC_SEQ // tm
        in_specs += [pl.BlockSpec((3, tm, LANE), lambda i: (0, i % per, 0))] * 2
        args += [tabs["gq"], tabs["ml"]]
    widths = (512, LANE, ML_HEADS * LANE, LANE, ML_HEADS * LANE, ML_HEADS * ML_V)
    return pl.pallas_call(
        functools.partial(_prep_kernel, rope=latent),
        grid=(M_ROWS // tm,),
        in_specs=in_specs,
        out_specs=[pl.BlockSpec((tm, w), lambda i: (i, 0)) for w in widths],
        out_shape=[jax.ShapeDtypeStruct((M_ROWS, w), dt)
                   for w, dt in zip(widths, (BF16, F32, BF16, F32, BF16, BF16))],
        compiler_params=_params("parallel"),
        name="prep_lat" if latent else "prep_ctx",
    )(*args)


def _mla_cache_kernel(ckv_ref, kr_ref, wk_ref, wv_ref, kd_ref, vd_ref):
    cb = ckv_ref[...].astype(BF16)
    kd_ref[...] = (_dot(cb, wk_ref[...]) + jnp.concatenate([kr_ref[...]] * ML_HEADS, axis=1)).astype(BF16)
    vd_ref[...] = _dot(cb, wv_ref[...]).astype(BF16)


def _mla_cache(ckv, kr_blk, lw, layer):
    rows = DEC_BATCH * PAST_LEN
    tm = PAST_LEN
    const = lambda i: (0, 0)
    return pl.pallas_call(
        _mla_cache_kernel,
        grid=(DEC_BATCH,),
        in_specs=[
            pl.BlockSpec((tm, LANE), lambda i: (i * DEPTH + layer, 0)),
            pl.BlockSpec((tm, LANE), lambda i: (i * DEPTH + layer, 0)),
            pl.BlockSpec((ML_KV_RANK, ML_HEADS * LANE), const),
            pl.BlockSpec((ML_KV_RANK, ML_HEADS * ML_V), const),
        ],
        out_specs=[pl.BlockSpec((tm, ML_HEADS * LANE), lambda i: (i, 0)),
                   pl.BlockSpec((tm, ML_HEADS * ML_V), lambda i: (i, 0))],
        out_shape=[jax.ShapeDtypeStruct((rows, ML_HEADS * LANE), BF16),
                   jax.ShapeDtypeStruct((rows, ML_HEADS * ML_V), BF16)],
        compiler_params=_params("parallel"),
        name="mla_cache",
    )(ckv, kr_blk, lw["w_k"], lw["w_v"])


def _softmax_pv(scores, values, transposed=None):
    transposed = transposed or [False] * len(values)
    m = scores[0].max(axis=-1, keepdims=True)
    for s in scores[1:]:
        m = jnp.maximum(m, s.max(axis=-1, keepdims=True))
    den = None
    out = None
    for s, v, vt in zip(scores, values, transposed):
        p = jnp.exp(s - m)
        d = p.sum(axis=-1, keepdims=True)
        o = _dot_nt(p.astype(BF16), v) if vt else _dot(p.astype(BF16), v)
        den = d if den is None else den + d
        out = o if out is None else out + o
    return out / den


def _lane_halves():
    lane = lax.broadcasted_iota(jnp.int32, (1, LANE), 1)
    return lane < HALF, lane >= HALF


def _attn_kernel(*refs, plan, nseg, scale, fold_scale):
    q_ref = refs[0]
    k_refs = [refs[1 + 2 * i] for i in range(nseg)]
    v_refs = [refs[2 + 2 * i] for i in range(nseg)]
    o_ref = refs[1 + 2 * nseg]
    tq = q_ref.shape[0]
    lo, hi = _lane_halves()
    cache = {}

    def block(kind, seg, blk, swap):
        key = (kind, seg, blk, swap)
        if key not in cache:
            ref = (k_refs if kind == "k" else v_refs)[seg]
            x = ref[:, blk * LANE:(blk + 1) * LANE]
            if swap:
                x = pltpu.roll(x, HALF, 1)
            cache[key] = x.astype(BF16)
        return cache[key]

    parts = {}
    for members, kblk, swap, vblk in plan:
        qs = []
        for qblk, qhalf, _, _ in members:
            q = q_ref[:, qblk * LANE:(qblk + 1) * LANE]
            if fold_scale:
                q = q * scale
            if qhalf is not None:
                q = jnp.where(lo if qhalf == 0 else hi, q, 0.0)
            qs.append(q)
        q = (jnp.concatenate(qs, axis=0) if len(qs) > 1 else qs[0]).astype(BF16)
        scores = [_dot_nt(q, block("k", s, kblk, swap)) for s in range(nseg)]
        if not fold_scale:
            scores = [s * scale for s in scores]
        out = _softmax_pv(scores, [block("v", s, vblk, swap) for s in range(nseg)])
        for n, (_, _, oblk, ohalf) in enumerate(members):
            parts[(oblk, ohalf)] = out[n * tq:(n + 1) * tq]
    for oblk in sorted({key[0] for key in parts}):
        o_ref[:, oblk * LANE:(oblk + 1) * LANE] = jnp.where(lo, parts[(oblk, 0)], parts[(oblk, 1)]).astype(BF16)


def _plan_gqa():
    per_kv = GQ_HEADS // GQ_KV_HEADS
    plan = []
    for g in range(GQ_KV_HEADS):
        for half in range(2):
            heads = [h for h in range(g * per_kv, (g + 1) * per_kv) if h % 2 == half]
            plan.append((tuple((h // 2, half, h // 2, half) for h in heads), 0, half != g, 0))
    return tuple(plan)


def _plan_na():
    return tuple((((j, 0, j, 0), (j, 1, j, 1)), j, False, j) for j in range(NA_HEADS // 2))


def _plan_mla():
    return tuple((((h, None, h // 2, h % 2),), h, False, h // 2) for h in range(ML_HEADS))


def _is_pow2(x):
    return float(np.log2(x)).is_integer()


def _attention(q, segs, plan, scale, batch, tq_total, name):
    tq = min(tq_total, 256)
    per = tq_total // tq
    q_arr, q_w, q_col, q_off = q
    in_specs = [pl.BlockSpec((tq, q_w), lambda b, i: (q_off + b * per + i, q_col))]
    args = [q_arr]
    for k, v in segs:
        for arr, w, col, rows, stride, off in (k, v):
            in_specs.append(pl.BlockSpec(
                (rows, w), functools.partial(lambda b, i, col, stride, off: (off + b * stride, col),
                                             col=col, stride=stride, off=off)))
            args.append(arr)
    out_w = 512
    return pl.pallas_call(
        functools.partial(_attn_kernel, plan=plan, nseg=len(segs), scale=scale, fold_scale=_is_pow2(scale)),
        grid=(batch, per),
        in_specs=in_specs,
        out_specs=pl.BlockSpec((tq, out_w), lambda b, i: (b * per + i, 0)),
        out_shape=jax.ShapeDtypeStruct((M_ROWS, out_w), BF16),
        compiler_params=_params("parallel", "parallel"),
        name=name,
    )(*args)


NA_DR = 2 * NA_WIN_R - 1
NA_DC = 2 * NA_WIN_C - 1
NA_ROWS = DEC_SEQ // GRID_W
NA_QROWS = 4
NA_WROWS = 12
NA_T_RIGHT = NA_DR - 1
NA_T_LEFT = NA_DR
NA_T_NONE = NA_DR + 1
NA_T_SIZE = NA_DR + 2
assert _is_pow2(NA_SCALE)


def _na_kernel(q_ref, kl_ref, vl_ref, kc_ref, vc_ref, tz_ref, o_ref, kc_scr, vc_scr):
    g = pl.program_id(1)

    @pl.when(g == 0)
    def _():
        kc_scr[...] = kc_ref[...].astype(BF16)
        vc_scr[...] = vc_ref[...].astype(BF16)

    w0 = (g // 2) * (NA_ROWS - NA_WROWS)
    k0 = pl.multiple_of(w0 * GRID_W, (NA_ROWS - NA_WROWS) * GRID_W)
    win = NA_WROWS * GRID_W
    nq = NA_QROWS * GRID_W
    entry = []
    for a in range(NA_QROWS):
        rq = g * NA_QROWS + a
        start = jnp.clip(rq - NA_WIN_R // 2, 0, NA_ROWS - NA_WIN_R)
        per_pair = []
        for i in range(NA_WROWS // 2):
            kr = w0 + 2 * i
            in_l = jnp.logical_and(kr >= start, kr < start + NA_WIN_R)
            in_r = jnp.logical_and(kr + 1 >= start, kr + 1 < start + NA_WIN_R)
            both = jnp.logical_and(in_l, in_r)
            d_l = kr - rq + NA_WIN_R - 1
            per_pair.append(jnp.where(both, d_l, jnp.where(in_r, NA_T_RIGHT, jnp.where(in_l, NA_T_LEFT, NA_T_NONE))))
        entry.append(per_pair)
    lo, hi = _lane_halves()
    for j in range(NA_HEADS // 2):
        sl = slice(j * LANE, (j + 1) * LANE)
        k_loc = kl_ref[pl.ds(k0, win), sl].astype(BF16)
        v_loc = vl_ref[pl.ds(k0, win), sl].astype(BF16)
        k_ctx_t = kc_scr[sl, :]
        v_ctx_t = vc_scr[sl, :]
        q_pair = q_ref[:, sl] * NA_SCALE
        q = jnp.concatenate([jnp.where(lo, q_pair, 0.0), jnp.where(hi, q_pair, 0.0)], axis=0).astype(BF16)
        bias = jnp.concatenate(
            [jnp.concatenate([tz_ref[2 * j + p, e] for e in entry[a]], axis=1)
             for p in range(2) for a in range(NA_QROWS)], axis=0)
        s_loc = _dot_nt(q, k_loc) + bias
        s_ctx = _dot(q, k_ctx_t)
        out = _softmax_pv([s_loc, s_ctx], [v_loc, v_ctx_t], [False, True])
        o_ref[:, sl] = jnp.where(lo, out[:nq], out[nq:]).astype(BF16)


def _na_bias_kernel(rpb_ref, sel_ref, neg_ref, o_ref):
    o_ref[...] = _dot_exact_rhs(rpb_ref[...], sel_ref[...]) + neg_ref[...]


def _na_bias_tables(rpb):
    col = np.arange(GRID_W)
    col_start = np.clip(col - NA_WIN_C // 2, 0, GRID_W - NA_WIN_C)
    col_ok = (col[None, :] >= col_start[:, None]) & (col[None, :] < col_start[:, None] + NA_WIN_C)
    dc = col[None, :] - col[:, None] + NA_WIN_C - 1
    kpad = 32
    sel = (dc[None, :, :] == np.arange(kpad)[:, None, None]) & col_ok[None]
    sel = jnp.asarray(sel.reshape(kpad, GRID_W * GRID_W), BF16)
    neg = jnp.asarray(np.where(col_ok, 0.0, NEG).reshape(1, GRID_W * GRID_W), F32)
    rpb2 = jnp.pad(rpb.reshape(NA_HEADS * NA_DR, NA_DC), ((0, 0), (0, kpad - NA_DC)))
    n_rows = NA_HEADS * NA_DR
    full = lambda shape: pl.BlockSpec(shape, lambda i: (0, 0))
    t = pl.pallas_call(
        _na_bias_kernel,
        grid=(1,),
        in_specs=[full((n_rows, kpad)), full((kpad, GRID_W * GRID_W)), full((1, GRID_W * GRID_W))],
        out_specs=full((n_rows, GRID_W * GRID_W)),
        out_shape=jax.ShapeDtypeStruct((n_rows, GRID_W * GRID_W), F32),
        compiler_params=_params("arbitrary"),
        name="na_bias",
    )(rpb2, sel, neg)
    t = t.reshape(NA_HEADS, NA_DR, GRID_W, GRID_W)
    masked = jnp.full((NA_HEADS, 1, GRID_W, GRID_W), NEG, F32)
    first, last = NA_WIN_R // 2 - 1, NA_WIN_R // 2 + NA_WIN_R - 2
    return jnp.concatenate([
        jnp.concatenate([t[:, :-1], t[:, 1:]], axis=-1),
        jnp.concatenate([masked, t[:, first:first + 1]], axis=-1),
        jnp.concatenate([t[:, last:last + 1], masked], axis=-1),
        jnp.concatenate([masked, masked], axis=-1)], axis=1)


def _na_latent(ymix, cache_k, cache_v, tz, layer):
    groups = NA_ROWS // NA_QROWS
    nq = NA_QROWS * GRID_W
    return pl.pallas_call(
        _na_kernel,
        grid=(DEC_BATCH, groups),
        in_specs=[
            pl.BlockSpec((nq, 512), lambda b, g: (b * groups + g, COL_NQ)),
            pl.BlockSpec((DEC_SEQ, 512), lambda b, g: (b, COL_NK)),
            pl.BlockSpec((DEC_SEQ, 512), lambda b, g: (b, COL_NV)),
            pl.BlockSpec((NA_HEADS * NA_HD, PAST_LEN), lambda b, g: (b * DEPTH + layer, 0)),
            pl.BlockSpec((NA_HEADS * NA_HD, PAST_LEN), lambda b, g: (b * DEPTH + layer, 0)),
            pl.BlockSpec((NA_HEADS, NA_T_SIZE, GRID_W, LANE), lambda b, g: (0, 0, 0, 0)),
        ],
        out_specs=pl.BlockSpec((nq, 512), lambda b, g: (b * groups + g, 0)),
        out_shape=jax.ShapeDtypeStruct((M_ROWS, 512), BF16),
        scratch_shapes=[pltpu.VMEM((NA_HEADS * NA_HD, PAST_LEN), BF16)] * 2,
        compiler_params=_params("parallel", "arbitrary"),
        name="na_lat",
    )(ymix, ymix, ymix, cache_k, cache_v, tz)


def _na_cache_out_kernel(*refs):
    srcs, (ko_ref, vo_ref) = refs[:2 * DEPTH], refs[2 * DEPTH:]
    layer = pl.program_id(1)
    for l in range(DEPTH):
        @pl.when(layer == l)
        def _():
            ko_ref[...] = srcs[2 * l][...].T
            vo_ref[...] = srcs[2 * l + 1][...].T


def _na_cache_out(ymix_layers):
    rows = NA_HEADS * NA_HD
    in_specs, args = [], []
    for ymix in ymix_layers:
        for col in (COL_NK, COL_NV):
            in_specs.append(pl.BlockSpec((SEQ, 512), functools.partial(lambda b, l, col: (b, col), col=col)))
            args.append(ymix)
    out = pl.pallas_call(
        _na_cache_out_kernel,
        grid=(BATCH, DEPTH),
        in_specs=in_specs,
        out_specs=[pl.BlockSpec((rows, SEQ), lambda b, l: (b * DEPTH + l, 0))] * 2,
        out_shape=[jax.ShapeDtypeStruct((BATCH * DEPTH * rows, SEQ), F32)] * 2,
        compiler_params=_params("parallel", "arbitrary"),
        name="na_cache_out",
    )(*args)
    return tuple(o.reshape(BATCH, DEPTH, NA_HEADS, NA_HD, SEQ).transpose(0, 1, 4, 2, 3) for o in out)


HG_BLOCK = 64
HG_STATE_UNROLL = 8
HG_DEC_ROWS = 8


def _hgrn_kernel(*refs, seq, has_state):
    if has_state:
        (q_ref, ff_ref, fb_ref, v_ref, g_ref, lb_ref, gain_ref, s0_ref,
         o_ref, of_ref, ob_ref, qin_ref, kend_ref, dec_ref, st_ref) = refs
    else:
        (q_ref, ff_ref, fb_ref, v_ref, g_ref, lb_ref, gain_ref,
         o_ref, sout_ref, of_ref, ob_ref, qin_ref, kend_ref, dec_ref, st_ref) = refs
    c = HG_CHUNK
    hc = c // 2
    rb = HG_BLOCK
    per_block = rb // c
    n_chunks = seq // c
    width = HG_HEADS * HG_DK
    sub = HG_DEC_ROWS

    ri = lax.broadcasted_iota(jnp.int32, (rb, rb), 0)
    ci = lax.broadcasted_iota(jnp.int32, (rb, rb), 1)
    same = (ri // c) == (ci // c)
    tri_f = jnp.where(jnp.logical_and(same, ci <= ri), 1.0, 0.0).astype(BF16)
    tri_b = jnp.where(jnp.logical_and(same, ci >= ri), 1.0, 0.0).astype(BF16)
    ones = jnp.ones((HG_DK, HG_DV), BF16)
    rowid = lax.broadcasted_iota(jnp.int32, (c, HG_DV), 0)
    laneid = lax.broadcasted_iota(jnp.int32, (c, HG_DV), 1)
    laneid_half = lax.broadcasted_iota(jnp.int32, (hc, HG_DV), 1)
    o_refs = (of_ref, ob_ref)

    n_blocks = seq // rb

    def block_of(d, step):
        return step if d == 0 else n_blocks - 1 - step

    heads = [slice(h * HG_DK, (h + 1) * HG_DK) for h in range(HG_HEADS)]
    per_chunk = hc * c + hc * hc
    dirs = (0, 1)

    def pair_block(step):
        fwd = [True, False]
        pre_refs = (ff_ref, fb_ref)
        tris = (tri_f, tri_b)
        r0 = [_aligned(block_of(d, step) * rb, rb) for d in dirs]
        wide = [range(0, hc), range(hc, c)]
        narrow = [range(hc, c), range(0, hc)]
        far = [slice(hc, c), slice(0, hc)]
        q_all = [q_ref[pl.ds(r0[d], rb), :] * (HG_DK ** -0.5) for d in dirs]
        v16_all = [v_ref[pl.ds(r0[d], rb), :].astype(BF16) for d in dirs]
        k_all, log_f = [], []
        for d in dirs:
            lb = lb_ref[d:d + 1, :]
            f = lb + (1.0 - lb) * jax.nn.sigmoid(pre_refs[d][pl.ds(r0[d], rb), :])
            k_all.append(1.0 - f)
            log_f.append(jnp.log(f))
        b_all = [_dot_exact_lhs(tris[d], log_f[d]) * LOG2E for d in dirs]
        src_all = [b_all[d] - jnp.log(k_all[d]) * LOG2E for d in dirs]
        pairs = []
        for d in dirs:
            blocks, q_in, k_end = [], [], []
            for m in range(per_block):
                rows = slice(m * c, (m + 1) * c)
                q, k, b, src = q_all[d][rows], k_all[d][rows], b_all[d][rows], src_all[d][rows]
                b_last = b[c - 1:c] if fwd[d] else b[0:1]
                q_in.append(q * jnp.exp2(b))
                k_end.append(k * jnp.exp2(b_last - b))
                dec_row = _aligned((block_of(d, step) * per_block + m) * sub, sub)
                dec_ref[d, pl.ds(dec_row, sub), :] = jnp.broadcast_to(jnp.exp2(b_last), (sub, width))
                for s in wide[d]:
                    blocks.append((q * jnp.exp2(b - src[s:s + 1])).astype(BF16))
                q_far, b_far = q[far[d]], b[far[d]]
                halves = [q_far * jnp.exp2(b_far - src[s:s + 1]) for s in narrow[d]]
                for i in range(0, hc, 2):
                    blocks.append(jnp.concatenate(halves[i:i + 2], axis=0).astype(BF16))
            qin_ref[d, pl.ds(r0[d], rb), :] = jnp.concatenate(q_in, axis=0).astype(BF16)
            kend_ref[d, pl.ds(r0[d], rb), :] = jnp.concatenate(k_end, axis=0).astype(BF16)
            pairs.append(jnp.concatenate(blocks, axis=0))
        sums = [[_dot(pairs[d][:, sl], ones) for sl in heads] for d in dirs]
        cells = [(d, h, m) for d in dirs for h in range(HG_HEADS) for m in range(per_block)]
        attn = {cell: jnp.zeros((c, HG_DV), F32) for cell in cells}
        far_attn = {cell: jnp.zeros((hc, HG_DV), F32) for cell in cells}
        for n in range(hc):
            for d, h, m in cells:
                base = m * per_chunk
                first = base + hc * c + n * hc
                attn[d, h, m] = jnp.where(laneid == wide[d][n], sums[d][h][base + n * c:base + (n + 1) * c],
                                          attn[d, h, m])
                far_attn[d, h, m] = jnp.where(laneid_half == narrow[d][n], sums[d][h][first:first + hc],
                                              far_attn[d, h, m])
        zero = jnp.zeros((hc, HG_DV), F32)
        v_pad = jnp.zeros((HG_DK - c, HG_DV), BF16)
        causal = [laneid <= rowid, laneid >= rowid]
        prods = {}
        for d, h, m in cells:
            halves = [zero, far_attn[d, h, m]] if fwd[d] else [far_attn[d, h, m], zero]
            full = jnp.where(causal[d], attn[d, h, m] + jnp.concatenate(halves, axis=0), 0.0).astype(BF16)
            values = jnp.concatenate([v16_all[d][m * c:(m + 1) * c, heads[h]], v_pad], axis=0)
            prods[d, h, m] = _dot(full, values)
        for d in dirs:
            o_refs[d][pl.ds(r0[d], rb), :] = jnp.concatenate(
                [jnp.concatenate([prods[d, h, m] for m in range(per_block)], axis=0) for h in range(HG_HEADS)], axis=1)


    def state_step(n, carry):
        for d in range(2):
            chunk = n if d == 0 else n_chunks - 1 - n
            r0 = _aligned(chunk * c, c)
            q_in = qin_ref[d, pl.ds(r0, c), :]
            k_end = kend_ref[d, pl.ds(r0, c), :]
            v16 = v_ref[pl.ds(r0, c), :].astype(BF16)
            dec = dec_ref[d, pl.ds(_aligned(chunk * sub, sub), sub), :][0:1]
            outs = []
            for h in range(HG_HEADS):
                sl = slice(h * HG_DK, (h + 1) * HG_DK)
                st = st_ref[d, h]
                outs.append(_dot_nt(q_in[:, sl], st.astype(BF16)))
                st_ref[d, h] = st * dec[:, sl] + _dot_tn(v16[:, sl], k_end[:, sl])
            o_refs[d][pl.ds(r0, c), :] += jnp.concatenate(outs, axis=1)
        return carry

    if has_state:
        for d in range(2):
            for h in range(HG_HEADS):
                st_ref[d, h] = s0_ref[d, h].T
    else:
        st_ref[...] = jnp.zeros(st_ref.shape, F32)

    def pair_step(i, carry):
        pair_block(i)
        return carry

    lax.fori_loop(0, n_blocks, pair_step, 0)
    lax.fori_loop(0, n_chunks, state_step, 0, unroll=HG_STATE_UNROLL)

    for h in range(HG_HEADS):
        sl = slice(h * HG_DV, (h + 1) * HG_DV)
        o = of_ref[:, sl] + ob_ref[:, sl]
        o_ref[:, sl] = (_rms(o) * gain_ref[:, sl] * _silu(g_ref[:, sl])).astype(BF16)
    if not has_state:
        for d in range(2):
            for h in range(HG_HEADS):
                sout_ref[d, h] = st_ref[d, h].T


def _hgrn(ymix, lb, gain, batch, seq, state, layer):
    width = HG_HEADS * HG_DK
    has_state = state is not None
    assert HG_DK == HG_DV
    st_shape = (2, HG_HEADS, HG_DK, HG_DV)
    in_specs = [pl.BlockSpec((seq, width), functools.partial(lambda b, col: (b, col), col=col))
                for col in (COL_HQ, COL_HFF, COL_HFB, COL_HI, COL_HG)]
    in_specs += [pl.BlockSpec((2, width), lambda b: (0, 0)), pl.BlockSpec((1, width), lambda b: (0, 0))]
    args = [ymix] * 5 + [lb, gain]
    out_specs = [pl.BlockSpec((seq, width), lambda b: (b, 0))]
    out_shape = [jax.ShapeDtypeStruct((M_ROWS, width), BF16)]
    if has_state:
        in_specs.append(pl.BlockSpec((None, None) + st_shape, lambda b: (b, layer, 0, 0, 0, 0)))
        args.append(state)
    else:
        out_specs.append(pl.BlockSpec((None,) + st_shape, lambda b: (b, 0, 0, 0, 0)))
        out_shape.append(jax.ShapeDtypeStruct((batch,) + st_shape, F32))
    return pl.pallas_call(
        functools.partial(_hgrn_kernel, seq=seq, has_state=has_state),
        grid=(batch,),
        in_specs=in_specs,
        out_specs=out_specs,
        out_shape=out_shape,
        scratch_shapes=[pltpu.VMEM((seq, width), F32), pltpu.VMEM((seq, width), F32),
                        pltpu.VMEM((2, seq, width), BF16), pltpu.VMEM((2, seq, width), BF16),
                        pltpu.VMEM((2, seq // HG_CHUNK * HG_DEC_ROWS, width), F32),
                        pltpu.VMEM(st_shape, F32)],
        compiler_params=_params("parallel"),
        name="hgrn_lat" if has_state else "hgrn_ctx",
    )(*args)


def _merge_kernel(oa_ref, ob_ref, oc_ref, od_ref, gt_ref, wb_ref, wo_ref, x_ref, g1_ref, out_ref):
    acc = None
    for n, o_ref in enumerate((oa_ref, ob_ref, oc_ref, od_ref)):
        bo = _dot(o_ref[...], wb_ref[n])
        term = gt_ref[:, n * D_MODEL:(n + 1) * D_MODEL] * bo
        acc = term if acc is None else acc + term
    out_ref[...] = x_ref[...] + g1_ref[...] * _dot(acc.astype(BF16), wo_ref[...])


def _merge(branches, gates, w_branch, w_out, x, mod, latent):
    tm = 512
    row = _mod_row(latent, tm)
    tile = lambda w: pl.BlockSpec((tm, w), lambda i: (i, 0))
    return pl.pallas_call(
        _merge_kernel,
        grid=(M_ROWS // tm,),
        in_specs=[tile(BRANCH_W)] * N_BRANCH + [
            tile(GATE_W),
            pl.BlockSpec((N_BRANCH, BRANCH_W, D_MODEL), lambda i: (0, 0, 0)),
            pl.BlockSpec((D_MODEL, D_MODEL), lambda i: (0, 0)),
            tile(D_MODEL),
            pl.BlockSpec((None, None, 1, D_MODEL), lambda i: (row(i), 2, 0, 0)),
        ],
        out_specs=tile(D_MODEL),
        out_shape=jax.ShapeDtypeStruct((M_ROWS, D_MODEL), F32),
        compiler_params=_params("parallel"),
        name="merge_lat" if latent else "merge_ctx",
    )(*branches, gates, w_branch, w_out, x, mod)


FFN_CHUNK = 256
FFN_STEPS = FFN_HIDDEN // FFN_CHUNK
assert FFN_STEPS * FFN_CHUNK == FFN_HIDDEN


def _ffn_kernel(*refs, final):
    x_ref, sc_ref, sh_ref, g2_ref, wa_ref, wg_ref, wo_ref = refs[:7]
    if final:
        fg_ref, out_ref, h_ref, acc_ref = refs[7:]
    else:
        out_ref, h_ref, acc_ref = refs[7:]
    j = pl.program_id(1)

    @pl.when(j == 0)
    def _():
        h = _rms(x_ref[...]) * (1.0 + sc_ref[...]) + sh_ref[...]
        h_ref[...] = h.astype(BF16)
        acc_ref[...] = jnp.zeros(acc_ref.shape, F32)

    h = h_ref[...]
    act = _silu(_dot(h, wg_ref[...].astype(BF16))) * _dot(h, wa_ref[...].astype(BF16))
    acc_ref[...] += _dot(act.astype(BF16), wo_ref[...].astype(BF16))

    @pl.when(j == FFN_STEPS - 1)
    def _():
        y = x_ref[...] + g2_ref[...] * acc_ref[...]
        out_ref[...] = _rms(y) * fg_ref[...] if final else y


def _ffn(x, mod, w_in, w_out, layer, latent, final_gain):
    tm = 1024
    row = _mod_row(latent, tm)
    modspec = lambda which: pl.BlockSpec((None, None, 1, D_MODEL), lambda i, j: (row(i), which, 0, 0))
    final = final_gain is not None
    in_specs = [
        pl.BlockSpec((tm, D_MODEL), lambda i, j: (i, 0)),
        modspec(4), modspec(3), modspec(5),
        pl.BlockSpec((None, D_MODEL, FFN_CHUNK), lambda i, j: (layer, 0, j)),
        pl.BlockSpec((None, D_MODEL, FFN_CHUNK), lambda i, j: (layer, 0, FFN_STEPS + j)),
        pl.BlockSpec((None, FFN_CHUNK, D_MODEL), lambda i, j: (layer, j, 0)),
    ]
    args = [x, mod, mod, mod, w_in, w_in, w_out]
    if final:
        in_specs.append(pl.BlockSpec((1, D_MODEL), lambda i, j: (0, 0)))
        args.append(final_gain.reshape(1, D_MODEL))
    return pl.pallas_call(
        functools.partial(_ffn_kernel, final=final),
        grid=(M_ROWS // tm, FFN_STEPS),
        in_specs=in_specs,
        out_specs=pl.BlockSpec((tm, D_MODEL), lambda i, j: (i, 0)),
        out_shape=jax.ShapeDtypeStruct((M_ROWS, D_MODEL), F32),
        scratch_shapes=[pltpu.VMEM((tm, D_MODEL), BF16), pltpu.VMEM((tm, D_MODEL), F32)],
        compiler_params=_params("parallel", "arbitrary"),
        name="ffn_lat" if latent else "ffn_ctx",
    )(*args)


def _rope_tables():
    t = jnp.arange(DEC_SEQ)
    row = (t // GRID_W).astype(F32)[:, None]
    col = (t % GRID_W).astype(F32)[:, None]

    def angles(rot_dim):
        n_freq = rot_dim // 4
        inv_freq = ROPE_THETA ** (-jnp.arange(n_freq, dtype=F32) / n_freq)
        return jnp.concatenate([row * inv_freq, col * inv_freq], axis=-1)

    def expand(ang):
        cos = jnp.repeat(jnp.cos(ang), 2, axis=-1)
        sin = jnp.repeat(jnp.sin(ang), 2, axis=-1)
        even = (jnp.arange(cos.shape[-1]) % 2 == 0)[None, :]
        return cos, jnp.where(even, -sin, 0.0), jnp.where(even, 0.0, sin)

    gq = [jnp.concatenate([a, a], axis=-1) for a in expand(angles(GQ_HD))]
    ml = []
    for idx, a in enumerate(expand(angles(ML_ROPE))):
        fill = 1.0 if idx == 0 else 0.0
        ml.append(jnp.concatenate([a, jnp.full((DEC_SEQ, LANE - ML_ROPE), fill, F32)], axis=-1))
    return {"gq": jnp.stack(gq), "ml": jnp.stack(ml)}


def _layer_weights(l, gq_q_gain, gq_k_gain, ml_q_a_gain, ml_kv_a_gain, ml_w_q_b, ml_w_kv_b,
                   w_branch, w_out, hg_gain, avg):
    pad = LANE - ML_NOPE - ML_ROPE
    qb = ml_w_q_b[l].reshape(ML_Q_RANK, ML_HEADS, ML_NOPE + ML_ROPE)
    qb = jnp.concatenate([qb[:, :, ML_NOPE:], qb[:, :, :ML_NOPE], jnp.zeros((ML_Q_RANK, ML_HEADS, pad), F32)],
                         axis=-1).reshape(ML_Q_RANK, ML_HEADS * LANE)
    kvb = ml_w_kv_b[l].reshape(ML_KV_RANK, ML_HEADS, ML_NOPE + ML_V)
    wk = jnp.pad(kvb[:, :, :ML_NOPE], ((0, 0), (0, 0), (ML_ROPE, pad))).reshape(ML_KV_RANK, ML_HEADS * LANE)
    wv = kvb[:, :, ML_NOPE:].reshape(ML_KV_RANK, ML_HEADS * ML_V)
    return {
        "gq_q_gain": jnp.tile(gq_q_gain[l], GQ_HEADS).reshape(1, -1),
        "gq_k_gain": jnp.tile(gq_k_gain[l], GQ_KV_HEADS).reshape(1, -1),
        "ml_q_gain": ml_q_a_gain[l].reshape(1, -1), "ml_kv_gain": ml_kv_a_gain[l].reshape(1, -1),
        "w_qb": qb.astype(BF16), "w_k": wk.astype(BF16), "w_v": wv.astype(BF16),
        "w_branch": w_branch[l].astype(BF16), "w_out": w_out[l].astype(BF16),
        "hg_gain": jnp.tile(hg_gain[l], HG_HEADS).reshape(1, -1), "avg": avg,
    }


def _seg(arr, width, col, rows, stride, off):
    return (arr, width, col, rows, stride, off)


def kernel(x_prompt, x_sample, state_hgrn, cache_gqa_k, cache_gqa_v, cache_na_k, cache_na_v, cache_mla_ckv, cache_mla_krope, c, c_ctx, w_ada, b_ada, w_in, hg_lb_logits, hg_gain, gq_q_gain, gq_k_gain, na_rpb, ml_q_a_gain, ml_kv_a_gain, ml_w_q_b, ml_w_kv_b, w_branch, w_out, w_ffn_in, w_ffn_out, final_gain):
    cond8 = jnp.concatenate([c_ctx[None, :], c, jnp.zeros((8 - 1 - DEC_BATCH, D_MODEL), F32)], axis=0)
    mods = _ada(cond8, w_ada, b_ada)

    lb = jnp.cumsum(jax.nn.softmax(hg_lb_logits.astype(F32), axis=0), axis=0)
    lb = lb - lb[:1]
    avg = jnp.asarray(np.kron(np.eye(512 // GQ_HD), np.full((GQ_HD, GQ_HD), 1.0 / GQ_HD)), BF16)
    tabs = _rope_tables()
    w_in_t = jnp.swapaxes(w_in, 1, 2).reshape(DEPTH * w_in.shape[2], D_MODEL)

    gqk_c = cache_gqa_k.reshape(DEC_BATCH * DEPTH * PAST_LEN, GQ_KV_HEADS * GQ_HD)
    gqv_c = cache_gqa_v.reshape(DEC_BATCH * DEPTH * PAST_LEN, GQ_KV_HEADS * GQ_HD)
    nak_c = cache_na_k.transpose(0, 1, 3, 4, 2).reshape(DEC_BATCH * DEPTH * NA_HEADS * NA_HD, PAST_LEN)
    nav_c = cache_na_v.transpose(0, 1, 3, 4, 2).reshape(DEC_BATCH * DEPTH * NA_HEADS * NA_HD, PAST_LEN)
    mckv_c = cache_mla_ckv.reshape(DEC_BATCH * DEPTH * PAST_LEN, ML_KV_RANK)
    mkr_c = jnp.pad(cache_mla_krope.reshape(DEC_BATCH * DEPTH * PAST_LEN, ML_ROPE),
                    ((0, 0), (0, LANE - ML_ROPE)))

    xp = x_prompt.reshape(M_ROWS, D_MODEL)
    xs = x_sample.reshape(M_ROWS, D_MODEL)
    new = []
    ymix_ctx = []
    for l in range(DEPTH):
        lw = _layer_weights(l, gq_q_gain, gq_k_gain, ml_q_a_gain, ml_kv_a_gain, ml_w_q_b,
                            ml_w_kv_b, w_branch, w_out, hg_gain, avg)
        last = l == DEPTH - 1
        mod = mods[l]

        ymix = _inproj(xp, mod, w_in_t, l, False, gate=False)
        gates = _inproj(xp, mod, w_in_t, l, False, gate=True)
        qb, kb, qd, ckv, kd, vd = _prep(ymix, lw, tabs, False)
        out_a, st = _hgrn(ymix, lb[l], lw["hg_gain"], BATCH, SEQ, None, l)
        out_b = _attention((qb, 512, 0, 0),
                           [(_seg(kb, LANE, 0, SEQ, 1, 0), _seg(ymix, LANE, COL_GV, SEQ, 1, 0))],
                           _plan_gqa(), GQ_SCALE, BATCH, SEQ, "gqa_ctx")
        out_c = _attention((ymix, 512, COL_NQ, 0),
                           [(_seg(ymix, 512, COL_NK, SEQ, 1, 0), _seg(ymix, 512, COL_NV, SEQ, 1, 0))],
                           _plan_na(), NA_SCALE, BATCH, SEQ, "na_ctx")
        out_d = _attention((qd, ML_HEADS * LANE, 0, 0),
                           [(_seg(kd, ML_HEADS * LANE, 0, SEQ, 1, 0), _seg(vd, 512, 0, SEQ, 1, 0))],
                           _plan_mla(), ML_SCALE, BATCH, SEQ, "mla_ctx")
        xp = _merge((out_a, out_b, out_c, out_d), gates, lw["w_branch"], lw["w_out"], xp, mod, False)
        xp = _ffn(xp, mod, w_ffn_in, w_ffn_out, l, False, final_gain if last else None)
        new.append((
            st,
            kb.reshape(BATCH, SEQ, GQ_KV_HEADS, GQ_HD),
            ymix[:, COL_GV * LANE:(COL_GV + 1) * LANE].reshape(BATCH, SEQ, GQ_KV_HEADS, GQ_HD),
            ckv.reshape(BATCH, SEQ, ML_KV_RANK),
            ymix[:, COL_MKR * LANE:COL_MKR * LANE + ML_ROPE].reshape(BATCH, SEQ, ML_ROPE),
        ))
        ymix_ctx.append(ymix)

        ymix = _inproj(xs, mod, w_in_t, l, True, gate=False)
        gates = _inproj(xs, mod, w_in_t, l, True, gate=True)
        qb, kb, qd, ckv, kd, vd = _prep(ymix, lw, tabs, True)
        kd_c, vd_c = _mla_cache(mckv_c, mkr_c, lw, l)
        out_a, = _hgrn(ymix, lb[l], lw["hg_gain"], DEC_BATCH, DEC_SEQ, state_hgrn, l)
        out_b = _attention((qb, 512, 0, 0),
                           [(_seg(gqk_c, LANE, 0, PAST_LEN, DEPTH, l), _seg(gqv_c, LANE, 0, PAST_LEN, DEPTH, l)),
                            (_seg(kb, LANE, 0, DEC_SEQ, 1, 0), _seg(ymix, LANE, COL_GV, DEC_SEQ, 1, 0))],
                           _plan_gqa(), GQ_SCALE, DEC_BATCH, DEC_SEQ, "gqa_lat")
        out_c = _na_latent(ymix, nak_c, nav_c, _na_bias_tables(na_rpb[l]), l)
        out_d = _attention((qd, ML_HEADS * LANE, 0, 0),
                           [(_seg(kd_c, ML_HEADS * LANE, 0, PAST_LEN, 1, 0), _seg(vd_c, 512, 0, PAST_LEN, 1, 0)),
                            (_seg(kd, ML_HEADS * LANE, 0, DEC_SEQ, 1, 0), _seg(vd, 512, 0, DEC_SEQ, 1, 0))],
                           _plan_mla(), ML_SCALE, DEC_BATCH, DEC_SEQ, "mla_lat")
        xs = _merge((out_a, out_b, out_c, out_d), gates, lw["w_branch"], lw["w_out"], xs, mod, True)
        xs = _ffn(xs, mod, w_ffn_in, w_ffn_out, l, True, final_gain if last else None)

    y_prompt = xp.reshape(BATCH, SEQ, D_MODEL)
    y_sample = xs.reshape(DEC_BATCH, DEC_SEQ, D_MODEL)
    state, gqa_k, gqa_v, ckv_new, krope_new = (jnp.stack([n[i] for n in new], axis=1) for i in range(5))
    na_k, na_v = _na_cache_out(ymix_ctx)
    return (y_prompt, y_sample, state, gqa_k, gqa_v, na_k, na_v, ckv_new, krope_new)
```

```python
import functools

import numpy as np
import jax
import jax.numpy as jnp
from jax import lax
from jax.experimental import pallas as pl
from jax.experimental.pallas import tpu as pltpu

F32 = jnp.float32
BF16 = jnp.bfloat16

D_MODEL = 1024
BATCH = 16
SEQ = 256
DEPTH = 2
DEC_BATCH = 4
DEC_SEQ = 1024
PAST_LEN = 512
GRID_W = 64
EPS = 1e-6
ROPE_THETA = 10000.0
N_BRANCH = 4
BRANCH_W = 512
HG_HEADS = 4
HG_DK = 128
HG_DV = 128
GQ_HEADS = 8
GQ_KV_HEADS = 2
GQ_HD = 64
NA_HEADS = 8
NA_HD = 64
NA_WIN_R = 8
NA_WIN_C = 16
ML_HEADS = 8
ML_NOPE = 64
ML_ROPE = 32
ML_V = 64
ML_Q_RANK = 256
ML_KV_RANK = 128
FFN_HIDDEN = 2816
GQ_SCALE = GQ_HD ** -0.5
NA_SCALE = NA_HD ** -0.5
ML_SCALE = (ML_NOPE + ML_ROPE) ** -0.5

M_ROWS = BATCH * SEQ
assert M_ROWS == DEC_BATCH * DEC_SEQ

LANE = 128
HALF = 64
MIX_W = 5376
IN_MIX = 5280
GATE_W = N_BRANCH * D_MODEL
HG_CHUNK = 16
NEG = -1e30
LOG2E = 1.4426950408889634
VMEM_LIMIT = 56 * 1024 * 1024

COL_HQ, COL_HFF, COL_HFB, COL_HI, COL_HG, COL_GQ, COL_NQ, COL_NK, COL_NV = range(9)
COL_MQA = 18
COL_GK, COL_GV, COL_MKVA, COL_MKR = 38, 39, 40, 41


def _dot(a, b):
    return jnp.dot(a, b, preferred_element_type=F32)


def _dot_nt(a, b):
    return lax.dot_general(a, b, (((1,), (1,)), ((), ())), preferred_element_type=F32)


def _dot_tn(a, b):
    return lax.dot_general(a, b, (((0,), (0,)), ((), ())), preferred_element_type=F32)


def _split3(x):
    x1 = x.astype(BF16)
    r1 = x - x1.astype(F32)
    x2 = r1.astype(BF16)
    x3 = (r1 - x2.astype(F32)).astype(BF16)
    return x1, x2, x3


def _dot_exact_lhs(a_bf16, x):
    x1, x2, x3 = _split3(x)
    return (_dot(a_bf16, x3) + _dot(a_bf16, x2)) + _dot(a_bf16, x1)


def _dot_exact_rhs(x, b_bf16):
    x1, x2, x3 = _split3(x)
    return (_dot(x3, b_bf16) + _dot(x2, b_bf16)) + _dot(x1, b_bf16)


def _rms(x):
    return x * lax.rsqrt(jnp.mean(x * x, axis=-1, keepdims=True) + EPS)


def _silu(x):
    return x * jax.nn.sigmoid(x)


def _aligned(x, m):
    return x if isinstance(x, int) else pl.multiple_of(x, m)


def _params(*sem):
    return pltpu.CompilerParams(dimension_semantics=sem, vmem_limit_bytes=VMEM_LIMIT)


def _mod_row(latent, tm):
    if latent:
        return lambda i: 1 + (i * tm) // DEC_SEQ
    return lambda i: 0


def _ada_kernel(c_ref, w_ref, b_ref, o_ref):
    c = c_ref[...]
    o_ref[...] = _dot(_silu(c).astype(BF16), w_ref[...].astype(BF16)) + b_ref[...]


def _ada(cond8, w_ada, b_ada):
    tn = 1536
    out = pl.pallas_call(
        _ada_kernel,
        grid=(DEPTH, 6 * D_MODEL // tn),
        in_specs=[
            pl.BlockSpec((8, D_MODEL), lambda l, j: (0, 0)),
            pl.BlockSpec((None, D_MODEL, tn), lambda l, j: (l, 0, j)),
            pl.BlockSpec((None, 1, tn), lambda l, j: (l, 0, j)),
        ],
        out_specs=pl.BlockSpec((None, 8, tn), lambda l, j: (l, 0, j)),
        out_shape=jax.ShapeDtypeStruct((DEPTH, 8, 6 * D_MODEL), F32),
        compiler_params=_params("parallel", "parallel"),
        name="ada",
    )(cond8, w_ada, b_ada.reshape(DEPTH, 1, 6 * D_MODEL))
    return out.reshape(DEPTH, 8, 6, 1, D_MODEL)


IN_PIECE = 256
IN_TM = 1024


def _mix_source(t):
    blk = jnp.where(t < 12, t, jnp.where(t < 18, t + 1, jnp.where(t == 18, 19, jnp.where(t == 19, 12, 20))))
    return blk * IN_PIECE


def _gate_source(t):
    return IN_MIX + t * IN_PIECE


def _inproj_kernel(*refs, pieces, gate):
    x_ref, sc_ref, sh_ref = refs[:3]
    w_refs = refs[3:3 + pieces]
    o_ref, h_ref = refs[3 + pieces:]
    rows = pl.ds(pl.multiple_of(pl.program_id(1) * IN_TM, IN_TM), IN_TM)

    @pl.when(pl.program_id(0) == 0)
    def _():
        h = _rms(x_ref[...]) * (1.0 + sc_ref[...]) + sh_ref[...]
        h_ref[rows, :] = h.astype(BF16)

    h = h_ref[rows, :]
    for p, w_ref in enumerate(w_refs):
        y = _dot_nt(h, w_ref[...].astype(BF16))
        o_ref[:, p * IN_PIECE:(p + 1) * IN_PIECE] = jax.nn.sigmoid(y).astype(o_ref.dtype) if gate else y


def _inproj(x, mod, w_in_t, layer, latent, gate):
    width, tile, source = (GATE_W, 2048, _gate_source) if gate else (MIX_W, 1792, _mix_source)
    pieces = tile // IN_PIECE
    row = _mod_row(latent, IN_TM)
    first = lambda c, i: jnp.where(c == 0, i, 0)
    in_width = w_in_t.shape[0] // DEPTH
    piece = lambda p: pl.BlockSpec((pl.Element(IN_PIECE), pl.Element(D_MODEL)),
                                   lambda c, i: (pl.multiple_of(layer * in_width + source(c * pieces + p), 8), 0))
    return pl.pallas_call(
        functools.partial(_inproj_kernel, pieces=pieces, gate=gate),
        grid=(width // tile, M_ROWS // IN_TM),
        in_specs=[
            pl.BlockSpec((IN_TM, D_MODEL), lambda c, i: (first(c, i), 0)),
            pl.BlockSpec((None, None, 1, D_MODEL), lambda c, i: (row(first(c, i)), 1, 0, 0)),
            pl.BlockSpec((None, None, 1, D_MODEL), lambda c, i: (row(first(c, i)), 0, 0, 0)),
        ] + [piece(p) for p in range(pieces)],
        out_specs=pl.BlockSpec((IN_TM, tile), lambda c, i: (i, c)),
        out_shape=jax.ShapeDtypeStruct((M_ROWS, width), BF16 if gate else F32),
        scratch_shapes=[pltpu.VMEM((M_ROWS, D_MODEL), BF16)],
        compiler_params=_params("arbitrary", "arbitrary"),
        name="inproj_" + ("gate_" if gate else "mix_") + ("lat" if latent else "ctx"),
    )(x, mod, mod, *([w_in_t] * pieces))


def _rope(x, tab_ref, reps):
    w = x.shape[1]
    c = jnp.concatenate([tab_ref[0]] * reps, axis=1) if reps > 1 else tab_ref[0]
    se = jnp.concatenate([tab_ref[1]] * reps, axis=1) if reps > 1 else tab_ref[1]
    so = jnp.concatenate([tab_ref[2]] * reps, axis=1) if reps > 1 else tab_ref[2]
    return x * c + pltpu.roll(x, w - 1, 1) * se + pltpu.roll(x, 1, 1) * so


def _head_rms(x, gain, avg_bf16):
    ms = _dot_exact_rhs(x * x, avg_bf16)
    return x * lax.rsqrt(ms + EPS) * gain


def _prep_kernel(*refs, rope):
    (gq_ref, gk_ref, mqa_ref, mkva_ref, mkr_ref, gqg_ref, gkg_ref, mqg_ref, mkvg_ref,
     wqb_ref, wk_ref, wv_ref, avg_ref) = refs[:13]
    if rope:
        gtab_ref, mtab_ref = refs[13:15]
        outs = refs[15:]
    else:
        outs = refs[13:]
    qb_ref, kb_ref, qd_ref, ckv_ref, kd_ref, vd_ref = outs

    q = _head_rms(gq_ref[...], gqg_ref[...], avg_ref[...])
    k = _head_rms(gk_ref[...], gkg_ref[...], avg_ref[0:LANE, 0:LANE])
    qd = _dot((_rms(mqa_ref[...]) * mqg_ref[...]).astype(BF16), wqb_ref[...])
    ckv = _rms(mkva_ref[...]) * mkvg_ref[...]
    lane = lax.broadcasted_iota(jnp.int32, (1, LANE), 1)
    kr = jnp.where(lane < ML_ROPE, mkr_ref[...], 0.0)
    if rope:
        q = _rope(q, gtab_ref, GQ_HEADS * GQ_HD // LANE)
        k = _rope(k, gtab_ref, 1)
        qd = _rope(qd, mtab_ref, ML_HEADS)
        kr = _rope(kr, mtab_ref, 1)
    qb_ref[...] = q.astype(BF16)
    kb_ref[...] = k
    qd_ref[...] = qd.astype(BF16)
    ckv_ref[...] = ckv
    cb = ckv.astype(BF16)
    kd_ref[...] = (_dot(cb, wk_ref[...]) + jnp.concatenate([kr] * ML_HEADS, axis=1)).astype(BF16)
    vd_ref[...] = _dot(cb, wv_ref[...]).astype(BF16)


def _prep(ymix, lw, tabs, latent):
    tm = 512
    const = lambda i: (0, 0)
    in_specs = [
        pl.BlockSpec((tm, 512), lambda i: (i, COL_GQ)),
        pl.BlockSpec((tm, LANE), lambda i: (i, COL_GK)),
        pl.BlockSpec((tm, 256), lambda i: (i, COL_MQA)),
        pl.BlockSpec((tm, LANE), lambda i: (i, COL_MKVA)),
        pl.BlockSpec((tm, LANE), lambda i: (i, COL_MKR)),
        pl.BlockSpec((1, 512), const),
        pl.BlockSpec((1, LANE), const),
        pl.BlockSpec((1, 256), const),
        pl.BlockSpec((1, LANE), const),
        pl.BlockSpec((ML_Q_RANK, ML_HEADS * LANE), const),
        pl.BlockSpec((ML_KV_RANK, ML_HEADS * LANE), const),
        pl.BlockSpec((ML_KV_RANK, ML_HEADS * ML_V), const),
        pl.BlockSpec((512, 512), const),
    ]
    args = [ymix, ymix, ymix, ymix, ymix, lw["gq_q_gain"], lw["gq_k_gain"], lw["ml_q_gain"],
            lw["ml_kv_gain"], lw["w_qb"], lw["w_k"], lw["w_v"], lw["avg"]]
    if latent:
        per = DEC_SEQ // tm
        in_specs += [pl.BlockSpec((3, tm, LANE), lambda i: (0, i % per, 0))] * 2
        args += [tabs["gq"], tabs["ml"]]
    widths = (512, LANE, ML_HEADS * LANE, LANE, ML_HEADS * LANE, ML_HEADS * ML_V)
    return pl.pallas_call(
        functools.partial(_prep_kernel, rope=latent),
        grid=(M_ROWS // tm,),
        in_specs=in_specs,
        out_specs=[pl.BlockSpec((tm, w), lambda i: (i, 0)) for w in widths],
        out_shape=[jax.ShapeDtypeStruct((M_ROWS, w), dt)
                   for w, dt in zip(widths, (BF16, F32, BF16, F32, BF16, BF16))],
        compiler_params=_params("parallel"),
        name="prep_lat" if latent else "prep_ctx",
    )(*args)


def _mla_cache_kernel(ckv_ref, kr_ref, wk_ref, wv_ref, kd_ref, vd_ref):
    cb = ckv_ref[...].astype(BF16)
    kd_ref[...] = (_dot(cb, wk_ref[...]) + jnp.concatenate([kr_ref[...]] * ML_HEADS, axis=1)).astype(BF16)
    vd_ref[...] = _dot(cb, wv_ref[...]).astype(BF16)


def _mla_cache(ckv, kr_blk, lw, layer):
    rows = DEC_BATCH * PAST_LEN
    tm = PAST_LEN
    const = lambda i: (0, 0)
    return pl.pallas_call(
        _mla_cache_kernel,
        grid=(DEC_BATCH,),
        in_specs=[
            pl.BlockSpec((tm, LANE), lambda i: (i * DEPTH + layer, 0)),
            pl.BlockSpec((tm, LANE), lambda i: (i * DEPTH + layer, 0)),
            pl.BlockSpec((ML_KV_RANK, ML_HEADS * LANE), const),
            pl.BlockSpec((ML_KV_RANK, ML_HEADS * ML_V), const),
        ],
        out_specs=[pl.BlockSpec((tm, ML_HEADS * LANE), lambda i: (i, 0)),
                   pl.BlockSpec((tm, ML_HEADS * ML_V), lambda i: (i, 0))],
        out_shape=[jax.ShapeDtypeStruct((rows, ML_HEADS * LANE), BF16),
                   jax.ShapeDtypeStruct((rows, ML_HEADS * ML_V), BF16)],
        compiler_params=_params("parallel"),
        name="mla_cache",
    )(ckv, kr_blk, lw["w_k"], lw["w_v"])


def _softmax_pv(scores, values, transposed=None):
    transposed = transposed or [False] * len(values)
    m = scores[0].max(axis=-1, keepdims=True)
    for s in scores[1:]:
        m = jnp.maximum(m, s.max(axis=-1, keepdims=True))
    den = None
    out = None
    for s, v, vt in zip(scores, values, transposed):
        p = jnp.exp(s - m)
        d = p.sum(axis=-1, keepdims=True)
        o = _dot_nt(p.astype(BF16), v) if vt else _dot(p.astype(BF16), v)
        den = d if den is None else den + d
        out = o if out is None else out + o
    return out / den


def _softmax_pv_phased(scores, values):
    maxima = [s.max(axis=-1, keepdims=True) for s in scores]
    probs = [jnp.exp(s - m) for s, m in zip(scores, maxima)]
    dens = [p.sum(axis=-1, keepdims=True) for p in probs]
    outs = [_dot(p.astype(BF16), v) for p, v in zip(probs, values)]
    return [o / d for o, d in zip(outs, dens)]


def _lane_halves():
    lane = lax.broadcasted_iota(jnp.int32, (1, LANE), 1)
    return lane < HALF, lane >= HALF


def _attn_kernel(*refs, plan, nseg, scale, fold_scale):
    q_ref = refs[0]
    k_refs = [refs[1 + 2 * i] for i in range(nseg)]
    v_refs = [refs[2 + 2 * i] for i in range(nseg)]
    o_ref = refs[1 + 2 * nseg]
    tq = q_ref.shape[0]
    lo, hi = _lane_halves()
    cache = {}

    def block(kind, seg, blk, swap):
        key = (kind, seg, blk, swap)
        if key not in cache:
            ref = (k_refs if kind == "k" else v_refs)[seg]
            x = ref[:, blk * LANE:(blk + 1) * LANE]
            if swap:
                x = pltpu.roll(x, HALF, 1)
            cache[key] = x.astype(BF16)
        return cache[key]

    def queries(members):
        qs = []
        for qblk, qhalf, _, _ in members:
            q = q_ref[:, qblk * LANE:(qblk + 1) * LANE]
            if fold_scale:
                q = q * scale
            if qhalf is not None:
                q = jnp.where(lo if qhalf == 0 else hi, q, 0.0)
            qs.append(q)
        return (jnp.concatenate(qs, axis=0) if len(qs) > 1 else qs[0]).astype(BF16)

    def scores(q, kblk, swap):
        out = [_dot_nt(q, block("k", s, kblk, swap)) for s in range(nseg)]
        return out if fold_scale else [s * scale for s in out]

    if nseg == 1:
        all_scores = [scores(queries(members), kblk, swap)[0] for members, kblk, swap, _ in plan]
        outs = _softmax_pv_phased(all_scores, [block("v", 0, vblk, swap) for _, _, swap, vblk in plan])
    else:
        outs = []
        ready = scores(queries(plan[0][0]), plan[0][1], plan[0][2])
        for g, (_, _, swap, vblk) in enumerate(plan):
            if g + 1 < len(plan):
                following = scores(queries(plan[g + 1][0]), plan[g + 1][1], plan[g + 1][2])
            outs.append(_softmax_pv(ready, [block("v", s, vblk, swap) for s in range(nseg)]))
            ready = following
    parts = {}
    for (members, _, _, _), out in zip(plan, outs):
        for n, (_, _, oblk, ohalf) in enumerate(members):
            parts[(oblk, ohalf)] = out[n * tq:(n + 1) * tq]
    for oblk in sorted({key[0] for key in parts}):
        o_ref[:, oblk * LANE:(oblk + 1) * LANE] = jnp.where(lo, parts[(oblk, 0)], parts[(oblk, 1)]).astype(BF16)


def _plan_gqa():
    per_kv = GQ_HEADS // GQ_KV_HEADS
    plan = []
    for g in range(GQ_KV_HEADS):
        for half in range(2):
            heads = [h for h in range(g * per_kv, (g + 1) * per_kv) if h % 2 == half]
            plan.append((tuple((h // 2, half, h // 2, half) for h in heads), 0, half != g, 0))
    return tuple(plan)


def _plan_na():
    return tuple((((j, 0, j, 0), (j, 1, j, 1)), j, False, j) for j in range(NA_HEADS // 2))


def _plan_mla():
    return tuple((((h, None, h // 2, h % 2),), h, False, h // 2) for h in range(ML_HEADS))


def _is_pow2(x):
    return float(np.log2(x)).is_integer()


def _attention(q, segs, plan, scale, batch, tq_total, name):
    tq = min(tq_total, 256)
    per = tq_total // tq
    q_arr, q_w, q_col, q_off = q
    in_specs = [pl.BlockSpec((tq, q_w), lambda b, i: (q_off + b * per + i, q_col))]
    args = [q_arr]
    for k, v in segs:
        for arr, w, col, rows, stride, off in (k, v):
            in_specs.append(pl.BlockSpec(
                (rows, w), functools.partial(lambda b, i, col, stride, off: (off + b * stride, col),
                                             col=col, stride=stride, off=off)))
            args.append(arr)
    out_w = 512
    return pl.pallas_call(
        functools.partial(_attn_kernel, plan=plan, nseg=len(segs), scale=scale, fold_scale=_is_pow2(scale)),
        grid=(batch, per),
        in_specs=in_specs,
        out_specs=pl.BlockSpec((tq, out_w), lambda b, i: (b * per + i, 0)),
        out_shape=jax.ShapeDtypeStruct((M_ROWS, out_w), BF16),
        compiler_params=_params("parallel", "parallel"),
        name=name,
    )(*args)


NA_DR = 2 * NA_WIN_R - 1
NA_DC = 2 * NA_WIN_C - 1
NA_ROWS = DEC_SEQ // GRID_W
NA_QROWS = 4
NA_WROWS = 12
NA_T_RIGHT = NA_DR - 1
NA_T_LEFT = NA_DR
NA_T_NONE = NA_DR + 1
NA_T_SIZE = NA_DR + 2
assert _is_pow2(NA_SCALE)


def _na_kernel(q_ref, kl_ref, vl_ref, kc_ref, vc_ref, tz_ref, o_ref, kc_scr, vc_scr):
    g = pl.program_id(1)

    @pl.when(g == 0)
    def _():
        kc_scr[...] = kc_ref[...].astype(BF16)
        vc_scr[...] = vc_ref[...].astype(BF16)

    w0 = (g // 2) * (NA_ROWS - NA_WROWS)
    k0 = pl.multiple_of(w0 * GRID_W, (NA_ROWS - NA_WROWS) * GRID_W)
    win = NA_WROWS * GRID_W
    nq = NA_QROWS * GRID_W
    entry = []
    for a in range(NA_QROWS):
        rq = g * NA_QROWS + a
        start = jnp.clip(rq - NA_WIN_R // 2, 0, NA_ROWS - NA_WIN_R)
        per_pair = []
        for i in range(NA_WROWS // 2):
            kr = w0 + 2 * i
            in_l = jnp.logical_and(kr >= start, kr < start + NA_WIN_R)
            in_r = jnp.logical_and(kr + 1 >= start, kr + 1 < start + NA_WIN_R)
            both = jnp.logical_and(in_l, in_r)
            d_l = kr - rq + NA_WIN_R - 1
            per_pair.append(jnp.where(both, d_l, jnp.where(in_r, NA_T_RIGHT, jnp.where(in_l, NA_T_LEFT, NA_T_NONE))))
        entry.append(per_pair)
    lo, hi = _lane_halves()
    def scores(j):
        sl = slice(j * LANE, (j + 1) * LANE)
        k_loc = kl_ref[pl.ds(k0, win), sl].astype(BF16)
        k_ctx_t = kc_scr[sl, :]
        q_pair = q_ref[:, sl] * NA_SCALE
        q = jnp.concatenate([jnp.where(lo, q_pair, 0.0), jnp.where(hi, q_pair, 0.0)], axis=0).astype(BF16)
        bias = jnp.concatenate(
            [jnp.concatenate([tz_ref[2 * j + p, e] for e in entry[a]], axis=1)
             for p in range(2) for a in range(NA_QROWS)], axis=0)
        return [_dot_nt(q, k_loc) + bias, _dot(q, k_ctx_t)]

    pairs = NA_HEADS // 2
    ready = scores(0)
    for j in range(pairs):
        sl = slice(j * LANE, (j + 1) * LANE)
        if j + 1 < pairs:
            following = scores(j + 1)
        v_loc = vl_ref[pl.ds(k0, win), sl].astype(BF16)
        out = _softmax_pv(ready, [v_loc, vc_scr[sl, :]], [False, True])
        o_ref[:, sl] = jnp.where(lo, out[:nq], out[nq:]).astype(BF16)
        ready = following


def _na_bias_kernel(rpb_ref, sel_ref, neg_ref, o_ref):
    o_ref[...] = _dot_exact_rhs(rpb_ref[...], sel_ref[...]) + neg_ref[...]


def _na_bias_tables(rpb):
    col = np.arange(GRID_W)
    col_start = np.clip(col - NA_WIN_C // 2, 0, GRID_W - NA_WIN_C)
    col_ok = (col[None, :] >= col_start[:, None]) & (col[None, :] < col_start[:, None] + NA_WIN_C)
    dc = col[None, :] - col[:, None] + NA_WIN_C - 1
    kpad = 32
    sel = (dc[None, :, :] == np.arange(kpad)[:, None, None]) & col_ok[None]
    sel = jnp.asarray(sel.reshape(kpad, GRID_W * GRID_W), BF16)
    neg = jnp.asarray(np.where(col_ok, 0.0, NEG).reshape(1, GRID_W * GRID_W), F32)
    rpb2 = jnp.pad(rpb.reshape(NA_HEADS * NA_DR, NA_DC), ((0, 0), (0, kpad - NA_DC)))
    n_rows = NA_HEADS * NA_DR
    full = lambda shape: pl.BlockSpec(shape, lambda i: (0, 0))
    t = pl.pallas_call(
        _na_bias_kernel,
        grid=(1,),
        in_specs=[full((n_rows, kpad)), full((kpad, GRID_W * GRID_W)), full((1, GRID_W * GRID_W))],
        out_specs=full((n_rows, GRID_W * GRID_W)),
        out_shape=jax.ShapeDtypeStruct((n_rows, GRID_W * GRID_W), F32),
        compiler_params=_params("arbitrary"),
        name="na_bias",
    )(rpb2, sel, neg)
    t = t.reshape(NA_HEADS, NA_DR, GRID_W, GRID_W)
    masked = jnp.full((NA_HEADS, 1, GRID_W, GRID_W), NEG, F32)
    first, last = NA_WIN_R // 2 - 1, NA_WIN_R // 2 + NA_WIN_R - 2
    return jnp.concatenate([
        jnp.concatenate([t[:, :-1], t[:, 1:]], axis=-1),
        jnp.concatenate([masked, t[:, first:first + 1]], axis=-1),
        jnp.concatenate([t[:, last:last + 1], masked], axis=-1),
        jnp.concatenate([masked, masked], axis=-1)], axis=1)


def _na_latent(ymix, cache_k, cache_v, tz, layer):
    groups = NA_ROWS // NA_QROWS
    nq = NA_QROWS * GRID_W
    return pl.pallas_call(
        _na_kernel,
        grid=(DEC_BATCH, groups),
        in_specs=[
            pl.BlockSpec((nq, 512), lambda b, g: (b * groups + g, COL_NQ)),
            pl.BlockSpec((DEC_SEQ, 512), lambda b, g: (b, COL_NK)),
            pl.BlockSpec((DEC_SEQ, 512), lambda b, g: (b, COL_NV)),
            pl.BlockSpec((NA_HEADS * NA_HD, PAST_LEN), lambda b, g: (b * DEPTH + layer, 0)),
            pl.BlockSpec((NA_HEADS * NA_HD, PAST_LEN), lambda b, g: (b * DEPTH + layer, 0)),
            pl.BlockSpec((NA_HEADS, NA_T_SIZE, GRID_W, LANE), lambda b, g: (0, 0, 0, 0)),
        ],
        out_specs=pl.BlockSpec((nq, 512), lambda b, g: (b * groups + g, 0)),
        out_shape=jax.ShapeDtypeStruct((M_ROWS, 512), BF16),
        scratch_shapes=[pltpu.VMEM((NA_HEADS * NA_HD, PAST_LEN), BF16)] * 2,
        compiler_params=_params("parallel", "arbitrary"),
        name="na_lat",
    )(ymix, ymix, ymix, cache_k, cache_v, tz)


def _na_cache_out_kernel(*refs):
    srcs, (ko_ref, vo_ref) = refs[:2 * DEPTH], refs[2 * DEPTH:]
    layer = pl.program_id(1)
    for l in range(DEPTH):
        @pl.when(layer == l)
        def _():
            ko_ref[...] = srcs[2 * l][...].T
            vo_ref[...] = srcs[2 * l + 1][...].T


def _na_cache_out(ymix_layers):
    rows = NA_HEADS * NA_HD
    in_specs, args = [], []
    for ymix in ymix_layers:
        for col in (COL_NK, COL_NV):
            in_specs.append(pl.BlockSpec((SEQ, 512), functools.partial(lambda b, l, col: (b, col), col=col)))
            args.append(ymix)
    out = pl.pallas_call(
        _na_cache_out_kernel,
        grid=(BATCH, DEPTH),
        in_specs=in_specs,
        out_specs=[pl.BlockSpec((rows, SEQ), lambda b, l: (b * DEPTH + l, 0))] * 2,
        out_shape=[jax.ShapeDtypeStruct((BATCH * DEPTH * rows, SEQ), F32)] * 2,
        compiler_params=_params("parallel", "arbitrary"),
        name="na_cache_out",
    )(*args)
    return tuple(o.reshape(BATCH, DEPTH, NA_HEADS, NA_HD, SEQ).transpose(0, 1, 4, 2, 3) for o in out)


HG_BLOCK = 64
HG_STATE_UNROLL = 8
HG_DEC_ROWS = 8


def _hgrn_kernel(*refs, seq, has_state):
    if has_state:
        (q_ref, ff_ref, fb_ref, v_ref, g_ref, lb_ref, gain_ref, s0_ref,
         o_ref, of_ref, ob_ref, qin_ref, kend_ref, dec_ref, st_ref) = refs
    else:
        (q_ref, ff_ref, fb_ref, v_ref, g_ref, lb_ref, gain_ref,
         o_ref, sout_ref, of_ref, ob_ref, qin_ref, kend_ref, dec_ref, st_ref) = refs
    c = HG_CHUNK
    hc = c // 2
    rb = HG_BLOCK
    per_block = rb // c
    n_chunks = seq // c
    width = HG_HEADS * HG_DK
    sub = HG_DEC_ROWS

    ri = lax.broadcasted_iota(jnp.int32, (rb, rb), 0)
    ci = lax.broadcasted_iota(jnp.int32, (rb, rb), 1)
    same = (ri // c) == (ci // c)
    tri_f = jnp.where(jnp.logical_and(same, ci <= ri), 1.0, 0.0).astype(BF16)
    tri_b = jnp.where(jnp.logical_and(same, ci >= ri), 1.0, 0.0).astype(BF16)
    ones = jnp.ones((HG_DK, HG_DV), BF16)
    rowid = lax.broadcasted_iota(jnp.int32, (c, HG_DV), 0)
    laneid = lax.broadcasted_iota(jnp.int32, (c, HG_DV), 1)
    laneid_half = lax.broadcasted_iota(jnp.int32, (hc, HG_DV), 1)
    o_refs = (of_ref, ob_ref)

    n_blocks = seq // rb

    def block_of(d, step):
        return step if d == 0 else n_blocks - 1 - step

    heads = [slice(h * HG_DK, (h + 1) * HG_DK) for h in range(HG_HEADS)]
    per_chunk = hc * c + hc * hc
    dirs = (0, 1)

    def pair_block(step):
        fwd = [True, False]
        pre_refs = (ff_ref, fb_ref)
        tris = (tri_f, tri_b)
        r0 = [_aligned(block_of(d, step) * rb, rb) for d in dirs]
        wide = [range(0, hc), range(hc, c)]
        narrow = [range(hc, c), range(0, hc)]
        far = [slice(hc, c), slice(0, hc)]
        q_all = [q_ref[pl.ds(r0[d], rb), :] * (HG_DK ** -0.5) for d in dirs]
        v16_all = [v_ref[pl.ds(r0[d], rb), :].astype(BF16) for d in dirs]
        k_all, log_f = [], []
        for d in dirs:
            lb = lb_ref[d:d + 1, :]
            f = lb + (1.0 - lb) * jax.nn.sigmoid(pre_refs[d][pl.ds(r0[d], rb), :])
            k_all.append(1.0 - f)
            log_f.append(jnp.log(f))
        b_all = [_dot_exact_lhs(tris[d], log_f[d]) * LOG2E for d in dirs]
        src_all = [b_all[d] - jnp.log(k_all[d]) * LOG2E for d in dirs]
        pairs = []
        for d in dirs:
            blocks, q_in, k_end = [], [], []
            for m in range(per_block):
                rows = slice(m * c, (m + 1) * c)
                q, k, b, src = q_all[d][rows], k_all[d][rows], b_all[d][rows], src_all[d][rows]
                b_last = b[c - 1:c] if fwd[d] else b[0:1]
                q_in.append(q * jnp.exp2(b))
                k_end.append(k * jnp.exp2(b_last - b))
                dec_row = _aligned((block_of(d, step) * per_block + m) * sub, sub)
                dec_ref[d, pl.ds(dec_row, sub), :] = jnp.broadcast_to(jnp.exp2(b_last), (sub, width))
                for s in wide[d]:
                    blocks.append((q * jnp.exp2(b - src[s:s + 1])).astype(BF16))
                q_far, b_far = q[far[d]], b[far[d]]
                halves = [q_far * jnp.exp2(b_far - src[s:s + 1]) for s in narrow[d]]
                for i in range(0, hc, 2):
                    blocks.append(jnp.concatenate(halves[i:i + 2], axis=0).astype(BF16))
            qin_ref[d, pl.ds(r0[d], rb), :] = jnp.concatenate(q_in, axis=0).astype(BF16)
            kend_ref[d, pl.ds(r0[d], rb), :] = jnp.concatenate(k_end, axis=0).astype(BF16)
            pairs.append(jnp.concatenate(blocks, axis=0))
        sums = [[_dot(pairs[d][:, sl], ones) for sl in heads] for d in dirs]
        cells = [(d, h, m) for d in dirs for h in range(HG_HEADS) for m in range(per_block)]
        attn = {cell: jnp.zeros((c, HG_DV), F32) for cell in cells}
        far_attn = {cell: jnp.zeros((hc, HG_DV), F32) for cell in cells}
        for n in range(hc):
            for d, h, m in cells:
                base = m * per_chunk
                first = base + hc * c + n * hc
                attn[d, h, m] = jnp.where(laneid == wide[d][n], sums[d][h][base + n * c:base + (n + 1) * c],
                                          attn[d, h, m])
                far_attn[d, h, m] = jnp.where(laneid_half == narrow[d][n], sums[d][h][first:first + hc],
                                              far_attn[d, h, m])
        zero = jnp.zeros((hc, HG_DV), F32)
        v_pad = jnp.zeros((HG_DK - c, HG_DV), BF16)
        causal = [laneid <= rowid, laneid >= rowid]
        prods = {}
        for d, h, m in cells:
            halves = [zero, far_attn[d, h, m]] if fwd[d] else [far_attn[d, h, m], zero]
            full = jnp.where(causal[d], attn[d, h, m] + jnp.concatenate(halves, axis=0), 0.0).astype(BF16)
            values = jnp.concatenate([v16_all[d][m * c:(m + 1) * c, heads[h]], v_pad], axis=0)
            prods[d, h, m] = _dot(full, values)
        for d in dirs:
            o_refs[d][pl.ds(r0[d], rb), :] = jnp.concatenate(
                [jnp.concatenate([prods[d, h, m] for m in range(per_block)], axis=0) for h in range(HG_HEADS)], axis=1)


    def state_step(n, carry):
        for d in range(2):
            chunk = n if d == 0 else n_chunks - 1 - n
            r0 = _aligned(chunk * c, c)
            q_in = qin_ref[d, pl.ds(r0, c), :]
            k_end = kend_ref[d, pl.ds(r0, c), :]
            v16 = v_ref[pl.ds(r0, c), :].astype(BF16)
            dec = dec_ref[d, pl.ds(_aligned(chunk * sub, sub), sub), :][0:1]
            outs = []
            for h in range(HG_HEADS):
                sl = slice(h * HG_DK, (h + 1) * HG_DK)
                st = st_ref[d, h]
                outs.append(_dot_nt(q_in[:, sl], st.astype(BF16)))
                st_ref[d, h] = st * dec[:, sl] + _dot_tn(v16[:, sl], k_end[:, sl])
            o_refs[d][pl.ds(r0, c), :] += jnp.concatenate(outs, axis=1)
        return carry

    if has_state:
        for d in range(2):
            for h in range(HG_HEADS):
                st_ref[d, h] = s0_ref[d, h].T
    else:
        st_ref[...] = jnp.zeros(st_ref.shape, F32)

    def pair_step(i, carry):
        pair_block(i)
        return carry

    lax.fori_loop(0, n_blocks, pair_step, 0)
    lax.fori_loop(0, n_chunks, state_step, 0, unroll=HG_STATE_UNROLL)

    for h in range(HG_HEADS):
        sl = slice(h * HG_DV, (h + 1) * HG_DV)
        o = of_ref[:, sl] + ob_ref[:, sl]
        o_ref[:, sl] = (_rms(o) * gain_ref[:, sl] * _silu(g_ref[:, sl])).astype(BF16)
    if not has_state:
        for d in range(2):
            for h in range(HG_HEADS):
                sout_ref[d, h] = st_ref[d, h].T


def _hgrn(ymix, lb, gain, batch, seq, state, layer):
    width = HG_HEADS * HG_DK
    has_state = state is not None
    assert HG_DK == HG_DV
    st_shape = (2, HG_HEADS, HG_DK, HG_DV)
    in_specs = [pl.BlockSpec((seq, width), functools.partial(lambda b, col: (b, col), col=col))
                for col in (COL_HQ, COL_HFF, COL_HFB, COL_HI, COL_HG)]
    in_specs += [pl.BlockSpec((2, width), lambda b: (0, 0)), pl.BlockSpec((1, width), lambda b: (0, 0))]
    args = [ymix] * 5 + [lb, gain]
    out_specs = [pl.BlockSpec((seq, width), lambda b: (b, 0))]
    out_shape = [jax.ShapeDtypeStruct((M_ROWS, width), BF16)]
    if has_state:
        in_specs.append(pl.BlockSpec((None, None) + st_shape, lambda b: (b, layer, 0, 0, 0, 0)))
        args.append(state)
    else:
        out_specs.append(pl.BlockSpec((None,) + st_shape, lambda b: (b, 0, 0, 0, 0)))
        out_shape.append(jax.ShapeDtypeStruct((batch,) + st_shape, F32))
    return pl.pallas_call(
        functools.partial(_hgrn_kernel, seq=seq, has_state=has_state),
        grid=(batch,),
        in_specs=in_specs,
        out_specs=out_specs,
        out_shape=out_shape,
        scratch_shapes=[pltpu.VMEM((seq, width), F32), pltpu.VMEM((seq, width), F32),
                        pltpu.VMEM((2, seq, width), BF16), pltpu.VMEM((2, seq, width), BF16),
                        pltpu.VMEM((2, seq // HG_CHUNK * HG_DEC_ROWS, width), F32),
                        pltpu.VMEM(st_shape, F32)],
        compiler_params=_params("parallel"),
        name="hgrn_lat" if has_state else "hgrn_ctx",
    )(*args)


def _merge_kernel(oa_ref, ob_ref, oc_ref, od_ref, gt_ref, wb_ref, wo_ref, x_ref, g1_ref, out_ref):
    acc = None
    for n, o_ref in enumerate((oa_ref, ob_ref, oc_ref, od_ref)):
        bo = _dot(o_ref[...], wb_ref[n])
        term = gt_ref[:, n * D_MODEL:(n + 1) * D_MODEL] * bo
        acc = term if acc is None else acc + term
    out_ref[...] = x_ref[...] + g1_ref[...] * _dot(acc.astype(BF16), wo_ref[...])


def _merge(branches, gates, w_branch, w_out, x, mod, latent):
    tm = 512
    row = _mod_row(latent, tm)
    tile = lambda w: pl.BlockSpec((tm, w), lambda i: (i, 0))
    return pl.pallas_call(
        _merge_kernel,
        grid=(M_ROWS // tm,),
        in_specs=[tile(BRANCH_W)] * N_BRANCH + [
            tile(GATE_W),
            pl.BlockSpec((N_BRANCH, BRANCH_W, D_MODEL), lambda i: (0, 0, 0)),
            pl.BlockSpec((D_MODEL, D_MODEL), lambda i: (0, 0)),
            tile(D_MODEL),
            pl.BlockSpec((None, None, 1, D_MODEL), lambda i: (row(i), 2, 0, 0)),
        ],
        out_specs=tile(D_MODEL),
        out_shape=jax.ShapeDtypeStruct((M_ROWS, D_MODEL), F32),
        compiler_params=_params("parallel"),
        name="merge_lat" if latent else "merge_ctx",
    )(*branches, gates, w_branch, w_out, x, mod)


FFN_CHUNK = 256
FFN_STEPS = FFN_HIDDEN // FFN_CHUNK
assert FFN_STEPS * FFN_CHUNK == FFN_HIDDEN


def _ffn_kernel(*refs, final):
    x_ref, sc_ref, sh_ref, g2_ref, wa_ref, wg_ref, wo_ref = refs[:7]
    if final:
        fg_ref, out_ref, h_ref, acc_ref = refs[7:]
    else:
        out_ref, h_ref, acc_ref = refs[7:]
    j = pl.program_id(1)

    @pl.when(j == 0)
    def _():
        h = _rms(x_ref[...]) * (1.0 + sc_ref[...]) + sh_ref[...]
        h_ref[...] = h.astype(BF16)
        acc_ref[...] = jnp.zeros(acc_ref.shape, F32)

    h = h_ref[...]
    act = _silu(_dot(h, wg_ref[...].astype(BF16))) * _dot(h, wa_ref[...].astype(BF16))
    acc_ref[...] += _dot(act.astype(BF16), wo_ref[...].astype(BF16))

    @pl.when(j == FFN_STEPS - 1)
    def _():
        y = x_ref[...] + g2_ref[...] * acc_ref[...]
        out_ref[...] = _rms(y) * fg_ref[...] if final else y


def _ffn(x, mod, w_in, w_out, layer, latent, final_gain):
    tm = 1024
    row = _mod_row(latent, tm)
    modspec = lambda which: pl.BlockSpec((None, None, 1, D_MODEL), lambda i, j: (row(i), which, 0, 0))
    final = final_gain is not None
    in_specs = [
        pl.BlockSpec((tm, D_MODEL), lambda i, j: (i, 0)),
        modspec(4), modspec(3), modspec(5),
        pl.BlockSpec((None, D_MODEL, FFN_CHUNK), lambda i, j: (layer, 0, j)),
        pl.BlockSpec((None, D_MODEL, FFN_CHUNK), lambda i, j: (layer, 0, FFN_STEPS + j)),
        pl.BlockSpec((None, FFN_CHUNK, D_MODEL), lambda i, j: (layer, j, 0)),
    ]
    args = [x, mod, mod, mod, w_in, w_in, w_out]
    if final:
        in_specs.append(pl.BlockSpec((1, D_MODEL), lambda i, j: (0, 0)))
        args.append(final_gain.reshape(1, D_MODEL))
    return pl.pallas_call(
        functools.partial(_ffn_kernel, final=final),
        grid=(M_ROWS // tm, FFN_STEPS),
        in_specs=in_specs,
        out_specs=pl.BlockSpec((tm, D_MODEL), lambda i, j: (i, 0)),
        out_shape=jax.ShapeDtypeStruct((M_ROWS, D_MODEL), F32),
        scratch_shapes=[pltpu.VMEM((tm, D_MODEL), BF16), pltpu.VMEM((tm, D_MODEL), F32)],
        compiler_params=_params("parallel", "arbitrary"),
        name="ffn_lat" if latent else "ffn_ctx",
    )(*args)


def _rope_tables():
    t = jnp.arange(DEC_SEQ)
    row = (t // GRID_W).astype(F32)[:, None]
    col = (t % GRID_W).astype(F32)[:, None]

    def angles(rot_dim):
        n_freq = rot_dim // 4
        inv_freq = ROPE_THETA ** (-jnp.arange(n_freq, dtype=F32) / n_freq)
        return jnp.concatenate([row * inv_freq, col * inv_freq], axis=-1)

    def expand(ang):
        cos = jnp.repeat(jnp.cos(ang), 2, axis=-1)
        sin = jnp.repeat(jnp.sin(ang), 2, axis=-1)
        even = (jnp.arange(cos.shape[-1]) % 2 == 0)[None, :]
        return cos, jnp.where(even, -sin, 0.0), jnp.where(even, 0.0, sin)

    gq = [jnp.concatenate([a, a], axis=-1) for a in expand(angles(GQ_HD))]
    ml = []
    for idx, a in enumerate(expand(angles(ML_ROPE))):
        fill = 1.0 if idx == 0 else 0.0
        ml.append(jnp.concatenate([a, jnp.full((DEC_SEQ, LANE - ML_ROPE), fill, F32)], axis=-1))
    return {"gq": jnp.stack(gq), "ml": jnp.stack(ml)}


def _layer_weights(l, gq_q_gain, gq_k_gain, ml_q_a_gain, ml_kv_a_gain, ml_w_q_b, ml_w_kv_b,
                   w_branch, w_out, hg_gain, avg):
    pad = LANE - ML_NOPE - ML_ROPE
    qb = ml_w_q_b[l].reshape(ML_Q_RANK, ML_HEADS, ML_NOPE + ML_ROPE)
    qb = jnp.concatenate([qb[:, :, ML_NOPE:], qb[:, :, :ML_NOPE], jnp.zeros((ML_Q_RANK, ML_HEADS, pad), F32)],
                         axis=-1).reshape(ML_Q_RANK, ML_HEADS * LANE)
    kvb = ml_w_kv_b[l].reshape(ML_KV_RANK, ML_HEADS, ML_NOPE + ML_V)
    wk = jnp.pad(kvb[:, :, :ML_NOPE], ((0, 0), (0, 0), (ML_ROPE, pad))).reshape(ML_KV_RANK, ML_HEADS * LANE)
    wv = kvb[:, :, ML_NOPE:].reshape(ML_KV_RANK, ML_HEADS * ML_V)
    return {
        "gq_q_gain": jnp.tile(gq_q_gain[l], GQ_HEADS).reshape(1, -1),
        "gq_k_gain": jnp.tile(gq_k_gain[l], GQ_KV_HEADS).reshape(1, -1),
        "ml_q_gain": ml_q_a_gain[l].reshape(1, -1), "ml_kv_gain": ml_kv_a_gain[l].reshape(1, -1),
        "w_qb": qb.astype(BF16), "w_k": wk.astype(BF16), "w_v": wv.astype(BF16),
        "w_branch": w_branch[l].astype(BF16), "w_out": w_out[l].astype(BF16),
        "hg_gain": jnp.tile(hg_gain[l], HG_HEADS).reshape(1, -1), "avg": avg,
    }


def _seg(arr, width, col, rows, stride, off):
    return (arr, width, col, rows, stride, off)


def kernel(x_prompt, x_sample, state_hgrn, cache_gqa_k, cache_gqa_v, cache_na_k, cache_na_v, cache_mla_ckv, cache_mla_krope, c, c_ctx, w_ada, b_ada, w_in, hg_lb_logits, hg_gain, gq_q_gain, gq_k_gain, na_rpb, ml_q_a_gain, ml_kv_a_gain, ml_w_q_b, ml_w_kv_b, w_branch, w_out, w_ffn_in, w_ffn_out, final_gain):
    cond8 = jnp.concatenate([c_ctx[None, :], c, jnp.zeros((8 - 1 - DEC_BATCH, D_MODEL), F32)], axis=0)
    mods = _ada(cond8, w_ada, b_ada)

    lb = jnp.cumsum(jax.nn.softmax(hg_lb_logits.astype(F32), axis=0), axis=0)
    lb = lb - lb[:1]
    avg = jnp.asarray(np.kron(np.eye(512 // GQ_HD), np.full((GQ_HD, GQ_HD), 1.0 / GQ_HD)), BF16)
    tabs = _rope_tables()
    w_in_t = jnp.swapaxes(w_in, 1, 2).reshape(DEPTH * w_in.shape[2], D_MODEL)

    gqk_c = cache_gqa_k.reshape(DEC_BATCH * DEPTH * PAST_LEN, GQ_KV_HEADS * GQ_HD)
    gqv_c = cache_gqa_v.reshape(DEC_BATCH * DEPTH * PAST_LEN, GQ_KV_HEADS * GQ_HD)
    nak_c = cache_na_k.transpose(0, 1, 3, 4, 2).reshape(DEC_BATCH * DEPTH * NA_HEADS * NA_HD, PAST_LEN)
    nav_c = cache_na_v.transpose(0, 1, 3, 4, 2).reshape(DEC_BATCH * DEPTH * NA_HEADS * NA_HD, PAST_LEN)
    mckv_c = cache_mla_ckv.reshape(DEC_BATCH * DEPTH * PAST_LEN, ML_KV_RANK)
    mkr_c = jnp.pad(cache_mla_krope.reshape(DEC_BATCH * DEPTH * PAST_LEN, ML_ROPE),
                    ((0, 0), (0, LANE - ML_ROPE)))

    xp = x_prompt.reshape(M_ROWS, D_MODEL)
    xs = x_sample.reshape(M_ROWS, D_MODEL)
    new = []
    ymix_ctx = []
    for l in range(DEPTH):
        lw = _layer_weights(l, gq_q_gain, gq_k_gain, ml_q_a_gain, ml_kv_a_gain, ml_w_q_b,
                            ml_w_kv_b, w_branch, w_out, hg_gain, avg)
        last = l == DEPTH - 1
        mod = mods[l]

        ymix = _inproj(xp, mod, w_in_t, l, False, gate=False)
        gates = _inproj(xp, mod, w_in_t, l, False, gate=True)
        qb, kb, qd, ckv, kd, vd = _prep(ymix, lw, tabs, False)
        out_a, st = _hgrn(ymix, lb[l], lw["hg_gain"], BATCH, SEQ, None, l)
        out_b = _attention((qb, 512, 0, 0),
                           [(_seg(kb, LANE, 0, SEQ, 1, 0), _seg(ymix, LANE, COL_GV, SEQ, 1, 0))],
                           _plan_gqa(), GQ_SCALE, BATCH, SEQ, "gqa_ctx")
        out_c = _attention((ymix, 512, COL_NQ, 0),
                           [(_seg(ymix, 512, COL_NK, SEQ, 1, 0), _seg(ymix, 512, COL_NV, SEQ, 1, 0))],
                           _plan_na(), NA_SCALE, BATCH, SEQ, "na_ctx")
        out_d = _attention((qd, ML_HEADS * LANE, 0, 0),
                           [(_seg(kd, ML_HEADS * LANE, 0, SEQ, 1, 0), _seg(vd, 512, 0, SEQ, 1, 0))],
                           _plan_mla(), ML_SCALE, BATCH, SEQ, "mla_ctx")
        xp = _merge((out_a, out_b, out_c, out_d), gates, lw["w_branch"], lw["w_out"], xp, mod, False)
        xp = _ffn(xp, mod, w_ffn_in, w_ffn_out, l, False, final_gain if last else None)
        new.append((
            st,
            kb.reshape(BATCH, SEQ, GQ_KV_HEADS, GQ_HD),
            ymix[:, COL_GV * LANE:(COL_GV + 1) * LANE].reshape(BATCH, SEQ, GQ_KV_HEADS, GQ_HD),
            ckv.reshape(BATCH, SEQ, ML_KV_RANK),
            ymix[:, COL_MKR * LANE:COL_MKR * LANE + ML_ROPE].reshape(BATCH, SEQ, ML_ROPE),
        ))
        ymix_ctx.append(ymix)

        ymix = _inproj(xs, mod, w_in_t, l, True, gate=False)
        gates = _inproj(xs, mod, w_in_t, l, True, gate=True)
        qb, kb, qd, ckv, kd, vd = _prep(ymix, lw, tabs, True)
        kd_c, vd_c = _mla_cache(mckv_c, mkr_c, lw, l)
        out_a, = _hgrn(ymix, lb[l], lw["hg_gain"], DEC_BATCH, DEC_SEQ, state_hgrn, l)
        out_b = _attention((qb, 512, 0, 0),
                           [(_seg(gqk_c, LANE, 0, PAST_LEN, DEPTH, l), _seg(gqv_c, LANE, 0, PAST_LEN, DEPTH, l)),
                            (_seg(kb, LANE, 0, DEC_SEQ, 1, 0), _seg(ymix, LANE, COL_GV, DEC_SEQ, 1, 0))],
                           _plan_gqa(), GQ_SCALE, DEC_BATCH, DEC_SEQ, "gqa_lat")
        out_c = _na_latent(ymix, nak_c, nav_c, _na_bias_tables(na_rpb[l]), l)
        out_d = _attention((qd, ML_HEADS * LANE, 0, 0),
                           [(_seg(kd_c, ML_HEADS * LANE, 0, PAST_LEN, 1, 0), _seg(vd_c, 512, 0, PAST_LEN, 1, 0)),
                            (_seg(kd, ML_HEADS * LANE, 0, DEC_SEQ, 1, 0), _seg(vd, 512, 0, DEC_SEQ, 1, 0))],
                           _plan_mla(), ML_SCALE, DEC_BATCH, DEC_SEQ, "mla_lat")
        xs = _merge((out_a, out_b, out_c, out_d), gates, lw["w_branch"], lw["w_out"], xs, mod, True)
        xs = _ffn(xs, mod, w_ffn_in, w_ffn_out, l, True, final_gain if last else None)

    y_prompt = xp.reshape(BATCH, SEQ, D_MODEL)
    y_sample = xs.reshape(DEC_BATCH, DEC_SEQ, D_MODEL)
    state, gqa_k, gqa_v, ckv_new, krope_new = (jnp.stack([n[i] for n in new], axis=1) for i in range(5))
    na_k, na_v = _na_cache_out(ymix_ctx)
    return (y_prompt, y_sample, state, gqa_k, gqa_v, na_k, na_v, ckv_new, krope_new)
```

```python
import functools

import numpy as np
import jax
import jax.numpy as jnp
from jax import lax
from jax.experimental import pallas as pl
from jax.experimental.pallas import tpu as pltpu

F32 = jnp.float32
BF16 = jnp.bfloat16

D_MODEL = 1024
BATCH = 16
SEQ = 256
DEPTH = 2
DEC_BATCH = 4
DEC_SEQ = 1024
PAST_LEN = 512
GRID_W = 64
EPS = 1e-6
ROPE_THETA = 10000.0
N_BRANCH = 4
BRANCH_W = 512
HG_HEADS = 4
HG_DK = 128
HG_DV = 128
GQ_HEADS = 8
GQ_KV_HEADS = 2
GQ_HD = 64
NA_HEADS = 8
NA_HD = 64
NA_WIN_R = 8
NA_WIN_C = 16
ML_HEADS = 8
ML_NOPE = 64
ML_ROPE = 32
ML_V = 64
ML_Q_RANK = 256
ML_KV_RANK = 128
FFN_HIDDEN = 2816
GQ_SCALE = GQ_HD ** -0.5
NA_SCALE = NA_HD ** -0.5
ML_SCALE = (ML_NOPE + ML_ROPE) ** -0.5

M_ROWS = BATCH * SEQ
assert M_ROWS == DEC_BATCH * DEC_SEQ

LANE = 128
HALF = 64
MIX_W = 5376
IN_MIX = 5280
GATE_W = N_BRANCH * D_MODEL
HG_CHUNK = 16
NEG = -1e30
LOG2E = 1.4426950408889634
VMEM_LIMIT = 56 * 1024 * 1024

COL_HQ, COL_HFF, COL_HFB, COL_HI, COL_HG, COL_GQ, COL_NQ, COL_NK, COL_NV = range(9)
COL_MQA = 18
COL_GK, COL_GV, COL_MKVA, COL_MKR = 38, 39, 40, 41


def _dot(a, b):
    return jnp.dot(a, b, preferred_element_type=F32)


def _dot_nt(a, b):
    return lax.dot_general(a, b, (((1,), (1,)), ((), ())), preferred_element_type=F32)


def _dot_tn(a, b):
    return lax.dot_general(a, b, (((0,), (0,)), ((), ())), preferred_element_type=F32)


def _split3(x):
    x1 = x.astype(BF16)
    r1 = x - x1.astype(F32)
    x2 = r1.astype(BF16)
    x3 = (r1 - x2.astype(F32)).astype(BF16)
    return x1, x2, x3


def _dot_exact_lhs(a_bf16, x):
    x1, x2, x3 = _split3(x)
    return (_dot(a_bf16, x3) + _dot(a_bf16, x2)) + _dot(a_bf16, x1)


def _dot_exact_rhs(x, b_bf16):
    x1, x2, x3 = _split3(x)
    return (_dot(x3, b_bf16) + _dot(x2, b_bf16)) + _dot(x1, b_bf16)


def _rms(x):
    return x * lax.rsqrt(jnp.mean(x * x, axis=-1, keepdims=True) + EPS)


def _silu(x):
    return x * jax.nn.sigmoid(x)


def _aligned(x, m):
    return x if isinstance(x, int) else pl.multiple_of(x, m)


def _params(*sem):
    return pltpu.CompilerParams(dimension_semantics=sem, vmem_limit_bytes=VMEM_LIMIT)


def _mod_row(latent, tm):
    if latent:
        return lambda i: 1 + (i * tm) // DEC_SEQ
    return lambda i: 0


def _ada_kernel(c_ref, w_ref, b_ref, o_ref):
    c = c_ref[...]
    o_ref[...] = _dot(_silu(c).astype(BF16), w_ref[...].astype(BF16)) + b_ref[...]


def _ada(cond8, w_ada, b_ada):
    tn = 1536
    out = pl.pallas_call(
        _ada_kernel,
        grid=(DEPTH, 6 * D_MODEL // tn),
        in_specs=[
            pl.BlockSpec((8, D_MODEL), lambda l, j: (0, 0)),
            pl.BlockSpec((None, D_MODEL, tn), lambda l, j: (l, 0, j)),
            pl.BlockSpec((None, 1, tn), lambda l, j: (l, 0, j)),
        ],
        out_specs=pl.BlockSpec((None, 8, tn), lambda l, j: (l, 0, j)),
        out_shape=jax.ShapeDtypeStruct((DEPTH, 8, 6 * D_MODEL), F32),
        compiler_params=_params("parallel", "parallel"),
        name="ada",
    )(cond8, w_ada, b_ada.reshape(DEPTH, 1, 6 * D_MODEL))
    return out.reshape(DEPTH, 8, 6, 1, D_MODEL)


IN_PIECE = 256
IN_TM = 1024


def _mix_source(t):
    blk = jnp.where(t < 12, t, jnp.where(t < 18, t + 1, jnp.where(t == 18, 19, jnp.where(t == 19, 12, 20))))
    return blk * IN_PIECE


def _gate_source(t):
    return IN_MIX + t * IN_PIECE


def _inproj_kernel(*refs, pieces, gate):
    x_ref, sc_ref, sh_ref = refs[:3]
    w_refs = refs[3:3 + pieces]
    o_ref, h_ref = refs[3 + pieces:]
    rows = pl.ds(pl.multiple_of(pl.program_id(1) * IN_TM, IN_TM), IN_TM)

    @pl.when(pl.program_id(0) == 0)
    def _():
        h = _rms(x_ref[...]) * (1.0 + sc_ref[...]) + sh_ref[...]
        h_ref[rows, :] = h.astype(BF16)

    h = h_ref[rows, :]
    for p, w_ref in enumerate(w_refs):
        y = _dot_nt(h, w_ref[...].astype(BF16))
        o_ref[:, p * IN_PIECE:(p + 1) * IN_PIECE] = jax.nn.sigmoid(y).astype(o_ref.dtype) if gate else y


def _inproj(x, mod, w_in_t, layer, latent, gate):
    width, tile, source = (GATE_W, 2048, _gate_source) if gate else (MIX_W, 1792, _mix_source)
    pieces = tile // IN_PIECE
    row = _mod_row(latent, IN_TM)
    first = lambda c, i: jnp.where(c == 0, i, 0)
    in_width = w_in_t.shape[0] // DEPTH
    piece = lambda p: pl.BlockSpec((pl.Element(IN_PIECE), pl.Element(D_MODEL)),
                                   lambda c, i: (pl.multiple_of(layer * in_width + source(c * pieces + p), 8), 0))
    return pl.pallas_call(
        functools.partial(_inproj_kernel, pieces=pieces, gate=gate),
        grid=(width // tile, M_ROWS // IN_TM),
        in_specs=[
            pl.BlockSpec((IN_TM, D_MODEL), lambda c, i: (first(c, i), 0)),
            pl.BlockSpec((None, None, 1, D_MODEL), lambda c, i: (row(first(c, i)), 1, 0, 0)),
            pl.BlockSpec((None, None, 1, D_MODEL), lambda c, i: (row(first(c, i)), 0, 0, 0)),
        ] + [piece(p) for p in range(pieces)],
        out_specs=pl.BlockSpec((IN_TM, tile), lambda c, i: (i, c)),
        out_shape=jax.ShapeDtypeStruct((M_ROWS, width), BF16 if gate else F32),
        scratch_shapes=[pltpu.VMEM((M_ROWS, D_MODEL), BF16)],
        compiler_params=_params("arbitrary", "arbitrary"),
        name="inproj_" + ("gate_" if gate else "mix_") + ("lat" if latent else "ctx"),
    )(x, mod, mod, *([w_in_t] * pieces))


def _rope(x, tab_ref, reps):
    w = x.shape[1]
    c = jnp.concatenate([tab_ref[0]] * reps, axis=1) if reps > 1 else tab_ref[0]
    se = jnp.concatenate([tab_ref[1]] * reps, axis=1) if reps > 1 else tab_ref[1]
    so = jnp.concatenate([tab_ref[2]] * reps, axis=1) if reps > 1 else tab_ref[2]
    return x * c + pltpu.roll(x, w - 1, 1) * se + pltpu.roll(x, 1, 1) * so


def _head_rms(x, gain, avg_bf16):
    ms = _dot_exact_rhs(x * x, avg_bf16)
    return x * lax.rsqrt(ms + EPS) * gain


def _prep_kernel(*refs, rope):
    (gq_ref, gk_ref, mqa_ref, mkva_ref, mkr_ref, gqg_ref, gkg_ref, mqg_ref, mkvg_ref,
     wqb_ref, wk_ref, wv_ref, avg_ref) = refs[:13]
    if rope:
        gtab_ref, mtab_ref = refs[13:15]
        outs = refs[15:]
    else:
        outs = refs[13:]
    qb_ref, kb_ref, qd_ref, ckv_ref, kd_ref, vd_ref = outs

    q = _head_rms(gq_ref[...], gqg_ref[...], avg_ref[...])
    k = _head_rms(gk_ref[...], gkg_ref[...], avg_ref[0:LANE, 0:LANE])
    qd = _dot((_rms(mqa_ref[...]) * mqg_ref[...]).astype(BF16), wqb_ref[...])
    ckv = _rms(mkva_ref[...]) * mkvg_ref[...]
    lane = lax.broadcasted_iota(jnp.int32, (1, LANE), 1)
    kr = jnp.where(lane < ML_ROPE, mkr_ref[...], 0.0)
    if rope:
        q = _rope(q, gtab_ref, GQ_HEADS * GQ_HD // LANE)
        k = _rope(k, gtab_ref, 1)
        qd = _rope(qd, mtab_ref, ML_HEADS)
        kr = _rope(kr, mtab_ref, 1)
    qb_ref[...] = q.astype(BF16)
    kb_ref[...] = k
    qd_ref[...] = qd.astype(BF16)
    ckv_ref[...] = ckv
    cb = ckv.astype(BF16)
    kd_ref[...] = (_dot(cb, wk_ref[...]) + jnp.concatenate([kr] * ML_HEADS, axis=1)).astype(BF16)
    vd_ref[...] = _dot(cb, wv_ref[...]).astype(BF16)


def _prep(ymix, lw, tabs, latent):
    tm = 512
    const = lambda i: (0, 0)
    in_specs = [
        pl.BlockSpec((tm, 512), lambda i: (i, COL_GQ)),
        pl.BlockSpec((tm, LANE), lambda i: (i, COL_GK)),
        pl.BlockSpec((tm, 256), lambda i: (i, COL_MQA)),
        pl.BlockSpec((tm, LANE), lambda i: (i, COL_MKVA)),
        pl.BlockSpec((tm, LANE), lambda i: (i, COL_MKR)),
        pl.BlockSpec((1, 512), const),
        pl.BlockSpec((1, LANE), const),
        pl.BlockSpec((1, 256), const),
        pl.BlockSpec((1, LANE), const),
        pl.BlockSpec((ML_Q_RANK, ML_HEADS * LANE), const),
        pl.BlockSpec((ML_KV_RANK, ML_HEADS * LANE), const),
        pl.BlockSpec((ML_KV_RANK, ML_HEADS * ML_V), const),
        pl.BlockSpec((512, 512), const),
    ]
    args = [ymix, ymix, ymix, ymix, ymix, lw["gq_q_gain"], lw["gq_k_gain"], lw["ml_q_gain"],
            lw["ml_kv_gain"], lw["w_qb"], lw["w_k"], lw["w_v"], lw["avg"]]
    if latent:
        per = DEC_SEQ // tm
        in_specs += [pl.BlockSpec((3, tm, LANE), lambda i: (0, i % per, 0))] * 2
        args += [tabs["gq"], tabs["ml"]]
    widths = (512, LANE, ML_HEADS * LANE, LANE, ML_HEADS * LANE, ML_HEADS * ML_V)
    return pl.pallas_call(
        functools.partial(_prep_kernel, rope=latent),
        grid=(M_ROWS // tm,),
        in_specs=in_specs,
        out_specs=[pl.BlockSpec((tm, w), lambda i: (i, 0)) for w in widths],
        out_shape=[jax.ShapeDtypeStruct((M_ROWS, w), dt)
                   for w, dt in zip(widths, (BF16, F32, BF16, F32, BF16, BF16))],
        compiler_params=_params("parallel"),
        name="prep_lat" if latent else "prep_ctx",
    )(*args)


def _mla_cache_kernel(ckv_ref, kr_ref, wk_ref, wv_ref, kd_ref, vd_ref):
    cb = ckv_ref[...].astype(BF16)
    kd_ref[...] = (_dot(cb, wk_ref[...]) + jnp.concatenate([kr_ref[...]] * ML_HEADS, axis=1)).astype(BF16)
    vd_ref[...] = _dot(cb, wv_ref[...]).astype(BF16)


def _mla_cache(ckv, kr_blk, lw, layer):
    rows = DEC_BATCH * PAST_LEN
    tm = PAST_LEN
    const = lambda i: (0, 0)
    return pl.pallas_call(
        _mla_cache_kernel,
        grid=(DEC_BATCH,),
        in_specs=[
            pl.BlockSpec((tm, LANE), lambda i: (i * DEPTH + layer, 0)),
            pl.BlockSpec((tm, LANE), lambda i: (i * DEPTH + layer, 0)),
            pl.BlockSpec((ML_KV_RANK, ML_HEADS * LANE), const),
            pl.BlockSpec((ML_KV_RANK, ML_HEADS * ML_V), const),
        ],
        out_specs=[pl.BlockSpec((tm, ML_HEADS * LANE), lambda i: (i, 0)),
                   pl.BlockSpec((tm, ML_HEADS * ML_V), lambda i: (i, 0))],
        out_shape=[jax.ShapeDtypeStruct((rows, ML_HEADS * LANE), BF16),
                   jax.ShapeDtypeStruct((rows, ML_HEADS * ML_V), BF16)],
        compiler_params=_params("parallel"),
        name="mla_cache",
    )(ckv, kr_blk, lw["w_k"], lw["w_v"])


def _softmax_pv(scores, values, transposed=None):
    transposed = transposed or [False] * len(values)
    m = scores[0].max(axis=-1, keepdims=True)
    for s in scores[1:]:
        m = jnp.maximum(m, s.max(axis=-1, keepdims=True))
    den = None
    out = None
    for s, v, vt in zip(scores, values, transposed):
        p = jnp.exp(s - m)
        d = p.sum(axis=-1, keepdims=True)
        o = _dot_nt(p.astype(BF16), v) if vt else _dot(p.astype(BF16), v)
        den = d if den is None else den + d
        out = o if out is None else out + o
    return out / den


def _softmax_pv_phased(scores, values):
    maxima = [s.max(axis=-1, keepdims=True) for s in scores]
    probs = [jnp.exp(s - m) for s, m in zip(scores, maxima)]
    dens = [p.sum(axis=-1, keepdims=True) for p in probs]
    outs = [_dot(p.astype(BF16), v) for p, v in zip(probs, values)]
    return [o / d for o, d in zip(outs, dens)]


def _lane_halves():
    lane = lax.broadcasted_iota(jnp.int32, (1, LANE), 1)
    return lane < HALF, lane >= HALF


def _attn_kernel(*refs, plan, nseg, scale, fold_scale):
    q_ref = refs[0]
    k_refs = [refs[1 + 2 * i] for i in range(nseg)]
    v_refs = [refs[2 + 2 * i] for i in range(nseg)]
    o_ref = refs[1 + 2 * nseg]
    tq = q_ref.shape[0]
    lo, hi = _lane_halves()
    cache = {}

    def block(kind, seg, blk, swap):
        key = (kind, seg, blk, swap)
        if key not in cache:
            ref = (k_refs if kind == "k" else v_refs)[seg]
            x = ref[:, blk * LANE:(blk + 1) * LANE]
            if swap:
                x = pltpu.roll(x, HALF, 1)
            cache[key] = x.astype(BF16)
        return cache[key]

    def queries(members):
        qs = []
        for qblk, qhalf, _, _ in members:
            q = q_ref[:, qblk * LANE:(qblk + 1) * LANE]
            if fold_scale:
                q = q * scale
            if qhalf is not None:
                q = jnp.where(lo if qhalf == 0 else hi, q, 0.0)
            qs.append(q)
        return (jnp.concatenate(qs, axis=0) if len(qs) > 1 else qs[0]).astype(BF16)

    def scores(q, kblk, swap):
        out = [_dot_nt(q, block("k", s, kblk, swap)) for s in range(nseg)]
        return out if fold_scale else [s * scale for s in out]

    if nseg == 1:
        all_scores = [scores(queries(members), kblk, swap)[0] for members, kblk, swap, _ in plan]
        outs = _softmax_pv_phased(all_scores, [block("v", 0, vblk, swap) for _, _, swap, vblk in plan])
    else:
        outs = []
        ready = scores(queries(plan[0][0]), plan[0][1], plan[0][2])
        for g, (_, _, swap, vblk) in enumerate(plan):
            if g + 1 < len(plan):
                following = scores(queries(plan[g + 1][0]), plan[g + 1][1], plan[g + 1][2])
            outs.append(_softmax_pv(ready, [block("v", s, vblk, swap) for s in range(nseg)]))
            ready = following
    parts = {}
    for (members, _, _, _), out in zip(plan, outs):
        for n, (_, _, oblk, ohalf) in enumerate(members):
            parts[(oblk, ohalf)] = out[n * tq:(n + 1) * tq]
    for oblk in sorted({key[0] for key in parts}):
        o_ref[:, oblk * LANE:(oblk + 1) * LANE] = jnp.where(lo, parts[(oblk, 0)], parts[(oblk, 1)]).astype(BF16)


def _plan_gqa():
    per_kv = GQ_HEADS // GQ_KV_HEADS
    plan = []
    for g in range(GQ_KV_HEADS):
        for half in range(2):
            heads = [h for h in range(g * per_kv, (g + 1) * per_kv) if h % 2 == half]
            plan.append((tuple((h // 2, half, h // 2, half) for h in heads), 0, half != g, 0))
    return tuple(plan)


def _plan_na():
    return tuple((((j, 0, j, 0), (j, 1, j, 1)), j, False, j) for j in range(NA_HEADS // 2))


def _plan_mla():
    return tuple((((h, None, h // 2, h % 2),), h, False, h // 2) for h in range(ML_HEADS))


def _is_pow2(x):
    return float(np.log2(x)).is_integer()


def _attention(q, segs, plan, scale, batch, tq_total, name, tq=256):
    tq = min(tq_total, tq)
    per = tq_total // tq
    q_arr, q_w, q_col, q_off = q
    in_specs = [pl.BlockSpec((tq, q_w), lambda b, i: (q_off + b * per + i, q_col))]
    args = [q_arr]
    for k, v in segs:
        for arr, w, col, rows, stride, off in (k, v):
            in_specs.append(pl.BlockSpec(
                (rows, w), functools.partial(lambda b, i, col, stride, off: (off + b * stride, col),
                                             col=col, stride=stride, off=off)))
            args.append(arr)
    out_w = 512
    return pl.pallas_call(
        functools.partial(_attn_kernel, plan=plan, nseg=len(segs), scale=scale, fold_scale=_is_pow2(scale)),
        grid=(batch, per),
        in_specs=in_specs,
        out_specs=pl.BlockSpec((tq, out_w), lambda b, i: (b * per + i, 0)),
        out_shape=jax.ShapeDtypeStruct((M_ROWS, out_w), BF16),
        compiler_params=_params("parallel", "parallel"),
        name=name,
    )(*args)


NA_DR = 2 * NA_WIN_R - 1
NA_DC = 2 * NA_WIN_C - 1
NA_ROWS = DEC_SEQ // GRID_W
NA_QROWS = 4
NA_WROWS = 12
NA_T_RIGHT = NA_DR - 1
NA_T_LEFT = NA_DR
NA_T_NONE = NA_DR + 1
NA_T_SIZE = NA_DR + 2
assert _is_pow2(NA_SCALE)


def _na_kernel(q_ref, kl_ref, vl_ref, kc_ref, vc_ref, tz_ref, o_ref, kc_scr, vc_scr):
    g = pl.program_id(1)

    @pl.when(g == 0)
    def _():
        kc_scr[...] = kc_ref[...].astype(BF16)
        vc_scr[...] = vc_ref[...].astype(BF16)

    w0 = (g // 2) * (NA_ROWS - NA_WROWS)
    k0 = pl.multiple_of(w0 * GRID_W, (NA_ROWS - NA_WROWS) * GRID_W)
    win = NA_WROWS * GRID_W
    nq = NA_QROWS * GRID_W
    entry = []
    for a in range(NA_QROWS):
        rq = g * NA_QROWS + a
        start = jnp.clip(rq - NA_WIN_R // 2, 0, NA_ROWS - NA_WIN_R)
        per_pair = []
        for i in range(NA_WROWS // 2):
            kr = w0 + 2 * i
            in_l = jnp.logical_and(kr >= start, kr < start + NA_WIN_R)
            in_r = jnp.logical_and(kr + 1 >= start, kr + 1 < start + NA_WIN_R)
            both = jnp.logical_and(in_l, in_r)
            d_l = kr - rq + NA_WIN_R - 1
            per_pair.append(jnp.where(both, d_l, jnp.where(in_r, NA_T_RIGHT, jnp.where(in_l, NA_T_LEFT, NA_T_NONE))))
        entry.append(per_pair)
    lo, hi = _lane_halves()
    def scores(j):
        sl = slice(j * LANE, (j + 1) * LANE)
        k_loc = kl_ref[pl.ds(k0, win), sl].astype(BF16)
        k_ctx_t = kc_scr[sl, :]
        q_pair = q_ref[:, sl] * NA_SCALE
        q = jnp.concatenate([jnp.where(lo, q_pair, 0.0), jnp.where(hi, q_pair, 0.0)], axis=0).astype(BF16)
        bias = jnp.concatenate(
            [jnp.concatenate([tz_ref[2 * j + p, e] for e in entry[a]], axis=1)
             for p in range(2) for a in range(NA_QROWS)], axis=0)
        return [_dot_nt(q, k_loc) + bias, _dot(q, k_ctx_t)]

    pairs = NA_HEADS // 2
    ready = scores(0)
    for j in range(pairs):
        sl = slice(j * LANE, (j + 1) * LANE)
        if j + 1 < pairs:
            following = scores(j + 1)
        v_loc = vl_ref[pl.ds(k0, win), sl].astype(BF16)
        out = _softmax_pv(ready, [v_loc, vc_scr[sl, :]], [False, True])
        o_ref[:, sl] = jnp.where(lo, out[:nq], out[nq:]).astype(BF16)
        ready = following


def _na_bias_kernel(rpb_ref, sel_ref, neg_ref, o_ref):
    o_ref[...] = _dot_exact_rhs(rpb_ref[...], sel_ref[...]) + neg_ref[...]


def _na_bias_tables(rpb):
    col = np.arange(GRID_W)
    col_start = np.clip(col - NA_WIN_C // 2, 0, GRID_W - NA_WIN_C)
    col_ok = (col[None, :] >= col_start[:, None]) & (col[None, :] < col_start[:, None] + NA_WIN_C)
    dc = col[None, :] - col[:, None] + NA_WIN_C - 1
    kpad = 32
    sel = (dc[None, :, :] == np.arange(kpad)[:, None, None]) & col_ok[None]
    sel = jnp.asarray(sel.reshape(kpad, GRID_W * GRID_W), BF16)
    neg = jnp.asarray(np.where(col_ok, 0.0, NEG).reshape(1, GRID_W * GRID_W), F32)
    rpb2 = jnp.pad(rpb.reshape(NA_HEADS * NA_DR, NA_DC), ((0, 0), (0, kpad - NA_DC)))
    n_rows = NA_HEADS * NA_DR
    full = lambda shape: pl.BlockSpec(shape, lambda i: (0, 0))
    t = pl.pallas_call(
        _na_bias_kernel,
        grid=(1,),
        in_specs=[full((n_rows, kpad)), full((kpad, GRID_W * GRID_W)), full((1, GRID_W * GRID_W))],
        out_specs=full((n_rows, GRID_W * GRID_W)),
        out_shape=jax.ShapeDtypeStruct((n_rows, GRID_W * GRID_W), F32),
        compiler_params=_params("arbitrary"),
        name="na_bias",
    )(rpb2, sel, neg)
    t = t.reshape(NA_HEADS, NA_DR, GRID_W, GRID_W)
    masked = jnp.full((NA_HEADS, 1, GRID_W, GRID_W), NEG, F32)
    first, last = NA_WIN_R // 2 - 1, NA_WIN_R // 2 + NA_WIN_R - 2
    return jnp.concatenate([
        jnp.concatenate([t[:, :-1], t[:, 1:]], axis=-1),
        jnp.concatenate([masked, t[:, first:first + 1]], axis=-1),
        jnp.concatenate([t[:, last:last + 1], masked], axis=-1),
        jnp.concatenate([masked, masked], axis=-1)], axis=1)


def _na_latent(ymix, cache_k, cache_v, tz, layer):
    groups = NA_ROWS // NA_QROWS
    nq = NA_QROWS * GRID_W
    return pl.pallas_call(
        _na_kernel,
        grid=(DEC_BATCH, groups),
        in_specs=[
            pl.BlockSpec((nq, 512), lambda b, g: (b * groups + g, COL_NQ)),
            pl.BlockSpec((DEC_SEQ, 512), lambda b, g: (b, COL_NK)),
            pl.BlockSpec((DEC_SEQ, 512), lambda b, g: (b, COL_NV)),
            pl.BlockSpec((NA_HEADS * NA_HD, PAST_LEN), lambda b, g: (b * DEPTH + layer, 0)),
            pl.BlockSpec((NA_HEADS * NA_HD, PAST_LEN), lambda b, g: (b * DEPTH + layer, 0)),
            pl.BlockSpec((NA_HEADS, NA_T_SIZE, GRID_W, LANE), lambda b, g: (0, 0, 0, 0)),
        ],
        out_specs=pl.BlockSpec((nq, 512), lambda b, g: (b * groups + g, 0)),
        out_shape=jax.ShapeDtypeStruct((M_ROWS, 512), BF16),
        scratch_shapes=[pltpu.VMEM((NA_HEADS * NA_HD, PAST_LEN), BF16)] * 2,
        compiler_params=_params("parallel", "arbitrary"),
        name="na_lat",
    )(ymix, ymix, ymix, cache_k, cache_v, tz)


def _na_cache_out_kernel(*refs):
    srcs, (ko_ref, vo_ref) = refs[:2 * DEPTH], refs[2 * DEPTH:]
    layer = pl.program_id(1)
    for l in range(DEPTH):
        @pl.when(layer == l)
        def _():
            ko_ref[...] = srcs[2 * l][...].T
            vo_ref[...] = srcs[2 * l + 1][...].T


def _na_cache_out(ymix_layers):
    rows = NA_HEADS * NA_HD
    in_specs, args = [], []
    for ymix in ymix_layers:
        for col in (COL_NK, COL_NV):
            in_specs.append(pl.BlockSpec((SEQ, 512), functools.partial(lambda b, l, col: (b, col), col=col)))
            args.append(ymix)
    out = pl.pallas_call(
        _na_cache_out_kernel,
        grid=(BATCH, DEPTH),
        in_specs=in_specs,
        out_specs=[pl.BlockSpec((rows, SEQ), lambda b, l: (b * DEPTH + l, 0))] * 2,
        out_shape=[jax.ShapeDtypeStruct((BATCH * DEPTH * rows, SEQ), F32)] * 2,
        compiler_params=_params("parallel", "arbitrary"),
        name="na_cache_out",
    )(*args)
    return tuple(o.reshape(BATCH, DEPTH, NA_HEADS, NA_HD, SEQ).transpose(0, 1, 4, 2, 3) for o in out)


HG_BLOCK = 64
HG_STATE_UNROLL = 4
HG_DEC_ROWS = 8


def _hgrn_kernel(*refs, seq, has_state):
    if has_state:
        (q_ref, ff_ref, fb_ref, v_ref, g_ref, lb_ref, gain_ref, s0_ref,
         o_ref, of_ref, ob_ref, qin_ref, kend_ref, dec_ref, st_ref) = refs
    else:
        (q_ref, ff_ref, fb_ref, v_ref, g_ref, lb_ref, gain_ref,
         o_ref, sout_ref, of_ref, ob_ref, qin_ref, kend_ref, dec_ref, st_ref) = refs
    c = HG_CHUNK
    hc = c // 2
    rb = HG_BLOCK
    per_block = rb // c
    n_chunks = seq // c
    width = HG_HEADS * HG_DK
    sub = HG_DEC_ROWS

    ri = lax.broadcasted_iota(jnp.int32, (rb, rb), 0)
    ci = lax.broadcasted_iota(jnp.int32, (rb, rb), 1)
    same = (ri // c) == (ci // c)
    tri_f = jnp.where(jnp.logical_and(same, ci <= ri), 1.0, 0.0).astype(BF16)
    tri_b = jnp.where(jnp.logical_and(same, ci >= ri), 1.0, 0.0).astype(BF16)
    ones = jnp.ones((HG_DK, HG_DV), BF16)
    rowid = lax.broadcasted_iota(jnp.int32, (c, HG_DV), 0)
    laneid = lax.broadcasted_iota(jnp.int32, (c, HG_DV), 1)
    laneid_half = lax.broadcasted_iota(jnp.int32, (hc, HG_DV), 1)
    o_refs = (of_ref, ob_ref)

    n_blocks = seq // rb

    def block_of(d, step):
        return step if d == 0 else n_blocks - 1 - step

    heads = [slice(h * HG_DK, (h + 1) * HG_DK) for h in range(HG_HEADS)]
    per_chunk = hc * c + hc * hc
    dirs = (0, 1)

    def pair_block(step):
        fwd = [True, False]
        pre_refs = (ff_ref, fb_ref)
        tris = (tri_f, tri_b)
        r0 = [_aligned(block_of(d, step) * rb, rb) for d in dirs]
        wide = [range(0, hc), range(hc, c)]
        narrow = [range(hc, c), range(0, hc)]
        far = [slice(hc, c), slice(0, hc)]
        q_all = [q_ref[pl.ds(r0[d], rb), :] * (HG_DK ** -0.5) for d in dirs]
        v16_all = [v_ref[pl.ds(r0[d], rb), :].astype(BF16) for d in dirs]
        k_all, log_f = [], []
        for d in dirs:
            lb = lb_ref[d:d + 1, :]
            f = lb + (1.0 - lb) * jax.nn.sigmoid(pre_refs[d][pl.ds(r0[d], rb), :])
            k_all.append(1.0 - f)
            log_f.append(jnp.log(f))
        b_all = [_dot_exact_lhs(tris[d], log_f[d]) * LOG2E for d in dirs]
        src_all = [b_all[d] - jnp.log(k_all[d]) * LOG2E for d in dirs]
        pairs = []
        for d in dirs:
            blocks, q_in, k_end = [], [], []
            for m in range(per_block):
                rows = slice(m * c, (m + 1) * c)
                q, k, b, src = q_all[d][rows], k_all[d][rows], b_all[d][rows], src_all[d][rows]
                b_last = b[c - 1:c] if fwd[d] else b[0:1]
                q_in.append(q * jnp.exp2(b))
                k_end.append(k * jnp.exp2(b_last - b))
                dec_row = _aligned((block_of(d, step) * per_block + m) * sub, sub)
                dec_ref[d, pl.ds(dec_row, sub), :] = jnp.broadcast_to(jnp.exp2(b_last), (sub, width))
                for s in wide[d]:
                    blocks.append((q * jnp.exp2(b - src[s:s + 1])).astype(BF16))
                q_far, b_far = q[far[d]], b[far[d]]
                halves = [q_far * jnp.exp2(b_far - src[s:s + 1]) for s in narrow[d]]
                for i in range(0, hc, 2):
                    blocks.append(jnp.concatenate(halves[i:i + 2], axis=0).astype(BF16))
            qin_ref[d, pl.ds(r0[d], rb), :] = jnp.concatenate(q_in, axis=0).astype(BF16)
            kend_ref[d, pl.ds(r0[d], rb), :] = jnp.concatenate(k_end, axis=0).astype(BF16)
            pairs.append(jnp.concatenate(blocks, axis=0))
        sums = [[_dot(pairs[d][:, sl], ones) for sl in heads] for d in dirs]
        cells = [(d, h, m) for d in dirs for h in range(HG_HEADS) for m in range(per_block)]
        attn = {cell: jnp.zeros((c, HG_DV), F32) for cell in cells}
        far_attn = {cell: jnp.zeros((hc, HG_DV), F32) for cell in cells}
        for n in range(hc):
            for d, h, m in cells:
                base = m * per_chunk
                first = base + hc * c + n * hc
                attn[d, h, m] = jnp.where(laneid == wide[d][n], sums[d][h][base + n * c:base + (n + 1) * c],
                                          attn[d, h, m])
                far_attn[d, h, m] = jnp.where(laneid_half == narrow[d][n], sums[d][h][first:first + hc],
                                              far_attn[d, h, m])
        zero = jnp.zeros((hc, HG_DV), F32)
        v_pad = jnp.zeros((HG_DK - c, HG_DV), BF16)
        causal = [laneid <= rowid, laneid >= rowid]
        prods = {}
        for d, h, m in cells:
            halves = [zero, far_attn[d, h, m]] if fwd[d] else [far_attn[d, h, m], zero]
            full = jnp.where(causal[d], attn[d, h, m] + jnp.concatenate(halves, axis=0), 0.0).astype(BF16)
            values = jnp.concatenate([v16_all[d][m * c:(m + 1) * c, heads[h]], v_pad], axis=0)
            prods[d, h, m] = _dot(full, values)
        for d in dirs:
            o_refs[d][pl.ds(r0[d], rb), :] = jnp.concatenate(
                [jnp.concatenate([prods[d, h, m] for m in range(per_block)], axis=0) for h in range(HG_HEADS)], axis=1)


    def state_block(step, carry):
        order = [list(range(per_block)), list(reversed(range(per_block)))]
        r0 = [_aligned(block_of(d, step) * rb, rb) for d in dirs]
        q_in = [qin_ref[d, pl.ds(r0[d], rb), :] for d in dirs]
        k_end = [kend_ref[d, pl.ds(r0[d], rb), :] for d in dirs]
        v16 = [v_ref[pl.ds(r0[d], rb), :].astype(BF16) for d in dirs]
        rows = [slice(m * c, (m + 1) * c) for m in range(per_block)]
        q_state, k_state, k_cross, v_cross, total = [], [], [], [], []
        for d in dirs:
            dec_row = _aligned(block_of(d, step) * per_block * sub, sub)
            dec_all = dec_ref[d, pl.ds(dec_row, per_block * sub), :]
            dec = [dec_all[m * sub:m * sub + 1] for m in order[d]]
            before = [None] * per_block
            after = [None] * per_block
            for j in range(1, per_block):
                before[j] = dec[j - 1] if before[j - 1] is None else before[j - 1] * dec[j - 1]
            for j in range(per_block - 2, -1, -1):
                after[j] = dec[j + 1] if after[j + 1] is None else after[j + 1] * dec[j + 1]
            total.append(before[-1] * dec[-1])
            scale = lambda x, f: x if f is None else x * f
            qs, ks = [None] * per_block, [None] * per_block
            for j, m in enumerate(order[d]):
                qs[m] = scale(q_in[d][rows[m]], before[j])
                ks[m] = scale(k_end[d][rows[m]], after[j])
            q_state.append(jnp.concatenate(qs, axis=0).astype(BF16))
            k_state.append(jnp.concatenate(ks, axis=0).astype(BF16))
            kc, vc = [None], [None]
            for j in range(1, per_block):
                keys, between = [], None
                for i in range(j - 1, -1, -1):
                    keys.insert(0, scale(k_end[d][rows[order[d][i]]], between))
                    between = dec[i] if between is None else between * dec[i]
                kc.append(jnp.concatenate(keys, axis=0).astype(BF16))
                vc.append(jnp.concatenate([v16[d][rows[order[d][i]]] for i in range(j)], axis=0))
            k_cross.append(kc)
            v_cross.append(vc)
        q16 = [q_in[d].astype(BF16) for d in dirs]
        cells = [(d, h) for d in dirs for h in range(HG_HEADS)]
        kv = {(d, h): _dot_tn(v16[d][:, heads[h]], k_state[d][:, heads[h]]) for d, h in cells}
        attn = {(d, h, j): _dot_nt(q16[d][rows[order[d][j]], heads[h]], k_cross[d][j][:, heads[h]])
                for d, h in cells for j in range(1, per_block)}
        st = {(d, h): st_ref[d, h] for d, h in cells}
        from_state = {(d, h): _dot_nt(q_state[d][:, heads[h]], st[d, h].astype(BF16)) for d, h in cells}
        cross = {key: _dot(a.astype(BF16), v_cross[key[0]][key[2]][:, heads[key[1]]]) for key, a in attn.items()}
        for d, h in cells:
            st_ref[d, h] = st[d, h] * total[d][:, heads[h]] + kv[d, h]
        for d in dirs:
            cols = []
            for h in range(HG_HEADS):
                parts = [None] * per_block
                for j, m in enumerate(order[d]):
                    part = from_state[d, h][rows[m]]
                    parts[m] = part if j == 0 else part + cross[d, h, j]
                cols.append(jnp.concatenate(parts, axis=0))
            o_refs[d][pl.ds(r0[d], rb), :] += jnp.concatenate(cols, axis=1)
        return carry

    if has_state:
        for d in range(2):
            for h in range(HG_HEADS):
                st_ref[d, h] = s0_ref[d, h].T
    else:
        st_ref[...] = jnp.zeros(st_ref.shape, F32)

    def pair_step(i, carry):
        pair_block(i)
        return carry

    lax.fori_loop(0, n_blocks, pair_step, 0)
    lax.fori_loop(0, n_blocks, state_block, 0, unroll=HG_STATE_UNROLL)

    for h in range(HG_HEADS):
        sl = slice(h * HG_DV, (h + 1) * HG_DV)
        o = of_ref[:, sl] + ob_ref[:, sl]
        o_ref[:, sl] = (_rms(o) * gain_ref[:, sl] * _silu(g_ref[:, sl])).astype(BF16)
    if not has_state:
        for d in range(2):
            for h in range(HG_HEADS):
                sout_ref[d, h] = st_ref[d, h].T


def _hgrn(ymix, lb, gain, batch, seq, state, layer):
    width = HG_HEADS * HG_DK
    has_state = state is not None
    assert HG_DK == HG_DV
    st_shape = (2, HG_HEADS, HG_DK, HG_DV)
    in_specs = [pl.BlockSpec((seq, width), functools.partial(lambda b, col: (b, col), col=col))
                for col in (COL_HQ, COL_HFF, COL_HFB, COL_HI, COL_HG)]
    in_specs += [pl.BlockSpec((2, width), lambda b: (0, 0)), pl.BlockSpec((1, width), lambda b: (0, 0))]
    args = [ymix] * 5 + [lb, gain]
    out_specs = [pl.BlockSpec((seq, width), lambda b: (b, 0))]
    out_shape = [jax.ShapeDtypeStruct((M_ROWS, width), BF16)]
    if has_state:
        in_specs.append(pl.BlockSpec((None, None) + st_shape, lambda b: (b, layer, 0, 0, 0, 0)))
        args.append(state)
    else:
        out_specs.append(pl.BlockSpec((None,) + st_shape, lambda b: (b, 0, 0, 0, 0)))
        out_shape.append(jax.ShapeDtypeStruct((batch,) + st_shape, F32))
    return pl.pallas_call(
        functools.partial(_hgrn_kernel, seq=seq, has_state=has_state),
        grid=(batch,),
        in_specs=in_specs,
        out_specs=out_specs,
        out_shape=out_shape,
        scratch_shapes=[pltpu.VMEM((seq, width), F32), pltpu.VMEM((seq, width), F32),
                        pltpu.VMEM((2, seq, width), BF16), pltpu.VMEM((2, seq, width), BF16),
                        pltpu.VMEM((2, seq // HG_CHUNK * HG_DEC_ROWS, width), F32),
                        pltpu.VMEM(st_shape, F32)],
        compiler_params=_params("parallel"),
        name="hgrn_lat" if has_state else "hgrn_ctx",
    )(*args)


def _merge_kernel(oa_ref, ob_ref, oc_ref, od_ref, gt_ref, wb_ref, wo_ref, x_ref, g1_ref, out_ref):
    acc = None
    for n, o_ref in enumerate((oa_ref, ob_ref, oc_ref, od_ref)):
        bo = _dot(o_ref[...], wb_ref[n])
        term = gt_ref[:, n * D_MODEL:(n + 1) * D_MODEL] * bo
        acc = term if acc is None else acc + term
    out_ref[...] = x_ref[...] + g1_ref[...] * _dot(acc.astype(BF16), wo_ref[...])


def _merge(branches, gates, w_branch, w_out, x, mod, latent):
    tm = 512
    row = _mod_row(latent, tm)
    tile = lambda w: pl.BlockSpec((tm, w), lambda i: (i, 0))
    return pl.pallas_call(
        _merge_kernel,
        grid=(M_ROWS // tm,),
        in_specs=[tile(BRANCH_W)] * N_BRANCH + [
            tile(GATE_W),
            pl.BlockSpec((N_BRANCH, BRANCH_W, D_MODEL), lambda i: (0, 0, 0)),
            pl.BlockSpec((D_MODEL, D_MODEL), lambda i: (0, 0)),
            tile(D_MODEL),
            pl.BlockSpec((None, None, 1, D_MODEL), lambda i: (row(i), 2, 0, 0)),
        ],
        out_specs=tile(D_MODEL),
        out_shape=jax.ShapeDtypeStruct((M_ROWS, D_MODEL), F32),
        compiler_params=_params("parallel"),
        name="merge_lat" if latent else "merge_ctx",
    )(*branches, gates, w_branch, w_out, x, mod)


FFN_CHUNK = 256
FFN_STEPS = FFN_HIDDEN // FFN_CHUNK
assert FFN_STEPS * FFN_CHUNK == FFN_HIDDEN


def _ffn_kernel(*refs, final):
    x_ref, sc_ref, sh_ref, g2_ref, wa_ref, wg_ref, wo_ref = refs[:7]
    if final:
        fg_ref, out_ref, h_ref, acc_ref = refs[7:]
    else:
        out_ref, h_ref, acc_ref = refs[7:]
    j = pl.program_id(1)

    @pl.when(j == 0)
    def _():
        h = _rms(x_ref[...]) * (1.0 + sc_ref[...]) + sh_ref[...]
        h_ref[...] = h.astype(BF16)
        acc_ref[...] = jnp.zeros(acc_ref.shape, F32)

    h = h_ref[...]
    act = _silu(_dot(h, wg_ref[...].astype(BF16))) * _dot(h, wa_ref[...].astype(BF16))
    acc_ref[...] += _dot(act.astype(BF16), wo_ref[...].astype(BF16))

    @pl.when(j == FFN_STEPS - 1)
    def _():
        y = x_ref[...] + g2_ref[...] * acc_ref[...]
        out_ref[...] = _rms(y) * fg_ref[...] if final else y


def _ffn(x, mod, w_in, w_out, layer, latent, final_gain):
    tm = 1024
    row = _mod_row(latent, tm)
    modspec = lambda which: pl.BlockSpec((None, None, 1, D_MODEL), lambda i, j: (row(i), which, 0, 0))
    final = final_gain is not None
    in_specs = [
        pl.BlockSpec((tm, D_MODEL), lambda i, j: (i, 0)),
        modspec(4), modspec(3), modspec(5),
        pl.BlockSpec((None, D_MODEL, FFN_CHUNK), lambda i, j: (layer, 0, j)),
        pl.BlockSpec((None, D_MODEL, FFN_CHUNK), lambda i, j: (layer, 0, FFN_STEPS + j)),
        pl.BlockSpec((None, FFN_CHUNK, D_MODEL), lambda i, j: (layer, j, 0)),
    ]
    args = [x, mod, mod, mod, w_in, w_in, w_out]
    if final:
        in_specs.append(pl.BlockSpec((1, D_MODEL), lambda i, j: (0, 0)))
        args.append(final_gain.reshape(1, D_MODEL))
    return pl.pallas_call(
        functools.partial(_ffn_kernel, final=final),
        grid=(M_ROWS // tm, FFN_STEPS),
        in_specs=in_specs,
        out_specs=pl.BlockSpec((tm, D_MODEL), lambda i, j: (i, 0)),
        out_shape=jax.ShapeDtypeStruct((M_ROWS, D_MODEL), F32),
        scratch_shapes=[pltpu.VMEM((tm, D_MODEL), BF16), pltpu.VMEM((tm, D_MODEL), F32)],
        compiler_params=_params("parallel", "arbitrary"),
        name="ffn_lat" if latent else "ffn_ctx",
    )(*args)


def _rope_tables():
    t = jnp.arange(DEC_SEQ)
    row = (t // GRID_W).astype(F32)[:, None]
    col = (t % GRID_W).astype(F32)[:, None]

    def angles(rot_dim):
        n_freq = rot_dim // 4
        inv_freq = ROPE_THETA ** (-jnp.arange(n_freq, dtype=F32) / n_freq)
        return jnp.concatenate([row * inv_freq, col * inv_freq], axis=-1)

    def expand(ang):
        cos = jnp.repeat(jnp.cos(ang), 2, axis=-1)
        sin = jnp.repeat(jnp.sin(ang), 2, axis=-1)
        even = (jnp.arange(cos.shape[-1]) % 2 == 0)[None, :]
        return cos, jnp.where(even, -sin, 0.0), jnp.where(even, 0.0, sin)

    gq = [jnp.concatenate([a, a], axis=-1) for a in expand(angles(GQ_HD))]
    ml = []
    for idx, a in enumerate(expand(angles(ML_ROPE))):
        fill = 1.0 if idx == 0 else 0.0
        ml.append(jnp.concatenate([a, jnp.full((DEC_SEQ, LANE - ML_ROPE), fill, F32)], axis=-1))
    return {"gq": jnp.stack(gq), "ml": jnp.stack(ml)}


def _layer_weights(l, gq_q_gain, gq_k_gain, ml_q_a_gain, ml_kv_a_gain, ml_w_q_b, ml_w_kv_b,
                   w_branch, w_out, hg_gain, avg):
    pad = LANE - ML_NOPE - ML_ROPE
    qb = ml_w_q_b[l].reshape(ML_Q_RANK, ML_HEADS, ML_NOPE + ML_ROPE)
    qb = jnp.concatenate([qb[:, :, ML_NOPE:], qb[:, :, :ML_NOPE], jnp.zeros((ML_Q_RANK, ML_HEADS, pad), F32)],
                         axis=-1).reshape(ML_Q_RANK, ML_HEADS * LANE)
    kvb = ml_w_kv_b[l].reshape(ML_KV_RANK, ML_HEADS, ML_NOPE + ML_V)
    wk = jnp.pad(kvb[:, :, :ML_NOPE], ((0, 0), (0, 0), (ML_ROPE, pad))).reshape(ML_KV_RANK, ML_HEADS * LANE)
    wv = kvb[:, :, ML_NOPE:].reshape(ML_KV_RANK, ML_HEADS * ML_V)
    return {
        "gq_q_gain": jnp.tile(gq_q_gain[l], GQ_HEADS).reshape(1, -1),
        "gq_k_gain": jnp.tile(gq_k_gain[l], GQ_KV_HEADS).reshape(1, -1),
        "ml_q_gain": ml_q_a_gain[l].reshape(1, -1), "ml_kv_gain": ml_kv_a_gain[l].reshape(1, -1),
        "w_qb": qb.astype(BF16), "w_k": wk.astype(BF16), "w_v": wv.astype(BF16),
        "w_branch": w_branch[l].astype(BF16), "w_out": w_out[l].astype(BF16),
        "hg_gain": jnp.tile(hg_gain[l], HG_HEADS).reshape(1, -1), "avg": avg,
    }


def _seg(arr, width, col, rows, stride, off):
    return (arr, width, col, rows, stride, off)


def kernel(x_prompt, x_sample, state_hgrn, cache_gqa_k, cache_gqa_v, cache_na_k, cache_na_v, cache_mla_ckv, cache_mla_krope, c, c_ctx, w_ada, b_ada, w_in, hg_lb_logits, hg_gain, gq_q_gain, gq_k_gain, na_rpb, ml_q_a_gain, ml_kv_a_gain, ml_w_q_b, ml_w_kv_b, w_branch, w_out, w_ffn_in, w_ffn_out, final_gain):
    cond8 = jnp.concatenate([c_ctx[None, :], c, jnp.zeros((8 - 1 - DEC_BATCH, D_MODEL), F32)], axis=0)
    mods = _ada(cond8, w_ada, b_ada)

    lb = jnp.cumsum(jax.nn.softmax(hg_lb_logits.astype(F32), axis=0), axis=0)
    lb = lb - lb[:1]
    avg = jnp.asarray(np.kron(np.eye(512 // GQ_HD), np.full((GQ_HD, GQ_HD), 1.0 / GQ_HD)), BF16)
    tabs = _rope_tables()
    w_in_t = jnp.swapaxes(w_in, 1, 2).reshape(DEPTH * w_in.shape[2], D_MODEL)

    gqk_c = cache_gqa_k.reshape(DEC_BATCH * DEPTH * PAST_LEN, GQ_KV_HEADS * GQ_HD)
    gqv_c = cache_gqa_v.reshape(DEC_BATCH * DEPTH * PAST_LEN, GQ_KV_HEADS * GQ_HD)
    nak_c = cache_na_k.transpose(0, 1, 3, 4, 2).reshape(DEC_BATCH * DEPTH * NA_HEADS * NA_HD, PAST_LEN)
    nav_c = cache_na_v.transpose(0, 1, 3, 4, 2).reshape(DEC_BATCH * DEPTH * NA_HEADS * NA_HD, PAST_LEN)
    mckv_c = cache_mla_ckv.reshape(DEC_BATCH * DEPTH * PAST_LEN, ML_KV_RANK)
    mkr_c = jnp.pad(cache_mla_krope.reshape(DEC_BATCH * DEPTH * PAST_LEN, ML_ROPE),
                    ((0, 0), (0, LANE - ML_ROPE)))

    xp = x_prompt.reshape(M_ROWS, D_MODEL)
    xs = x_sample.reshape(M_ROWS, D_MODEL)
    new = []
    ymix_ctx = []
    for l in range(DEPTH):
        lw = _layer_weights(l, gq_q_gain, gq_k_gain, ml_q_a_gain, ml_kv_a_gain, ml_w_q_b,
                            ml_w_kv_b, w_branch, w_out, hg_gain, avg)
        last = l == DEPTH - 1
        mod = mods[l]

        ymix = _inproj(xp, mod, w_in_t, l, False, gate=False)
        gates = _inproj(xp, mod, w_in_t, l, False, gate=True)
        qb, kb, qd, ckv, kd, vd = _prep(ymix, lw, tabs, False)
        out_a, st = _hgrn(ymix, lb[l], lw["hg_gain"], BATCH, SEQ, None, l)
        out_b = _attention((qb, 512, 0, 0),
                           [(_seg(kb, LANE, 0, SEQ, 1, 0), _seg(ymix, LANE, COL_GV, SEQ, 1, 0))],
                           _plan_gqa(), GQ_SCALE, BATCH, SEQ, "gqa_ctx")
        out_c = _attention((ymix, 512, COL_NQ, 0),
                           [(_seg(ymix, 512, COL_NK, SEQ, 1, 0), _seg(ymix, 512, COL_NV, SEQ, 1, 0))],
                           _plan_na(), NA_SCALE, BATCH, SEQ, "na_ctx")
        out_d = _attention((qd, ML_HEADS * LANE, 0, 0),
                           [(_seg(kd, ML_HEADS * LANE, 0, SEQ, 1, 0), _seg(vd, 512, 0, SEQ, 1, 0))],
                           _plan_mla(), ML_SCALE, BATCH, SEQ, "mla_ctx")
        xp = _merge((out_a, out_b, out_c, out_d), gates, lw["w_branch"], lw["w_out"], xp, mod, False)
        xp = _ffn(xp, mod, w_ffn_in, w_ffn_out, l, False, final_gain if last else None)
        new.append((
            st,
            kb.reshape(BATCH, SEQ, GQ_KV_HEADS, GQ_HD),
            ymix[:, COL_GV * LANE:(COL_GV + 1) * LANE].reshape(BATCH, SEQ, GQ_KV_HEADS, GQ_HD),
            ckv.reshape(BATCH, SEQ, ML_KV_RANK),
            ymix[:, COL_MKR * LANE:COL_MKR * LANE + ML_ROPE].reshape(BATCH, SEQ, ML_ROPE),
        ))
        ymix_ctx.append(ymix)

        ymix = _inproj(xs, mod, w_in_t, l, True, gate=False)
        gates = _inproj(xs, mod, w_in_t, l, True, gate=True)
        qb, kb, qd, ckv, kd, vd = _prep(ymix, lw, tabs, True)
        kd_c, vd_c = _mla_cache(mckv_c, mkr_c, lw, l)
        out_a, = _hgrn(ymix, lb[l], lw["hg_gain"], DEC_BATCH, DEC_SEQ, state_hgrn, l)
        out_b = _attention((qb, 512, 0, 0),
                           [(_seg(gqk_c, LANE, 0, PAST_LEN, DEPTH, l), _seg(gqv_c, LANE, 0, PAST_LEN, DEPTH, l)),
                            (_seg(kb, LANE, 0, DEC_SEQ, 1, 0), _seg(ymix, LANE, COL_GV, DEC_SEQ, 1, 0))],
                           _plan_gqa(), GQ_SCALE, DEC_BATCH, DEC_SEQ, "gqa_lat")
        out_c = _na_latent(ymix, nak_c, nav_c, _na_bias_tables(na_rpb[l]), l)
        out_d = _attention((qd, ML_HEADS * LANE, 0, 0),
                           [(_seg(kd_c, ML_HEADS * LANE, 0, PAST_LEN, 1, 0), _seg(vd_c, 512, 0, PAST_LEN, 1, 0)),
                            (_seg(kd, ML_HEADS * LANE, 0, DEC_SEQ, 1, 0), _seg(vd, 512, 0, DEC_SEQ, 1, 0))],
                           _plan_mla(), ML_SCALE, DEC_BATCH, DEC_SEQ, "mla_lat", tq=512)
        xs = _merge((out_a, out_b, out_c, out_d), gates, lw["w_branch"], lw["w_out"], xs, mod, True)
        xs = _ffn(xs, mod, w_ffn_in, w_ffn_out, l, True, final_gain if last else None)

    y_prompt = xp.reshape(BATCH, SEQ, D_MODEL)
    y_sample = xs.reshape(DEC_BATCH, DEC_SEQ, D_MODEL)
    state, gqa_k, gqa_v, ckv_new, krope_new = (jnp.stack([n[i] for n in new], axis=1) for i in range(5))
    na_k, na_v = _na_cache_out(ymix_ctx)
    return (y_prompt, y_sample, state, gqa_k, gqa_v, na_k, na_v, ckv_new, krope_new)
```

```python
import functools

import numpy as np
import jax
import jax.numpy as jnp
from jax import lax
from jax.experimental import pallas as pl
from jax.experimental.pallas import tpu as pltpu

F32 = jnp.float32
BF16 = jnp.bfloat16

D_MODEL = 1024
BATCH = 16
SEQ = 256
DEPTH = 2
DEC_BATCH = 4
DEC_SEQ = 1024
PAST_LEN = 512
GRID_W = 64
EPS = 1e-6
ROPE_THETA = 10000.0
N_BRANCH = 4
BRANCH_W = 512
HG_HEADS = 4
HG_DK = 128
HG_DV = 128
GQ_HEADS = 8
GQ_KV_HEADS = 2
GQ_HD = 64
NA_HEADS = 8
NA_HD = 64
NA_WIN_R = 8
NA_WIN_C = 16
ML_HEADS = 8
ML_NOPE = 64
ML_ROPE = 32
ML_V = 64
ML_Q_RANK = 256
ML_KV_RANK = 128
FFN_HIDDEN = 2816
GQ_SCALE = GQ_HD ** -0.5
NA_SCALE = NA_HD ** -0.5
ML_SCALE = (ML_NOPE + ML_ROPE) ** -0.5

M_ROWS = BATCH * SEQ
assert M_ROWS == DEC_BATCH * DEC_SEQ

LANE = 128
HALF = 64
MIX_W = 5376
IN_MIX = 5280
GATE_W = N_BRANCH * D_MODEL
HG_CHUNK = 16
NEG = -1e30
LOG2E = 1.4426950408889634
VMEM_LIMIT = 56 * 1024 * 1024

COL_HQ, COL_HFF, COL_HFB, COL_HI, COL_HG, COL_GQ, COL_NQ, COL_NK, COL_NV = range(9)
COL_MQA = 18
COL_GK, COL_GV, COL_MKVA, COL_MKR = 38, 39, 40, 41


def _dot(a, b):
    return jnp.dot(a, b, preferred_element_type=F32)


def _dot_nt(a, b):
    return lax.dot_general(a, b, (((1,), (1,)), ((), ())), preferred_element_type=F32)


def _dot_tn(a, b):
    return lax.dot_general(a, b, (((0,), (0,)), ((), ())), preferred_element_type=F32)


def _split3(x):
    x1 = x.astype(BF16)
    r1 = x - x1.astype(F32)
    x2 = r1.astype(BF16)
    x3 = (r1 - x2.astype(F32)).astype(BF16)
    return x1, x2, x3


def _dot_exact_lhs(a_bf16, x):
    x1, x2, x3 = _split3(x)
    return (_dot(a_bf16, x3) + _dot(a_bf16, x2)) + _dot(a_bf16, x1)


def _dot_exact_rhs(x, b_bf16):
    x1, x2, x3 = _split3(x)
    return (_dot(x3, b_bf16) + _dot(x2, b_bf16)) + _dot(x1, b_bf16)


def _rms(x):
    return x * lax.rsqrt(jnp.mean(x * x, axis=-1, keepdims=True) + EPS)


def _silu(x):
    return x * jax.nn.sigmoid(x)


def _aligned(x, m):
    return x if isinstance(x, int) else pl.multiple_of(x, m)


def _params(*sem):
    return pltpu.CompilerParams(dimension_semantics=sem, vmem_limit_bytes=VMEM_LIMIT)


def _mod_row(latent, tm):
    if latent:
        return lambda i: 1 + (i * tm) // DEC_SEQ
    return lambda i: 0


def _ada_kernel(c_ref, w_ref, b_ref, o_ref):
    c = c_ref[...]
    o_ref[...] = _dot(_silu(c).astype(BF16), w_ref[...].astype(BF16)) + b_ref[...]


def _ada(cond8, w_ada, b_ada):
    tn = 1536
    out = pl.pallas_call(
        _ada_kernel,
        grid=(DEPTH, 6 * D_MODEL // tn),
        in_specs=[
            pl.BlockSpec((8, D_MODEL), lambda l, j: (0, 0)),
            pl.BlockSpec((None, D_MODEL, tn), lambda l, j: (l, 0, j)),
            pl.BlockSpec((None, 1, tn), lambda l, j: (l, 0, j)),
        ],
        out_specs=pl.BlockSpec((None, 8, tn), lambda l, j: (l, 0, j)),
        out_shape=jax.ShapeDtypeStruct((DEPTH, 8, 6 * D_MODEL), F32),
        compiler_params=_params("parallel", "parallel"),
        name="ada",
    )(cond8, w_ada, b_ada.reshape(DEPTH, 1, 6 * D_MODEL))
    return out.reshape(DEPTH, 8, 6, 1, D_MODEL)


IN_PIECE = 256
IN_TM = 1024


def _mix_source(t):
    blk = jnp.where(t < 12, t, jnp.where(t < 18, t + 1, jnp.where(t == 18, 19, jnp.where(t == 19, 12, 20))))
    return blk * IN_PIECE


def _gate_source(t):
    return IN_MIX + t * IN_PIECE


def _inproj_kernel(*refs, pieces, gate):
    x_ref, sc_ref, sh_ref = refs[:3]
    w_refs = refs[3:3 + pieces]
    o_ref, h_ref = refs[3 + pieces:]
    rows = pl.ds(pl.multiple_of(pl.program_id(1) * IN_TM, IN_TM), IN_TM)

    @pl.when(pl.program_id(0) == 0)
    def _():
        h = _rms(x_ref[...]) * (1.0 + sc_ref[...]) + sh_ref[...]
        h_ref[rows, :] = h.astype(BF16)

    h = h_ref[rows, :]
    for p, w_ref in enumerate(w_refs):
        y = _dot_nt(h, w_ref[...].astype(BF16))
        o_ref[:, p * IN_PIECE:(p + 1) * IN_PIECE] = jax.nn.sigmoid(y).astype(o_ref.dtype) if gate else y


def _inproj(x, mod, w_in_t, layer, latent, gate):
    width, tile, source = (GATE_W, 2048, _gate_source) if gate else (MIX_W, 1792, _mix_source)
    pieces = tile // IN_PIECE
    row = _mod_row(latent, IN_TM)
    first = lambda c, i: jnp.where(c == 0, i, 0)
    in_width = w_in_t.shape[0] // DEPTH
    piece = lambda p: pl.BlockSpec((pl.Element(IN_PIECE), pl.Element(D_MODEL)),
                                   lambda c, i: (pl.multiple_of(layer * in_width + source(c * pieces + p), 8), 0))
    return pl.pallas_call(
        functools.partial(_inproj_kernel, pieces=pieces, gate=gate),
        grid=(width // tile, M_ROWS // IN_TM),
        in_specs=[
            pl.BlockSpec((IN_TM, D_MODEL), lambda c, i: (first(c, i), 0)),
            pl.BlockSpec((None, None, 1, D_MODEL), lambda c, i: (row(first(c, i)), 1, 0, 0)),
            pl.BlockSpec((None, None, 1, D_MODEL), lambda c, i: (row(first(c, i)), 0, 0, 0)),
        ] + [piece(p) for p in range(pieces)],
        out_specs=pl.BlockSpec((IN_TM, tile), lambda c, i: (i, c)),
        out_shape=jax.ShapeDtypeStruct((M_ROWS, width), BF16 if gate else F32),
        scratch_shapes=[pltpu.VMEM((M_ROWS, D_MODEL), BF16)],
        compiler_params=_params("arbitrary", "arbitrary"),
        name="inproj_" + ("gate_" if gate else "mix_") + ("lat" if latent else "ctx"),
    )(x, mod, mod, *([w_in_t] * pieces))


def _rope(x, tab_ref, reps):
    w = x.shape[1]
    c = jnp.concatenate([tab_ref[0]] * reps, axis=1) if reps > 1 else tab_ref[0]
    se = jnp.concatenate([tab_ref[1]] * reps, axis=1) if reps > 1 else tab_ref[1]
    so = jnp.concatenate([tab_ref[2]] * reps, axis=1) if reps > 1 else tab_ref[2]
    return x * c + pltpu.roll(x, w - 1, 1) * se + pltpu.roll(x, 1, 1) * so


def _head_rms(x, gain, avg_bf16):
    ms = _dot_exact_rhs(x * x, avg_bf16)
    return x * lax.rsqrt(ms + EPS) * gain


def _prep_kernel(*refs, rope):
    (gq_ref, gk_ref, mqa_ref, mkva_ref, mkr_ref, gqg_ref, gkg_ref, mqg_ref, mkvg_ref,
     wqb_ref, wk_ref, wv_ref, avg_ref) = refs[:13]
    if rope:
        gtab_ref, mtab_ref = refs[13:15]
        outs = refs[15:]
    else:
        outs = refs[13:]
    qb_ref, kb_ref, qd_ref, ckv_ref, kd_ref, vd_ref = outs

    q = _head_rms(gq_ref[...], gqg_ref[...], avg_ref[...])
    k = _head_rms(gk_ref[...], gkg_ref[...], avg_ref[0:LANE, 0:LANE])
    qd = _dot((_rms(mqa_ref[...]) * mqg_ref[...]).astype(BF16), wqb_ref[...])
    ckv = _rms(mkva_ref[...]) * mkvg_ref[...]
    lane = lax.broadcasted_iota(jnp.int32, (1, LANE), 1)
    kr = jnp.where(lane < ML_ROPE, mkr_ref[...], 0.0)
    if rope:
        q = _rope(q, gtab_ref, GQ_HEADS * GQ_HD // LANE)
        k = _rope(k, gtab_ref, 1)
        qd = _rope(qd, mtab_ref, ML_HEADS)
        kr = _rope(kr, mtab_ref, 1)
    qb_ref[...] = q.astype(BF16)
    kb_ref[...] = k
    qd_ref[...] = qd.astype(BF16)
    ckv_ref[...] = ckv
    cb = ckv.astype(BF16)
    kd_ref[...] = (_dot(cb, wk_ref[...]) + jnp.concatenate([kr] * ML_HEADS, axis=1)).astype(BF16)
    vd_ref[...] = _dot(cb, wv_ref[...]).astype(BF16)


def _prep(ymix, lw, tabs, latent):
    tm = 512
    const = lambda i: (0, 0)
    in_specs = [
        pl.BlockSpec((tm, 512), lambda i: (i, COL_GQ)),
        pl.BlockSpec((tm, LANE), lambda i: (i, COL_GK)),
        pl.BlockSpec((tm, 256), lambda i: (i, COL_MQA)),
        pl.BlockSpec((tm, LANE), lambda i: (i, COL_MKVA)),
        pl.BlockSpec((tm, LANE), lambda i: (i, COL_MKR)),
        pl.BlockSpec((1, 512), const),
        pl.BlockSpec((1, LANE), const),
        pl.BlockSpec((1, 256), const),
        pl.BlockSpec((1, LANE), const),
        pl.BlockSpec((ML_Q_RANK, ML_HEADS * LANE), const),
        pl.BlockSpec((ML_KV_RANK, ML_HEADS * LANE), const),
        pl.BlockSpec((ML_KV_RANK, ML_HEADS * ML_V), const),
        pl.BlockSpec((512, 512), const),
    ]
    args = [ymix, ymix, ymix, ymix, ymix, lw["gq_q_gain"], lw["gq_k_gain"], lw["ml_q_gain"],
            lw["ml_kv_gain"], lw["w_qb"], lw["w_k"], lw["w_v"], lw["avg"]]
    if latent:
        per = DEC_SEQ // tm
        in_specs += [pl.BlockSpec((3, tm, LANE), lambda i: (0, i % per, 0))] * 2
        args += [tabs["gq"], tabs["ml"]]
    widths = (512, LANE, ML_HEADS * LANE, LANE, ML_HEADS * LANE, ML_HEADS * ML_V)
    return pl.pallas_call(
        functools.partial(_prep_kernel, rope=latent),
        grid=(M_ROWS // tm,),
        in_specs=in_specs,
        out_specs=[pl.BlockSpec((tm, w), lambda i: (i, 0)) for w in widths],
        out_shape=[jax.ShapeDtypeStruct((M_ROWS, w), dt)
                   for w, dt in zip(widths, (BF16, F32, BF16, F32, BF16, BF16))],
        compiler_params=_params("parallel"),
        name="prep_lat" if latent else "prep_ctx",
    )(*args)


def _mla_cache_kernel(ckv_ref, kr_ref, wk_ref, wv_ref, kd_ref, vd_ref):
    cb = ckv_ref[...].astype(BF16)
    kd_ref[...] = (_dot(cb, wk_ref[...]) + jnp.concatenate([kr_ref[...]] * ML_HEADS, axis=1)).astype(BF16)
    vd_ref[...] = _dot(cb, wv_ref[...]).astype(BF16)


def _mla_cache(ckv, kr_blk, lw, layer):
    rows = DEC_BATCH * PAST_LEN
    tm = PAST_LEN
    const = lambda i: (0, 0)
    return pl.pallas_call(
        _mla_cache_kernel,
        grid=(DEC_BATCH,),
        in_specs=[
            pl.BlockSpec((tm, LANE), lambda i: (i * DEPTH + layer, 0)),
            pl.BlockSpec((tm, LANE), lambda i: (i * DEPTH + layer, 0)),
            pl.BlockSpec((ML_KV_RANK, ML_HEADS * LANE), const),
            pl.BlockSpec((ML_KV_RANK, ML_HEADS * ML_V), const),
        ],
        out_specs=[pl.BlockSpec((tm, ML_HEADS * LANE), lambda i: (i, 0)),
                   pl.BlockSpec((tm, ML_HEADS * ML_V), lambda i: (i, 0))],
        out_shape=[jax.ShapeDtypeStruct((rows, ML_HEADS * LANE), BF16),
                   jax.ShapeDtypeStruct((rows, ML_HEADS * ML_V), BF16)],
        compiler_params=_params("parallel"),
        name="mla_cache",
    )(ckv, kr_blk, lw["w_k"], lw["w_v"])


def _softmax_pv(scores, values, transposed=None):
    transposed = transposed or [False] * len(values)
    m = scores[0].max(axis=-1, keepdims=True)
    for s in scores[1:]:
        m = jnp.maximum(m, s.max(axis=-1, keepdims=True))
    den = None
    out = None
    for s, v, vt in zip(scores, values, transposed):
        p = jnp.exp(s - m)
        d = p.sum(axis=-1, keepdims=True)
        o = _dot_nt(p.astype(BF16), v) if vt else _dot(p.astype(BF16), v)
        den = d if den is None else den + d
        out = o if out is None else out + o
    return out / den


def _softmax_pv_phased(scores, values):
    maxima = [s.max(axis=-1, keepdims=True) for s in scores]
    probs = [jnp.exp(s - m) for s, m in zip(scores, maxima)]
    dens = [p.sum(axis=-1, keepdims=True) for p in probs]
    outs = [_dot(p.astype(BF16), v) for p, v in zip(probs, values)]
    return [o / d for o, d in zip(outs, dens)]


def _lane_halves():
    lane = lax.broadcasted_iota(jnp.int32, (1, LANE), 1)
    return lane < HALF, lane >= HALF


def _attn_kernel(*refs, plan, nseg, scale, fold_scale):
    q_ref = refs[0]
    k_refs = [refs[1 + 2 * i] for i in range(nseg)]
    v_refs = [refs[2 + 2 * i] for i in range(nseg)]
    o_ref = refs[1 + 2 * nseg]
    tq = q_ref.shape[0]
    lo, hi = _lane_halves()
    cache = {}

    def block(kind, seg, blk, swap):
        key = (kind, seg, blk, swap)
        if key not in cache:
            ref = (k_refs if kind == "k" else v_refs)[seg]
            x = ref[:, blk * LANE:(blk + 1) * LANE]
            if swap:
                x = pltpu.roll(x, HALF, 1)
            cache[key] = x.astype(BF16)
        return cache[key]

    def queries(members):
        qs = []
        for qblk, qhalf, _, _ in members:
            q = q_ref[:, qblk * LANE:(qblk + 1) * LANE]
            if fold_scale:
                q = q * scale
            if qhalf is not None:
                q = jnp.where(lo if qhalf == 0 else hi, q, 0.0)
            qs.append(q)
        return (jnp.concatenate(qs, axis=0) if len(qs) > 1 else qs[0]).astype(BF16)

    def scores(q, kblk, swap):
        out = [_dot_nt(q, block("k", s, kblk, swap)) for s in range(nseg)]
        return out if fold_scale else [s * scale for s in out]

    if nseg == 1:
        all_scores = [scores(queries(members), kblk, swap)[0] for members, kblk, swap, _ in plan]
        outs = _softmax_pv_phased(all_scores, [block("v", 0, vblk, swap) for _, _, swap, vblk in plan])
    else:
        outs = []
        ready = scores(queries(plan[0][0]), plan[0][1], plan[0][2])
        for g, (_, _, swap, vblk) in enumerate(plan):
            if g + 1 < len(plan):
                following = scores(queries(plan[g + 1][0]), plan[g + 1][1], plan[g + 1][2])
            outs.append(_softmax_pv(ready, [block("v", s, vblk, swap) for s in range(nseg)]))
            ready = following
    parts = {}
    for (members, _, _, _), out in zip(plan, outs):
        for n, (_, _, oblk, ohalf) in enumerate(members):
            parts[(oblk, ohalf)] = out[n * tq:(n + 1) * tq]
    for oblk in sorted({key[0] for key in parts}):
        o_ref[:, oblk * LANE:(oblk + 1) * LANE] = jnp.where(lo, parts[(oblk, 0)], parts[(oblk, 1)]).astype(BF16)


def _plan_gqa():
    per_kv = GQ_HEADS // GQ_KV_HEADS
    plan = []
    for g in range(GQ_KV_HEADS):
        for half in range(2):
            heads = [h for h in range(g * per_kv, (g + 1) * per_kv) if h % 2 == half]
            plan.append((tuple((h // 2, half, h // 2, half) for h in heads), 0, half != g, 0))
    return tuple(plan)


def _plan_na():
    return tuple((((j, 0, j, 0), (j, 1, j, 1)), j, False, j) for j in range(NA_HEADS // 2))


def _plan_mla():
    return tuple((((h, None, h // 2, h % 2),), h, False, h // 2) for h in range(ML_HEADS))


def _is_pow2(x):
    return float(np.log2(x)).is_integer()


def _attention(q, segs, plan, scale, batch, tq_total, name, tq=256):
    tq = min(tq_total, tq)
    per = tq_total // tq
    q_arr, q_w, q_col, q_off = q
    in_specs = [pl.BlockSpec((tq, q_w), lambda b, i: (q_off + b * per + i, q_col))]
    args = [q_arr]
    for k, v in segs:
        for arr, w, col, rows, stride, off in (k, v):
            in_specs.append(pl.BlockSpec(
                (rows, w), functools.partial(lambda b, i, col, stride, off: (off + b * stride, col),
                                             col=col, stride=stride, off=off)))
            args.append(arr)
    out_w = 512
    return pl.pallas_call(
        functools.partial(_attn_kernel, plan=plan, nseg=len(segs), scale=scale, fold_scale=_is_pow2(scale)),
        grid=(batch, per),
        in_specs=in_specs,
        out_specs=pl.BlockSpec((tq, out_w), lambda b, i: (b * per + i, 0)),
        out_shape=jax.ShapeDtypeStruct((M_ROWS, out_w), BF16),
        compiler_params=_params("parallel", "parallel"),
        name=name,
    )(*args)


NA_DR = 2 * NA_WIN_R - 1
NA_DC = 2 * NA_WIN_C - 1
NA_ROWS = DEC_SEQ // GRID_W
NA_QROWS = 4
NA_WROWS = 12
NA_T_RIGHT = NA_DR - 1
NA_T_LEFT = NA_DR
NA_T_NONE = NA_DR + 1
NA_T_SIZE = NA_DR + 2
assert _is_pow2(NA_SCALE)


def _na_kernel(q_ref, kl_ref, vl_ref, kc_ref, vc_ref, tz_ref, o_ref, kc_scr, vc_scr):
    g = pl.program_id(1)

    @pl.when(g == 0)
    def _():
        kc_scr[...] = kc_ref[...].astype(BF16)
        vc_scr[...] = vc_ref[...].astype(BF16)

    w0 = (g // 2) * (NA_ROWS - NA_WROWS)
    k0 = pl.multiple_of(w0 * GRID_W, (NA_ROWS - NA_WROWS) * GRID_W)
    win = NA_WROWS * GRID_W
    nq = NA_QROWS * GRID_W
    entry = []
    for a in range(NA_QROWS):
        rq = g * NA_QROWS + a
        start = jnp.clip(rq - NA_WIN_R // 2, 0, NA_ROWS - NA_WIN_R)
        per_pair = []
        for i in range(NA_WROWS // 2):
            kr = w0 + 2 * i
            in_l = jnp.logical_and(kr >= start, kr < start + NA_WIN_R)
            in_r = jnp.logical_and(kr + 1 >= start, kr + 1 < start + NA_WIN_R)
            both = jnp.logical_and(in_l, in_r)
            d_l = kr - rq + NA_WIN_R - 1
            per_pair.append(jnp.where(both, d_l, jnp.where(in_r, NA_T_RIGHT, jnp.where(in_l, NA_T_LEFT, NA_T_NONE))))
        entry.append(per_pair)
    lo, hi = _lane_halves()
    def scores(j):
        sl = slice(j * LANE, (j + 1) * LANE)
        k_loc = kl_ref[pl.ds(k0, win), sl].astype(BF16)
        k_ctx_t = kc_scr[sl, :]
        q_pair = q_ref[:, sl] * NA_SCALE
        q = jnp.concatenate([jnp.where(lo, q_pair, 0.0), jnp.where(hi, q_pair, 0.0)], axis=0).astype(BF16)
        bias = jnp.concatenate(
            [jnp.concatenate([tz_ref[2 * j + p, e] for e in entry[a]], axis=1)
             for p in range(2) for a in range(NA_QROWS)], axis=0)
        return [_dot_nt(q, k_loc) + bias, _dot(q, k_ctx_t)]

    pairs = NA_HEADS // 2
    ready = scores(0)
    for j in range(pairs):
        sl = slice(j * LANE, (j + 1) * LANE)
        if j + 1 < pairs:
            following = scores(j + 1)
        v_loc = vl_ref[pl.ds(k0, win), sl].astype(BF16)
        out = _softmax_pv(ready, [v_loc, vc_scr[sl, :]], [False, True])
        o_ref[:, sl] = jnp.where(lo, out[:nq], out[nq:]).astype(BF16)
        ready = following


def _na_bias_kernel(rpb_ref, sel_ref, neg_ref, o_ref):
    o_ref[...] = _dot_exact_rhs(rpb_ref[...], sel_ref[...]) + neg_ref[...]


def _na_bias_tables(rpb):
    col = np.arange(GRID_W)
    col_start = np.clip(col - NA_WIN_C // 2, 0, GRID_W - NA_WIN_C)
    col_ok = (col[None, :] >= col_start[:, None]) & (col[None, :] < col_start[:, None] + NA_WIN_C)
    dc = col[None, :] - col[:, None] + NA_WIN_C - 1
    kpad = 32
    sel = (dc[None, :, :] == np.arange(kpad)[:, None, None]) & col_ok[None]
    sel = jnp.asarray(sel.reshape(kpad, GRID_W * GRID_W), BF16)
    neg = jnp.asarray(np.where(col_ok, 0.0, NEG).reshape(1, GRID_W * GRID_W), F32)
    rpb2 = jnp.pad(rpb.reshape(NA_HEADS * NA_DR, NA_DC), ((0, 0), (0, kpad - NA_DC)))
    n_rows = NA_HEADS * NA_DR
    full = lambda shape: pl.BlockSpec(shape, lambda i: (0, 0))
    t = pl.pallas_call(
        _na_bias_kernel,
        grid=(1,),
        in_specs=[full((n_rows, kpad)), full((kpad, GRID_W * GRID_W)), full((1, GRID_W * GRID_W))],
        out_specs=full((n_rows, GRID_W * GRID_W)),
        out_shape=jax.ShapeDtypeStruct((n_rows, GRID_W * GRID_W), F32),
        compiler_params=_params("arbitrary"),
        name="na_bias",
    )(rpb2, sel, neg)
    t = t.reshape(NA_HEADS, NA_DR, GRID_W, GRID_W)
    masked = jnp.full((NA_HEADS, 1, GRID_W, GRID_W), NEG, F32)
    first, last = NA_WIN_R // 2 - 1, NA_WIN_R // 2 + NA_WIN_R - 2
    return jnp.concatenate([
        jnp.concatenate([t[:, :-1], t[:, 1:]], axis=-1),
        jnp.concatenate([masked, t[:, first:first + 1]], axis=-1),
        jnp.concatenate([t[:, last:last + 1], masked], axis=-1),
        jnp.concatenate([masked, masked], axis=-1)], axis=1)


def _na_latent(ymix, cache_k, cache_v, tz, layer):
    groups = NA_ROWS // NA_QROWS
    nq = NA_QROWS * GRID_W
    return pl.pallas_call(
        _na_kernel,
        grid=(DEC_BATCH, groups),
        in_specs=[
            pl.BlockSpec((nq, 512), lambda b, g: (b * groups + g, COL_NQ)),
            pl.BlockSpec((DEC_SEQ, 512), lambda b, g: (b, COL_NK)),
            pl.BlockSpec((DEC_SEQ, 512), lambda b, g: (b, COL_NV)),
            pl.BlockSpec((NA_HEADS * NA_HD, PAST_LEN), lambda b, g: (b * DEPTH + layer, 0)),
            pl.BlockSpec((NA_HEADS * NA_HD, PAST_LEN), lambda b, g: (b * DEPTH + layer, 0)),
            pl.BlockSpec((NA_HEADS, NA_T_SIZE, GRID_W, LANE), lambda b, g: (0, 0, 0, 0)),
        ],
        out_specs=pl.BlockSpec((nq, 512), lambda b, g: (b * groups + g, 0)),
        out_shape=jax.ShapeDtypeStruct((M_ROWS, 512), BF16),
        scratch_shapes=[pltpu.VMEM((NA_HEADS * NA_HD, PAST_LEN), BF16)] * 2,
        compiler_params=_params("parallel", "arbitrary"),
        name="na_lat",
    )(ymix, ymix, ymix, cache_k, cache_v, tz)


def _na_cache_out_kernel(*refs):
    srcs, (ko_ref, vo_ref) = refs[:2 * DEPTH], refs[2 * DEPTH:]
    layer = pl.program_id(1)
    for l in range(DEPTH):
        @pl.when(layer == l)
        def _():
            ko_ref[...] = srcs[2 * l][...].T
            vo_ref[...] = srcs[2 * l + 1][...].T


def _na_cache_out(ymix_layers):
    rows = NA_HEADS * NA_HD
    in_specs, args = [], []
    for ymix in ymix_layers:
        for col in (COL_NK, COL_NV):
            in_specs.append(pl.BlockSpec((SEQ, 512), functools.partial(lambda b, l, col: (b, col), col=col)))
            args.append(ymix)
    out = pl.pallas_call(
        _na_cache_out_kernel,
        grid=(BATCH, DEPTH),
        in_specs=in_specs,
        out_specs=[pl.BlockSpec((rows, SEQ), lambda b, l: (b * DEPTH + l, 0))] * 2,
        out_shape=[jax.ShapeDtypeStruct((BATCH * DEPTH * rows, SEQ), F32)] * 2,
        compiler_params=_params("parallel", "arbitrary"),
        name="na_cache_out",
    )(*args)
    return tuple(o.reshape(BATCH, DEPTH, NA_HEADS, NA_HD, SEQ).transpose(0, 1, 4, 2, 3) for o in out)


HG_BLOCK = 64
HG_STATE_UNROLL = 4
HG_DEC_ROWS = 8


def _hgrn_kernel(*refs, seq, has_state):
    if has_state:
        (q_ref, ff_ref, fb_ref, v_ref, g_ref, lb_ref, gain_ref, s0_ref,
         o_ref, of_ref, ob_ref, qin_ref, kend_ref, dec_ref, st_ref) = refs
    else:
        (q_ref, ff_ref, fb_ref, v_ref, g_ref, lb_ref, gain_ref,
         o_ref, sout_ref, of_ref, ob_ref, qin_ref, kend_ref, dec_ref, st_ref) = refs
    c = HG_CHUNK
    hc = c // 2
    rb = HG_BLOCK
    per_block = rb // c
    n_chunks = seq // c
    width = HG_HEADS * HG_DK
    sub = HG_DEC_ROWS

    ri = lax.broadcasted_iota(jnp.int32, (rb, rb), 0)
    ci = lax.broadcasted_iota(jnp.int32, (rb, rb), 1)
    same = (ri // c) == (ci // c)
    tri_f = jnp.where(jnp.logical_and(same, ci <= ri), 1.0, 0.0).astype(BF16)
    tri_b = jnp.where(jnp.logical_and(same, ci >= ri), 1.0, 0.0).astype(BF16)
    ones = jnp.ones((HG_DK, HG_DV), BF16)
    rowid = lax.broadcasted_iota(jnp.int32, (c, HG_DV), 0)
    laneid = lax.broadcasted_iota(jnp.int32, (c, HG_DV), 1)
    laneid_half = lax.broadcasted_iota(jnp.int32, (hc, HG_DV), 1)
    o_refs = (of_ref, ob_ref)

    n_blocks = seq // rb

    def block_of(d, step):
        return step if d == 0 else n_blocks - 1 - step

    heads = [slice(h * HG_DK, (h + 1) * HG_DK) for h in range(HG_HEADS)]
    per_chunk = c * hc
    dirs = (0, 1)

    def pair_block(step):
        fwd = [True, False]
        pre_refs = (ff_ref, fb_ref)
        tris = (tri_f, tri_b)
        r0 = [_aligned(block_of(d, step) * rb, rb) for d in dirs]
        near_rows = [range(0, hc), range(hc, c)]
        far_rows = [range(hc, c), range(0, hc)]
        near = [slice(0, hc), slice(hc, c)]
        far = [slice(hc, c), slice(0, hc)]
        edge = [hc - 1, hc]
        q_all = [q_ref[pl.ds(r0[d], rb), :] * (HG_DK ** -0.5) for d in dirs]
        v16_all = [v_ref[pl.ds(r0[d], rb), :].astype(BF16) for d in dirs]
        k_all, log_f = [], []
        for d in dirs:
            lb = lb_ref[d:d + 1, :]
            f = lb + (1.0 - lb) * jax.nn.sigmoid(pre_refs[d][pl.ds(r0[d], rb), :])
            k_all.append(1.0 - f)
            log_f.append(jnp.log(f))
        b_all = [_dot_exact_lhs(tris[d], log_f[d]) * LOG2E for d in dirs]
        src_all = [b_all[d] - jnp.log(k_all[d]) * LOG2E for d in dirs]
        pairs, q_edge, k_edge = [], {}, {}
        zero_half = jnp.zeros((hc, width), F32)
        for d in dirs:
            blocks, q_in, k_end = [], [], []
            for m in range(per_block):
                rows = slice(m * c, (m + 1) * c)
                q, k, b, src = q_all[d][rows], k_all[d][rows], b_all[d][rows], src_all[d][rows]
                b_last = b[c - 1:c] if fwd[d] else b[0:1]
                q_in.append(q * jnp.exp2(b))
                k_end.append(k * jnp.exp2(b_last - b))
                dec_row = _aligned((block_of(d, step) * per_block + m) * sub, sub)
                dec_ref[d, pl.ds(dec_row, sub), :] = jnp.broadcast_to(jnp.exp2(b_last), (sub, width))
                halves = []
                for rows_s, part in ((near_rows[d], near[d]), (far_rows[d], far[d])):
                    q_part, b_part = q[part], b[part]
                    halves += [q_part * jnp.exp2(b_part - src[s:s + 1]) for s in rows_s]
                for i in range(0, c, 2):
                    blocks.append(jnp.concatenate(halves[i:i + 2], axis=0).astype(BF16))
                b_edge = b[edge[d]:edge[d] + 1]
                q_far = q[far[d]] * jnp.exp2(b[far[d]] - b_edge)
                k_near = jnp.exp2(b_edge - src[near[d]])
                q_edge[d, m] = jnp.concatenate([zero_half, q_far] if fwd[d] else [q_far, zero_half], axis=0).astype(BF16)
                k_edge[d, m] = jnp.concatenate([k_near, zero_half] if fwd[d] else [zero_half, k_near], axis=0).astype(BF16)
            qin_ref[d, pl.ds(r0[d], rb), :] = jnp.concatenate(q_in, axis=0).astype(BF16)
            kend_ref[d, pl.ds(r0[d], rb), :] = jnp.concatenate(k_end, axis=0).astype(BF16)
            pairs.append(jnp.concatenate(blocks, axis=0))
        sums = [[_dot(pairs[d][:, sl], ones) for sl in heads] for d in dirs]
        cells = [(d, h, m) for d in dirs for h in range(HG_HEADS) for m in range(per_block)]
        k_pad = jnp.zeros((HG_DK - c, HG_DK), BF16)
        across = {(d, h, m): _dot_nt(q_edge[d, m][:, heads[h]],
                                     jnp.concatenate([k_edge[d, m][:, heads[h]], k_pad], axis=0))
                  for d, h, m in cells}
        near_attn = {cell: jnp.zeros((hc, HG_DV), F32) for cell in cells}
        far_attn = {cell: jnp.zeros((hc, HG_DV), F32) for cell in cells}
        for n in range(hc):
            for d, h, m in cells:
                base = m * per_chunk + n * hc
                near_attn[d, h, m] = jnp.where(laneid_half == near_rows[d][n], sums[d][h][base:base + hc],
                                               near_attn[d, h, m])
                base += hc * hc
                far_attn[d, h, m] = jnp.where(laneid_half == far_rows[d][n], sums[d][h][base:base + hc],
                                              far_attn[d, h, m])
        v_pad = jnp.zeros((HG_DK - c, HG_DV), BF16)
        causal = [laneid <= rowid, laneid >= rowid]
        prods = {}
        for d, h, m in cells:
            halves = [near_attn[d, h, m], far_attn[d, h, m]] if fwd[d] else [far_attn[d, h, m], near_attn[d, h, m]]
            inside = jnp.where(causal[d], jnp.concatenate(halves, axis=0), 0.0)
            values = jnp.concatenate([v16_all[d][m * c:(m + 1) * c, heads[h]], v_pad], axis=0)
            prods[d, h, m] = _dot((inside + across[d, h, m]).astype(BF16), values)
        for d in dirs:
            o_refs[d][pl.ds(r0[d], rb), :] = jnp.concatenate(
                [jnp.concatenate([prods[d, h, m] for m in range(per_block)], axis=0) for h in range(HG_HEADS)], axis=1)


    def state_block(step, carry):
        order = [list(range(per_block)), list(reversed(range(per_block)))]
        r0 = [_aligned(block_of(d, step) * rb, rb) for d in dirs]
        q_in = [qin_ref[d, pl.ds(r0[d], rb), :] for d in dirs]
        k_end = [kend_ref[d, pl.ds(r0[d], rb), :] for d in dirs]
        v16 = [v_ref[pl.ds(r0[d], rb), :].astype(BF16) for d in dirs]
        rows = [slice(m * c, (m + 1) * c) for m in range(per_block)]
        q_state, k_state, k_cross, v_cross, total = [], [], [], [], []
        for d in dirs:
            dec_row = _aligned(block_of(d, step) * per_block * sub, sub)
            dec_all = dec_ref[d, pl.ds(dec_row, per_block * sub), :]
            dec = [dec_all[m * sub:m * sub + 1] for m in order[d]]
            before = [None] * per_block
            after = [None] * per_block
            for j in range(1, per_block):
                before[j] = dec[j - 1] if before[j - 1] is None else before[j - 1] * dec[j - 1]
            for j in range(per_block - 2, -1, -1):
                after[j] = dec[j + 1] if after[j + 1] is None else after[j + 1] * dec[j + 1]
            total.append(before[-1] * dec[-1])
            scale = lambda x, f: x if f is None else x * f
            qs, ks = [None] * per_block, [None] * per_block
            for j, m in enumerate(order[d]):
                qs[m] = scale(q_in[d][rows[m]], before[j])
                ks[m] = scale(k_end[d][rows[m]], after[j])
            q_state.append(jnp.concatenate(qs, axis=0).astype(BF16))
            k_state.append(jnp.concatenate(ks, axis=0).astype(BF16))
            kc, vc = [None], [None]
            for j in range(1, per_block):
                keys, between = [], None
                for i in range(j - 1, -1, -1):
                    keys.insert(0, scale(k_end[d][rows[order[d][i]]], between))
                    between = dec[i] if between is None else between * dec[i]
                kc.append(jnp.concatenate(keys, axis=0).astype(BF16))
                vc.append(jnp.concatenate([v16[d][rows[order[d][i]]] for i in range(j)], axis=0))
            k_cross.append(kc)
            v_cross.append(vc)
        q16 = [q_in[d].astype(BF16) for d in dirs]
        cells = [(d, h) for d in dirs for h in range(HG_HEADS)]
        kv = {(d, h): _dot_tn(v16[d][:, heads[h]], k_state[d][:, heads[h]]) for d, h in cells}
        attn = {(d, h, j): _dot_nt(q16[d][rows[order[d][j]], heads[h]], k_cross[d][j][:, heads[h]])
                for d, h in cells for j in range(1, per_block)}
        st = {(d, h): st_ref[d, h] for d, h in cells}
        from_state = {(d, h): _dot_nt(q_state[d][:, heads[h]], st[d, h].astype(BF16)) for d, h in cells}
        cross = {key: _dot(a.astype(BF16), v_cross[key[0]][key[2]][:, heads[key[1]]]) for key, a in attn.items()}
        for d, h in cells:
            st_ref[d, h] = st[d, h] * total[d][:, heads[h]] + kv[d, h]
        for d in dirs:
            cols = []
            for h in range(HG_HEADS):
                parts = [None] * per_block
                for j, m in enumerate(order[d]):
                    part = from_state[d, h][rows[m]]
                    parts[m] = part if j == 0 else part + cross[d, h, j]
                cols.append(jnp.concatenate(parts, axis=0))
            o_refs[d][pl.ds(r0[d], rb), :] += jnp.concatenate(cols, axis=1)
        return carry

    if has_state:
        for d in range(2):
            for h in range(HG_HEADS):
                st_ref[d, h] = s0_ref[d, h].T
    else:
        st_ref[...] = jnp.zeros(st_ref.shape, F32)

    def pair_step(i, carry):
        pair_block(i)
        return carry

    lax.fori_loop(0, n_blocks, pair_step, 0)
    lax.fori_loop(0, n_blocks, state_block, 0, unroll=HG_STATE_UNROLL)

    for h in range(HG_HEADS):
        sl = slice(h * HG_DV, (h + 1) * HG_DV)
        o = of_ref[:, sl] + ob_ref[:, sl]
        o_ref[:, sl] = (_rms(o) * gain_ref[:, sl] * _silu(g_ref[:, sl])).astype(BF16)
    if not has_state:
        for d in range(2):
            for h in range(HG_HEADS):
                sout_ref[d, h] = st_ref[d, h].T


def _hgrn(ymix, lb, gain, batch, seq, state, layer):
    width = HG_HEADS * HG_DK
    has_state = state is not None
    assert HG_DK == HG_DV
    st_shape = (2, HG_HEADS, HG_DK, HG_DV)
    in_specs = [pl.BlockSpec((seq, width), functools.partial(lambda b, col: (b, col), col=col))
                for col in (COL_HQ, COL_HFF, COL_HFB, COL_HI, COL_HG)]
    in_specs += [pl.BlockSpec((2, width), lambda b: (0, 0)), pl.BlockSpec((1, width), lambda b: (0, 0))]
    args = [ymix] * 5 + [lb, gain]
    out_specs = [pl.BlockSpec((seq, width), lambda b: (b, 0))]
    out_shape = [jax.ShapeDtypeStruct((M_ROWS, width), BF16)]
    if has_state:
        in_specs.append(pl.BlockSpec((None, None) + st_shape, lambda b: (b, layer, 0, 0, 0, 0)))
        args.append(state)
    else:
        out_specs.append(pl.BlockSpec((None,) + st_shape, lambda b: (b, 0, 0, 0, 0)))
        out_shape.append(jax.ShapeDtypeStruct((batch,) + st_shape, F32))
    return pl.pallas_call(
        functools.partial(_hgrn_kernel, seq=seq, has_state=has_state),
        grid=(batch,),
        in_specs=in_specs,
        out_specs=out_specs,
        out_shape=out_shape,
        scratch_shapes=[pltpu.VMEM((seq, width), F32), pltpu.VMEM((seq, width), F32),
                        pltpu.VMEM((2, seq, width), BF16), pltpu.VMEM((2, seq, width), BF16),
                        pltpu.VMEM((2, seq // HG_CHUNK * HG_DEC_ROWS, width), F32),
                        pltpu.VMEM(st_shape, F32)],
        compiler_params=_params("parallel"),
        name="hgrn_lat" if has_state else "hgrn_ctx",
    )(*args)


def _merge_kernel(oa_ref, ob_ref, oc_ref, od_ref, gt_ref, wb_ref, wo_ref, x_ref, g1_ref, out_ref):
    acc = None
    for n, o_ref in enumerate((oa_ref, ob_ref, oc_ref, od_ref)):
        bo = _dot(o_ref[...], wb_ref[n])
        term = gt_ref[:, n * D_MODEL:(n + 1) * D_MODEL] * bo
        acc = term if acc is None else acc + term
    out_ref[...] = x_ref[...] + g1_ref[...] * _dot(acc.astype(BF16), wo_ref[...])


def _merge(branches, gates, w_branch, w_out, x, mod, latent):
    tm = 512
    row = _mod_row(latent, tm)
    tile = lambda w: pl.BlockSpec((tm, w), lambda i: (i, 0))
    return pl.pallas_call(
        _merge_kernel,
        grid=(M_ROWS // tm,),
        in_specs=[tile(BRANCH_W)] * N_BRANCH + [
            tile(GATE_W),
            pl.BlockSpec((N_BRANCH, BRANCH_W, D_MODEL), lambda i: (0, 0, 0)),
            pl.BlockSpec((D_MODEL, D_MODEL), lambda i: (0, 0)),
            tile(D_MODEL),
            pl.BlockSpec((None, None, 1, D_MODEL), lambda i: (row(i), 2, 0, 0)),
        ],
        out_specs=tile(D_MODEL),
        out_shape=jax.ShapeDtypeStruct((M_ROWS, D_MODEL), F32),
        compiler_params=_params("parallel"),
        name="merge_lat" if latent else "merge_ctx",
    )(*branches, gates, w_branch, w_out, x, mod)


FFN_CHUNK = 256
FFN_STEPS = FFN_HIDDEN // FFN_CHUNK
assert FFN_STEPS * FFN_CHUNK == FFN_HIDDEN


def _ffn_kernel(*refs, final):
    x_ref, sc_ref, sh_ref, g2_ref, wa_ref, wg_ref, wo_ref = refs[:7]
    if final:
        fg_ref, out_ref, h_ref, acc_ref = refs[7:]
    else:
        out_ref, h_ref, acc_ref = refs[7:]
    j = pl.program_id(1)

    @pl.when(j == 0)
    def _():
        h = _rms(x_ref[...]) * (1.0 + sc_ref[...]) + sh_ref[...]
        h_ref[...] = h.astype(BF16)
        acc_ref[...] = jnp.zeros(acc_ref.shape, F32)

    h = h_ref[...]
    act = _silu(_dot(h, wg_ref[...].astype(BF16))) * _dot(h, wa_ref[...].astype(BF16))
    acc_ref[...] += _dot(act.astype(BF16), wo_ref[...].astype(BF16))

    @pl.when(j == FFN_STEPS - 1)
    def _():
        y = x_ref[...] + g2_ref[...] * acc_ref[...]
        out_ref[...] = _rms(y) * fg_ref[...] if final else y


def _ffn(x, mod, w_in, w_out, layer, latent, final_gain):
    tm = 1024
    row = _mod_row(latent, tm)
    modspec = lambda which: pl.BlockSpec((None, None, 1, D_MODEL), lambda i, j: (row(i), which, 0, 0))
    final = final_gain is not None
    in_specs = [
        pl.BlockSpec((tm, D_MODEL), lambda i, j: (i, 0)),
        modspec(4), modspec(3), modspec(5),
        pl.BlockSpec((None, D_MODEL, FFN_CHUNK), lambda i, j: (layer, 0, j)),
        pl.BlockSpec((None, D_MODEL, FFN_CHUNK), lambda i, j: (layer, 0, FFN_STEPS + j)),
        pl.BlockSpec((None, FFN_CHUNK, D_MODEL), lambda i, j: (layer, j, 0)),
    ]
    args = [x, mod, mod, mod, w_in, w_in, w_out]
    if final:
        in_specs.append(pl.BlockSpec((1, D_MODEL), lambda i, j: (0, 0)))
        args.append(final_gain.reshape(1, D_MODEL))
    return pl.pallas_call(
        functools.partial(_ffn_kernel, final=final),
        grid=(M_ROWS // tm, FFN_STEPS),
        in_specs=in_specs,
        out_specs=pl.BlockSpec((tm, D_MODEL), lambda i, j: (i, 0)),
        out_shape=jax.ShapeDtypeStruct((M_ROWS, D_MODEL), F32),
        scratch_shapes=[pltpu.VMEM((tm, D_MODEL), BF16), pltpu.VMEM((tm, D_MODEL), F32)],
        compiler_params=_params("parallel", "arbitrary"),
        name="ffn_lat" if latent else "ffn_ctx",
    )(*args)


def _rope_tables():
    t = jnp.arange(DEC_SEQ)
    row = (t // GRID_W).astype(F32)[:, None]
    col = (t % GRID_W).astype(F32)[:, None]

    def angles(rot_dim):
        n_freq = rot_dim // 4
        inv_freq = ROPE_THETA ** (-jnp.arange(n_freq, dtype=F32) / n_freq)
        return jnp.concatenate([row * inv_freq, col * inv_freq], axis=-1)

    def expand(ang):
        cos = jnp.repeat(jnp.cos(ang), 2, axis=-1)
        sin = jnp.repeat(jnp.sin(ang), 2, axis=-1)
        even = (jnp.arange(cos.shape[-1]) % 2 == 0)[None, :]
        return cos, jnp.where(even, -sin, 0.0), jnp.where(even, 0.0, sin)

    gq = [jnp.concatenate([a, a], axis=-1) for a in expand(angles(GQ_HD))]
    ml = []
    for idx, a in enumerate(expand(angles(ML_ROPE))):
        fill = 1.0 if idx == 0 else 0.0
        ml.append(jnp.concatenate([a, jnp.full((DEC_SEQ, LANE - ML_ROPE), fill, F32)], axis=-1))
    return {"gq": jnp.stack(gq), "ml": jnp.stack(ml)}


def _layer_weights(l, gq_q_gain, gq_k_gain, ml_q_a_gain, ml_kv_a_gain, ml_w_q_b, ml_w_kv_b,
                   w_branch, w_out, hg_gain, avg):
    pad = LANE - ML_NOPE - ML_ROPE
    qb = ml_w_q_b[l].reshape(ML_Q_RANK, ML_HEADS, ML_NOPE + ML_ROPE)
    qb = jnp.concatenate([qb[:, :, ML_NOPE:], qb[:, :, :ML_NOPE], jnp.zeros((ML_Q_RANK, ML_HEADS, pad), F32)],
                         axis=-1).reshape(ML_Q_RANK, ML_HEADS * LANE)
    kvb = ml_w_kv_b[l].reshape(ML_KV_RANK, ML_HEADS, ML_NOPE + ML_V)
    wk = jnp.pad(kvb[:, :, :ML_NOPE], ((0, 0), (0, 0), (ML_ROPE, pad))).reshape(ML_KV_RANK, ML_HEADS * LANE)
    wv = kvb[:, :, ML_NOPE:].reshape(ML_KV_RANK, ML_HEADS * ML_V)
    return {
        "gq_q_gain": jnp.tile(gq_q_gain[l], GQ_HEADS).reshape(1, -1),
        "gq_k_gain": jnp.tile(gq_k_gain[l], GQ_KV_HEADS).reshape(1, -1),
        "ml_q_gain": ml_q_a_gain[l].reshape(1, -1), "ml_kv_gain": ml_kv_a_gain[l].reshape(1, -1),
        "w_qb": qb.astype(BF16), "w_k": wk.astype(BF16), "w_v": wv.astype(BF16),
        "w_branch": w_branch[l].astype(BF16), "w_out": w_out[l].astype(BF16),
        "hg_gain": jnp.tile(hg_gain[l], HG_HEADS).reshape(1, -1), "avg": avg,
    }


def _seg(arr, width, col, rows, stride, off):
    return (arr, width, col, rows, stride, off)


def kernel(x_prompt, x_sample, state_hgrn, cache_gqa_k, cache_gqa_v, cache_na_k, cache_na_v, cache_mla_ckv, cache_mla_krope, c, c_ctx, w_ada, b_ada, w_in, hg_lb_logits, hg_gain, gq_q_gain, gq_k_gain, na_rpb, ml_q_a_gain, ml_kv_a_gain, ml_w_q_b, ml_w_kv_b, w_branch, w_out, w_ffn_in, w_ffn_out, final_gain):
    cond8 = jnp.concatenate([c_ctx[None, :], c, jnp.zeros((8 - 1 - DEC_BATCH, D_MODEL), F32)], axis=0)
    mods = _ada(cond8, w_ada, b_ada)

    lb = jnp.cumsum(jax.nn.softmax(hg_lb_logits.astype(F32), axis=0), axis=0)
    lb = lb - lb[:1]
    avg = jnp.asarray(np.kron(np.eye(512 // GQ_HD), np.full((GQ_HD, GQ_HD), 1.0 / GQ_HD)), BF16)
    tabs = _rope_tables()
    w_in_t = jnp.swapaxes(w_in, 1, 2).reshape(DEPTH * w_in.shape[2], D_MODEL)

    gqk_c = cache_gqa_k.reshape(DEC_BATCH * DEPTH * PAST_LEN, GQ_KV_HEADS * GQ_HD)
    gqv_c = cache_gqa_v.reshape(DEC_BATCH * DEPTH * PAST_LEN, GQ_KV_HEADS * GQ_HD)
    nak_c = cache_na_k.transpose(0, 1, 3, 4, 2).reshape(DEC_BATCH * DEPTH * NA_HEADS * NA_HD, PAST_LEN)
    nav_c = cache_na_v.transpose(0, 1, 3, 4, 2).reshape(DEC_BATCH * DEPTH * NA_HEADS * NA_HD, PAST_LEN)
    mckv_c = cache_mla_ckv.reshape(DEC_BATCH * DEPTH * PAST_LEN, ML_KV_RANK)
    mkr_c = jnp.pad(cache_mla_krope.reshape(DEC_BATCH * DEPTH * PAST_LEN, ML_ROPE),
                    ((0, 0), (0, LANE - ML_ROPE)))

    xp = x_prompt.reshape(M_ROWS, D_MODEL)
    xs = x_sample.reshape(M_ROWS, D_MODEL)
    new = []
    ymix_ctx = []
    for l in range(DEPTH):
        lw = _layer_weights(l, gq_q_gain, gq_k_gain, ml_q_a_gain, ml_kv_a_gain, ml_w_q_b,
                            ml_w_kv_b, w_branch, w_out, hg_gain, avg)
        last = l == DEPTH - 1
        mod = mods[l]

        ymix = _inproj(xp, mod, w_in_t, l, False, gate=False)
        gates = _inproj(xp, mod, w_in_t, l, False, gate=True)
        qb, kb, qd, ckv, kd, vd = _prep(ymix, lw, tabs, False)
        out_a, st = _hgrn(ymix, lb[l], lw["hg_gain"], BATCH, SEQ, None, l)
        out_b = _attention((qb, 512, 0, 0),
                           [(_seg(kb, LANE, 0, SEQ, 1, 0), _seg(ymix, LANE, COL_GV, SEQ, 1, 0))],
                           _plan_gqa(), GQ_SCALE, BATCH, SEQ, "gqa_ctx")
        out_c = _attention((ymix, 512, COL_NQ, 0),
                           [(_seg(ymix, 512, COL_NK, SEQ, 1, 0), _seg(ymix, 512, COL_NV, SEQ, 1, 0))],
                           _plan_na(), NA_SCALE, BATCH, SEQ, "na_ctx")
        out_d = _attention((qd, ML_HEADS * LANE, 0, 0),
                           [(_seg(kd, ML_HEADS * LANE, 0, SEQ, 1, 0), _seg(vd, 512, 0, SEQ, 1, 0))],
                           _plan_mla(), ML_SCALE, BATCH, SEQ, "mla_ctx")
        xp = _merge((out_a, out_b, out_c, out_d), gates, lw["w_branch"], lw["w_out"], xp, mod, False)
        xp = _ffn(xp, mod, w_ffn_in, w_ffn_out, l, False, final_gain if last else None)
        new.append((
            st,
            kb.reshape(BATCH, SEQ, GQ_KV_HEADS, GQ_HD),
            ymix[:, COL_GV * LANE:(COL_GV + 1) * LANE].reshape(BATCH, SEQ, GQ_KV_HEADS, GQ_HD),
            ckv.reshape(BATCH, SEQ, ML_KV_RANK),
            ymix[:, COL_MKR * LANE:COL_MKR * LANE + ML_ROPE].reshape(BATCH, SEQ, ML_ROPE),
        ))
        ymix_ctx.append(ymix)

        ymix = _inproj(xs, mod, w_in_t, l, True, gate=False)
        gates = _inproj(xs, mod, w_in_t, l, True, gate=True)
        qb, kb, qd, ckv, kd, vd = _prep(ymix, lw, tabs, True)
        kd_c, vd_c = _mla_cache(mckv_c, mkr_c, lw, l)
        out_a, = _hgrn(ymix, lb[l], lw["hg_gain"], DEC_BATCH, DEC_SEQ, state_hgrn, l)
        out_b = _attention((qb, 512, 0, 0),
                           [(_seg(gqk_c, LANE, 0, PAST_LEN, DEPTH, l), _seg(gqv_c, LANE, 0, PAST_LEN, DEPTH, l)),
                            (_seg(kb, LANE, 0, DEC_SEQ, 1, 0), _seg(ymix, LANE, COL_GV, DEC_SEQ, 1, 0))],
                           _plan_gqa(), GQ_SCALE, DEC_BATCH, DEC_SEQ, "gqa_lat")
        out_c = _na_latent(ymix, nak_c, nav_c, _na_bias_tables(na_rpb[l]), l)
        out_d = _attention((qd, ML_HEADS * LANE, 0, 0),
                           [(_seg(kd_c, ML_HEADS * LANE, 0, PAST_LEN, 1, 0), _seg(vd_c, 512, 0, PAST_LEN, 1, 0)),
                            (_seg(kd, ML_HEADS * LANE, 0, DEC_SEQ, 1, 0), _seg(vd, 512, 0, DEC_SEQ, 1, 0))],
                           _plan_mla(), ML_SCALE, DEC_BATCH, DEC_SEQ, "mla_lat", tq=512)
        xs = _merge((out_a, out_b, out_c, out_d), gates, lw["w_branch"], lw["w_out"], xs, mod, True)
        xs = _ffn(xs, mod, w_ffn_in, w_ffn_out, l, True, final_gain if last else None)

    y_prompt = xp.reshape(BATCH, SEQ, D_MODEL)
    y_sample = xs.reshape(DEC_BATCH, DEC_SEQ, D_MODEL)
    state, gqa_k, gqa_v, ckv_new, krope_new = (jnp.stack([n[i] for n in new], axis=1) for i in range(5))
    na_k, na_v = _na_cache_out(ymix_ctx)
    return (y_prompt, y_sample, state, gqa_k, gqa_v, na_k, na_v, ckv_new, krope_new)
```

```python
import functools

import numpy as np
import jax
import jax.numpy as jnp
from jax import lax
from jax.experimental import pallas as pl
from jax.experimental.pallas import tpu as pltpu

F32 = jnp.float32
BF16 = jnp.bfloat16

D_MODEL = 1024
BATCH = 16
SEQ = 256
DEPTH = 2
DEC_BATCH = 4
DEC_SEQ = 1024
PAST_LEN = 512
GRID_W = 64
EPS = 1e-6
ROPE_THETA = 10000.0
N_BRANCH = 4
BRANCH_W = 512
HG_HEADS = 4
HG_DK = 128
HG_DV = 128
GQ_HEADS = 8
GQ_KV_HEADS = 2
GQ_HD = 64
NA_HEADS = 8
NA_HD = 64
NA_WIN_R = 8
NA_WIN_C = 16
ML_HEADS = 8
ML_NOPE = 64
ML_ROPE = 32
ML_V = 64
ML_Q_RANK = 256
ML_KV_RANK = 128
FFN_HIDDEN = 2816
GQ_SCALE = GQ_HD ** -0.5
NA_SCALE = NA_HD ** -0.5
ML_SCALE = (ML_NOPE + ML_ROPE) ** -0.5

M_ROWS = BATCH * SEQ
assert M_ROWS == DEC_BATCH * DEC_SEQ

LANE = 128
HALF = 64
MIX_W = 5376
IN_MIX = 5280
GATE_W = N_BRANCH * D_MODEL
HG_CHUNK = 16
NEG = -1e30
LOG2E = 1.4426950408889634
VMEM_LIMIT = 56 * 1024 * 1024

COL_HQ, COL_HFF, COL_HFB, COL_HI, COL_HG, COL_GQ, COL_NQ, COL_NK, COL_NV = range(9)
COL_MQA = 18
COL_GK, COL_GV, COL_MKVA, COL_MKR = 38, 39, 40, 41


def _dot(a, b):
    return jnp.dot(a, b, preferred_element_type=F32)


def _dot_nt(a, b):
    return lax.dot_general(a, b, (((1,), (1,)), ((), ())), preferred_element_type=F32)


def _dot_tn(a, b):
    return lax.dot_general(a, b, (((0,), (0,)), ((), ())), preferred_element_type=F32)


def _split3(x):
    x1 = x.astype(BF16)
    r1 = x - x1.astype(F32)
    x2 = r1.astype(BF16)
    x3 = (r1 - x2.astype(F32)).astype(BF16)
    return x1, x2, x3


def _dot_exact_lhs(a_bf16, x):
    x1, x2, x3 = _split3(x)
    return (_dot(a_bf16, x3) + _dot(a_bf16, x2)) + _dot(a_bf16, x1)


def _dot_exact_rhs(x, b_bf16):
    x1, x2, x3 = _split3(x)
    return (_dot(x3, b_bf16) + _dot(x2, b_bf16)) + _dot(x1, b_bf16)


def _rms(x):
    return x * lax.rsqrt(jnp.mean(x * x, axis=-1, keepdims=True) + EPS)


def _silu(x):
    return x * jax.nn.sigmoid(x)


def _aligned(x, m):
    return x if isinstance(x, int) else pl.multiple_of(x, m)


def _params(*sem):
    return pltpu.CompilerParams(dimension_semantics=sem, vmem_limit_bytes=VMEM_LIMIT)


def _mod_row(latent, tm):
    if latent:
        return lambda i: 1 + (i * tm) // DEC_SEQ
    return lambda i: 0


def _ada_kernel(c_ref, w_ref, b_ref, o_ref):
    c = c_ref[...]
    o_ref[...] = _dot(_silu(c).astype(BF16), w_ref[...].astype(BF16)) + b_ref[...]


def _ada(cond8, w_ada, b_ada):
    tn = 1536
    out = pl.pallas_call(
        _ada_kernel,
        grid=(DEPTH, 6 * D_MODEL // tn),
        in_specs=[
            pl.BlockSpec((8, D_MODEL), lambda l, j: (0, 0)),
            pl.BlockSpec((None, D_MODEL, tn), lambda l, j: (l, 0, j)),
            pl.BlockSpec((None, 1, tn), lambda l, j: (l, 0, j)),
        ],
        out_specs=pl.BlockSpec((None, 8, tn), lambda l, j: (l, 0, j)),
        out_shape=jax.ShapeDtypeStruct((DEPTH, 8, 6 * D_MODEL), F32),
        compiler_params=_params("parallel", "parallel"),
        name="ada",
    )(cond8, w_ada, b_ada.reshape(DEPTH, 1, 6 * D_MODEL))
    return out.reshape(DEPTH, 8, 6, 1, D_MODEL)


IN_PIECE = 256
IN_TM = 1024


def _mix_source(t):
    blk = jnp.where(t < 12, t, jnp.where(t < 18, t + 1, jnp.where(t == 18, 19, jnp.where(t == 19, 12, 20))))
    return blk * IN_PIECE


def _gate_source(t):
    return IN_MIX + t * IN_PIECE


def _inproj_kernel(*refs, pieces, gate):
    x_ref, sc_ref, sh_ref = refs[:3]
    w_refs = refs[3:3 + pieces]
    o_ref, h_ref = refs[3 + pieces:]
    rows = pl.ds(pl.multiple_of(pl.program_id(1) * IN_TM, IN_TM), IN_TM)

    @pl.when(pl.program_id(0) == 0)
    def _():
        h = _rms(x_ref[...]) * (1.0 + sc_ref[...]) + sh_ref[...]
        h_ref[rows, :] = h.astype(BF16)

    h = h_ref[rows, :]
    for p, w_ref in enumerate(w_refs):
        y = _dot_nt(h, w_ref[...].astype(BF16))
        o_ref[:, p * IN_PIECE:(p + 1) * IN_PIECE] = jax.nn.sigmoid(y).astype(o_ref.dtype) if gate else y


def _inproj(x, mod, w_in_t, layer, latent, gate):
    width, tile, source = (GATE_W, 2048, _gate_source) if gate else (MIX_W, 1792, _mix_source)
    pieces = tile // IN_PIECE
    row = _mod_row(latent, IN_TM)
    first = lambda c, i: jnp.where(c == 0, i, 0)
    in_width = w_in_t.shape[0] // DEPTH
    piece = lambda p: pl.BlockSpec((pl.Element(IN_PIECE), pl.Element(D_MODEL)),
                                   lambda c, i: (pl.multiple_of(layer * in_width + source(c * pieces + p), 8), 0))
    return pl.pallas_call(
        functools.partial(_inproj_kernel, pieces=pieces, gate=gate),
        grid=(width // tile, M_ROWS // IN_TM),
        in_specs=[
            pl.BlockSpec((IN_TM, D_MODEL), lambda c, i: (first(c, i), 0)),
            pl.BlockSpec((None, None, 1, D_MODEL), lambda c, i: (row(first(c, i)), 1, 0, 0)),
            pl.BlockSpec((None, None, 1, D_MODEL), lambda c, i: (row(first(c, i)), 0, 0, 0)),
        ] + [piece(p) for p in range(pieces)],
        out_specs=pl.BlockSpec((IN_TM, tile), lambda c, i: (i, c)),
        out_shape=jax.ShapeDtypeStruct((M_ROWS, width), BF16 if gate else F32),
        scratch_shapes=[pltpu.VMEM((M_ROWS, D_MODEL), BF16)],
        compiler_params=_params("arbitrary", "arbitrary"),
        name="inproj_" + ("gate_" if gate else "mix_") + ("lat" if latent else "ctx"),
    )(x, mod, mod, *([w_in_t] * pieces))


def _rope(x, tab_ref, reps):
    w = x.shape[1]
    c = jnp.concatenate([tab_ref[0]] * reps, axis=1) if reps > 1 else tab_ref[0]
    se = jnp.concatenate([tab_ref[1]] * reps, axis=1) if reps > 1 else tab_ref[1]
    so = jnp.concatenate([tab_ref[2]] * reps, axis=1) if reps > 1 else tab_ref[2]
    return x * c + pltpu.roll(x, w - 1, 1) * se + pltpu.roll(x, 1, 1) * so


def _head_rms(x, gain, avg_bf16):
    ms = _dot_exact_rhs(x * x, avg_bf16)
    return x * lax.rsqrt(ms + EPS) * gain


def _prep_kernel(*refs, rope):
    (gq_ref, gk_ref, mqa_ref, mkva_ref, mkr_ref, gqg_ref, gkg_ref, mqg_ref, mkvg_ref,
     wqb_ref, wk_ref, wv_ref, avg_ref) = refs[:13]
    if rope:
        gtab_ref, mtab_ref = refs[13:15]
        outs = refs[15:]
    else:
        outs = refs[13:]
    qb_ref, kb_ref, qd_ref, ckv_ref, kd_ref, vd_ref = outs

    q = _head_rms(gq_ref[...], gqg_ref[...], avg_ref[...])
    k = _head_rms(gk_ref[...], gkg_ref[...], avg_ref[0:LANE, 0:LANE])
    qd = _dot((_rms(mqa_ref[...]) * mqg_ref[...]).astype(BF16), wqb_ref[...])
    ckv = _rms(mkva_ref[...]) * mkvg_ref[...]
    lane = lax.broadcasted_iota(jnp.int32, (1, LANE), 1)
    kr = jnp.where(lane < ML_ROPE, mkr_ref[...], 0.0)
    if rope:
        q = _rope(q, gtab_ref, GQ_HEADS * GQ_HD // LANE)
        k = _rope(k, gtab_ref, 1)
        qd = _rope(qd, mtab_ref, ML_HEADS)
        kr = _rope(kr, mtab_ref, 1)
    qb_ref[...] = q.astype(BF16)
    kb_ref[...] = k
    qd_ref[...] = qd.astype(BF16)
    ckv_ref[...] = ckv
    cb = ckv.astype(BF16)
    kd_ref[...] = (_dot(cb, wk_ref[...]) + jnp.concatenate([kr] * ML_HEADS, axis=1)).astype(BF16)
    vd_ref[...] = _dot(cb, wv_ref[...]).astype(BF16)


def _prep(ymix, lw, tabs, latent):
    tm = 512
    const = lambda i: (0, 0)
    in_specs = [
        pl.BlockSpec((tm, 512), lambda i: (i, COL_GQ)),
        pl.BlockSpec((tm, LANE), lambda i: (i, COL_GK)),
        pl.BlockSpec((tm, 256), lambda i: (i, COL_MQA)),
        pl.BlockSpec((tm, LANE), lambda i: (i, COL_MKVA)),
        pl.BlockSpec((tm, LANE), lambda i: (i, COL_MKR)),
        pl.BlockSpec((1, 512), const),
        pl.BlockSpec((1, LANE), const),
        pl.BlockSpec((1, 256), const),
        pl.BlockSpec((1, LANE), const),
        pl.BlockSpec((ML_Q_RANK, ML_HEADS * LANE), const),
        pl.BlockSpec((ML_KV_RANK, ML_HEADS * LANE), const),
        pl.BlockSpec((ML_KV_RANK, ML_HEADS * ML_V), const),
        pl.BlockSpec((512, 512), const),
    ]
    args = [ymix, ymix, ymix, ymix, ymix, lw["gq_q_gain"], lw["gq_k_gain"], lw["ml_q_gain"],
            lw["ml_kv_gain"], lw["w_qb"], lw["w_k"], lw["w_v"], lw["avg"]]
    if latent:
        per = DEC_SEQ // tm
        in_specs += [pl.BlockSpec((3, tm, LANE), lambda i: (0, i % per, 0))] * 2
        args += [tabs["gq"], tabs["ml"]]
    widths = (512, LANE, ML_HEADS * LANE, LANE, ML_HEADS * LANE, ML_HEADS * ML_V)
    return pl.pallas_call(
        functools.partial(_prep_kernel, rope=latent),
        grid=(M_ROWS // tm,),
        in_specs=in_specs,
        out_specs=[pl.BlockSpec((tm, w), lambda i: (i, 0)) for w in widths],
        out_shape=[jax.ShapeDtypeStruct((M_ROWS, w), dt)
                   for w, dt in zip(widths, (BF16, F32, BF16, F32, BF16, BF16))],
        compiler_params=_params("parallel"),
        name="prep_lat" if latent else "prep_ctx",
    )(*args)


def _mla_cache_kernel(ckv_ref, kr_ref, wk_ref, wv_ref, kd_ref, vd_ref):
    cb = ckv_ref[...].astype(BF16)
    kd_ref[...] = (_dot(cb, wk_ref[...]) + jnp.concatenate([kr_ref[...]] * ML_HEADS, axis=1)).astype(BF16)
    vd_ref[...] = _dot(cb, wv_ref[...]).astype(BF16)


def _mla_cache(ckv, kr_blk, lw, layer):
    rows = DEC_BATCH * PAST_LEN
    tm = PAST_LEN
    const = lambda i: (0, 0)
    return pl.pallas_call(
        _mla_cache_kernel,
        grid=(DEC_BATCH,),
        in_specs=[
            pl.BlockSpec((tm, LANE), lambda i: (i * DEPTH + layer, 0)),
            pl.BlockSpec((tm, LANE), lambda i: (i * DEPTH + layer, 0)),
            pl.BlockSpec((ML_KV_RANK, ML_HEADS * LANE), const),
            pl.BlockSpec((ML_KV_RANK, ML_HEADS * ML_V), const),
        ],
        out_specs=[pl.BlockSpec((tm, ML_HEADS * LANE), lambda i: (i, 0)),
                   pl.BlockSpec((tm, ML_HEADS * ML_V), lambda i: (i, 0))],
        out_shape=[jax.ShapeDtypeStruct((rows, ML_HEADS * LANE), BF16),
                   jax.ShapeDtypeStruct((rows, ML_HEADS * ML_V), BF16)],
        compiler_params=_params("parallel"),
        name="mla_cache",
    )(ckv, kr_blk, lw["w_k"], lw["w_v"])


def _softmax_pv(scores, values, transposed=None):
    transposed = transposed or [False] * len(values)
    m = scores[0].max(axis=-1, keepdims=True)
    for s in scores[1:]:
        m = jnp.maximum(m, s.max(axis=-1, keepdims=True))
    den = None
    out = None
    for s, v, vt in zip(scores, values, transposed):
        p = jnp.exp(s - m)
        d = p.sum(axis=-1, keepdims=True)
        o = _dot_nt(p.astype(BF16), v) if vt else _dot(p.astype(BF16), v)
        den = d if den is None else den + d
        out = o if out is None else out + o
    return out / den


def _softmax_pv_phased(scores, values):
    maxima = [s.max(axis=-1, keepdims=True) for s in scores]
    probs = [jnp.exp(s - m) for s, m in zip(scores, maxima)]
    dens = [p.sum(axis=-1, keepdims=True) for p in probs]
    outs = [_dot(p.astype(BF16), v) for p, v in zip(probs, values)]
    return [o / d for o, d in zip(outs, dens)]


def _lane_halves():
    lane = lax.broadcasted_iota(jnp.int32, (1, LANE), 1)
    return lane < HALF, lane >= HALF


def _attn_kernel(*refs, plan, nseg, scale, fold_scale):
    q_ref = refs[0]
    k_refs = [refs[1 + 2 * i] for i in range(nseg)]
    v_refs = [refs[2 + 2 * i] for i in range(nseg)]
    o_ref = refs[1 + 2 * nseg]
    tq = q_ref.shape[0]
    lo, hi = _lane_halves()
    cache = {}

    def block(kind, seg, blk, swap):
        key = (kind, seg, blk, swap)
        if key not in cache:
            ref = (k_refs if kind == "k" else v_refs)[seg]
            x = ref[:, blk * LANE:(blk + 1) * LANE]
            if swap:
                x = pltpu.roll(x, HALF, 1)
            cache[key] = x.astype(BF16)
        return cache[key]

    def queries(members):
        qs = []
        for qblk, qhalf, _, _ in members:
            q = q_ref[:, qblk * LANE:(qblk + 1) * LANE]
            if fold_scale:
                q = q * scale
            if qhalf is not None:
                q = jnp.where(lo if qhalf == 0 else hi, q, 0.0)
            qs.append(q)
        return (jnp.concatenate(qs, axis=0) if len(qs) > 1 else qs[0]).astype(BF16)

    def scores(q, kblk, swap):
        out = [_dot_nt(q, block("k", s, kblk, swap)) for s in range(nseg)]
        return out if fold_scale else [s * scale for s in out]

    if nseg == 1:
        all_scores = [scores(queries(members), kblk, swap)[0] for members, kblk, swap, _ in plan]
        outs = _softmax_pv_phased(all_scores, [block("v", 0, vblk, swap) for _, _, swap, vblk in plan])
    else:
        outs = []
        ready = scores(queries(plan[0][0]), plan[0][1], plan[0][2])
        for g, (_, _, swap, vblk) in enumerate(plan):
            if g + 1 < len(plan):
                following = scores(queries(plan[g + 1][0]), plan[g + 1][1], plan[g + 1][2])
            outs.append(_softmax_pv(ready, [block("v", s, vblk, swap) for s in range(nseg)]))
            ready = following
    parts = {}
    for (members, _, _, _), out in zip(plan, outs):
        for n, (_, _, oblk, ohalf) in enumerate(members):
            parts[(oblk, ohalf)] = out[n * tq:(n + 1) * tq]
    for oblk in sorted({key[0] for key in parts}):
        o_ref[:, oblk * LANE:(oblk + 1) * LANE] = jnp.where(lo, parts[(oblk, 0)], parts[(oblk, 1)]).astype(BF16)


def _plan_gqa():
    per_kv = GQ_HEADS // GQ_KV_HEADS
    plan = []
    for g in range(GQ_KV_HEADS):
        for half in range(2):
            heads = [h for h in range(g * per_kv, (g + 1) * per_kv) if h % 2 == half]
            plan.append((tuple((h // 2, half, h // 2, half) for h in heads), 0, half != g, 0))
    return tuple(plan)


def _plan_na():
    return tuple((((j, 0, j, 0), (j, 1, j, 1)), j, False, j) for j in range(NA_HEADS // 2))


def _plan_mla():
    return tuple((((h, None, h // 2, h % 2),), h, False, h // 2) for h in range(ML_HEADS))


def _is_pow2(x):
    return float(np.log2(x)).is_integer()


def _attention(q, segs, plan, scale, batch, tq_total, name, tq=256):
    tq = min(tq_total, tq)
    per = tq_total // tq
    q_arr, q_w, q_col, q_off = q
    in_specs = [pl.BlockSpec((tq, q_w), lambda b, i: (q_off + b * per + i, q_col))]
    args = [q_arr]
    for k, v in segs:
        for arr, w, col, rows, stride, off in (k, v):
            in_specs.append(pl.BlockSpec(
                (rows, w), functools.partial(lambda b, i, col, stride, off: (off + b * stride, col),
                                             col=col, stride=stride, off=off)))
            args.append(arr)
    out_w = 512
    return pl.pallas_call(
        functools.partial(_attn_kernel, plan=plan, nseg=len(segs), scale=scale, fold_scale=_is_pow2(scale)),
        grid=(batch, per),
        in_specs=in_specs,
        out_specs=pl.BlockSpec((tq, out_w), lambda b, i: (b * per + i, 0)),
        out_shape=jax.ShapeDtypeStruct((M_ROWS, out_w), BF16),
        compiler_params=_params("parallel", "parallel"),
        name=name,
    )(*args)


NA_DR = 2 * NA_WIN_R - 1
NA_DC = 2 * NA_WIN_C - 1
NA_ROWS = DEC_SEQ // GRID_W
NA_QROWS = 4
NA_WROWS = 12
NA_T_RIGHT = NA_DR - 1
NA_T_LEFT = NA_DR
NA_T_NONE = NA_DR + 1
NA_T_SIZE = NA_DR + 2
assert _is_pow2(NA_SCALE)


def _na_kernel(q_ref, kl_ref, vl_ref, kc_ref, vc_ref, tz_ref, o_ref, kc_scr, vc_scr):
    g = pl.program_id(1)

    @pl.when(g == 0)
    def _():
        kc_scr[...] = kc_ref[...].astype(BF16)
        vc_scr[...] = vc_ref[...].astype(BF16)

    w0 = (g // 2) * (NA_ROWS - NA_WROWS)
    k0 = pl.multiple_of(w0 * GRID_W, (NA_ROWS - NA_WROWS) * GRID_W)
    win = NA_WROWS * GRID_W
    nq = NA_QROWS * GRID_W
    entry = []
    for a in range(NA_QROWS):
        rq = g * NA_QROWS + a
        start = jnp.clip(rq - NA_WIN_R // 2, 0, NA_ROWS - NA_WIN_R)
        per_pair = []
        for i in range(NA_WROWS // 2):
            kr = w0 + 2 * i
            in_l = jnp.logical_and(kr >= start, kr < start + NA_WIN_R)
            in_r = jnp.logical_and(kr + 1 >= start, kr + 1 < start + NA_WIN_R)
            both = jnp.logical_and(in_l, in_r)
            d_l = kr - rq + NA_WIN_R - 1
            per_pair.append(jnp.where(both, d_l, jnp.where(in_r, NA_T_RIGHT, jnp.where(in_l, NA_T_LEFT, NA_T_NONE))))
        entry.append(per_pair)
    lo, hi = _lane_halves()
    def scores(j):
        sl = slice(j * LANE, (j + 1) * LANE)
        k_loc = kl_ref[pl.ds(k0, win), sl].astype(BF16)
        k_ctx_t = kc_scr[sl, :]
        q_pair = q_ref[:, sl] * NA_SCALE
        q = jnp.concatenate([jnp.where(lo, q_pair, 0.0), jnp.where(hi, q_pair, 0.0)], axis=0).astype(BF16)
        bias = jnp.concatenate(
            [jnp.concatenate([tz_ref[2 * j + p, e] for e in entry[a]], axis=1)
             for p in range(2) for a in range(NA_QROWS)], axis=0)
        return [_dot_nt(q, k_loc) + bias, _dot(q, k_ctx_t)]

    pairs = NA_HEADS // 2
    ready = scores(0)
    for j in range(pairs):
        sl = slice(j * LANE, (j + 1) * LANE)
        if j + 1 < pairs:
            following = scores(j + 1)
        v_loc = vl_ref[pl.ds(k0, win), sl].astype(BF16)
        out = _softmax_pv(ready, [v_loc, vc_scr[sl, :]], [False, True])
        o_ref[:, sl] = jnp.where(lo, out[:nq], out[nq:]).astype(BF16)
        ready = following


def _na_bias_kernel(rpb_ref, sel_ref, neg_ref, o_ref):
    o_ref[...] = _dot_exact_rhs(rpb_ref[...], sel_ref[...]) + neg_ref[...]


def _na_bias_tables(rpb):
    col = np.arange(GRID_W)
    col_start = np.clip(col - NA_WIN_C // 2, 0, GRID_W - NA_WIN_C)
    col_ok = (col[None, :] >= col_start[:, None]) & (col[None, :] < col_start[:, None] + NA_WIN_C)
    dc = col[None, :] - col[:, None] + NA_WIN_C - 1
    kpad = 32
    sel = (dc[None, :, :] == np.arange(kpad)[:, None, None]) & col_ok[None]
    sel = jnp.asarray(sel.reshape(kpad, GRID_W * GRID_W), BF16)
    neg = jnp.asarray(np.where(col_ok, 0.0, NEG).reshape(1, GRID_W * GRID_W), F32)
    rpb2 = jnp.pad(rpb.reshape(NA_HEADS * NA_DR, NA_DC), ((0, 0), (0, kpad - NA_DC)))
    n_rows = NA_HEADS * NA_DR
    full = lambda shape: pl.BlockSpec(shape, lambda i: (0, 0))
    t = pl.pallas_call(
        _na_bias_kernel,
        grid=(1,),
        in_specs=[full((n_rows, kpad)), full((kpad, GRID_W * GRID_W)), full((1, GRID_W * GRID_W))],
        out_specs=full((n_rows, GRID_W * GRID_W)),
        out_shape=jax.ShapeDtypeStruct((n_rows, GRID_W * GRID_W), F32),
        compiler_params=_params("arbitrary"),
        name="na_bias",
    )(rpb2, sel, neg)
    t = t.reshape(NA_HEADS, NA_DR, GRID_W, GRID_W)
    masked = jnp.full((NA_HEADS, 1, GRID_W, GRID_W), NEG, F32)
    first, last = NA_WIN_R // 2 - 1, NA_WIN_R // 2 + NA_WIN_R - 2
    return jnp.concatenate([
        jnp.concatenate([t[:, :-1], t[:, 1:]], axis=-1),
        jnp.concatenate([masked, t[:, first:first + 1]], axis=-1),
        jnp.concatenate([t[:, last:last + 1], masked], axis=-1),
        jnp.concatenate([masked, masked], axis=-1)], axis=1)


def _na_latent(ymix, cache_k, cache_v, tz, layer):
    groups = NA_ROWS // NA_QROWS
    nq = NA_QROWS * GRID_W
    return pl.pallas_call(
        _na_kernel,
        grid=(DEC_BATCH, groups),
        in_specs=[
            pl.BlockSpec((nq, 512), lambda b, g: (b * groups + g, COL_NQ)),
            pl.BlockSpec((DEC_SEQ, 512), lambda b, g: (b, COL_NK)),
            pl.BlockSpec((DEC_SEQ, 512), lambda b, g: (b, COL_NV)),
            pl.BlockSpec((NA_HEADS * NA_HD, PAST_LEN), lambda b, g: (b * DEPTH + layer, 0)),
            pl.BlockSpec((NA_HEADS * NA_HD, PAST_LEN), lambda b, g: (b * DEPTH + layer, 0)),
            pl.BlockSpec((NA_HEADS, NA_T_SIZE, GRID_W, LANE), lambda b, g: (0, 0, 0, 0)),
        ],
        out_specs=pl.BlockSpec((nq, 512), lambda b, g: (b * groups + g, 0)),
        out_shape=jax.ShapeDtypeStruct((M_ROWS, 512), BF16),
        scratch_shapes=[pltpu.VMEM((NA_HEADS * NA_HD, PAST_LEN), BF16)] * 2,
        compiler_params=_params("parallel", "arbitrary"),
        name="na_lat",
    )(ymix, ymix, ymix, cache_k, cache_v, tz)


def _na_cache_out_kernel(*refs):
    srcs, (ko_ref, vo_ref) = refs[:2 * DEPTH], refs[2 * DEPTH:]
    layer = pl.program_id(1)
    for l in range(DEPTH):
        @pl.when(layer == l)
        def _():
            ko_ref[...] = srcs[2 * l][...].T
            vo_ref[...] = srcs[2 * l + 1][...].T


def _na_cache_out(ymix_layers):
    rows = NA_HEADS * NA_HD
    in_specs, args = [], []
    for ymix in ymix_layers:
        for col in (COL_NK, COL_NV):
            in_specs.append(pl.BlockSpec((SEQ, 512), functools.partial(lambda b, l, col: (b, col), col=col)))
            args.append(ymix)
    out = pl.pallas_call(
        _na_cache_out_kernel,
        grid=(BATCH, DEPTH),
        in_specs=in_specs,
        out_specs=[pl.BlockSpec((rows, SEQ), lambda b, l: (b * DEPTH + l, 0))] * 2,
        out_shape=[jax.ShapeDtypeStruct((BATCH * DEPTH * rows, SEQ), F32)] * 2,
        compiler_params=_params("parallel", "arbitrary"),
        name="na_cache_out",
    )(*args)
    return tuple(o.reshape(BATCH, DEPTH, NA_HEADS, NA_HD, SEQ).transpose(0, 1, 4, 2, 3) for o in out)


HG_BLOCK = 64
HG_STATE_UNROLL = 4
HG_DEC_ROWS = 8


def _hgrn_kernel(*refs, seq, has_state):
    if has_state:
        (q_ref, ff_ref, fb_ref, v_ref, g_ref, lb_ref, gain_ref, s0_ref,
         o_ref, of_ref, ob_ref, qin_ref, kend_ref, dec_ref, st_ref) = refs
    else:
        (q_ref, ff_ref, fb_ref, v_ref, g_ref, lb_ref, gain_ref,
         o_ref, sout_ref, of_ref, ob_ref, qin_ref, kend_ref, dec_ref, st_ref) = refs
    c = HG_CHUNK
    hc = c // 2
    rb = HG_BLOCK
    per_block = rb // c
    n_chunks = seq // c
    width = HG_HEADS * HG_DK
    sub = HG_DEC_ROWS

    ri = lax.broadcasted_iota(jnp.int32, (rb, rb), 0)
    ci = lax.broadcasted_iota(jnp.int32, (rb, rb), 1)
    same = (ri // c) == (ci // c)
    tri_f = jnp.where(jnp.logical_and(same, ci <= ri), 1.0, 0.0).astype(BF16)
    tri_b = jnp.where(jnp.logical_and(same, ci >= ri), 1.0, 0.0).astype(BF16)
    ones = jnp.ones((HG_DK, HG_DV), BF16)
    rowid = lax.broadcasted_iota(jnp.int32, (c, HG_DV), 0)
    laneid = lax.broadcasted_iota(jnp.int32, (c, HG_DV), 1)
    laneid_half = lax.broadcasted_iota(jnp.int32, (hc, HG_DV), 1)
    o_refs = (of_ref, ob_ref)

    n_blocks = seq // rb

    def block_of(d, step):
        return step if d == 0 else n_blocks - 1 - step

    heads = [slice(h * HG_DK, (h + 1) * HG_DK) for h in range(HG_HEADS)]
    per_chunk = c * hc
    dirs = (0, 1)

    def pair_block(step):
        fwd = [True, False]
        pre_refs = (ff_ref, fb_ref)
        tris = (tri_f, tri_b)
        r0 = [_aligned(block_of(d, step) * rb, rb) for d in dirs]
        near_rows = [range(0, hc), range(hc, c)]
        far_rows = [range(hc, c), range(0, hc)]
        near = [slice(0, hc), slice(hc, c)]
        far = [slice(hc, c), slice(0, hc)]
        edge = [hc - 1, hc]
        q_all = [q_ref[pl.ds(r0[d], rb), :] * (HG_DK ** -0.5) for d in dirs]
        v16_all = [v_ref[pl.ds(r0[d], rb), :].astype(BF16) for d in dirs]
        k_all, log_f = [], []
        for d in dirs:
            lb = lb_ref[d:d + 1, :]
            f = lb + (1.0 - lb) * jax.nn.sigmoid(pre_refs[d][pl.ds(r0[d], rb), :])
            k_all.append(1.0 - f)
            log_f.append(jnp.log(f))
        b_all = [_dot_exact_lhs(tris[d], log_f[d]) * LOG2E for d in dirs]
        src_all = [b_all[d] - jnp.log(k_all[d]) * LOG2E for d in dirs]
        pairs, q_edge, k_edge = [], {}, {}
        zero_half = jnp.zeros((hc, width), F32)
        for d in dirs:
            blocks, q_in, k_end = [], [], []
            for m in range(per_block):
                rows = slice(m * c, (m + 1) * c)
                q, k, b, src = q_all[d][rows], k_all[d][rows], b_all[d][rows], src_all[d][rows]
                b_last = b[c - 1:c] if fwd[d] else b[0:1]
                q_in.append(q * jnp.exp2(b))
                k_end.append(k * jnp.exp2(b_last - b))
                dec_row = _aligned((block_of(d, step) * per_block + m) * sub, sub)
                dec_ref[d, pl.ds(dec_row, sub), :] = jnp.broadcast_to(jnp.exp2(b_last), (sub, width))
                halves = []
                for rows_s, part in ((near_rows[d], near[d]), (far_rows[d], far[d])):
                    q_part, b_part = q[part], b[part]
                    halves += [q_part * jnp.exp2(b_part - src[s:s + 1]) for s in rows_s]
                for i in range(0, c, 2):
                    blocks.append(jnp.concatenate(halves[i:i + 2], axis=0).astype(BF16))
                b_edge = b[edge[d]:edge[d] + 1]
                q_far = q[far[d]] * jnp.exp2(b[far[d]] - b_edge)
                k_near = jnp.exp2(b_edge - src[near[d]])
                q_edge[d, m] = jnp.concatenate([zero_half, q_far] if fwd[d] else [q_far, zero_half], axis=0).astype(BF16)
                k_edge[d, m] = jnp.concatenate([k_near, zero_half] if fwd[d] else [zero_half, k_near], axis=0).astype(BF16)
            qin_ref[d, pl.ds(r0[d], rb), :] = jnp.concatenate(q_in, axis=0).astype(BF16)
            kend_ref[d, pl.ds(r0[d], rb), :] = jnp.concatenate(k_end, axis=0).astype(BF16)
            pairs.append(jnp.concatenate(blocks, axis=0))
        sums = [[_dot(pairs[d][:, sl], ones) for sl in heads] for d in dirs]
        cells = [(d, h, m) for d in dirs for h in range(HG_HEADS) for m in range(per_block)]
        k_pad = jnp.zeros((HG_DK - c, HG_DK), BF16)
        across = {(d, h, m): _dot_nt(q_edge[d, m][:, heads[h]],
                                     jnp.concatenate([k_edge[d, m][:, heads[h]], k_pad], axis=0))
                  for d, h, m in cells}
        near_attn = {cell: jnp.zeros((hc, HG_DV), F32) for cell in cells}
        far_attn = {cell: jnp.zeros((hc, HG_DV), F32) for cell in cells}
        for n in range(hc):
            for d, h, m in cells:
                base = m * per_chunk + n * hc
                near_attn[d, h, m] = jnp.where(laneid_half == near_rows[d][n], sums[d][h][base:base + hc],
                                               near_attn[d, h, m])
                base += hc * hc
                far_attn[d, h, m] = jnp.where(laneid_half == far_rows[d][n], sums[d][h][base:base + hc],
                                              far_attn[d, h, m])
        v_pad = jnp.zeros((HG_DK - c, HG_DV), BF16)
        causal = [laneid <= rowid, laneid >= rowid]
        prods = {}
        for d, h, m in cells:
            halves = [near_attn[d, h, m], far_attn[d, h, m]] if fwd[d] else [far_attn[d, h, m], near_attn[d, h, m]]
            inside = jnp.where(causal[d], jnp.concatenate(halves, axis=0), 0.0)
            values = jnp.concatenate([v16_all[d][m * c:(m + 1) * c, heads[h]], v_pad], axis=0)
            prods[d, h, m] = _dot((inside + across[d, h, m]).astype(BF16), values)
        for d in dirs:
            o_refs[d][pl.ds(r0[d], rb), :] = jnp.concatenate(
                [jnp.concatenate([prods[d, h, m] for m in range(per_block)], axis=0) for h in range(HG_HEADS)], axis=1)


    def state_block(step, carry):
        order = [list(range(per_block)), list(reversed(range(per_block)))]
        r0 = [_aligned(block_of(d, step) * rb, rb) for d in dirs]
        q_in = [qin_ref[d, pl.ds(r0[d], rb), :] for d in dirs]
        k_end = [kend_ref[d, pl.ds(r0[d], rb), :] for d in dirs]
        v16 = [v_ref[pl.ds(r0[d], rb), :].astype(BF16) for d in dirs]
        rows = [slice(m * c, (m + 1) * c) for m in range(per_block)]
        q_state, k_state, k_cross, v_cross, total = [], [], [], [], []
        for d in dirs:
            dec_row = _aligned(block_of(d, step) * per_block * sub, sub)
            dec_all = dec_ref[d, pl.ds(dec_row, per_block * sub), :]
            dec = [dec_all[m * sub:m * sub + 1] for m in order[d]]
            before = [None] * per_block
            after = [None] * per_block
            for j in range(1, per_block):
                before[j] = dec[j - 1] if before[j - 1] is None else before[j - 1] * dec[j - 1]
            for j in range(per_block - 2, -1, -1):
                after[j] = dec[j + 1] if after[j + 1] is None else after[j + 1] * dec[j + 1]
            total.append(before[-1] * dec[-1])
            scale = lambda x, f: x if f is None else x * f
            qs, ks = [None] * per_block, [None] * per_block
            for j, m in enumerate(order[d]):
                qs[m] = scale(q_in[d][rows[m]], before[j])
                ks[m] = scale(k_end[d][rows[m]], after[j])
            q_state.append(jnp.concatenate(qs, axis=0).astype(BF16))
            k_state.append(jnp.concatenate(ks, axis=0).astype(BF16))
            kc, vc = [None], [None]
            for j in range(1, per_block):
                keys, between = [], None
                for i in range(j - 1, -1, -1):
                    keys.insert(0, scale(k_end[d][rows[order[d][i]]], between))
                    between = dec[i] if between is None else between * dec[i]
                kc.append(jnp.concatenate(keys, axis=0).astype(BF16))
                vc.append(jnp.concatenate([v16[d][rows[order[d][i]]] for i in range(j)], axis=0))
            k_cross.append(kc)
            v_cross.append(vc)
        q16 = [q_in[d].astype(BF16) for d in dirs]
        cells = [(d, h) for d in dirs for h in range(HG_HEADS)]
        kv = {(d, h): _dot_tn(v16[d][:, heads[h]], k_state[d][:, heads[h]]) for d, h in cells}
        attn = {(d, h, j): _dot_nt(q16[d][rows[order[d][j]], heads[h]], k_cross[d][j][:, heads[h]])
                for d, h in cells for j in range(1, per_block)}
        st = {(d, h): st_ref[d, h] for d, h in cells}
        from_state = {(d, h): _dot_nt(q_state[d][:, heads[h]], st[d, h].astype(BF16)) for d, h in cells}
        cross = {key: _dot(a.astype(BF16), v_cross[key[0]][key[2]][:, heads[key[1]]]) for key, a in attn.items()}
        for d, h in cells:
            st_ref[d, h] = st[d, h] * total[d][:, heads[h]] + kv[d, h]
        for d in dirs:
            cols = []
            for h in range(HG_HEADS):
                parts = [None] * per_block
                for j, m in enumerate(order[d]):
                    part = from_state[d, h][rows[m]]
                    parts[m] = part if j == 0 else part + cross[d, h, j]
                cols.append(jnp.concatenate(parts, axis=0))
            o_refs[d][pl.ds(r0[d], rb), :] += jnp.concatenate(cols, axis=1)
        return carry

    if has_state:
        for d in range(2):
            for h in range(HG_HEADS):
                st_ref[d, h] = s0_ref[d, h].T
    else:
        st_ref[...] = jnp.zeros(st_ref.shape, F32)

    def pair_step(i, carry):
        pair_block(i)
        return carry

    lax.fori_loop(0, n_blocks, pair_step, 0)
    lax.fori_loop(0, n_blocks, state_block, 0, unroll=HG_STATE_UNROLL)

    for h in range(HG_HEADS):
        sl = slice(h * HG_DV, (h + 1) * HG_DV)
        o = of_ref[:, sl] + ob_ref[:, sl]
        o_ref[:, sl] = (_rms(o) * gain_ref[:, sl] * _silu(g_ref[:, sl])).astype(BF16)
    if not has_state:
        for d in range(2):
            for h in range(HG_HEADS):
                sout_ref[d, h] = st_ref[d, h].T


def _hgrn(ymix, lb, gain, batch, seq, state, layer):
    width = HG_HEADS * HG_DK
    has_state = state is not None
    assert HG_DK == HG_DV
    st_shape = (2, HG_HEADS, HG_DK, HG_DV)
    in_specs = [pl.BlockSpec((seq, width), functools.partial(lambda b, col: (b, col), col=col))
                for col in (COL_HQ, COL_HFF, COL_HFB, COL_HI, COL_HG)]
    in_specs += [pl.BlockSpec((2, width), lambda b: (0, 0)), pl.BlockSpec((1, width), lambda b: (0, 0))]
    args = [ymix] * 5 + [lb, gain]
    out_specs = [pl.BlockSpec((seq, width), lambda b: (b, 0))]
    out_shape = [jax.ShapeDtypeStruct((M_ROWS, width), BF16)]
    if has_state:
        in_specs.append(pl.BlockSpec((None, None) + st_shape, lambda b: (b, layer, 0, 0, 0, 0)))
        args.append(state)
    else:
        out_specs.append(pl.BlockSpec((None,) + st_shape, lambda b: (b, 0, 0, 0, 0)))
        out_shape.append(jax.ShapeDtypeStruct((batch,) + st_shape, F32))
    return pl.pallas_call(
        functools.partial(_hgrn_kernel, seq=seq, has_state=has_state),
        grid=(batch,),
        in_specs=in_specs,
        out_specs=out_specs,
        out_shape=out_shape,
        scratch_shapes=[pltpu.VMEM((seq, width), F32), pltpu.VMEM((seq, width), F32),
                        pltpu.VMEM((2, seq, width), BF16), pltpu.VMEM((2, seq, width), BF16),
                        pltpu.VMEM((2, seq // HG_CHUNK * HG_DEC_ROWS, width), F32),
                        pltpu.VMEM(st_shape, F32)],
        compiler_params=_params("parallel"),
        name="hgrn_lat" if has_state else "hgrn_ctx",
    )(*args)


def _merge_kernel(oa_ref, ob_ref, oc_ref, od_ref, gt_ref, wb_ref, wo_ref, x_ref, g1_ref, out_ref):
    half = out_ref.shape[0] // 2
    rows = [pl.ds(0, half), pl.ds(half, half)]
    projected = [[_dot(o_ref[r, :], wb_ref[n]) for n, o_ref in enumerate((oa_ref, ob_ref, oc_ref, od_ref))]
                 for r in rows]
    for r, branch in zip(rows, projected):
        acc = None
        for n, bo in enumerate(branch):
            term = gt_ref[r, n * D_MODEL:(n + 1) * D_MODEL] * bo
            acc = term if acc is None else acc + term
        out_ref[r, :] = x_ref[r, :] + g1_ref[...] * _dot(acc.astype(BF16), wo_ref[...])


def _merge(branches, gates, w_branch, w_out, x, mod, latent):
    tm = 512
    row = _mod_row(latent, tm)
    tile = lambda w: pl.BlockSpec((tm, w), lambda i: (i, 0))
    return pl.pallas_call(
        _merge_kernel,
        grid=(M_ROWS // tm,),
        in_specs=[tile(BRANCH_W)] * N_BRANCH + [
            tile(GATE_W),
            pl.BlockSpec((N_BRANCH, BRANCH_W, D_MODEL), lambda i: (0, 0, 0)),
            pl.BlockSpec((D_MODEL, D_MODEL), lambda i: (0, 0)),
            tile(D_MODEL),
            pl.BlockSpec((None, None, 1, D_MODEL), lambda i: (row(i), 2, 0, 0)),
        ],
        out_specs=tile(D_MODEL),
        out_shape=jax.ShapeDtypeStruct((M_ROWS, D_MODEL), F32),
        compiler_params=_params("parallel"),
        name="merge_lat" if latent else "merge_ctx",
    )(*branches, gates, w_branch, w_out, x, mod)


FFN_CHUNK = 256
FFN_STEPS = FFN_HIDDEN // FFN_CHUNK
assert FFN_STEPS * FFN_CHUNK == FFN_HIDDEN


def _ffn_kernel(*refs, final):
    x_ref, sc_ref, sh_ref, g2_ref, wa_ref, wg_ref, wo_ref = refs[:7]
    if final:
        fg_ref, out_ref, h_ref, acc_ref = refs[7:]
    else:
        out_ref, h_ref, acc_ref = refs[7:]
    j = pl.program_id(1)

    @pl.when(j == 0)
    def _():
        h = _rms(x_ref[...]) * (1.0 + sc_ref[...]) + sh_ref[...]
        h_ref[...] = h.astype(BF16)
        acc_ref[...] = jnp.zeros(acc_ref.shape, F32)

    wg, wa, wo = wg_ref[...].astype(BF16), wa_ref[...].astype(BF16), wo_ref[...].astype(BF16)
    half = h_ref.shape[0] // 2
    rows = [pl.ds(0, half), pl.ds(half, half)]
    up = [(_dot(h_ref[r, :], wg), _dot(h_ref[r, :], wa)) for r in rows]
    for r, (gate, lin) in zip(rows, up):
        acc_ref[r, :] += _dot((_silu(gate) * lin).astype(BF16), wo)

    @pl.when(j == FFN_STEPS - 1)
    def _():
        y = x_ref[...] + g2_ref[...] * acc_ref[...]
        out_ref[...] = _rms(y) * fg_ref[...] if final else y


def _ffn(x, mod, w_in, w_out, layer, latent, final_gain):
    tm = 1024
    row = _mod_row(latent, tm)
    modspec = lambda which: pl.BlockSpec((None, None, 1, D_MODEL), lambda i, j: (row(i), which, 0, 0))
    final = final_gain is not None
    in_specs = [
        pl.BlockSpec((tm, D_MODEL), lambda i, j: (i, 0)),
        modspec(4), modspec(3), modspec(5),
        pl.BlockSpec((None, D_MODEL, FFN_CHUNK), lambda i, j: (layer, 0, j)),
        pl.BlockSpec((None, D_MODEL, FFN_CHUNK), lambda i, j: (layer, 0, FFN_STEPS + j)),
        pl.BlockSpec((None, FFN_CHUNK, D_MODEL), lambda i, j: (layer, j, 0)),
    ]
    args = [x, mod, mod, mod, w_in, w_in, w_out]
    if final:
        in_specs.append(pl.BlockSpec((1, D_MODEL), lambda i, j: (0, 0)))
        args.append(final_gain.reshape(1, D_MODEL))
    return pl.pallas_call(
        functools.partial(_ffn_kernel, final=final),
        grid=(M_ROWS // tm, FFN_STEPS),
        in_specs=in_specs,
        out_specs=pl.BlockSpec((tm, D_MODEL), lambda i, j: (i, 0)),
        out_shape=jax.ShapeDtypeStruct((M_ROWS, D_MODEL), F32),
        scratch_shapes=[pltpu.VMEM((tm, D_MODEL), BF16), pltpu.VMEM((tm, D_MODEL), F32)],
        compiler_params=_params("parallel", "arbitrary"),
        name="ffn_lat" if latent else "ffn_ctx",
    )(*args)


def _rope_tables():
    t = jnp.arange(DEC_SEQ)
    row = (t // GRID_W).astype(F32)[:, None]
    col = (t % GRID_W).astype(F32)[:, None]

    def angles(rot_dim):
        n_freq = rot_dim // 4
        inv_freq = ROPE_THETA ** (-jnp.arange(n_freq, dtype=F32) / n_freq)
        return jnp.concatenate([row * inv_freq, col * inv_freq], axis=-1)

    def expand(ang):
        cos = jnp.repeat(jnp.cos(ang), 2, axis=-1)
        sin = jnp.repeat(jnp.sin(ang), 2, axis=-1)
        even = (jnp.arange(cos.shape[-1]) % 2 == 0)[None, :]
        return cos, jnp.where(even, -sin, 0.0), jnp.where(even, 0.0, sin)

    gq = [jnp.concatenate([a, a], axis=-1) for a in expand(angles(GQ_HD))]
    ml = []
    for idx, a in enumerate(expand(angles(ML_ROPE))):
        fill = 1.0 if idx == 0 else 0.0
        ml.append(jnp.concatenate([a, jnp.full((DEC_SEQ, LANE - ML_ROPE), fill, F32)], axis=-1))
    return {"gq": jnp.stack(gq), "ml": jnp.stack(ml)}


def _layer_weights(l, gq_q_gain, gq_k_gain, ml_q_a_gain, ml_kv_a_gain, ml_w_q_b, ml_w_kv_b,
                   w_branch, w_out, hg_gain, avg):
    pad = LANE - ML_NOPE - ML_ROPE
    qb = ml_w_q_b[l].reshape(ML_Q_RANK, ML_HEADS, ML_NOPE + ML_ROPE)
    qb = jnp.concatenate([qb[:, :, ML_NOPE:], qb[:, :, :ML_NOPE], jnp.zeros((ML_Q_RANK, ML_HEADS, pad), F32)],
                         axis=-1).reshape(ML_Q_RANK, ML_HEADS * LANE)
    kvb = ml_w_kv_b[l].reshape(ML_KV_RANK, ML_HEADS, ML_NOPE + ML_V)
    wk = jnp.pad(kvb[:, :, :ML_NOPE], ((0, 0), (0, 0), (ML_ROPE, pad))).reshape(ML_KV_RANK, ML_HEADS * LANE)
    wv = kvb[:, :, ML_NOPE:].reshape(ML_KV_RANK, ML_HEADS * ML_V)
    return {
        "gq_q_gain": jnp.tile(gq_q_gain[l], GQ_HEADS).reshape(1, -1),
        "gq_k_gain": jnp.tile(gq_k_gain[l], GQ_KV_HEADS).reshape(1, -1),
        "ml_q_gain": ml_q_a_gain[l].reshape(1, -1), "ml_kv_gain": ml_kv_a_gain[l].reshape(1, -1),
        "w_qb": qb.astype(BF16), "w_k": wk.astype(BF16), "w_v": wv.astype(BF16),
        "w_branch": w_branch[l].astype(BF16), "w_out": w_out[l].astype(BF16),
        "hg_gain": jnp.tile(hg_gain[l], HG_HEADS).reshape(1, -1), "avg": avg,
    }


def _seg(arr, width, col, rows, stride, off):
    return (arr, width, col, rows, stride, off)


def kernel(x_prompt, x_sample, state_hgrn, cache_gqa_k, cache_gqa_v, cache_na_k, cache_na_v, cache_mla_ckv, cache_mla_krope, c, c_ctx, w_ada, b_ada, w_in, hg_lb_logits, hg_gain, gq_q_gain, gq_k_gain, na_rpb, ml_q_a_gain, ml_kv_a_gain, ml_w_q_b, ml_w_kv_b, w_branch, w_out, w_ffn_in, w_ffn_out, final_gain):
    cond8 = jnp.concatenate([c_ctx[None, :], c, jnp.zeros((8 - 1 - DEC_BATCH, D_MODEL), F32)], axis=0)
    mods = _ada(cond8, w_ada, b_ada)

    lb = jnp.cumsum(jax.nn.softmax(hg_lb_logits.astype(F32), axis=0), axis=0)
    lb = lb - lb[:1]
    avg = jnp.asarray(np.kron(np.eye(512 // GQ_HD), np.full((GQ_HD, GQ_HD), 1.0 / GQ_HD)), BF16)
    tabs = _rope_tables()
    w_in_t = jnp.swapaxes(w_in, 1, 2).reshape(DEPTH * w_in.shape[2], D_MODEL)

    gqk_c = cache_gqa_k.reshape(DEC_BATCH * DEPTH * PAST_LEN, GQ_KV_HEADS * GQ_HD)
    gqv_c = cache_gqa_v.reshape(DEC_BATCH * DEPTH * PAST_LEN, GQ_KV_HEADS * GQ_HD)
    nak_c = cache_na_k.transpose(0, 1, 3, 4, 2).reshape(DEC_BATCH * DEPTH * NA_HEADS * NA_HD, PAST_LEN)
    nav_c = cache_na_v.transpose(0, 1, 3, 4, 2).reshape(DEC_BATCH * DEPTH * NA_HEADS * NA_HD, PAST_LEN)
    mckv_c = cache_mla_ckv.reshape(DEC_BATCH * DEPTH * PAST_LEN, ML_KV_RANK)
    mkr_c = jnp.pad(cache_mla_krope.reshape(DEC_BATCH * DEPTH * PAST_LEN, ML_ROPE),
                    ((0, 0), (0, LANE - ML_ROPE)))

    xp = x_prompt.reshape(M_ROWS, D_MODEL)
    xs = x_sample.reshape(M_ROWS, D_MODEL)
    new = []
    ymix_ctx = []
    for l in range(DEPTH):
        lw = _layer_weights(l, gq_q_gain, gq_k_gain, ml_q_a_gain, ml_kv_a_gain, ml_w_q_b,
                            ml_w_kv_b, w_branch, w_out, hg_gain, avg)
        last = l == DEPTH - 1
        mod = mods[l]

        ymix = _inproj(xp, mod, w_in_t, l, False, gate=False)
        gates = _inproj(xp, mod, w_in_t, l, False, gate=True)
        qb, kb, qd, ckv, kd, vd = _prep(ymix, lw, tabs, False)
        out_a, st = _hgrn(ymix, lb[l], lw["hg_gain"], BATCH, SEQ, None, l)
        out_b = _attention((qb, 512, 0, 0),
                           [(_seg(kb, LANE, 0, SEQ, 1, 0), _seg(ymix, LANE, COL_GV, SEQ, 1, 0))],
                           _plan_gqa(), GQ_SCALE, BATCH, SEQ, "gqa_ctx")
        out_c = _attention((ymix, 512, COL_NQ, 0),
                           [(_seg(ymix, 512, COL_NK, SEQ, 1, 0), _seg(ymix, 512, COL_NV, SEQ, 1, 0))],
                           _plan_na(), NA_SCALE, BATCH, SEQ, "na_ctx")
        out_d = _attention((qd, ML_HEADS * LANE, 0, 0),
                           [(_seg(kd, ML_HEADS * LANE, 0, SEQ, 1, 0), _seg(vd, 512, 0, SEQ, 1, 0))],
                           _plan_mla(), ML_SCALE, BATCH, SEQ, "mla_ctx")
        xp = _merge((out_a, out_b, out_c, out_d), gates, lw["w_branch"], lw["w_out"], xp, mod, False)
        xp = _ffn(xp, mod, w_ffn_in, w_ffn_out, l, False, final_gain if last else None)
        new.append((
            st,
            kb.reshape(BATCH, SEQ, GQ_KV_HEADS, GQ_HD),
            ymix[:, COL_GV * LANE:(COL_GV + 1) * LANE].reshape(BATCH, SEQ, GQ_KV_HEADS, GQ_HD),
            ckv.reshape(BATCH, SEQ, ML_KV_RANK),
            ymix[:, COL_MKR * LANE:COL_MKR * LANE + ML_ROPE].reshape(BATCH, SEQ, ML_ROPE),
        ))
        ymix_ctx.append(ymix)

        ymix = _inproj(xs, mod, w_in_t, l, True, gate=False)
        gates = _inproj(xs, mod, w_in_t, l, True, gate=True)
        qb, kb, qd, ckv, kd, vd = _prep(ymix, lw, tabs, True)
        kd_c, vd_c = _mla_cache(mckv_c, mkr_c, lw, l)
        out_a, = _hgrn(ymix, lb[l], lw["hg_gain"], DEC_BATCH, DEC_SEQ, state_hgrn, l)
        out_b = _attention((qb, 512, 0, 0),
                           [(_seg(gqk_c, LANE, 0, PAST_LEN, DEPTH, l), _seg(gqv_c, LANE, 0, PAST_LEN, DEPTH, l)),
                            (_seg(kb, LANE, 0, DEC_SEQ, 1, 0), _seg(ymix, LANE, COL_GV, DEC_SEQ, 1, 0))],
                           _plan_gqa(), GQ_SCALE, DEC_BATCH, DEC_SEQ, "gqa_lat")
        out_c = _na_latent(ymix, nak_c, nav_c, _na_bias_tables(na_rpb[l]), l)
        out_d = _attention((qd, ML_HEADS * LANE, 0, 0),
                           [(_seg(kd_c, ML_HEADS * LANE, 0, PAST_LEN, 1, 0), _seg(vd_c, 512, 0, PAST_LEN, 1, 0)),
                            (_seg(kd, ML_HEADS * LANE, 0, DEC_SEQ, 1, 0), _seg(vd, 512, 0, DEC_SEQ, 1, 0))],
                           _plan_mla(), ML_SCALE, DEC_BATCH, DEC_SEQ, "mla_lat", tq=512)
        xs = _merge((out_a, out_b, out_c, out_d), gates, lw["w_branch"], lw["w_out"], xs, mod, True)
        xs = _ffn(xs, mod, w_ffn_in, w_ffn_out, l, True, final_gain if last else None)

    y_prompt = xp.reshape(BATCH, SEQ, D_MODEL)
    y_sample = xs.reshape(DEC_BATCH, DEC_SEQ, D_MODEL)
    state, gqa_k, gqa_v, ckv_new, krope_new = (jnp.stack([n[i] for n in new], axis=1) for i in range(5))
    na_k, na_v = _na_cache_out(ymix_ctx)
    return (y_prompt, y_sample, state, gqa_k, gqa_v, na_k, na_v, ckv_new, krope_new)
```

```python
import functools

import numpy as np
import jax
import jax.numpy as jnp
from jax import lax
from jax.experimental import pallas as pl
from jax.experimental.pallas import tpu as pltpu

F32 = jnp.float32
BF16 = jnp.bfloat16

D_MODEL = 1024
BATCH = 16
SEQ = 256
DEPTH = 2
DEC_BATCH = 4
DEC_SEQ = 1024
PAST_LEN = 512
GRID_W = 64
EPS = 1e-6
ROPE_THETA = 10000.0
N_BRANCH = 4
BRANCH_W = 512
HG_HEADS = 4
HG_DK = 128
HG_DV = 128
GQ_HEADS = 8
GQ_KV_HEADS = 2
GQ_HD = 64
NA_HEADS = 8
NA_HD = 64
NA_WIN_R = 8
NA_WIN_C = 16
ML_HEADS = 8
ML_NOPE = 64
ML_ROPE = 32
ML_V = 64
ML_Q_RANK = 256
ML_KV_RANK = 128
FFN_HIDDEN = 2816
GQ_SCALE = GQ_HD ** -0.5
NA_SCALE = NA_HD ** -0.5
ML_SCALE = (ML_NOPE + ML_ROPE) ** -0.5

M_ROWS = BATCH * SEQ
assert M_ROWS == DEC_BATCH * DEC_SEQ

LANE = 128
HALF = 64
MIX_W = 5376
IN_MIX = 5280
GATE_W = N_BRANCH * D_MODEL
HG_CHUNK = 16
NEG = -1e30
LOG2E = 1.4426950408889634
VMEM_LIMIT = 56 * 1024 * 1024

COL_HQ, COL_HFF, COL_HFB, COL_HI, COL_HG, COL_GQ, COL_NQ, COL_NK, COL_NV = range(9)
COL_MQA = 18
COL_GK, COL_GV, COL_MKVA, COL_MKR = 38, 39, 40, 41


def _dot(a, b):
    return jnp.dot(a, b, preferred_element_type=F32)


def _dot_nt(a, b):
    return lax.dot_general(a, b, (((1,), (1,)), ((), ())), preferred_element_type=F32)


def _dot_tn(a, b):
    return lax.dot_general(a, b, (((0,), (0,)), ((), ())), preferred_element_type=F32)


def _split3(x):
    x1 = x.astype(BF16)
    r1 = x - x1.astype(F32)
    x2 = r1.astype(BF16)
    x3 = (r1 - x2.astype(F32)).astype(BF16)
    return x1, x2, x3


def _dot_exact_lhs(a_bf16, x):
    x1, x2, x3 = _split3(x)
    return (_dot(a_bf16, x3) + _dot(a_bf16, x2)) + _dot(a_bf16, x1)


def _dot_exact_rhs(x, b_bf16):
    x1, x2, x3 = _split3(x)
    return (_dot(x3, b_bf16) + _dot(x2, b_bf16)) + _dot(x1, b_bf16)


def _rms(x):
    return x * lax.rsqrt(jnp.mean(x * x, axis=-1, keepdims=True) + EPS)


def _silu(x):
    return x * jax.nn.sigmoid(x)


def _aligned(x, m):
    return x if isinstance(x, int) else pl.multiple_of(x, m)


def _params(*sem):
    return pltpu.CompilerParams(dimension_semantics=sem, vmem_limit_bytes=VMEM_LIMIT)


def _mod_row(latent, tm):
    if latent:
        return lambda i: 1 + (i * tm) // DEC_SEQ
    return lambda i: 0


def _ada_kernel(c_ref, w_ref, b_ref, o_ref):
    c = c_ref[...]
    o_ref[...] = _dot(_silu(c).astype(BF16), w_ref[...].astype(BF16)) + b_ref[...]


def _ada(cond8, w_ada, b_ada):
    tn = 1536
    out = pl.pallas_call(
        _ada_kernel,
        grid=(DEPTH, 6 * D_MODEL // tn),
        in_specs=[
            pl.BlockSpec((8, D_MODEL), lambda l, j: (0, 0)),
            pl.BlockSpec((None, D_MODEL, tn), lambda l, j: (l, 0, j)),
            pl.BlockSpec((None, 1, tn), lambda l, j: (l, 0, j)),
        ],
        out_specs=pl.BlockSpec((None, 8, tn), lambda l, j: (l, 0, j)),
        out_shape=jax.ShapeDtypeStruct((DEPTH, 8, 6 * D_MODEL), F32),
        compiler_params=_params("parallel", "parallel"),
        name="ada",
    )(cond8, w_ada, b_ada.reshape(DEPTH, 1, 6 * D_MODEL))
    return out.reshape(DEPTH, 8, 6, 1, D_MODEL)


IN_PIECE = 256
IN_TM = 1024


def _mix_source(t):
    blk = jnp.where(t < 12, t, jnp.where(t < 18, t + 1, jnp.where(t == 18, 19, jnp.where(t == 19, 12, 20))))
    return blk * IN_PIECE


def _gate_source(t):
    return IN_MIX + t * IN_PIECE


def _inproj_kernel(*refs, pieces, gate):
    x_ref, sc_ref, sh_ref = refs[:3]
    w_refs = refs[3:3 + pieces]
    o_ref, h_ref = refs[3 + pieces:]
    rows = pl.ds(pl.multiple_of(pl.program_id(1) * IN_TM, IN_TM), IN_TM)

    @pl.when(pl.program_id(0) == 0)
    def _():
        h = _rms(x_ref[...]) * (1.0 + sc_ref[...]) + sh_ref[...]
        h_ref[rows, :] = h.astype(BF16)

    h = h_ref[rows, :]
    for p, w_ref in enumerate(w_refs):
        y = _dot_nt(h, w_ref[...].astype(BF16))
        o_ref[:, p * IN_PIECE:(p + 1) * IN_PIECE] = jax.nn.sigmoid(y).astype(o_ref.dtype) if gate else y


def _inproj(x, mod, w_in_t, layer, latent, gate):
    width, tile, source = (GATE_W, 2048, _gate_source) if gate else (MIX_W, 1792, _mix_source)
    pieces = tile // IN_PIECE
    row = _mod_row(latent, IN_TM)
    first = lambda c, i: jnp.where(c == 0, i, 0)
    in_width = w_in_t.shape[0] // DEPTH
    piece = lambda p: pl.BlockSpec((pl.Element(IN_PIECE), pl.Element(D_MODEL)),
                                   lambda c, i: (pl.multiple_of(layer * in_width + source(c * pieces + p), 8), 0))
    return pl.pallas_call(
        functools.partial(_inproj_kernel, pieces=pieces, gate=gate),
        grid=(width // tile, M_ROWS // IN_TM),
        in_specs=[
            pl.BlockSpec((IN_TM, D_MODEL), lambda c, i: (first(c, i), 0)),
            pl.BlockSpec((None, None, 1, D_MODEL), lambda c, i: (row(first(c, i)), 1, 0, 0)),
            pl.BlockSpec((None, None, 1, D_MODEL), lambda c, i: (row(first(c, i)), 0, 0, 0)),
        ] + [piece(p) for p in range(pieces)],
        out_specs=pl.BlockSpec((IN_TM, tile), lambda c, i: (i, c)),
        out_shape=jax.ShapeDtypeStruct((M_ROWS, width), BF16 if gate else F32),
        scratch_shapes=[pltpu.VMEM((M_ROWS, D_MODEL), BF16)],
        compiler_params=_params("arbitrary", "arbitrary"),
        name="inproj_" + ("gate_" if gate else "mix_") + ("lat" if latent else "ctx"),
    )(x, mod, mod, *([w_in_t] * pieces))


def _rope(x, tab_ref, reps):
    w = x.shape[1]
    c = jnp.concatenate([tab_ref[0]] * reps, axis=1) if reps > 1 else tab_ref[0]
    se = jnp.concatenate([tab_ref[1]] * reps, axis=1) if reps > 1 else tab_ref[1]
    so = jnp.concatenate([tab_ref[2]] * reps, axis=1) if reps > 1 else tab_ref[2]
    return x * c + pltpu.roll(x, w - 1, 1) * se + pltpu.roll(x, 1, 1) * so


def _head_rms(x, gain, avg_bf16):
    ms = _dot_exact_rhs(x * x, avg_bf16)
    return x * lax.rsqrt(ms + EPS) * gain


def _prep_kernel(*refs, rope):
    (gq_ref, gk_ref, mqa_ref, mkva_ref, mkr_ref, gqg_ref, gkg_ref, mqg_ref, mkvg_ref,
     wqb_ref, wk_ref, wv_ref, avg_ref) = refs[:13]
    if rope:
        gtab_ref, mtab_ref = refs[13:15]
        outs = refs[15:]
    else:
        outs = refs[13:]
    qb_ref, kb_ref, qd_ref, ckv_ref, kd_ref, vd_ref = outs

    q = _head_rms(gq_ref[...], gqg_ref[...], avg_ref[...])
    k = _head_rms(gk_ref[...], gkg_ref[...], avg_ref[0:LANE, 0:LANE])
    qd = _dot((_rms(mqa_ref[...]) * mqg_ref[...]).astype(BF16), wqb_ref[...])
    ckv = _rms(mkva_ref[...]) * mkvg_ref[...]
    lane = lax.broadcasted_iota(jnp.int32, (1, LANE), 1)
    kr = jnp.where(lane < ML_ROPE, mkr_ref[...], 0.0)
    if rope:
        q = _rope(q, gtab_ref, GQ_HEADS * GQ_HD // LANE)
        k = _rope(k, gtab_ref, 1)
        qd = _rope(qd, mtab_ref, ML_HEADS)
        kr = _rope(kr, mtab_ref, 1)
    qb_ref[...] = q.astype(BF16)
    kb_ref[...] = k
    qd_ref[...] = qd.astype(BF16)
    ckv_ref[...] = ckv
    cb = ckv.astype(BF16)
    kd_ref[...] = (_dot(cb, wk_ref[...]) + jnp.concatenate([kr] * ML_HEADS, axis=1)).astype(BF16)
    vd_ref[...] = _dot(cb, wv_ref[...]).astype(BF16)


def _prep(ymix, lw, tabs, latent):
    tm = 512
    const = lambda i: (0, 0)
    in_specs = [
        pl.BlockSpec((tm, 512), lambda i: (i, COL_GQ)),
        pl.BlockSpec((tm, LANE), lambda i: (i, COL_GK)),
        pl.BlockSpec((tm, 256), lambda i: (i, COL_MQA)),
        pl.BlockSpec((tm, LANE), lambda i: (i, COL_MKVA)),
        pl.BlockSpec((tm, LANE), lambda i: (i, COL_MKR)),
        pl.BlockSpec((1, 512), const),
        pl.BlockSpec((1, LANE), const),
        pl.BlockSpec((1, 256), const),
        pl.BlockSpec((1, LANE), const),
        pl.BlockSpec((ML_Q_RANK, ML_HEADS * LANE), const),
        pl.BlockSpec((ML_KV_RANK, ML_HEADS * LANE), const),
        pl.BlockSpec((ML_KV_RANK, ML_HEADS * ML_V), const),
        pl.BlockSpec((512, 512), const),
    ]
    args = [ymix, ymix, ymix, ymix, ymix, lw["gq_q_gain"], lw["gq_k_gain"], lw["ml_q_gain"],
            lw["ml_kv_gain"], lw["w_qb"], lw["w_k"], lw["w_v"], lw["avg"]]
    if latent:
        per = DEC_SEQ // tm
        in_specs += [pl.BlockSpec((3, tm, LANE), lambda i: (0, i % per, 0))] * 2
        args += [tabs["gq"], tabs["ml"]]
    widths = (512, LANE, ML_HEADS * LANE, LANE, ML_HEADS * LANE, ML_HEADS * ML_V)
    return pl.pallas_call(
        functools.partial(_prep_kernel, rope=latent),
        grid=(M_ROWS // tm,),
        in_specs=in_specs,
        out_specs=[pl.BlockSpec((tm, w), lambda i: (i, 0)) for w in widths],
        out_shape=[jax.ShapeDtypeStruct((M_ROWS, w), dt)
                   for w, dt in zip(widths, (BF16, F32, BF16, F32, BF16, BF16))],
        compiler_params=_params("parallel"),
        name="prep_lat" if latent else "prep_ctx",
    )(*args)


def _mla_cache_kernel(ckv_ref, kr_ref, wk_ref, wv_ref, kd_ref, vd_ref):
    cb = ckv_ref[...].astype(BF16)
    kd_ref[...] = (_dot(cb, wk_ref[...]) + jnp.concatenate([kr_ref[...]] * ML_HEADS, axis=1)).astype(BF16)
    vd_ref[...] = _dot(cb, wv_ref[...]).astype(BF16)


def _mla_cache(ckv, kr_blk, lw, layer):
    rows = DEC_BATCH * PAST_LEN
    tm = PAST_LEN
    const = lambda i: (0, 0)
    return pl.pallas_call(
        _mla_cache_kernel,
        grid=(DEC_BATCH,),
        in_specs=[
            pl.BlockSpec((tm, LANE), lambda i: (i * DEPTH + layer, 0)),
            pl.BlockSpec((tm, LANE), lambda i: (i * DEPTH + layer, 0)),
            pl.BlockSpec((ML_KV_RANK, ML_HEADS * LANE), const),
            pl.BlockSpec((ML_KV_RANK, ML_HEADS * ML_V), const),
        ],
        out_specs=[pl.BlockSpec((tm, ML_HEADS * LANE), lambda i: (i, 0)),
                   pl.BlockSpec((tm, ML_HEADS * ML_V), lambda i: (i, 0))],
        out_shape=[jax.ShapeDtypeStruct((rows, ML_HEADS * LANE), BF16),
                   jax.ShapeDtypeStruct((rows, ML_HEADS * ML_V), BF16)],
        compiler_params=_params("parallel"),
        name="mla_cache",
    )(ckv, kr_blk, lw["w_k"], lw["w_v"])


def _softmax_pv(scores, values, transposed=None):
    transposed = transposed or [False] * len(values)
    m = scores[0].max(axis=-1, keepdims=True)
    for s in scores[1:]:
        m = jnp.maximum(m, s.max(axis=-1, keepdims=True))
    den = None
    out = None
    for s, v, vt in zip(scores, values, transposed):
        p = jnp.exp(s - m)
        d = p.sum(axis=-1, keepdims=True)
        o = _dot_nt(p.astype(BF16), v) if vt else _dot(p.astype(BF16), v)
        den = d if den is None else den + d
        out = o if out is None else out + o
    return out / den


def _softmax_pv_phased(scores, values):
    maxima = [s.max(axis=-1, keepdims=True) for s in scores]
    probs = [jnp.exp(s - m) for s, m in zip(scores, maxima)]
    dens = [p.sum(axis=-1, keepdims=True) for p in probs]
    outs = [_dot(p.astype(BF16), v) for p, v in zip(probs, values)]
    return [o / d for o, d in zip(outs, dens)]


def _lane_halves():
    lane = lax.broadcasted_iota(jnp.int32, (1, LANE), 1)
    return lane < HALF, lane >= HALF


def _attn_kernel(*refs, plan, nseg, scale, fold_scale):
    q_ref = refs[0]
    k_refs = [refs[1 + 2 * i] for i in range(nseg)]
    v_refs = [refs[2 + 2 * i] for i in range(nseg)]
    o_ref = refs[1 + 2 * nseg]
    tq = q_ref.shape[0]
    lo, hi = _lane_halves()
    cache = {}

    def block(kind, seg, blk, swap):
        key = (kind, seg, blk, swap)
        if key not in cache:
            ref = (k_refs if kind == "k" else v_refs)[seg]
            x = ref[:, blk * LANE:(blk + 1) * LANE]
            if swap:
                x = pltpu.roll(x, HALF, 1)
            cache[key] = x.astype(BF16)
        return cache[key]

    def queries(members):
        qs = []
        for qblk, qhalf, _, _ in members:
            q = q_ref[:, qblk * LANE:(qblk + 1) * LANE]
            if fold_scale:
                q = q * scale
            if qhalf is not None:
                q = jnp.where(lo if qhalf == 0 else hi, q, 0.0)
            qs.append(q)
        return (jnp.concatenate(qs, axis=0) if len(qs) > 1 else qs[0]).astype(BF16)

    def scores(q, kblk, swap):
        out = [_dot_nt(q, block("k", s, kblk, swap)) for s in range(nseg)]
        return out if fold_scale else [s * scale for s in out]

    if nseg == 1:
        all_scores = [scores(queries(members), kblk, swap)[0] for members, kblk, swap, _ in plan]
        outs = _softmax_pv_phased(all_scores, [block("v", 0, vblk, swap) for _, _, swap, vblk in plan])
    else:
        outs = []
        ready = scores(queries(plan[0][0]), plan[0][1], plan[0][2])
        for g, (_, _, swap, vblk) in enumerate(plan):
            if g + 1 < len(plan):
                following = scores(queries(plan[g + 1][0]), plan[g + 1][1], plan[g + 1][2])
            outs.append(_softmax_pv(ready, [block("v", s, vblk, swap) for s in range(nseg)]))
            ready = following
    parts = {}
    for (members, _, _, _), out in zip(plan, outs):
        for n, (_, _, oblk, ohalf) in enumerate(members):
            parts[(oblk, ohalf)] = out[n * tq:(n + 1) * tq]
    for oblk in sorted({key[0] for key in parts}):
        o_ref[:, oblk * LANE:(oblk + 1) * LANE] = jnp.where(lo, parts[(oblk, 0)], parts[(oblk, 1)]).astype(BF16)


def _plan_gqa():
    per_kv = GQ_HEADS // GQ_KV_HEADS
    plan = []
    for g in range(GQ_KV_HEADS):
        for half in range(2):
            heads = [h for h in range(g * per_kv, (g + 1) * per_kv) if h % 2 == half]
            plan.append((tuple((h // 2, half, h // 2, half) for h in heads), 0, half != g, 0))
    return tuple(plan)


def _plan_na():
    return tuple((((j, 0, j, 0), (j, 1, j, 1)), j, False, j) for j in range(NA_HEADS // 2))


def _plan_mla():
    return tuple((((h, None, h // 2, h % 2),), h, False, h // 2) for h in range(ML_HEADS))


def _is_pow2(x):
    return float(np.log2(x)).is_integer()


def _attention(q, segs, plan, scale, batch, tq_total, name, tq=256):
    tq = min(tq_total, tq)
    per = tq_total // tq
    q_arr, q_w, q_col, q_off = q
    in_specs = [pl.BlockSpec((tq, q_w), lambda b, i: (q_off + b * per + i, q_col))]
    args = [q_arr]
    for k, v in segs:
        for arr, w, col, rows, stride, off in (k, v):
            in_specs.append(pl.BlockSpec(
                (rows, w), functools.partial(lambda b, i, col, stride, off: (off + b * stride, col),
                                             col=col, stride=stride, off=off)))
            args.append(arr)
    out_w = 512
    return pl.pallas_call(
        functools.partial(_attn_kernel, plan=plan, nseg=len(segs), scale=scale, fold_scale=_is_pow2(scale)),
        grid=(batch, per),
        in_specs=in_specs,
        out_specs=pl.BlockSpec((tq, out_w), lambda b, i: (b * per + i, 0)),
        out_shape=jax.ShapeDtypeStruct((M_ROWS, out_w), BF16),
        compiler_params=_params("parallel", "parallel"),
        name=name,
    )(*args)


NA_DR = 2 * NA_WIN_R - 1
NA_DC = 2 * NA_WIN_C - 1
NA_ROWS = DEC_SEQ // GRID_W
NA_QROWS = 4
NA_WROWS = 12
NA_T_RIGHT = NA_DR - 1
NA_T_LEFT = NA_DR
NA_T_NONE = NA_DR + 1
NA_T_SIZE = NA_DR + 2
assert _is_pow2(NA_SCALE)


def _na_kernel(q_ref, kl_ref, vl_ref, kc_ref, vc_ref, tz_ref, o_ref, kc_scr, vc_scr):
    g = pl.program_id(1)

    @pl.when(g == 0)
    def _():
        kc_scr[...] = kc_ref[...].astype(BF16)
        vc_scr[...] = vc_ref[...].astype(BF16)

    w0 = (g // 2) * (NA_ROWS - NA_WROWS)
    k0 = pl.multiple_of(w0 * GRID_W, (NA_ROWS - NA_WROWS) * GRID_W)
    win = NA_WROWS * GRID_W
    nq = NA_QROWS * GRID_W
    entry = []
    for a in range(NA_QROWS):
        rq = g * NA_QROWS + a
        start = jnp.clip(rq - NA_WIN_R // 2, 0, NA_ROWS - NA_WIN_R)
        per_pair = []
        for i in range(NA_WROWS // 2):
            kr = w0 + 2 * i
            in_l = jnp.logical_and(kr >= start, kr < start + NA_WIN_R)
            in_r = jnp.logical_and(kr + 1 >= start, kr + 1 < start + NA_WIN_R)
            both = jnp.logical_and(in_l, in_r)
            d_l = kr - rq + NA_WIN_R - 1
            per_pair.append(jnp.where(both, d_l, jnp.where(in_r, NA_T_RIGHT, jnp.where(in_l, NA_T_LEFT, NA_T_NONE))))
        entry.append(per_pair)
    lo, hi = _lane_halves()
    def scores(j):
        sl = slice(j * LANE, (j + 1) * LANE)
        k_loc = kl_ref[pl.ds(k0, win), sl].astype(BF16)
        k_ctx_t = kc_scr[sl, :]
        q_pair = q_ref[:, sl] * NA_SCALE
        q = jnp.concatenate([jnp.where(lo, q_pair, 0.0), jnp.where(hi, q_pair, 0.0)], axis=0).astype(BF16)
        bias = jnp.concatenate(
            [jnp.concatenate([tz_ref[2 * j + p, e] for e in entry[a]], axis=1)
             for p in range(2) for a in range(NA_QROWS)], axis=0)
        return [_dot_nt(q, k_loc) + bias, _dot(q, k_ctx_t)]

    pairs = NA_HEADS // 2
    ready = scores(0)
    for j in range(pairs):
        sl = slice(j * LANE, (j + 1) * LANE)
        if j + 1 < pairs:
            following = scores(j + 1)
        v_loc = vl_ref[pl.ds(k0, win), sl].astype(BF16)
        out = _softmax_pv(ready, [v_loc, vc_scr[sl, :]], [False, True])
        o_ref[:, sl] = jnp.where(lo, out[:nq], out[nq:]).astype(BF16)
        ready = following


def _na_bias_kernel(rpb_ref, sel_ref, neg_ref, o_ref):
    o_ref[...] = _dot_exact_rhs(rpb_ref[...], sel_ref[...]) + neg_ref[...]


def _na_bias_tables(rpb):
    col = np.arange(GRID_W)
    col_start = np.clip(col - NA_WIN_C // 2, 0, GRID_W - NA_WIN_C)
    col_ok = (col[None, :] >= col_start[:, None]) & (col[None, :] < col_start[:, None] + NA_WIN_C)
    dc = col[None, :] - col[:, None] + NA_WIN_C - 1
    kpad = 32
    sel = (dc[None, :, :] == np.arange(kpad)[:, None, None]) & col_ok[None]
    sel = jnp.asarray(sel.reshape(kpad, GRID_W * GRID_W), BF16)
    neg = jnp.asarray(np.where(col_ok, 0.0, NEG).reshape(1, GRID_W * GRID_W), F32)
    rpb2 = jnp.pad(rpb.reshape(NA_HEADS * NA_DR, NA_DC), ((0, 0), (0, kpad - NA_DC)))
    n_rows = NA_HEADS * NA_DR
    full = lambda shape: pl.BlockSpec(shape, lambda i: (0, 0))
    t = pl.pallas_call(
        _na_bias_kernel,
        grid=(1,),
        in_specs=[full((n_rows, kpad)), full((kpad, GRID_W * GRID_W)), full((1, GRID_W * GRID_W))],
        out_specs=full((n_rows, GRID_W * GRID_W)),
        out_shape=jax.ShapeDtypeStruct((n_rows, GRID_W * GRID_W), F32),
        compiler_params=_params("arbitrary"),
        name="na_bias",
    )(rpb2, sel, neg)
    t = t.reshape(NA_HEADS, NA_DR, GRID_W, GRID_W)
    masked = jnp.full((NA_HEADS, 1, GRID_W, GRID_W), NEG, F32)
    first, last = NA_WIN_R // 2 - 1, NA_WIN_R // 2 + NA_WIN_R - 2
    return jnp.concatenate([
        jnp.concatenate([t[:, :-1], t[:, 1:]], axis=-1),
        jnp.concatenate([masked, t[:, first:first + 1]], axis=-1),
        jnp.concatenate([t[:, last:last + 1], masked], axis=-1),
        jnp.concatenate([masked, masked], axis=-1)], axis=1)


def _na_latent(ymix, cache_k, cache_v, tz, layer):
    groups = NA_ROWS // NA_QROWS
    nq = NA_QROWS * GRID_W
    return pl.pallas_call(
        _na_kernel,
        grid=(DEC_BATCH, groups),
        in_specs=[
            pl.BlockSpec((nq, 512), lambda b, g: (b * groups + g, COL_NQ)),
            pl.BlockSpec((DEC_SEQ, 512), lambda b, g: (b, COL_NK)),
            pl.BlockSpec((DEC_SEQ, 512), lambda b, g: (b, COL_NV)),
            pl.BlockSpec((NA_HEADS * NA_HD, PAST_LEN), lambda b, g: (b * DEPTH + layer, 0)),
            pl.BlockSpec((NA_HEADS * NA_HD, PAST_LEN), lambda b, g: (b * DEPTH + layer, 0)),
            pl.BlockSpec((NA_HEADS, NA_T_SIZE, GRID_W, LANE), lambda b, g: (0, 0, 0, 0)),
        ],
        out_specs=pl.BlockSpec((nq, 512), lambda b, g: (b * groups + g, 0)),
        out_shape=jax.ShapeDtypeStruct((M_ROWS, 512), BF16),
        scratch_shapes=[pltpu.VMEM((NA_HEADS * NA_HD, PAST_LEN), BF16)] * 2,
        compiler_params=_params("parallel", "arbitrary"),
        name="na_lat",
    )(ymix, ymix, ymix, cache_k, cache_v, tz)


def _na_cache_out_kernel(*refs):
    srcs, (ko_ref, vo_ref) = refs[:2 * DEPTH], refs[2 * DEPTH:]
    layer = pl.program_id(1)
    for l in range(DEPTH):
        @pl.when(layer == l)
        def _():
            ko_ref[...] = srcs[2 * l][...].T
            vo_ref[...] = srcs[2 * l + 1][...].T


def _na_cache_out(ymix_layers):
    rows = NA_HEADS * NA_HD
    in_specs, args = [], []
    for ymix in ymix_layers:
        for col in (COL_NK, COL_NV):
            in_specs.append(pl.BlockSpec((SEQ, 512), functools.partial(lambda b, l, col: (b, col), col=col)))
            args.append(ymix)
    out = pl.pallas_call(
        _na_cache_out_kernel,
        grid=(BATCH, DEPTH),
        in_specs=in_specs,
        out_specs=[pl.BlockSpec((rows, SEQ), lambda b, l: (b * DEPTH + l, 0))] * 2,
        out_shape=[jax.ShapeDtypeStruct((BATCH * DEPTH * rows, SEQ), F32)] * 2,
        compiler_params=_params("parallel", "arbitrary"),
        name="na_cache_out",
    )(*args)
    return tuple(o.reshape(BATCH, DEPTH, NA_HEADS, NA_HD, SEQ).transpose(0, 1, 4, 2, 3) for o in out)


HG_BLOCK = 128
HG_STATE_BLOCK = 64
HG_STATE_UNROLL = 4
HG_DEC_ROWS = 8


def _hgrn_kernel(*refs, seq, has_state):
    if has_state:
        (q_ref, ff_ref, fb_ref, v_ref, g_ref, lb_ref, gain_ref, s0_ref,
         o_ref, of_ref, ob_ref, qin_ref, kend_ref, dec_ref, st_ref) = refs
    else:
        (q_ref, ff_ref, fb_ref, v_ref, g_ref, lb_ref, gain_ref,
         o_ref, sout_ref, of_ref, ob_ref, qin_ref, kend_ref, dec_ref, st_ref) = refs
    c = HG_CHUNK
    hc = c // 2
    rb = HG_BLOCK
    per_block = rb // c
    n_chunks = seq // c
    width = HG_HEADS * HG_DK
    sub = HG_DEC_ROWS

    ri = lax.broadcasted_iota(jnp.int32, (rb, rb), 0)
    ci = lax.broadcasted_iota(jnp.int32, (rb, rb), 1)
    same = (ri // c) == (ci // c)
    tri_f = jnp.where(jnp.logical_and(same, ci <= ri), 1.0, 0.0).astype(BF16)
    tri_b = jnp.where(jnp.logical_and(same, ci >= ri), 1.0, 0.0).astype(BF16)
    ones = jnp.ones((HG_DK, HG_DV), BF16)
    rowid = lax.broadcasted_iota(jnp.int32, (c, HG_DV), 0)
    laneid = lax.broadcasted_iota(jnp.int32, (c, HG_DV), 1)
    laneid_half = lax.broadcasted_iota(jnp.int32, (hc, HG_DV), 1)
    o_refs = (of_ref, ob_ref)

    n_blocks = seq // rb

    def block_of(d, step):
        return step if d == 0 else n_blocks - 1 - step

    heads = [slice(h * HG_DK, (h + 1) * HG_DK) for h in range(HG_HEADS)]
    per_chunk = c * hc
    dirs = (0, 1)

    def pair_block(step):
        fwd = [True, False]
        pre_refs = (ff_ref, fb_ref)
        tris = (tri_f, tri_b)
        r0 = [_aligned(block_of(d, step) * rb, rb) for d in dirs]
        near_rows = [range(0, hc), range(hc, c)]
        far_rows = [range(hc, c), range(0, hc)]
        near = [slice(0, hc), slice(hc, c)]
        far = [slice(hc, c), slice(0, hc)]
        edge = [hc - 1, hc]
        q_all = [q_ref[pl.ds(r0[d], rb), :] * (HG_DK ** -0.5) for d in dirs]
        v16_all = [v_ref[pl.ds(r0[d], rb), :].astype(BF16) for d in dirs]
        k_all, log_f = [], []
        for d in dirs:
            lb = lb_ref[d:d + 1, :]
            f = lb + (1.0 - lb) * jax.nn.sigmoid(pre_refs[d][pl.ds(r0[d], rb), :])
            k_all.append(1.0 - f)
            log_f.append(jnp.log(f))
        b_all = [_dot_exact_lhs(tris[d], log_f[d]) * LOG2E for d in dirs]
        src_all = [b_all[d] - jnp.log(k_all[d]) * LOG2E for d in dirs]
        pairs, q_edge, k_edge = [], {}, {}
        zero_half = jnp.zeros((hc, width), F32)
        for d in dirs:
            blocks, q_in, k_end = [], [], []
            for m in range(per_block):
                rows = slice(m * c, (m + 1) * c)
                q, k, b, src = q_all[d][rows], k_all[d][rows], b_all[d][rows], src_all[d][rows]
                b_last = b[c - 1:c] if fwd[d] else b[0:1]
                q_in.append(q * jnp.exp2(b))
                k_end.append(k * jnp.exp2(b_last - b))
                dec_row = _aligned((block_of(d, step) * per_block + m) * sub, sub)
                dec_ref[d, pl.ds(dec_row, sub), :] = jnp.broadcast_to(jnp.exp2(b_last), (sub, width))
                halves = []
                for rows_s, part in ((near_rows[d], near[d]), (far_rows[d], far[d])):
                    q_part, b_part = q[part], b[part]
                    halves += [q_part * jnp.exp2(b_part - src[s:s + 1]) for s in rows_s]
                for i in range(0, c, 2):
                    blocks.append(jnp.concatenate(halves[i:i + 2], axis=0).astype(BF16))
                b_edge = b[edge[d]:edge[d] + 1]
                q_far = q[far[d]] * jnp.exp2(b[far[d]] - b_edge)
                k_near = jnp.exp2(b_edge - src[near[d]])
                q_edge[d, m] = jnp.concatenate([zero_half, q_far] if fwd[d] else [q_far, zero_half], axis=0).astype(BF16)
                k_edge[d, m] = jnp.concatenate([k_near, zero_half] if fwd[d] else [zero_half, k_near], axis=0).astype(BF16)
            qin_ref[d, pl.ds(r0[d], rb), :] = jnp.concatenate(q_in, axis=0).astype(BF16)
            kend_ref[d, pl.ds(r0[d], rb), :] = jnp.concatenate(k_end, axis=0).astype(BF16)
            pairs.append(jnp.concatenate(blocks, axis=0))
        sums = [[_dot(pairs[d][:, sl], ones) for sl in heads] for d in dirs]
        cells = [(d, h, m) for d in dirs for h in range(HG_HEADS) for m in range(per_block)]
        k_pad = jnp.zeros((HG_DK - c, HG_DK), BF16)
        across = {(d, h, m): _dot_nt(q_edge[d, m][:, heads[h]],
                                     jnp.concatenate([k_edge[d, m][:, heads[h]], k_pad], axis=0))
                  for d, h, m in cells}
        near_attn = {cell: jnp.zeros((hc, HG_DV), F32) for cell in cells}
        far_attn = {cell: jnp.zeros((hc, HG_DV), F32) for cell in cells}
        for n in range(hc):
            for d, h, m in cells:
                base = m * per_chunk + n * hc
                near_attn[d, h, m] = jnp.where(laneid_half == near_rows[d][n], sums[d][h][base:base + hc],
                                               near_attn[d, h, m])
                base += hc * hc
                far_attn[d, h, m] = jnp.where(laneid_half == far_rows[d][n], sums[d][h][base:base + hc],
                                              far_attn[d, h, m])
        v_pad = jnp.zeros((HG_DK - c, HG_DV), BF16)
        causal = [laneid <= rowid, laneid >= rowid]
        prods = {}
        for d, h, m in cells:
            halves = [near_attn[d, h, m], far_attn[d, h, m]] if fwd[d] else [far_attn[d, h, m], near_attn[d, h, m]]
            inside = jnp.where(causal[d], jnp.concatenate(halves, axis=0), 0.0)
            values = jnp.concatenate([v16_all[d][m * c:(m + 1) * c, heads[h]], v_pad], axis=0)
            prods[d, h, m] = _dot((inside + across[d, h, m]).astype(BF16), values)
        for d in dirs:
            o_refs[d][pl.ds(r0[d], rb), :] = jnp.concatenate(
                [jnp.concatenate([prods[d, h, m] for m in range(per_block)], axis=0) for h in range(HG_HEADS)], axis=1)


    def state_block(step, carry):
        rb, per_block = HG_STATE_BLOCK, HG_STATE_BLOCK // c
        block_of = lambda d, step: step if d == 0 else seq // rb - 1 - step
        order = [list(range(per_block)), list(reversed(range(per_block)))]
        r0 = [_aligned(block_of(d, step) * rb, rb) for d in dirs]
        q_in = [qin_ref[d, pl.ds(r0[d], rb), :] for d in dirs]
        k_end = [kend_ref[d, pl.ds(r0[d], rb), :] for d in dirs]
        v16 = [v_ref[pl.ds(r0[d], rb), :].astype(BF16) for d in dirs]
        rows = [slice(m * c, (m + 1) * c) for m in range(per_block)]
        q_state, k_state, k_cross, v_cross, total = [], [], [], [], []
        for d in dirs:
            dec_row = _aligned(block_of(d, step) * per_block * sub, sub)
            dec_all = dec_ref[d, pl.ds(dec_row, per_block * sub), :]
            dec = [dec_all[m * sub:m * sub + 1] for m in order[d]]
            before = [None] * per_block
            after = [None] * per_block
            for j in range(1, per_block):
                before[j] = dec[j - 1] if before[j - 1] is None else before[j - 1] * dec[j - 1]
            for j in range(per_block - 2, -1, -1):
                after[j] = dec[j + 1] if after[j + 1] is None else after[j + 1] * dec[j + 1]
            total.append(before[-1] * dec[-1])
            scale = lambda x, f: x if f is None else x * f
            qs, ks = [None] * per_block, [None] * per_block
            for j, m in enumerate(order[d]):
                qs[m] = scale(q_in[d][rows[m]], before[j])
                ks[m] = scale(k_end[d][rows[m]], after[j])
            q_state.append(jnp.concatenate(qs, axis=0).astype(BF16))
            k_state.append(jnp.concatenate(ks, axis=0).astype(BF16))
            kc, vc = [None], [None]
            for j in range(1, per_block):
                keys, between = [], None
                for i in range(j - 1, -1, -1):
                    keys.insert(0, scale(k_end[d][rows[order[d][i]]], between))
                    between = dec[i] if between is None else between * dec[i]
                kc.append(jnp.concatenate(keys, axis=0).astype(BF16))
                vc.append(jnp.concatenate([v16[d][rows[order[d][i]]] for i in range(j)], axis=0))
            k_cross.append(kc)
            v_cross.append(vc)
        q16 = [q_in[d].astype(BF16) for d in dirs]
        cells = [(d, h) for d in dirs for h in range(HG_HEADS)]
        kv = {(d, h): _dot_tn(v16[d][:, heads[h]], k_state[d][:, heads[h]]) for d, h in cells}
        attn = {(d, h, j): _dot_nt(q16[d][rows[order[d][j]], heads[h]], k_cross[d][j][:, heads[h]])
                for d, h in cells for j in range(1, per_block)}
        st = {(d, h): st_ref[d, h] for d, h in cells}
        from_state = {(d, h): _dot_nt(q_state[d][:, heads[h]], st[d, h].astype(BF16)) for d, h in cells}
        cross = {key: _dot(a.astype(BF16), v_cross[key[0]][key[2]][:, heads[key[1]]]) for key, a in attn.items()}
        for d, h in cells:
            st_ref[d, h] = st[d, h] * total[d][:, heads[h]] + kv[d, h]
        for d in dirs:
            cols = []
            for h in range(HG_HEADS):
                parts = [None] * per_block
                for j, m in enumerate(order[d]):
                    part = from_state[d, h][rows[m]]
                    parts[m] = part if j == 0 else part + cross[d, h, j]
                cols.append(jnp.concatenate(parts, axis=0))
            o_refs[d][pl.ds(r0[d], rb), :] += jnp.concatenate(cols, axis=1)
        return carry

    if has_state:
        for d in range(2):
            for h in range(HG_HEADS):
                st_ref[d, h] = s0_ref[d, h].T
    else:
        st_ref[...] = jnp.zeros(st_ref.shape, F32)

    def pair_step(i, carry):
        pair_block(i)
        return carry

    lax.fori_loop(0, n_blocks, pair_step, 0)
    lax.fori_loop(0, seq // HG_STATE_BLOCK, state_block, 0, unroll=HG_STATE_UNROLL)

    for h in range(HG_HEADS):
        sl = slice(h * HG_DV, (h + 1) * HG_DV)
        o = of_ref[:, sl] + ob_ref[:, sl]
        o_ref[:, sl] = (_rms(o) * gain_ref[:, sl] * _silu(g_ref[:, sl])).astype(BF16)
    if not has_state:
        for d in range(2):
            for h in range(HG_HEADS):
                sout_ref[d, h] = st_ref[d, h].T


def _hgrn(ymix, lb, gain, batch, seq, state, layer):
    width = HG_HEADS * HG_DK
    has_state = state is not None
    assert HG_DK == HG_DV
    st_shape = (2, HG_HEADS, HG_DK, HG_DV)
    in_specs = [pl.BlockSpec((seq, width), functools.partial(lambda b, col: (b, col), col=col))
                for col in (COL_HQ, COL_HFF, COL_HFB, COL_HI, COL_HG)]
    in_specs += [pl.BlockSpec((2, width), lambda b: (0, 0)), pl.BlockSpec((1, width), lambda b: (0, 0))]
    args = [ymix] * 5 + [lb, gain]
    out_specs = [pl.BlockSpec((seq, width), lambda b: (b, 0))]
    out_shape = [jax.ShapeDtypeStruct((M_ROWS, width), BF16)]
    if has_state:
        in_specs.append(pl.BlockSpec((None, None) + st_shape, lambda b: (b, layer, 0, 0, 0, 0)))
        args.append(state)
    else:
        out_specs.append(pl.BlockSpec((None,) + st_shape, lambda b: (b, 0, 0, 0, 0)))
        out_shape.append(jax.ShapeDtypeStruct((batch,) + st_shape, F32))
    return pl.pallas_call(
        functools.partial(_hgrn_kernel, seq=seq, has_state=has_state),
        grid=(batch,),
        in_specs=in_specs,
        out_specs=out_specs,
        out_shape=out_shape,
        scratch_shapes=[pltpu.VMEM((seq, width), F32), pltpu.VMEM((seq, width), F32),
                        pltpu.VMEM((2, seq, width), BF16), pltpu.VMEM((2, seq, width), BF16),
                        pltpu.VMEM((2, seq // HG_CHUNK * HG_DEC_ROWS, width), F32),
                        pltpu.VMEM(st_shape, F32)],
        compiler_params=_params("parallel"),
        name="hgrn_lat" if has_state else "hgrn_ctx",
    )(*args)


def _merge_kernel(oa_ref, ob_ref, oc_ref, od_ref, gt_ref, wb_ref, wo_ref, x_ref, g1_ref, out_ref):
    half = out_ref.shape[0] // 2
    rows = [pl.ds(0, half), pl.ds(half, half)]
    projected = [[_dot(o_ref[r, :], wb_ref[n]) for n, o_ref in enumerate((oa_ref, ob_ref, oc_ref, od_ref))]
                 for r in rows]
    for r, branch in zip(rows, projected):
        acc = None
        for n, bo in enumerate(branch):
            term = gt_ref[r, n * D_MODEL:(n + 1) * D_MODEL] * bo
            acc = term if acc is None else acc + term
        out_ref[r, :] = x_ref[r, :] + g1_ref[...] * _dot(acc.astype(BF16), wo_ref[...])


def _merge(branches, gates, w_branch, w_out, x, mod, latent):
    tm = 512
    row = _mod_row(latent, tm)
    tile = lambda w: pl.BlockSpec((tm, w), lambda i: (i, 0))
    return pl.pallas_call(
        _merge_kernel,
        grid=(M_ROWS // tm,),
        in_specs=[tile(BRANCH_W)] * N_BRANCH + [
            tile(GATE_W),
            pl.BlockSpec((N_BRANCH, BRANCH_W, D_MODEL), lambda i: (0, 0, 0)),
            pl.BlockSpec((D_MODEL, D_MODEL), lambda i: (0, 0)),
            tile(D_MODEL),
            pl.BlockSpec((None, None, 1, D_MODEL), lambda i: (row(i), 2, 0, 0)),
        ],
        out_specs=tile(D_MODEL),
        out_shape=jax.ShapeDtypeStruct((M_ROWS, D_MODEL), F32),
        compiler_params=_params("parallel"),
        name="merge_lat" if latent else "merge_ctx",
    )(*branches, gates, w_branch, w_out, x, mod)


FFN_CHUNK = 256
FFN_STEPS = FFN_HIDDEN // FFN_CHUNK
assert FFN_STEPS * FFN_CHUNK == FFN_HIDDEN


def _ffn_kernel(*refs, final):
    x_ref, sc_ref, sh_ref, g2_ref, wa_ref, wg_ref, wo_ref = refs[:7]
    if final:
        fg_ref, out_ref, h_ref, acc_ref = refs[7:]
    else:
        out_ref, h_ref, acc_ref = refs[7:]
    j = pl.program_id(1)

    @pl.when(j == 0)
    def _():
        h = _rms(x_ref[...]) * (1.0 + sc_ref[...]) + sh_ref[...]
        h_ref[...] = h.astype(BF16)
        acc_ref[...] = jnp.zeros(acc_ref.shape, F32)

    wg, wa, wo = wg_ref[...].astype(BF16), wa_ref[...].astype(BF16), wo_ref[...].astype(BF16)
    half = h_ref.shape[0] // 2
    rows = [pl.ds(0, half), pl.ds(half, half)]
    up = [(_dot(h_ref[r, :], wg), _dot(h_ref[r, :], wa)) for r in rows]
    for r, (gate, lin) in zip(rows, up):
        acc_ref[r, :] += _dot((_silu(gate) * lin).astype(BF16), wo)

    @pl.when(j == FFN_STEPS - 1)
    def _():
        y = x_ref[...] + g2_ref[...] * acc_ref[...]
        out_ref[...] = _rms(y) * fg_ref[...] if final else y


def _ffn(x, mod, w_in, w_out, layer, latent, final_gain):
    tm = 1024
    row = _mod_row(latent, tm)
    modspec = lambda which: pl.BlockSpec((None, None, 1, D_MODEL), lambda i, j: (row(i), which, 0, 0))
    final = final_gain is not None
    in_specs = [
        pl.BlockSpec((tm, D_MODEL), lambda i, j: (i, 0)),
        modspec(4), modspec(3), modspec(5),
        pl.BlockSpec((None, D_MODEL, FFN_CHUNK), lambda i, j: (layer, 0, j)),
        pl.BlockSpec((None, D_MODEL, FFN_CHUNK), lambda i, j: (layer, 0, FFN_STEPS + j)),
        pl.BlockSpec((None, FFN_CHUNK, D_MODEL), lambda i, j: (layer, j, 0)),
    ]
    args = [x, mod, mod, mod, w_in, w_in, w_out]
    if final:
        in_specs.append(pl.BlockSpec((1, D_MODEL), lambda i, j: (0, 0)))
        args.append(final_gain.reshape(1, D_MODEL))
    return pl.pallas_call(
        functools.partial(_ffn_kernel, final=final),
        grid=(M_ROWS // tm, FFN_STEPS),
        in_specs=in_specs,
        out_specs=pl.BlockSpec((tm, D_MODEL), lambda i, j: (i, 0)),
        out_shape=jax.ShapeDtypeStruct((M_ROWS, D_MODEL), F32),
        scratch_shapes=[pltpu.VMEM((tm, D_MODEL), BF16), pltpu.VMEM((tm, D_MODEL), F32)],
        compiler_params=_params("parallel", "arbitrary"),
        name="ffn_lat" if latent else "ffn_ctx",
    )(*args)


def _rope_tables():
    t = jnp.arange(DEC_SEQ)
    row = (t // GRID_W).astype(F32)[:, None]
    col = (t % GRID_W).astype(F32)[:, None]

    def angles(rot_dim):
        n_freq = rot_dim // 4
        inv_freq = ROPE_THETA ** (-jnp.arange(n_freq, dtype=F32) / n_freq)
        return jnp.concatenate([row * inv_freq, col * inv_freq], axis=-1)

    def expand(ang):
        cos = jnp.repeat(jnp.cos(ang), 2, axis=-1)
        sin = jnp.repeat(jnp.sin(ang), 2, axis=-1)
        even = (jnp.arange(cos.shape[-1]) % 2 == 0)[None, :]
        return cos, jnp.where(even, -sin, 0.0), jnp.where(even, 0.0, sin)

    gq = [jnp.concatenate([a, a], axis=-1) for a in expand(angles(GQ_HD))]
    ml = []
    for idx, a in enumerate(expand(angles(ML_ROPE))):
        fill = 1.0 if idx == 0 else 0.0
        ml.append(jnp.concatenate([a, jnp.full((DEC_SEQ, LANE - ML_ROPE), fill, F32)], axis=-1))
    return {"gq": jnp.stack(gq), "ml": jnp.stack(ml)}


def _layer_weights(l, gq_q_gain, gq_k_gain, ml_q_a_gain, ml_kv_a_gain, ml_w_q_b, ml_w_kv_b,
                   w_branch, w_out, hg_gain, avg):
    pad = LANE - ML_NOPE - ML_ROPE
    qb = ml_w_q_b[l].reshape(ML_Q_RANK, ML_HEADS, ML_NOPE + ML_ROPE)
    qb = jnp.concatenate([qb[:, :, ML_NOPE:], qb[:, :, :ML_NOPE], jnp.zeros((ML_Q_RANK, ML_HEADS, pad), F32)],
                         axis=-1).reshape(ML_Q_RANK, ML_HEADS * LANE)
    kvb = ml_w_kv_b[l].reshape(ML_KV_RANK, ML_HEADS, ML_NOPE + ML_V)
    wk = jnp.pad(kvb[:, :, :ML_NOPE], ((0, 0), (0, 0), (ML_ROPE, pad))).reshape(ML_KV_RANK, ML_HEADS * LANE)
    wv = kvb[:, :, ML_NOPE:].reshape(ML_KV_RANK, ML_HEADS * ML_V)
    return {
        "gq_q_gain": jnp.tile(gq_q_gain[l], GQ_HEADS).reshape(1, -1),
        "gq_k_gain": jnp.tile(gq_k_gain[l], GQ_KV_HEADS).reshape(1, -1),
        "ml_q_gain": ml_q_a_gain[l].reshape(1, -1), "ml_kv_gain": ml_kv_a_gain[l].reshape(1, -1),
        "w_qb": qb.astype(BF16), "w_k": wk.astype(BF16), "w_v": wv.astype(BF16),
        "w_branch": w_branch[l].astype(BF16), "w_out": w_out[l].astype(BF16),
        "hg_gain": jnp.tile(hg_gain[l], HG_HEADS).reshape(1, -1), "avg": avg,
    }


def _seg(arr, width, col, rows, stride, off):
    return (arr, width, col, rows, stride, off)


def kernel(x_prompt, x_sample, state_hgrn, cache_gqa_k, cache_gqa_v, cache_na_k, cache_na_v, cache_mla_ckv, cache_mla_krope, c, c_ctx, w_ada, b_ada, w_in, hg_lb_logits, hg_gain, gq_q_gain, gq_k_gain, na_rpb, ml_q_a_gain, ml_kv_a_gain, ml_w_q_b, ml_w_kv_b, w_branch, w_out, w_ffn_in, w_ffn_out, final_gain):
    cond8 = jnp.concatenate([c_ctx[None, :], c, jnp.zeros((8 - 1 - DEC_BATCH, D_MODEL), F32)], axis=0)
    mods = _ada(cond8, w_ada, b_ada)

    lb = jnp.cumsum(jax.nn.softmax(hg_lb_logits.astype(F32), axis=0), axis=0)
    lb = lb - lb[:1]
    avg = jnp.asarray(np.kron(np.eye(512 // GQ_HD), np.full((GQ_HD, GQ_HD), 1.0 / GQ_HD)), BF16)
    tabs = _rope_tables()
    w_in_t = jnp.swapaxes(w_in, 1, 2).reshape(DEPTH * w_in.shape[2], D_MODEL)

    gqk_c = cache_gqa_k.reshape(DEC_BATCH * DEPTH * PAST_LEN, GQ_KV_HEADS * GQ_HD)
    gqv_c = cache_gqa_v.reshape(DEC_BATCH * DEPTH * PAST_LEN, GQ_KV_HEADS * GQ_HD)
    nak_c = cache_na_k.transpose(0, 1, 3, 4, 2).reshape(DEC_BATCH * DEPTH * NA_HEADS * NA_HD, PAST_LEN)
    nav_c = cache_na_v.transpose(0, 1, 3, 4, 2).reshape(DEC_BATCH * DEPTH * NA_HEADS * NA_HD, PAST_LEN)
    mckv_c = cache_mla_ckv.reshape(DEC_BATCH * DEPTH * PAST_LEN, ML_KV_RANK)
    mkr_c = jnp.pad(cache_mla_krope.reshape(DEC_BATCH * DEPTH * PAST_LEN, ML_ROPE),
                    ((0, 0), (0, LANE - ML_ROPE)))

    xp = x_prompt.reshape(M_ROWS, D_MODEL)
    xs = x_sample.reshape(M_ROWS, D_MODEL)
    new = []
    ymix_ctx = []
    for l in range(DEPTH):
        lw = _layer_weights(l, gq_q_gain, gq_k_gain, ml_q_a_gain, ml_kv_a_gain, ml_w_q_b,
                            ml_w_kv_b, w_branch, w_out, hg_gain, avg)
        last = l == DEPTH - 1
        mod = mods[l]

        ymix = _inproj(xp, mod, w_in_t, l, False, gate=False)
        gates = _inproj(xp, mod, w_in_t, l, False, gate=True)
        qb, kb, qd, ckv, kd, vd = _prep(ymix, lw, tabs, False)
        out_a, st = _hgrn(ymix, lb[l], lw["hg_gain"], BATCH, SEQ, None, l)
        out_b = _attention((qb, 512, 0, 0),
                           [(_seg(kb, LANE, 0, SEQ, 1, 0), _seg(ymix, LANE, COL_GV, SEQ, 1, 0))],
                           _plan_gqa(), GQ_SCALE, BATCH, SEQ, "gqa_ctx")
        out_c = _attention((ymix, 512, COL_NQ, 0),
                           [(_seg(ymix, 512, COL_NK, SEQ, 1, 0), _seg(ymix, 512, COL_NV, SEQ, 1, 0))],
                           _plan_na(), NA_SCALE, BATCH, SEQ, "na_ctx")
        out_d = _attention((qd, ML_HEADS * LANE, 0, 0),
                           [(_seg(kd, ML_HEADS * LANE, 0, SEQ, 1, 0), _seg(vd, 512, 0, SEQ, 1, 0))],
                           _plan_mla(), ML_SCALE, BATCH, SEQ, "mla_ctx")
        xp = _merge((out_a, out_b, out_c, out_d), gates, lw["w_branch"], lw["w_out"], xp, mod, False)
        xp = _ffn(xp, mod, w_ffn_in, w_ffn_out, l, False, final_gain if last else None)
        new.append((
            st,
            kb.reshape(BATCH, SEQ, GQ_KV_HEADS, GQ_HD),
            ymix[:, COL_GV * LANE:(COL_GV + 1) * LANE].reshape(BATCH, SEQ, GQ_KV_HEADS, GQ_HD),
            ckv.reshape(BATCH, SEQ, ML_KV_RANK),
            ymix[:, COL_MKR * LANE:COL_MKR * LANE + ML_ROPE].reshape(BATCH, SEQ, ML_ROPE),
        ))
        ymix_ctx.append(ymix)

        ymix = _inproj(xs, mod, w_in_t, l, True, gate=False)
        gates = _inproj(xs, mod, w_in_t, l, True, gate=True)
        qb, kb, qd, ckv, kd, vd = _prep(ymix, lw, tabs, True)
        kd_c, vd_c = _mla_cache(mckv_c, mkr_c, lw, l)
        out_a, = _hgrn(ymix, lb[l], lw["hg_gain"], DEC_BATCH, DEC_SEQ, state_hgrn, l)
        out_b = _attention((qb, 512, 0, 0),
                           [(_seg(gqk_c, LANE, 0, PAST_LEN, DEPTH, l), _seg(gqv_c, LANE, 0, PAST_LEN, DEPTH, l)),
                            (_seg(kb, LANE, 0, DEC_SEQ, 1, 0), _seg(ymix, LANE, COL_GV, DEC_SEQ, 1, 0))],
                           _plan_gqa(), GQ_SCALE, DEC_BATCH, DEC_SEQ, "gqa_lat")
        out_c = _na_latent(ymix, nak_c, nav_c, _na_bias_tables(na_rpb[l]), l)
        out_d = _attention((qd, ML_HEADS * LANE, 0, 0),
                           [(_seg(kd_c, ML_HEADS * LANE, 0, PAST_LEN, 1, 0), _seg(vd_c, 512, 0, PAST_LEN, 1, 0)),
                            (_seg(kd, ML_HEADS * LANE, 0, DEC_SEQ, 1, 0), _seg(vd, 512, 0, DEC_SEQ, 1, 0))],
                           _plan_mla(), ML_SCALE, DEC_BATCH, DEC_SEQ, "mla_lat", tq=512)
        xs = _merge((out_a, out_b, out_c, out_d), gates, lw["w_branch"], lw["w_out"], xs, mod, True)
        xs = _ffn(xs, mod, w_ffn_in, w_ffn_out, l, True, final_gain if last else None)

    y_prompt = xp.reshape(BATCH, SEQ, D_MODEL)
    y_sample = xs.reshape(DEC_BATCH, DEC_SEQ, D_MODEL)
    state, gqa_k, gqa_v, ckv_new, krope_new = (jnp.stack([n[i] for n in new], axis=1) for i in range(5))
    na_k, na_v = _na_cache_out(ymix_ctx)
    return (y_prompt, y_sample, state, gqa_k, gqa_v, na_k, na_v, ckv_new, krope_new)
```

```python
import functools

import numpy as np
import jax
import jax.numpy as jnp
from jax import lax
from jax.experimental import pallas as pl
from jax.experimental.pallas import tpu as pltpu

F32 = jnp.float32
BF16 = jnp.bfloat16

D_MODEL = 1024
BATCH = 16
SEQ = 256
DEPTH = 2
DEC_BATCH = 4
DEC_SEQ = 1024
PAST_LEN = 512
GRID_W = 64
EPS = 1e-6
ROPE_THETA = 10000.0
N_BRANCH = 4
BRANCH_W = 512
HG_HEADS = 4
HG_DK = 128
HG_DV = 128
GQ_HEADS = 8
GQ_KV_HEADS = 2
GQ_HD = 64
NA_HEADS = 8
NA_HD = 64
NA_WIN_R = 8
NA_WIN_C = 16
ML_HEADS = 8
ML_NOPE = 64
ML_ROPE = 32
ML_V = 64
ML_Q_RANK = 256
ML_KV_RANK = 128
FFN_HIDDEN = 2816
GQ_SCALE = GQ_HD ** -0.5
NA_SCALE = NA_HD ** -0.5
ML_SCALE = (ML_NOPE + ML_ROPE) ** -0.5

M_ROWS = BATCH * SEQ
assert M_ROWS == DEC_BATCH * DEC_SEQ

LANE = 128
HALF = 64
MIX_W = 5376
IN_MIX = 5280
GATE_W = N_BRANCH * D_MODEL
HG_CHUNK = 16
NEG = -1e30
LOG2E = 1.4426950408889634
VMEM_LIMIT = 56 * 1024 * 1024

COL_HQ, COL_HFF, COL_HFB, COL_HI, COL_HG, COL_GQ, COL_NQ, COL_NK, COL_NV = range(9)
COL_MQA = 18
COL_GK, COL_GV, COL_MKVA, COL_MKR = 38, 39, 40, 41


def _dot(a, b):
    return jnp.dot(a, b, preferred_element_type=F32)


def _dot_nt(a, b):
    return lax.dot_general(a, b, (((1,), (1,)), ((), ())), preferred_element_type=F32)


def _dot_tn(a, b):
    return lax.dot_general(a, b, (((0,), (0,)), ((), ())), preferred_element_type=F32)


def _split3(x):
    x1 = x.astype(BF16)
    r1 = x - x1.astype(F32)
    x2 = r1.astype(BF16)
    x3 = (r1 - x2.astype(F32)).astype(BF16)
    return x1, x2, x3


def _dot_exact_lhs(a_bf16, x):
    x1, x2, x3 = _split3(x)
    return (_dot(a_bf16, x3) + _dot(a_bf16, x2)) + _dot(a_bf16, x1)


def _dot_exact_rhs(x, b_bf16):
    x1, x2, x3 = _split3(x)
    return (_dot(x3, b_bf16) + _dot(x2, b_bf16)) + _dot(x1, b_bf16)


def _rms(x):
    return x * lax.rsqrt(jnp.mean(x * x, axis=-1, keepdims=True) + EPS)


def _silu(x):
    return x * jax.nn.sigmoid(x)


def _aligned(x, m):
    return x if isinstance(x, int) else pl.multiple_of(x, m)


def _params(*sem):
    return pltpu.CompilerParams(dimension_semantics=sem, vmem_limit_bytes=VMEM_LIMIT)


def _mod_row(latent, tm):
    if latent:
        return lambda i: 1 + (i * tm) // DEC_SEQ
    return lambda i: 0


def _ada_kernel(c_ref, w_ref, b_ref, o_ref):
    c = c_ref[...]
    o_ref[...] = _dot(_silu(c).astype(BF16), w_ref[...].astype(BF16)) + b_ref[...]


def _ada(cond8, w_ada, b_ada):
    tn = 1536
    out = pl.pallas_call(
        _ada_kernel,
        grid=(DEPTH, 6 * D_MODEL // tn),
        in_specs=[
            pl.BlockSpec((8, D_MODEL), lambda l, j: (0, 0)),
            pl.BlockSpec((None, D_MODEL, tn), lambda l, j: (l, 0, j)),
            pl.BlockSpec((None, 1, tn), lambda l, j: (l, 0, j)),
        ],
        out_specs=pl.BlockSpec((None, 8, tn), lambda l, j: (l, 0, j)),
        out_shape=jax.ShapeDtypeStruct((DEPTH, 8, 6 * D_MODEL), F32),
        compiler_params=_params("parallel", "parallel"),
        name="ada",
    )(cond8, w_ada, b_ada.reshape(DEPTH, 1, 6 * D_MODEL))
    return out.reshape(DEPTH, 8, 6, 1, D_MODEL)


IN_PIECE = 256
IN_TM = 1024


def _mix_source(t):
    blk = jnp.where(t < 12, t, jnp.where(t < 18, t + 1, jnp.where(t == 18, 19, jnp.where(t == 19, 12, 20))))
    return blk * IN_PIECE


def _gate_source(t):
    return IN_MIX + t * IN_PIECE


def _inproj_kernel(*refs, pieces, gate):
    x_ref, sc_ref, sh_ref = refs[:3]
    w_refs = refs[3:3 + pieces]
    o_ref, h_ref = refs[3 + pieces:]
    rows = pl.ds(pl.multiple_of(pl.program_id(1) * IN_TM, IN_TM), IN_TM)

    @pl.when(pl.program_id(0) == 0)
    def _():
        h = _rms(x_ref[...]) * (1.0 + sc_ref[...]) + sh_ref[...]
        h_ref[rows, :] = h.astype(BF16)

    h = h_ref[rows, :]
    for p, w_ref in enumerate(w_refs):
        y = _dot_nt(h, w_ref[...].astype(BF16))
        o_ref[:, p * IN_PIECE:(p + 1) * IN_PIECE] = jax.nn.sigmoid(y).astype(o_ref.dtype) if gate else y


def _inproj(x, mod, w_in_t, layer, latent, gate):
    width, tile, source = (GATE_W, 2048, _gate_source) if gate else (MIX_W, 1792, _mix_source)
    pieces = tile // IN_PIECE
    row = _mod_row(latent, IN_TM)
    first = lambda c, i: jnp.where(c == 0, i, 0)
    in_width = w_in_t.shape[0] // DEPTH
    piece = lambda p: pl.BlockSpec((pl.Element(IN_PIECE), pl.Element(D_MODEL)),
                                   lambda c, i: (pl.multiple_of(layer * in_width + source(c * pieces + p), 8), 0))
    return pl.pallas_call(
        functools.partial(_inproj_kernel, pieces=pieces, gate=gate),
        grid=(width // tile, M_ROWS // IN_TM),
        in_specs=[
            pl.BlockSpec((IN_TM, D_MODEL), lambda c, i: (first(c, i), 0)),
            pl.BlockSpec((None, None, 1, D_MODEL), lambda c, i: (row(first(c, i)), 1, 0, 0)),
            pl.BlockSpec((None, None, 1, D_MODEL), lambda c, i: (row(first(c, i)), 0, 0, 0)),
        ] + [piece(p) for p in range(pieces)],
        out_specs=pl.BlockSpec((IN_TM, tile), lambda c, i: (i, c)),
        out_shape=jax.ShapeDtypeStruct((M_ROWS, width), BF16 if gate else F32),
        scratch_shapes=[pltpu.VMEM((M_ROWS, D_MODEL), BF16)],
        compiler_params=_params("arbitrary", "arbitrary"),
        name="inproj_" + ("gate_" if gate else "mix_") + ("lat" if latent else "ctx"),
    )(x, mod, mod, *([w_in_t] * pieces))


def _rope(x, tab_ref, reps):
    w = x.shape[1]
    c = jnp.concatenate([tab_ref[0]] * reps, axis=1) if reps > 1 else tab_ref[0]
    se = jnp.concatenate([tab_ref[1]] * reps, axis=1) if reps > 1 else tab_ref[1]
    so = jnp.concatenate([tab_ref[2]] * reps, axis=1) if reps > 1 else tab_ref[2]
    return x * c + pltpu.roll(x, w - 1, 1) * se + pltpu.roll(x, 1, 1) * so


def _head_rms(x, gain, avg_bf16):
    ms = _dot_exact_rhs(x * x, avg_bf16)
    return x * lax.rsqrt(ms + EPS) * gain


def _prep_kernel(*refs, rope):
    (gq_ref, gk_ref, mqa_ref, mkva_ref, mkr_ref, gqg_ref, gkg_ref, mqg_ref, mkvg_ref,
     wqb_ref, wk_ref, wv_ref, avg_ref) = refs[:13]
    if rope:
        gtab_ref, mtab_ref = refs[13:15]
        outs = refs[15:]
    else:
        outs = refs[13:]
    qb_ref, kb_ref, qd_ref, ckv_ref, kd_ref, vd_ref = outs

    q = _head_rms(gq_ref[...], gqg_ref[...], avg_ref[...])
    k = _head_rms(gk_ref[...], gkg_ref[...], avg_ref[0:LANE, 0:LANE])
    qd = _dot((_rms(mqa_ref[...]) * mqg_ref[...]).astype(BF16), wqb_ref[...])
    ckv = _rms(mkva_ref[...]) * mkvg_ref[...]
    lane = lax.broadcasted_iota(jnp.int32, (1, LANE), 1)
    kr = jnp.where(lane < ML_ROPE, mkr_ref[...], 0.0)
    if rope:
        q = _rope(q, gtab_ref, GQ_HEADS * GQ_HD // LANE)
        k = _rope(k, gtab_ref, 1)
        qd = _rope(qd, mtab_ref, ML_HEADS)
        kr = _rope(kr, mtab_ref, 1)
    qb_ref[...] = q.astype(BF16)
    kb_ref[...] = k
    qd_ref[...] = qd.astype(BF16)
    ckv_ref[...] = ckv
    cb = ckv.astype(BF16)
    kd_ref[...] = (_dot(cb, wk_ref[...]) + jnp.concatenate([kr] * ML_HEADS, axis=1)).astype(BF16)
    vd_ref[...] = _dot(cb, wv_ref[...]).astype(BF16)


def _prep(ymix, lw, tabs, latent):
    tm = 512
    const = lambda i: (0, 0)
    in_specs = [
        pl.BlockSpec((tm, 512), lambda i: (i, COL_GQ)),
        pl.BlockSpec((tm, LANE), lambda i: (i, COL_GK)),
        pl.BlockSpec((tm, 256), lambda i: (i, COL_MQA)),
        pl.BlockSpec((tm, LANE), lambda i: (i, COL_MKVA)),
        pl.BlockSpec((tm, LANE), lambda i: (i, COL_MKR)),
        pl.BlockSpec((1, 512), const),
        pl.BlockSpec((1, LANE), const),
        pl.BlockSpec((1, 256), const),
        pl.BlockSpec((1, LANE), const),
        pl.BlockSpec((ML_Q_RANK, ML_HEADS * LANE), const),
        pl.BlockSpec((ML_KV_RANK, ML_HEADS * LANE), const),
        pl.BlockSpec((ML_KV_RANK, ML_HEADS * ML_V), const),
        pl.BlockSpec((512, 512), const),
    ]
    args = [ymix, ymix, ymix, ymix, ymix, lw["gq_q_gain"], lw["gq_k_gain"], lw["ml_q_gain"],
            lw["ml_kv_gain"], lw["w_qb"], lw["w_k"], lw["w_v"], lw["avg"]]
    if latent:
        per = DEC_SEQ // tm
        in_specs += [pl.BlockSpec((3, tm, LANE), lambda i: (0, i % per, 0))] * 2
        args += [tabs["gq"], tabs["ml"]]
    widths = (512, LANE, ML_HEADS * LANE, LANE, ML_HEADS * LANE, ML_HEADS * ML_V)
    return pl.pallas_call(
        functools.partial(_prep_kernel, rope=latent),
        grid=(M_ROWS // tm,),
        in_specs=in_specs,
        out_specs=[pl.BlockSpec((tm, w), lambda i: (i, 0)) for w in widths],
        out_shape=[jax.ShapeDtypeStruct((M_ROWS, w), dt)
                   for w, dt in zip(widths, (BF16, F32, BF16, F32, BF16, BF16))],
        compiler_params=_params("parallel"),
        name="prep_lat" if latent else "prep_ctx",
    )(*args)


def _mla_cache_kernel(ckv_ref, kr_ref, wk_ref, wv_ref, kd_ref, vd_ref):
    cb = ckv_ref[...].astype(BF16)
    kd_ref[...] = (_dot(cb, wk_ref[...]) + jnp.concatenate([kr_ref[...]] * ML_HEADS, axis=1)).astype(BF16)
    vd_ref[...] = _dot(cb, wv_ref[...]).astype(BF16)


def _mla_cache(ckv, kr_blk, lw, layer):
    rows = DEC_BATCH * PAST_LEN
    tm = PAST_LEN
    const = lambda i: (0, 0)
    return pl.pallas_call(
        _mla_cache_kernel,
        grid=(DEC_BATCH,),
        in_specs=[
            pl.BlockSpec((tm, LANE), lambda i: (i * DEPTH + layer, 0)),
            pl.BlockSpec((tm, LANE), lambda i: (i * DEPTH + layer, 0)),
            pl.BlockSpec((ML_KV_RANK, ML_HEADS * LANE), const),
            pl.BlockSpec((ML_KV_RANK, ML_HEADS * ML_V), const),
        ],
        out_specs=[pl.BlockSpec((tm, ML_HEADS * LANE), lambda i: (i, 0)),
                   pl.BlockSpec((tm, ML_HEADS * ML_V), lambda i: (i, 0))],
        out_shape=[jax.ShapeDtypeStruct((rows, ML_HEADS * LANE), BF16),
                   jax.ShapeDtypeStruct((rows, ML_HEADS * ML_V), BF16)],
        compiler_params=_params("parallel"),
        name="mla_cache",
    )(ckv, kr_blk, lw["w_k"], lw["w_v"])


def _softmax_pv(scores, values, transposed=None):
    transposed = transposed or [False] * len(values)
    m = scores[0].max(axis=-1, keepdims=True)
    for s in scores[1:]:
        m = jnp.maximum(m, s.max(axis=-1, keepdims=True))
    den = None
    out = None
    for s, v, vt in zip(scores, values, transposed):
        p = jnp.exp(s - m)
        d = p.sum(axis=-1, keepdims=True)
        o = _dot_nt(p.astype(BF16), v) if vt else _dot(p.astype(BF16), v)
        den = d if den is None else den + d
        out = o if out is None else out + o
    return out / den


def _softmax_pv_phased(scores, values):
    maxima = [s.max(axis=-1, keepdims=True) for s in scores]
    probs = [jnp.exp(s - m) for s, m in zip(scores, maxima)]
    dens = [p.sum(axis=-1, keepdims=True) for p in probs]
    outs = [_dot(p.astype(BF16), v) for p, v in zip(probs, values)]
    return [o / d for o, d in zip(outs, dens)]


def _lane_halves():
    lane = lax.broadcasted_iota(jnp.int32, (1, LANE), 1)
    return lane < HALF, lane >= HALF


def _attn_kernel(*refs, plan, nseg, scale, fold_scale):
    q_ref = refs[0]
    k_refs = [refs[1 + 2 * i] for i in range(nseg)]
    v_refs = [refs[2 + 2 * i] for i in range(nseg)]
    o_ref = refs[1 + 2 * nseg]
    tq = q_ref.shape[0]
    lo, hi = _lane_halves()
    cache = {}

    def block(kind, seg, blk, swap):
        key = (kind, seg, blk, swap)
        if key not in cache:
            ref = (k_refs if kind == "k" else v_refs)[seg]
            x = ref[:, blk * LANE:(blk + 1) * LANE]
            if swap:
                x = pltpu.roll(x, HALF, 1)
            cache[key] = x.astype(BF16)
        return cache[key]

    def queries(members):
        qs = []
        for qblk, qhalf, _, _ in members:
            q = q_ref[:, qblk * LANE:(qblk + 1) * LANE]
            if fold_scale:
                q = q * scale
            if qhalf is not None:
                q = jnp.where(lo if qhalf == 0 else hi, q, 0.0)
            qs.append(q)
        return (jnp.concatenate(qs, axis=0) if len(qs) > 1 else qs[0]).astype(BF16)

    def scores(q, kblk, swap):
        out = [_dot_nt(q, block("k", s, kblk, swap)) for s in range(nseg)]
        return out if fold_scale else [s * scale for s in out]

    if nseg == 1:
        all_scores = [scores(queries(members), kblk, swap)[0] for members, kblk, swap, _ in plan]
        outs = _softmax_pv_phased(all_scores, [block("v", 0, vblk, swap) for _, _, swap, vblk in plan])
    else:
        outs = []
        ready = scores(queries(plan[0][0]), plan[0][1], plan[0][2])
        for g, (_, _, swap, vblk) in enumerate(plan):
            if g + 1 < len(plan):
                following = scores(queries(plan[g + 1][0]), plan[g + 1][1], plan[g + 1][2])
            outs.append(_softmax_pv(ready, [block("v", s, vblk, swap) for s in range(nseg)]))
            ready = following
    parts = {}
    for (members, _, _, _), out in zip(plan, outs):
        for n, (_, _, oblk, ohalf) in enumerate(members):
            parts[(oblk, ohalf)] = out[n * tq:(n + 1) * tq]
    for oblk in sorted({key[0] for key in parts}):
        o_ref[:, oblk * LANE:(oblk + 1) * LANE] = jnp.where(lo, parts[(oblk, 0)], parts[(oblk, 1)]).astype(BF16)


def _plan_gqa():
    per_kv = GQ_HEADS // GQ_KV_HEADS
    plan = []
    for g in range(GQ_KV_HEADS):
        for half in range(2):
            heads = [h for h in range(g * per_kv, (g + 1) * per_kv) if h % 2 == half]
            plan.append((tuple((h // 2, half, h // 2, half) for h in heads), 0, half != g, 0))
    return tuple(plan)


def _plan_na():
    return tuple((((j, 0, j, 0), (j, 1, j, 1)), j, False, j) for j in range(NA_HEADS // 2))


def _plan_mla():
    return tuple((((h, None, h // 2, h % 2),), h, False, h // 2) for h in range(ML_HEADS))


def _is_pow2(x):
    return float(np.log2(x)).is_integer()


def _attention(q, segs, plan, scale, batch, tq_total, name, tq=256):
    tq = min(tq_total, tq)
    per = tq_total // tq
    q_arr, q_w, q_col, q_off = q
    in_specs = [pl.BlockSpec((tq, q_w), lambda b, i: (q_off + b * per + i, q_col))]
    args = [q_arr]
    for k, v in segs:
        for arr, w, col, rows, stride, off in (k, v):
            in_specs.append(pl.BlockSpec(
                (rows, w), functools.partial(lambda b, i, col, stride, off: (off + b * stride, col),
                                             col=col, stride=stride, off=off)))
            args.append(arr)
    out_w = 512
    return pl.pallas_call(
        functools.partial(_attn_kernel, plan=plan, nseg=len(segs), scale=scale, fold_scale=_is_pow2(scale)),
        grid=(batch, per),
        in_specs=in_specs,
        out_specs=pl.BlockSpec((tq, out_w), lambda b, i: (b * per + i, 0)),
        out_shape=jax.ShapeDtypeStruct((M_ROWS, out_w), BF16),
        compiler_params=_params("parallel", "parallel"),
        name=name,
    )(*args)


NA_DR = 2 * NA_WIN_R - 1
NA_DC = 2 * NA_WIN_C - 1
NA_ROWS = DEC_SEQ // GRID_W
NA_QROWS = 4
NA_WROWS = 12
NA_T_RIGHT = NA_DR - 1
NA_T_LEFT = NA_DR
NA_T_NONE = NA_DR + 1
NA_T_SIZE = NA_DR + 2
assert _is_pow2(NA_SCALE)


def _na_kernel(q_ref, kl_ref, vl_ref, kc_ref, vc_ref, tz_ref, o_ref, kc_scr, vc_scr):
    g = pl.program_id(1)

    @pl.when(g == 0)
    def _():
        kc_scr[...] = kc_ref[...].astype(BF16)
        vc_scr[...] = vc_ref[...].astype(BF16)

    w0 = (g // 2) * (NA_ROWS - NA_WROWS)
    k0 = pl.multiple_of(w0 * GRID_W, (NA_ROWS - NA_WROWS) * GRID_W)
    win = NA_WROWS * GRID_W
    nq = NA_QROWS * GRID_W
    entry = []
    for a in range(NA_QROWS):
        rq = g * NA_QROWS + a
        start = jnp.clip(rq - NA_WIN_R // 2, 0, NA_ROWS - NA_WIN_R)
        per_pair = []
        for i in range(NA_WROWS // 2):
            kr = w0 + 2 * i
            in_l = jnp.logical_and(kr >= start, kr < start + NA_WIN_R)
            in_r = jnp.logical_and(kr + 1 >= start, kr + 1 < start + NA_WIN_R)
            both = jnp.logical_and(in_l, in_r)
            d_l = kr - rq + NA_WIN_R - 1
            per_pair.append(jnp.where(both, d_l, jnp.where(in_r, NA_T_RIGHT, jnp.where(in_l, NA_T_LEFT, NA_T_NONE))))
        entry.append(per_pair)
    lo, hi = _lane_halves()
    def scores(j):
        sl = slice(j * LANE, (j + 1) * LANE)
        k_loc = kl_ref[pl.ds(k0, win), sl].astype(BF16)
        k_ctx_t = kc_scr[sl, :]
        q_pair = q_ref[:, sl] * NA_SCALE
        q = jnp.concatenate([jnp.where(lo, q_pair, 0.0), jnp.where(hi, q_pair, 0.0)], axis=0).astype(BF16)
        bias = jnp.concatenate(
            [jnp.concatenate([tz_ref[2 * j + p, e] for e in entry[a]], axis=1)
             for p in range(2) for a in range(NA_QROWS)], axis=0)
        return [_dot_nt(q, k_loc) + bias, _dot(q, k_ctx_t)]

    pairs = NA_HEADS // 2
    ready = scores(0)
    for j in range(pairs):
        sl = slice(j * LANE, (j + 1) * LANE)
        if j + 1 < pairs:
            following = scores(j + 1)
        v_loc = vl_ref[pl.ds(k0, win), sl].astype(BF16)
        out = _softmax_pv(ready, [v_loc, vc_scr[sl, :]], [False, True])
        o_ref[:, sl] = jnp.where(lo, out[:nq], out[nq:]).astype(BF16)
        ready = following


def _na_bias_kernel(rpb_ref, sel_ref, neg_ref, o_ref):
    o_ref[...] = _dot_exact_rhs(rpb_ref[...], sel_ref[...]) + neg_ref[...]


def _na_bias_tables(rpb):
    col = np.arange(GRID_W)
    col_start = np.clip(col - NA_WIN_C // 2, 0, GRID_W - NA_WIN_C)
    col_ok = (col[None, :] >= col_start[:, None]) & (col[None, :] < col_start[:, None] + NA_WIN_C)
    dc = col[None, :] - col[:, None] + NA_WIN_C - 1
    kpad = 32
    sel = (dc[None, :, :] == np.arange(kpad)[:, None, None]) & col_ok[None]
    sel = jnp.asarray(sel.reshape(kpad, GRID_W * GRID_W), BF16)
    neg = jnp.asarray(np.where(col_ok, 0.0, NEG).reshape(1, GRID_W * GRID_W), F32)
    rpb2 = jnp.pad(rpb.reshape(NA_HEADS * NA_DR, NA_DC), ((0, 0), (0, kpad - NA_DC)))
    n_rows = NA_HEADS * NA_DR
    full = lambda shape: pl.BlockSpec(shape, lambda i: (0, 0))
    t = pl.pallas_call(
        _na_bias_kernel,
        grid=(1,),
        in_specs=[full((n_rows, kpad)), full((kpad, GRID_W * GRID_W)), full((1, GRID_W * GRID_W))],
        out_specs=full((n_rows, GRID_W * GRID_W)),
        out_shape=jax.ShapeDtypeStruct((n_rows, GRID_W * GRID_W), F32),
        compiler_params=_params("arbitrary"),
        name="na_bias",
    )(rpb2, sel, neg)
    t = t.reshape(NA_HEADS, NA_DR, GRID_W, GRID_W)
    masked = jnp.full((NA_HEADS, 1, GRID_W, GRID_W), NEG, F32)
    first, last = NA_WIN_R // 2 - 1, NA_WIN_R // 2 + NA_WIN_R - 2
    return jnp.concatenate([
        jnp.concatenate([t[:, :-1], t[:, 1:]], axis=-1),
        jnp.concatenate([masked, t[:, first:first + 1]], axis=-1),
        jnp.concatenate([t[:, last:last + 1], masked], axis=-1),
        jnp.concatenate([masked, masked], axis=-1)], axis=1)


def _na_latent(ymix, cache_k, cache_v, tz, layer):
    groups = NA_ROWS // NA_QROWS
    nq = NA_QROWS * GRID_W
    return pl.pallas_call(
        _na_kernel,
        grid=(DEC_BATCH, groups),
        in_specs=[
            pl.BlockSpec((nq, 512), lambda b, g: (b * groups + g, COL_NQ)),
            pl.BlockSpec((DEC_SEQ, 512), lambda b, g: (b, COL_NK)),
            pl.BlockSpec((DEC_SEQ, 512), lambda b, g: (b, COL_NV)),
            pl.BlockSpec((NA_HEADS * NA_HD, PAST_LEN), lambda b, g: (b * DEPTH + layer, 0)),
            pl.BlockSpec((NA_HEADS * NA_HD, PAST_LEN), lambda b, g: (b * DEPTH + layer, 0)),
            pl.BlockSpec((NA_HEADS, NA_T_SIZE, GRID_W, LANE), lambda b, g: (0, 0, 0, 0)),
        ],
        out_specs=pl.BlockSpec((nq, 512), lambda b, g: (b * groups + g, 0)),
        out_shape=jax.ShapeDtypeStruct((M_ROWS, 512), BF16),
        scratch_shapes=[pltpu.VMEM((NA_HEADS * NA_HD, PAST_LEN), BF16)] * 2,
        compiler_params=_params("parallel", "arbitrary"),
        name="na_lat",
    )(ymix, ymix, ymix, cache_k, cache_v, tz)


def _na_cache_out_kernel(*refs):
    srcs, (ko_ref, vo_ref) = refs[:2 * DEPTH], refs[2 * DEPTH:]
    layer = pl.program_id(1)
    for l in range(DEPTH):
        @pl.when(layer == l)
        def _():
            ko_ref[...] = srcs[2 * l][...].T
            vo_ref[...] = srcs[2 * l + 1][...].T


def _na_cache_out(ymix_layers):
    rows = NA_HEADS * NA_HD
    in_specs, args = [], []
    for ymix in ymix_layers:
        for col in (COL_NK, COL_NV):
            in_specs.append(pl.BlockSpec((SEQ, 512), functools.partial(lambda b, l, col: (b, col), col=col)))
            args.append(ymix)
    out = pl.pallas_call(
        _na_cache_out_kernel,
        grid=(BATCH, DEPTH),
        in_specs=in_specs,
        out_specs=[pl.BlockSpec((rows, SEQ), lambda b, l: (b * DEPTH + l, 0))] * 2,
        out_shape=[jax.ShapeDtypeStruct((BATCH * DEPTH * rows, SEQ), F32)] * 2,
        compiler_params=_params("parallel", "arbitrary"),
        name="na_cache_out",
    )(*args)
    return tuple(o.reshape(BATCH, DEPTH, NA_HEADS, NA_HD, SEQ).transpose(0, 1, 4, 2, 3) for o in out)


HG_BLOCK = 128
HG_STATE_BLOCK = 64
HG_STATE_UNROLL = 8
HG_DEC_ROWS = 8


def _hgrn_kernel(*refs, seq, has_state):
    if has_state:
        (q_ref, ff_ref, fb_ref, v_ref, g_ref, lb_ref, gain_ref, s0_ref,
         o_ref, of_ref, ob_ref, qin_ref, kend_ref, dec_ref, st_ref) = refs
    else:
        (q_ref, ff_ref, fb_ref, v_ref, g_ref, lb_ref, gain_ref,
         o_ref, sout_ref, of_ref, ob_ref, qin_ref, kend_ref, dec_ref, st_ref) = refs
    c = HG_CHUNK
    hc = c // 2
    rb = HG_BLOCK
    per_block = rb // c
    n_chunks = seq // c
    width = HG_HEADS * HG_DK
    sub = HG_DEC_ROWS

    ri = lax.broadcasted_iota(jnp.int32, (rb, rb), 0)
    ci = lax.broadcasted_iota(jnp.int32, (rb, rb), 1)
    same = (ri // c) == (ci // c)
    tri_f = jnp.where(jnp.logical_and(same, ci <= ri), 1.0, 0.0).astype(BF16)
    tri_b = jnp.where(jnp.logical_and(same, ci >= ri), 1.0, 0.0).astype(BF16)
    ones = jnp.ones((HG_DK, HG_DV), BF16)
    rowid = lax.broadcasted_iota(jnp.int32, (c, HG_DV), 0)
    laneid = lax.broadcasted_iota(jnp.int32, (c, HG_DV), 1)
    laneid_half = lax.broadcasted_iota(jnp.int32, (hc, HG_DV), 1)
    o_refs = (of_ref, ob_ref)

    n_blocks = seq // rb

    def block_of(d, step):
        return step if d == 0 else n_blocks - 1 - step

    heads = [slice(h * HG_DK, (h + 1) * HG_DK) for h in range(HG_HEADS)]
    per_chunk = c * hc
    dirs = (0, 1)

    def pair_block(step):
        fwd = [True, False]
        pre_refs = (ff_ref, fb_ref)
        tris = (tri_f, tri_b)
        r0 = [_aligned(block_of(d, step) * rb, rb) for d in dirs]
        near_rows = [range(0, hc), range(hc, c)]
        far_rows = [range(hc, c), range(0, hc)]
        near = [slice(0, hc), slice(hc, c)]
        far = [slice(hc, c), slice(0, hc)]
        edge = [hc - 1, hc]
        q_all = [q_ref[pl.ds(r0[d], rb), :] * (HG_DK ** -0.5) for d in dirs]
        v16_all = [v_ref[pl.ds(r0[d], rb), :].astype(BF16) for d in dirs]
        k_all, log_f = [], []
        for d in dirs:
            lb = lb_ref[d:d + 1, :]
            f = lb + (1.0 - lb) * jax.nn.sigmoid(pre_refs[d][pl.ds(r0[d], rb), :])
            k_all.append(1.0 - f)
            log_f.append(jnp.log(f))
        b_all = [_dot_exact_lhs(tris[d], log_f[d]) * LOG2E for d in dirs]
        src_all = [b_all[d] - jnp.log(k_all[d]) * LOG2E for d in dirs]
        pairs, q_edge, k_edge = [], {}, {}
        zero_half = jnp.zeros((hc, width), F32)
        for d in dirs:
            blocks, q_in, k_end = [], [], []
            for m in range(per_block):
                rows = slice(m * c, (m + 1) * c)
                q, k, b, src = q_all[d][rows], k_all[d][rows], b_all[d][rows], src_all[d][rows]
                b_last = b[c - 1:c] if fwd[d] else b[0:1]
                q_in.append(q * jnp.exp2(b))
                k_end.append(k * jnp.exp2(b_last - b))
                dec_row = _aligned((block_of(d, step) * per_block + m) * sub, sub)
                dec_ref[d, pl.ds(dec_row, sub), :] = jnp.broadcast_to(jnp.exp2(b_last), (sub, width))
                halves = []
                for rows_s, part in ((near_rows[d], near[d]), (far_rows[d], far[d])):
                    q_part, b_part = q[part], b[part]
                    halves += [q_part * jnp.exp2(b_part - src[s:s + 1]) for s in rows_s]
                for i in range(0, c, 2):
                    blocks.append(jnp.concatenate(halves[i:i + 2], axis=0).astype(BF16))
                b_edge = b[edge[d]:edge[d] + 1]
                q_far = q[far[d]] * jnp.exp2(b[far[d]] - b_edge)
                k_near = jnp.exp2(b_edge - src[near[d]])
                q_edge[d, m] = jnp.concatenate([zero_half, q_far] if fwd[d] else [q_far, zero_half], axis=0).astype(BF16)
                k_edge[d, m] = jnp.concatenate([k_near, zero_half] if fwd[d] else [zero_half, k_near], axis=0).astype(BF16)
            qin_ref[d, pl.ds(r0[d], rb), :] = jnp.concatenate(q_in, axis=0).astype(BF16)
            kend_ref[d, pl.ds(r0[d], rb), :] = jnp.concatenate(k_end, axis=0).astype(BF16)
            pairs.append(jnp.concatenate(blocks, axis=0))
        sums = [[_dot(pairs[d][:, sl], ones) for sl in heads] for d in dirs]
        cells = [(d, h, m) for d in dirs for h in range(HG_HEADS) for m in range(per_block)]
        k_pad = jnp.zeros((HG_DK - c, HG_DK), BF16)
        across = {(d, h, m): _dot_nt(q_edge[d, m][:, heads[h]],
                                     jnp.concatenate([k_edge[d, m][:, heads[h]], k_pad], axis=0))
                  for d, h, m in cells}
        near_attn = {cell: jnp.zeros((hc, HG_DV), F32) for cell in cells}
        far_attn = {cell: jnp.zeros((hc, HG_DV), F32) for cell in cells}
        for n in range(hc):
            for d, h, m in cells:
                base = m * per_chunk + n * hc
                near_attn[d, h, m] = jnp.where(laneid_half == near_rows[d][n], sums[d][h][base:base + hc],
                                               near_attn[d, h, m])
                base += hc * hc
                far_attn[d, h, m] = jnp.where(laneid_half == far_rows[d][n], sums[d][h][base:base + hc],
                                              far_attn[d, h, m])
        v_pad = jnp.zeros((HG_DK - c, HG_DV), BF16)
        causal = [laneid <= rowid, laneid >= rowid]
        prods = {}
        for d, h, m in cells:
            halves = [near_attn[d, h, m], far_attn[d, h, m]] if fwd[d] else [far_attn[d, h, m], near_attn[d, h, m]]
            inside = jnp.where(causal[d], jnp.concatenate(halves, axis=0), 0.0)
            values = jnp.concatenate([v16_all[d][m * c:(m + 1) * c, heads[h]], v_pad], axis=0)
            prods[d, h, m] = _dot((inside + across[d, h, m]).astype(BF16), values)
        for d in dirs:
            o_refs[d][pl.ds(r0[d], rb), :] = jnp.concatenate(
                [jnp.concatenate([prods[d, h, m] for m in range(per_block)], axis=0) for h in range(HG_HEADS)], axis=1)


    def state_block(step, carry):
        rb, per_block = HG_STATE_BLOCK, HG_STATE_BLOCK // c
        block_of = lambda d, step: step if d == 0 else seq // rb - 1 - step
        order = [list(range(per_block)), list(reversed(range(per_block)))]
        r0 = [_aligned(block_of(d, step) * rb, rb) for d in dirs]
        q_in = [qin_ref[d, pl.ds(r0[d], rb), :] for d in dirs]
        k_end = [kend_ref[d, pl.ds(r0[d], rb), :] for d in dirs]
        v16 = [v_ref[pl.ds(r0[d], rb), :].astype(BF16) for d in dirs]
        rows = [slice(m * c, (m + 1) * c) for m in range(per_block)]
        q_state, k_state, k_cross, v_cross, total = [], [], [], [], []
        for d in dirs:
            dec_row = _aligned(block_of(d, step) * per_block * sub, sub)
            dec_all = dec_ref[d, pl.ds(dec_row, per_block * sub), :]
            dec = [dec_all[m * sub:m * sub + 1] for m in order[d]]
            before = [None] * per_block
            after = [None] * per_block
            for j in range(1, per_block):
                before[j] = dec[j - 1] if before[j - 1] is None else before[j - 1] * dec[j - 1]
            for j in range(per_block - 2, -1, -1):
                after[j] = dec[j + 1] if after[j + 1] is None else after[j + 1] * dec[j + 1]
            total.append(before[-1] * dec[-1])
            scale = lambda x, f: x if f is None else x * f
            qs, ks = [None] * per_block, [None] * per_block
            for j, m in enumerate(order[d]):
                qs[m] = scale(q_in[d][rows[m]], before[j])
                ks[m] = scale(k_end[d][rows[m]], after[j])
            q_state.append(jnp.concatenate(qs, axis=0).astype(BF16))
            k_state.append(jnp.concatenate(ks, axis=0).astype(BF16))
            kc, vc = [None], [None]
            for j in range(1, per_block):
                keys, between = [], None
                for i in range(j - 1, -1, -1):
                    keys.insert(0, scale(k_end[d][rows[order[d][i]]], between))
                    between = dec[i] if between is None else between * dec[i]
                kc.append(jnp.concatenate(keys, axis=0).astype(BF16))
                vc.append(jnp.concatenate([v16[d][rows[order[d][i]]] for i in range(j)], axis=0))
            k_cross.append(kc)
            v_cross.append(vc)
        q16 = [q_in[d].astype(BF16) for d in dirs]
        cells = [(d, h) for d in dirs for h in range(HG_HEADS)]
        kv = {(d, h): _dot_tn(v16[d][:, heads[h]], k_state[d][:, heads[h]]) for d, h in cells}
        attn = {(d, h, j): _dot_nt(q16[d][rows[order[d][j]], heads[h]], k_cross[d][j][:, heads[h]])
                for d, h in cells for j in range(1, per_block)}
        st = {(d, h): st_ref[d, h] for d, h in cells}
        from_state = {(d, h): _dot_nt(q_state[d][:, heads[h]], st[d, h].astype(BF16)) for d, h in cells}
        cross = {key: _dot(a.astype(BF16), v_cross[key[0]][key[2]][:, heads[key[1]]]) for key, a in attn.items()}
        for d, h in cells:
            st_ref[d, h] = st[d, h] * total[d][:, heads[h]] + kv[d, h]
        for d in dirs:
            cols = []
            for h in range(HG_HEADS):
                parts = [None] * per_block
                for j, m in enumerate(order[d]):
                    part = from_state[d, h][rows[m]]
                    parts[m] = part if j == 0 else part + cross[d, h, j]
                cols.append(jnp.concatenate(parts, axis=0))
            o_refs[d][pl.ds(r0[d], rb), :] += jnp.concatenate(cols, axis=1)
        return carry

    if has_state:
        for d in range(2):
            for h in range(HG_HEADS):
                st_ref[d, h] = s0_ref[d, h].T
    else:
        st_ref[...] = jnp.zeros(st_ref.shape, F32)

    def pair_step(i, carry):
        pair_block(i)
        return carry

    lax.fori_loop(0, n_blocks, pair_step, 0)
    lax.fori_loop(0, seq // HG_STATE_BLOCK, state_block, 0, unroll=HG_STATE_UNROLL)

    for h in range(HG_HEADS):
        sl = slice(h * HG_DV, (h + 1) * HG_DV)
        o = of_ref[:, sl] + ob_ref[:, sl]
        o_ref[:, sl] = (_rms(o) * gain_ref[:, sl] * _silu(g_ref[:, sl])).astype(BF16)
    if not has_state:
        for d in range(2):
            for h in range(HG_HEADS):
                sout_ref[d, h] = st_ref[d, h].T


def _hgrn(ymix, lb, gain, batch, seq, state, layer):
    width = HG_HEADS * HG_DK
    has_state = state is not None
    assert HG_DK == HG_DV
    st_shape = (2, HG_HEADS, HG_DK, HG_DV)
    in_specs = [pl.BlockSpec((seq, width), functools.partial(lambda b, col: (b, col), col=col))
                for col in (COL_HQ, COL_HFF, COL_HFB, COL_HI, COL_HG)]
    in_specs += [pl.BlockSpec((2, width), lambda b: (0, 0)), pl.BlockSpec((1, width), lambda b: (0, 0))]
    args = [ymix] * 5 + [lb, gain]
    out_specs = [pl.BlockSpec((seq, width), lambda b: (b, 0))]
    out_shape = [jax.ShapeDtypeStruct((M_ROWS, width), BF16)]
    if has_state:
        in_specs.append(pl.BlockSpec((None, None) + st_shape, lambda b: (b, layer, 0, 0, 0, 0)))
        args.append(state)
    else:
        out_specs.append(pl.BlockSpec((None,) + st_shape, lambda b: (b, 0, 0, 0, 0)))
        out_shape.append(jax.ShapeDtypeStruct((batch,) + st_shape, F32))
    return pl.pallas_call(
        functools.partial(_hgrn_kernel, seq=seq, has_state=has_state),
        grid=(batch,),
        in_specs=in_specs,
        out_specs=out_specs,
        out_shape=out_shape,
        scratch_shapes=[pltpu.VMEM((seq, width), F32), pltpu.VMEM((seq, width), F32),
                        pltpu.VMEM((2, seq, width), BF16), pltpu.VMEM((2, seq, width), BF16),
                        pltpu.VMEM((2, seq // HG_CHUNK * HG_DEC_ROWS, width), F32),
                        pltpu.VMEM(st_shape, F32)],
        compiler_params=_params("parallel"),
        name="hgrn_lat" if has_state else "hgrn_ctx",
    )(*args)


def _merge_kernel(oa_ref, ob_ref, oc_ref, od_ref, gt_ref, wb_ref, wo_ref, x_ref, g1_ref, out_ref):
    half = out_ref.shape[0] // 2
    rows = [pl.ds(0, half), pl.ds(half, half)]
    projected = [[_dot(o_ref[r, :], wb_ref[n]) for n, o_ref in enumerate((oa_ref, ob_ref, oc_ref, od_ref))]
                 for r in rows]
    for r, branch in zip(rows, projected):
        acc = None
        for n, bo in enumerate(branch):
            term = gt_ref[r, n * D_MODEL:(n + 1) * D_MODEL] * bo
            acc = term if acc is None else acc + term
        out_ref[r, :] = x_ref[r, :] + g1_ref[...] * _dot(acc.astype(BF16), wo_ref[...])


def _merge(branches, gates, w_branch, w_out, x, mod, latent):
    tm = 512
    row = _mod_row(latent, tm)
    tile = lambda w: pl.BlockSpec((tm, w), lambda i: (i, 0))
    return pl.pallas_call(
        _merge_kernel,
        grid=(M_ROWS // tm,),
        in_specs=[tile(BRANCH_W)] * N_BRANCH + [
            tile(GATE_W),
            pl.BlockSpec((N_BRANCH, BRANCH_W, D_MODEL), lambda i: (0, 0, 0)),
            pl.BlockSpec((D_MODEL, D_MODEL), lambda i: (0, 0)),
            tile(D_MODEL),
            pl.BlockSpec((None, None, 1, D_MODEL), lambda i: (row(i), 2, 0, 0)),
        ],
        out_specs=tile(D_MODEL),
        out_shape=jax.ShapeDtypeStruct((M_ROWS, D_MODEL), F32),
        compiler_params=_params("parallel"),
        name="merge_lat" if latent else "merge_ctx",
    )(*branches, gates, w_branch, w_out, x, mod)


FFN_CHUNK = 256
FFN_STEPS = FFN_HIDDEN // FFN_CHUNK
assert FFN_STEPS * FFN_CHUNK == FFN_HIDDEN


def _ffn_kernel(*refs, final):
    x_ref, sc_ref, sh_ref, g2_ref, wa_ref, wg_ref, wo_ref = refs[:7]
    if final:
        fg_ref, out_ref, h_ref, acc_ref = refs[7:]
    else:
        out_ref, h_ref, acc_ref = refs[7:]
    j = pl.program_id(1)

    @pl.when(j == 0)
    def _():
        h = _rms(x_ref[...]) * (1.0 + sc_ref[...]) + sh_ref[...]
        h_ref[...] = h.astype(BF16)
        acc_ref[...] = jnp.zeros(acc_ref.shape, F32)

    wg, wa, wo = wg_ref[...].astype(BF16), wa_ref[...].astype(BF16), wo_ref[...].astype(BF16)
    half = h_ref.shape[0] // 2
    rows = [pl.ds(0, half), pl.ds(half, half)]
    up = [(_dot(h_ref[r, :], wg), _dot(h_ref[r, :], wa)) for r in rows]
    for r, (gate, lin) in zip(rows, up):
        acc_ref[r, :] += _dot((_silu(gate) * lin).astype(BF16), wo)

    @pl.when(j == FFN_STEPS - 1)
    def _():
        y = x_ref[...] + g2_ref[...] * acc_ref[...]
        out_ref[...] = _rms(y) * fg_ref[...] if final else y


def _ffn(x, mod, w_in, w_out, layer, latent, final_gain):
    tm = 1024
    row = _mod_row(latent, tm)
    modspec = lambda which: pl.BlockSpec((None, None, 1, D_MODEL), lambda i, j: (row(i), which, 0, 0))
    final = final_gain is not None
    in_specs = [
        pl.BlockSpec((tm, D_MODEL), lambda i, j: (i, 0)),
        modspec(4), modspec(3), modspec(5),
        pl.BlockSpec((None, D_MODEL, FFN_CHUNK), lambda i, j: (layer, 0, j)),
        pl.BlockSpec((None, D_MODEL, FFN_CHUNK), lambda i, j: (layer, 0, FFN_STEPS + j)),
        pl.BlockSpec((None, FFN_CHUNK, D_MODEL), lambda i, j: (layer, j, 0)),
    ]
    args = [x, mod, mod, mod, w_in, w_in, w_out]
    if final:
        in_specs.append(pl.BlockSpec((1, D_MODEL), lambda i, j: (0, 0)))
        args.append(final_gain.reshape(1, D_MODEL))
    return pl.pallas_call(
        functools.partial(_ffn_kernel, final=final),
        grid=(M_ROWS // tm, FFN_STEPS),
        in_specs=in_specs,
        out_specs=pl.BlockSpec((tm, D_MODEL), lambda i, j: (i, 0)),
        out_shape=jax.ShapeDtypeStruct((M_ROWS, D_MODEL), F32),
        scratch_shapes=[pltpu.VMEM((tm, D_MODEL), BF16), pltpu.VMEM((tm, D_MODEL), F32)],
        compiler_params=_params("parallel", "arbitrary"),
        name="ffn_lat" if latent else "ffn_ctx",
    )(*args)


def _rope_tables():
    t = np.arange(DEC_SEQ)
    row = (t // GRID_W).astype(np.float64)[:, None]
    col = (t % GRID_W).astype(np.float64)[:, None]

    def angles(rot_dim):
        n_freq = rot_dim // 4
        inv_freq = ROPE_THETA ** (-np.arange(n_freq, dtype=np.float64) / n_freq)
        return np.concatenate([row * inv_freq, col * inv_freq], axis=-1)

    def expand(ang):
        cos = np.repeat(np.cos(ang), 2, axis=-1).astype(np.float32)
        sin = np.repeat(np.sin(ang), 2, axis=-1).astype(np.float32)
        even = (np.arange(cos.shape[-1]) % 2 == 0)[None, :]
        return cos, np.where(even, -sin, 0.0).astype(np.float32), np.where(even, 0.0, sin).astype(np.float32)

    gq = [np.concatenate([a, a], axis=-1) for a in expand(angles(GQ_HD))]
    ml = []
    for idx, a in enumerate(expand(angles(ML_ROPE))):
        fill = 1.0 if idx == 0 else 0.0
        ml.append(np.concatenate([a, np.full((DEC_SEQ, LANE - ML_ROPE), fill, np.float32)], axis=-1))
    return {"gq": jnp.asarray(np.stack(gq)), "ml": jnp.asarray(np.stack(ml))}


def _layer_weights(l, gq_q_gain, gq_k_gain, ml_q_a_gain, ml_kv_a_gain, ml_w_q_b, ml_w_kv_b,
                   w_branch, w_out, hg_gain, avg):
    pad = LANE - ML_NOPE - ML_ROPE
    qb = ml_w_q_b[l].reshape(ML_Q_RANK, ML_HEADS, ML_NOPE + ML_ROPE)
    qb = jnp.concatenate([qb[:, :, ML_NOPE:], qb[:, :, :ML_NOPE], jnp.zeros((ML_Q_RANK, ML_HEADS, pad), F32)],
                         axis=-1).reshape(ML_Q_RANK, ML_HEADS * LANE)
    kvb = ml_w_kv_b[l].reshape(ML_KV_RANK, ML_HEADS, ML_NOPE + ML_V)
    wk = jnp.pad(kvb[:, :, :ML_NOPE], ((0, 0), (0, 0), (ML_ROPE, pad))).reshape(ML_KV_RANK, ML_HEADS * LANE)
    wv = kvb[:, :, ML_NOPE:].reshape(ML_KV_RANK, ML_HEADS * ML_V)
    return {
        "gq_q_gain": jnp.tile(gq_q_gain[l], GQ_HEADS).reshape(1, -1),
        "gq_k_gain": jnp.tile(gq_k_gain[l], GQ_KV_HEADS).reshape(1, -1),
        "ml_q_gain": ml_q_a_gain[l].reshape(1, -1), "ml_kv_gain": ml_kv_a_gain[l].reshape(1, -1),
        "w_qb": qb.astype(BF16), "w_k": wk.astype(BF16), "w_v": wv.astype(BF16),
        "w_branch": w_branch[l].astype(BF16), "w_out": w_out[l].astype(BF16),
        "hg_gain": jnp.tile(hg_gain[l], HG_HEADS).reshape(1, -1), "avg": avg,
    }


def _seg(arr, width, col, rows, stride, off):
    return (arr, width, col, rows, stride, off)


def kernel(x_prompt, x_sample, state_hgrn, cache_gqa_k, cache_gqa_v, cache_na_k, cache_na_v, cache_mla_ckv, cache_mla_krope, c, c_ctx, w_ada, b_ada, w_in, hg_lb_logits, hg_gain, gq_q_gain, gq_k_gain, na_rpb, ml_q_a_gain, ml_kv_a_gain, ml_w_q_b, ml_w_kv_b, w_branch, w_out, w_ffn_in, w_ffn_out, final_gain):
    cond8 = jnp.concatenate([c_ctx[None, :], c, jnp.zeros((8 - 1 - DEC_BATCH, D_MODEL), F32)], axis=0)
    mods = _ada(cond8, w_ada, b_ada)

    lb = jnp.cumsum(jax.nn.softmax(hg_lb_logits.astype(F32), axis=0), axis=0)
    lb = lb - lb[:1]
    avg = jnp.asarray(np.kron(np.eye(512 // GQ_HD), np.full((GQ_HD, GQ_HD), 1.0 / GQ_HD)), BF16)
    tabs = _rope_tables()
    w_in_t = jnp.swapaxes(w_in, 1, 2).reshape(DEPTH * w_in.shape[2], D_MODEL)

    gqk_c = cache_gqa_k.reshape(DEC_BATCH * DEPTH * PAST_LEN, GQ_KV_HEADS * GQ_HD)
    gqv_c = cache_gqa_v.reshape(DEC_BATCH * DEPTH * PAST_LEN, GQ_KV_HEADS * GQ_HD)
    nak_c = cache_na_k.transpose(0, 1, 3, 4, 2).reshape(DEC_BATCH * DEPTH * NA_HEADS * NA_HD, PAST_LEN)
    nav_c = cache_na_v.transpose(0, 1, 3, 4, 2).reshape(DEC_BATCH * DEPTH * NA_HEADS * NA_HD, PAST_LEN)
    mckv_c = cache_mla_ckv.reshape(DEC_BATCH * DEPTH * PAST_LEN, ML_KV_RANK)
    mkr_c = jnp.pad(cache_mla_krope.reshape(DEC_BATCH * DEPTH * PAST_LEN, ML_ROPE),
                    ((0, 0), (0, LANE - ML_ROPE)))

    xp = x_prompt.reshape(M_ROWS, D_MODEL)
    xs = x_sample.reshape(M_ROWS, D_MODEL)
    new = []
    ymix_ctx = []
    for l in range(DEPTH):
        lw = _layer_weights(l, gq_q_gain, gq_k_gain, ml_q_a_gain, ml_kv_a_gain, ml_w_q_b,
                            ml_w_kv_b, w_branch, w_out, hg_gain, avg)
        last = l == DEPTH - 1
        mod = mods[l]

        ymix = _inproj(xp, mod, w_in_t, l, False, gate=False)
        gates = _inproj(xp, mod, w_in_t, l, False, gate=True)
        qb, kb, qd, ckv, kd, vd = _prep(ymix, lw, tabs, False)
        out_a, st = _hgrn(ymix, lb[l], lw["hg_gain"], BATCH, SEQ, None, l)
        out_b = _attention((qb, 512, 0, 0),
                           [(_seg(kb, LANE, 0, SEQ, 1, 0), _seg(ymix, LANE, COL_GV, SEQ, 1, 0))],
                           _plan_gqa(), GQ_SCALE, BATCH, SEQ, "gqa_ctx")
        out_c = _attention((ymix, 512, COL_NQ, 0),
                           [(_seg(ymix, 512, COL_NK, SEQ, 1, 0), _seg(ymix, 512, COL_NV, SEQ, 1, 0))],
                           _plan_na(), NA_SCALE, BATCH, SEQ, "na_ctx")
        out_d = _attention((qd, ML_HEADS * LANE, 0, 0),
                           [(_seg(kd, ML_HEADS * LANE, 0, SEQ, 1, 0), _seg(vd, 512, 0, SEQ, 1, 0))],
                           _plan_mla(), ML_SCALE, BATCH, SEQ, "mla_ctx")
        xp = _merge((out_a, out_b, out_c, out_d), gates, lw["w_branch"], lw["w_out"], xp, mod, False)
        xp = _ffn(xp, mod, w_ffn_in, w_ffn_out, l, False, final_gain if last else None)
        new.append((
            st,
            kb.reshape(BATCH, SEQ, GQ_KV_HEADS, GQ_HD),
            ymix[:, COL_GV * LANE:(COL_GV + 1) * LANE].reshape(BATCH, SEQ, GQ_KV_HEADS, GQ_HD),
            ckv.reshape(BATCH, SEQ, ML_KV_RANK),
            ymix[:, COL_MKR * LANE:COL_MKR * LANE + ML_ROPE].reshape(BATCH, SEQ, ML_ROPE),
        ))
        ymix_ctx.append(ymix)

        ymix = _inproj(xs, mod, w_in_t, l, True, gate=False)
        gates = _inproj(xs, mod, w_in_t, l, True, gate=True)
        qb, kb, qd, ckv, kd, vd = _prep(ymix, lw, tabs, True)
        kd_c, vd_c = _mla_cache(mckv_c, mkr_c, lw, l)
        out_a, = _hgrn(ymix, lb[l], lw["hg_gain"], DEC_BATCH, DEC_SEQ, state_hgrn, l)
        out_b = _attention((qb, 512, 0, 0),
                           [(_seg(gqk_c, LANE, 0, PAST_LEN, DEPTH, l), _seg(gqv_c, LANE, 0, PAST_LEN, DEPTH, l)),
                            (_seg(kb, LANE, 0, DEC_SEQ, 1, 0), _seg(ymix, LANE, COL_GV, DEC_SEQ, 1, 0))],
                           _plan_gqa(), GQ_SCALE, DEC_BATCH, DEC_SEQ, "gqa_lat", tq=512)
        out_c = _na_latent(ymix, nak_c, nav_c, _na_bias_tables(na_rpb[l]), l)
        out_d = _attention((qd, ML_HEADS * LANE, 0, 0),
                           [(_seg(kd_c, ML_HEADS * LANE, 0, PAST_LEN, 1, 0), _seg(vd_c, 512, 0, PAST_LEN, 1, 0)),
                            (_seg(kd, ML_HEADS * LANE, 0, DEC_SEQ, 1, 0), _seg(vd, 512, 0, DEC_SEQ, 1, 0))],
                           _plan_mla(), ML_SCALE, DEC_BATCH, DEC_SEQ, "mla_lat", tq=512)
        xs = _merge((out_a, out_b, out_c, out_d), gates, lw["w_branch"], lw["w_out"], xs, mod, True)
        xs = _ffn(xs, mod, w_ffn_in, w_ffn_out, l, True, final_gain if last else None)

    y_prompt = xp.reshape(BATCH, SEQ, D_MODEL)
    y_sample = xs.reshape(DEC_BATCH, DEC_SEQ, D_MODEL)
    state, gqa_k, gqa_v, ckv_new, krope_new = (jnp.stack([n[i] for n in new], axis=1) for i in range(5))
    na_k, na_v = _na_cache_out(ymix_ctx)
    return (y_prompt, y_sample, state, gqa_k, gqa_v, na_k, na_v, ckv_new, krope_new)
```

```python
import functools

import numpy as np
import jax
import jax.numpy as jnp
from jax import lax
from jax.experimental import pallas as pl
from jax.experimental.pallas import tpu as pltpu

F32 = jnp.float32
BF16 = jnp.bfloat16

D_MODEL = 1024
BATCH = 16
SEQ = 256
DEPTH = 2
DEC_BATCH = 4
DEC_SEQ = 1024
PAST_LEN = 512
GRID_W = 64
EPS = 1e-6
ROPE_THETA = 10000.0
N_BRANCH = 4
BRANCH_W = 512
HG_HEADS = 4
HG_DK = 128
HG_DV = 128
GQ_HEADS = 8
GQ_KV_HEADS = 2
GQ_HD = 64
NA_HEADS = 8
NA_HD = 64
NA_WIN_R = 8
NA_WIN_C = 16
ML_HEADS = 8
ML_NOPE = 64
ML_ROPE = 32
ML_V = 64
ML_Q_RANK = 256
ML_KV_RANK = 128
FFN_HIDDEN = 2816
GQ_SCALE = GQ_HD ** -0.5
NA_SCALE = NA_HD ** -0.5
ML_SCALE = (ML_NOPE + ML_ROPE) ** -0.5

M_ROWS = BATCH * SEQ
assert M_ROWS == DEC_BATCH * DEC_SEQ

LANE = 128
HALF = 64
MIX_W = 5376
IN_MIX = 5280
GATE_W = N_BRANCH * D_MODEL
HG_CHUNK = 16
NEG = -1e30
LOG2E = 1.4426950408889634
VMEM_LIMIT = 56 * 1024 * 1024

COL_HQ, COL_HFF, COL_HFB, COL_HI, COL_HG, COL_GQ, COL_NQ, COL_NK, COL_NV = range(9)
COL_MQA = 18
COL_GK, COL_GV, COL_MKVA, COL_MKR = 38, 39, 40, 41


def _dot(a, b):
    return jnp.dot(a, b, preferred_element_type=F32)


def _dot_nt(a, b):
    return lax.dot_general(a, b, (((1,), (1,)), ((), ())), preferred_element_type=F32)


def _dot_tn(a, b):
    return lax.dot_general(a, b, (((0,), (0,)), ((), ())), preferred_element_type=F32)


def _split3(x):
    x1 = x.astype(BF16)
    r1 = x - x1.astype(F32)
    x2 = r1.astype(BF16)
    x3 = (r1 - x2.astype(F32)).astype(BF16)
    return x1, x2, x3


def _dot_exact_lhs(a_bf16, x):
    x1, x2, x3 = _split3(x)
    return (_dot(a_bf16, x3) + _dot(a_bf16, x2)) + _dot(a_bf16, x1)


def _dot_exact_rhs(x, b_bf16):
    x1, x2, x3 = _split3(x)
    return (_dot(x3, b_bf16) + _dot(x2, b_bf16)) + _dot(x1, b_bf16)


def _rms(x):
    return x * lax.rsqrt(jnp.mean(x * x, axis=-1, keepdims=True) + EPS)


def _silu(x):
    return x * jax.nn.sigmoid(x)


def _aligned(x, m):
    return x if isinstance(x, int) else pl.multiple_of(x, m)


def _params(*sem):
    return pltpu.CompilerParams(dimension_semantics=sem, vmem_limit_bytes=VMEM_LIMIT)


def _mod_row(latent, tm):
    if latent:
        return lambda i: 1 + (i * tm) // DEC_SEQ
    return lambda i: 0


def _ada_kernel(c_ref, w_ref, b_ref, o_ref):
    c = c_ref[...]
    o_ref[...] = _dot(_silu(c).astype(BF16), w_ref[...].astype(BF16)) + b_ref[...]


def _ada(cond8, w_ada, b_ada):
    tn = 1536
    out = pl.pallas_call(
        _ada_kernel,
        grid=(DEPTH, 6 * D_MODEL // tn),
        in_specs=[
            pl.BlockSpec((8, D_MODEL), lambda l, j: (0, 0)),
            pl.BlockSpec((None, D_MODEL, tn), lambda l, j: (l, 0, j)),
            pl.BlockSpec((None, 1, tn), lambda l, j: (l, 0, j)),
        ],
        out_specs=pl.BlockSpec((None, 8, tn), lambda l, j: (l, 0, j)),
        out_shape=jax.ShapeDtypeStruct((DEPTH, 8, 6 * D_MODEL), F32),
        compiler_params=_params("parallel", "parallel"),
        name="ada",
    )(cond8, w_ada, b_ada.reshape(DEPTH, 1, 6 * D_MODEL))
    return out.reshape(DEPTH, 8, 6, 1, D_MODEL)


IN_PIECE = 256
IN_TM = 1024


def _mix_source(t):
    blk = jnp.where(t < 12, t, jnp.where(t < 18, t + 1, jnp.where(t == 18, 19, jnp.where(t == 19, 12, 20))))
    return blk * IN_PIECE


def _gate_source(t):
    return IN_MIX + t * IN_PIECE


def _inproj_kernel(*refs, pieces, gate):
    x_ref, sc_ref, sh_ref = refs[:3]
    w_refs = refs[3:3 + pieces]
    o_ref, h_ref = refs[3 + pieces:]
    rows = pl.ds(pl.multiple_of(pl.program_id(1) * IN_TM, IN_TM), IN_TM)

    @pl.when(pl.program_id(0) == 0)
    def _():
        h = _rms(x_ref[...]) * (1.0 + sc_ref[...]) + sh_ref[...]
        h_ref[rows, :] = h.astype(BF16)

    h = h_ref[rows, :]
    for p, w_ref in enumerate(w_refs):
        y = _dot_nt(h, w_ref[...].astype(BF16))
        o_ref[:, p * IN_PIECE:(p + 1) * IN_PIECE] = jax.nn.sigmoid(y).astype(o_ref.dtype) if gate else y


def _inproj(x, mod, w_in_t, layer, latent, gate):
    width, tile, source = (GATE_W, 2048, _gate_source) if gate else (MIX_W, 1792, _mix_source)
    pieces = tile // IN_PIECE
    row = _mod_row(latent, IN_TM)
    first = lambda c, i: jnp.where(c == 0, i, 0)
    in_width = w_in_t.shape[0] // DEPTH
    piece = lambda p: pl.BlockSpec((pl.Element(IN_PIECE), pl.Element(D_MODEL)),
                                   lambda c, i: (pl.multiple_of(layer * in_width + source(c * pieces + p), 8), 0))
    return pl.pallas_call(
        functools.partial(_inproj_kernel, pieces=pieces, gate=gate),
        grid=(width // tile, M_ROWS // IN_TM),
        in_specs=[
            pl.BlockSpec((IN_TM, D_MODEL), lambda c, i: (first(c, i), 0)),
            pl.BlockSpec((None, None, 1, D_MODEL), lambda c, i: (row(first(c, i)), 1, 0, 0)),
            pl.BlockSpec((None, None, 1, D_MODEL), lambda c, i: (row(first(c, i)), 0, 0, 0)),
        ] + [piece(p) for p in range(pieces)],
        out_specs=pl.BlockSpec((IN_TM, tile), lambda c, i: (i, c)),
        out_shape=jax.ShapeDtypeStruct((M_ROWS, width), BF16 if gate else F32),
        scratch_shapes=[pltpu.VMEM((M_ROWS, D_MODEL), BF16)],
        compiler_params=_params("arbitrary", "arbitrary"),
        name="inproj_" + ("gate_" if gate else "mix_") + ("lat" if latent else "ctx"),
    )(x, mod, mod, *([w_in_t] * pieces))


def _rope(x, tab_ref, reps):
    w = x.shape[1]
    c = jnp.concatenate([tab_ref[0]] * reps, axis=1) if reps > 1 else tab_ref[0]
    se = jnp.concatenate([tab_ref[1]] * reps, axis=1) if reps > 1 else tab_ref[1]
    so = jnp.concatenate([tab_ref[2]] * reps, axis=1) if reps > 1 else tab_ref[2]
    return x * c + pltpu.roll(x, w - 1, 1) * se + pltpu.roll(x, 1, 1) * so


def _head_rms(x, gain, avg_bf16):
    ms = _dot_exact_rhs(x * x, avg_bf16)
    return x * lax.rsqrt(ms + EPS) * gain


def _prep_kernel(*refs, rope):
    (gq_ref, gk_ref, mqa_ref, mkva_ref, mkr_ref, gqg_ref, gkg_ref, mqg_ref, mkvg_ref,
     wqb_ref, wk_ref, wv_ref, avg_ref) = refs[:13]
    if rope:
        gtab_ref, mtab_ref = refs[13:15]
        outs = refs[15:]
    else:
        outs = refs[13:]
    qb_ref, kb_ref, qd_ref, ckv_ref, kd_ref, vd_ref = outs

    q = _head_rms(gq_ref[...], gqg_ref[...], avg_ref[...])
    k = _head_rms(gk_ref[...], gkg_ref[...], avg_ref[0:LANE, 0:LANE])
    qd = _dot((_rms(mqa_ref[...]) * mqg_ref[...]).astype(BF16), wqb_ref[...])
    ckv = _rms(mkva_ref[...]) * mkvg_ref[...]
    lane = lax.broadcasted_iota(jnp.int32, (1, LANE), 1)
    kr = jnp.where(lane < ML_ROPE, mkr_ref[...], 0.0)
    if rope:
        q = _rope(q, gtab_ref, GQ_HEADS * GQ_HD // LANE)
        k = _rope(k, gtab_ref, 1)
        qd = _rope(qd, mtab_ref, ML_HEADS)
        kr = _rope(kr, mtab_ref, 1)
    qb_ref[...] = q.astype(BF16)
    kb_ref[...] = k
    qd_ref[...] = qd.astype(BF16)
    ckv_ref[...] = ckv
    cb = ckv.astype(BF16)
    kd_ref[...] = (_dot(cb, wk_ref[...]) + jnp.concatenate([kr] * ML_HEADS, axis=1)).astype(BF16)
    vd_ref[...] = _dot(cb, wv_ref[...]).astype(BF16)


def _prep(ymix, lw, tabs, latent):
    tm = 512
    const = lambda i: (0, 0)
    in_specs = [
        pl.BlockSpec((tm, 512), lambda i: (i, COL_GQ)),
        pl.BlockSpec((tm, LANE), lambda i: (i, COL_GK)),
        pl.BlockSpec((tm, 256), lambda i: (i, COL_MQA)),
        pl.BlockSpec((tm, LANE), lambda i: (i, COL_MKVA)),
        pl.BlockSpec((tm, LANE), lambda i: (i, COL_MKR)),
        pl.BlockSpec((1, 512), const),
        pl.BlockSpec((1, LANE), const),
        pl.BlockSpec((1, 256), const),
        pl.BlockSpec((1, LANE), const),
        pl.BlockSpec((ML_Q_RANK, ML_HEADS * LANE), const),
        pl.BlockSpec((ML_KV_RANK, ML_HEADS * LANE), const),
        pl.BlockSpec((ML_KV_RANK, ML_HEADS * ML_V), const),
        pl.BlockSpec((512, 512), const),
    ]
    args = [ymix, ymix, ymix, ymix, ymix, lw["gq_q_gain"], lw["gq_k_gain"], lw["ml_q_gain"],
            lw["ml_kv_gain"], lw["w_qb"], lw["w_k"], lw["w_v"], lw["avg"]]
    if latent:
        per = DEC_SEQ // tm
        in_specs += [pl.BlockSpec((3, tm, LANE), lambda i: (0, i % per, 0))] * 2
        args += [tabs["gq"], tabs["ml"]]
    widths = (512, LANE, ML_HEADS * LANE, LANE, ML_HEADS * LANE, ML_HEADS * ML_V)
    return pl.pallas_call(
        functools.partial(_prep_kernel, rope=latent),
        grid=(M_ROWS // tm,),
        in_specs=in_specs,
        out_specs=[pl.BlockSpec((tm, w), lambda i: (i, 0)) for w in widths],
        out_shape=[jax.ShapeDtypeStruct((M_ROWS, w), dt)
                   for w, dt in zip(widths, (BF16, F32, BF16, F32, BF16, BF16))],
        compiler_params=_params("parallel"),
        name="prep_lat" if latent else "prep_ctx",
    )(*args)


def _mla_cache_kernel(ckv_ref, kr_ref, wk_ref, wv_ref, kd_ref, vd_ref):
    cb = ckv_ref[...].astype(BF16)
    kd_ref[...] = (_dot(cb, wk_ref[...]) + jnp.concatenate([kr_ref[...]] * ML_HEADS, axis=1)).astype(BF16)
    vd_ref[...] = _dot(cb, wv_ref[...]).astype(BF16)


def _mla_cache(ckv, kr_blk, lw, layer):
    rows = DEC_BATCH * PAST_LEN
    tm = PAST_LEN
    const = lambda i: (0, 0)
    return pl.pallas_call(
        _mla_cache_kernel,
        grid=(DEC_BATCH,),
        in_specs=[
            pl.BlockSpec((tm, LANE), lambda i: (i * DEPTH + layer, 0)),
            pl.BlockSpec((tm, LANE), lambda i: (i * DEPTH + layer, 0)),
            pl.BlockSpec((ML_KV_RANK, ML_HEADS * LANE), const),
            pl.BlockSpec((ML_KV_RANK, ML_HEADS * ML_V), const),
        ],
        out_specs=[pl.BlockSpec((tm, ML_HEADS * LANE), lambda i: (i, 0)),
                   pl.BlockSpec((tm, ML_HEADS * ML_V), lambda i: (i, 0))],
        out_shape=[jax.ShapeDtypeStruct((rows, ML_HEADS * LANE), BF16),
                   jax.ShapeDtypeStruct((rows, ML_HEADS * ML_V), BF16)],
        compiler_params=_params("parallel"),
        name="mla_cache",
    )(ckv, kr_blk, lw["w_k"], lw["w_v"])


def _softmax_pv(scores, values, transposed=None):
    transposed = transposed or [False] * len(values)
    m = scores[0].max(axis=-1, keepdims=True)
    for s in scores[1:]:
        m = jnp.maximum(m, s.max(axis=-1, keepdims=True))
    den = None
    out = None
    for s, v, vt in zip(scores, values, transposed):
        p = jnp.exp(s - m)
        d = p.sum(axis=-1, keepdims=True)
        o = _dot_nt(p.astype(BF16), v) if vt else _dot(p.astype(BF16), v)
        den = d if den is None else den + d
        out = o if out is None else out + o
    return out / den


def _softmax_pv_phased(scores, values):
    maxima = [s.max(axis=-1, keepdims=True) for s in scores]
    probs = [jnp.exp(s - m) for s, m in zip(scores, maxima)]
    dens = [p.sum(axis=-1, keepdims=True) for p in probs]
    outs = [_dot(p.astype(BF16), v) for p, v in zip(probs, values)]
    return [o / d for o, d in zip(outs, dens)]


def _lane_halves():
    lane = lax.broadcasted_iota(jnp.int32, (1, LANE), 1)
    return lane < HALF, lane >= HALF


def _attn_kernel(*refs, plan, nseg, scale, fold_scale):
    q_ref = refs[0]
    k_refs = [refs[1 + 2 * i] for i in range(nseg)]
    v_refs = [refs[2 + 2 * i] for i in range(nseg)]
    o_ref = refs[1 + 2 * nseg]
    tq = q_ref.shape[0]
    lo, hi = _lane_halves()
    cache = {}

    def block(kind, seg, blk, swap):
        key = (kind, seg, blk, swap)
        if key not in cache:
            ref = (k_refs if kind == "k" else v_refs)[seg]
            x = ref[:, blk * LANE:(blk + 1) * LANE]
            if swap:
                x = pltpu.roll(x, HALF, 1)
            cache[key] = x.astype(BF16)
        return cache[key]

    def queries(members):
        qs = []
        for qblk, qhalf, _, _ in members:
            q = q_ref[:, qblk * LANE:(qblk + 1) * LANE]
            if fold_scale:
                q = q * scale
            if qhalf is not None:
                q = jnp.where(lo if qhalf == 0 else hi, q, 0.0)
            qs.append(q)
        return (jnp.concatenate(qs, axis=0) if len(qs) > 1 else qs[0]).astype(BF16)

    def scores(q, kblk, swap):
        out = [_dot_nt(q, block("k", s, kblk, swap)) for s in range(nseg)]
        return out if fold_scale else [s * scale for s in out]

    if nseg == 1:
        all_scores = [scores(queries(members), kblk, swap)[0] for members, kblk, swap, _ in plan]
        outs = _softmax_pv_phased(all_scores, [block("v", 0, vblk, swap) for _, _, swap, vblk in plan])
    else:
        outs = []
        ready = scores(queries(plan[0][0]), plan[0][1], plan[0][2])
        for g, (_, _, swap, vblk) in enumerate(plan):
            if g + 1 < len(plan):
                following = scores(queries(plan[g + 1][0]), plan[g + 1][1], plan[g + 1][2])
            outs.append(_softmax_pv(ready, [block("v", s, vblk, swap) for s in range(nseg)]))
            ready = following
    parts = {}
    for (members, _, _, _), out in zip(plan, outs):
        for n, (_, _, oblk, ohalf) in enumerate(members):
            parts[(oblk, ohalf)] = out[n * tq:(n + 1) * tq]
    for oblk in sorted({key[0] for key in parts}):
        o_ref[:, oblk * LANE:(oblk + 1) * LANE] = jnp.where(lo, parts[(oblk, 0)], parts[(oblk, 1)]).astype(BF16)


def _plan_gqa():
    per_kv = GQ_HEADS // GQ_KV_HEADS
    plan = []
    for g in range(GQ_KV_HEADS):
        for half in range(2):
            heads = [h for h in range(g * per_kv, (g + 1) * per_kv) if h % 2 == half]
            plan.append((tuple((h // 2, half, h // 2, half) for h in heads), 0, half != g, 0))
    return tuple(plan)


def _plan_na():
    return tuple((((j, 0, j, 0), (j, 1, j, 1)), j, False, j) for j in range(NA_HEADS // 2))


def _plan_mla():
    return tuple((((h, None, h // 2, h % 2),), h, False, h // 2) for h in range(ML_HEADS))


def _is_pow2(x):
    return float(np.log2(x)).is_integer()


def _attention(q, segs, plan, scale, batch, tq_total, name, tq=256):
    tq = min(tq_total, tq)
    per = tq_total // tq
    q_arr, q_w, q_col, q_off = q
    in_specs = [pl.BlockSpec((tq, q_w), lambda b, i: (q_off + b * per + i, q_col))]
    args = [q_arr]
    for k, v in segs:
        for arr, w, col, rows, stride, off in (k, v):
            in_specs.append(pl.BlockSpec(
                (rows, w), functools.partial(lambda b, i, col, stride, off: (off + b * stride, col),
                                             col=col, stride=stride, off=off)))
            args.append(arr)
    out_w = 512
    return pl.pallas_call(
        functools.partial(_attn_kernel, plan=plan, nseg=len(segs), scale=scale, fold_scale=_is_pow2(scale)),
        grid=(batch, per),
        in_specs=in_specs,
        out_specs=pl.BlockSpec((tq, out_w), lambda b, i: (b * per + i, 0)),
        out_shape=jax.ShapeDtypeStruct((M_ROWS, out_w), BF16),
        compiler_params=_params("parallel", "parallel"),
        name=name,
    )(*args)


NA_DR = 2 * NA_WIN_R - 1
NA_DC = 2 * NA_WIN_C - 1
NA_ROWS = DEC_SEQ // GRID_W
NA_QROWS = 4
NA_WROWS = 12
NA_T_RIGHT = NA_DR - 1
NA_T_LEFT = NA_DR
NA_T_NONE = NA_DR + 1
NA_T_SIZE = NA_DR + 2
assert _is_pow2(NA_SCALE)


def _na_kernel(q_ref, kl_ref, vl_ref, kc_ref, vc_ref, tz_ref, o_ref, kc_scr, vc_scr):
    g = pl.program_id(1)

    @pl.when(g == 0)
    def _():
        kc_scr[...] = kc_ref[...].astype(BF16)
        vc_scr[...] = vc_ref[...].astype(BF16)

    nq = NA_QROWS * GRID_W
    lo, hi = _lane_halves()

    def attend(w0, wrows):
        k0 = pl.multiple_of(w0 * GRID_W, (NA_ROWS - NA_WROWS) * GRID_W)
        win = wrows * GRID_W
        entry = []
        for a in range(NA_QROWS):
            rq = g * NA_QROWS + a
            start = jnp.clip(rq - NA_WIN_R // 2, 0, NA_ROWS - NA_WIN_R)
            per_pair = []
            for i in range(wrows // 2):
                kr = w0 + 2 * i
                in_l = jnp.logical_and(kr >= start, kr < start + NA_WIN_R)
                in_r = jnp.logical_and(kr + 1 >= start, kr + 1 < start + NA_WIN_R)
                both = jnp.logical_and(in_l, in_r)
                d_l = kr - rq + NA_WIN_R - 1
                per_pair.append(
                    jnp.where(both, d_l, jnp.where(in_r, NA_T_RIGHT, jnp.where(in_l, NA_T_LEFT, NA_T_NONE))))
            entry.append(per_pair)

        def scores(j):
            sl = slice(j * LANE, (j + 1) * LANE)
            k_loc = kl_ref[pl.ds(k0, win), sl].astype(BF16)
            k_ctx_t = kc_scr[sl, :]
            q_pair = q_ref[:, sl] * NA_SCALE
            q = jnp.concatenate([jnp.where(lo, q_pair, 0.0), jnp.where(hi, q_pair, 0.0)], axis=0).astype(BF16)
            bias = jnp.concatenate(
                [jnp.concatenate([tz_ref[2 * j + p, e] for e in entry[a]], axis=1)
                 for p in range(2) for a in range(NA_QROWS)], axis=0)
            return [_dot_nt(q, k_loc) + bias, _dot(q, k_ctx_t)]

        pairs = NA_HEADS // 2
        ready = scores(0)
        for j in range(pairs):
            sl = slice(j * LANE, (j + 1) * LANE)
            if j + 1 < pairs:
                following = scores(j + 1)
            v_loc = vl_ref[pl.ds(k0, win), sl].astype(BF16)
            out = _softmax_pv(ready, [v_loc, vc_scr[sl, :]], [False, True])
            o_ref[:, sl] = jnp.where(lo, out[:nq], out[nq:]).astype(BF16)
            ready = following

    last = NA_ROWS // NA_QROWS - 1
    edge = jnp.logical_or(g == 0, g == last)

    @pl.when(edge)
    def _():
        attend(jnp.where(g == 0, 0, NA_ROWS - NA_WIN_R), NA_WIN_R)

    @pl.when(jnp.logical_not(edge))
    def _():
        attend((g // 2) * (NA_ROWS - NA_WROWS), NA_WROWS)


def _na_bias_kernel(rpb_ref, sel_ref, neg_ref, o_ref):
    o_ref[...] = _dot_exact_rhs(rpb_ref[...], sel_ref[...]) + neg_ref[...]


def _na_bias_tables(rpb):
    col = np.arange(GRID_W)
    col_start = np.clip(col - NA_WIN_C // 2, 0, GRID_W - NA_WIN_C)
    col_ok = (col[None, :] >= col_start[:, None]) & (col[None, :] < col_start[:, None] + NA_WIN_C)
    dc = col[None, :] - col[:, None] + NA_WIN_C - 1
    kpad = 32
    sel = (dc[None, :, :] == np.arange(kpad)[:, None, None]) & col_ok[None]
    sel = jnp.asarray(sel.reshape(kpad, GRID_W * GRID_W), BF16)
    neg = jnp.asarray(np.where(col_ok, 0.0, NEG).reshape(1, GRID_W * GRID_W), F32)
    rpb2 = jnp.pad(rpb.reshape(NA_HEADS * NA_DR, NA_DC), ((0, 0), (0, kpad - NA_DC)))
    n_rows = NA_HEADS * NA_DR
    full = lambda shape: pl.BlockSpec(shape, lambda i: (0, 0))
    t = pl.pallas_call(
        _na_bias_kernel,
        grid=(1,),
        in_specs=[full((n_rows, kpad)), full((kpad, GRID_W * GRID_W)), full((1, GRID_W * GRID_W))],
        out_specs=full((n_rows, GRID_W * GRID_W)),
        out_shape=jax.ShapeDtypeStruct((n_rows, GRID_W * GRID_W), F32),
        compiler_params=_params("arbitrary"),
        name="na_bias",
    )(rpb2, sel, neg)
    t = t.reshape(NA_HEADS, NA_DR, GRID_W, GRID_W)
    masked = jnp.full((NA_HEADS, 1, GRID_W, GRID_W), NEG, F32)
    first, last = NA_WIN_R // 2 - 1, NA_WIN_R // 2 + NA_WIN_R - 2
    return jnp.concatenate([
        jnp.concatenate([t[:, :-1], t[:, 1:]], axis=-1),
        jnp.concatenate([masked, t[:, first:first + 1]], axis=-1),
        jnp.concatenate([t[:, last:last + 1], masked], axis=-1),
        jnp.concatenate([masked, masked], axis=-1)], axis=1)


def _na_latent(ymix, cache_k, cache_v, tz, layer):
    groups = NA_ROWS // NA_QROWS
    nq = NA_QROWS * GRID_W
    return pl.pallas_call(
        _na_kernel,
        grid=(DEC_BATCH, groups),
        in_specs=[
            pl.BlockSpec((nq, 512), lambda b, g: (b * groups + g, COL_NQ)),
            pl.BlockSpec((DEC_SEQ, 512), lambda b, g: (b, COL_NK)),
            pl.BlockSpec((DEC_SEQ, 512), lambda b, g: (b, COL_NV)),
            pl.BlockSpec((NA_HEADS * NA_HD, PAST_LEN), lambda b, g: (b * DEPTH + layer, 0)),
            pl.BlockSpec((NA_HEADS * NA_HD, PAST_LEN), lambda b, g: (b * DEPTH + layer, 0)),
            pl.BlockSpec((NA_HEADS, NA_T_SIZE, GRID_W, LANE), lambda b, g: (0, 0, 0, 0)),
        ],
        out_specs=pl.BlockSpec((nq, 512), lambda b, g: (b * groups + g, 0)),
        out_shape=jax.ShapeDtypeStruct((M_ROWS, 512), BF16),
        scratch_shapes=[pltpu.VMEM((NA_HEADS * NA_HD, PAST_LEN), BF16)] * 2,
        compiler_params=_params("parallel", "arbitrary"),
        name="na_lat",
    )(ymix, ymix, ymix, cache_k, cache_v, tz)


def _na_cache_out_kernel(*refs):
    srcs, (ko_ref, vo_ref) = refs[:2 * DEPTH], refs[2 * DEPTH:]
    layer = pl.program_id(1)
    for l in range(DEPTH):
        @pl.when(layer == l)
        def _():
            ko_ref[...] = srcs[2 * l][...].T
            vo_ref[...] = srcs[2 * l + 1][...].T


def _na_cache_out(ymix_layers):
    rows = NA_HEADS * NA_HD
    in_specs, args = [], []
    for ymix in ymix_layers:
        for col in (COL_NK, COL_NV):
            in_specs.append(pl.BlockSpec((SEQ, 512), functools.partial(lambda b, l, col: (b, col), col=col)))
            args.append(ymix)
    out = pl.pallas_call(
        _na_cache_out_kernel,
        grid=(BATCH, DEPTH),
        in_specs=in_specs,
        out_specs=[pl.BlockSpec((rows, SEQ), lambda b, l: (b * DEPTH + l, 0))] * 2,
        out_shape=[jax.ShapeDtypeStruct((BATCH * DEPTH * rows, SEQ), F32)] * 2,
        compiler_params=_params("parallel", "arbitrary"),
        name="na_cache_out",
    )(*args)
    return tuple(o.reshape(BATCH, DEPTH, NA_HEADS, NA_HD, SEQ).transpose(0, 1, 4, 2, 3) for o in out)


HG_BLOCK = 128
HG_STATE_BLOCK = 64
HG_STATE_UNROLL = 8
HG_DEC_ROWS = 8


def _hgrn_kernel(*refs, seq, has_state):
    if has_state:
        (q_ref, ff_ref, fb_ref, v_ref, g_ref, lb_ref, gain_ref, s0_ref,
         o_ref, of_ref, ob_ref, qin_ref, kend_ref, dec_ref, st_ref) = refs
    else:
        (q_ref, ff_ref, fb_ref, v_ref, g_ref, lb_ref, gain_ref,
         o_ref, sout_ref, of_ref, ob_ref, qin_ref, kend_ref, dec_ref, st_ref) = refs
    c = HG_CHUNK
    hc = c // 2
    rb = HG_BLOCK
    per_block = rb // c
    n_chunks = seq // c
    width = HG_HEADS * HG_DK
    sub = HG_DEC_ROWS

    ri = lax.broadcasted_iota(jnp.int32, (rb, rb), 0)
    ci = lax.broadcasted_iota(jnp.int32, (rb, rb), 1)
    same = (ri // c) == (ci // c)
    tri_f = jnp.where(jnp.logical_and(same, ci <= ri), 1.0, 0.0).astype(BF16)
    tri_b = jnp.where(jnp.logical_and(same, ci >= ri), 1.0, 0.0).astype(BF16)
    ones = jnp.ones((HG_DK, HG_DV), BF16)
    rowid = lax.broadcasted_iota(jnp.int32, (c, HG_DV), 0)
    laneid = lax.broadcasted_iota(jnp.int32, (c, HG_DV), 1)
    laneid_half = lax.broadcasted_iota(jnp.int32, (hc, HG_DV), 1)
    o_refs = (of_ref, ob_ref)

    n_blocks = seq // rb

    def block_of(d, step):
        return step if d == 0 else n_blocks - 1 - step

    heads = [slice(h * HG_DK, (h + 1) * HG_DK) for h in range(HG_HEADS)]
    per_chunk = c * hc
    dirs = (0, 1)

    def pair_block(step):
        fwd = [True, False]
        pre_refs = (ff_ref, fb_ref)
        tris = (tri_f, tri_b)
        r0 = [_aligned(block_of(d, step) * rb, rb) for d in dirs]
        near_rows = [range(0, hc), range(hc, c)]
        far_rows = [range(hc, c), range(0, hc)]
        near = [slice(0, hc), slice(hc, c)]
        far = [slice(hc, c), slice(0, hc)]
        edge = [hc - 1, hc]
        q_all = [q_ref[pl.ds(r0[d], rb), :] * (HG_DK ** -0.5) for d in dirs]
        v16_all = [v_ref[pl.ds(r0[d], rb), :].astype(BF16) for d in dirs]
        k_all, log_f = [], []
        for d in dirs:
            lb = lb_ref[d:d + 1, :]
            f = lb + (1.0 - lb) * jax.nn.sigmoid(pre_refs[d][pl.ds(r0[d], rb), :])
            k_all.append(1.0 - f)
            log_f.append(jnp.log(f))
        b_all = [_dot_exact_lhs(tris[d], log_f[d]) * LOG2E for d in dirs]
        src_all = [b_all[d] - jnp.log(k_all[d]) * LOG2E for d in dirs]
        pairs, q_edge, k_edge = [], {}, {}
        zero_half = jnp.zeros((hc, width), F32)
        for d in dirs:
            blocks, q_in, k_end = [], [], []
            for m in range(per_block):
                rows = slice(m * c, (m + 1) * c)
                q, k, b, src = q_all[d][rows], k_all[d][rows], b_all[d][rows], src_all[d][rows]
                b_last = b[c - 1:c] if fwd[d] else b[0:1]
                q_in.append(q * jnp.exp2(b))
                k_end.append(k * jnp.exp2(b_last - b))
                dec_row = _aligned((block_of(d, step) * per_block + m) * sub, sub)
                dec_ref[d, pl.ds(dec_row, sub), :] = jnp.broadcast_to(jnp.exp2(b_last), (sub, width))
                halves = []
                for rows_s, part in ((near_rows[d], near[d]), (far_rows[d], far[d])):
                    q_part, b_part = q[part], b[part]
                    halves += [q_part * jnp.exp2(b_part - src[s:s + 1]) for s in rows_s]
                for i in range(0, c, 2):
                    blocks.append(jnp.concatenate(halves[i:i + 2], axis=0).astype(BF16))
                b_edge = b[edge[d]:edge[d] + 1]
                q_far = q[far[d]] * jnp.exp2(b[far[d]] - b_edge)
                k_near = jnp.exp2(b_edge - src[near[d]])
                q_edge[d, m] = jnp.concatenate([zero_half, q_far] if fwd[d] else [q_far, zero_half], axis=0).astype(BF16)
                k_edge[d, m] = jnp.concatenate([k_near, zero_half] if fwd[d] else [zero_half, k_near], axis=0).astype(BF16)
            qin_ref[d, pl.ds(r0[d], rb), :] = jnp.concatenate(q_in, axis=0).astype(BF16)
            kend_ref[d, pl.ds(r0[d], rb), :] = jnp.concatenate(k_end, axis=0).astype(BF16)
            pairs.append(jnp.concatenate(blocks, axis=0))
        sums = [[_dot(pairs[d][:, sl], ones) for sl in heads] for d in dirs]
        cells = [(d, h, m) for d in dirs for h in range(HG_HEADS) for m in range(per_block)]
        k_pad = jnp.zeros((HG_DK - c, HG_DK), BF16)
        across = {(d, h, m): _dot_nt(q_edge[d, m][:, heads[h]],
                                     jnp.concatenate([k_edge[d, m][:, heads[h]], k_pad], axis=0))
                  for d, h, m in cells}
        near_attn = {cell: jnp.zeros((hc, HG_DV), F32) for cell in cells}
        far_attn = {cell: jnp.zeros((hc, HG_DV), F32) for cell in cells}
        for n in range(hc):
            for d, h, m in cells:
                base = m * per_chunk + n * hc
                near_attn[d, h, m] = jnp.where(laneid_half == near_rows[d][n], sums[d][h][base:base + hc],
                                               near_attn[d, h, m])
                base += hc * hc
                far_attn[d, h, m] = jnp.where(laneid_half == far_rows[d][n], sums[d][h][base:base + hc],
                                              far_attn[d, h, m])
        v_pad = jnp.zeros((HG_DK - c, HG_DV), BF16)
        causal = [laneid <= rowid, laneid >= rowid]
        prods = {}
        for d, h, m in cells:
            halves = [near_attn[d, h, m], far_attn[d, h, m]] if fwd[d] else [far_attn[d, h, m], near_attn[d, h, m]]
            inside = jnp.where(causal[d], jnp.concatenate(halves, axis=0), 0.0)
            values = jnp.concatenate([v16_all[d][m * c:(m + 1) * c, heads[h]], v_pad], axis=0)
            prods[d, h, m] = _dot((inside + across[d, h, m]).astype(BF16), values)
        for d in dirs:
            o_refs[d][pl.ds(r0[d], rb), :] = jnp.concatenate(
                [jnp.concatenate([prods[d, h, m] for m in range(per_block)], axis=0) for h in range(HG_HEADS)], axis=1)


    def state_block(step, carry):
        rb, per_block = HG_STATE_BLOCK, HG_STATE_BLOCK // c
        block_of = lambda d, step: step if d == 0 else seq // rb - 1 - step
        order = [list(range(per_block)), list(reversed(range(per_block)))]
        r0 = [_aligned(block_of(d, step) * rb, rb) for d in dirs]
        q_in = [qin_ref[d, pl.ds(r0[d], rb), :] for d in dirs]
        k_end = [kend_ref[d, pl.ds(r0[d], rb), :] for d in dirs]
        v16 = [v_ref[pl.ds(r0[d], rb), :].astype(BF16) for d in dirs]
        rows = [slice(m * c, (m + 1) * c) for m in range(per_block)]
        q_state, k_state, k_cross, v_cross, total = [], [], [], [], []
        for d in dirs:
            dec_row = _aligned(block_of(d, step) * per_block * sub, sub)
            dec_all = dec_ref[d, pl.ds(dec_row, per_block * sub), :]
            dec = [dec_all[m * sub:m * sub + 1] for m in order[d]]
            before = [None] * per_block
            after = [None] * per_block
            for j in range(1, per_block):
                before[j] = dec[j - 1] if before[j - 1] is None else before[j - 1] * dec[j - 1]
            for j in range(per_block - 2, -1, -1):
                after[j] = dec[j + 1] if after[j + 1] is None else after[j + 1] * dec[j + 1]
            total.append(before[-1] * dec[-1])
            scale = lambda x, f: x if f is None else x * f
            qs, ks = [None] * per_block, [None] * per_block
            for j, m in enumerate(order[d]):
                qs[m] = scale(q_in[d][rows[m]], before[j])
                ks[m] = scale(k_end[d][rows[m]], after[j])
            q_state.append(jnp.concatenate(qs, axis=0).astype(BF16))
            k_state.append(jnp.concatenate(ks, axis=0).astype(BF16))
            kc, vc = [None], [None]
            for j in range(1, per_block):
                keys, between = [], None
                for i in range(j - 1, -1, -1):
                    keys.insert(0, scale(k_end[d][rows[order[d][i]]], between))
                    between = dec[i] if between is None else between * dec[i]
                kc.append(jnp.concatenate(keys, axis=0).astype(BF16))
                vc.append(jnp.concatenate([v16[d][rows[order[d][i]]] for i in range(j)], axis=0))
            k_cross.append(kc)
            v_cross.append(vc)
        q16 = [q_in[d].astype(BF16) for d in dirs]
        cells = [(d, h) for d in dirs for h in range(HG_HEADS)]
        kv = {(d, h): _dot_tn(v16[d][:, heads[h]], k_state[d][:, heads[h]]) for d, h in cells}
        attn = {(d, h, j): _dot_nt(q16[d][rows[order[d][j]], heads[h]], k_cross[d][j][:, heads[h]])
                for d, h in cells for j in range(1, per_block)}
        st = {(d, h): st_ref[d, h] for d, h in cells}
        from_state = {(d, h): _dot_nt(q_state[d][:, heads[h]], st[d, h].astype(BF16)) for d, h in cells}
        cross = {key: _dot(a.astype(BF16), v_cross[key[0]][key[2]][:, heads[key[1]]]) for key, a in attn.items()}
        for d, h in cells:
            st_ref[d, h] = st[d, h] * total[d][:, heads[h]] + kv[d, h]
        for d in dirs:
            cols = []
            for h in range(HG_HEADS):
                parts = [None] * per_block
                for j, m in enumerate(order[d]):
                    part = from_state[d, h][rows[m]]
                    parts[m] = part if j == 0 else part + cross[d, h, j]
                cols.append(jnp.concatenate(parts, axis=0))
            o_refs[d][pl.ds(r0[d], rb), :] += jnp.concatenate(cols, axis=1)
        return carry

    if has_state:
        for d in range(2):
            for h in range(HG_HEADS):
                st_ref[d, h] = s0_ref[d, h].T
    else:
        st_ref[...] = jnp.zeros(st_ref.shape, F32)

    def pair_step(i, carry):
        pair_block(i)
        return carry

    lax.fori_loop(0, n_blocks, pair_step, 0)
    lax.fori_loop(0, seq // HG_STATE_BLOCK, state_block, 0, unroll=HG_STATE_UNROLL)

    for h in range(HG_HEADS):
        sl = slice(h * HG_DV, (h + 1) * HG_DV)
        o = of_ref[:, sl] + ob_ref[:, sl]
        o_ref[:, sl] = (_rms(o) * gain_ref[:, sl] * _silu(g_ref[:, sl])).astype(BF16)
    if not has_state:
        for d in range(2):
            for h in range(HG_HEADS):
                sout_ref[d, h] = st_ref[d, h].T


def _hgrn(ymix, lb, gain, batch, seq, state, layer):
    width = HG_HEADS * HG_DK
    has_state = state is not None
    assert HG_DK == HG_DV
    st_shape = (2, HG_HEADS, HG_DK, HG_DV)
    in_specs = [pl.BlockSpec((seq, width), functools.partial(lambda b, col: (b, col), col=col))
                for col in (COL_HQ, COL_HFF, COL_HFB, COL_HI, COL_HG)]
    in_specs += [pl.BlockSpec((2, width), lambda b: (0, 0)), pl.BlockSpec((1, width), lambda b: (0, 0))]
    args = [ymix] * 5 + [lb, gain]
    out_specs = [pl.BlockSpec((seq, width), lambda b: (b, 0))]
    out_shape = [jax.ShapeDtypeStruct((M_ROWS, width), BF16)]
    if has_state:
        in_specs.append(pl.BlockSpec((None, None) + st_shape, lambda b: (b, layer, 0, 0, 0, 0)))
        args.append(state)
    else:
        out_specs.append(pl.BlockSpec((None,) + st_shape, lambda b: (b, 0, 0, 0, 0)))
        out_shape.append(jax.ShapeDtypeStruct((batch,) + st_shape, F32))
    return pl.pallas_call(
        functools.partial(_hgrn_kernel, seq=seq, has_state=has_state),
        grid=(batch,),
        in_specs=in_specs,
        out_specs=out_specs,
        out_shape=out_shape,
        scratch_shapes=[pltpu.VMEM((seq, width), F32), pltpu.VMEM((seq, width), F32),
                        pltpu.VMEM((2, seq, width), BF16), pltpu.VMEM((2, seq, width), BF16),
                        pltpu.VMEM((2, seq // HG_CHUNK * HG_DEC_ROWS, width), F32),
                        pltpu.VMEM(st_shape, F32)],
        compiler_params=_params("parallel"),
        name="hgrn_lat" if has_state else "hgrn_ctx",
    )(*args)


def _merge_kernel(oa_ref, ob_ref, oc_ref, od_ref, gt_ref, wb_ref, wo_ref, x_ref, g1_ref, out_ref):
    half = out_ref.shape[0] // 2
    rows = [pl.ds(0, half), pl.ds(half, half)]
    projected = [[_dot(o_ref[r, :], wb_ref[n]) for n, o_ref in enumerate((oa_ref, ob_ref, oc_ref, od_ref))]
                 for r in rows]
    for r, branch in zip(rows, projected):
        acc = None
        for n, bo in enumerate(branch):
            term = gt_ref[r, n * D_MODEL:(n + 1) * D_MODEL] * bo
            acc = term if acc is None else acc + term
        out_ref[r, :] = x_ref[r, :] + g1_ref[...] * _dot(acc.astype(BF16), wo_ref[...])


def _merge(branches, gates, w_branch, w_out, x, mod, latent):
    tm = 512
    row = _mod_row(latent, tm)
    tile = lambda w: pl.BlockSpec((tm, w), lambda i: (i, 0))
    return pl.pallas_call(
        _merge_kernel,
        grid=(M_ROWS // tm,),
        in_specs=[tile(BRANCH_W)] * N_BRANCH + [
            tile(GATE_W),
            pl.BlockSpec((N_BRANCH, BRANCH_W, D_MODEL), lambda i: (0, 0, 0)),
            pl.BlockSpec((D_MODEL, D_MODEL), lambda i: (0, 0)),
            tile(D_MODEL),
            pl.BlockSpec((None, None, 1, D_MODEL), lambda i: (row(i), 2, 0, 0)),
        ],
        out_specs=tile(D_MODEL),
        out_shape=jax.ShapeDtypeStruct((M_ROWS, D_MODEL), F32),
        compiler_params=_params("parallel"),
        name="merge_lat" if latent else "merge_ctx",
    )(*branches, gates, w_branch, w_out, x, mod)


FFN_CHUNK = 256
FFN_STEPS = FFN_HIDDEN // FFN_CHUNK
assert FFN_STEPS * FFN_CHUNK == FFN_HIDDEN


def _ffn_kernel(*refs, final):
    x_ref, sc_ref, sh_ref, g2_ref, wa_ref, wg_ref, wo_ref = refs[:7]
    if final:
        fg_ref, out_ref, h_ref, acc_ref = refs[7:]
    else:
        out_ref, h_ref, acc_ref = refs[7:]
    j = pl.program_id(1)

    @pl.when(j == 0)
    def _():
        h = _rms(x_ref[...]) * (1.0 + sc_ref[...]) + sh_ref[...]
        h_ref[...] = h.astype(BF16)
        acc_ref[...] = jnp.zeros(acc_ref.shape, F32)

    wg, wa, wo = wg_ref[...].astype(BF16), wa_ref[...].astype(BF16), wo_ref[...].astype(BF16)
    half = h_ref.shape[0] // 2
    rows = [pl.ds(0, half), pl.ds(half, half)]
    up = [(_dot(h_ref[r, :], wg), _dot(h_ref[r, :], wa)) for r in rows]
    for r, (gate, lin) in zip(rows, up):
        acc_ref[r, :] += _dot((_silu(gate) * lin).astype(BF16), wo)

    @pl.when(j == FFN_STEPS - 1)
    def _():
        y = x_ref[...] + g2_ref[...] * acc_ref[...]
        out_ref[...] = _rms(y) * fg_ref[...] if final else y


def _ffn(x, mod, w_in, w_out, layer, latent, final_gain):
    tm = 1024
    row = _mod_row(latent, tm)
    modspec = lambda which: pl.BlockSpec((None, None, 1, D_MODEL), lambda i, j: (row(i), which, 0, 0))
    final = final_gain is not None
    in_specs = [
        pl.BlockSpec((tm, D_MODEL), lambda i, j: (i, 0)),
        modspec(4), modspec(3), modspec(5),
        pl.BlockSpec((None, D_MODEL, FFN_CHUNK), lambda i, j: (layer, 0, j)),
        pl.BlockSpec((None, D_MODEL, FFN_CHUNK), lambda i, j: (layer, 0, FFN_STEPS + j)),
        pl.BlockSpec((None, FFN_CHUNK, D_MODEL), lambda i, j: (layer, j, 0)),
    ]
    args = [x, mod, mod, mod, w_in, w_in, w_out]
    if final:
        in_specs.append(pl.BlockSpec((1, D_MODEL), lambda i, j: (0, 0)))
        args.append(final_gain.reshape(1, D_MODEL))
    return pl.pallas_call(
        functools.partial(_ffn_kernel, final=final),
        grid=(M_ROWS // tm, FFN_STEPS),
        in_specs=in_specs,
        out_specs=pl.BlockSpec((tm, D_MODEL), lambda i, j: (i, 0)),
        out_shape=jax.ShapeDtypeStruct((M_ROWS, D_MODEL), F32),
        scratch_shapes=[pltpu.VMEM((tm, D_MODEL), BF16), pltpu.VMEM((tm, D_MODEL), F32)],
        compiler_params=_params("parallel", "arbitrary"),
        name="ffn_lat" if latent else "ffn_ctx",
    )(*args)


def _rope_tables():
    t = np.arange(DEC_SEQ)
    row = (t // GRID_W).astype(np.float64)[:, None]
    col = (t % GRID_W).astype(np.float64)[:, None]

    def angles(rot_dim):
        n_freq = rot_dim // 4
        inv_freq = ROPE_THETA ** (-np.arange(n_freq, dtype=np.float64) / n_freq)
        return np.concatenate([row * inv_freq, col * inv_freq], axis=-1)

    def expand(ang):
        cos = np.repeat(np.cos(ang), 2, axis=-1).astype(np.float32)
        sin = np.repeat(np.sin(ang), 2, axis=-1).astype(np.float32)
        even = (np.arange(cos.shape[-1]) % 2 == 0)[None, :]
        return cos, np.where(even, -sin, 0.0).astype(np.float32), np.where(even, 0.0, sin).astype(np.float32)

    gq = [np.concatenate([a, a], axis=-1) for a in expand(angles(GQ_HD))]
    ml = []
    for idx, a in enumerate(expand(angles(ML_ROPE))):
        fill = 1.0 if idx == 0 else 0.0
        ml.append(np.concatenate([a, np.full((DEC_SEQ, LANE - ML_ROPE), fill, np.float32)], axis=-1))
    return {"gq": jnp.asarray(np.stack(gq)), "ml": jnp.asarray(np.stack(ml))}


def _layer_weights(l, gq_q_gain, gq_k_gain, ml_q_a_gain, ml_kv_a_gain, ml_w_q_b, ml_w_kv_b,
                   w_branch, w_out, hg_gain, avg):
    pad = LANE - ML_NOPE - ML_ROPE
    qb = ml_w_q_b[l].reshape(ML_Q_RANK, ML_HEADS, ML_NOPE + ML_ROPE)
    qb = jnp.concatenate([qb[:, :, ML_NOPE:], qb[:, :, :ML_NOPE], jnp.zeros((ML_Q_RANK, ML_HEADS, pad), F32)],
                         axis=-1).reshape(ML_Q_RANK, ML_HEADS * LANE)
    kvb = ml_w_kv_b[l].reshape(ML_KV_RANK, ML_HEADS, ML_NOPE + ML_V)
    wk = jnp.pad(kvb[:, :, :ML_NOPE], ((0, 0), (0, 0), (ML_ROPE, pad))).reshape(ML_KV_RANK, ML_HEADS * LANE)
    wv = kvb[:, :, ML_NOPE:].reshape(ML_KV_RANK, ML_HEADS * ML_V)
    return {
        "gq_q_gain": jnp.tile(gq_q_gain[l], GQ_HEADS).reshape(1, -1),
        "gq_k_gain": jnp.tile(gq_k_gain[l], GQ_KV_HEADS).reshape(1, -1),
        "ml_q_gain": ml_q_a_gain[l].reshape(1, -1), "ml_kv_gain": ml_kv_a_gain[l].reshape(1, -1),
        "w_qb": qb.astype(BF16), "w_k": wk.astype(BF16), "w_v": wv.astype(BF16),
        "w_branch": w_branch[l].astype(BF16), "w_out": w_out[l].astype(BF16),
        "hg_gain": jnp.tile(hg_gain[l], HG_HEADS).reshape(1, -1), "avg": avg,
    }


def _seg(arr, width, col, rows, stride, off):
    return (arr, width, col, rows, stride, off)


def kernel(x_prompt, x_sample, state_hgrn, cache_gqa_k, cache_gqa_v, cache_na_k, cache_na_v, cache_mla_ckv, cache_mla_krope, c, c_ctx, w_ada, b_ada, w_in, hg_lb_logits, hg_gain, gq_q_gain, gq_k_gain, na_rpb, ml_q_a_gain, ml_kv_a_gain, ml_w_q_b, ml_w_kv_b, w_branch, w_out, w_ffn_in, w_ffn_out, final_gain):
    cond8 = jnp.concatenate([c_ctx[None, :], c, jnp.zeros((8 - 1 - DEC_BATCH, D_MODEL), F32)], axis=0)
    mods = _ada(cond8, w_ada, b_ada)

    lb = jnp.cumsum(jax.nn.softmax(hg_lb_logits.astype(F32), axis=0), axis=0)
    lb = lb - lb[:1]
    avg = jnp.asarray(np.kron(np.eye(512 // GQ_HD), np.full((GQ_HD, GQ_HD), 1.0 / GQ_HD)), BF16)
    tabs = _rope_tables()
    w_in_t = jnp.swapaxes(w_in, 1, 2).reshape(DEPTH * w_in.shape[2], D_MODEL)

    gqk_c = cache_gqa_k.reshape(DEC_BATCH * DEPTH * PAST_LEN, GQ_KV_HEADS * GQ_HD)
    gqv_c = cache_gqa_v.reshape(DEC_BATCH * DEPTH * PAST_LEN, GQ_KV_HEADS * GQ_HD)
    nak_c = cache_na_k.transpose(0, 1, 3, 4, 2).reshape(DEC_BATCH * DEPTH * NA_HEADS * NA_HD, PAST_LEN)
    nav_c = cache_na_v.transpose(0, 1, 3, 4, 2).reshape(DEC_BATCH * DEPTH * NA_HEADS * NA_HD, PAST_LEN)
    mckv_c = cache_mla_ckv.reshape(DEC_BATCH * DEPTH * PAST_LEN, ML_KV_RANK)
    mkr_c = jnp.pad(cache_mla_krope.reshape(DEC_BATCH * DEPTH * PAST_LEN, ML_ROPE),
                    ((0, 0), (0, LANE - ML_ROPE)))

    xp = x_prompt.reshape(M_ROWS, D_MODEL)
    xs = x_sample.reshape(M_ROWS, D_MODEL)
    new = []
    ymix_ctx = []
    for l in range(DEPTH):
        lw = _layer_weights(l, gq_q_gain, gq_k_gain, ml_q_a_gain, ml_kv_a_gain, ml_w_q_b,
                            ml_w_kv_b, w_branch, w_out, hg_gain, avg)
        last = l == DEPTH - 1
        mod = mods[l]

        ymix = _inproj(xp, mod, w_in_t, l, False, gate=False)
        gates = _inproj(xp, mod, w_in_t, l, False, gate=True)
        qb, kb, qd, ckv, kd, vd = _prep(ymix, lw, tabs, False)
        out_a, st = _hgrn(ymix, lb[l], lw["hg_gain"], BATCH, SEQ, None, l)
        out_b = _attention((qb, 512, 0, 0),
                           [(_seg(kb, LANE, 0, SEQ, 1, 0), _seg(ymix, LANE, COL_GV, SEQ, 1, 0))],
                           _plan_gqa(), GQ_SCALE, BATCH, SEQ, "gqa_ctx")
        out_c = _attention((ymix, 512, COL_NQ, 0),
                           [(_seg(ymix, 512, COL_NK, SEQ, 1, 0), _seg(ymix, 512, COL_NV, SEQ, 1, 0))],
                           _plan_na(), NA_SCALE, BATCH, SEQ, "na_ctx")
        out_d = _attention((qd, ML_HEADS * LANE, 0, 0),
                           [(_seg(kd, ML_HEADS * LANE, 0, SEQ, 1, 0), _seg(vd, 512, 0, SEQ, 1, 0))],
                           _plan_mla(), ML_SCALE, BATCH, SEQ, "mla_ctx")
        xp = _merge((out_a, out_b, out_c, out_d), gates, lw["w_branch"], lw["w_out"], xp, mod, False)
        xp = _ffn(xp, mod, w_ffn_in, w_ffn_out, l, False, final_gain if last else None)
        new.append((
            st,
            kb.reshape(BATCH, SEQ, GQ_KV_HEADS, GQ_HD),
            ymix[:, COL_GV * LANE:(COL_GV + 1) * LANE].reshape(BATCH, SEQ, GQ_KV_HEADS, GQ_HD),
            ckv.reshape(BATCH, SEQ, ML_KV_RANK),
            ymix[:, COL_MKR * LANE:COL_MKR * LANE + ML_ROPE].reshape(BATCH, SEQ, ML_ROPE),
        ))
        ymix_ctx.append(ymix)

        ymix = _inproj(xs, mod, w_in_t, l, True, gate=False)
        gates = _inproj(xs, mod, w_in_t, l, True, gate=True)
        qb, kb, qd, ckv, kd, vd = _prep(ymix, lw, tabs, True)
        kd_c, vd_c = _mla_cache(mckv_c, mkr_c, lw, l)
        out_a, = _hgrn(ymix, lb[l], lw["hg_gain"], DEC_BATCH, DEC_SEQ, state_hgrn, l)
        out_b = _attention((qb, 512, 0, 0),
                           [(_seg(gqk_c, LANE, 0, PAST_LEN, DEPTH, l), _seg(gqv_c, LANE, 0, PAST_LEN, DEPTH, l)),
                            (_seg(kb, LANE, 0, DEC_SEQ, 1, 0), _seg(ymix, LANE, COL_GV, DEC_SEQ, 1, 0))],
                           _plan_gqa(), GQ_SCALE, DEC_BATCH, DEC_SEQ, "gqa_lat", tq=512)
        out_c = _na_latent(ymix, nak_c, nav_c, _na_bias_tables(na_rpb[l]), l)
        out_d = _attention((qd, ML_HEADS * LANE, 0, 0),
                           [(_seg(kd_c, ML_HEADS * LANE, 0, PAST_LEN, 1, 0), _seg(vd_c, 512, 0, PAST_LEN, 1, 0)),
                            (_seg(kd, ML_HEADS * LANE, 0, DEC_SEQ, 1, 0), _seg(vd, 512, 0, DEC_SEQ, 1, 0))],
                           _plan_mla(), ML_SCALE, DEC_BATCH, DEC_SEQ, "mla_lat", tq=512)
        xs = _merge((out_a, out_b, out_c, out_d), gates, lw["w_branch"], lw["w_out"], xs, mod, True)
        xs = _ffn(xs, mod, w_ffn_in, w_ffn_out, l, True, final_gain if last else None)

    y_prompt = xp.reshape(BATCH, SEQ, D_MODEL)
    y_sample = xs.reshape(DEC_BATCH, DEC_SEQ, D_MODEL)
    state, gqa_k, gqa_v, ckv_new, krope_new = (jnp.stack([n[i] for n in new], axis=1) for i in range(5))
    na_k, na_v = _na_cache_out(ymix_ctx)
    return (y_prompt, y_sample, state, gqa_k, gqa_v, na_k, na_v, ckv_new, krope_new)
```

```python
import functools

import numpy as np
import jax
import jax.numpy as jnp
from jax import lax
from jax.experimental import pallas as pl
from jax.experimental.pallas import tpu as pltpu

F32 = jnp.float32
BF16 = jnp.bfloat16

D_MODEL = 1024
BATCH = 16
SEQ = 256
DEPTH = 2
DEC_BATCH = 4
DEC_SEQ = 1024
PAST_LEN = 512
GRID_W = 64
EPS = 1e-6
ROPE_THETA = 10000.0
N_BRANCH = 4
BRANCH_W = 512
HG_HEADS = 4
HG_DK = 128
HG_DV = 128
GQ_HEADS = 8
GQ_KV_HEADS = 2
GQ_HD = 64
NA_HEADS = 8
NA_HD = 64
NA_WIN_R = 8
NA_WIN_C = 16
ML_HEADS = 8
ML_NOPE = 64
ML_ROPE = 32
ML_V = 64
ML_Q_RANK = 256
ML_KV_RANK = 128
FFN_HIDDEN = 2816
GQ_SCALE = GQ_HD ** -0.5
NA_SCALE = NA_HD ** -0.5
ML_SCALE = (ML_NOPE + ML_ROPE) ** -0.5

M_ROWS = BATCH * SEQ
assert M_ROWS == DEC_BATCH * DEC_SEQ

LANE = 128
HALF = 64
MIX_W = 5376
IN_MIX = 5280
GATE_W = N_BRANCH * D_MODEL
HG_CHUNK = 16
NEG = -1e30
LOG2E = 1.4426950408889634
VMEM_LIMIT = 56 * 1024 * 1024

COL_HQ, COL_HFF, COL_HFB, COL_HI, COL_HG, COL_GQ, COL_NQ, COL_NK, COL_NV = range(9)
COL_MQA = 18
COL_GK, COL_GV, COL_MKVA, COL_MKR = 38, 39, 40, 41


def _dot(a, b):
    return jnp.dot(a, b, preferred_element_type=F32)


def _dot_nt(a, b):
    return lax.dot_general(a, b, (((1,), (1,)), ((), ())), preferred_element_type=F32)


def _dot_tn(a, b):
    return lax.dot_general(a, b, (((0,), (0,)), ((), ())), preferred_element_type=F32)


def _split3(x):
    x1 = x.astype(BF16)
    r1 = x - x1.astype(F32)
    x2 = r1.astype(BF16)
    x3 = (r1 - x2.astype(F32)).astype(BF16)
    return x1, x2, x3


def _dot_exact_lhs(a_bf16, x):
    x1, x2, x3 = _split3(x)
    return (_dot(a_bf16, x3) + _dot(a_bf16, x2)) + _dot(a_bf16, x1)


def _dot_exact_rhs(x, b_bf16):
    x1, x2, x3 = _split3(x)
    return (_dot(x3, b_bf16) + _dot(x2, b_bf16)) + _dot(x1, b_bf16)


def _rms(x):
    return x * lax.rsqrt(jnp.mean(x * x, axis=-1, keepdims=True) + EPS)


def _silu(x):
    return x * jax.nn.sigmoid(x)


def _aligned(x, m):
    return x if isinstance(x, int) else pl.multiple_of(x, m)


def _params(*sem):
    return pltpu.CompilerParams(dimension_semantics=sem, vmem_limit_bytes=VMEM_LIMIT)


def _mod_row(latent, tm):
    if latent:
        return lambda i: 1 + (i * tm) // DEC_SEQ
    return lambda i: 0


def _ada_kernel(c_ref, w_ref, b_ref, o_ref):
    c = c_ref[...]
    o_ref[...] = _dot(_silu(c).astype(BF16), w_ref[...].astype(BF16)) + b_ref[...]


def _ada(cond8, w_ada, b_ada):
    tn = 1536
    out = pl.pallas_call(
        _ada_kernel,
        grid=(DEPTH, 6 * D_MODEL // tn),
        in_specs=[
            pl.BlockSpec((8, D_MODEL), lambda l, j: (0, 0)),
            pl.BlockSpec((None, D_MODEL, tn), lambda l, j: (l, 0, j)),
            pl.BlockSpec((None, 1, tn), lambda l, j: (l, 0, j)),
        ],
        out_specs=pl.BlockSpec((None, 8, tn), lambda l, j: (l, 0, j)),
        out_shape=jax.ShapeDtypeStruct((DEPTH, 8, 6 * D_MODEL), F32),
        compiler_params=_params("parallel", "parallel"),
        name="ada",
    )(cond8, w_ada, b_ada.reshape(DEPTH, 1, 6 * D_MODEL))
    return out.reshape(DEPTH, 8, 6, 1, D_MODEL)


IN_PIECE = 256
IN_TM = 1024


def _mix_source(t):
    blk = jnp.where(t < 12, t, jnp.where(t < 18, t + 1, jnp.where(t == 18, 19, jnp.where(t == 19, 12, 20))))
    return blk * IN_PIECE


def _gate_source(t):
    return IN_MIX + t * IN_PIECE


def _inproj_kernel(*refs, pieces, gate):
    x_ref, sc_ref, sh_ref = refs[:3]
    w_refs = refs[3:3 + pieces]
    o_ref, h_ref = refs[3 + pieces:]
    rows = pl.ds(pl.multiple_of(pl.program_id(1) * IN_TM, IN_TM), IN_TM)

    @pl.when(pl.program_id(0) == 0)
    def _():
        h = _rms(x_ref[...]) * (1.0 + sc_ref[...]) + sh_ref[...]
        h_ref[rows, :] = h.astype(BF16)

    h = h_ref[rows, :]
    for p, w_ref in enumerate(w_refs):
        y = _dot_nt(h, w_ref[...].astype(BF16))
        o_ref[:, p * IN_PIECE:(p + 1) * IN_PIECE] = jax.nn.sigmoid(y).astype(o_ref.dtype) if gate else y


def _inproj(x, mod, w_in_t, layer, latent, gate):
    width, tile, source = (GATE_W, 2048, _gate_source) if gate else (MIX_W, 1792, _mix_source)
    pieces = tile // IN_PIECE
    row = _mod_row(latent, IN_TM)
    first = lambda c, i: jnp.where(c == 0, i, 0)
    in_width = w_in_t.shape[0] // DEPTH
    piece = lambda p: pl.BlockSpec((pl.Element(IN_PIECE), pl.Element(D_MODEL)),
                                   lambda c, i: (pl.multiple_of(layer * in_width + source(c * pieces + p), 8), 0))
    return pl.pallas_call(
        functools.partial(_inproj_kernel, pieces=pieces, gate=gate),
        grid=(width // tile, M_ROWS // IN_TM),
        in_specs=[
            pl.BlockSpec((IN_TM, D_MODEL), lambda c, i: (first(c, i), 0)),
            pl.BlockSpec((None, None, 1, D_MODEL), lambda c, i: (row(first(c, i)), 1, 0, 0)),
            pl.BlockSpec((None, None, 1, D_MODEL), lambda c, i: (row(first(c, i)), 0, 0, 0)),
        ] + [piece(p) for p in range(pieces)],
        out_specs=pl.BlockSpec((IN_TM, tile), lambda c, i: (i, c)),
        out_shape=jax.ShapeDtypeStruct((M_ROWS, width), BF16 if gate else F32),
        scratch_shapes=[pltpu.VMEM((M_ROWS, D_MODEL), BF16)],
        compiler_params=_params("arbitrary", "arbitrary"),
        name="inproj_" + ("gate_" if gate else "mix_") + ("lat" if latent else "ctx"),
    )(x, mod, mod, *([w_in_t] * pieces))


def _rope(x, tab_ref, reps):
    w = x.shape[1]
    c = jnp.concatenate([tab_ref[0]] * reps, axis=1) if reps > 1 else tab_ref[0]
    se = jnp.concatenate([tab_ref[1]] * reps, axis=1) if reps > 1 else tab_ref[1]
    so = jnp.concatenate([tab_ref[2]] * reps, axis=1) if reps > 1 else tab_ref[2]
    return x * c + pltpu.roll(x, w - 1, 1) * se + pltpu.roll(x, 1, 1) * so


def _head_rms(x, gain, avg_bf16):
    ms = _dot_exact_rhs(x * x, avg_bf16)
    return x * lax.rsqrt(ms + EPS) * gain


def _prep_kernel(*refs, rope):
    (gq_ref, gk_ref, mqa_ref, mkva_ref, mkr_ref, gqg_ref, gkg_ref, mqg_ref, mkvg_ref,
     wqb_ref, wk_ref, wv_ref, avg_ref) = refs[:13]
    if rope:
        gtab_ref, mtab_ref = refs[13:15]
        outs = refs[15:]
    else:
        outs = refs[13:]
    qb_ref, kb_ref, qd_ref, ckv_ref, kd_ref, vd_ref = outs

    q = _head_rms(gq_ref[...], gqg_ref[...], avg_ref[...])
    k = _head_rms(gk_ref[...], gkg_ref[...], avg_ref[0:LANE, 0:LANE])
    qd = _dot((_rms(mqa_ref[...]) * mqg_ref[...]).astype(BF16), wqb_ref[...])
    ckv = _rms(mkva_ref[...]) * mkvg_ref[...]
    lane = lax.broadcasted_iota(jnp.int32, (1, LANE), 1)
    kr = jnp.where(lane < ML_ROPE, mkr_ref[...], 0.0)
    if rope:
        q = _rope(q, gtab_ref, GQ_HEADS * GQ_HD // LANE)
        k = _rope(k, gtab_ref, 1)
        qd = _rope(qd, mtab_ref, ML_HEADS)
        kr = _rope(kr, mtab_ref, 1)
    qb_ref[...] = q.astype(BF16)
    kb_ref[...] = k
    qd_ref[...] = qd.astype(BF16)
    ckv_ref[...] = ckv
    cb = ckv.astype(BF16)
    kd_ref[...] = (_dot(cb, wk_ref[...]) + jnp.concatenate([kr] * ML_HEADS, axis=1)).astype(BF16)
    vd_ref[...] = _dot(cb, wv_ref[...]).astype(BF16)


def _prep(ymix, lw, tabs, latent):
    tm = 512
    const = lambda i: (0, 0)
    in_specs = [
        pl.BlockSpec((tm, 512), lambda i: (i, COL_GQ)),
        pl.BlockSpec((tm, LANE), lambda i: (i, COL_GK)),
        pl.BlockSpec((tm, 256), lambda i: (i, COL_MQA)),
        pl.BlockSpec((tm, LANE), lambda i: (i, COL_MKVA)),
        pl.BlockSpec((tm, LANE), lambda i: (i, COL_MKR)),
        pl.BlockSpec((1, 512), const),
        pl.BlockSpec((1, LANE), const),
        pl.BlockSpec((1, 256), const),
        pl.BlockSpec((1, LANE), const),
        pl.BlockSpec((ML_Q_RANK, ML_HEADS * LANE), const),
        pl.BlockSpec((ML_KV_RANK, ML_HEADS * LANE), const),
        pl.BlockSpec((ML_KV_RANK, ML_HEADS * ML_V), const),
        pl.BlockSpec((512, 512), const),
    ]
    args = [ymix, ymix, ymix, ymix, ymix, lw["gq_q_gain"], lw["gq_k_gain"], lw["ml_q_gain"],
            lw["ml_kv_gain"], lw["w_qb"], lw["w_k"], lw["w_v"], lw["avg"]]
    if latent:
        per = DEC_SEQ // tm
        in_specs += [pl.BlockSpec((3, tm, LANE), lambda i: (0, i % per, 0))] * 2
        args += [tabs["gq"], tabs["ml"]]
    widths = (512, LANE, ML_HEADS * LANE, LANE, ML_HEADS * LANE, ML_HEADS * ML_V)
    return pl.pallas_call(
        functools.partial(_prep_kernel, rope=latent),
        grid=(M_ROWS // tm,),
        in_specs=in_specs,
        out_specs=[pl.BlockSpec((tm, w), lambda i: (i, 0)) for w in widths],
        out_shape=[jax.ShapeDtypeStruct((M_ROWS, w), dt)
                   for w, dt in zip(widths, (BF16, F32, BF16, F32, BF16, BF16))],
        compiler_params=_params("parallel"),
        name="prep_lat" if latent else "prep_ctx",
    )(*args)


def _mla_cache_kernel(ckv_ref, kr_ref, wk_ref, wv_ref, kd_ref, vd_ref):
    cb = ckv_ref[...].astype(BF16)
    kd_ref[...] = (_dot(cb, wk_ref[...]) + jnp.concatenate([kr_ref[...]] * ML_HEADS, axis=1)).astype(BF16)
    vd_ref[...] = _dot(cb, wv_ref[...]).astype(BF16)


def _mla_cache(ckv, kr_blk, lw, layer):
    rows = DEC_BATCH * PAST_LEN
    tm = PAST_LEN
    const = lambda i: (0, 0)
    return pl.pallas_call(
        _mla_cache_kernel,
        grid=(DEC_BATCH,),
        in_specs=[
            pl.BlockSpec((tm, LANE), lambda i: (i * DEPTH + layer, 0)),
            pl.BlockSpec((tm, LANE), lambda i: (i * DEPTH + layer, 0)),
            pl.BlockSpec((ML_KV_RANK, ML_HEADS * LANE), const),
            pl.BlockSpec((ML_KV_RANK, ML_HEADS * ML_V), const),
        ],
        out_specs=[pl.BlockSpec((tm, ML_HEADS * LANE), lambda i: (i, 0)),
                   pl.BlockSpec((tm, ML_HEADS * ML_V), lambda i: (i, 0))],
        out_shape=[jax.ShapeDtypeStruct((rows, ML_HEADS * LANE), BF16),
                   jax.ShapeDtypeStruct((rows, ML_HEADS * ML_V), BF16)],
        compiler_params=_params("parallel"),
        name="mla_cache",
    )(ckv, kr_blk, lw["w_k"], lw["w_v"])


def _softmax_pv(scores, values, transposed=None):
    transposed = transposed or [False] * len(values)
    m = scores[0].max(axis=-1, keepdims=True)
    for s in scores[1:]:
        m = jnp.maximum(m, s.max(axis=-1, keepdims=True))
    den = None
    out = None
    for s, v, vt in zip(scores, values, transposed):
        p = jnp.exp(s - m)
        d = p.sum(axis=-1, keepdims=True)
        o = _dot_nt(p.astype(BF16), v) if vt else _dot(p.astype(BF16), v)
        den = d if den is None else den + d
        out = o if out is None else out + o
    return out / den


def _softmax_pv_phased(scores, values):
    maxima = [s.max(axis=-1, keepdims=True) for s in scores]
    probs = [jnp.exp(s - m) for s, m in zip(scores, maxima)]
    dens = [p.sum(axis=-1, keepdims=True) for p in probs]
    outs = [_dot(p.astype(BF16), v) for p, v in zip(probs, values)]
    return [o / d for o, d in zip(outs, dens)]


def _lane_halves():
    lane = lax.broadcasted_iota(jnp.int32, (1, LANE), 1)
    return lane < HALF, lane >= HALF


def _attn_kernel(*refs, plan, nseg, scale, fold_scale):
    q_ref = refs[0]
    k_refs = [refs[1 + 2 * i] for i in range(nseg)]
    v_refs = [refs[2 + 2 * i] for i in range(nseg)]
    o_ref = refs[1 + 2 * nseg]
    tq = q_ref.shape[0]
    lo, hi = _lane_halves()
    cache = {}

    def block(kind, seg, blk, swap):
        key = (kind, seg, blk, swap)
        if key not in cache:
            ref = (k_refs if kind == "k" else v_refs)[seg]
            x = ref[:, blk * LANE:(blk + 1) * LANE]
            if swap:
                x = pltpu.roll(x, HALF, 1)
            cache[key] = x.astype(BF16)
        return cache[key]

    def queries(members):
        qs = []
        for qblk, qhalf, _, _ in members:
            q = q_ref[:, qblk * LANE:(qblk + 1) * LANE]
            if fold_scale:
                q = q * scale
            if qhalf is not None:
                q = jnp.where(lo if qhalf == 0 else hi, q, 0.0)
            qs.append(q)
        return (jnp.concatenate(qs, axis=0) if len(qs) > 1 else qs[0]).astype(BF16)

    def scores(q, kblk, swap):
        out = [_dot_nt(q, block("k", s, kblk, swap)) for s in range(nseg)]
        return out if fold_scale else [s * scale for s in out]

    if nseg == 1:
        all_scores = [scores(queries(members), kblk, swap)[0] for members, kblk, swap, _ in plan]
        outs = _softmax_pv_phased(all_scores, [block("v", 0, vblk, swap) for _, _, swap, vblk in plan])
    else:
        outs = []
        ready = scores(queries(plan[0][0]), plan[0][1], plan[0][2])
        for g, (_, _, swap, vblk) in enumerate(plan):
            if g + 1 < len(plan):
                following = scores(queries(plan[g + 1][0]), plan[g + 1][1], plan[g + 1][2])
            outs.append(_softmax_pv(ready, [block("v", s, vblk, swap) for s in range(nseg)]))
            ready = following
    parts = {}
    for (members, _, _, _), out in zip(plan, outs):
        for n, (_, _, oblk, ohalf) in enumerate(members):
            parts[(oblk, ohalf)] = out[n * tq:(n + 1) * tq]
    for oblk in sorted({key[0] for key in parts}):
        o_ref[:, oblk * LANE:(oblk + 1) * LANE] = jnp.where(lo, parts[(oblk, 0)], parts[(oblk, 1)]).astype(BF16)


def _plan_gqa():
    per_kv = GQ_HEADS // GQ_KV_HEADS
    plan = []
    for g in range(GQ_KV_HEADS):
        for half in range(2):
            heads = [h for h in range(g * per_kv, (g + 1) * per_kv) if h % 2 == half]
            plan.append((tuple((h // 2, half, h // 2, half) for h in heads), 0, half != g, 0))
    return tuple(plan)


def _plan_na():
    return tuple((((j, 0, j, 0), (j, 1, j, 1)), j, False, j) for j in range(NA_HEADS // 2))


def _plan_mla():
    return tuple((((h, None, h // 2, h % 2),), h, False, h // 2) for h in range(ML_HEADS))


def _is_pow2(x):
    return float(np.log2(x)).is_integer()


def _attention(q, segs, plan, scale, batch, tq_total, name, tq=256):
    tq = min(tq_total, tq)
    per = tq_total // tq
    q_arr, q_w, q_col, q_off = q
    in_specs = [pl.BlockSpec((tq, q_w), lambda b, i: (q_off + b * per + i, q_col))]
    args = [q_arr]
    for k, v in segs:
        for arr, w, col, rows, stride, off in (k, v):
            in_specs.append(pl.BlockSpec(
                (rows, w), functools.partial(lambda b, i, col, stride, off: (off + b * stride, col),
                                             col=col, stride=stride, off=off)))
            args.append(arr)
    out_w = 512
    return pl.pallas_call(
        functools.partial(_attn_kernel, plan=plan, nseg=len(segs), scale=scale, fold_scale=_is_pow2(scale)),
        grid=(batch, per),
        in_specs=in_specs,
        out_specs=pl.BlockSpec((tq, out_w), lambda b, i: (b * per + i, 0)),
        out_shape=jax.ShapeDtypeStruct((M_ROWS, out_w), BF16),
        compiler_params=_params("parallel", "parallel"),
        name=name,
    )(*args)


NA_DR = 2 * NA_WIN_R - 1
NA_DC = 2 * NA_WIN_C - 1
NA_ROWS = DEC_SEQ // GRID_W
NA_QROWS = 4
NA_WROWS = 12
NA_T_RIGHT = NA_DR - 1
NA_T_LEFT = NA_DR
NA_T_NONE = NA_DR + 1
NA_T_SIZE = NA_DR + 2
assert _is_pow2(NA_SCALE)


def _na_kernel(q_ref, kl_ref, vl_ref, kc_ref, vc_ref, tz_ref, o_ref, kc_scr, vc_scr):
    g = pl.program_id(1)

    @pl.when(g == 0)
    def _():
        kc_scr[...] = kc_ref[...].astype(BF16)
        vc_scr[...] = vc_ref[...].astype(BF16)

    nq = NA_QROWS * GRID_W
    lo, hi = _lane_halves()

    def attend(w0, wrows):
        k0 = pl.multiple_of(w0 * GRID_W, (NA_ROWS - NA_WROWS) * GRID_W)
        win = wrows * GRID_W
        entry = []
        for a in range(NA_QROWS):
            rq = g * NA_QROWS + a
            start = jnp.clip(rq - NA_WIN_R // 2, 0, NA_ROWS - NA_WIN_R)
            per_pair = []
            for i in range(wrows // 2):
                kr = w0 + 2 * i
                in_l = jnp.logical_and(kr >= start, kr < start + NA_WIN_R)
                in_r = jnp.logical_and(kr + 1 >= start, kr + 1 < start + NA_WIN_R)
                both = jnp.logical_and(in_l, in_r)
                d_l = kr - rq + NA_WIN_R - 1
                per_pair.append(
                    jnp.where(both, d_l, jnp.where(in_r, NA_T_RIGHT, jnp.where(in_l, NA_T_LEFT, NA_T_NONE))))
            entry.append(per_pair)

        def scores(j):
            sl = slice(j * LANE, (j + 1) * LANE)
            k_loc = kl_ref[pl.ds(k0, win), sl].astype(BF16)
            k_ctx_t = kc_scr[sl, :]
            q_pair = q_ref[:, sl] * NA_SCALE
            q = jnp.concatenate([jnp.where(lo, q_pair, 0.0), jnp.where(hi, q_pair, 0.0)], axis=0).astype(BF16)
            bias = jnp.concatenate(
                [jnp.concatenate([tz_ref[2 * j + p, e] for e in entry[a]], axis=1)
                 for p in range(2) for a in range(NA_QROWS)], axis=0)
            return [_dot_nt(q, k_loc) + bias, _dot(q, k_ctx_t)]

        pairs = NA_HEADS // 2
        ready = scores(0)
        for j in range(pairs):
            sl = slice(j * LANE, (j + 1) * LANE)
            if j + 1 < pairs:
                following = scores(j + 1)
            v_loc = vl_ref[pl.ds(k0, win), sl].astype(BF16)
            out = _softmax_pv(ready, [v_loc, vc_scr[sl, :]], [False, True])
            o_ref[:, sl] = jnp.where(lo, out[:nq], out[nq:]).astype(BF16)
            ready = following

    last = NA_ROWS // NA_QROWS - 1
    edge = jnp.logical_or(g == 0, g == last)

    @pl.when(edge)
    def _():
        attend(jnp.where(g == 0, 0, NA_ROWS - NA_WIN_R), NA_WIN_R)

    @pl.when(jnp.logical_not(edge))
    def _():
        attend((g // 2) * (NA_ROWS - NA_WROWS), NA_WROWS)


def _na_bias_kernel(rpb_ref, sel_ref, neg_ref, o_ref):
    o_ref[...] = _dot_exact_rhs(rpb_ref[...], sel_ref[...]) + neg_ref[...]


def _na_bias_tables(rpb):
    col = np.arange(GRID_W)
    col_start = np.clip(col - NA_WIN_C // 2, 0, GRID_W - NA_WIN_C)
    col_ok = (col[None, :] >= col_start[:, None]) & (col[None, :] < col_start[:, None] + NA_WIN_C)
    dc = col[None, :] - col[:, None] + NA_WIN_C - 1
    kpad = 32
    sel = (dc[None, :, :] == np.arange(kpad)[:, None, None]) & col_ok[None]
    sel = jnp.asarray(sel.reshape(kpad, GRID_W * GRID_W), BF16)
    neg = jnp.asarray(np.where(col_ok, 0.0, NEG).reshape(1, GRID_W * GRID_W), F32)
    rpb2 = jnp.pad(rpb.reshape(NA_HEADS * NA_DR, NA_DC), ((0, 0), (0, kpad - NA_DC)))
    n_rows = NA_HEADS * NA_DR
    full = lambda shape: pl.BlockSpec(shape, lambda i: (0, 0))
    t = pl.pallas_call(
        _na_bias_kernel,
        grid=(1,),
        in_specs=[full((n_rows, kpad)), full((kpad, GRID_W * GRID_W)), full((1, GRID_W * GRID_W))],
        out_specs=full((n_rows, GRID_W * GRID_W)),
        out_shape=jax.ShapeDtypeStruct((n_rows, GRID_W * GRID_W), F32),
        compiler_params=_params("arbitrary"),
        name="na_bias",
    )(rpb2, sel, neg)
    t = t.reshape(NA_HEADS, NA_DR, GRID_W, GRID_W)
    masked = jnp.full((NA_HEADS, 1, GRID_W, GRID_W), NEG, F32)
    first, last = NA_WIN_R // 2 - 1, NA_WIN_R // 2 + NA_WIN_R - 2
    return jnp.concatenate([
        jnp.concatenate([t[:, :-1], t[:, 1:]], axis=-1),
        jnp.concatenate([masked, t[:, first:first + 1]], axis=-1),
        jnp.concatenate([t[:, last:last + 1], masked], axis=-1),
        jnp.concatenate([masked, masked], axis=-1)], axis=1)


def _na_latent(ymix, cache_k, cache_v, tz, layer):
    groups = NA_ROWS // NA_QROWS
    nq = NA_QROWS * GRID_W
    return pl.pallas_call(
        _na_kernel,
        grid=(DEC_BATCH, groups),
        in_specs=[
            pl.BlockSpec((nq, 512), lambda b, g: (b * groups + g, COL_NQ)),
            pl.BlockSpec((DEC_SEQ, 512), lambda b, g: (b, COL_NK)),
            pl.BlockSpec((DEC_SEQ, 512), lambda b, g: (b, COL_NV)),
            pl.BlockSpec((NA_HEADS * NA_HD, PAST_LEN), lambda b, g: (b * DEPTH + layer, 0)),
            pl.BlockSpec((NA_HEADS * NA_HD, PAST_LEN), lambda b, g: (b * DEPTH + layer, 0)),
            pl.BlockSpec((NA_HEADS, NA_T_SIZE, GRID_W, LANE), lambda b, g: (0, 0, 0, 0)),
        ],
        out_specs=pl.BlockSpec((nq, 512), lambda b, g: (b * groups + g, 0)),
        out_shape=jax.ShapeDtypeStruct((M_ROWS, 512), BF16),
        scratch_shapes=[pltpu.VMEM((NA_HEADS * NA_HD, PAST_LEN), BF16)] * 2,
        compiler_params=_params("parallel", "arbitrary"),
        name="na_lat",
    )(ymix, ymix, ymix, cache_k, cache_v, tz)


def _na_cache_out_kernel(*refs):
    srcs, (ko_ref, vo_ref) = refs[:2 * DEPTH], refs[2 * DEPTH:]
    rows = NA_HEADS * NA_HD
    for l in range(DEPTH):
        ko_ref[l * rows:(l + 1) * rows, :] = srcs[2 * l][...].T
        vo_ref[l * rows:(l + 1) * rows, :] = srcs[2 * l + 1][...].T


def _na_cache_out(ymix_layers):
    rows = NA_HEADS * NA_HD
    in_specs, args = [], []
    for ymix in ymix_layers:
        for col in (COL_NK, COL_NV):
            in_specs.append(pl.BlockSpec((SEQ, 512), functools.partial(lambda b, col: (b, col), col=col)))
            args.append(ymix)
    out = pl.pallas_call(
        _na_cache_out_kernel,
        grid=(BATCH,),
        in_specs=in_specs,
        out_specs=[pl.BlockSpec((DEPTH * rows, SEQ), lambda b: (b, 0))] * 2,
        out_shape=[jax.ShapeDtypeStruct((BATCH * DEPTH * rows, SEQ), F32)] * 2,
        compiler_params=_params("parallel"),
        name="na_cache_out",
    )(*args)
    return tuple(o.reshape(BATCH, DEPTH, NA_HEADS, NA_HD, SEQ).transpose(0, 1, 4, 2, 3) for o in out)


HG_BLOCK = 128
HG_STATE_BLOCK = 64
HG_STATE_UNROLL = 8
HG_DEC_ROWS = 8


def _hgrn_kernel(*refs, seq, has_state):
    if has_state:
        (q_ref, ff_ref, fb_ref, v_ref, g_ref, lb_ref, gain_ref, s0_ref,
         o_ref, of_ref, ob_ref, qin_ref, kend_ref, dec_ref, st_ref) = refs
    else:
        (q_ref, ff_ref, fb_ref, v_ref, g_ref, lb_ref, gain_ref,
         o_ref, sout_ref, of_ref, ob_ref, qin_ref, kend_ref, dec_ref, st_ref) = refs
    c = HG_CHUNK
    hc = c // 2
    rb = HG_BLOCK
    per_block = rb // c
    n_chunks = seq // c
    width = HG_HEADS * HG_DK
    sub = HG_DEC_ROWS

    ri = lax.broadcasted_iota(jnp.int32, (rb, rb), 0)
    ci = lax.broadcasted_iota(jnp.int32, (rb, rb), 1)
    same = (ri // c) == (ci // c)
    tri_f = jnp.where(jnp.logical_and(same, ci <= ri), 1.0, 0.0).astype(BF16)
    tri_b = jnp.where(jnp.logical_and(same, ci >= ri), 1.0, 0.0).astype(BF16)
    ones = jnp.ones((HG_DK, HG_DV), BF16)
    rowid = lax.broadcasted_iota(jnp.int32, (c, HG_DV), 0)
    laneid = lax.broadcasted_iota(jnp.int32, (c, HG_DV), 1)
    laneid_half = lax.broadcasted_iota(jnp.int32, (hc, HG_DV), 1)
    o_refs = (of_ref, ob_ref)

    n_blocks = seq // rb

    def block_of(d, step):
        return step if d == 0 else n_blocks - 1 - step

    heads = [slice(h * HG_DK, (h + 1) * HG_DK) for h in range(HG_HEADS)]
    per_chunk = c * hc
    dirs = (0, 1)

    def pair_block(step):
        fwd = [True, False]
        pre_refs = (ff_ref, fb_ref)
        tris = (tri_f, tri_b)
        r0 = [_aligned(block_of(d, step) * rb, rb) for d in dirs]
        near_rows = [range(0, hc), range(hc, c)]
        far_rows = [range(hc, c), range(0, hc)]
        near = [slice(0, hc), slice(hc, c)]
        far = [slice(hc, c), slice(0, hc)]
        edge = [hc - 1, hc]
        q_all = [q_ref[pl.ds(r0[d], rb), :] * (HG_DK ** -0.5) for d in dirs]
        v16_all = [v_ref[pl.ds(r0[d], rb), :].astype(BF16) for d in dirs]
        k_all, log_f = [], []
        for d in dirs:
            lb = lb_ref[d:d + 1, :]
            f = lb + (1.0 - lb) * jax.nn.sigmoid(pre_refs[d][pl.ds(r0[d], rb), :])
            k_all.append(1.0 - f)
            log_f.append(jnp.log(f))
        b_all = [_dot_exact_lhs(tris[d], log_f[d]) * LOG2E for d in dirs]
        src_all = [b_all[d] - jnp.log(k_all[d]) * LOG2E for d in dirs]
        pairs, q_edge, k_edge = [], {}, {}
        zero_half = jnp.zeros((hc, width), F32)
        for d in dirs:
            blocks, q_in, k_end = [], [], []
            for m in range(per_block):
                rows = slice(m * c, (m + 1) * c)
                q, k, b, src = q_all[d][rows], k_all[d][rows], b_all[d][rows], src_all[d][rows]
                b_last = b[c - 1:c] if fwd[d] else b[0:1]
                q_in.append(q * jnp.exp2(b))
                k_end.append(k * jnp.exp2(b_last - b))
                dec_row = _aligned((block_of(d, step) * per_block + m) * sub, sub)
                dec_ref[d, pl.ds(dec_row, sub), :] = jnp.broadcast_to(jnp.exp2(b_last), (sub, width))
                halves = []
                for rows_s, part in ((near_rows[d], near[d]), (far_rows[d], far[d])):
                    q_part, b_part = q[part], b[part]
                    halves += [q_part * jnp.exp2(b_part - src[s:s + 1]) for s in rows_s]
                for i in range(0, c, 2):
                    blocks.append(jnp.concatenate(halves[i:i + 2], axis=0).astype(BF16))
                b_edge = b[edge[d]:edge[d] + 1]
                q_far = q[far[d]] * jnp.exp2(b[far[d]] - b_edge)
                k_near = jnp.exp2(b_edge - src[near[d]])
                q_edge[d, m] = jnp.concatenate([zero_half, q_far] if fwd[d] else [q_far, zero_half], axis=0).astype(BF16)
                k_edge[d, m] = jnp.concatenate([k_near, zero_half] if fwd[d] else [zero_half, k_near], axis=0).astype(BF16)
            qin_ref[d, pl.ds(r0[d], rb), :] = jnp.concatenate(q_in, axis=0).astype(BF16)
            kend_ref[d, pl.ds(r0[d], rb), :] = jnp.concatenate(k_end, axis=0).astype(BF16)
            pairs.append(jnp.concatenate(blocks, axis=0))
        sums = [[_dot(pairs[d][:, sl], ones) for sl in heads] for d in dirs]
        cells = [(d, h, m) for d in dirs for h in range(HG_HEADS) for m in range(per_block)]
        k_pad = jnp.zeros((HG_DK - c, HG_DK), BF16)
        across = {(d, h, m): _dot_nt(q_edge[d, m][:, heads[h]],
                                     jnp.concatenate([k_edge[d, m][:, heads[h]], k_pad], axis=0))
                  for d, h, m in cells}
        near_attn = {cell: jnp.zeros((hc, HG_DV), F32) for cell in cells}
        far_attn = {cell: jnp.zeros((hc, HG_DV), F32) for cell in cells}
        for n in range(hc):
            for d, h, m in cells:
                base = m * per_chunk + n * hc
                near_attn[d, h, m] = jnp.where(laneid_half == near_rows[d][n], sums[d][h][base:base + hc],
                                               near_attn[d, h, m])
                base += hc * hc
                far_attn[d, h, m] = jnp.where(laneid_half == far_rows[d][n], sums[d][h][base:base + hc],
                                              far_attn[d, h, m])
        v_pad = jnp.zeros((HG_DK - c, HG_DV), BF16)
        causal = [laneid <= rowid, laneid >= rowid]
        prods = {}
        for d, h, m in cells:
            halves = [near_attn[d, h, m], far_attn[d, h, m]] if fwd[d] else [far_attn[d, h, m], near_attn[d, h, m]]
            inside = jnp.where(causal[d], jnp.concatenate(halves, axis=0), 0.0)
            values = jnp.concatenate([v16_all[d][m * c:(m + 1) * c, heads[h]], v_pad], axis=0)
            prods[d, h, m] = _dot((inside + across[d, h, m]).astype(BF16), values)
        for d in dirs:
            o_refs[d][pl.ds(r0[d], rb), :] = jnp.concatenate(
                [jnp.concatenate([prods[d, h, m] for m in range(per_block)], axis=0) for h in range(HG_HEADS)], axis=1)


    def state_block(step, carry):
        rb, per_block = HG_STATE_BLOCK, HG_STATE_BLOCK // c
        block_of = lambda d, step: step if d == 0 else seq // rb - 1 - step
        order = [list(range(per_block)), list(reversed(range(per_block)))]
        r0 = [_aligned(block_of(d, step) * rb, rb) for d in dirs]
        q_in = [qin_ref[d, pl.ds(r0[d], rb), :] for d in dirs]
        k_end = [kend_ref[d, pl.ds(r0[d], rb), :] for d in dirs]
        v16 = [v_ref[pl.ds(r0[d], rb), :].astype(BF16) for d in dirs]
        rows = [slice(m * c, (m + 1) * c) for m in range(per_block)]
        q_state, k_state, k_cross, v_cross, total = [], [], [], [], []
        for d in dirs:
            dec_row = _aligned(block_of(d, step) * per_block * sub, sub)
            dec_all = dec_ref[d, pl.ds(dec_row, per_block * sub), :]
            dec = [dec_all[m * sub:m * sub + 1] for m in order[d]]
            before = [None] * per_block
            after = [None] * per_block
            for j in range(1, per_block):
                before[j] = dec[j - 1] if before[j - 1] is None else before[j - 1] * dec[j - 1]
            for j in range(per_block - 2, -1, -1):
                after[j] = dec[j + 1] if after[j + 1] is None else after[j + 1] * dec[j + 1]
            total.append(before[-1] * dec[-1])
            scale = lambda x, f: x if f is None else x * f
            qs, ks = [None] * per_block, [None] * per_block
            for j, m in enumerate(order[d]):
                qs[m] = scale(q_in[d][rows[m]], before[j])
                ks[m] = scale(k_end[d][rows[m]], after[j])
            q_state.append(jnp.concatenate(qs, axis=0).astype(BF16))
            k_state.append(jnp.concatenate(ks, axis=0).astype(BF16))
            kc, vc = [None], [None]
            for j in range(1, per_block):
                keys, between = [], None
                for i in range(j - 1, -1, -1):
                    keys.insert(0, scale(k_end[d][rows[order[d][i]]], between))
                    between = dec[i] if between is None else between * dec[i]
                kc.append(jnp.concatenate(keys, axis=0).astype(BF16))
                vc.append(jnp.concatenate([v16[d][rows[order[d][i]]] for i in range(j)], axis=0))
            k_cross.append(kc)
            v_cross.append(vc)
        q16 = [q_in[d].astype(BF16) for d in dirs]
        cells = [(d, h) for d in dirs for h in range(HG_HEADS)]
        kv = {(d, h): _dot_tn(v16[d][:, heads[h]], k_state[d][:, heads[h]]) for d, h in cells}
        attn = {(d, h, j): _dot_nt(q16[d][rows[order[d][j]], heads[h]], k_cross[d][j][:, heads[h]])
                for d, h in cells for j in range(1, per_block)}
        st = {(d, h): st_ref[d, h] for d, h in cells}
        from_state = {(d, h): _dot_nt(q_state[d][:, heads[h]], st[d, h].astype(BF16)) for d, h in cells}
        cross = {key: _dot(a.astype(BF16), v_cross[key[0]][key[2]][:, heads[key[1]]]) for key, a in attn.items()}
        for d, h in cells:
            st_ref[d, h] = st[d, h] * total[d][:, heads[h]] + kv[d, h]
        for d in dirs:
            cols = []
            for h in range(HG_HEADS):
                parts = [None] * per_block
                for j, m in enumerate(order[d]):
                    part = from_state[d, h][rows[m]]
                    parts[m] = part if j == 0 else part + cross[d, h, j]
                cols.append(jnp.concatenate(parts, axis=0))
            o_refs[d][pl.ds(r0[d], rb), :] += jnp.concatenate(cols, axis=1)
        return carry

    if has_state:
        for d in range(2):
            for h in range(HG_HEADS):
                st_ref[d, h] = s0_ref[d, h].T
    else:
        st_ref[...] = jnp.zeros(st_ref.shape, F32)

    def pair_step(i, carry):
        pair_block(i)
        return carry

    lax.fori_loop(0, n_blocks, pair_step, 0)
    lax.fori_loop(0, seq // HG_STATE_BLOCK, state_block, 0, unroll=HG_STATE_UNROLL)

    for h in range(HG_HEADS):
        sl = slice(h * HG_DV, (h + 1) * HG_DV)
        o = of_ref[:, sl] + ob_ref[:, sl]
        o_ref[:, sl] = (_rms(o) * gain_ref[:, sl] * _silu(g_ref[:, sl])).astype(BF16)
    if not has_state:
        for d in range(2):
            for h in range(HG_HEADS):
                sout_ref[d, h] = st_ref[d, h].T


def _hgrn(ymix, lb, gain, batch, seq, state, layer):
    width = HG_HEADS * HG_DK
    has_state = state is not None
    assert HG_DK == HG_DV
    st_shape = (2, HG_HEADS, HG_DK, HG_DV)
    in_specs = [pl.BlockSpec((seq, width), functools.partial(lambda b, col: (b, col), col=col))
                for col in (COL_HQ, COL_HFF, COL_HFB, COL_HI, COL_HG)]
    in_specs += [pl.BlockSpec((2, width), lambda b: (0, 0)), pl.BlockSpec((1, width), lambda b: (0, 0))]
    args = [ymix] * 5 + [lb, gain]
    out_specs = [pl.BlockSpec((seq, width), lambda b: (b, 0))]
    out_shape = [jax.ShapeDtypeStruct((M_ROWS, width), BF16)]
    if has_state:
        in_specs.append(pl.BlockSpec((None, None) + st_shape, lambda b: (b, layer, 0, 0, 0, 0)))
        args.append(state)
    else:
        out_specs.append(pl.BlockSpec((None,) + st_shape, lambda b: (b, 0, 0, 0, 0)))
        out_shape.append(jax.ShapeDtypeStruct((batch,) + st_shape, F32))
    return pl.pallas_call(
        functools.partial(_hgrn_kernel, seq=seq, has_state=has_state),
        grid=(batch,),
        in_specs=in_specs,
        out_specs=out_specs,
        out_shape=out_shape,
        scratch_shapes=[pltpu.VMEM((seq, width), F32), pltpu.VMEM((seq, width), F32),
                        pltpu.VMEM((2, seq, width), BF16), pltpu.VMEM((2, seq, width), BF16),
                        pltpu.VMEM((2, seq // HG_CHUNK * HG_DEC_ROWS, width), F32),
                        pltpu.VMEM(st_shape, F32)],
        compiler_params=_params("parallel"),
        name="hgrn_lat" if has_state else "hgrn_ctx",
    )(*args)


def _merge_kernel(oa_ref, ob_ref, oc_ref, od_ref, gt_ref, wb_ref, wo_ref, x_ref, g1_ref, out_ref):
    half = out_ref.shape[0] // 2
    rows = [pl.ds(0, half), pl.ds(half, half)]
    projected = [[_dot(o_ref[r, :], wb_ref[n]) for n, o_ref in enumerate((oa_ref, ob_ref, oc_ref, od_ref))]
                 for r in rows]
    for r, branch in zip(rows, projected):
        acc = None
        for n, bo in enumerate(branch):
            term = gt_ref[r, n * D_MODEL:(n + 1) * D_MODEL] * bo
            acc = term if acc is None else acc + term
        out_ref[r, :] = x_ref[r, :] + g1_ref[...] * _dot(acc.astype(BF16), wo_ref[...])


def _merge(branches, gates, w_branch, w_out, x, mod, latent):
    tm = 512
    row = _mod_row(latent, tm)
    tile = lambda w: pl.BlockSpec((tm, w), lambda i: (i, 0))
    return pl.pallas_call(
        _merge_kernel,
        grid=(M_ROWS // tm,),
        in_specs=[tile(BRANCH_W)] * N_BRANCH + [
            tile(GATE_W),
            pl.BlockSpec((N_BRANCH, BRANCH_W, D_MODEL), lambda i: (0, 0, 0)),
            pl.BlockSpec((D_MODEL, D_MODEL), lambda i: (0, 0)),
            tile(D_MODEL),
            pl.BlockSpec((None, None, 1, D_MODEL), lambda i: (row(i), 2, 0, 0)),
        ],
        out_specs=tile(D_MODEL),
        out_shape=jax.ShapeDtypeStruct((M_ROWS, D_MODEL), F32),
        compiler_params=_params("parallel"),
        name="merge_lat" if latent else "merge_ctx",
    )(*branches, gates, w_branch, w_out, x, mod)


FFN_CHUNK = 256
FFN_STEPS = FFN_HIDDEN // FFN_CHUNK
assert FFN_STEPS * FFN_CHUNK == FFN_HIDDEN


def _ffn_kernel(*refs, final):
    x_ref, sc_ref, sh_ref, g2_ref, wa_ref, wg_ref, wo_ref = refs[:7]
    if final:
        fg_ref, out_ref, h_ref, acc_ref = refs[7:]
    else:
        out_ref, h_ref, acc_ref = refs[7:]
    j = pl.program_id(1)

    @pl.when(j == 0)
    def _():
        h = _rms(x_ref[...]) * (1.0 + sc_ref[...]) + sh_ref[...]
        h_ref[...] = h.astype(BF16)
        acc_ref[...] = jnp.zeros(acc_ref.shape, F32)

    wg, wa, wo = wg_ref[...].astype(BF16), wa_ref[...].astype(BF16), wo_ref[...].astype(BF16)
    half = h_ref.shape[0] // 2
    rows = [pl.ds(0, half), pl.ds(half, half)]
    up = [(_dot(h_ref[r, :], wg), _dot(h_ref[r, :], wa)) for r in rows]
    for r, (gate, lin) in zip(rows, up):
        acc_ref[r, :] += _dot((_silu(gate) * lin).astype(BF16), wo)

    @pl.when(j == FFN_STEPS - 1)
    def _():
        y = x_ref[...] + g2_ref[...] * acc_ref[...]
        out_ref[...] = _rms(y) * fg_ref[...] if final else y


def _ffn(x, mod, w_in, w_out, layer, latent, final_gain):
    tm = 1024
    row = _mod_row(latent, tm)
    modspec = lambda which: pl.BlockSpec((None, None, 1, D_MODEL), lambda i, j: (row(i), which, 0, 0))
    final = final_gain is not None
    in_specs = [
        pl.BlockSpec((tm, D_MODEL), lambda i, j: (i, 0)),
        modspec(4), modspec(3), modspec(5),
        pl.BlockSpec((None, D_MODEL, FFN_CHUNK), lambda i, j: (layer, 0, j)),
        pl.BlockSpec((None, D_MODEL, FFN_CHUNK), lambda i, j: (layer, 0, FFN_STEPS + j)),
        pl.BlockSpec((None, FFN_CHUNK, D_MODEL), lambda i, j: (layer, j, 0)),
    ]
    args = [x, mod, mod, mod, w_in, w_in, w_out]
    if final:
        in_specs.append(pl.BlockSpec((1, D_MODEL), lambda i, j: (0, 0)))
        args.append(final_gain.reshape(1, D_MODEL))
    return pl.pallas_call(
        functools.partial(_ffn_kernel, final=final),
        grid=(M_ROWS // tm, FFN_STEPS),
        in_specs=in_specs,
        out_specs=pl.BlockSpec((tm, D_MODEL), lambda i, j: (i, 0)),
        out_shape=jax.ShapeDtypeStruct((M_ROWS, D_MODEL), F32),
        scratch_shapes=[pltpu.VMEM((tm, D_MODEL), BF16), pltpu.VMEM((tm, D_MODEL), F32)],
        compiler_params=_params("parallel", "arbitrary"),
        name="ffn_lat" if latent else "ffn_ctx",
    )(*args)


def _rope_tables():
    t = np.arange(DEC_SEQ)
    row = (t // GRID_W).astype(np.float64)[:, None]
    col = (t % GRID_W).astype(np.float64)[:, None]

    def angles(rot_dim):
        n_freq = rot_dim // 4
        inv_freq = ROPE_THETA ** (-np.arange(n_freq, dtype=np.float64) / n_freq)
        return np.concatenate([row * inv_freq, col * inv_freq], axis=-1)

    def expand(ang):
        cos = np.repeat(np.cos(ang), 2, axis=-1).astype(np.float32)
        sin = np.repeat(np.sin(ang), 2, axis=-1).astype(np.float32)
        even = (np.arange(cos.shape[-1]) % 2 == 0)[None, :]
        return cos, np.where(even, -sin, 0.0).astype(np.float32), np.where(even, 0.0, sin).astype(np.float32)

    gq = [np.concatenate([a, a], axis=-1) for a in expand(angles(GQ_HD))]
    ml = []
    for idx, a in enumerate(expand(angles(ML_ROPE))):
        fill = 1.0 if idx == 0 else 0.0
        ml.append(np.concatenate([a, np.full((DEC_SEQ, LANE - ML_ROPE), fill, np.float32)], axis=-1))
    return {"gq": jnp.asarray(np.stack(gq)), "ml": jnp.asarray(np.stack(ml))}


def _layer_weights(l, gq_q_gain, gq_k_gain, ml_q_a_gain, ml_kv_a_gain, ml_w_q_b, ml_w_kv_b,
                   w_branch, w_out, hg_gain, avg):
    pad = LANE - ML_NOPE - ML_ROPE
    qb = ml_w_q_b[l].reshape(ML_Q_RANK, ML_HEADS, ML_NOPE + ML_ROPE)
    qb = jnp.concatenate([qb[:, :, ML_NOPE:], qb[:, :, :ML_NOPE], jnp.zeros((ML_Q_RANK, ML_HEADS, pad), F32)],
                         axis=-1).reshape(ML_Q_RANK, ML_HEADS * LANE)
    kvb = ml_w_kv_b[l].reshape(ML_KV_RANK, ML_HEADS, ML_NOPE + ML_V)
    wk = jnp.pad(kvb[:, :, :ML_NOPE], ((0, 0), (0, 0), (ML_ROPE, pad))).reshape(ML_KV_RANK, ML_HEADS * LANE)
    wv = kvb[:, :, ML_NOPE:].reshape(ML_KV_RANK, ML_HEADS * ML_V)
    return {
        "gq_q_gain": jnp.tile(gq_q_gain[l], GQ_HEADS).reshape(1, -1),
        "gq_k_gain": jnp.tile(gq_k_gain[l], GQ_KV_HEADS).reshape(1, -1),
        "ml_q_gain": ml_q_a_gain[l].reshape(1, -1), "ml_kv_gain": ml_kv_a_gain[l].reshape(1, -1),
        "w_qb": qb.astype(BF16), "w_k": wk.astype(BF16), "w_v": wv.astype(BF16),
        "w_branch": w_branch[l].astype(BF16), "w_out": w_out[l].astype(BF16),
        "hg_gain": jnp.tile(hg_gain[l], HG_HEADS).reshape(1, -1), "avg": avg,
    }


def _seg(arr, width, col, rows, stride, off):
    return (arr, width, col, rows, stride, off)


def kernel(x_prompt, x_sample, state_hgrn, cache_gqa_k, cache_gqa_v, cache_na_k, cache_na_v, cache_mla_ckv, cache_mla_krope, c, c_ctx, w_ada, b_ada, w_in, hg_lb_logits, hg_gain, gq_q_gain, gq_k_gain, na_rpb, ml_q_a_gain, ml_kv_a_gain, ml_w_q_b, ml_w_kv_b, w_branch, w_out, w_ffn_in, w_ffn_out, final_gain):
    cond8 = jnp.concatenate([c_ctx[None, :], c, jnp.zeros((8 - 1 - DEC_BATCH, D_MODEL), F32)], axis=0)
    mods = _ada(cond8, w_ada, b_ada)

    lb = jnp.cumsum(jax.nn.softmax(hg_lb_logits.astype(F32), axis=0), axis=0)
    lb = lb - lb[:1]
    avg = jnp.asarray(np.kron(np.eye(512 // GQ_HD), np.full((GQ_HD, GQ_HD), 1.0 / GQ_HD)), BF16)
    tabs = _rope_tables()
    w_in_t = jnp.swapaxes(w_in, 1, 2).reshape(DEPTH * w_in.shape[2], D_MODEL)

    gqk_c = cache_gqa_k.reshape(DEC_BATCH * DEPTH * PAST_LEN, GQ_KV_HEADS * GQ_HD)
    gqv_c = cache_gqa_v.reshape(DEC_BATCH * DEPTH * PAST_LEN, GQ_KV_HEADS * GQ_HD)
    nak_c = cache_na_k.transpose(0, 1, 3, 4, 2).reshape(DEC_BATCH * DEPTH * NA_HEADS * NA_HD, PAST_LEN)
    nav_c = cache_na_v.transpose(0, 1, 3, 4, 2).reshape(DEC_BATCH * DEPTH * NA_HEADS * NA_HD, PAST_LEN)
    mckv_c = cache_mla_ckv.reshape(DEC_BATCH * DEPTH * PAST_LEN, ML_KV_RANK)
    mkr_c = jnp.pad(cache_mla_krope.reshape(DEC_BATCH * DEPTH * PAST_LEN, ML_ROPE),
                    ((0, 0), (0, LANE - ML_ROPE)))

    xp = x_prompt.reshape(M_ROWS, D_MODEL)
    xs = x_sample.reshape(M_ROWS, D_MODEL)
    new = []
    ymix_ctx = []
    for l in range(DEPTH):
        lw = _layer_weights(l, gq_q_gain, gq_k_gain, ml_q_a_gain, ml_kv_a_gain, ml_w_q_b,
                            ml_w_kv_b, w_branch, w_out, hg_gain, avg)
        last = l == DEPTH - 1
        mod = mods[l]

        ymix = _inproj(xp, mod, w_in_t, l, False, gate=False)
        gates = _inproj(xp, mod, w_in_t, l, False, gate=True)
        qb, kb, qd, ckv, kd, vd = _prep(ymix, lw, tabs, False)
        out_a, st = _hgrn(ymix, lb[l], lw["hg_gain"], BATCH, SEQ, None, l)
        out_b = _attention((qb, 512, 0, 0),
                           [(_seg(kb, LANE, 0, SEQ, 1, 0), _seg(ymix, LANE, COL_GV, SEQ, 1, 0))],
                           _plan_gqa(), GQ_SCALE, BATCH, SEQ, "gqa_ctx")
        out_c = _attention((ymix, 512, COL_NQ, 0),
                           [(_seg(ymix, 512, COL_NK, SEQ, 1, 0), _seg(ymix, 512, COL_NV, SEQ, 1, 0))],
                           _plan_na(), NA_SCALE, BATCH, SEQ, "na_ctx")
        out_d = _attention((qd, ML_HEADS * LANE, 0, 0),
                           [(_seg(kd, ML_HEADS * LANE, 0, SEQ, 1, 0), _seg(vd, 512, 0, SEQ, 1, 0))],
                           _plan_mla(), ML_SCALE, BATCH, SEQ, "mla_ctx")
        xp = _merge((out_a, out_b, out_c, out_d), gates, lw["w_branch"], lw["w_out"], xp, mod, False)
        xp = _ffn(xp, mod, w_ffn_in, w_ffn_out, l, False, final_gain if last else None)
        new.append((
            st,
            kb.reshape(BATCH, SEQ, GQ_KV_HEADS, GQ_HD),
            ymix[:, COL_GV * LANE:(COL_GV + 1) * LANE].reshape(BATCH, SEQ, GQ_KV_HEADS, GQ_HD),
            ckv.reshape(BATCH, SEQ, ML_KV_RANK),
            ymix[:, COL_MKR * LANE:COL_MKR * LANE + ML_ROPE].reshape(BATCH, SEQ, ML_ROPE),
        ))
        ymix_ctx.append(ymix)

        ymix = _inproj(xs, mod, w_in_t, l, True, gate=False)
        gates = _inproj(xs, mod, w_in_t, l, True, gate=True)
        qb, kb, qd, ckv, kd, vd = _prep(ymix, lw, tabs, True)
        kd_c, vd_c = _mla_cache(mckv_c, mkr_c, lw, l)
        out_a, = _hgrn(ymix, lb[l], lw["hg_gain"], DEC_BATCH, DEC_SEQ, state_hgrn, l)
        out_b = _attention((qb, 512, 0, 0),
                           [(_seg(gqk_c, LANE, 0, PAST_LEN, DEPTH, l), _seg(gqv_c, LANE, 0, PAST_LEN, DEPTH, l)),
                            (_seg(kb, LANE, 0, DEC_SEQ, 1, 0), _seg(ymix, LANE, COL_GV, DEC_SEQ, 1, 0))],
                           _plan_gqa(), GQ_SCALE, DEC_BATCH, DEC_SEQ, "gqa_lat", tq=512)
        out_c = _na_latent(ymix, nak_c, nav_c, _na_bias_tables(na_rpb[l]), l)
        out_d = _attention((qd, ML_HEADS * LANE, 0, 0),
                           [(_seg(kd_c, ML_HEADS * LANE, 0, PAST_LEN, 1, 0), _seg(vd_c, 512, 0, PAST_LEN, 1, 0)),
                            (_seg(kd, ML_HEADS * LANE, 0, DEC_SEQ, 1, 0), _seg(vd, 512, 0, DEC_SEQ, 1, 0))],
                           _plan_mla(), ML_SCALE, DEC_BATCH, DEC_SEQ, "mla_lat", tq=512)
        xs = _merge((out_a, out_b, out_c, out_d), gates, lw["w_branch"], lw["w_out"], xs, mod, True)
        xs = _ffn(xs, mod, w_ffn_in, w_ffn_out, l, True, final_gain if last else None)

    y_prompt = xp.reshape(BATCH, SEQ, D_MODEL)
    y_sample = xs.reshape(DEC_BATCH, DEC_SEQ, D_MODEL)
    state, gqa_k, gqa_v, ckv_new, krope_new = (jnp.stack([n[i] for n in new], axis=1) for i in range(5))
    na_k, na_v = _na_cache_out(ymix_ctx)
    return (y_prompt, y_sample, state, gqa_k, gqa_v, na_k, na_v, ckv_new, krope_new)
```

```python
import functools

import numpy as np
import jax
import jax.numpy as jnp
from jax import lax
from jax.experimental import pallas as pl
from jax.experimental.pallas import tpu as pltpu

F32 = jnp.float32
BF16 = jnp.bfloat16

D_MODEL = 1024
BATCH = 16
SEQ = 256
DEPTH = 2
DEC_BATCH = 4
DEC_SEQ = 1024
PAST_LEN = 512
GRID_W = 64
EPS = 1e-6
ROPE_THETA = 10000.0
N_BRANCH = 4
BRANCH_W = 512
HG_HEADS = 4
HG_DK = 128
HG_DV = 128
GQ_HEADS = 8
GQ_KV_HEADS = 2
GQ_HD = 64
NA_HEADS = 8
NA_HD = 64
NA_WIN_R = 8
NA_WIN_C = 16
ML_HEADS = 8
ML_NOPE = 64
ML_ROPE = 32
ML_V = 64
ML_Q_RANK = 256
ML_KV_RANK = 128
FFN_HIDDEN = 2816
GQ_SCALE = GQ_HD ** -0.5
NA_SCALE = NA_HD ** -0.5
ML_SCALE = (ML_NOPE + ML_ROPE) ** -0.5

M_ROWS = BATCH * SEQ
assert M_ROWS == DEC_BATCH * DEC_SEQ

LANE = 128
HALF = 64
MIX_W = 5376
IN_MIX = 5280
GATE_W = N_BRANCH * D_MODEL
HG_CHUNK = 16
NEG = -1e30
LOG2E = 1.4426950408889634
VMEM_LIMIT = 56 * 1024 * 1024

COL_HQ, COL_HFF, COL_HFB, COL_HI, COL_HG, COL_GQ, COL_NQ, COL_NK, COL_NV = range(9)
COL_MQA = 18
COL_GK, COL_GV, COL_MKVA, COL_MKR = 38, 39, 40, 41


def _dot(a, b):
    return jnp.dot(a, b, preferred_element_type=F32)


def _dot_nt(a, b):
    return lax.dot_general(a, b, (((1,), (1,)), ((), ())), preferred_element_type=F32)


def _dot_tn(a, b):
    return lax.dot_general(a, b, (((0,), (0,)), ((), ())), preferred_element_type=F32)


def _split3(x):
    x1 = x.astype(BF16)
    r1 = x - x1.astype(F32)
    x2 = r1.astype(BF16)
    x3 = (r1 - x2.astype(F32)).astype(BF16)
    return x1, x2, x3


def _dot_exact_lhs(a_bf16, x):
    x1, x2, x3 = _split3(x)
    return (_dot(a_bf16, x3) + _dot(a_bf16, x2)) + _dot(a_bf16, x1)


def _dot_exact_rhs(x, b_bf16):
    x1, x2, x3 = _split3(x)
    return (_dot(x3, b_bf16) + _dot(x2, b_bf16)) + _dot(x1, b_bf16)


def _rms(x):
    return x * lax.rsqrt(jnp.mean(x * x, axis=-1, keepdims=True) + EPS)


def _silu(x):
    return x * jax.nn.sigmoid(x)


def _aligned(x, m):
    return x if isinstance(x, int) else pl.multiple_of(x, m)


def _params(*sem):
    return pltpu.CompilerParams(dimension_semantics=sem, vmem_limit_bytes=VMEM_LIMIT)


def _mod_row(latent, tm):
    if latent:
        return lambda i: 1 + (i * tm) // DEC_SEQ
    return lambda i: 0


def _ada_kernel(c_ref, w_ref, b_ref, o_ref):
    c = c_ref[...]
    o_ref[...] = _dot(_silu(c).astype(BF16), w_ref[...].astype(BF16)) + b_ref[...]


def _ada(cond8, w_ada, b_ada):
    tn = 1536
    out = pl.pallas_call(
        _ada_kernel,
        grid=(DEPTH, 6 * D_MODEL // tn),
        in_specs=[
            pl.BlockSpec((8, D_MODEL), lambda l, j: (0, 0)),
            pl.BlockSpec((None, D_MODEL, tn), lambda l, j: (l, 0, j)),
            pl.BlockSpec((None, 1, tn), lambda l, j: (l, 0, j)),
        ],
        out_specs=pl.BlockSpec((None, 8, tn), lambda l, j: (l, 0, j)),
        out_shape=jax.ShapeDtypeStruct((DEPTH, 8, 6 * D_MODEL), F32),
        compiler_params=_params("parallel", "parallel"),
        name="ada",
    )(cond8, w_ada, b_ada.reshape(DEPTH, 1, 6 * D_MODEL))
    return out.reshape(DEPTH, 8, 6, 1, D_MODEL)


IN_PIECE = 256
IN_TM = 1024


def _mix_source(t):
    blk = jnp.where(t < 12, t, jnp.where(t < 18, t + 1, jnp.where(t == 18, 19, jnp.where(t == 19, 12, 20))))
    return blk * IN_PIECE


def _gate_source(t):
    return IN_MIX + t * IN_PIECE


def _inproj_kernel(*refs, pieces, gate):
    x_ref, sc_ref, sh_ref = refs[:3]
    w_refs = refs[3:3 + pieces]
    o_ref, h_ref = refs[3 + pieces:]
    rows = pl.ds(pl.multiple_of(pl.program_id(1) * IN_TM, IN_TM), IN_TM)

    @pl.when(pl.program_id(0) == 0)
    def _():
        h = _rms(x_ref[...]) * (1.0 + sc_ref[...]) + sh_ref[...]
        h_ref[rows, :] = h.astype(BF16)

    h = h_ref[rows, :]
    for p, w_ref in enumerate(w_refs):
        y = _dot_nt(h, w_ref[...].astype(BF16))
        o_ref[:, p * IN_PIECE:(p + 1) * IN_PIECE] = jax.nn.sigmoid(y).astype(o_ref.dtype) if gate else y


def _inproj(x, mod, w_in_t, layer, latent, gate):
    width, tile, source = (GATE_W, 2048, _gate_source) if gate else (MIX_W, 1792, _mix_source)
    pieces = tile // IN_PIECE
    row = _mod_row(latent, IN_TM)
    first = lambda c, i: jnp.where(c == 0, i, 0)
    in_width = w_in_t.shape[0] // DEPTH
    piece = lambda p: pl.BlockSpec((pl.Element(IN_PIECE), pl.Element(D_MODEL)),
                                   lambda c, i: (pl.multiple_of(layer * in_width + source(c * pieces + p), 8), 0))
    return pl.pallas_call(
        functools.partial(_inproj_kernel, pieces=pieces, gate=gate),
        grid=(width // tile, M_ROWS // IN_TM),
        in_specs=[
            pl.BlockSpec((IN_TM, D_MODEL), lambda c, i: (first(c, i), 0)),
            pl.BlockSpec((None, None, 1, D_MODEL), lambda c, i: (row(first(c, i)), 1, 0, 0)),
            pl.BlockSpec((None, None, 1, D_MODEL), lambda c, i: (row(first(c, i)), 0, 0, 0)),
        ] + [piece(p) for p in range(pieces)],
        out_specs=pl.BlockSpec((IN_TM, tile), lambda c, i: (i, c)),
        out_shape=jax.ShapeDtypeStruct((M_ROWS, width), BF16 if gate else F32),
        scratch_shapes=[pltpu.VMEM((M_ROWS, D_MODEL), BF16)],
        compiler_params=_params("arbitrary", "arbitrary"),
        name="inproj_" + ("gate_" if gate else "mix_") + ("lat" if latent else "ctx"),
    )(x, mod, mod, *([w_in_t] * pieces))


def _rope(x, tab_ref, reps):
    w = x.shape[1]
    c = jnp.concatenate([tab_ref[0]] * reps, axis=1) if reps > 1 else tab_ref[0]
    se = jnp.concatenate([tab_ref[1]] * reps, axis=1) if reps > 1 else tab_ref[1]
    so = jnp.concatenate([tab_ref[2]] * reps, axis=1) if reps > 1 else tab_ref[2]
    return x * c + pltpu.roll(x, w - 1, 1) * se + pltpu.roll(x, 1, 1) * so


def _head_rms(x, gain, avg_bf16):
    xx = x * x
    hi = xx.astype(BF16)
    ms = _dot((xx - hi.astype(F32)).astype(BF16), avg_bf16) + _dot(hi, avg_bf16)
    return x * lax.rsqrt(ms + EPS) * gain


def _prep_kernel(*refs, rope):
    (gq_ref, gk_ref, mqa_ref, mkva_ref, mkr_ref, gqg_ref, gkg_ref, mqg_ref, mkvg_ref,
     wqb_ref, wk_ref, wv_ref, avg_ref) = refs[:13]
    if rope:
        gtab_ref, mtab_ref = refs[13:15]
        outs = refs[15:]
    else:
        outs = refs[13:]
    qb_ref, kb_ref, qd_ref, ckv_ref, kd_ref, vd_ref = outs

    q = _head_rms(gq_ref[...], gqg_ref[...], avg_ref[...])
    k = _head_rms(gk_ref[...], gkg_ref[...], avg_ref[0:LANE, 0:LANE])
    qd = _dot((_rms(mqa_ref[...]) * mqg_ref[...]).astype(BF16), wqb_ref[...])
    ckv = _rms(mkva_ref[...]) * mkvg_ref[...]
    lane = lax.broadcasted_iota(jnp.int32, (1, LANE), 1)
    kr = jnp.where(lane < ML_ROPE, mkr_ref[...], 0.0)
    if rope:
        q = _rope(q, gtab_ref, GQ_HEADS * GQ_HD // LANE)
        k = _rope(k, gtab_ref, 1)
        qd = _rope(qd, mtab_ref, ML_HEADS)
        kr = _rope(kr, mtab_ref, 1)
    qb_ref[...] = q.astype(BF16)
    kb_ref[...] = k
    qd_ref[...] = qd.astype(BF16)
    ckv_ref[...] = ckv
    cb = ckv.astype(BF16)
    kd_ref[...] = (_dot(cb, wk_ref[...]) + jnp.concatenate([kr] * ML_HEADS, axis=1)).astype(BF16)
    vd_ref[...] = _dot(cb, wv_ref[...]).astype(BF16)


def _prep(ymix, lw, tabs, latent):
    tm = 512
    const = lambda i: (0, 0)
    in_specs = [
        pl.BlockSpec((tm, 512), lambda i: (i, COL_GQ)),
        pl.BlockSpec((tm, LANE), lambda i: (i, COL_GK)),
        pl.BlockSpec((tm, 256), lambda i: (i, COL_MQA)),
        pl.BlockSpec((tm, LANE), lambda i: (i, COL_MKVA)),
        pl.BlockSpec((tm, LANE), lambda i: (i, COL_MKR)),
        pl.BlockSpec((1, 512), const),
        pl.BlockSpec((1, LANE), const),
        pl.BlockSpec((1, 256), const),
        pl.BlockSpec((1, LANE), const),
        pl.BlockSpec((ML_Q_RANK, ML_HEADS * LANE), const),
        pl.BlockSpec((ML_KV_RANK, ML_HEADS * LANE), const),
        pl.BlockSpec((ML_KV_RANK, ML_HEADS * ML_V), const),
        pl.BlockSpec((512, 512), const),
    ]
    args = [ymix, ymix, ymix, ymix, ymix, lw["gq_q_gain"], lw["gq_k_gain"], lw["ml_q_gain"],
            lw["ml_kv_gain"], lw["w_qb"], lw["w_k"], lw["w_v"], lw["avg"]]
    if latent:
        per = DEC_SEQ // tm
        in_specs += [pl.BlockSpec((3, tm, LANE), lambda i: (0, i % per, 0))] * 2
        args += [tabs["gq"], tabs["ml"]]
    widths = (512, LANE, ML_HEADS * LANE, LANE, ML_HEADS * LANE, ML_HEADS * ML_V)
    return pl.pallas_call(
        functools.partial(_prep_kernel, rope=latent),
        grid=(M_ROWS // tm,),
        in_specs=in_specs,
        out_specs=[pl.BlockSpec((tm, w), lambda i: (i, 0)) for w in widths],
        out_shape=[jax.ShapeDtypeStruct((M_ROWS, w), dt)
                   for w, dt in zip(widths, (BF16, F32, BF16, F32, BF16, BF16))],
        compiler_params=_params("parallel"),
        name="prep_lat" if latent else "prep_ctx",
    )(*args)


def _mla_cache_kernel(ckv_ref, kr_ref, wk_ref, wv_ref, kd_ref, vd_ref):
    cb = ckv_ref[...].astype(BF16)
    kd_ref[...] = (_dot(cb, wk_ref[...]) + jnp.concatenate([kr_ref[...]] * ML_HEADS, axis=1)).astype(BF16)
    vd_ref[...] = _dot(cb, wv_ref[...]).astype(BF16)


def _mla_cache(ckv, kr_blk, lw, layer):
    rows = DEC_BATCH * PAST_LEN
    tm = PAST_LEN
    const = lambda i: (0, 0)
    return pl.pallas_call(
        _mla_cache_kernel,
        grid=(DEC_BATCH,),
        in_specs=[
            pl.BlockSpec((tm, LANE), lambda i: (i * DEPTH + layer, 0)),
            pl.BlockSpec((tm, LANE), lambda i: (i * DEPTH + layer, 0)),
            pl.BlockSpec((ML_KV_RANK, ML_HEADS * LANE), const),
            pl.BlockSpec((ML_KV_RANK, ML_HEADS * ML_V), const),
        ],
        out_specs=[pl.BlockSpec((tm, ML_HEADS * LANE), lambda i: (i, 0)),
                   pl.BlockSpec((tm, ML_HEADS * ML_V), lambda i: (i, 0))],
        out_shape=[jax.ShapeDtypeStruct((rows, ML_HEADS * LANE), BF16),
                   jax.ShapeDtypeStruct((rows, ML_HEADS * ML_V), BF16)],
        compiler_params=_params("parallel"),
        name="mla_cache",
    )(ckv, kr_blk, lw["w_k"], lw["w_v"])


def _softmax_pv(scores, values, transposed=None):
    transposed = transposed or [False] * len(values)
    m = scores[0].max(axis=-1, keepdims=True)
    for s in scores[1:]:
        m = jnp.maximum(m, s.max(axis=-1, keepdims=True))
    den = None
    out = None
    for s, v, vt in zip(scores, values, transposed):
        p = jnp.exp(s - m)
        d = p.sum(axis=-1, keepdims=True)
        o = _dot_nt(p.astype(BF16), v) if vt else _dot(p.astype(BF16), v)
        den = d if den is None else den + d
        out = o if out is None else out + o
    return out / den


def _softmax_pv_phased(scores, values):
    maxima = [s.max(axis=-1, keepdims=True) for s in scores]
    probs = [jnp.exp(s - m) for s, m in zip(scores, maxima)]
    dens = [p.sum(axis=-1, keepdims=True) for p in probs]
    outs = [_dot(p.astype(BF16), v) for p, v in zip(probs, values)]
    return [o / d for o, d in zip(outs, dens)]


def _lane_halves():
    lane = lax.broadcasted_iota(jnp.int32, (1, LANE), 1)
    return lane < HALF, lane >= HALF


def _attn_kernel(*refs, plan, nseg, scale, fold_scale):
    q_ref = refs[0]
    k_refs = [refs[1 + 2 * i] for i in range(nseg)]
    v_refs = [refs[2 + 2 * i] for i in range(nseg)]
    o_ref = refs[1 + 2 * nseg]
    tq = q_ref.shape[0]
    lo, hi = _lane_halves()
    cache = {}

    def block(kind, seg, blk, swap):
        key = (kind, seg, blk, swap)
        if key not in cache:
            ref = (k_refs if kind == "k" else v_refs)[seg]
            x = ref[:, blk * LANE:(blk + 1) * LANE]
            if swap:
                x = pltpu.roll(x, HALF, 1)
            cache[key] = x.astype(BF16)
        return cache[key]

    def queries(members):
        qs = []
        for qblk, qhalf, _, _ in members:
            q = q_ref[:, qblk * LANE:(qblk + 1) * LANE]
            if fold_scale:
                q = q * scale
            if qhalf is not None:
                q = jnp.where(lo if qhalf == 0 else hi, q, 0.0)
            qs.append(q)
        return (jnp.concatenate(qs, axis=0) if len(qs) > 1 else qs[0]).astype(BF16)

    def scores(q, kblk, swap):
        out = [_dot_nt(q, block("k", s, kblk, swap)) for s in range(nseg)]
        return out if fold_scale else [s * scale for s in out]

    if nseg == 1:
        all_scores = [scores(queries(members), kblk, swap)[0] for members, kblk, swap, _ in plan]
        outs = _softmax_pv_phased(all_scores, [block("v", 0, vblk, swap) for _, _, swap, vblk in plan])
    else:
        outs = []
        ready = scores(queries(plan[0][0]), plan[0][1], plan[0][2])
        for g, (_, _, swap, vblk) in enumerate(plan):
            if g + 1 < len(plan):
                following = scores(queries(plan[g + 1][0]), plan[g + 1][1], plan[g + 1][2])
            outs.append(_softmax_pv(ready, [block("v", s, vblk, swap) for s in range(nseg)]))
            ready = following
    parts = {}
    for (members, _, _, _), out in zip(plan, outs):
        for n, (_, _, oblk, ohalf) in enumerate(members):
            parts[(oblk, ohalf)] = out[n * tq:(n + 1) * tq]
    for oblk in sorted({key[0] for key in parts}):
        o_ref[:, oblk * LANE:(oblk + 1) * LANE] = jnp.where(lo, parts[(oblk, 0)], parts[(oblk, 1)]).astype(BF16)


def _plan_gqa():
    per_kv = GQ_HEADS // GQ_KV_HEADS
    plan = []
    for g in range(GQ_KV_HEADS):
        for half in range(2):
            heads = [h for h in range(g * per_kv, (g + 1) * per_kv) if h % 2 == half]
            plan.append((tuple((h // 2, half, h // 2, half) for h in heads), 0, half != g, 0))
    return tuple(plan)


def _plan_na():
    return tuple((((j, 0, j, 0), (j, 1, j, 1)), j, False, j) for j in range(NA_HEADS // 2))


def _plan_mla():
    return tuple((((h, None, h // 2, h % 2),), h, False, h // 2) for h in range(ML_HEADS))


def _is_pow2(x):
    return float(np.log2(x)).is_integer()


def _attention(q, segs, plan, scale, batch, tq_total, name, tq=256):
    tq = min(tq_total, tq)
    per = tq_total // tq
    q_arr, q_w, q_col, q_off = q
    in_specs = [pl.BlockSpec((tq, q_w), lambda b, i: (q_off + b * per + i, q_col))]
    args = [q_arr]
    for k, v in segs:
        for arr, w, col, rows, stride, off in (k, v):
            in_specs.append(pl.BlockSpec(
                (rows, w), functools.partial(lambda b, i, col, stride, off: (off + b * stride, col),
                                             col=col, stride=stride, off=off)))
            args.append(arr)
    out_w = 512
    return pl.pallas_call(
        functools.partial(_attn_kernel, plan=plan, nseg=len(segs), scale=scale, fold_scale=_is_pow2(scale)),
        grid=(batch, per),
        in_specs=in_specs,
        out_specs=pl.BlockSpec((tq, out_w), lambda b, i: (b * per + i, 0)),
        out_shape=jax.ShapeDtypeStruct((M_ROWS, out_w), BF16),
        compiler_params=_params("parallel", "parallel"),
        name=name,
    )(*args)


NA_DR = 2 * NA_WIN_R - 1
NA_DC = 2 * NA_WIN_C - 1
NA_ROWS = DEC_SEQ // GRID_W
NA_QROWS = 4
NA_WROWS = 12
NA_T_RIGHT = NA_DR - 1
NA_T_LEFT = NA_DR
NA_T_NONE = NA_DR + 1
NA_T_SIZE = NA_DR + 2
assert _is_pow2(NA_SCALE)


def _na_kernel(q_ref, kl_ref, vl_ref, kc_ref, vc_ref, tz_ref, o_ref, kc_scr, vc_scr):
    g = pl.program_id(1)

    @pl.when(g == 0)
    def _():
        kc_scr[...] = kc_ref[...].astype(BF16)
        vc_scr[...] = vc_ref[...].astype(BF16)

    nq = NA_QROWS * GRID_W
    lo, hi = _lane_halves()

    def attend(w0, wrows):
        k0 = pl.multiple_of(w0 * GRID_W, (NA_ROWS - NA_WROWS) * GRID_W)
        win = wrows * GRID_W
        entry = []
        for a in range(NA_QROWS):
            rq = g * NA_QROWS + a
            start = jnp.clip(rq - NA_WIN_R // 2, 0, NA_ROWS - NA_WIN_R)
            per_pair = []
            for i in range(wrows // 2):
                kr = w0 + 2 * i
                in_l = jnp.logical_and(kr >= start, kr < start + NA_WIN_R)
                in_r = jnp.logical_and(kr + 1 >= start, kr + 1 < start + NA_WIN_R)
                both = jnp.logical_and(in_l, in_r)
                d_l = kr - rq + NA_WIN_R - 1
                per_pair.append(
                    jnp.where(both, d_l, jnp.where(in_r, NA_T_RIGHT, jnp.where(in_l, NA_T_LEFT, NA_T_NONE))))
            entry.append(per_pair)

        def scores(j):
            sl = slice(j * LANE, (j + 1) * LANE)
            k_loc = kl_ref[pl.ds(k0, win), sl].astype(BF16)
            k_ctx_t = kc_scr[sl, :]
            q_pair = q_ref[:, sl] * NA_SCALE
            q = jnp.concatenate([jnp.where(lo, q_pair, 0.0), jnp.where(hi, q_pair, 0.0)], axis=0).astype(BF16)
            bias = jnp.concatenate(
                [jnp.concatenate([tz_ref[2 * j + p, e] for e in entry[a]], axis=1)
                 for p in range(2) for a in range(NA_QROWS)], axis=0)
            return [_dot_nt(q, k_loc) + bias, _dot(q, k_ctx_t)]

        pairs = NA_HEADS // 2
        ready = scores(0)
        for j in range(pairs):
            sl = slice(j * LANE, (j + 1) * LANE)
            if j + 1 < pairs:
                following = scores(j + 1)
            v_loc = vl_ref[pl.ds(k0, win), sl].astype(BF16)
            out = _softmax_pv(ready, [v_loc, vc_scr[sl, :]], [False, True])
            o_ref[:, sl] = jnp.where(lo, out[:nq], out[nq:]).astype(BF16)
            ready = following

    last = NA_ROWS // NA_QROWS - 1
    edge = jnp.logical_or(g == 0, g == last)

    @pl.when(edge)
    def _():
        attend(jnp.where(g == 0, 0, NA_ROWS - NA_WIN_R), NA_WIN_R)

    @pl.when(jnp.logical_not(edge))
    def _():
        attend((g // 2) * (NA_ROWS - NA_WROWS), NA_WROWS)


def _na_bias_kernel(rpb_ref, sel_ref, neg_ref, o_ref):
    o_ref[...] = _dot_exact_rhs(rpb_ref[...], sel_ref[...]) + neg_ref[...]


def _na_bias_tables(rpb):
    col = np.arange(GRID_W)
    col_start = np.clip(col - NA_WIN_C // 2, 0, GRID_W - NA_WIN_C)
    col_ok = (col[None, :] >= col_start[:, None]) & (col[None, :] < col_start[:, None] + NA_WIN_C)
    dc = col[None, :] - col[:, None] + NA_WIN_C - 1
    kpad = 32
    sel = (dc[None, :, :] == np.arange(kpad)[:, None, None]) & col_ok[None]
    sel = jnp.asarray(sel.reshape(kpad, GRID_W * GRID_W), BF16)
    neg = jnp.asarray(np.where(col_ok, 0.0, NEG).reshape(1, GRID_W * GRID_W), F32)
    rpb2 = jnp.pad(rpb.reshape(NA_HEADS * NA_DR, NA_DC), ((0, 0), (0, kpad - NA_DC)))
    n_rows = NA_HEADS * NA_DR
    full = lambda shape: pl.BlockSpec(shape, lambda i: (0, 0))
    t = pl.pallas_call(
        _na_bias_kernel,
        grid=(1,),
        in_specs=[full((n_rows, kpad)), full((kpad, GRID_W * GRID_W)), full((1, GRID_W * GRID_W))],
        out_specs=full((n_rows, GRID_W * GRID_W)),
        out_shape=jax.ShapeDtypeStruct((n_rows, GRID_W * GRID_W), F32),
        compiler_params=_params("arbitrary"),
        name="na_bias",
    )(rpb2, sel, neg)
    t = t.reshape(NA_HEADS, NA_DR, GRID_W, GRID_W)
    masked = jnp.full((NA_HEADS, 1, GRID_W, GRID_W), NEG, F32)
    first, last = NA_WIN_R // 2 - 1, NA_WIN_R // 2 + NA_WIN_R - 2
    return jnp.concatenate([
        jnp.concatenate([t[:, :-1], t[:, 1:]], axis=-1),
        jnp.concatenate([masked, t[:, first:first + 1]], axis=-1),
        jnp.concatenate([t[:, last:last + 1], masked], axis=-1),
        jnp.concatenate([masked, masked], axis=-1)], axis=1)


def _na_latent(ymix, cache_k, cache_v, tz, layer):
    groups = NA_ROWS // NA_QROWS
    nq = NA_QROWS * GRID_W
    return pl.pallas_call(
        _na_kernel,
        grid=(DEC_BATCH, groups),
        in_specs=[
            pl.BlockSpec((nq, 512), lambda b, g: (b * groups + g, COL_NQ)),
            pl.BlockSpec((DEC_SEQ, 512), lambda b, g: (b, COL_NK)),
            pl.BlockSpec((DEC_SEQ, 512), lambda b, g: (b, COL_NV)),
            pl.BlockSpec((NA_HEADS * NA_HD, PAST_LEN), lambda b, g: (b * DEPTH + layer, 0)),
            pl.BlockSpec((NA_HEADS * NA_HD, PAST_LEN), lambda b, g: (b * DEPTH + layer, 0)),
            pl.BlockSpec((NA_HEADS, NA_T_SIZE, GRID_W, LANE), lambda b, g: (0, 0, 0, 0)),
        ],
        out_specs=pl.BlockSpec((nq, 512), lambda b, g: (b * groups + g, 0)),
        out_shape=jax.ShapeDtypeStruct((M_ROWS, 512), BF16),
        scratch_shapes=[pltpu.VMEM((NA_HEADS * NA_HD, PAST_LEN), BF16)] * 2,
        compiler_params=_params("parallel", "arbitrary"),
        name="na_lat",
    )(ymix, ymix, ymix, cache_k, cache_v, tz)


def _na_cache_out_kernel(*refs):
    srcs, (ko_ref, vo_ref) = refs[:2 * DEPTH], refs[2 * DEPTH:]
    rows = NA_HEADS * NA_HD
    for l in range(DEPTH):
        ko_ref[l * rows:(l + 1) * rows, :] = srcs[2 * l][...].T
        vo_ref[l * rows:(l + 1) * rows, :] = srcs[2 * l + 1][...].T


def _na_cache_out(ymix_layers):
    rows = NA_HEADS * NA_HD
    in_specs, args = [], []
    for ymix in ymix_layers:
        for col in (COL_NK, COL_NV):
            in_specs.append(pl.BlockSpec((SEQ, 512), functools.partial(lambda b, col: (b, col), col=col)))
            args.append(ymix)
    out = pl.pallas_call(
        _na_cache_out_kernel,
        grid=(BATCH,),
        in_specs=in_specs,
        out_specs=[pl.BlockSpec((DEPTH * rows, SEQ), lambda b: (b, 0))] * 2,
        out_shape=[jax.ShapeDtypeStruct((BATCH * DEPTH * rows, SEQ), F32)] * 2,
        compiler_params=_params("parallel"),
        name="na_cache_out",
    )(*args)
    return tuple(o.reshape(BATCH, DEPTH, NA_HEADS, NA_HD, SEQ).transpose(0, 1, 4, 2, 3) for o in out)


HG_BLOCK = 128
HG_STATE_BLOCK = 64
HG_STATE_UNROLL = 8
HG_DEC_ROWS = 8


def _hgrn_kernel(*refs, seq, has_state):
    if has_state:
        (q_ref, ff_ref, fb_ref, v_ref, g_ref, lb_ref, gain_ref, s0_ref,
         o_ref, of_ref, ob_ref, qin_ref, kend_ref, dec_ref, st_ref) = refs
    else:
        (q_ref, ff_ref, fb_ref, v_ref, g_ref, lb_ref, gain_ref,
         o_ref, sout_ref, of_ref, ob_ref, qin_ref, kend_ref, dec_ref, st_ref) = refs
    c = HG_CHUNK
    hc = c // 2
    rb = HG_BLOCK
    per_block = rb // c
    n_chunks = seq // c
    width = HG_HEADS * HG_DK
    sub = HG_DEC_ROWS

    ri = lax.broadcasted_iota(jnp.int32, (rb, rb), 0)
    ci = lax.broadcasted_iota(jnp.int32, (rb, rb), 1)
    same = (ri // c) == (ci // c)
    tri_f = jnp.where(jnp.logical_and(same, ci <= ri), 1.0, 0.0).astype(BF16)
    tri_b = jnp.where(jnp.logical_and(same, ci >= ri), 1.0, 0.0).astype(BF16)
    ones = jnp.ones((HG_DK, HG_DV), BF16)
    rowid = lax.broadcasted_iota(jnp.int32, (c, HG_DV), 0)
    laneid = lax.broadcasted_iota(jnp.int32, (c, HG_DV), 1)
    laneid_half = lax.broadcasted_iota(jnp.int32, (hc, HG_DV), 1)
    o_refs = (of_ref, ob_ref)

    n_blocks = seq // rb

    def block_of(d, step):
        return step if d == 0 else n_blocks - 1 - step

    heads = [slice(h * HG_DK, (h + 1) * HG_DK) for h in range(HG_HEADS)]
    per_chunk = c * hc
    dirs = (0, 1)

    def pair_block(step):
        fwd = [True, False]
        pre_refs = (ff_ref, fb_ref)
        tris = (tri_f, tri_b)
        r0 = [_aligned(block_of(d, step) * rb, rb) for d in dirs]
        near_rows = [range(0, hc), range(hc, c)]
        far_rows = [range(hc, c), range(0, hc)]
        near = [slice(0, hc), slice(hc, c)]
        far = [slice(hc, c), slice(0, hc)]
        edge = [hc - 1, hc]
        q_all = [q_ref[pl.ds(r0[d], rb), :] * (HG_DK ** -0.5) for d in dirs]
        v16_all = [v_ref[pl.ds(r0[d], rb), :].astype(BF16) for d in dirs]
        k_all, log_f = [], []
        for d in dirs:
            lb = lb_ref[d:d + 1, :]
            f = lb + (1.0 - lb) * jax.nn.sigmoid(pre_refs[d][pl.ds(r0[d], rb), :])
            k_all.append(1.0 - f)
            log_f.append(jnp.log(f))
        b_all = [_dot_exact_lhs(tris[d], log_f[d]) * LOG2E for d in dirs]
        src_all = [b_all[d] - jnp.log(k_all[d]) * LOG2E for d in dirs]
        pairs, q_edge, k_edge = [], {}, {}
        zero_half = jnp.zeros((hc, width), F32)
        for d in dirs:
            blocks, q_in, k_end = [], [], []
            for m in range(per_block):
                rows = slice(m * c, (m + 1) * c)
                q, k, b, src = q_all[d][rows], k_all[d][rows], b_all[d][rows], src_all[d][rows]
                b_last = b[c - 1:c] if fwd[d] else b[0:1]
                q_in.append(q * jnp.exp2(b))
                k_end.append(k * jnp.exp2(b_last - b))
                dec_row = _aligned((block_of(d, step) * per_block + m) * sub, sub)
                dec_ref[d, pl.ds(dec_row, sub), :] = jnp.broadcast_to(jnp.exp2(b_last), (sub, width))
                halves = []
                for rows_s, part in ((near_rows[d], near[d]), (far_rows[d], far[d])):
                    q_part, b_part = q[part], b[part]
                    halves += [q_part * jnp.exp2(b_part - src[s:s + 1]) for s in rows_s]
                for i in range(0, c, 2):
                    blocks.append(jnp.concatenate(halves[i:i + 2], axis=0).astype(BF16))
                b_edge = b[edge[d]:edge[d] + 1]
                q_far = q[far[d]] * jnp.exp2(b[far[d]] - b_edge)
                k_near = jnp.exp2(b_edge - src[near[d]])
                q_edge[d, m] = jnp.concatenate([zero_half, q_far] if fwd[d] else [q_far, zero_half], axis=0).astype(BF16)
                k_edge[d, m] = jnp.concatenate([k_near, zero_half] if fwd[d] else [zero_half, k_near], axis=0).astype(BF16)
            qin_ref[d, pl.ds(r0[d], rb), :] = jnp.concatenate(q_in, axis=0).astype(BF16)
            kend_ref[d, pl.ds(r0[d], rb), :] = jnp.concatenate(k_end, axis=0).astype(BF16)
            pairs.append(jnp.concatenate(blocks, axis=0))
        sums = [[_dot(pairs[d][:, sl], ones) for sl in heads] for d in dirs]
        cells = [(d, h, m) for d in dirs for h in range(HG_HEADS) for m in range(per_block)]
        k_pad = jnp.zeros((HG_DK - c, HG_DK), BF16)
        across = {(d, h, m): _dot_nt(q_edge[d, m][:, heads[h]],
                                     jnp.concatenate([k_edge[d, m][:, heads[h]], k_pad], axis=0))
                  for d, h, m in cells}
        near_attn = {cell: jnp.zeros((hc, HG_DV), F32) for cell in cells}
        far_attn = {cell: jnp.zeros((hc, HG_DV), F32) for cell in cells}
        for n in range(hc):
            for d, h, m in cells:
                base = m * per_chunk + n * hc
                near_attn[d, h, m] = jnp.where(laneid_half == near_rows[d][n], sums[d][h][base:base + hc],
                                               near_attn[d, h, m])
                base += hc * hc
                far_attn[d, h, m] = jnp.where(laneid_half == far_rows[d][n], sums[d][h][base:base + hc],
                                              far_attn[d, h, m])
        v_pad = jnp.zeros((HG_DK - c, HG_DV), BF16)
        causal = [laneid <= rowid, laneid >= rowid]
        prods = {}
        for d, h, m in cells:
            halves = [near_attn[d, h, m], far_attn[d, h, m]] if fwd[d] else [far_attn[d, h, m], near_attn[d, h, m]]
            inside = jnp.where(causal[d], jnp.concatenate(halves, axis=0), 0.0)
            values = jnp.concatenate([v16_all[d][m * c:(m + 1) * c, heads[h]], v_pad], axis=0)
            prods[d, h, m] = _dot((inside + across[d, h, m]).astype(BF16), values)
        for d in dirs:
            o_refs[d][pl.ds(r0[d], rb), :] = jnp.concatenate(
                [jnp.concatenate([prods[d, h, m] for m in range(per_block)], axis=0) for h in range(HG_HEADS)], axis=1)


    def state_block(step, carry):
        rb, per_block = HG_STATE_BLOCK, HG_STATE_BLOCK // c
        block_of = lambda d, step: step if d == 0 else seq // rb - 1 - step
        order = [list(range(per_block)), list(reversed(range(per_block)))]
        r0 = [_aligned(block_of(d, step) * rb, rb) for d in dirs]
        q_in = [qin_ref[d, pl.ds(r0[d], rb), :] for d in dirs]
        k_end = [kend_ref[d, pl.ds(r0[d], rb), :] for d in dirs]
        v16 = [v_ref[pl.ds(r0[d], rb), :].astype(BF16) for d in dirs]
        rows = [slice(m * c, (m + 1) * c) for m in range(per_block)]
        q_state, k_state, k_cross, v_cross, total = [], [], [], [], []
        for d in dirs:
            dec_row = _aligned(block_of(d, step) * per_block * sub, sub)
            dec_all = dec_ref[d, pl.ds(dec_row, per_block * sub), :]
            dec = [dec_all[m * sub:m * sub + 1] for m in order[d]]
            before = [None] * per_block
            after = [None] * per_block
            for j in range(1, per_block):
                before[j] = dec[j - 1] if before[j - 1] is None else before[j - 1] * dec[j - 1]
            for j in range(per_block - 2, -1, -1):
                after[j] = dec[j + 1] if after[j + 1] is None else after[j + 1] * dec[j + 1]
            total.append(before[-1] * dec[-1])
            scale = lambda x, f: x if f is None else x * f
            qs, ks = [None] * per_block, [None] * per_block
            for j, m in enumerate(order[d]):
                qs[m] = scale(q_in[d][rows[m]], before[j])
                ks[m] = scale(k_end[d][rows[m]], after[j])
            q_state.append(jnp.concatenate(qs, axis=0).astype(BF16))
            k_state.append(jnp.concatenate(ks, axis=0).astype(BF16))
            kc, vc = [None], [None]
            for j in range(1, per_block):
                keys, between = [], None
                for i in range(j - 1, -1, -1):
                    keys.insert(0, scale(k_end[d][rows[order[d][i]]], between))
                    between = dec[i] if between is None else between * dec[i]
                kc.append(jnp.concatenate(keys, axis=0).astype(BF16))
                vc.append(jnp.concatenate([v16[d][rows[order[d][i]]] for i in range(j)], axis=0))
            k_cross.append(kc)
            v_cross.append(vc)
        q16 = [q_in[d].astype(BF16) for d in dirs]
        cells = [(d, h) for d in dirs for h in range(HG_HEADS)]
        kv = {(d, h): _dot_tn(v16[d][:, heads[h]], k_state[d][:, heads[h]]) for d, h in cells}
        attn = {(d, h, j): _dot_nt(q16[d][rows[order[d][j]], heads[h]], k_cross[d][j][:, heads[h]])
                for d, h in cells for j in range(1, per_block)}
        st = {(d, h): st_ref[d, h] for d, h in cells}
        from_state = {(d, h): _dot_nt(q_state[d][:, heads[h]], st[d, h].astype(BF16)) for d, h in cells}
        cross = {key: _dot(a.astype(BF16), v_cross[key[0]][key[2]][:, heads[key[1]]]) for key, a in attn.items()}
        for d, h in cells:
            st_ref[d, h] = st[d, h] * total[d][:, heads[h]] + kv[d, h]
        for d in dirs:
            cols = []
            for h in range(HG_HEADS):
                parts = [None] * per_block
                for j, m in enumerate(order[d]):
                    part = from_state[d, h][rows[m]]
                    parts[m] = part if j == 0 else part + cross[d, h, j]
                cols.append(jnp.concatenate(parts, axis=0))
            o_refs[d][pl.ds(r0[d], rb), :] += jnp.concatenate(cols, axis=1)
        return carry

    if has_state:
        for d in range(2):
            for h in range(HG_HEADS):
                st_ref[d, h] = s0_ref[d, h].T
    else:
        st_ref[...] = jnp.zeros(st_ref.shape, F32)

    def pair_step(i, carry):
        pair_block(i)
        return carry

    lax.fori_loop(0, n_blocks, pair_step, 0)
    lax.fori_loop(0, seq // HG_STATE_BLOCK, state_block, 0, unroll=HG_STATE_UNROLL)

    for h in range(HG_HEADS):
        sl = slice(h * HG_DV, (h + 1) * HG_DV)
        o = of_ref[:, sl] + ob_ref[:, sl]
        o_ref[:, sl] = (_rms(o) * gain_ref[:, sl] * _silu(g_ref[:, sl])).astype(BF16)
    if not has_state:
        for d in range(2):
            for h in range(HG_HEADS):
                sout_ref[d, h] = st_ref[d, h].T


def _hgrn(ymix, lb, gain, batch, seq, state, layer):
    width = HG_HEADS * HG_DK
    has_state = state is not None
    assert HG_DK == HG_DV
    st_shape = (2, HG_HEADS, HG_DK, HG_DV)
    in_specs = [pl.BlockSpec((seq, width), functools.partial(lambda b, col: (b, col), col=col))
                for col in (COL_HQ, COL_HFF, COL_HFB, COL_HI, COL_HG)]
    in_specs += [pl.BlockSpec((2, width), lambda b: (0, 0)), pl.BlockSpec((1, width), lambda b: (0, 0))]
    args = [ymix] * 5 + [lb, gain]
    out_specs = [pl.BlockSpec((seq, width), lambda b: (b, 0))]
    out_shape = [jax.ShapeDtypeStruct((M_ROWS, width), BF16)]
    if has_state:
        in_specs.append(pl.BlockSpec((None, None) + st_shape, lambda b: (b, layer, 0, 0, 0, 0)))
        args.append(state)
    else:
        out_specs.append(pl.BlockSpec((None,) + st_shape, lambda b: (b, 0, 0, 0, 0)))
        out_shape.append(jax.ShapeDtypeStruct((batch,) + st_shape, F32))
    return pl.pallas_call(
        functools.partial(_hgrn_kernel, seq=seq, has_state=has_state),
        grid=(batch,),
        in_specs=in_specs,
        out_specs=out_specs,
        out_shape=out_shape,
        scratch_shapes=[pltpu.VMEM((seq, width), F32), pltpu.VMEM((seq, width), F32),
                        pltpu.VMEM((2, seq, width), BF16), pltpu.VMEM((2, seq, width), BF16),
                        pltpu.VMEM((2, seq // HG_CHUNK * HG_DEC_ROWS, width), F32),
                        pltpu.VMEM(st_shape, F32)],
        compiler_params=_params("parallel"),
        name="hgrn_lat" if has_state else "hgrn_ctx",
    )(*args)


def _merge_kernel(oa_ref, ob_ref, oc_ref, od_ref, gt_ref, wb_ref, wo_ref, x_ref, g1_ref, out_ref):
    half = out_ref.shape[0] // 2
    rows = [pl.ds(0, half), pl.ds(half, half)]
    projected = [[_dot(o_ref[r, :], wb_ref[n]) for n, o_ref in enumerate((oa_ref, ob_ref, oc_ref, od_ref))]
                 for r in rows]
    for r, branch in zip(rows, projected):
        acc = None
        for n, bo in enumerate(branch):
            term = gt_ref[r, n * D_MODEL:(n + 1) * D_MODEL] * bo
            acc = term if acc is None else acc + term
        out_ref[r, :] = x_ref[r, :] + g1_ref[...] * _dot(acc.astype(BF16), wo_ref[...])


def _merge(branches, gates, w_branch, w_out, x, mod, latent):
    tm = 512
    row = _mod_row(latent, tm)
    tile = lambda w: pl.BlockSpec((tm, w), lambda i: (i, 0))
    return pl.pallas_call(
        _merge_kernel,
        grid=(M_ROWS // tm,),
        in_specs=[tile(BRANCH_W)] * N_BRANCH + [
            tile(GATE_W),
            pl.BlockSpec((N_BRANCH, BRANCH_W, D_MODEL), lambda i: (0, 0, 0)),
            pl.BlockSpec((D_MODEL, D_MODEL), lambda i: (0, 0)),
            tile(D_MODEL),
            pl.BlockSpec((None, None, 1, D_MODEL), lambda i: (row(i), 2, 0, 0)),
        ],
        out_specs=tile(D_MODEL),
        out_shape=jax.ShapeDtypeStruct((M_ROWS, D_MODEL), F32),
        compiler_params=_params("parallel"),
        name="merge_lat" if latent else "merge_ctx",
    )(*branches, gates, w_branch, w_out, x, mod)


FFN_CHUNK = 256
FFN_STEPS = FFN_HIDDEN // FFN_CHUNK
assert FFN_STEPS * FFN_CHUNK == FFN_HIDDEN


def _ffn_kernel(*refs, final):
    x_ref, sc_ref, sh_ref, g2_ref, wa_ref, wg_ref, wo_ref = refs[:7]
    if final:
        fg_ref, out_ref, h_ref, acc_ref = refs[7:]
    else:
        out_ref, h_ref, acc_ref = refs[7:]
    j = pl.program_id(1)

    @pl.when(j == 0)
    def _():
        h = _rms(x_ref[...]) * (1.0 + sc_ref[...]) + sh_ref[...]
        h_ref[...] = h.astype(BF16)
        acc_ref[...] = jnp.zeros(acc_ref.shape, F32)

    wg, wa, wo = wg_ref[...].astype(BF16), wa_ref[...].astype(BF16), wo_ref[...].astype(BF16)
    half = h_ref.shape[0] // 2
    rows = [pl.ds(0, half), pl.ds(half, half)]
    up = [(_dot(h_ref[r, :], wg), _dot(h_ref[r, :], wa)) for r in rows]
    for r, (gate, lin) in zip(rows, up):
        acc_ref[r, :] += _dot((_silu(gate) * lin).astype(BF16), wo)

    @pl.when(j == FFN_STEPS - 1)
    def _():
        y = x_ref[...] + g2_ref[...] * acc_ref[...]
        out_ref[...] = _rms(y) * fg_ref[...] if final else y


def _ffn(x, mod, w_in, w_out, layer, latent, final_gain):
    tm = 1024
    row = _mod_row(latent, tm)
    modspec = lambda which: pl.BlockSpec((None, None, 1, D_MODEL), lambda i, j: (row(i), which, 0, 0))
    final = final_gain is not None
    in_specs = [
        pl.BlockSpec((tm, D_MODEL), lambda i, j: (i, 0)),
        modspec(4), modspec(3), modspec(5),
        pl.BlockSpec((None, D_MODEL, FFN_CHUNK), lambda i, j: (layer, 0, j)),
        pl.BlockSpec((None, D_MODEL, FFN_CHUNK), lambda i, j: (layer, 0, FFN_STEPS + j)),
        pl.BlockSpec((None, FFN_CHUNK, D_MODEL), lambda i, j: (layer, j, 0)),
    ]
    args = [x, mod, mod, mod, w_in, w_in, w_out]
    if final:
        in_specs.append(pl.BlockSpec((1, D_MODEL), lambda i, j: (0, 0)))
        args.append(final_gain.reshape(1, D_MODEL))
    return pl.pallas_call(
        functools.partial(_ffn_kernel, final=final),
        grid=(M_ROWS // tm, FFN_STEPS),
        in_specs=in_specs,
        out_specs=pl.BlockSpec((tm, D_MODEL), lambda i, j: (i, 0)),
        out_shape=jax.ShapeDtypeStruct((M_ROWS, D_MODEL), F32),
        scratch_shapes=[pltpu.VMEM((tm, D_MODEL), BF16), pltpu.VMEM((tm, D_MODEL), F32)],
        compiler_params=_params("parallel", "arbitrary"),
        name="ffn_lat" if latent else "ffn_ctx",
    )(*args)


def _rope_tables():
    t = np.arange(DEC_SEQ)
    row = (t // GRID_W).astype(np.float64)[:, None]
    col = (t % GRID_W).astype(np.float64)[:, None]

    def angles(rot_dim):
        n_freq = rot_dim // 4
        inv_freq = ROPE_THETA ** (-np.arange(n_freq, dtype=np.float64) / n_freq)
        return np.concatenate([row * inv_freq, col * inv_freq], axis=-1)

    def expand(ang):
        cos = np.repeat(np.cos(ang), 2, axis=-1).astype(np.float32)
        sin = np.repeat(np.sin(ang), 2, axis=-1).astype(np.float32)
        even = (np.arange(cos.shape[-1]) % 2 == 0)[None, :]
        return cos, np.where(even, -sin, 0.0).astype(np.float32), np.where(even, 0.0, sin).astype(np.float32)

    gq = [np.concatenate([a, a], axis=-1) for a in expand(angles(GQ_HD))]
    ml = []
    for idx, a in enumerate(expand(angles(ML_ROPE))):
        fill = 1.0 if idx == 0 else 0.0
        ml.append(np.concatenate([a, np.full((DEC_SEQ, LANE - ML_ROPE), fill, np.float32)], axis=-1))
    return {"gq": jnp.asarray(np.stack(gq)), "ml": jnp.asarray(np.stack(ml))}


def _layer_weights(l, gq_q_gain, gq_k_gain, ml_q_a_gain, ml_kv_a_gain, ml_w_q_b, ml_w_kv_b,
                   w_branch, w_out, hg_gain, avg):
    pad = LANE - ML_NOPE - ML_ROPE
    qb = ml_w_q_b[l].reshape(ML_Q_RANK, ML_HEADS, ML_NOPE + ML_ROPE)
    qb = jnp.concatenate([qb[:, :, ML_NOPE:], qb[:, :, :ML_NOPE], jnp.zeros((ML_Q_RANK, ML_HEADS, pad), F32)],
                         axis=-1).reshape(ML_Q_RANK, ML_HEADS * LANE)
    kvb = ml_w_kv_b[l].reshape(ML_KV_RANK, ML_HEADS, ML_NOPE + ML_V)
    wk = jnp.pad(kvb[:, :, :ML_NOPE], ((0, 0), (0, 0), (ML_ROPE, pad))).reshape(ML_KV_RANK, ML_HEADS * LANE)
    wv = kvb[:, :, ML_NOPE:].reshape(ML_KV_RANK, ML_HEADS * ML_V)
    return {
        "gq_q_gain": jnp.tile(gq_q_gain[l], GQ_HEADS).reshape(1, -1),
        "gq_k_gain": jnp.tile(gq_k_gain[l], GQ_KV_HEADS).reshape(1, -1),
        "ml_q_gain": ml_q_a_gain[l].reshape(1, -1), "ml_kv_gain": ml_kv_a_gain[l].reshape(1, -1),
        "w_qb": qb.astype(BF16), "w_k": wk.astype(BF16), "w_v": wv.astype(BF16),
        "w_branch": w_branch[l].astype(BF16), "w_out": w_out[l].astype(BF16),
        "hg_gain": jnp.tile(hg_gain[l], HG_HEADS).reshape(1, -1), "avg": avg,
    }


def _seg(arr, width, col, rows, stride, off):
    return (arr, width, col, rows, stride, off)


def kernel(x_prompt, x_sample, state_hgrn, cache_gqa_k, cache_gqa_v, cache_na_k, cache_na_v, cache_mla_ckv, cache_mla_krope, c, c_ctx, w_ada, b_ada, w_in, hg_lb_logits, hg_gain, gq_q_gain, gq_k_gain, na_rpb, ml_q_a_gain, ml_kv_a_gain, ml_w_q_b, ml_w_kv_b, w_branch, w_out, w_ffn_in, w_ffn_out, final_gain):
    cond8 = jnp.concatenate([c_ctx[None, :], c, jnp.zeros((8 - 1 - DEC_BATCH, D_MODEL), F32)], axis=0)
    mods = _ada(cond8, w_ada, b_ada)

    lb = jnp.cumsum(jax.nn.softmax(hg_lb_logits.astype(F32), axis=0), axis=0)
    lb = lb - lb[:1]
    avg = jnp.asarray(np.kron(np.eye(512 // GQ_HD), np.full((GQ_HD, GQ_HD), 1.0 / GQ_HD)), BF16)
    tabs = _rope_tables()
    w_in_t = jnp.swapaxes(w_in, 1, 2).reshape(DEPTH * w_in.shape[2], D_MODEL)

    gqk_c = cache_gqa_k.reshape(DEC_BATCH * DEPTH * PAST_LEN, GQ_KV_HEADS * GQ_HD)
    gqv_c = cache_gqa_v.reshape(DEC_BATCH * DEPTH * PAST_LEN, GQ_KV_HEADS * GQ_HD)
    nak_c = cache_na_k.transpose(0, 1, 3, 4, 2).reshape(DEC_BATCH * DEPTH * NA_HEADS * NA_HD, PAST_LEN)
    nav_c = cache_na_v.transpose(0, 1, 3, 4, 2).reshape(DEC_BATCH * DEPTH * NA_HEADS * NA_HD, PAST_LEN)
    mckv_c = cache_mla_ckv.reshape(DEC_BATCH * DEPTH * PAST_LEN, ML_KV_RANK)
    mkr_c = jnp.pad(cache_mla_krope.reshape(DEC_BATCH * DEPTH * PAST_LEN, ML_ROPE),
                    ((0, 0), (0, LANE - ML_ROPE)))

    xp = x_prompt.reshape(M_ROWS, D_MODEL)
    xs = x_sample.reshape(M_ROWS, D_MODEL)
    new = []
    ymix_ctx = []
    for l in range(DEPTH):
        lw = _layer_weights(l, gq_q_gain, gq_k_gain, ml_q_a_gain, ml_kv_a_gain, ml_w_q_b,
                            ml_w_kv_b, w_branch, w_out, hg_gain, avg)
        last = l == DEPTH - 1
        mod = mods[l]

        ymix = _inproj(xp, mod, w_in_t, l, False, gate=False)
        gates = _inproj(xp, mod, w_in_t, l, False, gate=True)
        qb, kb, qd, ckv, kd, vd = _prep(ymix, lw, tabs, False)
        out_a, st = _hgrn(ymix, lb[l], lw["hg_gain"], BATCH, SEQ, None, l)
        out_b = _attention((qb, 512, 0, 0),
                           [(_seg(kb, LANE, 0, SEQ, 1, 0), _seg(ymix, LANE, COL_GV, SEQ, 1, 0))],
                           _plan_gqa(), GQ_SCALE, BATCH, SEQ, "gqa_ctx")
        out_c = _attention((ymix, 512, COL_NQ, 0),
                           [(_seg(ymix, 512, COL_NK, SEQ, 1, 0), _seg(ymix, 512, COL_NV, SEQ, 1, 0))],
                           _plan_na(), NA_SCALE, BATCH, SEQ, "na_ctx")
        out_d = _attention((qd, ML_HEADS * LANE, 0, 0),
                           [(_seg(kd, ML_HEADS * LANE, 0, SEQ, 1, 0), _seg(vd, 512, 0, SEQ, 1, 0))],
                           _plan_mla(), ML_SCALE, BATCH, SEQ, "mla_ctx")
        xp = _merge((out_a, out_b, out_c, out_d), gates, lw["w_branch"], lw["w_out"], xp, mod, False)
        xp = _ffn(xp, mod, w_ffn_in, w_ffn_out, l, False, final_gain if last else None)
        new.append((
            st,
            kb.reshape(BATCH, SEQ, GQ_KV_HEADS, GQ_HD),
            ymix[:, COL_GV * LANE:(COL_GV + 1) * LANE].reshape(BATCH, SEQ, GQ_KV_HEADS, GQ_HD),
            ckv.reshape(BATCH, SEQ, ML_KV_RANK),
            ymix[:, COL_MKR * LANE:COL_MKR * LANE + ML_ROPE].reshape(BATCH, SEQ, ML_ROPE),
        ))
        ymix_ctx.append(ymix)

        ymix = _inproj(xs, mod, w_in_t, l, True, gate=False)
        gates = _inproj(xs, mod, w_in_t, l, True, gate=True)
        qb, kb, qd, ckv, kd, vd = _prep(ymix, lw, tabs, True)
        kd_c, vd_c = _mla_cache(mckv_c, mkr_c, lw, l)
        out_a, = _hgrn(ymix, lb[l], lw["hg_gain"], DEC_BATCH, DEC_SEQ, state_hgrn, l)
        out_b = _attention((qb, 512, 0, 0),
                           [(_seg(gqk_c, LANE, 0, PAST_LEN, DEPTH, l), _seg(gqv_c, LANE, 0, PAST_LEN, DEPTH, l)),
                            (_seg(kb, LANE, 0, DEC_SEQ, 1, 0), _seg(ymix, LANE, COL_GV, DEC_SEQ, 1, 0))],
                           _plan_gqa(), GQ_SCALE, DEC_BATCH, DEC_SEQ, "gqa_lat", tq=512)
        out_c = _na_latent(ymix, nak_c, nav_c, _na_bias_tables(na_rpb[l]), l)
        out_d = _attention((qd, ML_HEADS * LANE, 0, 0),
                           [(_seg(kd_c, ML_HEADS * LANE, 0, PAST_LEN, 1, 0), _seg(vd_c, 512, 0, PAST_LEN, 1, 0)),
                            (_seg(kd, ML_HEADS * LANE, 0, DEC_SEQ, 1, 0), _seg(vd, 512, 0, DEC_SEQ, 1, 0))],
                           _plan_mla(), ML_SCALE, DEC_BATCH, DEC_SEQ, "mla_lat", tq=512)
        xs = _merge((out_a, out_b, out_c, out_d), gates, lw["w_branch"], lw["w_out"], xs, mod, True)
        xs = _ffn(xs, mod, w_ffn_in, w_ffn_out, l, True, final_gain if last else None)

    y_prompt = xp.reshape(BATCH, SEQ, D_MODEL)
    y_sample = xs.reshape(DEC_BATCH, DEC_SEQ, D_MODEL)
    state, gqa_k, gqa_v, ckv_new, krope_new = (jnp.stack([n[i] for n in new], axis=1) for i in range(5))
    na_k, na_v = _na_cache_out(ymix_ctx)
    return (y_prompt, y_sample, state, gqa_k, gqa_v, na_k, na_v, ckv_new, krope_new)
```
